```python
import jax, jax.numpy as jnp
from jax import lax
import numpy as np

D_MODEL = 1024
BATCH = 8
SEQ = 8192
DEPTH = 4

POOL_WINDOWS = (2, 4, 8, 16)
N_POOL_GROUPS = 4
POOL_GROUP_DIM = D_MODEL // 16
POOL_WIDTH = N_POOL_GROUPS * POOL_GROUP_DIM
HEAD_DIM = 64
N_Q_HEADS = D_MODEL // 128
N_KV_HEADS = 2
GROUP = N_Q_HEADS // N_KV_HEADS
Q_WIDTH = N_Q_HEADS * HEAD_DIM
KV_WIDTH = N_KV_HEADS * HEAD_DIM
WINDOW = 128
ROT_DIM = HEAD_DIM // 4
ROPE_THETA = 500000.0
CONV_WIDTH = D_MODEL // 4
CONV_K = 3
N_BRANCHES = 3
IN_WIDTH = POOL_WIDTH + Q_WIDTH + 2 * KV_WIDTH + 3 * CONV_WIDTH + N_BRANCHES * D_MODEL
D_FF = ((8 * D_MODEL // 3 + 127) // 128) * 128
FFN_K = 3
ALPHA = (2 * DEPTH) ** 0.25
BETA = (8 * DEPTH) ** -0.25
LN_EPS = 1e-5
MASK_VALUE = -1e30

kernel_name = "hybrid_pool_swa_shortconv_deepnorm"


def _split_points(sizes):
    pts, acc = [], 0
    for s in sizes[:-1]:
        acc += s
        pts.append(acc)
    return pts


def layer_norm(x, g, b):
    xf = x.astype(jnp.float32)
    mu = jnp.mean(xf, axis=-1, keepdims=True)
    var = jnp.mean(jnp.square(xf - mu), axis=-1, keepdims=True)
    y = (xf - mu) * lax.rsqrt(var + LN_EPS)
    return (y * g.astype(jnp.float32) + b.astype(jnp.float32)).astype(x.dtype)


def causal_dwconv(u, w):
    K = w.shape[0]
    S = u.shape[1]
    up = jnp.pad(u, ((0, 0), (K - 1, 0), (0, 0)))
    y = up[:, 0:S] * w[0]
    for k in range(1, K):
        y = y + up[:, k:k + S] * w[k]
    return y


def rope_tables(positions):
    inv_freq = ROPE_THETA ** (-jnp.arange(0, ROT_DIM, 2, dtype=jnp.float32) / ROT_DIM)
    ang = positions.astype(jnp.float32)[..., None] * inv_freq
    return jnp.cos(ang), jnp.sin(ang)


def apply_partial_rope(t, cos, sin):
    tf = t.astype(jnp.float32)
    half = ROT_DIM // 2
    x1, x2, rest = tf[..., :half], tf[..., half:ROT_DIM], tf[..., ROT_DIM:]
    c, s = cos[:, :, None, :], sin[:, :, None, :]
    out = jnp.concatenate([x1 * c - x2 * s, x2 * c + x1 * s, rest], axis=-1)
    return out.astype(t.dtype)


def multiscale_pool_mixer(u, w_pool, pool_scale):
    B, S, _ = u.shape
    ug = u.reshape(B, S, N_POOL_GROUPS, POOL_GROUP_DIM).astype(jnp.float32)
    csum = jnp.cumsum(ug, axis=1)
    t = jnp.arange(S)
    outs = []
    for g, w in enumerate(POOL_WINDOWS):
        c = csum[:, :, g]
        c_lag = jnp.pad(c[:, :S - w], ((0, 0), (w, 0), (0, 0)))
        count = jnp.minimum(t + 1, w).astype(jnp.float32)[None, :, None]
        outs.append((c - c_lag) / count - ug[:, :, g])
    pooled = jnp.stack(outs, axis=2).astype(u.dtype)
    mixed = jnp.einsum('bsgc,gcd->bsgd', pooled, w_pool)
    return mixed.reshape(B, S, POOL_WIDTH) * pool_scale


def sliding_window_gqa_sinks(q, k, v, sinks):
    B, S, _, D = q.shape
    nb = S // WINDOW
    qb = q.reshape(B, nb, WINDOW, N_KV_HEADS, GROUP, D)
    kb = k.reshape(B, nb, WINDOW, N_KV_HEADS, D)
    vb = v.reshape(B, nb, WINDOW, N_KV_HEADS, D)
    pad = ((0, 0), (1, 0), (0, 0), (0, 0), (0, 0))
    kcat = jnp.concatenate([jnp.pad(kb[:, :-1], pad), kb], axis=2)
    vcat = jnp.concatenate([jnp.pad(vb[:, :-1], pad), vb], axis=2)
    scores = jnp.einsum('bnqhgd,bnkhd->bnhgqk', qb, kcat).astype(jnp.float32)
    scores = scores * (HEAD_DIM ** -0.5)
    i = jnp.arange(WINDOW)[:, None]
    j = jnp.arange(2 * WINDOW)[None, :]
    band = (j > i) & (j <= i + WINDOW)
    blk = jnp.arange(nb)[:, None, None]
    valid = band[None] & ((blk > 0) | (j[None] >= WINDOW))
    scores = jnp.where(valid[None, :, None, None], scores, MASK_VALUE)
    sink = jnp.broadcast_to(
        sinks.astype(jnp.float32).reshape(1, 1, N_KV_HEADS, GROUP, 1, 1),
        scores.shape[:-1] + (1,))
    probs = jax.nn.softmax(jnp.concatenate([scores, sink], axis=-1), axis=-1)[..., :-1]
    out = jnp.einsum('bnhgqk,bnkhd->bnqhgd', probs.astype(v.dtype), vcat)
    return out.reshape(B, S, N_Q_HEADS * D)


def short_gated_conv(xc, gate_b, gate_c, conv_w):
    return gate_b * causal_dwconv(gate_c * xc, conv_w)


def hybrid_layer(x, cos, sin, w_in, w_pool, pool_scale, attn_sinks, conv_w,
                 w_branch_a, w_branch_b, w_branch_c, w_o, ln1_g, ln1_b,
                 w_up, ffn_conv_w, w_down, ln2_g, ln2_b):
    B, S, _ = x.shape
    proj = jnp.einsum('bsd,de->bse', x, w_in)
    sizes = [POOL_WIDTH, Q_WIDTH, KV_WIDTH, KV_WIDTH,
             CONV_WIDTH, CONV_WIDTH, CONV_WIDTH, N_BRANCHES * D_MODEL]
    u_pool, q, k, v, xc, gate_b, gate_c, gate_logits = jnp.split(
        proj, _split_points(sizes), axis=-1)
    o_a = multiscale_pool_mixer(u_pool, w_pool, pool_scale)
    q = apply_partial_rope(q.reshape(B, S, N_Q_HEADS, HEAD_DIM), cos, sin)
    k = apply_partial_rope(k.reshape(B, S, N_KV_HEADS, HEAD_DIM), cos, sin)
    v = v.reshape(B, S, N_KV_HEADS, HEAD_DIM)
    o_b = sliding_window_gqa_sinks(q, k, v, attn_sinks)
    o_c = short_gated_conv(xc, gate_b, gate_c, conv_w)
    gates = jax.nn.sigmoid(gate_logits).reshape(B, S, N_BRANCHES, D_MODEL)
    merged = (gates[:, :, 0] * jnp.einsum('bsc,cd->bsd', o_a, w_branch_a)
              + gates[:, :, 1] * jnp.einsum('bsc,cd->bsd', o_b, w_branch_b)
              + gates[:, :, 2] * jnp.einsum('bsc,cd->bsd', o_c, w_branch_c))
    mix = jnp.einsum('bsd,de->bse', merged, w_o)
    x = layer_norm(ALPHA * x + mix, ln1_g, ln1_b)
    up = causal_dwconv(jnp.einsum('bsd,df->bsf', x, w_up), ffn_conv_w)
    a, b = jnp.split(up, 2, axis=-1)
    ffn = jnp.einsum('bsf,fd->bsd', jax.nn.silu(a) * b, w_down)
    return layer_norm(ALPHA * x + ffn, ln2_g, ln2_b)


def _fwd_setup_inputs(seed: int = 0) -> dict:
    key = jax.random.key(seed)
    ks = jax.random.split(key, 20)
    nrm = lambda k, shape, scale: jax.random.normal(k, shape, jnp.float32) * scale
    x = nrm(ks[0], (BATCH, SEQ, D_MODEL), 1.0)
    offset = jax.random.randint(ks[1], (BATCH, 1), 0, 1024, dtype=jnp.int32)
    positions = offset + jnp.arange(SEQ, dtype=jnp.int32)[None, :]
    return {
        "x": x,
        "positions": positions,
        "w_in": nrm(ks[2], (DEPTH, D_MODEL, IN_WIDTH), D_MODEL ** -0.5),
        "w_pool": nrm(ks[3], (DEPTH, N_POOL_GROUPS, POOL_GROUP_DIM, POOL_GROUP_DIM), POOL_GROUP_DIM ** -0.5),
        "pool_scale": 1.0 + nrm(ks[4], (DEPTH, POOL_WIDTH), 0.1),
        "attn_sinks": nrm(ks[5], (DEPTH, N_Q_HEADS), 0.5),
        "conv_w": nrm(ks[6], (DEPTH, CONV_K, CONV_WIDTH), CONV_K ** -0.5),
        "w_branch_a": nrm(ks[7], (DEPTH, POOL_WIDTH, D_MODEL), POOL_WIDTH ** -0.5),
        "w_branch_b": nrm(ks[8], (DEPTH, Q_WIDTH, D_MODEL), Q_WIDTH ** -0.5),
        "w_branch_c": nrm(ks[9], (DEPTH, CONV_WIDTH, D_MODEL), CONV_WIDTH ** -0.5),
        "w_o": nrm(ks[10], (DEPTH, D_MODEL, D_MODEL), BETA * D_MODEL ** -0.5),
        "ln1_g": 1.0 + nrm(ks[11], (DEPTH, D_MODEL), 0.02),
        "ln1_b": nrm(ks[12], (DEPTH, D_MODEL), 0.02),
        "w_up": nrm(ks[13], (DEPTH, D_MODEL, 2 * D_FF), D_MODEL ** -0.5),
        "ffn_conv_w": nrm(ks[14], (DEPTH, FFN_K, 2 * D_FF), FFN_K ** -0.5),
        "w_down": nrm(ks[15], (DEPTH, D_FF, D_MODEL), BETA * D_FF ** -0.5),
        "ln2_g": 1.0 + nrm(ks[16], (DEPTH, D_MODEL), 0.02),
        "ln2_b": nrm(ks[17], (DEPTH, D_MODEL), 0.02),
    }


def _fwd_reference(x, positions, w_in, w_pool, pool_scale, attn_sinks, conv_w,
              w_branch_a, w_branch_b, w_branch_c, w_o, ln1_g, ln1_b,
              w_up, ffn_conv_w, w_down, ln2_g, ln2_b):
    cos, sin = rope_tables(positions)
    for l in range(DEPTH):
        x = hybrid_layer(x, cos, sin, w_in[l], w_pool[l], pool_scale[l], attn_sinks[l],
                         conv_w[l], w_branch_a[l], w_branch_b[l], w_branch_c[l], w_o[l],
                         ln1_g[l], ln1_b[l], w_up[l], ffn_conv_w[l], w_down[l],
                         ln2_g[l], ln2_b[l])
    return x


import jax as _jax
import jax.numpy as _jnp

TWIN_FORMAT = 'train_step'
FWD_PARAMS = ['x', 'positions', 'w_in', 'w_pool', 'pool_scale', 'attn_sinks', 'conv_w', 'w_branch_a', 'w_branch_b', 'w_branch_c', 'w_o', 'ln1_g', 'ln1_b', 'w_up', 'ffn_conv_w', 'w_down', 'ln2_g', 'ln2_b']
TWIN_WEIGHTS = ['w_in', 'w_pool', 'pool_scale', 'attn_sinks', 'conv_w', 'w_branch_a', 'w_branch_b', 'w_branch_c', 'w_o', 'ln1_g', 'ln1_b', 'w_up', 'ffn_conv_w', 'w_down', 'ln2_g', 'ln2_b']
TWIN_DIFF_INPUT = 'x'
TWIN_INPUTS = ['x', 'positions', 'w_in', 'w_pool', 'pool_scale', 'attn_sinks', 'conv_w', 'w_branch_a', 'w_branch_b', 'w_branch_c', 'w_o', 'ln1_g', 'ln1_b', 'w_up', 'ffn_conv_w', 'w_down', 'ln2_g', 'ln2_b', 'loss_target', 'm_w_in', 'm_w_pool', 'm_pool_scale', 'm_attn_sinks', 'm_conv_w', 'm_w_branch_a', 'm_w_branch_b', 'm_w_branch_c', 'm_w_o', 'm_ln1_g', 'm_ln1_b', 'm_w_up', 'm_ffn_conv_w', 'm_w_down', 'm_ln2_g', 'm_ln2_b', 'v_w_in', 'v_w_pool', 'v_pool_scale', 'v_attn_sinks', 'v_conv_w', 'v_w_branch_a', 'v_w_branch_b', 'v_w_branch_c', 'v_w_o', 'v_ln1_g', 'v_ln1_b', 'v_w_up', 'v_ffn_conv_w', 'v_w_down', 'v_ln2_g', 'v_ln2_b']
TWIN_OUTPUTS = ['loss', 'grad_x', 'grad_w_in', 'grad_w_pool', 'grad_pool_scale', 'grad_attn_sinks', 'grad_conv_w', 'grad_w_branch_a', 'grad_w_branch_b', 'grad_w_branch_c', 'grad_w_o', 'grad_ln1_g', 'grad_ln1_b', 'grad_w_up', 'grad_ffn_conv_w', 'grad_w_down', 'grad_ln2_g', 'grad_ln2_b', 'delta_w_in', 'delta_w_pool', 'delta_pool_scale', 'delta_attn_sinks', 'delta_conv_w', 'delta_w_branch_a', 'delta_w_branch_b', 'delta_w_branch_c', 'delta_w_o', 'delta_ln1_g', 'delta_ln1_b', 'delta_w_up', 'delta_ffn_conv_w', 'delta_w_down', 'delta_ln2_g', 'delta_ln2_b', 'new_m_w_in', 'new_m_w_pool', 'new_m_pool_scale', 'new_m_attn_sinks', 'new_m_conv_w', 'new_m_w_branch_a', 'new_m_w_branch_b', 'new_m_w_branch_c', 'new_m_w_o', 'new_m_ln1_g', 'new_m_ln1_b', 'new_m_w_up', 'new_m_ffn_conv_w', 'new_m_w_down', 'new_m_ln2_g', 'new_m_ln2_b', 'new_v_w_in', 'new_v_w_pool', 'new_v_pool_scale', 'new_v_attn_sinks', 'new_v_conv_w', 'new_v_w_branch_a', 'new_v_w_branch_b', 'new_v_w_branch_c', 'new_v_w_o', 'new_v_ln1_g', 'new_v_ln1_b', 'new_v_w_up', 'new_v_ffn_conv_w', 'new_v_w_down', 'new_v_ln2_g', 'new_v_ln2_b']
TWIN_LEAF_KINDS = {'loss': 'loss', 'grad_x': 'grad_x', 'grad_w_in': 'grad_w', 'grad_w_pool': 'grad_w', 'grad_pool_scale': 'grad_w', 'grad_attn_sinks': 'grad_w', 'grad_conv_w': 'grad_w', 'grad_w_branch_a': 'grad_w', 'grad_w_branch_b': 'grad_w', 'grad_w_branch_c': 'grad_w', 'grad_w_o': 'grad_w', 'grad_ln1_g': 'grad_w', 'grad_ln1_b': 'grad_w', 'grad_w_up': 'grad_w', 'grad_ffn_conv_w': 'grad_w', 'grad_w_down': 'grad_w', 'grad_ln2_g': 'grad_w', 'grad_ln2_b': 'grad_w', 'delta_w_in': 'delta_w', 'delta_w_pool': 'delta_w', 'delta_pool_scale': 'delta_w', 'delta_attn_sinks': 'delta_w', 'delta_conv_w': 'delta_w', 'delta_w_branch_a': 'delta_w', 'delta_w_branch_b': 'delta_w', 'delta_w_branch_c': 'delta_w', 'delta_w_o': 'delta_w', 'delta_ln1_g': 'delta_w', 'delta_ln1_b': 'delta_w', 'delta_w_up': 'delta_w', 'delta_ffn_conv_w': 'delta_w', 'delta_w_down': 'delta_w', 'delta_ln2_g': 'delta_w', 'delta_ln2_b': 'delta_w', 'new_m_w_in': 'new_m', 'new_m_w_pool': 'new_m', 'new_m_pool_scale': 'new_m', 'new_m_attn_sinks': 'new_m', 'new_m_conv_w': 'new_m', 'new_m_w_branch_a': 'new_m', 'new_m_w_branch_b': 'new_m', 'new_m_w_branch_c': 'new_m', 'new_m_w_o': 'new_m', 'new_m_ln1_g': 'new_m', 'new_m_ln1_b': 'new_m', 'new_m_w_up': 'new_m', 'new_m_ffn_conv_w': 'new_m', 'new_m_w_down': 'new_m', 'new_m_ln2_g': 'new_m', 'new_m_ln2_b': 'new_m', 'new_v_w_in': 'new_v', 'new_v_w_pool': 'new_v', 'new_v_pool_scale': 'new_v', 'new_v_attn_sinks': 'new_v', 'new_v_conv_w': 'new_v', 'new_v_w_branch_a': 'new_v', 'new_v_w_branch_b': 'new_v', 'new_v_w_branch_c': 'new_v', 'new_v_w_o': 'new_v', 'new_v_ln1_g': 'new_v', 'new_v_ln1_b': 'new_v', 'new_v_w_up': 'new_v', 'new_v_ffn_conv_w': 'new_v', 'new_v_w_down': 'new_v', 'new_v_ln2_g': 'new_v', 'new_v_ln2_b': 'new_v'}


def _forward(args):
    return _fwd_reference(*[args[k] for k in FWD_PARAMS])


def _output_shape():
    out = _jax.eval_shape(lambda: _forward(_fwd_setup_inputs(0)))
    return out.shape, out.dtype

N_MICROBATCH = 1
ADAM_LR = 0.001
ADAM_B1 = 0.9
ADAM_B2 = 0.999
ADAM_EPS = 1e-08
ADAM_WD = 0.01
ADAM_STEP = 10
PER_EXAMPLE_BATCH_AXIS = {'x': 0, 'positions': 0, 'loss_target': 0}
SHARED_INPUTS = []
_WEIGHT_DTYPES = {'w_in': _jnp.float32, 'w_pool': _jnp.float32, 'pool_scale': _jnp.float32, 'attn_sinks': _jnp.float32, 'conv_w': _jnp.float32, 'w_branch_a': _jnp.float32, 'w_branch_b': _jnp.float32, 'w_branch_c': _jnp.float32, 'w_o': _jnp.float32, 'ln1_g': _jnp.float32, 'ln1_b': _jnp.float32, 'w_up': _jnp.float32, 'ffn_conv_w': _jnp.float32, 'w_down': _jnp.float32, 'ln2_g': _jnp.float32, 'ln2_b': _jnp.float32}
MOMENT_SCALE = {'w_in': 3.288964e-02, 'w_pool': 6.237396e-02, 'pool_scale': 6.567739e-02, 'attn_sinks': 7.546227e-03, 'conv_w': 7.236497e-02, 'w_branch_a': 3.117427e-02, 'w_branch_b': 7.425540e-03, 'w_branch_c': 3.546102e-02, 'w_o': 1.125029e-01, 'ln1_g': 1.785878e+00, 'ln1_b': 7.128670e-01, 'w_up': 2.409911e-02, 'ffn_conv_w': 2.451099e-02, 'w_down': 9.362301e-02, 'ln2_g': 3.207425e+01, 'ln2_b': 1.345594e+00}


def _to_microbatches(a, axis):
    t = _jnp.moveaxis(a, axis, 0)
    t = t.reshape((N_MICROBATCH, t.shape[0] // N_MICROBATCH) + t.shape[1:])
    return _jnp.moveaxis(t, 1, axis + 1)


def setup_inputs(seed: int = 0) -> dict:
    inp = _fwd_setup_inputs(seed)
    key = _jax.random.fold_in(_jax.random.key(seed), 7919)
    shape, _ = _output_shape()
    out = dict(inp)
    out["loss_target"] = _jax.random.normal(_jax.random.fold_in(key, 0), shape, _jnp.float32)
    for i, name in enumerate(TWIN_WEIGHTS):
        w = inp[name].astype(_jnp.float32)
        if MOMENT_SCALE is None:
            s = _jnp.sqrt(_jnp.mean(_jnp.square(w)) + 1e-30)
        else:
            s = MOMENT_SCALE[name]
        km, kv = _jax.random.split(_jax.random.fold_in(key, i + 1))
        out[name] = w
        out["m_" + name] = s * _jax.random.normal(km, w.shape, _jnp.float32)
        out["v_" + name] = (s * s) * _jax.random.uniform(kv, w.shape, _jnp.float32, 0.5, 1.5)
    if N_MICROBATCH > 1:
        for name, axis in PER_EXAMPLE_BATCH_AXIS.items():
            out[name] = _to_microbatches(out[name], axis)
    return {'x': out['x'], 'positions': out['positions'], 'w_in': out['w_in'], 'w_pool': out['w_pool'], 'pool_scale': out['pool_scale'], 'attn_sinks': out['attn_sinks'], 'conv_w': out['conv_w'], 'w_branch_a': out['w_branch_a'], 'w_branch_b': out['w_branch_b'], 'w_branch_c': out['w_branch_c'], 'w_o': out['w_o'], 'ln1_g': out['ln1_g'], 'ln1_b': out['ln1_b'], 'w_up': out['w_up'], 'ffn_conv_w': out['ffn_conv_w'], 'w_down': out['w_down'], 'ln2_g': out['ln2_g'], 'ln2_b': out['ln2_b'], 'loss_target': out['loss_target'], 'm_w_in': out['m_w_in'], 'm_w_pool': out['m_w_pool'], 'm_pool_scale': out['m_pool_scale'], 'm_attn_sinks': out['m_attn_sinks'], 'm_conv_w': out['m_conv_w'], 'm_w_branch_a': out['m_w_branch_a'], 'm_w_branch_b': out['m_w_branch_b'], 'm_w_branch_c': out['m_w_branch_c'], 'm_w_o': out['m_w_o'], 'm_ln1_g': out['m_ln1_g'], 'm_ln1_b': out['m_ln1_b'], 'm_w_up': out['m_w_up'], 'm_ffn_conv_w': out['m_ffn_conv_w'], 'm_w_down': out['m_w_down'], 'm_ln2_g': out['m_ln2_g'], 'm_ln2_b': out['m_ln2_b'], 'v_w_in': out['v_w_in'], 'v_w_pool': out['v_w_pool'], 'v_pool_scale': out['v_pool_scale'], 'v_attn_sinks': out['v_attn_sinks'], 'v_conv_w': out['v_conv_w'], 'v_w_branch_a': out['v_w_branch_a'], 'v_w_branch_b': out['v_w_branch_b'], 'v_w_branch_c': out['v_w_branch_c'], 'v_w_o': out['v_w_o'], 'v_ln1_g': out['v_ln1_g'], 'v_ln1_b': out['v_ln1_b'], 'v_w_up': out['v_w_up'], 'v_ffn_conv_w': out['v_ffn_conv_w'], 'v_w_down': out['v_w_down'], 'v_ln2_g': out['v_ln2_g'], 'v_ln2_b': out['v_ln2_b']}


def _loss(weights, diff, rest, loss_target):
    with _jax.named_scope("forward"):
        args = {**rest, TWIN_DIFF_INPUT: diff, **{k: w.astype(_WEIGHT_DTYPES[k]) for k, w in weights.items()}}
        y = _forward(args)
    with _jax.named_scope("loss_head"):
        err = _jnp.square(y.astype(_jnp.float32) - loss_target)
        return 0.5 * _jnp.sum(_jnp.mean(err, axis=-1)) if err.ndim else 0.5 * err


def _adamw(w, g, m, v):
    m = ADAM_B1 * m + (1.0 - ADAM_B1) * g
    v = ADAM_B2 * v + (1.0 - ADAM_B2) * _jnp.square(g)
    m_hat = m / (1.0 - ADAM_B1 ** ADAM_STEP)
    v_hat = v / (1.0 - ADAM_B2 ** ADAM_STEP)
    delta = -ADAM_LR * (m_hat / (_jnp.sqrt(v_hat) + ADAM_EPS) + ADAM_WD * w)
    return delta, m, v


def reference(x, positions, w_in, w_pool, pool_scale, attn_sinks, conv_w, w_branch_a, w_branch_b, w_branch_c, w_o, ln1_g, ln1_b, w_up, ffn_conv_w, w_down, ln2_g, ln2_b, loss_target, m_w_in, m_w_pool, m_pool_scale, m_attn_sinks, m_conv_w, m_w_branch_a, m_w_branch_b, m_w_branch_c, m_w_o, m_ln1_g, m_ln1_b, m_w_up, m_ffn_conv_w, m_w_down, m_ln2_g, m_ln2_b, v_w_in, v_w_pool, v_pool_scale, v_attn_sinks, v_conv_w, v_w_branch_a, v_w_branch_b, v_w_branch_c, v_w_o, v_ln1_g, v_ln1_b, v_w_up, v_ffn_conv_w, v_w_down, v_ln2_g, v_ln2_b):
    given = dict(x=x, positions=positions, w_in=w_in, w_pool=w_pool, pool_scale=pool_scale, attn_sinks=attn_sinks, conv_w=conv_w, w_branch_a=w_branch_a, w_branch_b=w_branch_b, w_branch_c=w_branch_c, w_o=w_o, ln1_g=ln1_g, ln1_b=ln1_b, w_up=w_up, ffn_conv_w=ffn_conv_w, w_down=w_down, ln2_g=ln2_g, ln2_b=ln2_b, loss_target=loss_target, m_w_in=m_w_in, m_w_pool=m_w_pool, m_pool_scale=m_pool_scale, m_attn_sinks=m_attn_sinks, m_conv_w=m_conv_w, m_w_branch_a=m_w_branch_a, m_w_branch_b=m_w_branch_b, m_w_branch_c=m_w_branch_c, m_w_o=m_w_o, m_ln1_g=m_ln1_g, m_ln1_b=m_ln1_b, m_w_up=m_w_up, m_ffn_conv_w=m_ffn_conv_w, m_w_down=m_w_down, m_ln2_g=m_ln2_g, m_ln2_b=m_ln2_b, v_w_in=v_w_in, v_w_pool=v_w_pool, v_pool_scale=v_pool_scale, v_attn_sinks=v_attn_sinks, v_conv_w=v_conv_w, v_w_branch_a=v_w_branch_a, v_w_branch_b=v_w_branch_b, v_w_branch_c=v_w_branch_c, v_w_o=v_w_o, v_ln1_g=v_ln1_g, v_ln1_b=v_ln1_b, v_w_up=v_w_up, v_ffn_conv_w=v_ffn_conv_w, v_w_down=v_w_down, v_ln2_g=v_ln2_g, v_ln2_b=v_ln2_b)
    weights = {n: given[n] for n in TWIN_WEIGHTS}
    shared = {n: given[n] for n in SHARED_INPUTS}
    per_example = {n: given[n] for n in ['x', 'positions']}
    grad_fn = _jax.value_and_grad(_loss, argnums=(0, 1))

    def one_microbatch(ex, loss_target):
        ex = dict(ex)
        diff = ex.pop(TWIN_DIFF_INPUT)
        return grad_fn(weights, diff, {**shared, **ex}, loss_target)

    if N_MICROBATCH == 1:
        loss, (grad_w, grad_x) = one_microbatch(per_example, given["loss_target"])
    else:
        def body(carry, xs):
            loss_sum, grad_sum = carry
            l_k, (gw_k, gx_k) = one_microbatch(xs[0], xs[1])
            with _jax.named_scope("update"):
                return (loss_sum + l_k, _jax.tree.map(_jnp.add, grad_sum, gw_k)), gx_k

        init = (_jnp.zeros((), _jnp.float32), _jax.tree.map(_jnp.zeros_like, weights))
        (loss, grad_w), grad_x = _jax.lax.scan(body, init, (per_example, given["loss_target"]))
    with _jax.named_scope("update"):
        delta_w, new_m, new_v = {}, {}, {}
        for n in TWIN_WEIGHTS:
            delta_w[n], new_m[n], new_v[n] = _adamw(weights[n], grad_w[n], given["m_" + n], given["v_" + n])
    return (loss, grad_x, *[grad_w[n] for n in TWIN_WEIGHTS], *[delta_w[n] for n in TWIN_WEIGHTS],
            *[new_m[n] for n in TWIN_WEIGHTS], *[new_v[n] for n in TWIN_WEIGHTS])
```

```python
import functools

import jax
import jax.numpy as jnp
from jax import lax
from jax.experimental import pallas as pl
from jax.experimental.pallas import tpu as pltpu

F32 = jnp.float32
BF16 = jnp.bfloat16

HEAD_DIM = 64
N_Q_HEADS = 8
GROUP = 4
WINDOW = 128
ROT_DIM = 16
ROPE_THETA = 500000.0
POOL_WINDOWS = (2, 4, 8, 16)
LN_EPS = 1e-5
MASK_VALUE = -1e30
ADAM_LR, ADAM_B1, ADAM_B2, ADAM_EPS, ADAM_WD, ADAM_STEP = 0.001, 0.9, 0.999, 1e-08, 0.01, 10

N_DEV = 8
LANES = 1024
HALO = 16
MESH = pl.DeviceIdType.MESH
VMEM_LIMIT = 56 * 1024 * 1024


def _div_tile(n, want, mult=8):
    for t in range(min(want, n) // mult * mult, 0, -mult):
        if n % t == 0:
            return t
    return n


def _cp(*sem):
    return pltpu.CompilerParams(dimension_semantics=sem, vmem_limit_bytes=VMEM_LIMIT)


def _coords():
    return lax.axis_index("x"), lax.axis_index("y"), lax.axis_index("c")


def _all_gather(xs, name):
    R, C = xs.shape

    def body(x_ref, out_ref, send_sems, recv_sems, local_sem):
        x, y, c = _coords()
        me, sibling = (x, y, c), (x, y, 1 - c)
        chips = [(1 - x, y), (x, 1 - y), (1 - x, 1 - y)]

        def slot(px, py, pc):
            return out_ref.at[4 * px + 2 * py + pc]

        def copy(k, block, to, src=None):
            return pltpu.make_async_remote_copy(
                src_ref=slot(*block) if src is None else src, dst_ref=slot(*block),
                send_sem=send_sems.at[k], recv_sem=recv_sems.at[k], device_id=to, device_id_type=MESH)

        mine = pltpu.make_async_copy(x_ref, slot(*me), local_sem)
        mine.start()
        first = [copy(0, me, sibling, src=x_ref)]
        first += [copy(1 + j, me, (*chip, c), src=x_ref) for j, chip in enumerate(chips)]
        for cp in first:
            cp.start()
        passed = [copy(4 + j, (*chip, c), sibling) for j, chip in enumerate(chips)]
        for j, chip in enumerate(chips):
            copy(1 + j, (*chip, c), me).wait_recv()
            passed[j].start()
        copy(0, sibling, me).wait_recv()
        for j, chip in enumerate(chips):
            copy(4 + j, (*chip, 1 - c), me).wait_recv()
        for cp in first + passed:
            cp.wait_send()
        mine.wait()

    return pl.pallas_call(
        body, name=name,
        out_shape=jax.ShapeDtypeStruct((N_DEV, R, C), xs.dtype),
        in_specs=[pl.BlockSpec(memory_space=pl.ANY)],
        out_specs=pl.BlockSpec(memory_space=pl.ANY),
        scratch_shapes=[pltpu.SemaphoreType.DMA((7,)), pltpu.SemaphoreType.DMA((7,)), pltpu.SemaphoreType.DMA(())],
    )(xs)


def _rs_sibling(p, name):
    _, R, C = p.shape

    def body(p_ref, out_ref, send_sems, recv_sems):
        x, y, c = _coords()
        copies = []
        for j in range(4):
            cx, cy = j // 2, j % 2
            copies.append(pltpu.make_async_remote_copy(
                src_ref=p_ref.at[4 * cx + 2 * cy + (1 - c)], dst_ref=out_ref.at[j],
                send_sem=send_sems.at[j], recv_sem=recv_sems.at[j], device_id=(x, y, 1 - c), device_id_type=MESH))
        for cp in copies:
            cp.start()
        for cp in copies:
            cp.wait_recv()
        for cp in copies:
            cp.wait_send()

    return pl.pallas_call(
        body, name=name,
        out_shape=jax.ShapeDtypeStruct((4, R, C), p.dtype),
        in_specs=[pl.BlockSpec(memory_space=pl.ANY)],
        out_specs=pl.BlockSpec(memory_space=pl.ANY),
        scratch_shapes=[pltpu.SemaphoreType.DMA((4,)), pltpu.SemaphoreType.DMA((4,))],
    )(p)


def _rs_chips(q, name):
    _, R, C = q.shape

    def body(q_ref, out_ref, send_sems, recv_sems):
        x, y, c = _coords()
        chips = [(1 - x, y), (x, 1 - y), (1 - x, 1 - y)]
        copies = []
        for k, (cx, cy) in enumerate(chips):
            copies.append(pltpu.make_async_remote_copy(
                src_ref=q_ref.at[2 * cx + cy], dst_ref=out_ref.at[k],
                send_sem=send_sems.at[k], recv_sem=recv_sems.at[k], device_id=(cx, cy, c), device_id_type=MESH))
        for cp in copies:
            cp.start()
        for cp in copies:
            cp.wait_recv()
        for cp in copies:
            cp.wait_send()

    return pl.pallas_call(
        body, name=name,
        out_shape=jax.ShapeDtypeStruct((3, R, C), q.dtype),
        in_specs=[pl.BlockSpec(memory_space=pl.ANY)],
        out_specs=pl.BlockSpec(memory_space=pl.ANY),
        scratch_shapes=[pltpu.SemaphoreType.DMA((3,)), pltpu.SemaphoreType.DMA((3,))],
    )(q)


def _sum_sibling(p, recv, my_c, name, tr=512):
    _, R, C = p.shape
    tr = _div_tile(R, tr, 16)

    def body(c_ref, p_ref, r_ref, o_ref):
        o_ref[...] = (p_ref[...].astype(F32) + r_ref[...].astype(F32)).astype(o_ref.dtype)

    grid_spec = pltpu.PrefetchScalarGridSpec(
        num_scalar_prefetch=1, grid=(4, R // tr),
        in_specs=[pl.BlockSpec((1, tr, C), lambda j, r, c_ref: (4 * (j // 2) + 2 * (j % 2) + c_ref[0], r, 0)),
                  pl.BlockSpec((1, tr, C), lambda j, r, c_ref: (j, r, 0))],
        out_specs=pl.BlockSpec((1, tr, C), lambda j, r, c_ref: (j, r, 0)))
    return pl.pallas_call(body, name=name, grid_spec=grid_spec,
                          out_shape=jax.ShapeDtypeStruct((4, R, C), p.dtype),
                          compiler_params=_cp("parallel", "parallel"))(my_c, p, recv)


def _sum_chips(q, recv, my_chip, name, tr=512):
    _, R, C = q.shape
    tr = _div_tile(R, tr, 16)

    def body(i_ref, q_ref, r_ref, o_ref):
        acc = q_ref[0].astype(F32)
        for k in range(3):
            acc = acc + r_ref[k].astype(F32)
        o_ref[...] = acc

    grid_spec = pltpu.PrefetchScalarGridSpec(
        num_scalar_prefetch=1, grid=(R // tr,),
        in_specs=[pl.BlockSpec((1, tr, C), lambda r, i_ref: (i_ref[0], r, 0)),
                  pl.BlockSpec((3, tr, C), lambda r, i_ref: (0, r, 0))],
        out_specs=pl.BlockSpec((tr, C), lambda r, i_ref: (r, 0)))
    return pl.pallas_call(body, name=name, grid_spec=grid_spec,
                          out_shape=jax.ShapeDtypeStruct((R, C), F32),
                          compiler_params=_cp("parallel"))(my_chip, q, recv)


def _small_reduce(g, n_rep, n_mine, inv_d, loss_row, name):
    _, R, C = g.shape

    def body(g_ref, rep_ref, mine_ref, loss_ref):
        x, y, c = _coords()
        start = pl.multiple_of(n_rep + (4 * x + 2 * y + c) * n_mine, 8)
        rep = g_ref[0, 0:n_rep, :]
        mine = g_ref[0, pl.ds(start, n_mine), :]
        sq = g_ref[0, loss_row:loss_row + 1, :]
        for d in range(1, N_DEV):
            rep = rep + g_ref[d, 0:n_rep, :]
            mine = mine + g_ref[d, pl.ds(start, n_mine), :]
            sq = sq + g_ref[d, loss_row:loss_row + 1, :]
        rep_ref[...] = rep
        mine_ref[...] = mine
        loss_ref[...] = (0.5 * inv_d) * jnp.sum(sq, axis=1, keepdims=True)

    return pl.pallas_call(
        body, name=name,
        out_shape=(jax.ShapeDtypeStruct((n_rep, C), F32), jax.ShapeDtypeStruct((n_mine, C), F32),
                   jax.ShapeDtypeStruct((1, 1), F32)),
        compiler_params=pltpu.CompilerParams(vmem_limit_bytes=VMEM_LIMIT),
    )(g)


def _mm(a, b, *, out_dtype, name, tm=512, tn=None, tk=None, add=None, add_scale=1.0):
    M, K = a.shape
    N = b.shape[1]
    tm = min(tm, M)
    tn = N if tn is None else tn
    tk = K if tk is None else tk
    nk = K // tk
    has_add = add is not None

    def body(*refs):
        a_ref, b_ref = refs[0], refs[1]
        add_ref = refs[2] if has_add else None
        o_ref = refs[3] if has_add else refs[2]
        part = jnp.dot(a_ref[...].astype(BF16), b_ref[...].astype(BF16), preferred_element_type=F32)

        def finish(r):
            if has_add:
                r = r + add_scale * add_ref[...].astype(F32)
            o_ref[...] = r.astype(out_dtype)

        if nk == 1:
            finish(part)
        else:
            acc_ref = refs[-1]
            k = pl.program_id(2)

            @pl.when(k == 0)
            def _():
                acc_ref[...] = part

            @pl.when(k > 0)
            def _():
                acc_ref[...] += part

            @pl.when(k == nk - 1)
            def _():
                finish(acc_ref[...])

    in_specs = [pl.BlockSpec((tm, tk), lambda i, j, k: (i, k)), pl.BlockSpec((tk, tn), lambda i, j, k: (k, j))]
    args = [a, b]
    if has_add:
        in_specs.append(pl.BlockSpec((tm, tn), lambda i, j, k: (i, j)))
        args.append(add)
    return pl.pallas_call(
        body, name=name, grid=(M // tm, N // tn, nk),
        in_specs=in_specs, out_specs=pl.BlockSpec((tm, tn), lambda i, j, k: (i, j)),
        out_shape=jax.ShapeDtypeStruct((M, N), out_dtype),
        scratch_shapes=[pltpu.VMEM((tm, tn), F32)] if nk > 1 else [],
        compiler_params=_cp("parallel", "parallel", "arbitrary"),
    )(*args)


def _mm_ln(a, b, resid, gamma, beta, *, alpha, name, tm=512, tk=None):
    M, K = a.shape
    D = b.shape[1]
    tm = min(tm, M)
    tk = K if tk is None else tk
    nk = K // tk

    def body(a_ref, b_ref, r_ref, g_ref, be_ref, y_ref, xh_ref, rs_ref, *scratch):
        part = jnp.dot(a_ref[...].astype(BF16), b_ref[...].astype(BF16), preferred_element_type=F32)

        def finish(acc):
            z = alpha * r_ref[...] + acc
            mu = jnp.mean(z, axis=-1, keepdims=True)
            zc = z - mu
            var = jnp.mean(zc * zc, axis=-1, keepdims=True)
            rstd = lax.rsqrt(var + LN_EPS)
            xhat = zc * rstd
            y_ref[...] = xhat * g_ref[...] + be_ref[...]
            xh_ref[...] = xhat.astype(BF16)
            rs_ref[...] = rstd

        if nk == 1:
            finish(part)
        else:
            acc_ref = scratch[0]
            k = pl.program_id(1)

            @pl.when(k == 0)
            def _():
                acc_ref[...] = part

            @pl.when(k > 0)
            def _():
                acc_ref[...] += part

            @pl.when(k == nk - 1)
            def _():
                finish(acc_ref[...])

    row = lambda i, k: (i, 0)
    vec = lambda i, k: (0, 0)
    return pl.pallas_call(
        body, name=name, grid=(M // tm, nk),
        in_specs=[pl.BlockSpec((tm, tk), lambda i, k: (i, k)), pl.BlockSpec((tk, D), lambda i, k: (k, 0)),
                  pl.BlockSpec((tm, D), row), pl.BlockSpec((1, D), vec), pl.BlockSpec((1, D), vec)],
        out_specs=[pl.BlockSpec((tm, D), row), pl.BlockSpec((tm, D), row), pl.BlockSpec((tm, 1), row)],
        out_shape=(jax.ShapeDtypeStruct((M, D), F32), jax.ShapeDtypeStruct((M, D), BF16),
                   jax.ShapeDtypeStruct((M, 1), F32)),
        scratch_shapes=[pltpu.VMEM((tm, D), F32)] if nk > 1 else [],
        compiler_params=_cp("parallel", "arbitrary"),
    )(a, b, resid, gamma, beta)


def _mm_tn(a, b, *, name, tka, tn, a_off=0, na=1, b_off=0, nb=1, ts=512):
    S = a.shape[0]
    ts = min(ts, S)

    def body(a_ref, b_ref, o_ref):
        s = pl.program_id(2)
        part = lax.dot_general(a_ref[...].astype(BF16), b_ref[...].astype(BF16),
                               (((0,), (0,)), ((), ())), preferred_element_type=F32)

        @pl.when(s == 0)
        def _():
            o_ref[...] = part

        @pl.when(s > 0)
        def _():
            o_ref[...] += part

    return pl.pallas_call(
        body, name=name, grid=(na, nb, S // ts),
        in_specs=[pl.BlockSpec((ts, tka), lambda i, j, s: (s, a_off + i)),
                  pl.BlockSpec((ts, tn), lambda i, j, s: (s, b_off + j))],
        out_specs=pl.BlockSpec((tka, tn), lambda i, j, s: (i, j)),
        out_shape=jax.ShapeDtypeStruct((na * tka, nb * tn), F32),
        compiler_params=_cp("parallel", "parallel", "arbitrary"),
    )(a, b)


def _rope_tables(pos, inv_lane, sign_lane, name, ts=512):
    S = pos.shape[0]
    ts = min(ts, S)

    def body(p_ref, inv_ref, sg_ref, cos_ref, sin_ref):
        ang = p_ref[...].astype(F32) * inv_ref[...]
        cos_ref[...] = jnp.cos(ang)
        sin_ref[...] = jnp.sin(ang) * sg_ref[...]

    return pl.pallas_call(
        body, name=name, grid=(S // ts,),
        in_specs=[pl.BlockSpec((ts, 1), lambda i: (i, 0)), pl.BlockSpec((1, 128), lambda i: (0, 0)),
                  pl.BlockSpec((1, 128), lambda i: (0, 0))],
        out_specs=[pl.BlockSpec((ts, 128), lambda i: (i, 0))] * 2,
        out_shape=(jax.ShapeDtypeStruct((S, 128), F32),) * 2,
        compiler_params=_cp("parallel"),
    )(pos, inv_lane, sign_lane)


def _rope_swap(t):
    lane = lax.broadcasted_iota(jnp.int32, (1, 128), 1)
    lo = (lane % HEAD_DIM) < (ROT_DIM // 2)
    return jnp.where(lo, pltpu.roll(t, 128 - ROT_DIM // 2, 1), pltpu.roll(t, ROT_DIM // 2, 1))


def _rope_fwd(t, cos, sin):
    return t * cos + _rope_swap(t) * sin


def _rope_bwd(d, cos, sin):
    lane = lax.broadcasted_iota(jnp.int32, (1, 128), 1)
    return d * cos + jnp.where((lane % HEAD_DIM) < ROT_DIM, _rope_swap(d * sin), 0.0)


def _tile_heads(t):
    lane = lax.broadcasted_iota(jnp.int32, (1, 128), 1)
    r = pltpu.roll(t, 64, 1)
    h0 = jnp.where(lane < 64, t, r)
    h1 = jnp.where(lane < 64, r, t)
    return jnp.concatenate([h0, h0], axis=1), jnp.concatenate([h1, h1], axis=1)


def _fold_heads(d0, d1):
    lane = lax.broadcasted_iota(jnp.int32, (1, 128), 1)

    def fold(d):
        s = d[:, 0:128] + d[:, 128:256]
        return s + pltpu.roll(s, 64, 1)

    return jnp.where(lane < 64, fold(d0), fold(d1))


def _band(n_keys):
    row = lax.broadcasted_iota(jnp.int32, (WINDOW, n_keys), 0)
    col = lax.broadcasted_iota(jnp.int32, (WINDOW, n_keys), 1)
    return (col > row) & (col <= row + WINDOW), col


def _attn_fwd(pq, cos_t, sin_t, sinks_b, *, name, ts=256):
    S = pq.shape[0]
    ts = min(ts, S)
    nq = ts // WINDOW
    scale = HEAD_DIM ** -0.5

    def body(cur_ref, prev_ref, cosc_ref, sinc_ref, cosp_ref, sinp_ref, sink_ref, o_ref, lse_ref):
        i = pl.program_id(0)
        cosc, sinc = cosc_ref[...], sinc_ref[...]
        q = cur_ref[:, 0:512].astype(F32)
        qr = jnp.concatenate(
            [_rope_fwd(q[:, j * 128:(j + 1) * 128], cosc, sinc) for j in range(4)], axis=1) * scale
        qr = qr.astype(BF16)
        kc = _rope_fwd(cur_ref[:, 512:640].astype(F32), cosc, sinc)
        kp = _rope_fwd(prev_ref[:, 0:128].astype(F32), cosp_ref[...], sinp_ref[...])
        k_all = jnp.concatenate([kp, kc], axis=0)
        v_all = jnp.concatenate([prev_ref[:, 128:256].astype(F32), cur_ref[:, 640:768].astype(F32)], axis=0)
        kt = [t.astype(BF16) for t in _tile_heads(k_all)]
        vt = [t.astype(BF16) for t in _tile_heads(v_all)]
        band, col = _band(2 * WINDOW)
        lane = lax.broadcasted_iota(jnp.int32, (1, 256), 1)
        for qb in range(nq):
            rows = slice(qb * WINDOW, (qb + 1) * WINDOW)
            keys = slice(qb * WINDOW, (qb + 2) * WINDOW)
            valid = band & ((col >= WINDOW) | (i * nq + qb > 0))
            for g in range(2):
                q_g = qr[rows, g * 256:(g + 1) * 256]
                acc = jnp.zeros((WINDOW, 256), F32)
                for hl in range(GROUP):
                    h = g * GROUP + hl
                    hm = (lane // HEAD_DIM) == hl
                    qm = jnp.where(hm, q_g, jnp.zeros_like(q_g))
                    s = lax.dot_general(qm, kt[g][keys], (((1,), (1,)), ((), ())), preferred_element_type=F32)
                    s = jnp.where(valid, s, MASK_VALUE)
                    sink = sink_ref[h:h + 1, 0:1]
                    m = jnp.maximum(jnp.max(s, axis=1, keepdims=True), sink)
                    e = jnp.exp(s - m)
                    l = jnp.sum(e, axis=1, keepdims=True) + jnp.exp(sink - m)
                    p = e * (1.0 / l)
                    pv = jnp.dot(p.astype(BF16), vt[g][keys], preferred_element_type=F32)
                    acc = acc + jnp.where(hm, pv, 0.0)
                    lse_ref[rows, h:h + 1] = m + jnp.log(l)
                o_ref[rows, g * 256:(g + 1) * 256] = acc.astype(BF16)

    hb = ts // WINDOW
    cur = lambda i: (i, 0)
    prev = lambda i: (jnp.maximum(i * hb - 1, 0), 0)
    return pl.pallas_call(
        body, name=name, grid=(S // ts,),
        in_specs=[pl.BlockSpec((ts, 768), cur),
                  pl.BlockSpec((WINDOW, 256), lambda i: (jnp.maximum(i * hb - 1, 0), 2)),
                  pl.BlockSpec((ts, 128), cur), pl.BlockSpec((ts, 128), cur),
                  pl.BlockSpec((WINDOW, 128), prev), pl.BlockSpec((WINDOW, 128), prev),
                  pl.BlockSpec((8, 128), lambda i: (0, 0))],
        out_specs=[pl.BlockSpec((ts, 512), cur), pl.BlockSpec((ts, 8), cur)],
        out_shape=(jax.ShapeDtypeStruct((S, 512), BF16), jax.ShapeDtypeStruct((S, 8), F32)),
        compiler_params=_cp("parallel"),
    )(pq, pq, cos_t, sin_t, cos_t, sin_t, sinks_b)


def _attn_bwd(pq, cos_t, sin_t, sinks_b, do, o, lse, *, name, ts=256):
    S = pq.shape[0]
    ts = min(ts, S)
    nq = ts // WINDOW
    nt = S // ts
    scale = HEAD_DIM ** -0.5
    NT = (((1,), (1,)), ((), ()))
    TN = (((0,), (0,)), ((), ()))

    def body(cur_ref, prev_ref, nxt_ref, cosc_ref, sinc_ref, cosp_ref, sinp_ref, cosn_ref, sinn_ref, sink_ref,
             doc_ref, don_ref, oc_ref, on_ref, lsec_ref, lsen_ref, dpq_ref, dsink_ref):
        i = pl.program_id(0)
        last = i == nt - 1
        cosc, sinc = cosc_ref[...], sinc_ref[...]
        cose = jnp.concatenate([cosc, cosn_ref[...]], axis=0)
        sine = jnp.concatenate([sinc, sinn_ref[...]], axis=0)
        q = jnp.concatenate([cur_ref[:, 0:512], nxt_ref[:, 0:512]], axis=0).astype(F32)
        qr = jnp.concatenate(
            [_rope_fwd(q[:, j * 128:(j + 1) * 128], cose, sine) for j in range(4)], axis=1) * scale
        qr = qr.astype(BF16)
        kc = _rope_fwd(cur_ref[:, 512:640].astype(F32), cosc, sinc)
        kp = _rope_fwd(prev_ref[:, 0:128].astype(F32), cosp_ref[...], sinp_ref[...])
        k_all = jnp.concatenate([kp, kc], axis=0)
        v_all = jnp.concatenate([prev_ref[:, 128:256].astype(F32), cur_ref[:, 640:768].astype(F32)], axis=0)
        kt = [t.astype(BF16) for t in _tile_heads(k_all)]
        vt = [t.astype(BF16) for t in _tile_heads(v_all)]
        don = jnp.where(last, jnp.zeros_like(don_ref[...]), don_ref[...])
        do_e = jnp.concatenate([doc_ref[...], don], axis=0)
        o_e = jnp.concatenate([oc_ref[...], on_ref[...]], axis=0)
        band2, col2 = _band(2 * WINDOW)
        band1, _ = _band(WINDOW)
        lane = lax.broadcasted_iota(jnp.int32, (1, 256), 1)

        @pl.when(i == 0)
        def _():
            dsink_ref[...] = jnp.zeros_like(dsink_ref)

        dk_acc = [[None] * (nq + 1) for _ in range(2)]
        dv_acc = [[None] * (nq + 1) for _ in range(2)]

        def add(acc, g, e, val):
            acc[g][e] = val if acc[g][e] is None else acc[g][e] + val

        for qb in range(nq + 1):
            halo = qb == nq
            rows = slice(qb * WINDOW, (qb + 1) * WINDOW)
            if halo:
                keys = slice(qb * WINDOW, (qb + 1) * WINDOW)
                valid = band1 & jnp.logical_not(last)
            else:
                keys = slice(qb * WINDOW, (qb + 2) * WINDOW)
                valid = band2 & ((col2 >= WINDOW) | (i * nq + qb > 0))
            dq_parts = []
            for g in range(2):
                q_g = qr[rows, g * 256:(g + 1) * 256]
                do_g = do_e[rows, g * 256:(g + 1) * 256]
                o_g = o_e[rows, g * 256:(g + 1) * 256].astype(F32)
                kt_b, vt_b = kt[g][keys], vt[g][keys]
                dq_g = jnp.zeros((WINDOW, 256), F32)
                dk_g = None
                dv_g = None
                for hl in range(GROUP):
                    h = g * GROUP + hl
                    hm = (lane // HEAD_DIM) == hl
                    qm = jnp.where(hm, q_g, jnp.zeros_like(q_g))
                    dom = jnp.where(hm, do_g, jnp.zeros_like(do_g))
                    big_l = lsen_ref[:, h:h + 1] if halo else lsec_ref[rows, h:h + 1]
                    s = lax.dot_general(qm, kt_b, NT, preferred_element_type=F32)
                    p = jnp.exp(jnp.where(valid, s, MASK_VALUE) - big_l)
                    dp = lax.dot_general(dom, vt_b, NT, preferred_element_type=F32)
                    delta = jnp.sum(dom.astype(F32) * o_g, axis=1, keepdims=True)
                    ds = (p * (dp - delta)).astype(BF16)
                    dk_h = lax.dot_general(ds, qm, TN, preferred_element_type=F32)
                    dv_h = lax.dot_general(p.astype(BF16), dom, TN, preferred_element_type=F32)
                    dk_g = dk_h if dk_g is None else dk_g + dk_h
                    dv_g = dv_h if dv_g is None else dv_g + dv_h
                    if not halo:
                        dq_h = jnp.dot(ds, kt_b, preferred_element_type=F32)
                        dq_g = dq_g + jnp.where(hm, dq_h, 0.0)
                        sink = sink_ref[h:h + 1, 0:1]
                        dsink_h = -jnp.sum(jnp.exp(sink - big_l) * delta, axis=0, keepdims=True)
                        dsink_ref[h:h + 1, :] += jnp.broadcast_to(dsink_h, (1, 128))
                dq_parts.append(dq_g)
                add(dk_acc, g, qb, dk_g[0:WINDOW])
                add(dv_acc, g, qb, dv_g[0:WINDOW])
                if not halo:
                    add(dk_acc, g, qb + 1, dk_g[WINDOW:2 * WINDOW])
                    add(dv_acc, g, qb + 1, dv_g[WINDOW:2 * WINDOW])
            if not halo:
                cs, sn = cosc[rows], sinc[rows]
                for g in range(2):
                    dq_g = dq_parts[g] * scale
                    for j in range(2):
                        c0 = g * 256 + j * 128
                        dpq_ref[rows, c0:c0 + 128] = _rope_bwd(dq_g[:, j * 128:(j + 1) * 128], cs, sn).astype(BF16)
        for e in range(1, nq + 1):
            rows = slice((e - 1) * WINDOW, e * WINDOW)
            dk = _fold_heads(dk_acc[0][e], dk_acc[1][e])
            dv = _fold_heads(dv_acc[0][e], dv_acc[1][e])
            dpq_ref[rows, 512:640] = _rope_bwd(dk, cosc[rows], sinc[rows]).astype(BF16)
            dpq_ref[rows, 640:768] = dv.astype(BF16)

    hb = ts // WINDOW
    nblk = S // WINDOW
    cur = lambda i: (i, 0)
    prev = lambda i: (jnp.maximum(i * hb - 1, 0), 0)
    nxt = lambda i: (jnp.minimum((i + 1) * hb, nblk - 1), 0)
    return pl.pallas_call(
        body, name=name, grid=(nt,),
        in_specs=[pl.BlockSpec((ts, 768), cur),
                  pl.BlockSpec((WINDOW, 256), lambda i: (jnp.maximum(i * hb - 1, 0), 2)),
                  pl.BlockSpec((WINDOW, 768), nxt),
                  pl.BlockSpec((ts, 128), cur), pl.BlockSpec((ts, 128), cur),
                  pl.BlockSpec((WINDOW, 128), prev), pl.BlockSpec((WINDOW, 128), prev),
                  pl.BlockSpec((WINDOW, 128), nxt), pl.BlockSpec((WINDOW, 128), nxt),
                  pl.BlockSpec((8, 128), lambda i: (0, 0)),
                  pl.BlockSpec((ts, 512), cur), pl.BlockSpec((WINDOW, 512), nxt),
                  pl.BlockSpec((ts, 512), cur), pl.BlockSpec((WINDOW, 512), nxt),
                  pl.BlockSpec((ts, 8), cur), pl.BlockSpec((WINDOW, 8), nxt)],
        out_specs=[pl.BlockSpec((ts, 768), cur), pl.BlockSpec((8, 128), lambda i: (0, 0))],
        out_shape=(jax.ShapeDtypeStruct((S, 768), BF16), jax.ShapeDtypeStruct((8, 128), F32)),
        compiler_params=_cp("arbitrary"),
    )(pq, pq, pq, cos_t, sin_t, cos_t, sin_t, cos_t, sin_t, sinks_b, do, do, o, o, lse, lse)


def _shift_dn(x, k):
    return pltpu.roll(x, k, 0)


def _shift_up(x, k):
    return pltpu.roll(x, x.shape[0] - k, 0)


def _pool_lane_select(vals):
    lane = lax.broadcasted_iota(jnp.int32, (1, 256), 1)
    out = vals[3]
    for g in (2, 1, 0):
        out = jnp.where(lane < 64 * (g + 1), vals[g], out)
    return out


def _pool_inv_count(t0, n):
    t = t0 + lax.broadcasted_iota(jnp.int32, (n, 256), 0)
    lane = lax.broadcasted_iota(jnp.int32, (n, 256), 1)
    w = jnp.where(lane < 64, 2, jnp.where(lane < 128, 4, jnp.where(lane < 192, 8, 16)))
    return 1.0 / jnp.minimum(t + 1, w).astype(F32)


def _pooled(u_ext, t0, n):
    s2 = u_ext + _shift_dn(u_ext, 1)
    s4 = s2 + _shift_dn(s2, 2)
    s8 = s4 + _shift_dn(s4, 4)
    s16 = s8 + _shift_dn(s8, 8)
    win = _pool_lane_select([s2, s4, s8, s16])[HALO:HALO + n]
    return win * _pool_inv_count(t0, n) - u_ext[HALO:HALO + n]


def _poolconv_fwd(pp, wbd, pool_scale, conv_w, *, name, ts=512):
    S = pp.shape[0]
    ts = min(ts, S)

    def body(cur_ref, prev_ref, wbd_ref, sc_ref, cw_ref, oa_ref, oc_ref):
        i = pl.program_id(0)
        prev = jnp.where(i > 0, prev_ref[...].astype(F32), 0.0)
        u_ext = jnp.concatenate([prev[:, 0:256], cur_ref[:, 0:256].astype(F32)], axis=0)
        pooled = _pooled(u_ext, i * ts, ts)
        mixed = jnp.dot(pooled.astype(BF16), wbd_ref[...], preferred_element_type=F32)
        oa_ref[...] = (mixed * sc_ref[...]).astype(BF16)
        v_ext = jnp.concatenate([prev[:, 256:512] * prev[:, 768:1024],
                                 cur_ref[:, 256:512].astype(F32) * cur_ref[:, 768:1024].astype(F32)], axis=0)
        cv = cw_ref[2:3, :] * v_ext + cw_ref[1:2, :] * _shift_dn(v_ext, 1) + cw_ref[0:1, :] * _shift_dn(v_ext, 2)
        oc_ref[...] = (cur_ref[:, 512:768].astype(F32) * cv[HALO:HALO + ts]).astype(BF16)

    hb = ts // HALO
    cur = lambda i: (i, 0)
    const = lambda i: (0, 0)
    return pl.pallas_call(
        body, name=name, grid=(S // ts,),
        in_specs=[pl.BlockSpec((ts, 1024), cur),
                  pl.BlockSpec((HALO, 1024), lambda i: (jnp.maximum(i * hb - 1, 0), 0)),
                  pl.BlockSpec((256, 256), const), pl.BlockSpec((1, 256), const), pl.BlockSpec((3, 256), const)],
        out_specs=[pl.BlockSpec((ts, 256), cur)] * 2,
        out_shape=(jax.ShapeDtypeStruct((S, 256), BF16),) * 2,
        compiler_params=_cp("parallel"),
    )(pp, pp, wbd, pool_scale, conv_w)


def _poolconv_bwd(pp, do_a, do_c, wbd, wbd_t, pool_scale, conv_w, *, name, ts=512):
    S = pp.shape[0]
    ts = min(ts, S)
    nt = S // ts
    n_e = ts + 2 * HALO

    def body(cur_ref, prev_ref, nxt_ref, dac_ref, dan_ref, dcc_ref, dcn_ref, wbd_ref, wbdt_ref, sc_ref, cw_ref,
             dpp_ref, pooled_ref, dmixed_ref, dsc_ref, dcw_ref):
        i = pl.program_id(0)

        @pl.when(i == 0)
        def _():
            dsc_ref[...] = jnp.zeros_like(dsc_ref)
            dcw_ref[...] = jnp.zeros_like(dcw_ref)

        prev = jnp.where(i > 0, prev_ref[...].astype(F32), 0.0)
        nxt = nxt_ref[...].astype(F32)
        cur = cur_ref[...].astype(F32)
        not_last = i < nt - 1
        da_n = jnp.where(not_last, dan_ref[...].astype(F32), 0.0)
        dc_n = jnp.where(not_last, dcn_ref[...].astype(F32), 0.0)
        zeros_h = jnp.zeros((HALO, 256), F32)
        sc = sc_ref[...]

        u_ext = jnp.concatenate([prev[:, 0:256], cur[:, 0:256]], axis=0)
        pooled = _pooled(u_ext, i * ts, ts)
        pooled_b = pooled.astype(BF16)
        pooled_ref[...] = pooled_b
        mixed = jnp.dot(pooled_b, wbd_ref[...], preferred_element_type=F32)
        da_c = dac_ref[...].astype(F32)
        dsc_ref[...] += jnp.sum(da_c * mixed, axis=0, keepdims=True)
        dmixed_e = jnp.concatenate([da_c, da_n], axis=0) * sc
        dmixed_ref[...] = dmixed_e[0:ts].astype(BF16)
        dpooled = jnp.dot(dmixed_e.astype(BF16), wbdt_ref[...], preferred_element_type=F32)
        qd = dpooled * _pool_inv_count(i * ts, ts + HALO)
        f2 = qd + _shift_up(qd, 1)
        f4 = f2 + _shift_up(f2, 2)
        f8 = f4 + _shift_up(f4, 4)
        f16 = f8 + _shift_up(f8, 8)
        du = (_pool_lane_select([f2, f4, f8, f16]) - dpooled)[0:ts]
        dpp_ref[:, 0:256] = du.astype(BF16)

        xc_e = jnp.concatenate([prev[:, 256:512], cur[:, 256:512], nxt[:, 256:512]], axis=0)
        gc_e = jnp.concatenate([prev[:, 768:1024], cur[:, 768:1024], nxt[:, 768:1024]], axis=0)
        gb_e = jnp.concatenate([zeros_h, cur[:, 512:768], nxt[:, 512:768]], axis=0)
        dc_e = jnp.concatenate([zeros_h, dcc_ref[...].astype(F32), dc_n], axis=0)
        v_e = xc_e * gc_e
        v1, v2 = _shift_dn(v_e, 1), _shift_dn(v_e, 2)
        w0, w1, w2 = cw_ref[0:1, :], cw_ref[1:2, :], cw_ref[2:3, :]
        cv = w2 * v_e + w1 * v1 + w0 * v2
        dcv = dc_e * gb_e
        dv = w2 * dcv + w1 * _shift_up(dcv, 1) + w0 * _shift_up(dcv, 2)
        tile = slice(HALO, HALO + ts)
        dpp_ref[:, 256:512] = (dv * gc_e)[tile].astype(BF16)
        dpp_ref[:, 512:768] = (dc_e * cv)[tile].astype(BF16)
        dpp_ref[:, 768:1024] = (dv * xc_e)[tile].astype(BF16)
        dcv_t = dcv[tile]
        dcw_ref[0:1, :] += jnp.sum(dcv_t * v2[tile], axis=0, keepdims=True)
        dcw_ref[1:2, :] += jnp.sum(dcv_t * v1[tile], axis=0, keepdims=True)
        dcw_ref[2:3, :] += jnp.sum(dcv_t * v_e[tile], axis=0, keepdims=True)

    hb = ts // HALO
    nblk = S // HALO
    cur = lambda i: (i, 0)
    const = lambda i: (0, 0)
    prev = lambda i: (jnp.maximum(i * hb - 1, 0), 0)
    nxt = lambda i: (jnp.minimum((i + 1) * hb, nblk - 1), 0)
    del n_e
    return pl.pallas_call(
        body, name=name, grid=(nt,),
        in_specs=[pl.BlockSpec((ts, 1024), cur), pl.BlockSpec((HALO, 1024), prev), pl.BlockSpec((HALO, 1024), nxt),
                  pl.BlockSpec((ts, 256), cur), pl.BlockSpec((HALO, 256), nxt),
                  pl.BlockSpec((ts, 256), cur), pl.BlockSpec((HALO, 256), nxt),
                  pl.BlockSpec((256, 256), const), pl.BlockSpec((256, 256), const),
                  pl.BlockSpec((1, 256), const), pl.BlockSpec((3, 256), const)],
        out_specs=[pl.BlockSpec((ts, 1024), cur), pl.BlockSpec((ts, 256), cur), pl.BlockSpec((ts, 256), cur),
                   pl.BlockSpec((1, 256), const), pl.BlockSpec((3, 256), const)],
        out_shape=(jax.ShapeDtypeStruct((S, 1024), BF16), jax.ShapeDtypeStruct((S, 256), BF16),
                   jax.ShapeDtypeStruct((S, 256), BF16), jax.ShapeDtypeStruct((1, 256), F32),
                   jax.ShapeDtypeStruct((3, 256), F32)),
        compiler_params=_cp("arbitrary"),
    )(pp, pp, pp, do_a, do_a, do_c, do_c, wbd, wbd_t, pool_scale, conv_w)


def _sigmoid(x):
    return 1.0 / (1.0 + jnp.exp(-x))


def _merge_fwd(o_a, o_b, o_c, glog, w_br, *, name, ts=512):
    S = o_a.shape[0]
    D = w_br.shape[1]
    ts = min(ts, S)

    def body(oa_ref, ob_ref, oc_ref, gl_ref, w_ref, m_ref):
        pa = jnp.dot(oa_ref[...], w_ref[0:256, :], preferred_element_type=F32)
        pb = jnp.dot(ob_ref[...], w_ref[256:768, :], preferred_element_type=F32)
        pc = jnp.dot(oc_ref[...], w_ref[768:1024, :], preferred_element_type=F32)
        m = _sigmoid(gl_ref[:, 0:D].astype(F32)) * pa
        m = m + _sigmoid(gl_ref[:, D:2 * D].astype(F32)) * pb
        m = m + _sigmoid(gl_ref[:, 2 * D:3 * D].astype(F32)) * pc
        m_ref[...] = m.astype(BF16)

    cur = lambda i: (i, 0)
    return pl.pallas_call(
        body, name=name, grid=(S // ts,),
        in_specs=[pl.BlockSpec((ts, 256), cur), pl.BlockSpec((ts, 512), cur), pl.BlockSpec((ts, 256), cur),
                  pl.BlockSpec((ts, 3 * D), cur), pl.BlockSpec((1024, D), lambda i: (0, 0))],
        out_specs=pl.BlockSpec((ts, D), cur),
        out_shape=jax.ShapeDtypeStruct((S, D), BF16),
        compiler_params=_cp("parallel"),
    )(o_a, o_b, o_c, glog, w_br)


def _merge_bwd(dm, o_a, o_b, o_c, glog, w_br, w_br_t, *, name, ts=256):
    S = o_a.shape[0]
    D = w_br.shape[1]
    ts = min(ts, S)

    def body(dm_ref, oa_ref, ob_ref, oc_ref, gl_ref, w_ref, wt_ref, dgl_ref, dp_ref, doa_ref, dob_ref, doc_ref):
        dmv = dm_ref[...].astype(F32)
        branches = ((oa_ref, 0, 256, doa_ref), (ob_ref, 256, 768, dob_ref), (oc_ref, 768, 1024, doc_ref))
        for b, (o_ref, r0, r1, do_ref) in enumerate(branches):
            prod = jnp.dot(o_ref[...], w_ref[r0:r1, :], preferred_element_type=F32)
            gate = _sigmoid(gl_ref[:, b * D:(b + 1) * D].astype(F32))
            dgl_ref[:, b * D:(b + 1) * D] = (dmv * prod * gate * (1.0 - gate)).astype(BF16)
            dprod = (dmv * gate).astype(BF16)
            dp_ref[:, b * D:(b + 1) * D] = dprod
            do_ref[...] = jnp.dot(dprod, wt_ref[:, r0:r1], preferred_element_type=F32).astype(BF16)

    cur = lambda i: (i, 0)
    const = lambda i: (0, 0)
    return pl.pallas_call(
        body, name=name, grid=(S // ts,),
        in_specs=[pl.BlockSpec((ts, D), cur), pl.BlockSpec((ts, 256), cur), pl.BlockSpec((ts, 512), cur),
                  pl.BlockSpec((ts, 256), cur), pl.BlockSpec((ts, 3 * D), cur),
                  pl.BlockSpec((1024, D), const), pl.BlockSpec((D, 1024), const)],
        out_specs=[pl.BlockSpec((ts, 3 * D), cur), pl.BlockSpec((ts, 3 * D), cur), pl.BlockSpec((ts, 256), cur),
                   pl.BlockSpec((ts, 512), cur), pl.BlockSpec((ts, 256), cur)],
        out_shape=(jax.ShapeDtypeStruct((S, 3 * D), BF16), jax.ShapeDtypeStruct((S, 3 * D), BF16),
                   jax.ShapeDtypeStruct((S, 256), BF16), jax.ShapeDtypeStruct((S, 512), BF16),
                   jax.ShapeDtypeStruct((S, 256), BF16)),
        compiler_params=_cp("parallel"),
    )(dm, o_a, o_b, o_c, glog, w_br, w_br_t)


def _ffn_act_fwd(up_pre, fcw, *, name, tc, ts=512):
    S, F2 = up_pre.shape
    ts = min(ts, S)
    nj = F2 // (2 * tc)

    def body(cur_ref, prev_ref, w_ref, h_ref):
        i = pl.program_id(1)
        prev = jnp.where(i > 0, prev_ref[...].astype(F32), 0.0)
        x = jnp.concatenate([prev, cur_ref[...].astype(F32)], axis=0)
        up = (w_ref[2:3, :] * x + w_ref[1:2, :] * _shift_dn(x, 1) + w_ref[0:1, :] * _shift_dn(x, 2))[HALO:HALO + ts]
        a, b = up[:, 0:tc], up[:, tc:2 * tc]
        h_ref[...] = (a * _sigmoid(a) * b).astype(BF16)

    hb = ts // HALO
    return pl.pallas_call(
        body, name=name, grid=(nj, S // ts),
        in_specs=[pl.BlockSpec((ts, 2 * tc), lambda j, i: (i, j)),
                  pl.BlockSpec((HALO, 2 * tc), lambda j, i: (jnp.maximum(i * hb - 1, 0), j)),
                  pl.BlockSpec((3, 2 * tc), lambda j, i: (0, j))],
        out_specs=pl.BlockSpec((ts, tc), lambda j, i: (i, j)),
        out_shape=jax.ShapeDtypeStruct((S, F2 // 2), BF16),
        compiler_params=_cp("parallel", "parallel"),
    )(up_pre, up_pre, fcw)


def _ffn_act_bwd(up_pre, dh, fcw, *, name, tc, ts=512):
    S, F2 = up_pre.shape
    ts = min(ts, S)
    nt = S // ts
    nj = F2 // (2 * tc)

    def body(cur_ref, prev_ref, nxt_ref, dhc_ref, dhn_ref, w_ref, dpre_ref, dw_ref):
        i = pl.program_id(1)

        @pl.when(i == 0)
        def _():
            dw_ref[...] = jnp.zeros_like(dw_ref)

        prev = jnp.where(i > 0, prev_ref[...].astype(F32), 0.0)
        x = jnp.concatenate([prev, cur_ref[...].astype(F32), nxt_ref[...].astype(F32)], axis=0)
        dh_n = jnp.where(i < nt - 1, dhn_ref[...].astype(F32), 0.0)
        dh_e = jnp.concatenate([jnp.zeros((HALO, tc), F32), dhc_ref[...].astype(F32), dh_n], axis=0)
        w0, w1, w2 = w_ref[0:1, :], w_ref[1:2, :], w_ref[2:3, :]
        x1, x2 = _shift_dn(x, 1), _shift_dn(x, 2)
        up = w2 * x + w1 * x1 + w0 * x2
        a, b = up[:, 0:tc], up[:, tc:2 * tc]
        sg = _sigmoid(a)
        da = dh_e * b * (sg * (1.0 + a * (1.0 - sg)))
        db = dh_e * (a * sg)
        dup = jnp.concatenate([da, db], axis=1)
        dpre = w2 * dup + w1 * _shift_up(dup, 1) + w0 * _shift_up(dup, 2)
        tile = slice(HALO, HALO + ts)
        dpre_ref[...] = dpre[tile].astype(BF16)
        dup_t = dup[tile]
        dw_ref[0:1, :] += jnp.sum(dup_t * x2[tile], axis=0, keepdims=True)
        dw_ref[1:2, :] += jnp.sum(dup_t * x1[tile], axis=0, keepdims=True)
        dw_ref[2:3, :] += jnp.sum(dup_t * x[tile], axis=0, keepdims=True)

    hb = ts // HALO
    nblk = S // HALO
    prev = lambda j, i: (jnp.maximum(i * hb - 1, 0), j)
    nxt = lambda j, i: (jnp.minimum((i + 1) * hb, nblk - 1), j)
    return pl.pallas_call(
        body, name=name, grid=(nj, nt),
        in_specs=[pl.BlockSpec((ts, 2 * tc), lambda j, i: (i, j)), pl.BlockSpec((HALO, 2 * tc), prev),
                  pl.BlockSpec((HALO, 2 * tc), nxt),
                  pl.BlockSpec((ts, tc), lambda j, i: (i, j)), pl.BlockSpec((HALO, tc), nxt),
                  pl.BlockSpec((3, 2 * tc), lambda j, i: (0, j))],
        out_specs=[pl.BlockSpec((ts, 2 * tc), lambda j, i: (i, j)), pl.BlockSpec((3, 2 * tc), lambda j, i: (0, j))],
        out_shape=(jax.ShapeDtypeStruct((S, F2), BF16), jax.ShapeDtypeStruct((3, F2), F32)),
        compiler_params=_cp("parallel", "arbitrary"),
    )(up_pre, up_pre, up_pre, dh, dh, fcw)


def _ln_bwd(dy, xhat, rstd, gamma, *, name, ts=512):
    S, D = dy.shape
    ts = min(ts, S)

    def body(dy_ref, xh_ref, rs_ref, g_ref, dz_ref, dg_ref, db_ref):
        @pl.when(pl.program_id(0) == 0)
        def _():
            dg_ref[...] = jnp.zeros_like(dg_ref)
            db_ref[...] = jnp.zeros_like(db_ref)

        dyv = dy_ref[...]
        xh = xh_ref[...].astype(F32)
        dyg = dyv * g_ref[...]
        c1 = jnp.mean(dyg, axis=-1, keepdims=True)
        c2 = jnp.mean(dyg * xh, axis=-1, keepdims=True)
        dz_ref[...] = rs_ref[...] * (dyg - c1 - xh * c2)
        dg_ref[...] += jnp.sum(dyv * xh, axis=0, keepdims=True)
        db_ref[...] += jnp.sum(dyv, axis=0, keepdims=True)

    cur = lambda i: (i, 0)
    const = lambda i: (0, 0)
    return pl.pallas_call(
        body, name=name, grid=(S // ts,),
        in_specs=[pl.BlockSpec((ts, D), cur), pl.BlockSpec((ts, D), cur), pl.BlockSpec((ts, 1), cur),
                  pl.BlockSpec((1, D), const)],
        out_specs=[pl.BlockSpec((ts, D), cur), pl.BlockSpec((1, D), const), pl.BlockSpec((1, D), const)],
        out_shape=(jax.ShapeDtypeStruct((S, D), F32), jax.ShapeDtypeStruct((1, D), F32),
                   jax.ShapeDtypeStruct((1, D), F32)),
        compiler_params=_cp("arbitrary"),
    )(dy, xhat, rstd, gamma)


def _loss_head(y, tgt, *, name, ts=512):
    S, D = y.shape
    ts = min(ts, S)

    def body(y_ref, t_ref, dy_ref, sq_ref):
        @pl.when(pl.program_id(0) == 0)
        def _():
            sq_ref[...] = jnp.zeros_like(sq_ref)

        e = y_ref[...] - t_ref[...]
        dy_ref[...] = e * (1.0 / D)
        sq_ref[...] += jnp.sum(e * e, axis=0, keepdims=True)

    cur = lambda i: (i, 0)
    return pl.pallas_call(
        body, name=name, grid=(S // ts,),
        in_specs=[pl.BlockSpec((ts, D), cur), pl.BlockSpec((ts, D), cur)],
        out_specs=[pl.BlockSpec((ts, D), cur), pl.BlockSpec((1, D), lambda i: (0, 0))],
        out_shape=(jax.ShapeDtypeStruct((S, D), F32), jax.ShapeDtypeStruct((1, D), F32)),
        compiler_params=_cp("arbitrary"),
    )(y, tgt)


def _adamw(w, g, m, v, *, name, tr=512):
    R, C = w.shape
    tr = _div_tile(R, tr)
    c1 = 1.0 - ADAM_B1 ** ADAM_STEP
    c2 = 1.0 - ADAM_B2 ** ADAM_STEP

    def body(w_ref, g_ref, m_ref, v_ref, d_ref, mo_ref, vo_ref):
        gv = g_ref[...]
        m2 = ADAM_B1 * m_ref[...] + (1.0 - ADAM_B1) * gv
        v2 = ADAM_B2 * v_ref[...] + (1.0 - ADAM_B2) * (gv * gv)
        m_hat = m2 / c1
        v_hat = v2 / c2
        d_ref[...] = -ADAM_LR * (m_hat / (jnp.sqrt(v_hat) + ADAM_EPS) + ADAM_WD * w_ref[...])
        mo_ref[...] = m2
        vo_ref[...] = v2

    spec = pl.BlockSpec((tr, C), lambda i: (i, 0))
    return pl.pallas_call(
        body, name=name, grid=(R // tr,),
        in_specs=[spec] * 4, out_specs=[spec] * 3,
        out_shape=(jax.ShapeDtypeStruct((R, C), F32),) * 3,
        compiler_params=_cp("parallel"),
    )(w, g, m, v)


def _interleave_cols(w, nj):
    lead, f2 = w.shape[:-1], w.shape[-1]
    tc = f2 // (2 * nj)
    w = w.reshape(lead + (2, nj, tc))
    return jnp.swapaxes(w, -3, -2).reshape(lead + (f2,))


def _deinterleave_cols(w, nj):
    lead, f2 = w.shape[:-1], w.shape[-1]
    tc = f2 // (2 * nj)
    w = w.reshape(lead + (nj, 2, tc))
    return jnp.swapaxes(w, -3, -2).reshape(lead + (f2,))


def _block_diag(w_pool):
    return jnp.concatenate([jnp.pad(w_pool[g], ((0, 0), (64 * g, 192 - 64 * g))) for g in range(4)], axis=0)


def _pad_rows(v, rows):
    return jnp.pad(v, (0, rows * LANES - v.shape[0])).reshape(rows, LANES)


def kernel(x, positions, w_in, w_pool, pool_scale, attn_sinks, conv_w, w_branch_a, w_branch_b, w_branch_c, w_o, ln1_g, ln1_b, w_up, ffn_conv_w, w_down, ln2_g, ln2_b, loss_target, m_w_in, m_w_pool, m_pool_scale, m_attn_sinks, m_conv_w, m_w_branch_a, m_w_branch_b, m_w_branch_c, m_w_o, m_ln1_g, m_ln1_b, m_w_up, m_ffn_conv_w, m_w_down, m_ln2_g, m_ln2_b, v_w_in, v_w_pool, v_pool_scale, v_attn_sinks, v_conv_w, v_w_branch_a, v_w_branch_b, v_w_branch_c, v_w_o, v_ln1_g, v_ln1_b, v_w_up, v_ffn_conv_w, v_w_down, v_ln2_g, v_ln2_b):
    L, D, in_shard = w_in.shape
    S = x.shape[1]
    IN = in_shard * N_DEV
    F2 = w_up.shape[2] * N_DEV
    F = F2 // 2
    assert D == 1024 and IN == 1792 + 3 * D and x.shape[0] == 1 and S % 512 == 0
    alpha = (2 * L) ** 0.25
    NJ = 2
    TC = F // NJ
    xs = x.reshape(S, D)
    tgt = loss_target.reshape(S, D)

    big = [w_in, w_branch_a, w_branch_b, w_branch_c, w_o, w_up, w_down]
    big_rows = [a.size // LANES for a in big]
    packed = jnp.concatenate([a.reshape(-1, LANES).astype(BF16) for a in big], axis=0)
    gathered = _all_gather(packed, "ag_weights")
    n_cw, n_fw = conv_w.size, ffn_conv_w.size
    small_rows = -(-(n_cw + n_fw) // LANES)
    small = _pad_rows(jnp.concatenate([conv_w.reshape(-1), ffn_conv_w.reshape(-1)]), small_rows)
    gsmall = _all_gather(small, "ag_conv_weights").reshape(N_DEV, -1)
    conv_full = gsmall[:, :n_cw].reshape(N_DEV, L, 3, -1).transpose(1, 2, 0, 3).reshape(L, 3, 256)
    fcw_full = gsmall[:, n_cw:n_cw + n_fw].reshape(N_DEV, L, 3, -1).transpose(1, 2, 0, 3).reshape(L, 3, F2)
    fcw_full = _interleave_cols(fcw_full, NJ)

    offs = [0]
    for r in big_rows:
        offs.append(offs[-1] + r)

    def shard_of(widx, l, shape):
        r = big_rows[widx] // L
        return gathered[:, offs[widx] + l * r: offs[widx] + (l + 1) * r, :].reshape((N_DEV,) + shape)

    W = []
    for l in range(L):
        win = shard_of(0, l, (D, in_shard)).transpose(1, 0, 2).reshape(D, IN)
        wg = win[:, 1792:]
        wp = jnp.concatenate([win[:, 0:256], win[:, 1024:1792]], axis=1)
        wq = win[:, 256:1024]
        wa = shard_of(1, l, (256, D // N_DEV)).transpose(1, 0, 2).reshape(256, D)
        wb = shard_of(2, l, (512, D // N_DEV)).transpose(1, 0, 2).reshape(512, D)
        wc = shard_of(3, l, (256, D // N_DEV)).transpose(1, 0, 2).reshape(256, D)
        wbr = jnp.concatenate([wa, wb, wc], axis=0)
        wo = shard_of(4, l, (D // N_DEV, D)).reshape(D, D)
        wup = _interleave_cols(shard_of(5, l, (D, F2 // N_DEV)).transpose(1, 0, 2).reshape(D, F2), NJ)
        wdn = shard_of(6, l, (F // N_DEV, D)).reshape(F, D)
        wbd = _block_diag(w_pool[l]).astype(BF16)
        W.append(dict(wg=wg, wp=wp, wq=wq, wg_t=wg.T, wp_t=wp.T, wq_t=wq.T, wbr=wbr, wbr_t=wbr.T,
                      wo=wo, wo_t=wo.T, wup=wup, wup_t=wup.T, wdn=wdn, wdn_t=wdn.T,
                      wbd=wbd, wbd_t=wbd.T, scale=pool_scale[l].reshape(1, 256), conv=conv_full[l],
                      fcw=fcw_full[l], sinks=jnp.broadcast_to(attn_sinks[l].reshape(8, 1), (8, 128)),
                      g1=ln1_g[l].reshape(1, D), b1=ln1_b[l].reshape(1, D),
                      g2=ln2_g[l].reshape(1, D), b2=ln2_b[l].reshape(1, D)))

    inv_freq = ROPE_THETA ** (-jnp.arange(0, ROT_DIM, 2, dtype=F32) / ROT_DIM)
    head_lane = jnp.concatenate([inv_freq, inv_freq, jnp.zeros((HEAD_DIM - ROT_DIM,), F32)])
    head_sign = jnp.concatenate([-jnp.ones((8,), F32), jnp.ones((8,), F32), jnp.zeros((HEAD_DIM - ROT_DIM,), F32)])
    inv_lane = jnp.tile(head_lane, 2).reshape(1, 128)
    sign_lane = jnp.tile(head_sign, 2).reshape(1, 128)
    cos_t, sin_t = _rope_tables(positions.reshape(S, 1), inv_lane, sign_lane, "rope_tables")

    saved = []
    h_in = xs
    for l in range(L):
        w = W[l]
        pg = _mm(h_in, w["wg"], out_dtype=BF16, name="proj_gate", tn=1024)
        pp = _mm(h_in, w["wp"], out_dtype=BF16, name="proj_poolconv")
        pq = _mm(h_in, w["wq"], out_dtype=BF16, name="proj_qkv")
        o_a, o_c = _poolconv_fwd(pp, w["wbd"], w["scale"], w["conv"], name="poolconv_fwd")
        o_b, lse = _attn_fwd(pq, cos_t, sin_t, w["sinks"], name="attn_fwd")
        merged = _merge_fwd(o_a, o_b, o_c, pg, w["wbr"], name="merge_fwd")
        x1, xh1, rs1 = _mm_ln(merged, w["wo"], h_in, w["g1"], w["b1"], alpha=alpha, name="wo_ln1")
        up_pre = _mm(x1, w["wup"], out_dtype=BF16, name="ffn_up", tn=2 * TC)
        hact = _ffn_act_fwd(up_pre, w["fcw"], name="ffn_act_fwd", tc=TC, ts=256)
        x2, xh2, rs2 = _mm_ln(hact, w["wdn"], x1, w["g2"], w["b2"], alpha=alpha, name="down_ln2")
        saved.append(dict(x0=h_in, pg=pg, pp=pp, pq=pq, o_a=o_a, o_b=o_b, o_c=o_c, lse=lse, merged=merged,
                          x1=x1, xh1=xh1, rs1=rs1, up_pre=up_pre, hact=hact, xh2=xh2, rs2=rs2))
        h_in = x2

    dy, sq_lanes = _loss_head(h_in, tgt, name="loss_head")

    gw = [None] * L
    for l in reversed(range(L)):
        w, sv = W[l], saved[l]
        dz2, dg2, db2 = _ln_bwd(dy, sv["xh2"], sv["rs2"], w["g2"], name="ln2_bwd")
        dh = _mm(dz2, w["wdn_t"], out_dtype=BF16, name="down_bwd_x", tn=TC)
        dw_dn = _mm_tn(sv["hact"], dz2, name="down_bwd_w", tka=TC, na=NJ, tn=D)
        dpre, dfcw = _ffn_act_bwd(sv["up_pre"], dh, w["fcw"], name="ffn_act_bwd", tc=TC, ts=256)
        dx1 = _mm(dpre, w["wup_t"], out_dtype=F32, name="up_bwd_x", tk=2 * TC, add=dz2, add_scale=alpha)
        dw_up = _mm_tn(sv["x1"], dpre, name="up_bwd_w", tka=D, tn=2 * TC, nb=NJ)
        dz1, dg1, db1 = _ln_bwd(dx1, sv["xh1"], sv["rs1"], w["g1"], name="ln1_bwd")
        dmerged = _mm(dz1, w["wo_t"], out_dtype=BF16, name="wo_bwd_x")
        dw_o = _mm_tn(sv["merged"], dz1, name="wo_bwd_w", tka=D, tn=D)
        dpg, dprod, do_a, do_b, do_c = _merge_bwd(dmerged, sv["o_a"], sv["o_b"], sv["o_c"], sv["pg"],
                                                  w["wbr"], w["wbr_t"], name="merge_bwd")
        dw_a = _mm_tn(sv["o_a"], dprod, name="branch_a_bwd_w", tka=256, tn=D, b_off=0)
        dw_b = _mm_tn(sv["o_b"], dprod, name="branch_b_bwd_w", tka=512, tn=D, b_off=1)
        dw_c = _mm_tn(sv["o_c"], dprod, name="branch_c_bwd_w", tka=256, tn=D, b_off=2)
        dpq, dsink = _attn_bwd(sv["pq"], cos_t, sin_t, w["sinks"], do_b, sv["o_b"], sv["lse"], name="attn_bwd")
        dpp, pooled, dmixed, dscale, dconv = _poolconv_bwd(sv["pp"], do_a, do_c, w["wbd"], w["wbd_t"], w["scale"],
                                                           w["conv"], name="poolconv_bwd")
        dwbd = _mm_tn(pooled, dmixed, name="pool_bwd_w", tka=256, tn=256)
        dx = _mm(dpg, w["wg_t"], out_dtype=F32, name="proj_gate_bwd_x", tk=1024, add=dz1, add_scale=alpha)
        dx = _mm(dpp, w["wp_t"], out_dtype=F32, name="proj_poolconv_bwd_x", add=dx)
        dx = _mm(dpq, w["wq_t"], out_dtype=F32, name="proj_qkv_bwd_x", add=dx)
        dw_g = _mm_tn(sv["x0"], dpg, name="proj_gate_bwd_w", tka=D, tn=1024, nb=3)
        dw_p = _mm_tn(sv["x0"], dpp, name="proj_poolconv_bwd_w", tka=D, tn=1024)
        dw_q = _mm_tn(sv["x0"], dpq, name="proj_qkv_bwd_w", tka=D, tn=768)
        dw_in = jnp.concatenate([dw_p[:, 0:256], dw_q, dw_p[:, 256:1024], dw_g], axis=1)
        dw_pool = jnp.stack([dwbd[64 * g:64 * (g + 1), 64 * g:64 * (g + 1)] for g in range(4)])
        gw[l] = dict(w_in=dw_in, a=dw_a, b=dw_b, c=dw_c, w_o=dw_o, w_up=_deinterleave_cols(dw_up, NJ), w_down=dw_dn,
                     w_pool=dw_pool, scale=dscale, sinks=dsink[:, 0], conv=dconv, fcw=_deinterleave_cols(dfcw, NJ),
                     g1=dg1, b1=db1, g2=dg2, b2=db2)
        dy = dx
    grad_x = dy.reshape(1, S, D)

    def stack(k):
        return jnp.stack([gw[l][k] for l in range(L)])

    col = lambda g, n: g.reshape(L, g.shape[1], N_DEV, n).transpose(2, 0, 1, 3)
    row = lambda g, n: g.reshape(L, N_DEV, n, g.shape[2]).transpose(1, 0, 2, 3)
    parts = [col(stack("w_in"), in_shard), col(stack("a"), D // N_DEV), col(stack("b"), D // N_DEV),
             col(stack("c"), D // N_DEV), row(stack("w_o"), D // N_DEV), col(stack("w_up"), F2 // N_DEV),
             row(stack("w_down"), F // N_DEV)]
    p_all = jnp.concatenate([p.reshape(N_DEV, -1, LANES).astype(BF16) for p in parts], axis=1)
    my_c = lax.axis_index("c").astype(jnp.int32).reshape(1)
    my_chip = (2 * lax.axis_index("x") + lax.axis_index("y")).astype(jnp.int32).reshape(1)
    from_sibling = _rs_sibling(p_all, "rs_sibling")
    pair_sum = _sum_sibling(p_all, from_sibling, my_c, "rs_sum_sibling")
    from_chips = _rs_chips(pair_sum, "rs_chips")
    g_big = _sum_chips(pair_sum, from_chips, my_chip, "rs_sum_chips")

    rep_vec = jnp.concatenate([
        stack("w_pool").reshape(-1), stack("scale").reshape(-1), stack("g1").reshape(-1), stack("b1").reshape(-1),
        stack("g2").reshape(-1), stack("b2").reshape(-1)])
    n_rep_full = -(-rep_vec.shape[0] // LANES)
    sinks_row = jnp.pad(stack("sinks").reshape(-1), (0, LANES - 8 * L))
    rep_vec = jnp.concatenate([_pad_rows(rep_vec, n_rep_full).reshape(-1), sinks_row, sq_lanes.reshape(-1)])
    loss_row = n_rep_full + 1
    n_rep = -(-(loss_row + 1) // 8) * 8
    rep_rows = _pad_rows(rep_vec, n_rep)
    dconv_by_dev = stack("conv").reshape(L, 3, N_DEV, -1).transpose(2, 0, 1, 3).reshape(N_DEV, -1)
    dfcw_by_dev = stack("fcw").reshape(L, 3, N_DEV, -1).transpose(2, 0, 1, 3).reshape(N_DEV, -1)
    n_mine = -(-(small_rows) // 8) * 8
    by_dev = jnp.concatenate([dconv_by_dev, dfcw_by_dev], axis=1)
    by_dev = jnp.pad(by_dev, ((0, 0), (0, n_mine * LANES - by_dev.shape[1]))).reshape(N_DEV * n_mine, LANES)
    small_g = _all_gather(jnp.concatenate([rep_rows, by_dev], axis=0), "ag_small_grads")
    rep_sum, mine_sum, loss11 = _small_reduce(small_g, n_rep, n_mine, 1.0 / D, loss_row, "small_reduce")
    loss = loss11[0, 0]

    names_big = ["w_in", "w_branch_a", "w_branch_b", "w_branch_c", "w_o", "w_up", "w_down"]
    ms_big = [m_w_in, m_w_branch_a, m_w_branch_b, m_w_branch_c, m_w_o, m_w_up, m_w_down]
    vs_big = [v_w_in, v_w_branch_a, v_w_branch_b, v_w_branch_c, v_w_o, v_w_up, v_w_down]
    out = {}
    for k, name in enumerate(names_big):
        wk = big[k]
        g2d = g_big[offs[k]:offs[k + 1]]
        c2 = wk.shape[-1]
        as2d = lambda a: a.reshape(-1, c2)
        g_nat = g2d.reshape(wk.shape)
        d, mo, vo = _adamw(as2d(wk), as2d(g_nat), as2d(ms_big[k]), as2d(vs_big[k]), name="adamw_" + name)
        out[name] = (g_nat, d.reshape(wk.shape), mo.reshape(wk.shape), vo.reshape(wk.shape))

    def rep_pack(wp_, sc_, g1_, b1_, g2_, b2_, sk_):
        v = jnp.concatenate([wp_.reshape(-1), sc_.reshape(-1), g1_.reshape(-1), b1_.reshape(-1), g2_.reshape(-1),
                             b2_.reshape(-1)])
        return _pad_rows(jnp.concatenate([_pad_rows(v, n_rep_full).reshape(-1), sk_.reshape(-1)]), n_rep)

    def mine_pack(cw_, fw_):
        return _pad_rows(jnp.concatenate([cw_.reshape(-1), fw_.reshape(-1)]), n_mine)

    w_rep = rep_pack(w_pool, pool_scale, ln1_g, ln1_b, ln2_g, ln2_b, attn_sinks)
    m_rep = rep_pack(m_w_pool, m_pool_scale, m_ln1_g, m_ln1_b, m_ln2_g, m_ln2_b, m_attn_sinks)
    v_rep = rep_pack(v_w_pool, v_pool_scale, v_ln1_g, v_ln1_b, v_ln2_g, v_ln2_b, v_attn_sinks)
    g_rep = jnp.concatenate([rep_sum[:loss_row], jnp.zeros((n_rep - loss_row, LANES), F32)], axis=0)
    rep_res = (g_rep,) + tuple(_adamw(w_rep, g_rep, m_rep, v_rep, name="adamw_replicated"))
    w_mine = mine_pack(conv_w, ffn_conv_w)
    mine_res = (mine_sum,) + tuple(_adamw(w_mine, mine_sum, mine_pack(m_conv_w, m_ffn_conv_w),
                                          mine_pack(v_conv_w, v_ffn_conv_w), name="adamw_conv"))

    def rep_unpack(buf):
        flat = buf.reshape(-1)
        res, o = {}, 0
        for nm, ref in (("w_pool", w_pool), ("pool_scale", pool_scale), ("ln1_g", ln1_g), ("ln1_b", ln1_b),
                        ("ln2_g", ln2_g), ("ln2_b", ln2_b)):
            res[nm] = flat[o:o + ref.size].reshape(ref.shape)
            o += ref.size
        o = n_rep_full * LANES
        res["attn_sinks"] = flat[o:o + attn_sinks.size].reshape(attn_sinks.shape)
        return res

    def mine_unpack(buf):
        flat = buf.reshape(-1)
        return {"conv_w": flat[:n_cw].reshape(conv_w.shape),
                "ffn_conv_w": flat[n_cw:n_cw + n_fw].reshape(ffn_conv_w.shape)}

    order = ["w_in", "w_pool", "pool_scale", "attn_sinks", "conv_w", "w_branch_a", "w_branch_b", "w_branch_c", "w_o",
             "ln1_g", "ln1_b", "w_up", "ffn_conv_w", "w_down", "ln2_g", "ln2_b"]
    results = [loss, grad_x]
    for kind in range(4):
        rep_k, mine_k = rep_unpack(rep_res[kind]), mine_unpack(mine_res[kind])
        for nm in order:
            if nm in out:
                results.append(out[nm][kind])
            elif nm in rep_k:
                results.append(rep_k[nm])
            else:
                results.append(mine_k[nm])
    return tuple(results)
```

```python
import functools

import jax
import jax.numpy as jnp
from jax import lax
from jax.experimental import pallas as pl
from jax.experimental.pallas import tpu as pltpu

F32 = jnp.float32
BF16 = jnp.bfloat16

HEAD_DIM = 64
N_Q_HEADS = 8
GROUP = 4
WINDOW = 128
ROT_DIM = 16
ROPE_THETA = 500000.0
POOL_WINDOWS = (2, 4, 8, 16)
LN_EPS = 1e-5
MASK_VALUE = -1e30
ADAM_LR, ADAM_B1, ADAM_B2, ADAM_EPS, ADAM_WD, ADAM_STEP = 0.001, 0.9, 0.999, 1e-08, 0.01, 10

N_DEV = 8
LANES = 1024
HALO = 16
MESH = pl.DeviceIdType.MESH
VMEM_LIMIT = 56 * 1024 * 1024


def _div_tile(n, want, mult=8):
    for t in range(min(want, n) // mult * mult, 0, -mult):
        if n % t == 0:
            return t
    return n


def _cp(*sem):
    return pltpu.CompilerParams(dimension_semantics=sem, vmem_limit_bytes=VMEM_LIMIT)


def _coords():
    return lax.axis_index("x"), lax.axis_index("y"), lax.axis_index("c")


def _all_gather(xs, name):
    R, C = xs.shape

    def body(x_ref, out_ref, send_sems, recv_sems, local_sem):
        x, y, c = _coords()
        me, sibling = (x, y, c), (x, y, 1 - c)
        chips = [(1 - x, y), (x, 1 - y), (1 - x, 1 - y)]

        def slot(px, py, pc):
            return out_ref.at[4 * px + 2 * py + pc]

        def copy(k, block, to, src=None):
            return pltpu.make_async_remote_copy(
                src_ref=slot(*block) if src is None else src, dst_ref=slot(*block),
                send_sem=send_sems.at[k], recv_sem=recv_sems.at[k], device_id=to, device_id_type=MESH)

        mine = pltpu.make_async_copy(x_ref, slot(*me), local_sem)
        mine.start()
        first = [copy(0, me, sibling, src=x_ref)]
        first += [copy(1 + j, me, (*chip, c), src=x_ref) for j, chip in enumerate(chips)]
        for cp in first:
            cp.start()
        passed = [copy(4 + j, (*chip, c), sibling) for j, chip in enumerate(chips)]
        for j, chip in enumerate(chips):
            copy(1 + j, (*chip, c), me).wait_recv()
            passed[j].start()
        copy(0, sibling, me).wait_recv()
        for j, chip in enumerate(chips):
            copy(4 + j, (*chip, 1 - c), me).wait_recv()
        for cp in first + passed:
            cp.wait_send()
        mine.wait()

    return pl.pallas_call(
        body, name=name,
        out_shape=jax.ShapeDtypeStruct((N_DEV, R, C), xs.dtype),
        in_specs=[pl.BlockSpec(memory_space=pl.ANY)],
        out_specs=pl.BlockSpec(memory_space=pl.ANY),
        scratch_shapes=[pltpu.SemaphoreType.DMA((7,)), pltpu.SemaphoreType.DMA((7,)), pltpu.SemaphoreType.DMA(())],
    )(xs)


def _rs_sibling(p, name):
    _, R, C = p.shape

    def body(p_ref, out_ref, send_sems, recv_sems):
        x, y, c = _coords()
        copies = []
        for j in range(4):
            cx, cy = j // 2, j % 2
            copies.append(pltpu.make_async_remote_copy(
                src_ref=p_ref.at[4 * cx + 2 * cy + (1 - c)], dst_ref=out_ref.at[j],
                send_sem=send_sems.at[j], recv_sem=recv_sems.at[j], device_id=(x, y, 1 - c), device_id_type=MESH))
        for cp in copies:
            cp.start()
        for cp in copies:
            cp.wait_recv()
        for cp in copies:
            cp.wait_send()

    return pl.pallas_call(
        body, name=name,
        out_shape=jax.ShapeDtypeStruct((4, R, C), p.dtype),
        in_specs=[pl.BlockSpec(memory_space=pl.ANY)],
        out_specs=pl.BlockSpec(memory_space=pl.ANY),
        scratch_shapes=[pltpu.SemaphoreType.DMA((4,)), pltpu.SemaphoreType.DMA((4,))],
    )(p)


def _rs_chips(q, name):
    _, R, C = q.shape

    def body(q_ref, out_ref, send_sems, recv_sems):
        x, y, c = _coords()
        chips = [(1 - x, y), (x, 1 - y), (1 - x, 1 - y)]
        copies = []
        for k, (cx, cy) in enumerate(chips):
            copies.append(pltpu.make_async_remote_copy(
                src_ref=q_ref.at[2 * cx + cy], dst_ref=out_ref.at[k],
                send_sem=send_sems.at[k], recv_sem=recv_sems.at[k], device_id=(cx, cy, c), device_id_type=MESH))
        for cp in copies:
            cp.start()
        for cp in copies:
            cp.wait_recv()
        for cp in copies:
            cp.wait_send()

    return pl.pallas_call(
        body, name=name,
        out_shape=jax.ShapeDtypeStruct((3, R, C), q.dtype),
        in_specs=[pl.BlockSpec(memory_space=pl.ANY)],
        out_specs=pl.BlockSpec(memory_space=pl.ANY),
        scratch_shapes=[pltpu.SemaphoreType.DMA((3,)), pltpu.SemaphoreType.DMA((3,))],
    )(q)


def _sum_sibling(p, recv, my_c, name, tr=512):
    _, R, C = p.shape
    tr = _div_tile(R, tr, 16)

    def body(c_ref, p_ref, r_ref, o_ref):
        o_ref[...] = (p_ref[...].astype(F32) + r_ref[...].astype(F32)).astype(o_ref.dtype)

    grid_spec = pltpu.PrefetchScalarGridSpec(
        num_scalar_prefetch=1, grid=(4, R // tr),
        in_specs=[pl.BlockSpec((1, tr, C), lambda j, r, c_ref: (4 * (j // 2) + 2 * (j % 2) + c_ref[0], r, 0)),
                  pl.BlockSpec((1, tr, C), lambda j, r, c_ref: (j, r, 0))],
        out_specs=pl.BlockSpec((1, tr, C), lambda j, r, c_ref: (j, r, 0)))
    return pl.pallas_call(body, name=name, grid_spec=grid_spec,
                          out_shape=jax.ShapeDtypeStruct((4, R, C), p.dtype),
                          compiler_params=_cp("parallel", "parallel"))(my_c, p, recv)


def _sum_chips(q, recv, my_chip, name, tr=512):
    _, R, C = q.shape
    tr = _div_tile(R, tr, 16)

    def body(i_ref, q_ref, r_ref, o_ref):
        acc = q_ref[0].astype(F32)
        for k in range(3):
            acc = acc + r_ref[k].astype(F32)
        o_ref[...] = acc

    grid_spec = pltpu.PrefetchScalarGridSpec(
        num_scalar_prefetch=1, grid=(R // tr,),
        in_specs=[pl.BlockSpec((1, tr, C), lambda r, i_ref: (i_ref[0], r, 0)),
                  pl.BlockSpec((3, tr, C), lambda r, i_ref: (0, r, 0))],
        out_specs=pl.BlockSpec((tr, C), lambda r, i_ref: (r, 0)))
    return pl.pallas_call(body, name=name, grid_spec=grid_spec,
                          out_shape=jax.ShapeDtypeStruct((R, C), F32),
                          compiler_params=_cp("parallel"))(my_chip, q, recv)


def _small_reduce(g, n_rep, n_mine, inv_d, loss_row, name):
    _, R, C = g.shape

    def body(g_ref, rep_ref, mine_ref, loss_ref):
        x, y, c = _coords()
        start = pl.multiple_of(n_rep + (4 * x + 2 * y + c) * n_mine, 8)
        rep = g_ref[0, 0:n_rep, :]
        mine = g_ref[0, pl.ds(start, n_mine), :]
        sq = g_ref[0, loss_row:loss_row + 1, :]
        for d in range(1, N_DEV):
            rep = rep + g_ref[d, 0:n_rep, :]
            mine = mine + g_ref[d, pl.ds(start, n_mine), :]
            sq = sq + g_ref[d, loss_row:loss_row + 1, :]
        rep_ref[...] = rep
        mine_ref[...] = mine
        loss_ref[...] = (0.5 * inv_d) * jnp.sum(sq, axis=1, keepdims=True)

    return pl.pallas_call(
        body, name=name,
        out_shape=(jax.ShapeDtypeStruct((n_rep, C), F32), jax.ShapeDtypeStruct((n_mine, C), F32),
                   jax.ShapeDtypeStruct((1, 1), F32)),
        compiler_params=pltpu.CompilerParams(vmem_limit_bytes=VMEM_LIMIT),
    )(g)


def _mm(a, b, *, out_dtype, name, tm=512, tn=None, tk=None, add=None, add_scale=1.0):
    M, K = a.shape
    N = b.shape[1]
    tm = min(tm, M)
    tn = N if tn is None else tn
    tk = K if tk is None else tk
    nk = K // tk
    has_add = add is not None

    def body(*refs):
        a_ref, b_ref = refs[0], refs[1]
        add_ref = refs[2] if has_add else None
        o_ref = refs[3] if has_add else refs[2]
        part = jnp.dot(a_ref[...].astype(BF16), b_ref[...].astype(BF16), preferred_element_type=F32)

        def finish(r):
            if has_add:
                r = r + add_scale * add_ref[...].astype(F32)
            o_ref[...] = r.astype(out_dtype)

        if nk == 1:
            finish(part)
        else:
            acc_ref = refs[-1]
            k = pl.program_id(2)

            @pl.when(k == 0)
            def _():
                acc_ref[...] = part

            @pl.when(k > 0)
            def _():
                acc_ref[...] += part

            @pl.when(k == nk - 1)
            def _():
                finish(acc_ref[...])

    in_specs = [pl.BlockSpec((tm, tk), lambda i, j, k: (i, k)), pl.BlockSpec((tk, tn), lambda i, j, k: (k, j))]
    args = [a, b]
    if has_add:
        in_specs.append(pl.BlockSpec((tm, tn), lambda i, j, k: (i, j)))
        args.append(add)
    return pl.pallas_call(
        body, name=name, grid=(M // tm, N // tn, nk),
        in_specs=in_specs, out_specs=pl.BlockSpec((tm, tn), lambda i, j, k: (i, j)),
        out_shape=jax.ShapeDtypeStruct((M, N), out_dtype),
        scratch_shapes=[pltpu.VMEM((tm, tn), F32)] if nk > 1 else [],
        compiler_params=_cp("parallel", "parallel", "arbitrary"),
    )(*args)


def _mm_ln(a, b, resid, gamma, beta, *, alpha, name, tm=512, tk=None):
    M, K = a.shape
    D = b.shape[1]
    tm = min(tm, M)
    tk = K if tk is None else tk
    nk = K // tk

    def body(a_ref, b_ref, r_ref, g_ref, be_ref, y_ref, xh_ref, rs_ref, *scratch):
        part = jnp.dot(a_ref[...].astype(BF16), b_ref[...].astype(BF16), preferred_element_type=F32)

        def finish(acc):
            z = alpha * r_ref[...] + acc
            mu = jnp.mean(z, axis=-1, keepdims=True)
            zc = z - mu
            var = jnp.mean(zc * zc, axis=-1, keepdims=True)
            rstd = lax.rsqrt(var + LN_EPS)
            xhat = zc * rstd
            y_ref[...] = xhat * g_ref[...] + be_ref[...]
            xh_ref[...] = xhat.astype(BF16)
            rs_ref[...] = rstd

        if nk == 1:
            finish(part)
        else:
            acc_ref = scratch[0]
            k = pl.program_id(1)

            @pl.when(k == 0)
            def _():
                acc_ref[...] = part

            @pl.when(k > 0)
            def _():
                acc_ref[...] += part

            @pl.when(k == nk - 1)
            def _():
                finish(acc_ref[...])

    row = lambda i, k: (i, 0)
    vec = lambda i, k: (0, 0)
    return pl.pallas_call(
        body, name=name, grid=(M // tm, nk),
        in_specs=[pl.BlockSpec((tm, tk), lambda i, k: (i, k)), pl.BlockSpec((tk, D), lambda i, k: (k, 0)),
                  pl.BlockSpec((tm, D), row), pl.BlockSpec((1, D), vec), pl.BlockSpec((1, D), vec)],
        out_specs=[pl.BlockSpec((tm, D), row), pl.BlockSpec((tm, D), row), pl.BlockSpec((tm, 1), row)],
        out_shape=(jax.ShapeDtypeStruct((M, D), F32), jax.ShapeDtypeStruct((M, D), BF16),
                   jax.ShapeDtypeStruct((M, 1), F32)),
        scratch_shapes=[pltpu.VMEM((tm, D), F32)] if nk > 1 else [],
        compiler_params=_cp("parallel", "arbitrary"),
    )(a, b, resid, gamma, beta)


def _mm_tn(a, b, *, name, tka, tn, a_off=0, na=1, b_off=0, nb=1, ts=512):
    S = a.shape[0]
    ts = min(ts, S)

    def body(a_ref, b_ref, o_ref):
        s = pl.program_id(2)
        part = lax.dot_general(a_ref[...].astype(BF16), b_ref[...].astype(BF16),
                               (((0,), (0,)), ((), ())), preferred_element_type=F32)

        @pl.when(s == 0)
        def _():
            o_ref[...] = part

        @pl.when(s > 0)
        def _():
            o_ref[...] += part

    return pl.pallas_call(
        body, name=name, grid=(na, nb, S // ts),
        in_specs=[pl.BlockSpec((ts, tka), lambda i, j, s: (s, a_off + i)),
                  pl.BlockSpec((ts, tn), lambda i, j, s: (s, b_off + j))],
        out_specs=pl.BlockSpec((tka, tn), lambda i, j, s: (i, j)),
        out_shape=jax.ShapeDtypeStruct((na * tka, nb * tn), F32),
        compiler_params=_cp("parallel", "parallel", "arbitrary"),
    )(a, b)


def _rope_tables(pos, inv_lane, sign_lane, name, ts=512):
    S = pos.shape[0]
    ts = min(ts, S)

    def body(p_ref, inv_ref, sg_ref, cos_ref, sin_ref):
        ang = p_ref[...].astype(F32) * inv_ref[...]
        cos_ref[...] = jnp.cos(ang)
        sin_ref[...] = jnp.sin(ang) * sg_ref[...]

    return pl.pallas_call(
        body, name=name, grid=(S // ts,),
        in_specs=[pl.BlockSpec((ts, 1), lambda i: (i, 0)), pl.BlockSpec((1, 128), lambda i: (0, 0)),
                  pl.BlockSpec((1, 128), lambda i: (0, 0))],
        out_specs=[pl.BlockSpec((ts, 128), lambda i: (i, 0))] * 2,
        out_shape=(jax.ShapeDtypeStruct((S, 128), F32),) * 2,
        compiler_params=_cp("parallel"),
    )(pos, inv_lane, sign_lane)


def _rope_swap(t):
    lane = lax.broadcasted_iota(jnp.int32, (1, 128), 1)
    lo = (lane % HEAD_DIM) < (ROT_DIM // 2)
    return jnp.where(lo, pltpu.roll(t, 128 - ROT_DIM // 2, 1), pltpu.roll(t, ROT_DIM // 2, 1))


def _rope_fwd(t, cos, sin):
    return t * cos + _rope_swap(t) * sin


def _rope_bwd(d, cos, sin):
    lane = lax.broadcasted_iota(jnp.int32, (1, 128), 1)
    return d * cos + jnp.where((lane % HEAD_DIM) < ROT_DIM, _rope_swap(d * sin), 0.0)


def _tile_heads(t):
    lane = lax.broadcasted_iota(jnp.int32, (1, 128), 1)
    r = pltpu.roll(t, 64, 1)
    h0 = jnp.where(lane < 64, t, r)
    h1 = jnp.where(lane < 64, r, t)
    return jnp.concatenate([h0, h0], axis=1), jnp.concatenate([h1, h1], axis=1)


def _fold_heads(d0, d1):
    lane = lax.broadcasted_iota(jnp.int32, (1, 128), 1)

    def fold(d):
        s = d[:, 0:128] + d[:, 128:256]
        return s + pltpu.roll(s, 64, 1)

    return jnp.where(lane < 64, fold(d0), fold(d1))


def _band4(n_keys):
    row = lax.broadcasted_iota(jnp.int32, (GROUP * WINDOW, n_keys), 0) % WINDOW
    col = lax.broadcasted_iota(jnp.int32, (GROUP * WINDOW, n_keys), 1)
    return (col > row) & (col <= row + WINDOW), col


def _head_masks():
    lane = lax.broadcasted_iota(jnp.int32, (1, GROUP * HEAD_DIM), 1)
    return [(lane // HEAD_DIM) == hl for hl in range(GROUP)]


def _stack_heads(t):
    zero = jnp.zeros_like(t)
    return jnp.concatenate([jnp.where(hm, t, zero) for hm in _head_masks()], axis=0)


def _unstack_heads(t4):
    out = None
    for hl, hm in enumerate(_head_masks()):
        part = jnp.where(hm, t4[hl * WINDOW:(hl + 1) * WINDOW], 0.0)
        out = part if out is None else out + part
    return out


def _sink_block(sink_ref, g):
    return jnp.concatenate([jnp.broadcast_to(sink_ref[g * GROUP + hl:g * GROUP + hl + 1, 0:1], (WINDOW, 256))
                            for hl in range(GROUP)], axis=0)


def _sink_column(sink_ref, g):
    return jnp.concatenate([jnp.broadcast_to(sink_ref[g * GROUP + hl:g * GROUP + hl + 1, 0:1], (WINDOW, 1))
                            for hl in range(GROUP)], axis=0)


def _attn_fwd(pq, cos_t, sin_t, sinks_b, *, name, ts=256):
    S = pq.shape[0]
    ts = min(ts, S)
    nq = ts // WINDOW
    scale = HEAD_DIM ** -0.5

    def body(cur_ref, prev_ref, cosc_ref, sinc_ref, cosp_ref, sinp_ref, sink_ref, o_ref, lse_ref):
        i = pl.program_id(0)
        cosc, sinc = cosc_ref[...], sinc_ref[...]
        q = cur_ref[:, 0:512].astype(F32)
        qr = jnp.concatenate(
            [_rope_fwd(q[:, j * 128:(j + 1) * 128], cosc, sinc) for j in range(4)], axis=1) * scale
        qr = qr.astype(BF16)
        kc = _rope_fwd(cur_ref[:, 512:640].astype(F32), cosc, sinc)
        kp = _rope_fwd(prev_ref[:, 0:128].astype(F32), cosp_ref[...], sinp_ref[...])
        k_all = jnp.concatenate([kp, kc], axis=0)
        v_all = jnp.concatenate([prev_ref[:, 128:256].astype(F32), cur_ref[:, 640:768].astype(F32)], axis=0)
        kt = [t.astype(BF16) for t in _tile_heads(k_all)]
        vt = [t.astype(BF16) for t in _tile_heads(v_all)]
        band, col = _band4(2 * WINDOW)
        ones = jnp.ones((2 * WINDOW, 256), BF16)
        key_t = lax.broadcasted_iota(jnp.int32, (2 * WINDOW, GROUP * WINDOW), 0)
        qry_t = lax.broadcasted_iota(jnp.int32, (2 * WINDOW, GROUP * WINDOW), 1) % WINDOW
        band_t = (key_t > qry_t) & (key_t <= qry_t + WINDOW)
        NT = (((1,), (1,)), ((), ()))
        for qb in range(nq):
            rows = slice(qb * WINDOW, (qb + 1) * WINDOW)
            keys = slice(qb * WINDOW, (qb + 2) * WINDOW)
            valid = band & ((col >= WINDOW) | (i * nq + qb > 0))
            valid_t = band_t & ((key_t >= WINDOW) | (i * nq + qb > 0))
            for g in range(2):
                qs = _stack_heads(qr[rows, g * 256:(g + 1) * 256])
                sink = _sink_block(sink_ref, g)
                s = lax.dot_general(qs, kt[g][keys], NT, preferred_element_type=F32)
                s_t = lax.dot_general(kt[g][keys], qs, NT, preferred_element_type=F32)
                m_t = jnp.max(jnp.where(valid_t, s_t, MASK_VALUE), axis=0, keepdims=True)
                m_rep = jnp.broadcast_to(m_t, (WINDOW, GROUP * WINDOW)).T
                m = jnp.maximum(jnp.concatenate([m_rep, m_rep], axis=1), sink)
                e = jnp.exp(jnp.where(valid, s, MASK_VALUE) - m).astype(BF16)
                l = jnp.dot(e, ones, preferred_element_type=F32) + jnp.exp(sink - m)
                pv = jnp.dot(e, vt[g][keys], preferred_element_type=F32)
                o_ref[rows, g * 256:(g + 1) * 256] = (_unstack_heads(pv) / _unstack_heads(l)).astype(BF16)
                lse4 = (m + jnp.log(l))[:, 0:1]
                for hl in range(GROUP):
                    h = g * GROUP + hl
                    lse_ref[rows, h:h + 1] = lse4[hl * WINDOW:(hl + 1) * WINDOW]

    hb = ts // WINDOW
    cur = lambda i: (i, 0)
    prev = lambda i: (jnp.maximum(i * hb - 1, 0), 0)
    return pl.pallas_call(
        body, name=name, grid=(S // ts,),
        in_specs=[pl.BlockSpec((ts, 768), cur),
                  pl.BlockSpec((WINDOW, 256), lambda i: (jnp.maximum(i * hb - 1, 0), 2)),
                  pl.BlockSpec((ts, 128), cur), pl.BlockSpec((ts, 128), cur),
                  pl.BlockSpec((WINDOW, 128), prev), pl.BlockSpec((WINDOW, 128), prev),
                  pl.BlockSpec((8, 128), lambda i: (0, 0))],
        out_specs=[pl.BlockSpec((ts, 512), cur), pl.BlockSpec((ts, 8), cur)],
        out_shape=(jax.ShapeDtypeStruct((S, 512), BF16), jax.ShapeDtypeStruct((S, 8), F32)),
        compiler_params=_cp("parallel"),
    )(pq, pq, cos_t, sin_t, cos_t, sin_t, sinks_b)


def _attn_bwd(pq, cos_t, sin_t, sinks_b, do, o, lse, *, name, ts=256):
    S = pq.shape[0]
    ts = min(ts, S)
    nq = ts // WINDOW
    nt = S // ts
    scale = HEAD_DIM ** -0.5
    NT = (((1,), (1,)), ((), ()))
    TN = (((0,), (0,)), ((), ()))

    def body(cur_ref, prev_ref, nxt_ref, cosc_ref, sinc_ref, cosp_ref, sinp_ref, cosn_ref, sinn_ref, sink_ref,
             doc_ref, don_ref, oc_ref, on_ref, lsec_ref, lsen_ref, dpq_ref, dsink_ref):
        i = pl.program_id(0)
        last = i == nt - 1
        cosc, sinc = cosc_ref[...], sinc_ref[...]
        cose = jnp.concatenate([cosc, cosn_ref[...]], axis=0)
        sine = jnp.concatenate([sinc, sinn_ref[...]], axis=0)
        q = jnp.concatenate([cur_ref[:, 0:512], nxt_ref[:, 0:512]], axis=0).astype(F32)
        qr = jnp.concatenate(
            [_rope_fwd(q[:, j * 128:(j + 1) * 128], cose, sine) for j in range(4)], axis=1) * scale
        qr = qr.astype(BF16)
        kc = _rope_fwd(cur_ref[:, 512:640].astype(F32), cosc, sinc)
        kp = _rope_fwd(prev_ref[:, 0:128].astype(F32), cosp_ref[...], sinp_ref[...])
        k_all = jnp.concatenate([kp, kc], axis=0)
        v_all = jnp.concatenate([prev_ref[:, 128:256].astype(F32), cur_ref[:, 640:768].astype(F32)], axis=0)
        kt = [t.astype(BF16) for t in _tile_heads(k_all)]
        vt = [t.astype(BF16) for t in _tile_heads(v_all)]
        don = jnp.where(last, jnp.zeros_like(don_ref[...]), don_ref[...])
        do_e = jnp.concatenate([doc_ref[...], don], axis=0)
        o_e = jnp.concatenate([oc_ref[...], on_ref[...]], axis=0)
        band2, col2 = _band4(2 * WINDOW)
        band1, _ = _band4(WINDOW)
        ones = jnp.ones((256, 256), BF16)

        @pl.when(i == 0)
        def _():
            dsink_ref[...] = jnp.zeros_like(dsink_ref)

        dk_acc = [[None] * (nq + 1) for _ in range(2)]
        dv_acc = [[None] * (nq + 1) for _ in range(2)]

        def add(acc, g, e, val):
            acc[g][e] = val if acc[g][e] is None else acc[g][e] + val

        for qb in range(nq + 1):
            halo = qb == nq
            rows = slice(qb * WINDOW, (qb + 1) * WINDOW)
            if halo:
                keys = slice(qb * WINDOW, (qb + 1) * WINDOW)
                valid = band1 & jnp.logical_not(last)
            else:
                keys = slice(qb * WINDOW, (qb + 2) * WINDOW)
                valid = band2 & ((col2 >= WINDOW) | (i * nq + qb > 0))
            dq_parts = []
            for g in range(2):
                qs = _stack_heads(qr[rows, g * 256:(g + 1) * 256])
                dos = _stack_heads(do_e[rows, g * 256:(g + 1) * 256])
                o_g = o_e[rows, g * 256:(g + 1) * 256].astype(F32)
                kt_b, vt_b = kt[g][keys], vt[g][keys]
                lse_src = lsen_ref if halo else lsec_ref
                lse_rows = slice(0, WINDOW) if halo else rows
                big_l = jnp.concatenate([lse_src[lse_rows, g * GROUP + hl:g * GROUP + hl + 1] for hl in range(GROUP)],
                                        axis=0)
                delta = jnp.dot((dos.astype(F32) * jnp.concatenate([o_g] * GROUP, axis=0)).astype(BF16), ones,
                                preferred_element_type=F32)
                s = lax.dot_general(qs, kt_b, NT, preferred_element_type=F32)
                p = jnp.exp(jnp.where(valid, s, MASK_VALUE) - big_l)
                dp = lax.dot_general(dos, vt_b, NT, preferred_element_type=F32)
                ds = (p * (dp - delta[:, 0:p.shape[1]])).astype(BF16)
                dk_g = lax.dot_general(ds, qs, TN, preferred_element_type=F32)
                dv_g = lax.dot_general(p.astype(BF16), dos, TN, preferred_element_type=F32)
                if not halo:
                    dq_parts.append(_unstack_heads(jnp.dot(ds, kt_b, preferred_element_type=F32)))
                    dsink4 = jnp.exp(_sink_column(sink_ref, g) - big_l) * delta[:, 0:1]
                    for hl in range(GROUP):
                        h = g * GROUP + hl
                        dsink_h = -jnp.sum(dsink4[hl * WINDOW:(hl + 1) * WINDOW], axis=0, keepdims=True)
                        dsink_ref[h:h + 1, :] += jnp.broadcast_to(dsink_h, (1, 128))
                add(dk_acc, g, qb, dk_g[0:WINDOW])
                add(dv_acc, g, qb, dv_g[0:WINDOW])
                if not halo:
                    add(dk_acc, g, qb + 1, dk_g[WINDOW:2 * WINDOW])
                    add(dv_acc, g, qb + 1, dv_g[WINDOW:2 * WINDOW])
            if not halo:
                cs, sn = cosc[rows], sinc[rows]
                for g in range(2):
                    dq_g = dq_parts[g] * scale
                    for j in range(2):
                        c0 = g * 256 + j * 128
                        dpq_ref[rows, c0:c0 + 128] = _rope_bwd(dq_g[:, j * 128:(j + 1) * 128], cs, sn).astype(BF16)
        for e in range(1, nq + 1):
            rows = slice((e - 1) * WINDOW, e * WINDOW)
            dk = _fold_heads(dk_acc[0][e], dk_acc[1][e])
            dv = _fold_heads(dv_acc[0][e], dv_acc[1][e])
            dpq_ref[rows, 512:640] = _rope_bwd(dk, cosc[rows], sinc[rows]).astype(BF16)
            dpq_ref[rows, 640:768] = dv.astype(BF16)

    hb = ts // WINDOW
    nblk = S // WINDOW
    cur = lambda i: (i, 0)
    prev = lambda i: (jnp.maximum(i * hb - 1, 0), 0)
    nxt = lambda i: (jnp.minimum((i + 1) * hb, nblk - 1), 0)
    return pl.pallas_call(
        body, name=name, grid=(nt,),
        in_specs=[pl.BlockSpec((ts, 768), cur),
                  pl.BlockSpec((WINDOW, 256), lambda i: (jnp.maximum(i * hb - 1, 0), 2)),
                  pl.BlockSpec((WINDOW, 768), nxt),
                  pl.BlockSpec((ts, 128), cur), pl.BlockSpec((ts, 128), cur),
                  pl.BlockSpec((WINDOW, 128), prev), pl.BlockSpec((WINDOW, 128), prev),
                  pl.BlockSpec((WINDOW, 128), nxt), pl.BlockSpec((WINDOW, 128), nxt),
                  pl.BlockSpec((8, 128), lambda i: (0, 0)),
                  pl.BlockSpec((ts, 512), cur), pl.BlockSpec((WINDOW, 512), nxt),
                  pl.BlockSpec((ts, 512), cur), pl.BlockSpec((WINDOW, 512), nxt),
                  pl.BlockSpec((ts, 8), cur), pl.BlockSpec((WINDOW, 8), nxt)],
        out_specs=[pl.BlockSpec((ts, 768), cur), pl.BlockSpec((8, 128), lambda i: (0, 0))],
        out_shape=(jax.ShapeDtypeStruct((S, 768), BF16), jax.ShapeDtypeStruct((8, 128), F32)),
        compiler_params=_cp("arbitrary"),
    )(pq, pq, pq, cos_t, sin_t, cos_t, sin_t, cos_t, sin_t, sinks_b, do, do, o, o, lse, lse)


def _shift_dn(x, k):
    return pltpu.roll(x, k, 0)


def _shift_up(x, k):
    return pltpu.roll(x, x.shape[0] - k, 0)


def _pool_lane_select(vals):
    lane = lax.broadcasted_iota(jnp.int32, (1, 256), 1)
    out = vals[3]
    for g in (2, 1, 0):
        out = jnp.where(lane < 64 * (g + 1), vals[g], out)
    return out


def _pool_inv_count(t0, n):
    t = t0 + lax.broadcasted_iota(jnp.int32, (n, 256), 0)
    lane = lax.broadcasted_iota(jnp.int32, (n, 256), 1)
    w = jnp.where(lane < 64, 2, jnp.where(lane < 128, 4, jnp.where(lane < 192, 8, 16)))
    return 1.0 / jnp.minimum(t + 1, w).astype(F32)


def _pooled(u_ext, t0, n):
    s2 = u_ext + _shift_dn(u_ext, 1)
    s4 = s2 + _shift_dn(s2, 2)
    s8 = s4 + _shift_dn(s4, 4)
    s16 = s8 + _shift_dn(s8, 8)
    win = _pool_lane_select([s2, s4, s8, s16])[HALO:HALO + n]
    return win * _pool_inv_count(t0, n) - u_ext[HALO:HALO + n]


def _poolconv_fwd(pp, wbd, pool_scale, conv_w, *, name, ts=512):
    S = pp.shape[0]
    ts = min(ts, S)

    def body(cur_ref, prev_ref, wbd_ref, sc_ref, cw_ref, oa_ref, oc_ref):
        i = pl.program_id(0)
        prev = jnp.where(i > 0, prev_ref[...].astype(F32), 0.0)
        u_ext = jnp.concatenate([prev[:, 0:256], cur_ref[:, 0:256].astype(F32)], axis=0)
        pooled = _pooled(u_ext, i * ts, ts)
        mixed = jnp.dot(pooled.astype(BF16), wbd_ref[...], preferred_element_type=F32)
        oa_ref[...] = (mixed * sc_ref[...]).astype(BF16)
        v_ext = jnp.concatenate([prev[:, 256:512] * prev[:, 768:1024],
                                 cur_ref[:, 256:512].astype(F32) * cur_ref[:, 768:1024].astype(F32)], axis=0)
        cv = cw_ref[2:3, :] * v_ext + cw_ref[1:2, :] * _shift_dn(v_ext, 1) + cw_ref[0:1, :] * _shift_dn(v_ext, 2)
        oc_ref[...] = (cur_ref[:, 512:768].astype(F32) * cv[HALO:HALO + ts]).astype(BF16)

    hb = ts // HALO
    cur = lambda i: (i, 0)
    const = lambda i: (0, 0)
    return pl.pallas_call(
        body, name=name, grid=(S // ts,),
        in_specs=[pl.BlockSpec((ts, 1024), cur),
                  pl.BlockSpec((HALO, 1024), lambda i: (jnp.maximum(i * hb - 1, 0), 0)),
                  pl.BlockSpec((256, 256), const), pl.BlockSpec((1, 256), const), pl.BlockSpec((3, 256), const)],
        out_specs=[pl.BlockSpec((ts, 256), cur)] * 2,
        out_shape=(jax.ShapeDtypeStruct((S, 256), BF16),) * 2,
        compiler_params=_cp("parallel"),
    )(pp, pp, wbd, pool_scale, conv_w)


def _poolconv_bwd(pp, do_a, do_c, wbd, wbd_t, pool_scale, conv_w, *, name, ts=512):
    S = pp.shape[0]
    ts = min(ts, S)
    nt = S // ts
    n_e = ts + 2 * HALO

    def body(cur_ref, prev_ref, nxt_ref, dac_ref, dan_ref, dcc_ref, dcn_ref, wbd_ref, wbdt_ref, sc_ref, cw_ref,
             dpp_ref, pooled_ref, dmixed_ref, dsc_ref, dcw_ref):
        i = pl.program_id(0)

        @pl.when(i == 0)
        def _():
            dsc_ref[...] = jnp.zeros_like(dsc_ref)
            dcw_ref[...] = jnp.zeros_like(dcw_ref)

        prev = jnp.where(i > 0, prev_ref[...].astype(F32), 0.0)
        nxt = nxt_ref[...].astype(F32)
        cur = cur_ref[...].astype(F32)
        not_last = i < nt - 1
        da_n = jnp.where(not_last, dan_ref[...].astype(F32), 0.0)
        dc_n = jnp.where(not_last, dcn_ref[...].astype(F32), 0.0)
        zeros_h = jnp.zeros((HALO, 256), F32)
        sc = sc_ref[...]

        u_ext = jnp.concatenate([prev[:, 0:256], cur[:, 0:256]], axis=0)
        pooled = _pooled(u_ext, i * ts, ts)
        pooled_b = pooled.astype(BF16)
        pooled_ref[...] = pooled_b
        mixed = jnp.dot(pooled_b, wbd_ref[...], preferred_element_type=F32)
        da_c = dac_ref[...].astype(F32)
        dsc_ref[...] += jnp.sum(da_c * mixed, axis=0, keepdims=True)
        dmixed_e = jnp.concatenate([da_c, da_n], axis=0) * sc
        dmixed_ref[...] = dmixed_e[0:ts].astype(BF16)
        dpooled = jnp.dot(dmixed_e.astype(BF16), wbdt_ref[...], preferred_element_type=F32)
        qd = dpooled * _pool_inv_count(i * ts, ts + HALO)
        f2 = qd + _shift_up(qd, 1)
        f4 = f2 + _shift_up(f2, 2)
        f8 = f4 + _shift_up(f4, 4)
        f16 = f8 + _shift_up(f8, 8)
        du = (_pool_lane_select([f2, f4, f8, f16]) - dpooled)[0:ts]
        dpp_ref[:, 0:256] = du.astype(BF16)

        xc_e = jnp.concatenate([prev[:, 256:512], cur[:, 256:512], nxt[:, 256:512]], axis=0)
        gc_e = jnp.concatenate([prev[:, 768:1024], cur[:, 768:1024], nxt[:, 768:1024]], axis=0)
        gb_e = jnp.concatenate([zeros_h, cur[:, 512:768], nxt[:, 512:768]], axis=0)
        dc_e = jnp.concatenate([zeros_h, dcc_ref[...].astype(F32), dc_n], axis=0)
        v_e = xc_e * gc_e
        v1, v2 = _shift_dn(v_e, 1), _shift_dn(v_e, 2)
        w0, w1, w2 = cw_ref[0:1, :], cw_ref[1:2, :], cw_ref[2:3, :]
        cv = w2 * v_e + w1 * v1 + w0 * v2
        dcv = dc_e * gb_e
        dv = w2 * dcv + w1 * _shift_up(dcv, 1) + w0 * _shift_up(dcv, 2)
        tile = slice(HALO, HALO + ts)
        dpp_ref[:, 256:512] = (dv * gc_e)[tile].astype(BF16)
        dpp_ref[:, 512:768] = (dc_e * cv)[tile].astype(BF16)
        dpp_ref[:, 768:1024] = (dv * xc_e)[tile].astype(BF16)
        dcv_t = dcv[tile]
        dcw_ref[0:1, :] += jnp.sum(dcv_t * v2[tile], axis=0, keepdims=True)
        dcw_ref[1:2, :] += jnp.sum(dcv_t * v1[tile], axis=0, keepdims=True)
        dcw_ref[2:3, :] += jnp.sum(dcv_t * v_e[tile], axis=0, keepdims=True)

    hb = ts // HALO
    nblk = S // HALO
    cur = lambda i: (i, 0)
    const = lambda i: (0, 0)
    prev = lambda i: (jnp.maximum(i * hb - 1, 0), 0)
    nxt = lambda i: (jnp.minimum((i + 1) * hb, nblk - 1), 0)
    del n_e
    return pl.pallas_call(
        body, name=name, grid=(nt,),
        in_specs=[pl.BlockSpec((ts, 1024), cur), pl.BlockSpec((HALO, 1024), prev), pl.BlockSpec((HALO, 1024), nxt),
                  pl.BlockSpec((ts, 256), cur), pl.BlockSpec((HALO, 256), nxt),
                  pl.BlockSpec((ts, 256), cur), pl.BlockSpec((HALO, 256), nxt),
                  pl.BlockSpec((256, 256), const), pl.BlockSpec((256, 256), const),
                  pl.BlockSpec((1, 256), const), pl.BlockSpec((3, 256), const)],
        out_specs=[pl.BlockSpec((ts, 1024), cur), pl.BlockSpec((ts, 256), cur), pl.BlockSpec((ts, 256), cur),
                   pl.BlockSpec((1, 256), const), pl.BlockSpec((3, 256), const)],
        out_shape=(jax.ShapeDtypeStruct((S, 1024), BF16), jax.ShapeDtypeStruct((S, 256), BF16),
                   jax.ShapeDtypeStruct((S, 256), BF16), jax.ShapeDtypeStruct((1, 256), F32),
                   jax.ShapeDtypeStruct((3, 256), F32)),
        compiler_params=_cp("arbitrary"),
    )(pp, pp, pp, do_a, do_a, do_c, do_c, wbd, wbd_t, pool_scale, conv_w)


def _sigmoid(x):
    return 1.0 / (1.0 + jnp.exp(-x))


def _merge_fwd(o_a, o_b, o_c, glog, w_br, *, name, ts=512):
    S = o_a.shape[0]
    D = w_br.shape[1]
    ts = min(ts, S)

    def body(oa_ref, ob_ref, oc_ref, gl_ref, w_ref, m_ref):
        pa = jnp.dot(oa_ref[...], w_ref[0:256, :], preferred_element_type=F32)
        pb = jnp.dot(ob_ref[...], w_ref[256:768, :], preferred_element_type=F32)
        pc = jnp.dot(oc_ref[...], w_ref[768:1024, :], preferred_element_type=F32)
        m = _sigmoid(gl_ref[:, 0:D].astype(F32)) * pa
        m = m + _sigmoid(gl_ref[:, D:2 * D].astype(F32)) * pb
        m = m + _sigmoid(gl_ref[:, 2 * D:3 * D].astype(F32)) * pc
        m_ref[...] = m.astype(BF16)

    cur = lambda i: (i, 0)
    return pl.pallas_call(
        body, name=name, grid=(S // ts,),
        in_specs=[pl.BlockSpec((ts, 256), cur), pl.BlockSpec((ts, 512), cur), pl.BlockSpec((ts, 256), cur),
                  pl.BlockSpec((ts, 3 * D), cur), pl.BlockSpec((1024, D), lambda i: (0, 0))],
        out_specs=pl.BlockSpec((ts, D), cur),
        out_shape=jax.ShapeDtypeStruct((S, D), BF16),
        compiler_params=_cp("parallel"),
    )(o_a, o_b, o_c, glog, w_br)


def _merge_bwd(dm, o_a, o_b, o_c, glog, w_br, w_br_t, *, name, ts=256):
    S = o_a.shape[0]
    D = w_br.shape[1]
    ts = min(ts, S)

    def body(dm_ref, oa_ref, ob_ref, oc_ref, gl_ref, w_ref, wt_ref, dgl_ref, dp_ref, doa_ref, dob_ref, doc_ref):
        dmv = dm_ref[...].astype(F32)
        branches = ((oa_ref, 0, 256, doa_ref), (ob_ref, 256, 768, dob_ref), (oc_ref, 768, 1024, doc_ref))
        for b, (o_ref, r0, r1, do_ref) in enumerate(branches):
            prod = jnp.dot(o_ref[...], w_ref[r0:r1, :], preferred_element_type=F32)
            gate = _sigmoid(gl_ref[:, b * D:(b + 1) * D].astype(F32))
            dgl_ref[:, b * D:(b + 1) * D] = (dmv * prod * gate * (1.0 - gate)).astype(BF16)
            dprod = (dmv * gate).astype(BF16)
            dp_ref[:, b * D:(b + 1) * D] = dprod
            do_ref[...] = jnp.dot(dprod, wt_ref[:, r0:r1], preferred_element_type=F32).astype(BF16)

    cur = lambda i: (i, 0)
    const = lambda i: (0, 0)
    return pl.pallas_call(
        body, name=name, grid=(S // ts,),
        in_specs=[pl.BlockSpec((ts, D), cur), pl.BlockSpec((ts, 256), cur), pl.BlockSpec((ts, 512), cur),
                  pl.BlockSpec((ts, 256), cur), pl.BlockSpec((ts, 3 * D), cur),
                  pl.BlockSpec((1024, D), const), pl.BlockSpec((D, 1024), const)],
        out_specs=[pl.BlockSpec((ts, 3 * D), cur), pl.BlockSpec((ts, 3 * D), cur), pl.BlockSpec((ts, 256), cur),
                   pl.BlockSpec((ts, 512), cur), pl.BlockSpec((ts, 256), cur)],
        out_shape=(jax.ShapeDtypeStruct((S, 3 * D), BF16), jax.ShapeDtypeStruct((S, 3 * D), BF16),
                   jax.ShapeDtypeStruct((S, 256), BF16), jax.ShapeDtypeStruct((S, 512), BF16),
                   jax.ShapeDtypeStruct((S, 256), BF16)),
        compiler_params=_cp("parallel"),
    )(dm, o_a, o_b, o_c, glog, w_br, w_br_t)


def _ffn_act_fwd(up_pre, fcw, *, name, tc, ts=512):
    S, F2 = up_pre.shape
    ts = min(ts, S)
    nj = F2 // (2 * tc)

    def body(cur_ref, prev_ref, w_ref, h_ref):
        i = pl.program_id(1)
        prev = jnp.where(i > 0, prev_ref[...].astype(F32), 0.0)
        x = jnp.concatenate([prev, cur_ref[...].astype(F32)], axis=0)
        up = (w_ref[2:3, :] * x + w_ref[1:2, :] * _shift_dn(x, 1) + w_ref[0:1, :] * _shift_dn(x, 2))[HALO:HALO + ts]
        a, b = up[:, 0:tc], up[:, tc:2 * tc]
        h_ref[...] = (a * _sigmoid(a) * b).astype(BF16)

    hb = ts // HALO
    return pl.pallas_call(
        body, name=name, grid=(nj, S // ts),
        in_specs=[pl.BlockSpec((ts, 2 * tc), lambda j, i: (i, j)),
                  pl.BlockSpec((HALO, 2 * tc), lambda j, i: (jnp.maximum(i * hb - 1, 0), j)),
                  pl.BlockSpec((3, 2 * tc), lambda j, i: (0, j))],
        out_specs=pl.BlockSpec((ts, tc), lambda j, i: (i, j)),
        out_shape=jax.ShapeDtypeStruct((S, F2 // 2), BF16),
        compiler_params=_cp("parallel", "parallel"),
    )(up_pre, up_pre, fcw)


def _ffn_act_bwd(up_pre, dh, fcw, *, name, tc, ts=512):
    S, F2 = up_pre.shape
    ts = min(ts, S)
    nt = S // ts
    nj = F2 // (2 * tc)

    def body(cur_ref, prev_ref, nxt_ref, dhc_ref, dhn_ref, w_ref, dpre_ref, dw_ref):
        i = pl.program_id(1)

        @pl.when(i == 0)
        def _():
            dw_ref[...] = jnp.zeros_like(dw_ref)

        prev = jnp.where(i > 0, prev_ref[...].astype(F32), 0.0)
        x = jnp.concatenate([prev, cur_ref[...].astype(F32), nxt_ref[...].astype(F32)], axis=0)
        dh_n = jnp.where(i < nt - 1, dhn_ref[...].astype(F32), 0.0)
        dh_e = jnp.concatenate([jnp.zeros((HALO, tc), F32), dhc_ref[...].astype(F32), dh_n], axis=0)
        w0, w1, w2 = w_ref[0:1, :], w_ref[1:2, :], w_ref[2:3, :]
        x1, x2 = _shift_dn(x, 1), _shift_dn(x, 2)
        up = w2 * x + w1 * x1 + w0 * x2
        a, b = up[:, 0:tc], up[:, tc:2 * tc]
        sg = _sigmoid(a)
        da = dh_e * b * (sg * (1.0 + a * (1.0 - sg)))
        db = dh_e * (a * sg)
        dup = jnp.concatenate([da, db], axis=1)
        dpre = w2 * dup + w1 * _shift_up(dup, 1) + w0 * _shift_up(dup, 2)
        tile = slice(HALO, HALO + ts)
        dpre_ref[...] = dpre[tile].astype(BF16)
        dup_t = dup[tile]
        dw_ref[0:1, :] += jnp.sum(dup_t * x2[tile], axis=0, keepdims=True)
        dw_ref[1:2, :] += jnp.sum(dup_t * x1[tile], axis=0, keepdims=True)
        dw_ref[2:3, :] += jnp.sum(dup_t * x[tile], axis=0, keepdims=True)

    hb = ts // HALO
    nblk = S // HALO
    prev = lambda j, i: (jnp.maximum(i * hb - 1, 0), j)
    nxt = lambda j, i: (jnp.minimum((i + 1) * hb, nblk - 1), j)
    return pl.pallas_call(
        body, name=name, grid=(nj, nt),
        in_specs=[pl.BlockSpec((ts, 2 * tc), lambda j, i: (i, j)), pl.BlockSpec((HALO, 2 * tc), prev),
                  pl.BlockSpec((HALO, 2 * tc), nxt),
                  pl.BlockSpec((ts, tc), lambda j, i: (i, j)), pl.BlockSpec((HALO, tc), nxt),
                  pl.BlockSpec((3, 2 * tc), lambda j, i: (0, j))],
        out_specs=[pl.BlockSpec((ts, 2 * tc), lambda j, i: (i, j)), pl.BlockSpec((3, 2 * tc), lambda j, i: (0, j))],
        out_shape=(jax.ShapeDtypeStruct((S, F2), BF16), jax.ShapeDtypeStruct((3, F2), F32)),
        compiler_params=_cp("parallel", "arbitrary"),
    )(up_pre, up_pre, up_pre, dh, dh, fcw)


def _ln_bwd(dy, xhat, rstd, gamma, *, name, ts=512):
    S, D = dy.shape
    ts = min(ts, S)

    def body(dy_ref, xh_ref, rs_ref, g_ref, dz_ref, dg_ref, db_ref):
        @pl.when(pl.program_id(0) == 0)
        def _():
            dg_ref[...] = jnp.zeros_like(dg_ref)
            db_ref[...] = jnp.zeros_like(db_ref)

        dyv = dy_ref[...]
        xh = xh_ref[...].astype(F32)
        dyg = dyv * g_ref[...]
        c1 = jnp.mean(dyg, axis=-1, keepdims=True)
        c2 = jnp.mean(dyg * xh, axis=-1, keepdims=True)
        dz_ref[...] = rs_ref[...] * (dyg - c1 - xh * c2)
        dg_ref[...] += jnp.sum(dyv * xh, axis=0, keepdims=True)
        db_ref[...] += jnp.sum(dyv, axis=0, keepdims=True)

    cur = lambda i: (i, 0)
    const = lambda i: (0, 0)
    return pl.pallas_call(
        body, name=name, grid=(S // ts,),
        in_specs=[pl.BlockSpec((ts, D), cur), pl.BlockSpec((ts, D), cur), pl.BlockSpec((ts, 1), cur),
                  pl.BlockSpec((1, D), const)],
        out_specs=[pl.BlockSpec((ts, D), cur), pl.BlockSpec((1, D), const), pl.BlockSpec((1, D), const)],
        out_shape=(jax.ShapeDtypeStruct((S, D), F32), jax.ShapeDtypeStruct((1, D), F32),
                   jax.ShapeDtypeStruct((1, D), F32)),
        compiler_params=_cp("arbitrary"),
    )(dy, xhat, rstd, gamma)


def _loss_head(y, tgt, *, name, ts=512):
    S, D = y.shape
    ts = min(ts, S)

    def body(y_ref, t_ref, dy_ref, sq_ref):
        @pl.when(pl.program_id(0) == 0)
        def _():
            sq_ref[...] = jnp.zeros_like(sq_ref)

        e = y_ref[...] - t_ref[...]
        dy_ref[...] = e * (1.0 / D)
        sq_ref[...] += jnp.sum(e * e, axis=0, keepdims=True)

    cur = lambda i: (i, 0)
    return pl.pallas_call(
        body, name=name, grid=(S // ts,),
        in_specs=[pl.BlockSpec((ts, D), cur), pl.BlockSpec((ts, D), cur)],
        out_specs=[pl.BlockSpec((ts, D), cur), pl.BlockSpec((1, D), lambda i: (0, 0))],
        out_shape=(jax.ShapeDtypeStruct((S, D), F32), jax.ShapeDtypeStruct((1, D), F32)),
        compiler_params=_cp("arbitrary"),
    )(y, tgt)


def _adamw(w, g, m, v, *, name, tr=512):
    R, C = w.shape
    tr = _div_tile(R, tr)
    c1 = 1.0 - ADAM_B1 ** ADAM_STEP
    c2 = 1.0 - ADAM_B2 ** ADAM_STEP

    def body(w_ref, g_ref, m_ref, v_ref, d_ref, mo_ref, vo_ref):
        gv = g_ref[...]
        m2 = ADAM_B1 * m_ref[...] + (1.0 - ADAM_B1) * gv
        v2 = ADAM_B2 * v_ref[...] + (1.0 - ADAM_B2) * (gv * gv)
        m_hat = m2 / c1
        v_hat = v2 / c2
        d_ref[...] = -ADAM_LR * (m_hat / (jnp.sqrt(v_hat) + ADAM_EPS) + ADAM_WD * w_ref[...])
        mo_ref[...] = m2
        vo_ref[...] = v2

    spec = pl.BlockSpec((tr, C), lambda i: (i, 0))
    return pl.pallas_call(
        body, name=name, grid=(R // tr,),
        in_specs=[spec] * 4, out_specs=[spec] * 3,
        out_shape=(jax.ShapeDtypeStruct((R, C), F32),) * 3,
        compiler_params=_cp("parallel"),
    )(w, g, m, v)


def _interleave_cols(w, nj):
    lead, f2 = w.shape[:-1], w.shape[-1]
    tc = f2 // (2 * nj)
    w = w.reshape(lead + (2, nj, tc))
    return jnp.swapaxes(w, -3, -2).reshape(lead + (f2,))


def _deinterleave_cols(w, nj):
    lead, f2 = w.shape[:-1], w.shape[-1]
    tc = f2 // (2 * nj)
    w = w.reshape(lead + (nj, 2, tc))
    return jnp.swapaxes(w, -3, -2).reshape(lead + (f2,))


def _block_diag(w_pool):
    return jnp.concatenate([jnp.pad(w_pool[g], ((0, 0), (64 * g, 192 - 64 * g))) for g in range(4)], axis=0)


def _pad_rows(v, rows):
    return jnp.pad(v, (0, rows * LANES - v.shape[0])).reshape(rows, LANES)


def kernel(x, positions, w_in, w_pool, pool_scale, attn_sinks, conv_w, w_branch_a, w_branch_b, w_branch_c, w_o, ln1_g, ln1_b, w_up, ffn_conv_w, w_down, ln2_g, ln2_b, loss_target, m_w_in, m_w_pool, m_pool_scale, m_attn_sinks, m_conv_w, m_w_branch_a, m_w_branch_b, m_w_branch_c, m_w_o, m_ln1_g, m_ln1_b, m_w_up, m_ffn_conv_w, m_w_down, m_ln2_g, m_ln2_b, v_w_in, v_w_pool, v_pool_scale, v_attn_sinks, v_conv_w, v_w_branch_a, v_w_branch_b, v_w_branch_c, v_w_o, v_ln1_g, v_ln1_b, v_w_up, v_ffn_conv_w, v_w_down, v_ln2_g, v_ln2_b):
    L, D, in_shard = w_in.shape
    S = x.shape[1]
    IN = in_shard * N_DEV
    F2 = w_up.shape[2] * N_DEV
    F = F2 // 2
    assert D == 1024 and IN == 1792 + 3 * D and x.shape[0] == 1 and S % 512 == 0
    alpha = (2 * L) ** 0.25
    NJ = 2
    TC = F // NJ
    xs = x.reshape(S, D)
    tgt = loss_target.reshape(S, D)

    big = [w_in, w_branch_a, w_branch_b, w_branch_c, w_o, w_up, w_down]
    big_rows = [a.size // LANES for a in big]
    packed = jnp.concatenate([a.reshape(-1, LANES).astype(BF16) for a in big], axis=0)
    gathered = _all_gather(packed, "ag_weights")
    n_cw, n_fw = conv_w.size, ffn_conv_w.size
    small_rows = -(-(n_cw + n_fw) // LANES)
    small = _pad_rows(jnp.concatenate([conv_w.reshape(-1), ffn_conv_w.reshape(-1)]), small_rows)
    gsmall = _all_gather(small, "ag_conv_weights").reshape(N_DEV, -1)
    conv_full = gsmall[:, :n_cw].reshape(N_DEV, L, 3, -1).transpose(1, 2, 0, 3).reshape(L, 3, 256)
    fcw_full = gsmall[:, n_cw:n_cw + n_fw].reshape(N_DEV, L, 3, -1).transpose(1, 2, 0, 3).reshape(L, 3, F2)
    fcw_full = _interleave_cols(fcw_full, NJ)

    offs = [0]
    for r in big_rows:
        offs.append(offs[-1] + r)

    def shard_of(widx, l, shape):
        r = big_rows[widx] // L
        return gathered[:, offs[widx] + l * r: offs[widx] + (l + 1) * r, :].reshape((N_DEV,) + shape)

    W = []
    for l in range(L):
        win = shard_of(0, l, (D, in_shard)).transpose(1, 0, 2).reshape(D, IN)
        wg = win[:, 1792:]
        wp = jnp.concatenate([win[:, 0:256], win[:, 1024:1792]], axis=1)
        wq = win[:, 256:1024]
        wa = shard_of(1, l, (256, D // N_DEV)).transpose(1, 0, 2).reshape(256, D)
        wb = shard_of(2, l, (512, D // N_DEV)).transpose(1, 0, 2).reshape(512, D)
        wc = shard_of(3, l, (256, D // N_DEV)).transpose(1, 0, 2).reshape(256, D)
        wbr = jnp.concatenate([wa, wb, wc], axis=0)
        wo = shard_of(4, l, (D // N_DEV, D)).reshape(D, D)
        wup = _interleave_cols(shard_of(5, l, (D, F2 // N_DEV)).transpose(1, 0, 2).reshape(D, F2), NJ)
        wdn = shard_of(6, l, (F // N_DEV, D)).reshape(F, D)
        wbd = _block_diag(w_pool[l]).astype(BF16)
        W.append(dict(wg=wg, wp=wp, wq=wq, wg_t=wg.T, wp_t=wp.T, wq_t=wq.T, wbr=wbr, wbr_t=wbr.T,
                      wo=wo, wo_t=wo.T, wup=wup, wup_t=wup.T, wdn=wdn, wdn_t=wdn.T,
                      wbd=wbd, wbd_t=wbd.T, scale=pool_scale[l].reshape(1, 256), conv=conv_full[l],
                      fcw=fcw_full[l], sinks=jnp.broadcast_to(attn_sinks[l].reshape(8, 1), (8, 128)),
                      g1=ln1_g[l].reshape(1, D), b1=ln1_b[l].reshape(1, D),
                      g2=ln2_g[l].reshape(1, D), b2=ln2_b[l].reshape(1, D)))

    inv_freq = ROPE_THETA ** (-jnp.arange(0, ROT_DIM, 2, dtype=F32) / ROT_DIM)
    head_lane = jnp.concatenate([inv_freq, inv_freq, jnp.zeros((HEAD_DIM - ROT_DIM,), F32)])
    head_sign = jnp.concatenate([-jnp.ones((8,), F32), jnp.ones((8,), F32), jnp.zeros((HEAD_DIM - ROT_DIM,), F32)])
    inv_lane = jnp.tile(head_lane, 2).reshape(1, 128)
    sign_lane = jnp.tile(head_sign, 2).reshape(1, 128)
    cos_t, sin_t = _rope_tables(positions.reshape(S, 1), inv_lane, sign_lane, "rope_tables")

    saved = []
    h_in = xs
    for l in range(L):
        w = W[l]
        pg = _mm(h_in, w["wg"], out_dtype=BF16, name="proj_gate", tn=1024)
        pp = _mm(h_in, w["wp"], out_dtype=BF16, name="proj_poolconv")
        pq = _mm(h_in, w["wq"], out_dtype=BF16, name="proj_qkv")
        o_a, o_c = _poolconv_fwd(pp, w["wbd"], w["scale"], w["conv"], name="poolconv_fwd")
        o_b, lse = _attn_fwd(pq, cos_t, sin_t, w["sinks"], name="attn_fwd")
        merged = _merge_fwd(o_a, o_b, o_c, pg, w["wbr"], name="merge_fwd")
        x1, xh1, rs1 = _mm_ln(merged, w["wo"], h_in, w["g1"], w["b1"], alpha=alpha, name="wo_ln1")
        up_pre = _mm(x1, w["wup"], out_dtype=BF16, name="ffn_up", tn=2 * TC)
        hact = _ffn_act_fwd(up_pre, w["fcw"], name="ffn_act_fwd", tc=TC, ts=256)
        x2, xh2, rs2 = _mm_ln(hact, w["wdn"], x1, w["g2"], w["b2"], alpha=alpha, name="down_ln2")
        saved.append(dict(x0=h_in, pg=pg, pp=pp, pq=pq, o_a=o_a, o_b=o_b, o_c=o_c, lse=lse, merged=merged,
                          x1=x1, xh1=xh1, rs1=rs1, up_pre=up_pre, hact=hact, xh2=xh2, rs2=rs2))
        h_in = x2

    dy, sq_lanes = _loss_head(h_in, tgt, name="loss_head")

    gw = [None] * L
    for l in reversed(range(L)):
        w, sv = W[l], saved[l]
        dz2, dg2, db2 = _ln_bwd(dy, sv["xh2"], sv["rs2"], w["g2"], name="ln2_bwd")
        dh = _mm(dz2, w["wdn_t"], out_dtype=BF16, name="down_bwd_x", tn=TC)
        dw_dn = _mm_tn(sv["hact"], dz2, name="down_bwd_w", tka=TC, na=NJ, tn=D)
        dpre, dfcw = _ffn_act_bwd(sv["up_pre"], dh, w["fcw"], name="ffn_act_bwd", tc=TC, ts=256)
        dx1 = _mm(dpre, w["wup_t"], out_dtype=F32, name="up_bwd_x", tk=2 * TC, add=dz2, add_scale=alpha)
        dw_up = _mm_tn(sv["x1"], dpre, name="up_bwd_w", tka=D, tn=2 * TC, nb=NJ)
        dz1, dg1, db1 = _ln_bwd(dx1, sv["xh1"], sv["rs1"], w["g1"], name="ln1_bwd")
        dmerged = _mm(dz1, w["wo_t"], out_dtype=BF16, name="wo_bwd_x")
        dw_o = _mm_tn(sv["merged"], dz1, name="wo_bwd_w", tka=D, tn=D)
        dpg, dprod, do_a, do_b, do_c = _merge_bwd(dmerged, sv["o_a"], sv["o_b"], sv["o_c"], sv["pg"],
                                                  w["wbr"], w["wbr_t"], name="merge_bwd")
        dw_a = _mm_tn(sv["o_a"], dprod, name="branch_a_bwd_w", tka=256, tn=D, b_off=0)
        dw_b = _mm_tn(sv["o_b"], dprod, name="branch_b_bwd_w", tka=512, tn=D, b_off=1)
        dw_c = _mm_tn(sv["o_c"], dprod, name="branch_c_bwd_w", tka=256, tn=D, b_off=2)
        dpq, dsink = _attn_bwd(sv["pq"], cos_t, sin_t, w["sinks"], do_b, sv["o_b"], sv["lse"], name="attn_bwd")
        dpp, pooled, dmixed, dscale, dconv = _poolconv_bwd(sv["pp"], do_a, do_c, w["wbd"], w["wbd_t"], w["scale"],
                                                           w["conv"], name="poolconv_bwd")
        dwbd = _mm_tn(pooled, dmixed, name="pool_bwd_w", tka=256, tn=256)
        dx = _mm(dpg, w["wg_t"], out_dtype=F32, name="proj_gate_bwd_x", tk=1024, add=dz1, add_scale=alpha)
        dx = _mm(dpp, w["wp_t"], out_dtype=F32, name="proj_poolconv_bwd_x", add=dx)
        dx = _mm(dpq, w["wq_t"], out_dtype=F32, name="proj_qkv_bwd_x", add=dx)
        dw_g = _mm_tn(sv["x0"], dpg, name="proj_gate_bwd_w", tka=D, tn=1024, nb=3)
        dw_p = _mm_tn(sv["x0"], dpp, name="proj_poolconv_bwd_w", tka=D, tn=1024)
        dw_q = _mm_tn(sv["x0"], dpq, name="proj_qkv_bwd_w", tka=D, tn=768)
        dw_in = jnp.concatenate([dw_p[:, 0:256], dw_q, dw_p[:, 256:1024], dw_g], axis=1)
        dw_pool = jnp.stack([dwbd[64 * g:64 * (g + 1), 64 * g:64 * (g + 1)] for g in range(4)])
        gw[l] = dict(w_in=dw_in, a=dw_a, b=dw_b, c=dw_c, w_o=dw_o, w_up=_deinterleave_cols(dw_up, NJ), w_down=dw_dn,
                     w_pool=dw_pool, scale=dscale, sinks=dsink[:, 0], conv=dconv, fcw=_deinterleave_cols(dfcw, NJ),
                     g1=dg1, b1=db1, g2=dg2, b2=db2)
        dy = dx
    grad_x = dy.reshape(1, S, D)

    def stack(k):
        return jnp.stack([gw[l][k] for l in range(L)])

    col = lambda g, n: g.reshape(L, g.shape[1], N_DEV, n).transpose(2, 0, 1, 3)
    row = lambda g, n: g.reshape(L, N_DEV, n, g.shape[2]).transpose(1, 0, 2, 3)
    parts = [col(stack("w_in"), in_shard), col(stack("a"), D // N_DEV), col(stack("b"), D // N_DEV),
             col(stack("c"), D // N_DEV), row(stack("w_o"), D // N_DEV), col(stack("w_up"), F2 // N_DEV),
             row(stack("w_down"), F // N_DEV)]
    p_all = jnp.concatenate([p.reshape(N_DEV, -1, LANES).astype(BF16) for p in parts], axis=1)
    my_c = lax.axis_index("c").astype(jnp.int32).reshape(1)
    my_chip = (2 * lax.axis_index("x") + lax.axis_index("y")).astype(jnp.int32).reshape(1)
    from_sibling = _rs_sibling(p_all, "rs_sibling")
    pair_sum = _sum_sibling(p_all, from_sibling, my_c, "rs_sum_sibling")
    from_chips = _rs_chips(pair_sum, "rs_chips")
    g_big = _sum_chips(pair_sum, from_chips, my_chip, "rs_sum_chips")

    rep_vec = jnp.concatenate([
        stack("w_pool").reshape(-1), stack("scale").reshape(-1), stack("g1").reshape(-1), stack("b1").reshape(-1),
        stack("g2").reshape(-1), stack("b2").reshape(-1)])
    n_rep_full = -(-rep_vec.shape[0] // LANES)
    sinks_row = jnp.pad(stack("sinks").reshape(-1), (0, LANES - 8 * L))
    rep_vec = jnp.concatenate([_pad_rows(rep_vec, n_rep_full).reshape(-1), sinks_row, sq_lanes.reshape(-1)])
    loss_row = n_rep_full + 1
    n_rep = -(-(loss_row + 1) // 8) * 8
    rep_rows = _pad_rows(rep_vec, n_rep)
    dconv_by_dev = stack("conv").reshape(L, 3, N_DEV, -1).transpose(2, 0, 1, 3).reshape(N_DEV, -1)
    dfcw_by_dev = stack("fcw").reshape(L, 3, N_DEV, -1).transpose(2, 0, 1, 3).reshape(N_DEV, -1)
    n_mine = -(-(small_rows) // 8) * 8
    by_dev = jnp.concatenate([dconv_by_dev, dfcw_by_dev], axis=1)
    by_dev = jnp.pad(by_dev, ((0, 0), (0, n_mine * LANES - by_dev.shape[1]))).reshape(N_DEV * n_mine, LANES)
    small_g = _all_gather(jnp.concatenate([rep_rows, by_dev], axis=0), "ag_small_grads")
    rep_sum, mine_sum, loss11 = _small_reduce(small_g, n_rep, n_mine, 1.0 / D, loss_row, "small_reduce")
    loss = loss11[0, 0]

    names_big = ["w_in", "w_branch_a", "w_branch_b", "w_branch_c", "w_o", "w_up", "w_down"]
    ms_big = [m_w_in, m_w_branch_a, m_w_branch_b, m_w_branch_c, m_w_o, m_w_up, m_w_down]
    vs_big = [v_w_in, v_w_branch_a, v_w_branch_b, v_w_branch_c, v_w_o, v_w_up, v_w_down]
    out = {}
    for k, name in enumerate(names_big):
        wk = big[k]
        g2d = g_big[offs[k]:offs[k + 1]]
        c2 = wk.shape[-1]
        as2d = lambda a: a.reshape(-1, c2)
        g_nat = g2d.reshape(wk.shape)
        d, mo, vo = _adamw(as2d(wk), as2d(g_nat), as2d(ms_big[k]), as2d(vs_big[k]), name="adamw_" + name)
        out[name] = (g_nat, d.reshape(wk.shape), mo.reshape(wk.shape), vo.reshape(wk.shape))

    def rep_pack(wp_, sc_, g1_, b1_, g2_, b2_, sk_):
        v = jnp.concatenate([wp_.reshape(-1), sc_.reshape(-1), g1_.reshape(-1), b1_.reshape(-1), g2_.reshape(-1),
                             b2_.reshape(-1)])
        return _pad_rows(jnp.concatenate([_pad_rows(v, n_rep_full).reshape(-1), sk_.reshape(-1)]), n_rep)

    def mine_pack(cw_, fw_):
        return _pad_rows(jnp.concatenate([cw_.reshape(-1), fw_.reshape(-1)]), n_mine)

    w_rep = rep_pack(w_pool, pool_scale, ln1_g, ln1_b, ln2_g, ln2_b, attn_sinks)
    m_rep = rep_pack(m_w_pool, m_pool_scale, m_ln1_g, m_ln1_b, m_ln2_g, m_ln2_b, m_attn_sinks)
    v_rep = rep_pack(v_w_pool, v_pool_scale, v_ln1_g, v_ln1_b, v_ln2_g, v_ln2_b, v_attn_sinks)
    g_rep = jnp.concatenate([rep_sum[:loss_row], jnp.zeros((n_rep - loss_row, LANES), F32)], axis=0)
    rep_res = (g_rep,) + tuple(_adamw(w_rep, g_rep, m_rep, v_rep, name="adamw_replicated"))
    w_mine = mine_pack(conv_w, ffn_conv_w)
    mine_res = (mine_sum,) + tuple(_adamw(w_mine, mine_sum, mine_pack(m_conv_w, m_ffn_conv_w),
                                          mine_pack(v_conv_w, v_ffn_conv_w), name="adamw_conv"))

    def rep_unpack(buf):
        flat = buf.reshape(-1)
        res, o = {}, 0
        for nm, ref in (("w_pool", w_pool), ("pool_scale", pool_scale), ("ln1_g", ln1_g), ("ln1_b", ln1_b),
                        ("ln2_g", ln2_g), ("ln2_b", ln2_b)):
            res[nm] = flat[o:o + ref.size].reshape(ref.shape)
            o += ref.size
        o = n_rep_full * LANES
        res["attn_sinks"] = flat[o:o + attn_sinks.size].reshape(attn_sinks.shape)
        return res

    def mine_unpack(buf):
        flat = buf.reshape(-1)
        return {"conv_w": flat[:n_cw].reshape(conv_w.shape),
                "ffn_conv_w": flat[n_cw:n_cw + n_fw].reshape(ffn_conv_w.shape)}

    order = ["w_in", "w_pool", "pool_scale", "attn_sinks", "conv_w", "w_branch_a", "w_branch_b", "w_branch_c", "w_o",
             "ln1_g", "ln1_b", "w_up", "ffn_conv_w", "w_down", "ln2_g", "ln2_b"]
    results = [loss, grad_x]
    for kind in range(4):
        rep_k, mine_k = rep_unpack(rep_res[kind]), mine_unpack(mine_res[kind])
        for nm in order:
            if nm in out:
                results.append(out[nm][kind])
            elif nm in rep_k:
                results.append(rep_k[nm])
            else:
                results.append(mine_k[nm])
    return tuple(results)
```

```python
import functools

import jax
import jax.numpy as jnp
from jax import lax
from jax.experimental import pallas as pl
from jax.experimental.pallas import tpu as pltpu

F32 = jnp.float32
BF16 = jnp.bfloat16

HEAD_DIM = 64
N_Q_HEADS = 8
GROUP = 4
WINDOW = 128
ROT_DIM = 16
ROPE_THETA = 500000.0
POOL_WINDOWS = (2, 4, 8, 16)
LN_EPS = 1e-5
MASK_VALUE = -1e30
ADAM_LR, ADAM_B1, ADAM_B2, ADAM_EPS, ADAM_WD, ADAM_STEP = 0.001, 0.9, 0.999, 1e-08, 0.01, 10

N_DEV = 8
LANES = 1024
HALO = 16
MESH = pl.DeviceIdType.MESH
VMEM_LIMIT = 56 * 1024 * 1024


def _div_tile(n, want, mult=8):
    for t in range(min(want, n) // mult * mult, 0, -mult):
        if n % t == 0:
            return t
    return n


def _cp(*sem):
    return pltpu.CompilerParams(dimension_semantics=sem, vmem_limit_bytes=VMEM_LIMIT)


def _coords():
    return lax.axis_index("x"), lax.axis_index("y"), lax.axis_index("c")


def _all_gather(xs, name):
    R, C = xs.shape

    def body(x_ref, out_ref, send_sems, recv_sems, local_sem):
        _ag_start(x_ref, out_ref, send_sems, recv_sems, local_sem)
        _ag_finish(x_ref, out_ref, send_sems, recv_sems, local_sem)

    return pl.pallas_call(
        body, name=name,
        out_shape=jax.ShapeDtypeStruct((N_DEV, R, C), xs.dtype),
        in_specs=[pl.BlockSpec(memory_space=pl.ANY)],
        out_specs=pl.BlockSpec(memory_space=pl.ANY),
        scratch_shapes=_AG_SEMS,
    )(xs)


_AG_SEMS = [pltpu.SemaphoreType.DMA((7,)), pltpu.SemaphoreType.DMA((7,)), pltpu.SemaphoreType.DMA(())]


def _ag_copies(x_ref, out_ref, send_sems, recv_sems, local_sem):
    x, y, c = _coords()
    me, sibling = (x, y, c), (x, y, 1 - c)
    chips = [(1 - x, y), (x, 1 - y), (1 - x, 1 - y)]

    def slot(px, py, pc):
        return out_ref.at[4 * px + 2 * py + pc]

    def copy(k, block, to, src=None):
        return pltpu.make_async_remote_copy(
            src_ref=slot(*block) if src is None else src, dst_ref=slot(*block),
            send_sem=send_sems.at[k], recv_sem=recv_sems.at[k], device_id=to, device_id_type=MESH)

    mine = pltpu.make_async_copy(x_ref, slot(*me), local_sem)
    first = [copy(0, me, sibling, src=x_ref)]
    first += [copy(1 + j, me, (*chip, c), src=x_ref) for j, chip in enumerate(chips)]
    passed = [copy(4 + j, (*chip, c), sibling) for j, chip in enumerate(chips)]
    from_chips = [copy(1 + j, (*chip, c), me) for j, chip in enumerate(chips)]
    from_sibling = [copy(0, sibling, me)] + [copy(4 + j, (*chip, 1 - c), me) for j, chip in enumerate(chips)]
    return mine, first, passed, from_chips, from_sibling


def _ag_start(*refs):
    mine, first, _, _, _ = _ag_copies(*refs)
    mine.start()
    for cp in first:
        cp.start()


def _ag_finish(*refs):
    mine, first, passed, from_chips, from_sibling = _ag_copies(*refs)
    for j in range(3):
        from_chips[j].wait_recv()
        passed[j].start()
    for cp in from_sibling:
        cp.wait_recv()
    for cp in first + passed:
        cp.wait_send()
    mine.wait()


def _rs_sibling(p, name):
    _, R, C = p.shape

    def body(p_ref, out_ref, send_sems, recv_sems):
        x, y, c = _coords()
        copies = []
        for j in range(4):
            cx, cy = j // 2, j % 2
            copies.append(pltpu.make_async_remote_copy(
                src_ref=p_ref.at[4 * cx + 2 * cy + (1 - c)], dst_ref=out_ref.at[j],
                send_sem=send_sems.at[j], recv_sem=recv_sems.at[j], device_id=(x, y, 1 - c), device_id_type=MESH))
        for cp in copies:
            cp.start()
        for cp in copies:
            cp.wait_recv()
        for cp in copies:
            cp.wait_send()

    return pl.pallas_call(
        body, name=name,
        out_shape=jax.ShapeDtypeStruct((4, R, C), p.dtype),
        in_specs=[pl.BlockSpec(memory_space=pl.ANY)],
        out_specs=pl.BlockSpec(memory_space=pl.ANY),
        scratch_shapes=[pltpu.SemaphoreType.DMA((4,)), pltpu.SemaphoreType.DMA((4,))],
    )(p)


def _rs_chips(q, name):
    _, R, C = q.shape

    def body(q_ref, out_ref, send_sems, recv_sems):
        _rs_chips_start(q_ref, out_ref, send_sems, recv_sems)
        _rs_chips_finish(q_ref, out_ref, send_sems, recv_sems)

    return pl.pallas_call(
        body, name=name,
        out_shape=jax.ShapeDtypeStruct((3, R, C), q.dtype),
        in_specs=[pl.BlockSpec(memory_space=pl.ANY)],
        out_specs=pl.BlockSpec(memory_space=pl.ANY),
        scratch_shapes=_RS_SEMS,
    )(q)


_RS_SEMS = [pltpu.SemaphoreType.DMA((3,)), pltpu.SemaphoreType.DMA((3,))]


def _rs_chips_copies(q_ref, out_ref, send_sems, recv_sems):
    x, y, c = _coords()
    chips = [(1 - x, y), (x, 1 - y), (1 - x, 1 - y)]
    return [pltpu.make_async_remote_copy(
        src_ref=q_ref.at[2 * cx + cy], dst_ref=out_ref.at[k],
        send_sem=send_sems.at[k], recv_sem=recv_sems.at[k], device_id=(cx, cy, c), device_id_type=MESH)
        for k, (cx, cy) in enumerate(chips)]


def _rs_chips_start(*refs):
    for cp in _rs_chips_copies(*refs):
        cp.start()


def _rs_chips_finish(*refs):
    copies = _rs_chips_copies(*refs)
    for cp in copies:
        cp.wait_recv()
    for cp in copies:
        cp.wait_send()


def _sum_sibling(p, recv, my_c, name, tr=512):
    _, R, C = p.shape
    tr = _div_tile(R, tr, 16)

    def body(c_ref, p_ref, r_ref, o_ref):
        o_ref[...] = (p_ref[...].astype(F32) + r_ref[...].astype(F32)).astype(o_ref.dtype)

    grid_spec = pltpu.PrefetchScalarGridSpec(
        num_scalar_prefetch=1, grid=(4, R // tr),
        in_specs=[pl.BlockSpec((1, tr, C), lambda j, r, c_ref: (4 * (j // 2) + 2 * (j % 2) + c_ref[0], r, 0)),
                  pl.BlockSpec((1, tr, C), lambda j, r, c_ref: (j, r, 0))],
        out_specs=pl.BlockSpec((1, tr, C), lambda j, r, c_ref: (j, r, 0)))
    return pl.pallas_call(body, name=name, grid_spec=grid_spec,
                          out_shape=jax.ShapeDtypeStruct((4, R, C), p.dtype),
                          compiler_params=_cp("parallel", "parallel"))(my_c, p, recv)


def _sum_chips(q, recv, my_chip, name, tr=512):
    _, R, C = q.shape
    tr = _div_tile(R, tr, 16)

    def body(i_ref, q_ref, r_ref, o_ref):
        acc = q_ref[0].astype(F32)
        for k in range(3):
            acc = acc + r_ref[k].astype(F32)
        o_ref[...] = acc

    grid_spec = pltpu.PrefetchScalarGridSpec(
        num_scalar_prefetch=1, grid=(R // tr,),
        in_specs=[pl.BlockSpec((1, tr, C), lambda r, i_ref: (i_ref[0], r, 0)),
                  pl.BlockSpec((3, tr, C), lambda r, i_ref: (0, r, 0))],
        out_specs=pl.BlockSpec((tr, C), lambda r, i_ref: (r, 0)))
    return pl.pallas_call(body, name=name, grid_spec=grid_spec,
                          out_shape=jax.ShapeDtypeStruct((R, C), F32),
                          compiler_params=_cp("parallel"))(my_chip, q, recv)


def _small_reduce(g, n_rep, n_mine, inv_d, loss_row, name):
    _, R, C = g.shape

    def body(g_ref, rep_ref, mine_ref, loss_ref):
        x, y, c = _coords()
        start = pl.multiple_of(n_rep + (4 * x + 2 * y + c) * n_mine, 8)
        rep = g_ref[0, 0:n_rep, :]
        mine = g_ref[0, pl.ds(start, n_mine), :]
        sq = g_ref[0, loss_row:loss_row + 1, :]
        for d in range(1, N_DEV):
            rep = rep + g_ref[d, 0:n_rep, :]
            mine = mine + g_ref[d, pl.ds(start, n_mine), :]
            sq = sq + g_ref[d, loss_row:loss_row + 1, :]
        rep_ref[...] = rep
        mine_ref[...] = mine
        loss_ref[...] = (0.5 * inv_d) * jnp.sum(sq, axis=1, keepdims=True)

    return pl.pallas_call(
        body, name=name,
        out_shape=(jax.ShapeDtypeStruct((n_rep, C), F32), jax.ShapeDtypeStruct((n_mine, C), F32),
                   jax.ShapeDtypeStruct((1, 1), F32)),
        compiler_params=pltpu.CompilerParams(vmem_limit_bytes=VMEM_LIMIT),
    )(g)


def _mm(a, b, *, out_dtype, name, tm=512, tn=None, tk=None, add=None, add_scale=1.0, gather=None):
    M, K = a.shape
    N = b.shape[1]
    tm = min(tm, M)
    tn = N if tn is None else tn
    tk = K if tk is None else tk
    nk = K // tk
    has_add = add is not None
    has_ag = gather is not None
    n_i, n_j = M // tm, N // tn

    def body(*refs):
        a_ref, b_ref = refs[0], refs[1]
        add_ref = refs[2] if has_add else None
        n_in = 2 + has_add + has_ag
        o_ref = refs[n_in]
        if has_ag:
            ag_refs = (refs[n_in - 1], refs[n_in + 1]) + tuple(refs[n_in + 2:n_in + 5])
            pid = (pl.program_id(0), pl.program_id(1), pl.program_id(2))

            @pl.when((pid[0] == 0) & (pid[1] == 0) & (pid[2] == 0))
            def _():
                _ag_start(*ag_refs)

        part = jnp.dot(a_ref[...].astype(BF16), b_ref[...].astype(BF16), preferred_element_type=F32)

        def finish(r):
            if has_add:
                r = r + add_scale * add_ref[...].astype(F32)
            o_ref[...] = r.astype(out_dtype)

        if nk == 1:
            finish(part)
        else:
            acc_ref = refs[-1]
            k = pl.program_id(2)

            @pl.when(k == 0)
            def _():
                acc_ref[...] = part

            @pl.when(k > 0)
            def _():
                acc_ref[...] += part

            @pl.when(k == nk - 1)
            def _():
                finish(acc_ref[...])

        if has_ag:
            @pl.when((pid[0] == n_i - 1) & (pid[1] == n_j - 1) & (pid[2] == nk - 1))
            def _():
                _ag_finish(*ag_refs)

    in_specs = [pl.BlockSpec((tm, tk), lambda i, j, k: (i, k)), pl.BlockSpec((tk, tn), lambda i, j, k: (k, j))]
    args = [a, b]
    if has_add:
        in_specs.append(pl.BlockSpec((tm, tn), lambda i, j, k: (i, j)))
        args.append(add)
    out_specs = [pl.BlockSpec((tm, tn), lambda i, j, k: (i, j))]
    out_shape = [jax.ShapeDtypeStruct((M, N), out_dtype)]
    scratch = []
    if has_ag:
        in_specs.append(pl.BlockSpec(memory_space=pl.ANY))
        args.append(gather)
        out_specs.append(pl.BlockSpec(memory_space=pl.ANY))
        out_shape.append(jax.ShapeDtypeStruct((N_DEV,) + gather.shape, gather.dtype))
        scratch += _AG_SEMS
    if nk > 1:
        scratch.append(pltpu.VMEM((tm, tn), F32))
    sem = ("arbitrary",) * 3 if has_ag else ("parallel", "parallel", "arbitrary")
    res = pl.pallas_call(
        body, name=name, grid=(n_i, n_j, nk), in_specs=in_specs, out_specs=out_specs, out_shape=out_shape,
        scratch_shapes=scratch, compiler_params=_cp(*sem),
    )(*args)
    return tuple(res) if has_ag else res[0]


def _mm_ln(a, b, resid, gamma, beta, *, alpha, name, tm=512, tk=None):
    M, K = a.shape
    D = b.shape[1]
    tm = min(tm, M)
    tk = K if tk is None else tk
    nk = K // tk

    def body(a_ref, b_ref, r_ref, g_ref, be_ref, y_ref, xh_ref, rs_ref, *scratch):
        part = jnp.dot(a_ref[...].astype(BF16), b_ref[...].astype(BF16), preferred_element_type=F32)

        def finish(acc):
            z = alpha * r_ref[...] + acc
            mu = jnp.mean(z, axis=-1, keepdims=True)
            zc = z - mu
            var = jnp.mean(zc * zc, axis=-1, keepdims=True)
            rstd = lax.rsqrt(var + LN_EPS)
            xhat = zc * rstd
            y_ref[...] = xhat * g_ref[...] + be_ref[...]
            xh_ref[...] = xhat.astype(BF16)
            rs_ref[...] = rstd

        if nk == 1:
            finish(part)
        else:
            acc_ref = scratch[0]
            k = pl.program_id(1)

            @pl.when(k == 0)
            def _():
                acc_ref[...] = part

            @pl.when(k > 0)
            def _():
                acc_ref[...] += part

            @pl.when(k == nk - 1)
            def _():
                finish(acc_ref[...])

    row = lambda i, k: (i, 0)
    vec = lambda i, k: (0, 0)
    return pl.pallas_call(
        body, name=name, grid=(M // tm, nk),
        in_specs=[pl.BlockSpec((tm, tk), lambda i, k: (i, k)), pl.BlockSpec((tk, D), lambda i, k: (k, 0)),
                  pl.BlockSpec((tm, D), row), pl.BlockSpec((1, D), vec), pl.BlockSpec((1, D), vec)],
        out_specs=[pl.BlockSpec((tm, D), row), pl.BlockSpec((tm, D), row), pl.BlockSpec((tm, 1), row)],
        out_shape=(jax.ShapeDtypeStruct((M, D), F32), jax.ShapeDtypeStruct((M, D), BF16),
                   jax.ShapeDtypeStruct((M, 1), F32)),
        scratch_shapes=[pltpu.VMEM((tm, D), F32)] if nk > 1 else [],
        compiler_params=_cp("parallel", "arbitrary"),
    )(a, b, resid, gamma, beta)


def _mm_tn(a, b, *, name, tka, tn, a_off=0, na=1, b_off=0, nb=1, ts=512):
    S = a.shape[0]
    ts = min(ts, S)

    def body(a_ref, b_ref, o_ref):
        s = pl.program_id(2)
        part = lax.dot_general(a_ref[...].astype(BF16), b_ref[...].astype(BF16),
                               (((0,), (0,)), ((), ())), preferred_element_type=F32)

        @pl.when(s == 0)
        def _():
            o_ref[...] = part

        @pl.when(s > 0)
        def _():
            o_ref[...] += part

    return pl.pallas_call(
        body, name=name, grid=(na, nb, S // ts),
        in_specs=[pl.BlockSpec((ts, tka), lambda i, j, s: (s, a_off + i)),
                  pl.BlockSpec((ts, tn), lambda i, j, s: (s, b_off + j))],
        out_specs=pl.BlockSpec((tka, tn), lambda i, j, s: (i, j)),
        out_shape=jax.ShapeDtypeStruct((na * tka, nb * tn), F32),
        compiler_params=_cp("parallel", "parallel", "arbitrary"),
    )(a, b)


def _rope_tables(pos, inv_lane, sign_lane, name, ts=512):
    S = pos.shape[0]
    ts = min(ts, S)

    def body(p_ref, inv_ref, sg_ref, cos_ref, sin_ref):
        ang = p_ref[...].astype(F32) * inv_ref[...]
        cos_ref[...] = jnp.cos(ang)
        sin_ref[...] = jnp.sin(ang) * sg_ref[...]

    return pl.pallas_call(
        body, name=name, grid=(S // ts,),
        in_specs=[pl.BlockSpec((ts, 1), lambda i: (i, 0)), pl.BlockSpec((1, 128), lambda i: (0, 0)),
                  pl.BlockSpec((1, 128), lambda i: (0, 0))],
        out_specs=[pl.BlockSpec((ts, 128), lambda i: (i, 0))] * 2,
        out_shape=(jax.ShapeDtypeStruct((S, 128), F32),) * 2,
        compiler_params=_cp("parallel"),
    )(pos, inv_lane, sign_lane)


def _rope_swap(t):
    lane = lax.broadcasted_iota(jnp.int32, (1, 128), 1)
    lo = (lane % HEAD_DIM) < (ROT_DIM // 2)
    return jnp.where(lo, pltpu.roll(t, 128 - ROT_DIM // 2, 1), pltpu.roll(t, ROT_DIM // 2, 1))


def _rope_fwd(t, cos, sin):
    return t * cos + _rope_swap(t) * sin


def _rope_bwd(d, cos, sin):
    lane = lax.broadcasted_iota(jnp.int32, (1, 128), 1)
    return d * cos + jnp.where((lane % HEAD_DIM) < ROT_DIM, _rope_swap(d * sin), 0.0)


def _tile_heads(t):
    lane = lax.broadcasted_iota(jnp.int32, (1, 128), 1)
    r = pltpu.roll(t, 64, 1)
    h0 = jnp.where(lane < 64, t, r)
    h1 = jnp.where(lane < 64, r, t)
    return jnp.concatenate([h0, h0], axis=1), jnp.concatenate([h1, h1], axis=1)


def _fold_heads(d0, d1):
    lane = lax.broadcasted_iota(jnp.int32, (1, 128), 1)

    def fold(d):
        s = d[:, 0:128] + d[:, 128:256]
        return s + pltpu.roll(s, 64, 1)

    return jnp.where(lane < 64, fold(d0), fold(d1))


def _band4(n_keys):
    row = lax.broadcasted_iota(jnp.int32, (GROUP * WINDOW, n_keys), 0) % WINDOW
    col = lax.broadcasted_iota(jnp.int32, (GROUP * WINDOW, n_keys), 1)
    return (col > row) & (col <= row + WINDOW), col


def _head_masks():
    lane = lax.broadcasted_iota(jnp.int32, (1, GROUP * HEAD_DIM), 1)
    return [(lane // HEAD_DIM) == hl for hl in range(GROUP)]


def _stack_heads(t):
    zero = jnp.zeros_like(t)
    return jnp.concatenate([jnp.where(hm, t, zero) for hm in _head_masks()], axis=0)


def _unstack_heads(t4):
    out = None
    for hl, hm in enumerate(_head_masks()):
        part = jnp.where(hm, t4[hl * WINDOW:(hl + 1) * WINDOW], 0.0)
        out = part if out is None else out + part
    return out


def _sink_block(sink_ref, g):
    return jnp.concatenate([jnp.broadcast_to(sink_ref[g * GROUP + hl:g * GROUP + hl + 1, 0:1], (WINDOW, 256))
                            for hl in range(GROUP)], axis=0)


def _sink_column(sink_ref, g):
    return jnp.concatenate([jnp.broadcast_to(sink_ref[g * GROUP + hl:g * GROUP + hl + 1, 0:1], (WINDOW, 1))
                            for hl in range(GROUP)], axis=0)


def _attn_fwd(pq, cos_t, sin_t, sinks_b, *, name, ts=256):
    S = pq.shape[0]
    ts = min(ts, S)
    nq = ts // WINDOW
    scale = HEAD_DIM ** -0.5

    def body(cur_ref, prev_ref, cosc_ref, sinc_ref, cosp_ref, sinp_ref, sink_ref, o_ref, lse_ref):
        i = pl.program_id(0)
        cosc, sinc = cosc_ref[...], sinc_ref[...]
        q = cur_ref[:, 0:512].astype(F32)
        qr = jnp.concatenate(
            [_rope_fwd(q[:, j * 128:(j + 1) * 128], cosc, sinc) for j in range(4)], axis=1) * scale
        qr = qr.astype(BF16)
        kc = _rope_fwd(cur_ref[:, 512:640].astype(F32), cosc, sinc)
        kp = _rope_fwd(prev_ref[:, 0:128].astype(F32), cosp_ref[...], sinp_ref[...])
        k_all = jnp.concatenate([kp, kc], axis=0)
        v_all = jnp.concatenate([prev_ref[:, 128:256].astype(F32), cur_ref[:, 640:768].astype(F32)], axis=0)
        kt = [t.astype(BF16) for t in _tile_heads(k_all)]
        vt = [t.astype(BF16) for t in _tile_heads(v_all)]
        band, col = _band4(2 * WINDOW)
        ones = jnp.ones((2 * WINDOW, 256), BF16)
        key_t = lax.broadcasted_iota(jnp.int32, (2 * WINDOW, GROUP * WINDOW), 0)
        qry_t = lax.broadcasted_iota(jnp.int32, (2 * WINDOW, GROUP * WINDOW), 1) % WINDOW
        band_t = (key_t > qry_t) & (key_t <= qry_t + WINDOW)
        NT = (((1,), (1,)), ((), ()))
        for qb in range(nq):
            rows = slice(qb * WINDOW, (qb + 1) * WINDOW)
            keys = slice(qb * WINDOW, (qb + 2) * WINDOW)
            valid = band & ((col >= WINDOW) | (i * nq + qb > 0))
            valid_t = band_t & ((key_t >= WINDOW) | (i * nq + qb > 0))
            for g in range(2):
                qs = _stack_heads(qr[rows, g * 256:(g + 1) * 256])
                sink = _sink_block(sink_ref, g)
                s = lax.dot_general(qs, kt[g][keys], NT, preferred_element_type=F32)
                s_t = lax.dot_general(kt[g][keys], qs, NT, preferred_element_type=F32)
                m_t = jnp.max(jnp.where(valid_t, s_t, MASK_VALUE), axis=0, keepdims=True)
                m_rep = jnp.broadcast_to(m_t, (WINDOW, GROUP * WINDOW)).T
                m = jnp.maximum(jnp.concatenate([m_rep, m_rep], axis=1), sink)
                e = jnp.exp(jnp.where(valid, s, MASK_VALUE) - m).astype(BF16)
                l = jnp.dot(e, ones, preferred_element_type=F32) + jnp.exp(sink - m)
                pv = jnp.dot(e, vt[g][keys], preferred_element_type=F32)
                o_ref[rows, g * 256:(g + 1) * 256] = (_unstack_heads(pv) / _unstack_heads(l)).astype(BF16)
                lse4 = (m + jnp.log(l))[:, 0:1]
                for hl in range(GROUP):
                    h = g * GROUP + hl
                    lse_ref[rows, h:h + 1] = lse4[hl * WINDOW:(hl + 1) * WINDOW]

    hb = ts // WINDOW
    cur = lambda i: (i, 0)
    prev = lambda i: (jnp.maximum(i * hb - 1, 0), 0)
    return pl.pallas_call(
        body, name=name, grid=(S // ts,),
        in_specs=[pl.BlockSpec((ts, 768), cur),
                  pl.BlockSpec((WINDOW, 256), lambda i: (jnp.maximum(i * hb - 1, 0), 2)),
                  pl.BlockSpec((ts, 128), cur), pl.BlockSpec((ts, 128), cur),
                  pl.BlockSpec((WINDOW, 128), prev), pl.BlockSpec((WINDOW, 128), prev),
                  pl.BlockSpec((8, 128), lambda i: (0, 0))],
        out_specs=[pl.BlockSpec((ts, 512), cur), pl.BlockSpec((ts, 8), cur)],
        out_shape=(jax.ShapeDtypeStruct((S, 512), BF16), jax.ShapeDtypeStruct((S, 8), F32)),
        compiler_params=_cp("parallel"),
    )(pq, pq, cos_t, sin_t, cos_t, sin_t, sinks_b)


def _attn_bwd(pq, cos_t, sin_t, sinks_b, do, o, lse, *, name, ts=256):
    S = pq.shape[0]
    ts = min(ts, S)
    nq = ts // WINDOW
    nt = S // ts
    scale = HEAD_DIM ** -0.5
    NT = (((1,), (1,)), ((), ()))
    TN = (((0,), (0,)), ((), ()))

    def body(cur_ref, prev_ref, nxt_ref, cosc_ref, sinc_ref, cosp_ref, sinp_ref, cosn_ref, sinn_ref, sink_ref,
             doc_ref, don_ref, oc_ref, on_ref, lsec_ref, lsen_ref, dpq_ref, dsink_ref):
        i = pl.program_id(0)
        last = i == nt - 1
        cosc, sinc = cosc_ref[...], sinc_ref[...]
        cose = jnp.concatenate([cosc, cosn_ref[...]], axis=0)
        sine = jnp.concatenate([sinc, sinn_ref[...]], axis=0)
        q = jnp.concatenate([cur_ref[:, 0:512], nxt_ref[:, 0:512]], axis=0).astype(F32)
        qr = jnp.concatenate(
            [_rope_fwd(q[:, j * 128:(j + 1) * 128], cose, sine) for j in range(4)], axis=1) * scale
        qr = qr.astype(BF16)
        kc = _rope_fwd(cur_ref[:, 512:640].astype(F32), cosc, sinc)
        kp = _rope_fwd(prev_ref[:, 0:128].astype(F32), cosp_ref[...], sinp_ref[...])
        k_all = jnp.concatenate([kp, kc], axis=0)
        v_all = jnp.concatenate([prev_ref[:, 128:256].astype(F32), cur_ref[:, 640:768].astype(F32)], axis=0)
        kt = [t.astype(BF16) for t in _tile_heads(k_all)]
        vt = [t.astype(BF16) for t in _tile_heads(v_all)]
        don = jnp.where(last, jnp.zeros_like(don_ref[...]), don_ref[...])
        do_e = jnp.concatenate([doc_ref[...], don], axis=0)
        o_e = jnp.concatenate([oc_ref[...], on_ref[...]], axis=0)
        band2, col2 = _band4(2 * WINDOW)
        band1, _ = _band4(WINDOW)
        ones = jnp.ones((256, 256), BF16)

        @pl.when(i == 0)
        def _():
            dsink_ref[...] = jnp.zeros_like(dsink_ref)

        dk_acc = [[None] * (nq + 1) for _ in range(2)]
        dv_acc = [[None] * (nq + 1) for _ in range(2)]

        def add(acc, g, e, val):
            acc[g][e] = val if acc[g][e] is None else acc[g][e] + val

        for qb in range(nq + 1):
            halo = qb == nq
            rows = slice(qb * WINDOW, (qb + 1) * WINDOW)
            if halo:
                keys = slice(qb * WINDOW, (qb + 1) * WINDOW)
                valid = band1 & jnp.logical_not(last)
            else:
                keys = slice(qb * WINDOW, (qb + 2) * WINDOW)
                valid = band2 & ((col2 >= WINDOW) | (i * nq + qb > 0))
            dq_parts = []
            for g in range(2):
                qs = _stack_heads(qr[rows, g * 256:(g + 1) * 256])
                dos = _stack_heads(do_e[rows, g * 256:(g + 1) * 256])
                o_g = o_e[rows, g * 256:(g + 1) * 256].astype(F32)
                kt_b, vt_b = kt[g][keys], vt[g][keys]
                lse_src = lsen_ref if halo else lsec_ref
                lse_rows = slice(0, WINDOW) if halo else rows
                big_l = jnp.concatenate([lse_src[lse_rows, g * GROUP + hl:g * GROUP + hl + 1] for hl in range(GROUP)],
                                        axis=0)
                delta = jnp.dot((dos.astype(F32) * jnp.concatenate([o_g] * GROUP, axis=0)).astype(BF16), ones,
                                preferred_element_type=F32)
                s = lax.dot_general(qs, kt_b, NT, preferred_element_type=F32)
                p = jnp.exp(jnp.where(valid, s, MASK_VALUE) - big_l)
                dp = lax.dot_general(dos, vt_b, NT, preferred_element_type=F32)
                ds = (p * (dp - delta[:, 0:p.shape[1]])).astype(BF16)
                dk_g = lax.dot_general(ds, qs, TN, preferred_element_type=F32)
                dv_g = lax.dot_general(p.astype(BF16), dos, TN, preferred_element_type=F32)
                if not halo:
                    dq_parts.append(_unstack_heads(jnp.dot(ds, kt_b, preferred_element_type=F32)))
                    dsink4 = jnp.exp(_sink_column(sink_ref, g) - big_l) * delta[:, 0:1]
                    for hl in range(GROUP):
                        h = g * GROUP + hl
                        dsink_h = -jnp.sum(dsink4[hl * WINDOW:(hl + 1) * WINDOW], axis=0, keepdims=True)
                        dsink_ref[h:h + 1, :] += jnp.broadcast_to(dsink_h, (1, 128))
                add(dk_acc, g, qb, dk_g[0:WINDOW])
                add(dv_acc, g, qb, dv_g[0:WINDOW])
                if not halo:
                    add(dk_acc, g, qb + 1, dk_g[WINDOW:2 * WINDOW])
                    add(dv_acc, g, qb + 1, dv_g[WINDOW:2 * WINDOW])
            if not halo:
                cs, sn = cosc[rows], sinc[rows]
                for g in range(2):
                    dq_g = dq_parts[g] * scale
                    for j in range(2):
                        c0 = g * 256 + j * 128
                        dpq_ref[rows, c0:c0 + 128] = _rope_bwd(dq_g[:, j * 128:(j + 1) * 128], cs, sn).astype(BF16)
        for e in range(1, nq + 1):
            rows = slice((e - 1) * WINDOW, e * WINDOW)
            dk = _fold_heads(dk_acc[0][e], dk_acc[1][e])
            dv = _fold_heads(dv_acc[0][e], dv_acc[1][e])
            dpq_ref[rows, 512:640] = _rope_bwd(dk, cosc[rows], sinc[rows]).astype(BF16)
            dpq_ref[rows, 640:768] = dv.astype(BF16)

    hb = ts // WINDOW
    nblk = S // WINDOW
    cur = lambda i: (i, 0)
    prev = lambda i: (jnp.maximum(i * hb - 1, 0), 0)
    nxt = lambda i: (jnp.minimum((i + 1) * hb, nblk - 1), 0)
    return pl.pallas_call(
        body, name=name, grid=(nt,),
        in_specs=[pl.BlockSpec((ts, 768), cur),
                  pl.BlockSpec((WINDOW, 256), lambda i: (jnp.maximum(i * hb - 1, 0), 2)),
                  pl.BlockSpec((WINDOW, 768), nxt),
                  pl.BlockSpec((ts, 128), cur), pl.BlockSpec((ts, 128), cur),
                  pl.BlockSpec((WINDOW, 128), prev), pl.BlockSpec((WINDOW, 128), prev),
                  pl.BlockSpec((WINDOW, 128), nxt), pl.BlockSpec((WINDOW, 128), nxt),
                  pl.BlockSpec((8, 128), lambda i: (0, 0)),
                  pl.BlockSpec((ts, 512), cur), pl.BlockSpec((WINDOW, 512), nxt),
                  pl.BlockSpec((ts, 512), cur), pl.BlockSpec((WINDOW, 512), nxt),
                  pl.BlockSpec((ts, 8), cur), pl.BlockSpec((WINDOW, 8), nxt)],
        out_specs=[pl.BlockSpec((ts, 768), cur), pl.BlockSpec((8, 128), lambda i: (0, 0))],
        out_shape=(jax.ShapeDtypeStruct((S, 768), BF16), jax.ShapeDtypeStruct((8, 128), F32)),
        compiler_params=_cp("arbitrary"),
    )(pq, pq, pq, cos_t, sin_t, cos_t, sin_t, cos_t, sin_t, sinks_b, do, do, o, o, lse, lse)


def _shift_dn(x, k):
    return pltpu.roll(x, k, 0)


def _shift_up(x, k):
    return pltpu.roll(x, x.shape[0] - k, 0)


def _pool_lane_select(vals):
    lane = lax.broadcasted_iota(jnp.int32, (1, 256), 1)
    out = vals[3]
    for g in (2, 1, 0):
        out = jnp.where(lane < 64 * (g + 1), vals[g], out)
    return out


def _pool_inv_count(t0, n):
    t = t0 + lax.broadcasted_iota(jnp.int32, (n, 256), 0)
    lane = lax.broadcasted_iota(jnp.int32, (n, 256), 1)
    w = jnp.where(lane < 64, 2, jnp.where(lane < 128, 4, jnp.where(lane < 192, 8, 16)))
    return 1.0 / jnp.minimum(t + 1, w).astype(F32)


def _pooled(u_ext, t0, n):
    s2 = u_ext + _shift_dn(u_ext, 1)
    s4 = s2 + _shift_dn(s2, 2)
    s8 = s4 + _shift_dn(s4, 4)
    s16 = s8 + _shift_dn(s8, 8)
    win = _pool_lane_select([s2, s4, s8, s16])[HALO:HALO + n]
    return win * _pool_inv_count(t0, n) - u_ext[HALO:HALO + n]


def _poolconv_fwd(pp, wbd, pool_scale, conv_w, *, name, ts=512):
    S = pp.shape[0]
    ts = min(ts, S)

    def body(cur_ref, prev_ref, wbd_ref, sc_ref, cw_ref, oa_ref, oc_ref):
        i = pl.program_id(0)
        prev = jnp.where(i > 0, prev_ref[...].astype(F32), 0.0)
        u_ext = jnp.concatenate([prev[:, 0:256], cur_ref[:, 0:256].astype(F32)], axis=0)
        pooled = _pooled(u_ext, i * ts, ts)
        mixed = jnp.dot(pooled.astype(BF16), wbd_ref[...], preferred_element_type=F32)
        oa_ref[...] = (mixed * sc_ref[...]).astype(BF16)
        v_ext = jnp.concatenate([prev[:, 256:512] * prev[:, 768:1024],
                                 cur_ref[:, 256:512].astype(F32) * cur_ref[:, 768:1024].astype(F32)], axis=0)
        cv = cw_ref[2:3, :] * v_ext + cw_ref[1:2, :] * _shift_dn(v_ext, 1) + cw_ref[0:1, :] * _shift_dn(v_ext, 2)
        oc_ref[...] = (cur_ref[:, 512:768].astype(F32) * cv[HALO:HALO + ts]).astype(BF16)

    hb = ts // HALO
    cur = lambda i: (i, 0)
    const = lambda i: (0, 0)
    return pl.pallas_call(
        body, name=name, grid=(S // ts,),
        in_specs=[pl.BlockSpec((ts, 1024), cur),
                  pl.BlockSpec((HALO, 1024), lambda i: (jnp.maximum(i * hb - 1, 0), 0)),
                  pl.BlockSpec((256, 256), const), pl.BlockSpec((1, 256), const), pl.BlockSpec((3, 256), const)],
        out_specs=[pl.BlockSpec((ts, 256), cur)] * 2,
        out_shape=(jax.ShapeDtypeStruct((S, 256), BF16),) * 2,
        compiler_params=_cp("parallel"),
    )(pp, pp, wbd, pool_scale, conv_w)


def _poolconv_bwd(pp, do_a, do_c, wbd, wbd_t, pool_scale, conv_w, *, name, ts=512):
    S = pp.shape[0]
    ts = min(ts, S)
    nt = S // ts
    n_e = ts + 2 * HALO

    def body(cur_ref, prev_ref, nxt_ref, dac_ref, dan_ref, dcc_ref, dcn_ref, wbd_ref, wbdt_ref, sc_ref, cw_ref,
             dpp_ref, pooled_ref, dmixed_ref, dsc_ref, dcw_ref):
        i = pl.program_id(0)

        @pl.when(i == 0)
        def _():
            dsc_ref[...] = jnp.zeros_like(dsc_ref)
            dcw_ref[...] = jnp.zeros_like(dcw_ref)

        prev = jnp.where(i > 0, prev_ref[...].astype(F32), 0.0)
        nxt = nxt_ref[...].astype(F32)
        cur = cur_ref[...].astype(F32)
        not_last = i < nt - 1
        da_n = jnp.where(not_last, dan_ref[...].astype(F32), 0.0)
        dc_n = jnp.where(not_last, dcn_ref[...].astype(F32), 0.0)
        zeros_h = jnp.zeros((HALO, 256), F32)
        sc = sc_ref[...]

        u_ext = jnp.concatenate([prev[:, 0:256], cur[:, 0:256]], axis=0)
        pooled = _pooled(u_ext, i * ts, ts)
        pooled_b = pooled.astype(BF16)
        pooled_ref[...] = pooled_b
        mixed = jnp.dot(pooled_b, wbd_ref[...], preferred_element_type=F32)
        da_c = dac_ref[...].astype(F32)
        dsc_ref[...] += jnp.sum(da_c * mixed, axis=0, keepdims=True)
        dmixed_e = jnp.concatenate([da_c, da_n], axis=0) * sc
        dmixed_ref[...] = dmixed_e[0:ts].astype(BF16)
        dpooled = jnp.dot(dmixed_e.astype(BF16), wbdt_ref[...], preferred_element_type=F32)
        qd = dpooled * _pool_inv_count(i * ts, ts + HALO)
        f2 = qd + _shift_up(qd, 1)
        f4 = f2 + _shift_up(f2, 2)
        f8 = f4 + _shift_up(f4, 4)
        f16 = f8 + _shift_up(f8, 8)
        du = (_pool_lane_select([f2, f4, f8, f16]) - dpooled)[0:ts]
        dpp_ref[:, 0:256] = du.astype(BF16)

        xc_e = jnp.concatenate([prev[:, 256:512], cur[:, 256:512], nxt[:, 256:512]], axis=0)
        gc_e = jnp.concatenate([prev[:, 768:1024], cur[:, 768:1024], nxt[:, 768:1024]], axis=0)
        gb_e = jnp.concatenate([zeros_h, cur[:, 512:768], nxt[:, 512:768]], axis=0)
        dc_e = jnp.concatenate([zeros_h, dcc_ref[...].astype(F32), dc_n], axis=0)
        v_e = xc_e * gc_e
        v1, v2 = _shift_dn(v_e, 1), _shift_dn(v_e, 2)
        w0, w1, w2 = cw_ref[0:1, :], cw_ref[1:2, :], cw_ref[2:3, :]
        cv = w2 * v_e + w1 * v1 + w0 * v2
        dcv = dc_e * gb_e
        dv = w2 * dcv + w1 * _shift_up(dcv, 1) + w0 * _shift_up(dcv, 2)
        tile = slice(HALO, HALO + ts)
        dpp_ref[:, 256:512] = (dv * gc_e)[tile].astype(BF16)
        dpp_ref[:, 512:768] = (dc_e * cv)[tile].astype(BF16)
        dpp_ref[:, 768:1024] = (dv * xc_e)[tile].astype(BF16)
        dcv_t = dcv[tile]
        dcw_ref[0:1, :] += jnp.sum(dcv_t * v2[tile], axis=0, keepdims=True)
        dcw_ref[1:2, :] += jnp.sum(dcv_t * v1[tile], axis=0, keepdims=True)
        dcw_ref[2:3, :] += jnp.sum(dcv_t * v_e[tile], axis=0, keepdims=True)

    hb = ts // HALO
    nblk = S // HALO
    cur = lambda i: (i, 0)
    const = lambda i: (0, 0)
    prev = lambda i: (jnp.maximum(i * hb - 1, 0), 0)
    nxt = lambda i: (jnp.minimum((i + 1) * hb, nblk - 1), 0)
    del n_e
    return pl.pallas_call(
        body, name=name, grid=(nt,),
        in_specs=[pl.BlockSpec((ts, 1024), cur), pl.BlockSpec((HALO, 1024), prev), pl.BlockSpec((HALO, 1024), nxt),
                  pl.BlockSpec((ts, 256), cur), pl.BlockSpec((HALO, 256), nxt),
                  pl.BlockSpec((ts, 256), cur), pl.BlockSpec((HALO, 256), nxt),
                  pl.BlockSpec((256, 256), const), pl.BlockSpec((256, 256), const),
                  pl.BlockSpec((1, 256), const), pl.BlockSpec((3, 256), const)],
        out_specs=[pl.BlockSpec((ts, 1024), cur), pl.BlockSpec((ts, 256), cur), pl.BlockSpec((ts, 256), cur),
                   pl.BlockSpec((1, 256), const), pl.BlockSpec((3, 256), const)],
        out_shape=(jax.ShapeDtypeStruct((S, 1024), BF16), jax.ShapeDtypeStruct((S, 256), BF16),
                   jax.ShapeDtypeStruct((S, 256), BF16), jax.ShapeDtypeStruct((1, 256), F32),
                   jax.ShapeDtypeStruct((3, 256), F32)),
        compiler_params=_cp("arbitrary"),
    )(pp, pp, pp, do_a, do_a, do_c, do_c, wbd, wbd_t, pool_scale, conv_w)


def _sigmoid(x):
    return 1.0 / (1.0 + jnp.exp(-x))


def _merge_fwd(o_a, o_b, o_c, glog, w_br, *, name, ts=512):
    S = o_a.shape[0]
    D = w_br.shape[1]
    ts = min(ts, S)

    def body(oa_ref, ob_ref, oc_ref, gl_ref, w_ref, m_ref):
        pa = jnp.dot(oa_ref[...], w_ref[0:256, :], preferred_element_type=F32)
        pb = jnp.dot(ob_ref[...], w_ref[256:768, :], preferred_element_type=F32)
        pc = jnp.dot(oc_ref[...], w_ref[768:1024, :], preferred_element_type=F32)
        m = _sigmoid(gl_ref[:, 0:D].astype(F32)) * pa
        m = m + _sigmoid(gl_ref[:, D:2 * D].astype(F32)) * pb
        m = m + _sigmoid(gl_ref[:, 2 * D:3 * D].astype(F32)) * pc
        m_ref[...] = m.astype(BF16)

    cur = lambda i: (i, 0)
    return pl.pallas_call(
        body, name=name, grid=(S // ts,),
        in_specs=[pl.BlockSpec((ts, 256), cur), pl.BlockSpec((ts, 512), cur), pl.BlockSpec((ts, 256), cur),
                  pl.BlockSpec((ts, 3 * D), cur), pl.BlockSpec((1024, D), lambda i: (0, 0))],
        out_specs=pl.BlockSpec((ts, D), cur),
        out_shape=jax.ShapeDtypeStruct((S, D), BF16),
        compiler_params=_cp("parallel"),
    )(o_a, o_b, o_c, glog, w_br)


def _merge_bwd(dm, o_a, o_b, o_c, glog, w_br, w_br_t, *, name, ts=256):
    S = o_a.shape[0]
    D = w_br.shape[1]
    ts = min(ts, S)

    def body(dm_ref, oa_ref, ob_ref, oc_ref, gl_ref, w_ref, wt_ref, dgl_ref, dp_ref, doa_ref, dob_ref, doc_ref):
        dmv = dm_ref[...].astype(F32)
        branches = ((oa_ref, 0, 256, doa_ref), (ob_ref, 256, 768, dob_ref), (oc_ref, 768, 1024, doc_ref))
        for b, (o_ref, r0, r1, do_ref) in enumerate(branches):
            prod = jnp.dot(o_ref[...], w_ref[r0:r1, :], preferred_element_type=F32)
            gate = _sigmoid(gl_ref[:, b * D:(b + 1) * D].astype(F32))
            dgl_ref[:, b * D:(b + 1) * D] = (dmv * prod * gate * (1.0 - gate)).astype(BF16)
            dprod = (dmv * gate).astype(BF16)
            dp_ref[:, b * D:(b + 1) * D] = dprod
            do_ref[...] = jnp.dot(dprod, wt_ref[:, r0:r1], preferred_element_type=F32).astype(BF16)

    cur = lambda i: (i, 0)
    const = lambda i: (0, 0)
    return pl.pallas_call(
        body, name=name, grid=(S // ts,),
        in_specs=[pl.BlockSpec((ts, D), cur), pl.BlockSpec((ts, 256), cur), pl.BlockSpec((ts, 512), cur),
                  pl.BlockSpec((ts, 256), cur), pl.BlockSpec((ts, 3 * D), cur),
                  pl.BlockSpec((1024, D), const), pl.BlockSpec((D, 1024), const)],
        out_specs=[pl.BlockSpec((ts, 3 * D), cur), pl.BlockSpec((ts, 3 * D), cur), pl.BlockSpec((ts, 256), cur),
                   pl.BlockSpec((ts, 512), cur), pl.BlockSpec((ts, 256), cur)],
        out_shape=(jax.ShapeDtypeStruct((S, 3 * D), BF16), jax.ShapeDtypeStruct((S, 3 * D), BF16),
                   jax.ShapeDtypeStruct((S, 256), BF16), jax.ShapeDtypeStruct((S, 512), BF16),
                   jax.ShapeDtypeStruct((S, 256), BF16)),
        compiler_params=_cp("parallel"),
    )(dm, o_a, o_b, o_c, glog, w_br, w_br_t)


def _ffn_act_fwd(up_pre, fcw, *, name, tc, ts=512):
    S, F2 = up_pre.shape
    ts = min(ts, S)
    nj = F2 // (2 * tc)

    def body(cur_ref, prev_ref, w_ref, h_ref):
        i = pl.program_id(1)
        prev = jnp.where(i > 0, prev_ref[...].astype(F32), 0.0)
        x = jnp.concatenate([prev, cur_ref[...].astype(F32)], axis=0)
        up = (w_ref[2:3, :] * x + w_ref[1:2, :] * _shift_dn(x, 1) + w_ref[0:1, :] * _shift_dn(x, 2))[HALO:HALO + ts]
        a, b = up[:, 0:tc], up[:, tc:2 * tc]
        h_ref[...] = (a * _sigmoid(a) * b).astype(BF16)

    hb = ts // HALO
    return pl.pallas_call(
        body, name=name, grid=(nj, S // ts),
        in_specs=[pl.BlockSpec((ts, 2 * tc), lambda j, i: (i, j)),
                  pl.BlockSpec((HALO, 2 * tc), lambda j, i: (jnp.maximum(i * hb - 1, 0), j)),
                  pl.BlockSpec((3, 2 * tc), lambda j, i: (0, j))],
        out_specs=pl.BlockSpec((ts, tc), lambda j, i: (i, j)),
        out_shape=jax.ShapeDtypeStruct((S, F2 // 2), BF16),
        compiler_params=_cp("parallel", "parallel"),
    )(up_pre, up_pre, fcw)


def _ffn_act_bwd(up_pre, dh, fcw, *, name, tc, ts=512, scatter=None):
    S, F2 = up_pre.shape
    ts = min(ts, S)
    nt = S // ts
    nj = F2 // (2 * tc)
    has_rs = scatter is not None

    def body(cur_ref, prev_ref, nxt_ref, dhc_ref, dhn_ref, w_ref, *rest):
        if has_rs:
            q_ref, dpre_ref, dw_ref, recv_ref, send_sems, recv_sems = rest
            rs_refs = (q_ref, recv_ref, send_sems, recv_sems)

            @pl.when((pl.program_id(0) == 0) & (pl.program_id(1) == 0))
            def _():
                _rs_chips_start(*rs_refs)
        else:
            dpre_ref, dw_ref = rest
        i = pl.program_id(1)

        @pl.when(i == 0)
        def _():
            dw_ref[...] = jnp.zeros_like(dw_ref)

        prev = jnp.where(i > 0, prev_ref[...].astype(F32), 0.0)
        x = jnp.concatenate([prev, cur_ref[...].astype(F32), nxt_ref[...].astype(F32)], axis=0)
        dh_n = jnp.where(i < nt - 1, dhn_ref[...].astype(F32), 0.0)
        dh_e = jnp.concatenate([jnp.zeros((HALO, tc), F32), dhc_ref[...].astype(F32), dh_n], axis=0)
        w0, w1, w2 = w_ref[0:1, :], w_ref[1:2, :], w_ref[2:3, :]
        x1, x2 = _shift_dn(x, 1), _shift_dn(x, 2)
        up = w2 * x + w1 * x1 + w0 * x2
        a, b = up[:, 0:tc], up[:, tc:2 * tc]
        sg = _sigmoid(a)
        da = dh_e * b * (sg * (1.0 + a * (1.0 - sg)))
        db = dh_e * (a * sg)
        dup = jnp.concatenate([da, db], axis=1)
        dpre = w2 * dup + w1 * _shift_up(dup, 1) + w0 * _shift_up(dup, 2)
        tile = slice(HALO, HALO + ts)
        dpre_ref[...] = dpre[tile].astype(BF16)
        dup_t = dup[tile]
        dw_ref[0:1, :] += jnp.sum(dup_t * x2[tile], axis=0, keepdims=True)
        dw_ref[1:2, :] += jnp.sum(dup_t * x1[tile], axis=0, keepdims=True)
        dw_ref[2:3, :] += jnp.sum(dup_t * x[tile], axis=0, keepdims=True)

        if has_rs:
            @pl.when((pl.program_id(0) == nj - 1) & (pl.program_id(1) == nt - 1))
            def _():
                _rs_chips_finish(*rs_refs)

    hb = ts // HALO
    nblk = S // HALO
    prev = lambda j, i: (jnp.maximum(i * hb - 1, 0), j)
    nxt = lambda j, i: (jnp.minimum((i + 1) * hb, nblk - 1), j)
    in_specs = [pl.BlockSpec((ts, 2 * tc), lambda j, i: (i, j)), pl.BlockSpec((HALO, 2 * tc), prev),
                pl.BlockSpec((HALO, 2 * tc), nxt),
                pl.BlockSpec((ts, tc), lambda j, i: (i, j)), pl.BlockSpec((HALO, tc), nxt),
                pl.BlockSpec((3, 2 * tc), lambda j, i: (0, j))]
    out_specs = [pl.BlockSpec((ts, 2 * tc), lambda j, i: (i, j)), pl.BlockSpec((3, 2 * tc), lambda j, i: (0, j))]
    out_shape = [jax.ShapeDtypeStruct((S, F2), BF16), jax.ShapeDtypeStruct((3, F2), F32)]
    args = [up_pre, up_pre, up_pre, dh, dh, fcw]
    if has_rs:
        in_specs.append(pl.BlockSpec(memory_space=pl.ANY))
        args.append(scatter)
        out_specs.append(pl.BlockSpec(memory_space=pl.ANY))
        out_shape.append(jax.ShapeDtypeStruct((3,) + scatter.shape[1:], scatter.dtype))
    return pl.pallas_call(
        body, name=name, grid=(nj, nt), in_specs=in_specs, out_specs=out_specs, out_shape=out_shape,
        scratch_shapes=_RS_SEMS if has_rs else [],
        compiler_params=_cp("arbitrary", "arbitrary") if has_rs else _cp("parallel", "arbitrary"),
    )(*args)


def _ln_bwd(dy, xhat, rstd, gamma, *, name, ts=512):
    S, D = dy.shape
    ts = min(ts, S)

    def body(dy_ref, xh_ref, rs_ref, g_ref, dz_ref, dg_ref, db_ref):
        @pl.when(pl.program_id(0) == 0)
        def _():
            dg_ref[...] = jnp.zeros_like(dg_ref)
            db_ref[...] = jnp.zeros_like(db_ref)

        dyv = dy_ref[...]
        xh = xh_ref[...].astype(F32)
        dyg = dyv * g_ref[...]
        c1 = jnp.mean(dyg, axis=-1, keepdims=True)
        c2 = jnp.mean(dyg * xh, axis=-1, keepdims=True)
        dz_ref[...] = rs_ref[...] * (dyg - c1 - xh * c2)
        dg_ref[...] += jnp.sum(dyv * xh, axis=0, keepdims=True)
        db_ref[...] += jnp.sum(dyv, axis=0, keepdims=True)

    cur = lambda i: (i, 0)
    const = lambda i: (0, 0)
    return pl.pallas_call(
        body, name=name, grid=(S // ts,),
        in_specs=[pl.BlockSpec((ts, D), cur), pl.BlockSpec((ts, D), cur), pl.BlockSpec((ts, 1), cur),
                  pl.BlockSpec((1, D), const)],
        out_specs=[pl.BlockSpec((ts, D), cur), pl.BlockSpec((1, D), const), pl.BlockSpec((1, D), const)],
        out_shape=(jax.ShapeDtypeStruct((S, D), F32), jax.ShapeDtypeStruct((1, D), F32),
                   jax.ShapeDtypeStruct((1, D), F32)),
        compiler_params=_cp("arbitrary"),
    )(dy, xhat, rstd, gamma)


def _loss_head(y, tgt, *, name, ts=512):
    S, D = y.shape
    ts = min(ts, S)

    def body(y_ref, t_ref, dy_ref, sq_ref):
        @pl.when(pl.program_id(0) == 0)
        def _():
            sq_ref[...] = jnp.zeros_like(sq_ref)

        e = y_ref[...] - t_ref[...]
        dy_ref[...] = e * (1.0 / D)
        sq_ref[...] += jnp.sum(e * e, axis=0, keepdims=True)

    cur = lambda i: (i, 0)
    return pl.pallas_call(
        body, name=name, grid=(S // ts,),
        in_specs=[pl.BlockSpec((ts, D), cur), pl.BlockSpec((ts, D), cur)],
        out_specs=[pl.BlockSpec((ts, D), cur), pl.BlockSpec((1, D), lambda i: (0, 0))],
        out_shape=(jax.ShapeDtypeStruct((S, D), F32), jax.ShapeDtypeStruct((1, D), F32)),
        compiler_params=_cp("arbitrary"),
    )(y, tgt)


def _adamw(w, g, m, v, *, name, tr=512):
    R, C = w.shape
    tr = _div_tile(R, tr)
    c1 = 1.0 - ADAM_B1 ** ADAM_STEP
    c2 = 1.0 - ADAM_B2 ** ADAM_STEP

    def body(w_ref, g_ref, m_ref, v_ref, d_ref, mo_ref, vo_ref):
        gv = g_ref[...]
        m2 = ADAM_B1 * m_ref[...] + (1.0 - ADAM_B1) * gv
        v2 = ADAM_B2 * v_ref[...] + (1.0 - ADAM_B2) * (gv * gv)
        m_hat = m2 / c1
        v_hat = v2 / c2
        d_ref[...] = -ADAM_LR * (m_hat / (jnp.sqrt(v_hat) + ADAM_EPS) + ADAM_WD * w_ref[...])
        mo_ref[...] = m2
        vo_ref[...] = v2

    spec = pl.BlockSpec((tr, C), lambda i: (i, 0))
    return pl.pallas_call(
        body, name=name, grid=(R // tr,),
        in_specs=[spec] * 4, out_specs=[spec] * 3,
        out_shape=(jax.ShapeDtypeStruct((R, C), F32),) * 3,
        compiler_params=_cp("parallel"),
    )(w, g, m, v)


def _interleave_cols(w, nj):
    lead, f2 = w.shape[:-1], w.shape[-1]
    tc = f2 // (2 * nj)
    w = w.reshape(lead + (2, nj, tc))
    return jnp.swapaxes(w, -3, -2).reshape(lead + (f2,))


def _deinterleave_cols(w, nj):
    lead, f2 = w.shape[:-1], w.shape[-1]
    tc = f2 // (2 * nj)
    w = w.reshape(lead + (nj, 2, tc))
    return jnp.swapaxes(w, -3, -2).reshape(lead + (f2,))


def _block_diag(w_pool):
    return jnp.concatenate([jnp.pad(w_pool[g], ((0, 0), (64 * g, 192 - 64 * g))) for g in range(4)], axis=0)


def _pad_rows(v, rows):
    return jnp.pad(v, (0, rows * LANES - v.shape[0])).reshape(rows, LANES)


def kernel(x, positions, w_in, w_pool, pool_scale, attn_sinks, conv_w, w_branch_a, w_branch_b, w_branch_c, w_o, ln1_g, ln1_b, w_up, ffn_conv_w, w_down, ln2_g, ln2_b, loss_target, m_w_in, m_w_pool, m_pool_scale, m_attn_sinks, m_conv_w, m_w_branch_a, m_w_branch_b, m_w_branch_c, m_w_o, m_ln1_g, m_ln1_b, m_w_up, m_ffn_conv_w, m_w_down, m_ln2_g, m_ln2_b, v_w_in, v_w_pool, v_pool_scale, v_attn_sinks, v_conv_w, v_w_branch_a, v_w_branch_b, v_w_branch_c, v_w_o, v_ln1_g, v_ln1_b, v_w_up, v_ffn_conv_w, v_w_down, v_ln2_g, v_ln2_b):
    L, D, in_shard = w_in.shape
    S = x.shape[1]
    IN = in_shard * N_DEV
    F2 = w_up.shape[2] * N_DEV
    F = F2 // 2
    assert D == 1024 and IN == 1792 + 3 * D and x.shape[0] == 1 and S % 512 == 0
    alpha = (2 * L) ** 0.25
    NJ = 2
    TC = F // NJ
    xs = x.reshape(S, D)
    tgt = loss_target.reshape(S, D)

    big = [w_in, w_branch_a, w_branch_b, w_branch_c, w_o, w_up, w_down]
    PART_A, PART_B = (0, 1, 2, 3, 4), (5, 6)
    rows_l = [a.size // L // LANES for a in big]
    offs_l = [sum(rows_l[:k]) for k in range(len(big) + 1)]

    def pack_part(l, part):
        return jnp.concatenate([big[k][l].reshape(-1, LANES).astype(BF16) for k in part], axis=0)

    n_cw, n_fw = conv_w.size, ffn_conv_w.size
    small_rows = -(-(n_cw + n_fw) // LANES)
    small = _pad_rows(jnp.concatenate([conv_w.reshape(-1), ffn_conv_w.reshape(-1)]), small_rows)
    gsmall = _all_gather(small, "ag_conv_weights").reshape(N_DEV, -1)
    conv_full = gsmall[:, :n_cw].reshape(N_DEV, L, 3, -1).transpose(1, 2, 0, 3).reshape(L, 3, 256)
    fcw_full = gsmall[:, n_cw:n_cw + n_fw].reshape(N_DEV, L, 3, -1).transpose(1, 2, 0, 3).reshape(L, 3, F2)
    fcw_full = _interleave_cols(fcw_full, NJ)

    def shard_of(g, part, k, shape):
        o = offs_l[k] - offs_l[part[0]]
        return g[:, o:o + rows_l[k], :].reshape((N_DEV,) + shape)

    def unpack_a(g):
        win = shard_of(g, PART_A, 0, (D, in_shard)).transpose(1, 0, 2).reshape(D, IN)
        wg = win[:, 1792:]
        wp = jnp.concatenate([win[:, 0:256], win[:, 1024:1792]], axis=1)
        wq = win[:, 256:1024]
        wa = shard_of(g, PART_A, 1, (256, D // N_DEV)).transpose(1, 0, 2).reshape(256, D)
        wb = shard_of(g, PART_A, 2, (512, D // N_DEV)).transpose(1, 0, 2).reshape(512, D)
        wc = shard_of(g, PART_A, 3, (256, D // N_DEV)).transpose(1, 0, 2).reshape(256, D)
        wbr = jnp.concatenate([wa, wb, wc], axis=0)
        wo = shard_of(g, PART_A, 4, (D // N_DEV, D)).reshape(D, D)
        return dict(wg=wg, wp=wp, wq=wq, wg_t=wg.T, wp_t=wp.T, wq_t=wq.T, wbr=wbr, wbr_t=wbr.T, wo=wo, wo_t=wo.T)

    def unpack_b(g):
        wup = _interleave_cols(shard_of(g, PART_B, 5, (D, F2 // N_DEV)).transpose(1, 0, 2).reshape(D, F2), NJ)
        wdn = shard_of(g, PART_B, 6, (F // N_DEV, D)).reshape(F, D)
        return dict(wup=wup, wup_t=wup.T, wdn=wdn, wdn_t=wdn.T)

    def local_weights(l):
        wbd = _block_diag(w_pool[l]).astype(BF16)
        return dict(wbd=wbd, wbd_t=wbd.T, scale=pool_scale[l].reshape(1, 256), conv=conv_full[l],
                    fcw=fcw_full[l], sinks=jnp.broadcast_to(attn_sinks[l].reshape(8, 1), (8, 128)),
                    g1=ln1_g[l].reshape(1, D), b1=ln1_b[l].reshape(1, D),
                    g2=ln2_g[l].reshape(1, D), b2=ln2_b[l].reshape(1, D))

    inv_freq = ROPE_THETA ** (-jnp.arange(0, ROT_DIM, 2, dtype=F32) / ROT_DIM)
    head_lane = jnp.concatenate([inv_freq, inv_freq, jnp.zeros((HEAD_DIM - ROT_DIM,), F32)])
    head_sign = jnp.concatenate([-jnp.ones((8,), F32), jnp.ones((8,), F32), jnp.zeros((HEAD_DIM - ROT_DIM,), F32)])
    inv_lane = jnp.tile(head_lane, 2).reshape(1, 128)
    sign_lane = jnp.tile(head_sign, 2).reshape(1, 128)
    cos_t, sin_t = _rope_tables(positions.reshape(S, 1), inv_lane, sign_lane, "rope_tables")

    saved, W = [], []
    h_in = xs
    gathered_a = _all_gather(pack_part(0, PART_A), "ag_weights_first")
    for l in range(L):
        w = {**unpack_a(gathered_a), **local_weights(l)}
        pg, gathered_b = _mm(h_in, w["wg"], out_dtype=BF16, name="proj_gate", tn=1024, gather=pack_part(l, PART_B))
        w.update(unpack_b(gathered_b))
        W.append(w)
        pp = _mm(h_in, w["wp"], out_dtype=BF16, name="proj_poolconv")
        pq = _mm(h_in, w["wq"], out_dtype=BF16, name="proj_qkv")
        o_a, o_c = _poolconv_fwd(pp, w["wbd"], w["scale"], w["conv"], name="poolconv_fwd")
        o_b, lse = _attn_fwd(pq, cos_t, sin_t, w["sinks"], name="attn_fwd")
        merged = _merge_fwd(o_a, o_b, o_c, pg, w["wbr"], name="merge_fwd")
        x1, xh1, rs1 = _mm_ln(merged, w["wo"], h_in, w["g1"], w["b1"], alpha=alpha, name="wo_ln1")
        if l + 1 < L:
            up_pre, gathered_a = _mm(x1, w["wup"], out_dtype=BF16, name="ffn_up", tn=2 * TC,
                                     gather=pack_part(l + 1, PART_A))
        else:
            up_pre = _mm(x1, w["wup"], out_dtype=BF16, name="ffn_up", tn=2 * TC)
        hact =_ffn_act_fwd(up_pre, w["fcw"], name="ffn_act_fwd", tc=TC, ts=256)
        x2, xh2, rs2 = _mm_ln(hact, w["wdn"], x1, w["g2"], w["b2"], alpha=alpha, name="down_ln2")
        saved.append(dict(x0=h_in, pg=pg, pp=pp, pq=pq, o_a=o_a, o_b=o_b, o_c=o_c, lse=lse, merged=merged,
                          x1=x1, xh1=xh1, rs1=rs1, up_pre=up_pre, hact=hact, xh2=xh2, rs2=rs2))
        h_in = x2

    dy, sq_lanes = _loss_head(h_in, tgt, name="loss_head")

    def pack_grads(g):
        col = lambda a, n: a.reshape(a.shape[0], N_DEV, n).transpose(1, 0, 2)
        row = lambda a, n: a.reshape(N_DEV, n, a.shape[1])
        parts = [col(g["w_in"], in_shard), col(g["a"], D // N_DEV), col(g["b"], D // N_DEV), col(g["c"], D // N_DEV),
                 row(g["w_o"], D // N_DEV), col(g["w_up"], F2 // N_DEV), row(g["w_down"], F // N_DEV)]
        return jnp.concatenate([p.reshape(N_DEV, -1, LANES).astype(BF16) for p in parts], axis=1)

    my_c = lax.axis_index("c").astype(jnp.int32).reshape(1)
    my_chip = (2 * lax.axis_index("x") + lax.axis_index("y")).astype(jnp.int32).reshape(1)
    gw = [None] * L
    pair_sum = [None] * L
    from_chips = [None] * L
    for l in reversed(range(L)):
        w, sv = W[l], saved[l]
        dz2, dg2, db2 = _ln_bwd(dy, sv["xh2"], sv["rs2"], w["g2"], name="ln2_bwd")
        dh = _mm(dz2, w["wdn_t"], out_dtype=BF16, name="down_bwd_x", tn=TC)
        dw_dn = _mm_tn(sv["hact"], dz2, name="down_bwd_w", tka=TC, na=NJ, tn=D)
        if l + 1 < L:
            dpre, dfcw, from_chips[l + 1] = _ffn_act_bwd(sv["up_pre"], dh, w["fcw"], name="ffn_act_bwd", tc=TC, ts=256,
                                                         scatter=pair_sum[l + 1])
        else:
            dpre, dfcw = _ffn_act_bwd(sv["up_pre"], dh, w["fcw"], name="ffn_act_bwd", tc=TC, ts=256)
        dx1 = _mm(dpre, w["wup_t"], out_dtype=F32, name="up_bwd_x", tk=2 * TC, add=dz2, add_scale=alpha)
        dw_up = _mm_tn(sv["x1"], dpre, name="up_bwd_w", tka=D, tn=2 * TC, nb=NJ)
        dz1, dg1, db1 = _ln_bwd(dx1, sv["xh1"], sv["rs1"], w["g1"], name="ln1_bwd")
        dmerged = _mm(dz1, w["wo_t"], out_dtype=BF16, name="wo_bwd_x")
        dw_o = _mm_tn(sv["merged"], dz1, name="wo_bwd_w", tka=D, tn=D)
        dpg, dprod, do_a, do_b, do_c = _merge_bwd(dmerged, sv["o_a"], sv["o_b"], sv["o_c"], sv["pg"],
                                                  w["wbr"], w["wbr_t"], name="merge_bwd")
        dw_a = _mm_tn(sv["o_a"], dprod, name="branch_a_bwd_w", tka=256, tn=D, b_off=0)
        dw_b = _mm_tn(sv["o_b"], dprod, name="branch_b_bwd_w", tka=512, tn=D, b_off=1)
        dw_c = _mm_tn(sv["o_c"], dprod, name="branch_c_bwd_w", tka=256, tn=D, b_off=2)
        dpq, dsink = _attn_bwd(sv["pq"], cos_t, sin_t, w["sinks"], do_b, sv["o_b"], sv["lse"], name="attn_bwd")
        dpp, pooled, dmixed, dscale, dconv = _poolconv_bwd(sv["pp"], do_a, do_c, w["wbd"], w["wbd_t"], w["scale"],
                                                           w["conv"], name="poolconv_bwd")
        dwbd = _mm_tn(pooled, dmixed, name="pool_bwd_w", tka=256, tn=256)
        dx = _mm(dpg, w["wg_t"], out_dtype=F32, name="proj_gate_bwd_x", tk=1024, add=dz1, add_scale=alpha)
        dx = _mm(dpp, w["wp_t"], out_dtype=F32, name="proj_poolconv_bwd_x", add=dx)
        dx = _mm(dpq, w["wq_t"], out_dtype=F32, name="proj_qkv_bwd_x", add=dx)
        dw_g = _mm_tn(sv["x0"], dpg, name="proj_gate_bwd_w", tka=D, tn=1024, nb=3)
        dw_p = _mm_tn(sv["x0"], dpp, name="proj_poolconv_bwd_w", tka=D, tn=1024)
        dw_q = _mm_tn(sv["x0"], dpq, name="proj_qkv_bwd_w", tka=D, tn=768)
        dw_in = jnp.concatenate([dw_p[:, 0:256], dw_q, dw_p[:, 256:1024], dw_g], axis=1)
        dw_pool = jnp.stack([dwbd[64 * g:64 * (g + 1), 64 * g:64 * (g + 1)] for g in range(4)])
        gw[l] = dict(w_in=dw_in, a=dw_a, b=dw_b, c=dw_c, w_o=dw_o, w_up=_deinterleave_cols(dw_up, NJ), w_down=dw_dn,
                     w_pool=dw_pool, scale=dscale, sinks=dsink[:, 0], conv=dconv, fcw=_deinterleave_cols(dfcw, NJ),
                     g1=dg1, b1=db1, g2=dg2, b2=db2)
        p_l = pack_grads(gw[l])
        pair_sum[l] = _sum_sibling(p_l, _rs_sibling(p_l, "rs_sibling"), my_c, "rs_sum_sibling")
        dy = dx
    grad_x = dy.reshape(1, S, D)
    from_chips[0] = _rs_chips(pair_sum[0], "rs_chips_last")
    g_layers = [_sum_chips(pair_sum[l], from_chips[l], my_chip, "rs_sum_chips") for l in range(L)]

    def stack(k):
        return jnp.stack([gw[l][k] for l in range(L)])

    rep_vec = jnp.concatenate([
        stack("w_pool").reshape(-1), stack("scale").reshape(-1), stack("g1").reshape(-1), stack("b1").reshape(-1),
        stack("g2").reshape(-1), stack("b2").reshape(-1)])
    n_rep_full = -(-rep_vec.shape[0] // LANES)
    sinks_row = jnp.pad(stack("sinks").reshape(-1), (0, LANES - 8 * L))
    rep_vec = jnp.concatenate([_pad_rows(rep_vec, n_rep_full).reshape(-1), sinks_row, sq_lanes.reshape(-1)])
    loss_row = n_rep_full + 1
    n_rep = -(-(loss_row + 1) // 8) * 8
    rep_rows = _pad_rows(rep_vec, n_rep)
    dconv_by_dev = stack("conv").reshape(L, 3, N_DEV, -1).transpose(2, 0, 1, 3).reshape(N_DEV, -1)
    dfcw_by_dev = stack("fcw").reshape(L, 3, N_DEV, -1).transpose(2, 0, 1, 3).reshape(N_DEV, -1)
    n_mine = -(-(small_rows) // 8) * 8
    by_dev = jnp.concatenate([dconv_by_dev, dfcw_by_dev], axis=1)
    by_dev = jnp.pad(by_dev, ((0, 0), (0, n_mine * LANES - by_dev.shape[1]))).reshape(N_DEV * n_mine, LANES)
    small_g = _all_gather(jnp.concatenate([rep_rows, by_dev], axis=0), "ag_small_grads")
    rep_sum, mine_sum, loss11 = _small_reduce(small_g, n_rep, n_mine, 1.0 / D, loss_row, "small_reduce")
    loss = loss11[0, 0]

    names_big = ["w_in", "w_branch_a", "w_branch_b", "w_branch_c", "w_o", "w_up", "w_down"]
    ms_big = [m_w_in, m_w_branch_a, m_w_branch_b, m_w_branch_c, m_w_o, m_w_up, m_w_down]
    vs_big = [v_w_in, v_w_branch_a, v_w_branch_b, v_w_branch_c, v_w_o, v_w_up, v_w_down]
    out = {}
    for k, name in enumerate(names_big):
        wk = big[k]
        c2 = wk.shape[-1]
        as2d = lambda a: a.reshape(-1, c2)
        g_nat = jnp.concatenate([g[offs_l[k]:offs_l[k + 1]] for g in g_layers], axis=0).reshape(wk.shape)
        d, mo, vo = _adamw(as2d(wk), as2d(g_nat), as2d(ms_big[k]), as2d(vs_big[k]), name="adamw_" + name)
        out[name] = (g_nat, d.reshape(wk.shape), mo.reshape(wk.shape), vo.reshape(wk.shape))

    def rep_pack(wp_, sc_, g1_, b1_, g2_, b2_, sk_):
        v = jnp.concatenate([wp_.reshape(-1), sc_.reshape(-1), g1_.reshape(-1), b1_.reshape(-1), g2_.reshape(-1),
                             b2_.reshape(-1)])
        return _pad_rows(jnp.concatenate([_pad_rows(v, n_rep_full).reshape(-1), sk_.reshape(-1)]), n_rep)

    def mine_pack(cw_, fw_):
        return _pad_rows(jnp.concatenate([cw_.reshape(-1), fw_.reshape(-1)]), n_mine)

    w_rep = rep_pack(w_pool, pool_scale, ln1_g, ln1_b, ln2_g, ln2_b, attn_sinks)
    m_rep = rep_pack(m_w_pool, m_pool_scale, m_ln1_g, m_ln1_b, m_ln2_g, m_ln2_b, m_attn_sinks)
    v_rep = rep_pack(v_w_pool, v_pool_scale, v_ln1_g, v_ln1_b, v_ln2_g, v_ln2_b, v_attn_sinks)
    g_rep = jnp.concatenate([rep_sum[:loss_row], jnp.zeros((n_rep - loss_row, LANES), F32)], axis=0)
    rep_res = (g_rep,) + tuple(_adamw(w_rep, g_rep, m_rep, v_rep, name="adamw_replicated"))
    w_mine = mine_pack(conv_w, ffn_conv_w)
    mine_res = (mine_sum,) + tuple(_adamw(w_mine, mine_sum, mine_pack(m_conv_w, m_ffn_conv_w),
                                          mine_pack(v_conv_w, v_ffn_conv_w), name="adamw_conv"))

    def rep_unpack(buf):
        flat = buf.reshape(-1)
        res, o = {}, 0
        for nm, ref in (("w_pool", w_pool), ("pool_scale", pool_scale), ("ln1_g", ln1_g), ("ln1_b", ln1_b),
                        ("ln2_g", ln2_g), ("ln2_b", ln2_b)):
            res[nm] = flat[o:o + ref.size].reshape(ref.shape)
            o += ref.size
        o = n_rep_full * LANES
        res["attn_sinks"] = flat[o:o + attn_sinks.size].reshape(attn_sinks.shape)
        return res

    def mine_unpack(buf):
        flat = buf.reshape(-1)
        return {"conv_w": flat[:n_cw].reshape(conv_w.shape),
                "ffn_conv_w": flat[n_cw:n_cw + n_fw].reshape(ffn_conv_w.shape)}

    order = ["w_in", "w_pool", "pool_scale", "attn_sinks", "conv_w", "w_branch_a", "w_branch_b", "w_branch_c", "w_o",
             "ln1_g", "ln1_b", "w_up", "ffn_conv_w", "w_down", "ln2_g", "ln2_b"]
    results = [loss, grad_x]
    for kind in range(4):
        rep_k, mine_k = rep_unpack(rep_res[kind]), mine_unpack(mine_res[kind])
        for nm in order:
            if nm in out:
                results.append(out[nm][kind])
            elif nm in rep_k:
                results.append(rep_k[nm])
            else:
                results.append(mine_k[nm])
    return tuple(results)
```

```python
import functools

import jax
import jax.numpy as jnp
from jax import lax
from jax.experimental import pallas as pl
from jax.experimental.pallas import tpu as pltpu

F32 = jnp.float32
BF16 = jnp.bfloat16

HEAD_DIM = 64
N_Q_HEADS = 8
GROUP = 4
WINDOW = 128
ROT_DIM = 16
ROPE_THETA = 500000.0
POOL_WINDOWS = (2, 4, 8, 16)
LN_EPS = 1e-5
MASK_VALUE = -1e30
ADAM_LR, ADAM_B1, ADAM_B2, ADAM_EPS, ADAM_WD, ADAM_STEP = 0.001, 0.9, 0.999, 1e-08, 0.01, 10

N_DEV = 8
LANES = 1024
HALO = 16
MESH = pl.DeviceIdType.MESH
VMEM_LIMIT = 56 * 1024 * 1024


def _div_tile(n, want, mult=8):
    for t in range(min(want, n) // mult * mult, 0, -mult):
        if n % t == 0:
            return t
    return n


def _cp(*sem):
    return pltpu.CompilerParams(dimension_semantics=sem, vmem_limit_bytes=VMEM_LIMIT)


def _coords():
    return lax.axis_index("x"), lax.axis_index("y"), lax.axis_index("c")


def _all_gather(xs, name):
    R, C = xs.shape

    def body(x_ref, out_ref, send_sems, recv_sems, local_sem):
        _ag_start(x_ref, out_ref, send_sems, recv_sems, local_sem)
        _ag_finish(x_ref, out_ref, send_sems, recv_sems, local_sem)

    return pl.pallas_call(
        body, name=name,
        out_shape=jax.ShapeDtypeStruct((N_DEV, R, C), xs.dtype),
        in_specs=[pl.BlockSpec(memory_space=pl.ANY)],
        out_specs=pl.BlockSpec(memory_space=pl.ANY),
        scratch_shapes=_AG_SEMS,
    )(xs)


_AG_SEMS = [pltpu.SemaphoreType.DMA((7,)), pltpu.SemaphoreType.DMA((7,)), pltpu.SemaphoreType.DMA(())]


def _ag_copies(x_ref, out_ref, send_sems, recv_sems, local_sem):
    x, y, c = _coords()
    me, sibling = (x, y, c), (x, y, 1 - c)
    chips = [(1 - x, y), (x, 1 - y), (1 - x, 1 - y)]

    def slot(px, py, pc):
        return out_ref.at[4 * px + 2 * py + pc]

    def copy(k, block, to, src=None):
        return pltpu.make_async_remote_copy(
            src_ref=slot(*block) if src is None else src, dst_ref=slot(*block),
            send_sem=send_sems.at[k], recv_sem=recv_sems.at[k], device_id=to, device_id_type=MESH)

    mine = pltpu.make_async_copy(x_ref, slot(*me), local_sem)
    first = [copy(0, me, sibling, src=x_ref)]
    first += [copy(1 + j, me, (*chip, c), src=x_ref) for j, chip in enumerate(chips)]
    passed = [copy(4 + j, (*chip, c), sibling) for j, chip in enumerate(chips)]
    from_chips = [copy(1 + j, (*chip, c), me) for j, chip in enumerate(chips)]
    from_sibling = [copy(0, sibling, me)] + [copy(4 + j, (*chip, 1 - c), me) for j, chip in enumerate(chips)]
    return mine, first, passed, from_chips, from_sibling


def _ag_start(*refs):
    mine, first, _, _, _ = _ag_copies(*refs)
    mine.start()
    for cp in first:
        cp.start()


def _ag_finish(*refs):
    mine, first, passed, from_chips, from_sibling = _ag_copies(*refs)
    for j in range(3):
        from_chips[j].wait_recv()
        passed[j].start()
    for cp in from_sibling:
        cp.wait_recv()
    for cp in first + passed:
        cp.wait_send()
    mine.wait()


def _rs_sibling(p, name):
    _, R, C = p.shape

    def body(p_ref, out_ref, send_sems, recv_sems):
        x, y, c = _coords()
        copies = []
        for j in range(4):
            cx, cy = j // 2, j % 2
            copies.append(pltpu.make_async_remote_copy(
                src_ref=p_ref.at[4 * cx + 2 * cy + (1 - c)], dst_ref=out_ref.at[j],
                send_sem=send_sems.at[j], recv_sem=recv_sems.at[j], device_id=(x, y, 1 - c), device_id_type=MESH))
        for cp in copies:
            cp.start()
        for cp in copies:
            cp.wait_recv()
        for cp in copies:
            cp.wait_send()

    return pl.pallas_call(
        body, name=name,
        out_shape=jax.ShapeDtypeStruct((4, R, C), p.dtype),
        in_specs=[pl.BlockSpec(memory_space=pl.ANY)],
        out_specs=pl.BlockSpec(memory_space=pl.ANY),
        scratch_shapes=[pltpu.SemaphoreType.DMA((4,)), pltpu.SemaphoreType.DMA((4,))],
    )(p)


def _rs_chips(q, name):
    _, R, C = q.shape

    def body(q_ref, out_ref, send_sems, recv_sems):
        _rs_chips_start(q_ref, out_ref, send_sems, recv_sems)
        _rs_chips_finish(q_ref, out_ref, send_sems, recv_sems)

    return pl.pallas_call(
        body, name=name,
        out_shape=jax.ShapeDtypeStruct((3, R, C), q.dtype),
        in_specs=[pl.BlockSpec(memory_space=pl.ANY)],
        out_specs=pl.BlockSpec(memory_space=pl.ANY),
        scratch_shapes=_RS_SEMS,
    )(q)


_RS_SEMS = [pltpu.SemaphoreType.DMA((3,)), pltpu.SemaphoreType.DMA((3,))]


def _rs_chips_copies(q_ref, out_ref, send_sems, recv_sems):
    x, y, c = _coords()
    chips = [(1 - x, y), (x, 1 - y), (1 - x, 1 - y)]
    return [pltpu.make_async_remote_copy(
        src_ref=q_ref.at[2 * cx + cy], dst_ref=out_ref.at[k],
        send_sem=send_sems.at[k], recv_sem=recv_sems.at[k], device_id=(cx, cy, c), device_id_type=MESH)
        for k, (cx, cy) in enumerate(chips)]


def _rs_chips_start(*refs):
    for cp in _rs_chips_copies(*refs):
        cp.start()


def _rs_chips_finish(*refs):
    copies = _rs_chips_copies(*refs)
    for cp in copies:
        cp.wait_recv()
    for cp in copies:
        cp.wait_send()


def _sum_sibling(p, recv, my_c, name, tr=512):
    _, R, C = p.shape
    tr = _div_tile(R, tr, 16)

    def body(c_ref, p_ref, r_ref, o_ref):
        o_ref[...] = (p_ref[...].astype(F32) + r_ref[...].astype(F32)).astype(o_ref.dtype)

    grid_spec = pltpu.PrefetchScalarGridSpec(
        num_scalar_prefetch=1, grid=(4, R // tr),
        in_specs=[pl.BlockSpec((1, tr, C), lambda j, r, c_ref: (4 * (j // 2) + 2 * (j % 2) + c_ref[0], r, 0)),
                  pl.BlockSpec((1, tr, C), lambda j, r, c_ref: (j, r, 0))],
        out_specs=pl.BlockSpec((1, tr, C), lambda j, r, c_ref: (j, r, 0)))
    return pl.pallas_call(body, name=name, grid_spec=grid_spec,
                          out_shape=jax.ShapeDtypeStruct((4, R, C), p.dtype),
                          compiler_params=_cp("parallel", "parallel"))(my_c, p, recv)


def _sum_chips(q, recv, my_chip, name, tr=512):
    _, R, C = q.shape
    tr = _div_tile(R, tr, 16)

    def body(i_ref, q_ref, r_ref, o_ref):
        acc = q_ref[0].astype(F32)
        for k in range(3):
            acc = acc + r_ref[k].astype(F32)
        o_ref[...] = acc

    grid_spec = pltpu.PrefetchScalarGridSpec(
        num_scalar_prefetch=1, grid=(R // tr,),
        in_specs=[pl.BlockSpec((1, tr, C), lambda r, i_ref: (i_ref[0], r, 0)),
                  pl.BlockSpec((3, tr, C), lambda r, i_ref: (0, r, 0))],
        out_specs=pl.BlockSpec((tr, C), lambda r, i_ref: (r, 0)))
    return pl.pallas_call(body, name=name, grid_spec=grid_spec,
                          out_shape=jax.ShapeDtypeStruct((R, C), F32),
                          compiler_params=_cp("parallel"))(my_chip, q, recv)


def _small_reduce(g, n_rep, n_mine, inv_d, loss_row, name):
    _, R, C = g.shape

    def body(g_ref, rep_ref, mine_ref, loss_ref):
        x, y, c = _coords()
        start = pl.multiple_of(n_rep + (4 * x + 2 * y + c) * n_mine, 8)
        rep = g_ref[0, 0:n_rep, :]
        mine = g_ref[0, pl.ds(start, n_mine), :]
        sq = g_ref[0, loss_row:loss_row + 1, :]
        for d in range(1, N_DEV):
            rep = rep + g_ref[d, 0:n_rep, :]
            mine = mine + g_ref[d, pl.ds(start, n_mine), :]
            sq = sq + g_ref[d, loss_row:loss_row + 1, :]
        rep_ref[...] = rep
        mine_ref[...] = mine
        loss_ref[...] = (0.5 * inv_d) * jnp.sum(sq, axis=1, keepdims=True)

    return pl.pallas_call(
        body, name=name,
        out_shape=(jax.ShapeDtypeStruct((n_rep, C), F32), jax.ShapeDtypeStruct((n_mine, C), F32),
                   jax.ShapeDtypeStruct((1, 1), F32)),
        compiler_params=pltpu.CompilerParams(vmem_limit_bytes=VMEM_LIMIT),
    )(g)


def _mm(a, b, *, out_dtype, name, tm=512, tn=None, tk=None, add=None, add_scale=1.0, gather=None):
    M, K = a.shape
    N = b.shape[1]
    tm = min(tm, M)
    tn = N if tn is None else tn
    tk = K if tk is None else tk
    nk = K // tk
    has_add = add is not None
    has_ag = gather is not None
    n_i, n_j = M // tm, N // tn

    def body(*refs):
        a_ref, b_ref = refs[0], refs[1]
        add_ref = refs[2] if has_add else None
        n_in = 2 + has_add + has_ag
        o_ref = refs[n_in]
        if has_ag:
            ag_refs = (refs[n_in - 1], refs[n_in + 1]) + tuple(refs[n_in + 2:n_in + 5])
            pid = (pl.program_id(0), pl.program_id(1), pl.program_id(2))

            @pl.when((pid[0] == 0) & (pid[1] == 0) & (pid[2] == 0))
            def _():
                _ag_start(*ag_refs)

        part = jnp.dot(a_ref[...].astype(BF16), b_ref[...].astype(BF16), preferred_element_type=F32)

        def finish(r):
            if has_add:
                r = r + add_scale * add_ref[...].astype(F32)
            o_ref[...] = r.astype(out_dtype)

        if nk == 1:
            finish(part)
        else:
            acc_ref = refs[-1]
            k = pl.program_id(2)

            @pl.when(k == 0)
            def _():
                acc_ref[...] = part

            @pl.when(k > 0)
            def _():
                acc_ref[...] += part

            @pl.when(k == nk - 1)
            def _():
                finish(acc_ref[...])

        if has_ag:
            @pl.when((pid[0] == n_i - 1) & (pid[1] == n_j - 1) & (pid[2] == nk - 1))
            def _():
                _ag_finish(*ag_refs)

    in_specs = [pl.BlockSpec((tm, tk), lambda i, j, k: (i, k)), pl.BlockSpec((tk, tn), lambda i, j, k: (k, j))]
    args = [a, b]
    if has_add:
        in_specs.append(pl.BlockSpec((tm, tn), lambda i, j, k: (i, j)))
        args.append(add)
    out_specs = [pl.BlockSpec((tm, tn), lambda i, j, k: (i, j))]
    out_shape = [jax.ShapeDtypeStruct((M, N), out_dtype)]
    scratch = []
    if has_ag:
        in_specs.append(pl.BlockSpec(memory_space=pl.ANY))
        args.append(gather)
        out_specs.append(pl.BlockSpec(memory_space=pl.ANY))
        out_shape.append(jax.ShapeDtypeStruct((N_DEV,) + gather.shape, gather.dtype))
        scratch += _AG_SEMS
    if nk > 1:
        scratch.append(pltpu.VMEM((tm, tn), F32))
    sem = ("arbitrary",) * 3 if has_ag else ("parallel", "parallel", "arbitrary")
    res = pl.pallas_call(
        body, name=name, grid=(n_i, n_j, nk), in_specs=in_specs, out_specs=out_specs, out_shape=out_shape,
        scratch_shapes=scratch, compiler_params=_cp(*sem),
    )(*args)
    return tuple(res) if has_ag else res[0]


def _mm_ln(a, b, resid, gamma, beta, *, alpha, name, tm=512, tk=None):
    M, K = a.shape
    D = b.shape[1]
    tm = min(tm, M)
    tk = K if tk is None else tk
    nk = K // tk

    def body(a_ref, b_ref, r_ref, g_ref, be_ref, y_ref, xh_ref, rs_ref, *scratch):
        part = jnp.dot(a_ref[...].astype(BF16), b_ref[...].astype(BF16), preferred_element_type=F32)

        def finish(acc):
            z = alpha * r_ref[...] + acc
            mu = jnp.mean(z, axis=-1, keepdims=True)
            zc = z - mu
            var = jnp.mean(zc * zc, axis=-1, keepdims=True)
            rstd = lax.rsqrt(var + LN_EPS)
            xhat = zc * rstd
            y_ref[...] = xhat * g_ref[...] + be_ref[...]
            xh_ref[...] = xhat.astype(BF16)
            rs_ref[...] = rstd

        if nk == 1:
            finish(part)
        else:
            acc_ref = scratch[0]
            k = pl.program_id(1)

            @pl.when(k == 0)
            def _():
                acc_ref[...] = part

            @pl.when(k > 0)
            def _():
                acc_ref[...] += part

            @pl.when(k == nk - 1)
            def _():
                finish(acc_ref[...])

    row = lambda i, k: (i, 0)
    vec = lambda i, k: (0, 0)
    return pl.pallas_call(
        body, name=name, grid=(M // tm, nk),
        in_specs=[pl.BlockSpec((tm, tk), lambda i, k: (i, k)), pl.BlockSpec((tk, D), lambda i, k: (k, 0)),
                  pl.BlockSpec((tm, D), row), pl.BlockSpec((1, D), vec), pl.BlockSpec((1, D), vec)],
        out_specs=[pl.BlockSpec((tm, D), row), pl.BlockSpec((tm, D), row), pl.BlockSpec((tm, 1), row)],
        out_shape=(jax.ShapeDtypeStruct((M, D), F32), jax.ShapeDtypeStruct((M, D), BF16),
                   jax.ShapeDtypeStruct((M, 1), F32)),
        scratch_shapes=[pltpu.VMEM((tm, D), F32)] if nk > 1 else [],
        compiler_params=_cp("parallel", "arbitrary"),
    )(a, b, resid, gamma, beta)


def _mm_tn(a, b, *, name, tka, tn, a_off=0, na=1, b_off=0, nb=1, ts=2048):
    S = a.shape[0]
    ts = min(ts, S)

    def body(a_ref, b_ref, o_ref):
        s = pl.program_id(2)
        part = lax.dot_general(a_ref[...].astype(BF16), b_ref[...].astype(BF16),
                               (((0,), (0,)), ((), ())), preferred_element_type=F32)

        @pl.when(s == 0)
        def _():
            o_ref[...] = part

        @pl.when(s > 0)
        def _():
            o_ref[...] += part

    return pl.pallas_call(
        body, name=name, grid=(na, nb, S // ts),
        in_specs=[pl.BlockSpec((ts, tka), lambda i, j, s: (s, a_off + i)),
                  pl.BlockSpec((ts, tn), lambda i, j, s: (s, b_off + j))],
        out_specs=pl.BlockSpec((tka, tn), lambda i, j, s: (i, j)),
        out_shape=jax.ShapeDtypeStruct((na * tka, nb * tn), F32),
        compiler_params=_cp("parallel", "parallel", "arbitrary"),
    )(a, b)


def _rope_tables(pos, inv_lane, sign_lane, name, ts=512):
    S = pos.shape[0]
    ts = min(ts, S)

    def body(p_ref, inv_ref, sg_ref, cos_ref, sin_ref):
        ang = p_ref[...].astype(F32) * inv_ref[...]
        cos_ref[...] = jnp.cos(ang)
        sin_ref[...] = jnp.sin(ang) * sg_ref[...]

    return pl.pallas_call(
        body, name=name, grid=(S // ts,),
        in_specs=[pl.BlockSpec((ts, 1), lambda i: (i, 0)), pl.BlockSpec((1, 128), lambda i: (0, 0)),
                  pl.BlockSpec((1, 128), lambda i: (0, 0))],
        out_specs=[pl.BlockSpec((ts, 128), lambda i: (i, 0))] * 2,
        out_shape=(jax.ShapeDtypeStruct((S, 128), F32),) * 2,
        compiler_params=_cp("parallel"),
    )(pos, inv_lane, sign_lane)


def _rope_swap(t):
    lane = lax.broadcasted_iota(jnp.int32, (1, 128), 1)
    lo = (lane % HEAD_DIM) < (ROT_DIM // 2)
    return jnp.where(lo, pltpu.roll(t, 128 - ROT_DIM // 2, 1), pltpu.roll(t, ROT_DIM // 2, 1))


def _rope_fwd(t, cos, sin):
    return t * cos + _rope_swap(t) * sin


def _rope_bwd(d, cos, sin):
    lane = lax.broadcasted_iota(jnp.int32, (1, 128), 1)
    return d * cos + jnp.where((lane % HEAD_DIM) < ROT_DIM, _rope_swap(d * sin), 0.0)


def _tile_heads(t):
    lane = lax.broadcasted_iota(jnp.int32, (1, 128), 1)
    r = pltpu.roll(t, 64, 1)
    h0 = jnp.where(lane < 64, t, r)
    h1 = jnp.where(lane < 64, r, t)
    return jnp.concatenate([h0, h0], axis=1), jnp.concatenate([h1, h1], axis=1)


def _fold_heads(d0, d1):
    lane = lax.broadcasted_iota(jnp.int32, (1, 128), 1)

    def fold(d):
        s = d[:, 0:128] + d[:, 128:256]
        return s + pltpu.roll(s, 64, 1)

    return jnp.where(lane < 64, fold(d0), fold(d1))


def _band4(n_keys):
    row = lax.broadcasted_iota(jnp.int32, (GROUP * WINDOW, n_keys), 0) % WINDOW
    col = lax.broadcasted_iota(jnp.int32, (GROUP * WINDOW, n_keys), 1)
    return (col > row) & (col <= row + WINDOW), col


def _head_masks():
    lane = lax.broadcasted_iota(jnp.int32, (1, GROUP * HEAD_DIM), 1)
    return [(lane // HEAD_DIM) == hl for hl in range(GROUP)]


def _stack_heads(t):
    zero = jnp.zeros_like(t)
    return jnp.concatenate([jnp.where(hm, t, zero) for hm in _head_masks()], axis=0)


def _unstack_heads(t4):
    out = None
    for hl, hm in enumerate(_head_masks()):
        part = jnp.where(hm, t4[hl * WINDOW:(hl + 1) * WINDOW], 0.0)
        out = part if out is None else out + part
    return out


def _sink_block(sink_ref, g):
    return jnp.concatenate([jnp.broadcast_to(sink_ref[g * GROUP + hl:g * GROUP + hl + 1, 0:1], (WINDOW, 256))
                            for hl in range(GROUP)], axis=0)


def _sink_column(sink_ref, g):
    return jnp.concatenate([jnp.broadcast_to(sink_ref[g * GROUP + hl:g * GROUP + hl + 1, 0:1], (WINDOW, 1))
                            for hl in range(GROUP)], axis=0)


def _attn_fwd(pq, cos_t, sin_t, sinks_b, *, name, ts=256):
    S = pq.shape[0]
    ts = min(ts, S)
    nq = ts // WINDOW
    scale = HEAD_DIM ** -0.5

    def body(cur_ref, prev_ref, cosc_ref, sinc_ref, cosp_ref, sinp_ref, sink_ref, o_ref, lse_ref):
        i = pl.program_id(0)
        cosc, sinc = cosc_ref[...], sinc_ref[...]
        q = cur_ref[:, 0:512].astype(F32)
        qr = jnp.concatenate(
            [_rope_fwd(q[:, j * 128:(j + 1) * 128], cosc, sinc) for j in range(4)], axis=1) * scale
        qr = qr.astype(BF16)
        kc = _rope_fwd(cur_ref[:, 512:640].astype(F32), cosc, sinc)
        kp = _rope_fwd(prev_ref[:, 0:128].astype(F32), cosp_ref[...], sinp_ref[...])
        k_all = jnp.concatenate([kp, kc], axis=0)
        v_all = jnp.concatenate([prev_ref[:, 128:256].astype(F32), cur_ref[:, 640:768].astype(F32)], axis=0)
        kt = [t.astype(BF16) for t in _tile_heads(k_all)]
        vt = [t.astype(BF16) for t in _tile_heads(v_all)]
        band, col = _band4(2 * WINDOW)
        ones = jnp.ones((2 * WINDOW, 256), BF16)
        key_t = lax.broadcasted_iota(jnp.int32, (2 * WINDOW, GROUP * WINDOW), 0)
        qry_t = lax.broadcasted_iota(jnp.int32, (2 * WINDOW, GROUP * WINDOW), 1) % WINDOW
        band_t = (key_t > qry_t) & (key_t <= qry_t + WINDOW)
        NT = (((1,), (1,)), ((), ()))
        for qb in range(nq):
            rows = slice(qb * WINDOW, (qb + 1) * WINDOW)
            keys = slice(qb * WINDOW, (qb + 2) * WINDOW)
            valid = band & ((col >= WINDOW) | (i * nq + qb > 0))
            valid_t = band_t & ((key_t >= WINDOW) | (i * nq + qb > 0))
            for g in range(2):
                qs = _stack_heads(qr[rows, g * 256:(g + 1) * 256])
                sink = _sink_block(sink_ref, g)
                s = lax.dot_general(qs, kt[g][keys], NT, preferred_element_type=F32)
                s_t = lax.dot_general(kt[g][keys], qs, NT, preferred_element_type=F32)
                m_t = jnp.max(jnp.where(valid_t, s_t, MASK_VALUE), axis=0, keepdims=True)
                m_rep = jnp.broadcast_to(m_t, (WINDOW, GROUP * WINDOW)).T
                m = jnp.maximum(jnp.concatenate([m_rep, m_rep], axis=1), sink)
                e = jnp.exp(jnp.where(valid, s, MASK_VALUE) - m).astype(BF16)
                l = jnp.dot(e, ones, preferred_element_type=F32) + jnp.exp(sink - m)
                pv = jnp.dot(e, vt[g][keys], preferred_element_type=F32)
                o_ref[rows, g * 256:(g + 1) * 256] = (_unstack_heads(pv) / _unstack_heads(l)).astype(BF16)
                lse4 = (m + jnp.log(l))[:, 0:1]
                for hl in range(GROUP):
                    h = g * GROUP + hl
                    lse_ref[rows, h:h + 1] = lse4[hl * WINDOW:(hl + 1) * WINDOW]

    hb = ts // WINDOW
    cur = lambda i: (i, 0)
    prev = lambda i: (jnp.maximum(i * hb - 1, 0), 0)
    return pl.pallas_call(
        body, name=name, grid=(S // ts,),
        in_specs=[pl.BlockSpec((ts, 768), cur),
                  pl.BlockSpec((WINDOW, 256), lambda i: (jnp.maximum(i * hb - 1, 0), 2)),
                  pl.BlockSpec((ts, 128), cur), pl.BlockSpec((ts, 128), cur),
                  pl.BlockSpec((WINDOW, 128), prev), pl.BlockSpec((WINDOW, 128), prev),
                  pl.BlockSpec((8, 128), lambda i: (0, 0))],
        out_specs=[pl.BlockSpec((ts, 512), cur), pl.BlockSpec((ts, 8), cur)],
        out_shape=(jax.ShapeDtypeStruct((S, 512), BF16), jax.ShapeDtypeStruct((S, 8), F32)),
        compiler_params=_cp("parallel"),
    )(pq, pq, cos_t, sin_t, cos_t, sin_t, sinks_b)


def _attn_bwd(pq, cos_t, sin_t, sinks_b, do, o, lse, *, name, ts=256):
    S = pq.shape[0]
    ts = min(ts, S)
    nq = ts // WINDOW
    nt = S // ts
    scale = HEAD_DIM ** -0.5
    NT = (((1,), (1,)), ((), ()))
    TN = (((0,), (0,)), ((), ()))

    def body(cur_ref, prev_ref, nxt_ref, cosc_ref, sinc_ref, cosp_ref, sinp_ref, cosn_ref, sinn_ref, sink_ref,
             doc_ref, don_ref, oc_ref, on_ref, lsec_ref, lsen_ref, dpq_ref, dsink_ref):
        i = pl.program_id(0)
        last = i == nt - 1
        cosc, sinc = cosc_ref[...], sinc_ref[...]
        cose = jnp.concatenate([cosc, cosn_ref[...]], axis=0)
        sine = jnp.concatenate([sinc, sinn_ref[...]], axis=0)
        q = jnp.concatenate([cur_ref[:, 0:512], nxt_ref[:, 0:512]], axis=0).astype(F32)
        qr = jnp.concatenate(
            [_rope_fwd(q[:, j * 128:(j + 1) * 128], cose, sine) for j in range(4)], axis=1) * scale
        qr = qr.astype(BF16)
        kc = _rope_fwd(cur_ref[:, 512:640].astype(F32), cosc, sinc)
        kp = _rope_fwd(prev_ref[:, 0:128].astype(F32), cosp_ref[...], sinp_ref[...])
        k_all = jnp.concatenate([kp, kc], axis=0)
        v_all = jnp.concatenate([prev_ref[:, 128:256].astype(F32), cur_ref[:, 640:768].astype(F32)], axis=0)
        kt = [t.astype(BF16) for t in _tile_heads(k_all)]
        vt = [t.astype(BF16) for t in _tile_heads(v_all)]
        don = jnp.where(last, jnp.zeros_like(don_ref[...]), don_ref[...])
        do_e = jnp.concatenate([doc_ref[...], don], axis=0)
        o_e = jnp.concatenate([oc_ref[...], on_ref[...]], axis=0)
        band2, col2 = _band4(2 * WINDOW)
        band1, _ = _band4(WINDOW)
        ones = jnp.ones((256, 256), BF16)

        @pl.when(i == 0)
        def _():
            dsink_ref[...] = jnp.zeros_like(dsink_ref)

        dk_acc = [[None] * (nq + 1) for _ in range(2)]
        dv_acc = [[None] * (nq + 1) for _ in range(2)]

        def add(acc, g, e, val):
            acc[g][e] = val if acc[g][e] is None else acc[g][e] + val

        for qb in range(nq + 1):
            halo = qb == nq
            rows = slice(qb * WINDOW, (qb + 1) * WINDOW)
            if halo:
                keys = slice(qb * WINDOW, (qb + 1) * WINDOW)
                valid = band1 & jnp.logical_not(last)
            else:
                keys = slice(qb * WINDOW, (qb + 2) * WINDOW)
                valid = band2 & ((col2 >= WINDOW) | (i * nq + qb > 0))
            dq_parts = []
            for g in range(2):
                qs = _stack_heads(qr[rows, g * 256:(g + 1) * 256])
                dos = _stack_heads(do_e[rows, g * 256:(g + 1) * 256])
                o_g = o_e[rows, g * 256:(g + 1) * 256].astype(F32)
                kt_b, vt_b = kt[g][keys], vt[g][keys]
                lse_src = lsen_ref if halo else lsec_ref
                lse_rows = slice(0, WINDOW) if halo else rows
                big_l = jnp.concatenate([lse_src[lse_rows, g * GROUP + hl:g * GROUP + hl + 1] for hl in range(GROUP)],
                                        axis=0)
                delta = jnp.dot((dos.astype(F32) * jnp.concatenate([o_g] * GROUP, axis=0)).astype(BF16), ones,
                                preferred_element_type=F32)
                s = lax.dot_general(qs, kt_b, NT, preferred_element_type=F32)
                p = jnp.exp(jnp.where(valid, s, MASK_VALUE) - big_l)
                dp = lax.dot_general(dos, vt_b, NT, preferred_element_type=F32)
                ds = (p * (dp - delta[:, 0:p.shape[1]])).astype(BF16)
                dk_g = lax.dot_general(ds, qs, TN, preferred_element_type=F32)
                dv_g = lax.dot_general(p.astype(BF16), dos, TN, preferred_element_type=F32)
                if not halo:
                    dq_parts.append(_unstack_heads(jnp.dot(ds, kt_b, preferred_element_type=F32)))
                    dsink4 = jnp.exp(_sink_column(sink_ref, g) - big_l) * delta[:, 0:1]
                    for hl in range(GROUP):
                        h = g * GROUP + hl
                        dsink_h = -jnp.sum(dsink4[hl * WINDOW:(hl + 1) * WINDOW], axis=0, keepdims=True)
                        dsink_ref[h:h + 1, :] += jnp.broadcast_to(dsink_h, (1, 128))
                add(dk_acc, g, qb, dk_g[0:WINDOW])
                add(dv_acc, g, qb, dv_g[0:WINDOW])
                if not halo:
                    add(dk_acc, g, qb + 1, dk_g[WINDOW:2 * WINDOW])
                    add(dv_acc, g, qb + 1, dv_g[WINDOW:2 * WINDOW])
            if not halo:
                cs, sn = cosc[rows], sinc[rows]
                for g in range(2):
                    dq_g = dq_parts[g] * scale
                    for j in range(2):
                        c0 = g * 256 + j * 128
                        dpq_ref[rows, c0:c0 + 128] = _rope_bwd(dq_g[:, j * 128:(j + 1) * 128], cs, sn).astype(BF16)
        for e in range(1, nq + 1):
            rows = slice((e - 1) * WINDOW, e * WINDOW)
            dk = _fold_heads(dk_acc[0][e], dk_acc[1][e])
            dv = _fold_heads(dv_acc[0][e], dv_acc[1][e])
            dpq_ref[rows, 512:640] = _rope_bwd(dk, cosc[rows], sinc[rows]).astype(BF16)
            dpq_ref[rows, 640:768] = dv.astype(BF16)

    hb = ts // WINDOW
    nblk = S // WINDOW
    cur = lambda i: (i, 0)
    prev = lambda i: (jnp.maximum(i * hb - 1, 0), 0)
    nxt = lambda i: (jnp.minimum((i + 1) * hb, nblk - 1), 0)
    return pl.pallas_call(
        body, name=name, grid=(nt,),
        in_specs=[pl.BlockSpec((ts, 768), cur),
                  pl.BlockSpec((WINDOW, 256), lambda i: (jnp.maximum(i * hb - 1, 0), 2)),
                  pl.BlockSpec((WINDOW, 768), nxt),
                  pl.BlockSpec((ts, 128), cur), pl.BlockSpec((ts, 128), cur),
                  pl.BlockSpec((WINDOW, 128), prev), pl.BlockSpec((WINDOW, 128), prev),
                  pl.BlockSpec((WINDOW, 128), nxt), pl.BlockSpec((WINDOW, 128), nxt),
                  pl.BlockSpec((8, 128), lambda i: (0, 0)),
                  pl.BlockSpec((ts, 512), cur), pl.BlockSpec((WINDOW, 512), nxt),
                  pl.BlockSpec((ts, 512), cur), pl.BlockSpec((WINDOW, 512), nxt),
                  pl.BlockSpec((ts, 8), cur), pl.BlockSpec((WINDOW, 8), nxt)],
        out_specs=[pl.BlockSpec((ts, 768), cur), pl.BlockSpec((8, 128), lambda i: (0, 0))],
        out_shape=(jax.ShapeDtypeStruct((S, 768), BF16), jax.ShapeDtypeStruct((8, 128), F32)),
        compiler_params=_cp("arbitrary"),
    )(pq, pq, pq, cos_t, sin_t, cos_t, sin_t, cos_t, sin_t, sinks_b, do, do, o, o, lse, lse)


def _shift_dn(x, k):
    return pltpu.roll(x, k, 0)


def _shift_up(x, k):
    return pltpu.roll(x, x.shape[0] - k, 0)


def _pool_lane_select(vals):
    lane = lax.broadcasted_iota(jnp.int32, (1, 256), 1)
    out = vals[3]
    for g in (2, 1, 0):
        out = jnp.where(lane < 64 * (g + 1), vals[g], out)
    return out


def _pool_inv_count(t0, n):
    t = t0 + lax.broadcasted_iota(jnp.int32, (n, 256), 0)
    lane = lax.broadcasted_iota(jnp.int32, (n, 256), 1)
    w = jnp.where(lane < 64, 2, jnp.where(lane < 128, 4, jnp.where(lane < 192, 8, 16)))
    return 1.0 / jnp.minimum(t + 1, w).astype(F32)


def _pooled(u_ext, t0, n):
    s2 = u_ext + _shift_dn(u_ext, 1)
    s4 = s2 + _shift_dn(s2, 2)
    s8 = s4 + _shift_dn(s4, 4)
    s16 = s8 + _shift_dn(s8, 8)
    win = _pool_lane_select([s2, s4, s8, s16])[HALO:HALO + n]
    return win * _pool_inv_count(t0, n) - u_ext[HALO:HALO + n]


def _poolconv_fwd(pp, wbd, pool_scale, conv_w, *, name, ts=512):
    S = pp.shape[0]
    ts = min(ts, S)

    def body(cur_ref, prev_ref, wbd_ref, sc_ref, cw_ref, oa_ref, oc_ref):
        i = pl.program_id(0)
        prev = jnp.where(i > 0, prev_ref[...].astype(F32), 0.0)
        u_ext = jnp.concatenate([prev[:, 0:256], cur_ref[:, 0:256].astype(F32)], axis=0)
        pooled = _pooled(u_ext, i * ts, ts)
        mixed = jnp.dot(pooled.astype(BF16), wbd_ref[...], preferred_element_type=F32)
        oa_ref[...] = (mixed * sc_ref[...]).astype(BF16)
        v_ext = jnp.concatenate([prev[:, 256:512] * prev[:, 768:1024],
                                 cur_ref[:, 256:512].astype(F32) * cur_ref[:, 768:1024].astype(F32)], axis=0)
        cv = cw_ref[2:3, :] * v_ext + cw_ref[1:2, :] * _shift_dn(v_ext, 1) + cw_ref[0:1, :] * _shift_dn(v_ext, 2)
        oc_ref[...] = (cur_ref[:, 512:768].astype(F32) * cv[HALO:HALO + ts]).astype(BF16)

    hb = ts // HALO
    cur = lambda i: (i, 0)
    const = lambda i: (0, 0)
    return pl.pallas_call(
        body, name=name, grid=(S // ts,),
        in_specs=[pl.BlockSpec((ts, 1024), cur),
                  pl.BlockSpec((HALO, 1024), lambda i: (jnp.maximum(i * hb - 1, 0), 0)),
                  pl.BlockSpec((256, 256), const), pl.BlockSpec((1, 256), const), pl.BlockSpec((3, 256), const)],
        out_specs=[pl.BlockSpec((ts, 256), cur)] * 2,
        out_shape=(jax.ShapeDtypeStruct((S, 256), BF16),) * 2,
        compiler_params=_cp("parallel"),
    )(pp, pp, wbd, pool_scale, conv_w)


def _poolconv_bwd(pp, do_a, do_c, wbd, wbd_t, pool_scale, conv_w, *, name, ts=512):
    S = pp.shape[0]
    ts = min(ts, S)
    nt = S // ts
    n_e = ts + 2 * HALO

    def body(cur_ref, prev_ref, nxt_ref, dac_ref, dan_ref, dcc_ref, dcn_ref, wbd_ref, wbdt_ref, sc_ref, cw_ref,
             dpp_ref, pooled_ref, dmixed_ref, dsc_ref, dcw_ref):
        i = pl.program_id(0)

        @pl.when(i == 0)
        def _():
            dsc_ref[...] = jnp.zeros_like(dsc_ref)
            dcw_ref[...] = jnp.zeros_like(dcw_ref)

        prev = jnp.where(i > 0, prev_ref[...].astype(F32), 0.0)
        nxt = nxt_ref[...].astype(F32)
        cur = cur_ref[...].astype(F32)
        not_last = i < nt - 1
        da_n = jnp.where(not_last, dan_ref[...].astype(F32), 0.0)
        dc_n = jnp.where(not_last, dcn_ref[...].astype(F32), 0.0)
        zeros_h = jnp.zeros((HALO, 256), F32)
        sc = sc_ref[...]

        u_ext = jnp.concatenate([prev[:, 0:256], cur[:, 0:256]], axis=0)
        pooled = _pooled(u_ext, i * ts, ts)
        pooled_b = pooled.astype(BF16)
        pooled_ref[...] = pooled_b
        mixed = jnp.dot(pooled_b, wbd_ref[...], preferred_element_type=F32)
        da_c = dac_ref[...].astype(F32)
        dsc_ref[...] += jnp.sum(da_c * mixed, axis=0, keepdims=True)
        dmixed_e = jnp.concatenate([da_c, da_n], axis=0) * sc
        dmixed_ref[...] = dmixed_e[0:ts].astype(BF16)
        dpooled = jnp.dot(dmixed_e.astype(BF16), wbdt_ref[...], preferred_element_type=F32)
        qd = dpooled * _pool_inv_count(i * ts, ts + HALO)
        f2 = qd + _shift_up(qd, 1)
        f4 = f2 + _shift_up(f2, 2)
        f8 = f4 + _shift_up(f4, 4)
        f16 = f8 + _shift_up(f8, 8)
        du = (_pool_lane_select([f2, f4, f8, f16]) - dpooled)[0:ts]
        dpp_ref[:, 0:256] = du.astype(BF16)

        xc_e = jnp.concatenate([prev[:, 256:512], cur[:, 256:512], nxt[:, 256:512]], axis=0)
        gc_e = jnp.concatenate([prev[:, 768:1024], cur[:, 768:1024], nxt[:, 768:1024]], axis=0)
        gb_e = jnp.concatenate([zeros_h, cur[:, 512:768], nxt[:, 512:768]], axis=0)
        dc_e = jnp.concatenate([zeros_h, dcc_ref[...].astype(F32), dc_n], axis=0)
        v_e = xc_e * gc_e
        v1, v2 = _shift_dn(v_e, 1), _shift_dn(v_e, 2)
        w0, w1, w2 = cw_ref[0:1, :], cw_ref[1:2, :], cw_ref[2:3, :]
        cv = w2 * v_e + w1 * v1 + w0 * v2
        dcv = dc_e * gb_e
        dv = w2 * dcv + w1 * _shift_up(dcv, 1) + w0 * _shift_up(dcv, 2)
        tile = slice(HALO, HALO + ts)
        dpp_ref[:, 256:512] = (dv * gc_e)[tile].astype(BF16)
        dpp_ref[:, 512:768] = (dc_e * cv)[tile].astype(BF16)
        dpp_ref[:, 768:1024] = (dv * xc_e)[tile].astype(BF16)
        dcv_t = dcv[tile]
        dcw_ref[0:1, :] += jnp.sum(dcv_t * v2[tile], axis=0, keepdims=True)
        dcw_ref[1:2, :] += jnp.sum(dcv_t * v1[tile], axis=0, keepdims=True)
        dcw_ref[2:3, :] += jnp.sum(dcv_t * v_e[tile], axis=0, keepdims=True)

    hb = ts // HALO
    nblk = S // HALO
    cur = lambda i: (i, 0)
    const = lambda i: (0, 0)
    prev = lambda i: (jnp.maximum(i * hb - 1, 0), 0)
    nxt = lambda i: (jnp.minimum((i + 1) * hb, nblk - 1), 0)
    del n_e
    return pl.pallas_call(
        body, name=name, grid=(nt,),
        in_specs=[pl.BlockSpec((ts, 1024), cur), pl.BlockSpec((HALO, 1024), prev), pl.BlockSpec((HALO, 1024), nxt),
                  pl.BlockSpec((ts, 256), cur), pl.BlockSpec((HALO, 256), nxt),
                  pl.BlockSpec((ts, 256), cur), pl.BlockSpec((HALO, 256), nxt),
                  pl.BlockSpec((256, 256), const), pl.BlockSpec((256, 256), const),
                  pl.BlockSpec((1, 256), const), pl.BlockSpec((3, 256), const)],
        out_specs=[pl.BlockSpec((ts, 1024), cur), pl.BlockSpec((ts, 256), cur), pl.BlockSpec((ts, 256), cur),
                   pl.BlockSpec((1, 256), const), pl.BlockSpec((3, 256), const)],
        out_shape=(jax.ShapeDtypeStruct((S, 1024), BF16), jax.ShapeDtypeStruct((S, 256), BF16),
                   jax.ShapeDtypeStruct((S, 256), BF16), jax.ShapeDtypeStruct((1, 256), F32),
                   jax.ShapeDtypeStruct((3, 256), F32)),
        compiler_params=_cp("arbitrary"),
    )(pp, pp, pp, do_a, do_a, do_c, do_c, wbd, wbd_t, pool_scale, conv_w)


def _sigmoid(x):
    return 1.0 / (1.0 + jnp.exp(-x))


def _merge_fwd(o_a, o_b, o_c, glog, w_br, *, name, ts=512):
    S = o_a.shape[0]
    D = w_br.shape[1]
    ts = min(ts, S)

    def body(oa_ref, ob_ref, oc_ref, gl_ref, w_ref, m_ref):
        pa = jnp.dot(oa_ref[...], w_ref[0:256, :], preferred_element_type=F32)
        pb = jnp.dot(ob_ref[...], w_ref[256:768, :], preferred_element_type=F32)
        pc = jnp.dot(oc_ref[...], w_ref[768:1024, :], preferred_element_type=F32)
        m = _sigmoid(gl_ref[:, 0:D].astype(F32)) * pa
        m = m + _sigmoid(gl_ref[:, D:2 * D].astype(F32)) * pb
        m = m + _sigmoid(gl_ref[:, 2 * D:3 * D].astype(F32)) * pc
        m_ref[...] = m.astype(BF16)

    cur = lambda i: (i, 0)
    return pl.pallas_call(
        body, name=name, grid=(S // ts,),
        in_specs=[pl.BlockSpec((ts, 256), cur), pl.BlockSpec((ts, 512), cur), pl.BlockSpec((ts, 256), cur),
                  pl.BlockSpec((ts, 3 * D), cur), pl.BlockSpec((1024, D), lambda i: (0, 0))],
        out_specs=pl.BlockSpec((ts, D), cur),
        out_shape=jax.ShapeDtypeStruct((S, D), BF16),
        compiler_params=_cp("parallel"),
    )(o_a, o_b, o_c, glog, w_br)


def _merge_bwd(dm, o_a, o_b, o_c, glog, w_br, w_br_t, *, name, ts=256):
    S = o_a.shape[0]
    D = w_br.shape[1]
    ts = min(ts, S)

    def body(dm_ref, oa_ref, ob_ref, oc_ref, gl_ref, w_ref, wt_ref, dgl_ref, dp_ref, doa_ref, dob_ref, doc_ref):
        dmv = dm_ref[...].astype(F32)
        branches = ((oa_ref, 0, 256, doa_ref), (ob_ref, 256, 768, dob_ref), (oc_ref, 768, 1024, doc_ref))
        for b, (o_ref, r0, r1, do_ref) in enumerate(branches):
            prod = jnp.dot(o_ref[...], w_ref[r0:r1, :], preferred_element_type=F32)
            gate = _sigmoid(gl_ref[:, b * D:(b + 1) * D].astype(F32))
            dgl_ref[:, b * D:(b + 1) * D] = (dmv * prod * gate * (1.0 - gate)).astype(BF16)
            dprod = (dmv * gate).astype(BF16)
            dp_ref[:, b * D:(b + 1) * D] = dprod
            do_ref[...] = jnp.dot(dprod, wt_ref[:, r0:r1], preferred_element_type=F32).astype(BF16)

    cur = lambda i: (i, 0)
    const = lambda i: (0, 0)
    return pl.pallas_call(
        body, name=name, grid=(S // ts,),
        in_specs=[pl.BlockSpec((ts, D), cur), pl.BlockSpec((ts, 256), cur), pl.BlockSpec((ts, 512), cur),
                  pl.BlockSpec((ts, 256), cur), pl.BlockSpec((ts, 3 * D), cur),
                  pl.BlockSpec((1024, D), const), pl.BlockSpec((D, 1024), const)],
        out_specs=[pl.BlockSpec((ts, 3 * D), cur), pl.BlockSpec((ts, 3 * D), cur), pl.BlockSpec((ts, 256), cur),
                   pl.BlockSpec((ts, 512), cur), pl.BlockSpec((ts, 256), cur)],
        out_shape=(jax.ShapeDtypeStruct((S, 3 * D), BF16), jax.ShapeDtypeStruct((S, 3 * D), BF16),
                   jax.ShapeDtypeStruct((S, 256), BF16), jax.ShapeDtypeStruct((S, 512), BF16),
                   jax.ShapeDtypeStruct((S, 256), BF16)),
        compiler_params=_cp("parallel"),
    )(dm, o_a, o_b, o_c, glog, w_br, w_br_t)


def _ffn_act_fwd(up_pre, fcw, *, name, tc, ts=512):
    S, F2 = up_pre.shape
    ts = min(ts, S)
    nj = F2 // (2 * tc)

    def body(cur_ref, prev_ref, w_ref, h_ref):
        i = pl.program_id(1)
        prev = jnp.where(i > 0, prev_ref[...].astype(F32), 0.0)
        x = jnp.concatenate([prev, cur_ref[...].astype(F32)], axis=0)
        up = (w_ref[2:3, :] * x + w_ref[1:2, :] * _shift_dn(x, 1) + w_ref[0:1, :] * _shift_dn(x, 2))[HALO:HALO + ts]
        a, b = up[:, 0:tc], up[:, tc:2 * tc]
        h_ref[...] = (a * _sigmoid(a) * b).astype(BF16)

    hb = ts // HALO
    return pl.pallas_call(
        body, name=name, grid=(nj, S // ts),
        in_specs=[pl.BlockSpec((ts, 2 * tc), lambda j, i: (i, j)),
                  pl.BlockSpec((HALO, 2 * tc), lambda j, i: (jnp.maximum(i * hb - 1, 0), j)),
                  pl.BlockSpec((3, 2 * tc), lambda j, i: (0, j))],
        out_specs=pl.BlockSpec((ts, tc), lambda j, i: (i, j)),
        out_shape=jax.ShapeDtypeStruct((S, F2 // 2), BF16),
        compiler_params=_cp("parallel", "parallel"),
    )(up_pre, up_pre, fcw)


def _ffn_act_bwd(up_pre, dh, fcw, *, name, tc, ts=512, scatter=None):
    S, F2 = up_pre.shape
    ts = min(ts, S)
    nt = S // ts
    nj = F2 // (2 * tc)
    has_rs = scatter is not None

    def body(cur_ref, prev_ref, nxt_ref, dhc_ref, dhn_ref, w_ref, *rest):
        if has_rs:
            q_ref, dpre_ref, dw_ref, recv_ref, send_sems, recv_sems = rest
            rs_refs = (q_ref, recv_ref, send_sems, recv_sems)

            @pl.when((pl.program_id(0) == 0) & (pl.program_id(1) == 0))
            def _():
                _rs_chips_start(*rs_refs)
        else:
            dpre_ref, dw_ref = rest
        i = pl.program_id(1)

        @pl.when(i == 0)
        def _():
            dw_ref[...] = jnp.zeros_like(dw_ref)

        prev = jnp.where(i > 0, prev_ref[...].astype(F32), 0.0)
        x = jnp.concatenate([prev, cur_ref[...].astype(F32), nxt_ref[...].astype(F32)], axis=0)
        dh_n = jnp.where(i < nt - 1, dhn_ref[...].astype(F32), 0.0)
        dh_e = jnp.concatenate([jnp.zeros((HALO, tc), F32), dhc_ref[...].astype(F32), dh_n], axis=0)
        w0, w1, w2 = w_ref[0:1, :], w_ref[1:2, :], w_ref[2:3, :]
        x1, x2 = _shift_dn(x, 1), _shift_dn(x, 2)
        up = w2 * x + w1 * x1 + w0 * x2
        a, b = up[:, 0:tc], up[:, tc:2 * tc]
        sg = _sigmoid(a)
        da = dh_e * b * (sg * (1.0 + a * (1.0 - sg)))
        db = dh_e * (a * sg)
        dup = jnp.concatenate([da, db], axis=1)
        dpre = w2 * dup + w1 * _shift_up(dup, 1) + w0 * _shift_up(dup, 2)
        tile = slice(HALO, HALO + ts)
        dpre_ref[...] = dpre[tile].astype(BF16)
        dup_t = dup[tile]
        dw_ref[0:1, :] += jnp.sum(dup_t * x2[tile], axis=0, keepdims=True)
        dw_ref[1:2, :] += jnp.sum(dup_t * x1[tile], axis=0, keepdims=True)
        dw_ref[2:3, :] += jnp.sum(dup_t * x[tile], axis=0, keepdims=True)

        if has_rs:
            @pl.when((pl.program_id(0) == nj - 1) & (pl.program_id(1) == nt - 1))
            def _():
                _rs_chips_finish(*rs_refs)

    hb = ts // HALO
    nblk = S // HALO
    prev = lambda j, i: (jnp.maximum(i * hb - 1, 0), j)
    nxt = lambda j, i: (jnp.minimum((i + 1) * hb, nblk - 1), j)
    in_specs = [pl.BlockSpec((ts, 2 * tc), lambda j, i: (i, j)), pl.BlockSpec((HALO, 2 * tc), prev),
                pl.BlockSpec((HALO, 2 * tc), nxt),
                pl.BlockSpec((ts, tc), lambda j, i: (i, j)), pl.BlockSpec((HALO, tc), nxt),
                pl.BlockSpec((3, 2 * tc), lambda j, i: (0, j))]
    out_specs = [pl.BlockSpec((ts, 2 * tc), lambda j, i: (i, j)), pl.BlockSpec((3, 2 * tc), lambda j, i: (0, j))]
    out_shape = [jax.ShapeDtypeStruct((S, F2), BF16), jax.ShapeDtypeStruct((3, F2), F32)]
    args = [up_pre, up_pre, up_pre, dh, dh, fcw]
    if has_rs:
        in_specs.append(pl.BlockSpec(memory_space=pl.ANY))
        args.append(scatter)
        out_specs.append(pl.BlockSpec(memory_space=pl.ANY))
        out_shape.append(jax.ShapeDtypeStruct((3,) + scatter.shape[1:], scatter.dtype))
    return pl.pallas_call(
        body, name=name, grid=(nj, nt), in_specs=in_specs, out_specs=out_specs, out_shape=out_shape,
        scratch_shapes=_RS_SEMS if has_rs else [],
        compiler_params=_cp("arbitrary", "arbitrary") if has_rs else _cp("parallel", "arbitrary"),
    )(*args)


def _ln_bwd(dy, xhat, rstd, gamma, *, name, ts=512):
    S, D = dy.shape
    ts = min(ts, S)

    def body(dy_ref, xh_ref, rs_ref, g_ref, dz_ref, dg_ref, db_ref):
        @pl.when(pl.program_id(0) == 0)
        def _():
            dg_ref[...] = jnp.zeros_like(dg_ref)
            db_ref[...] = jnp.zeros_like(db_ref)

        dyv = dy_ref[...]
        xh = xh_ref[...].astype(F32)
        dyg = dyv * g_ref[...]
        c1 = jnp.mean(dyg, axis=-1, keepdims=True)
        c2 = jnp.mean(dyg * xh, axis=-1, keepdims=True)
        dz_ref[...] = rs_ref[...] * (dyg - c1 - xh * c2)
        dg_ref[...] += jnp.sum(dyv * xh, axis=0, keepdims=True)
        db_ref[...] += jnp.sum(dyv, axis=0, keepdims=True)

    cur = lambda i: (i, 0)
    const = lambda i: (0, 0)
    return pl.pallas_call(
        body, name=name, grid=(S // ts,),
        in_specs=[pl.BlockSpec((ts, D), cur), pl.BlockSpec((ts, D), cur), pl.BlockSpec((ts, 1), cur),
                  pl.BlockSpec((1, D), const)],
        out_specs=[pl.BlockSpec((ts, D), cur), pl.BlockSpec((1, D), const), pl.BlockSpec((1, D), const)],
        out_shape=(jax.ShapeDtypeStruct((S, D), F32), jax.ShapeDtypeStruct((1, D), F32),
                   jax.ShapeDtypeStruct((1, D), F32)),
        compiler_params=_cp("arbitrary"),
    )(dy, xhat, rstd, gamma)


def _loss_head(y, tgt, *, name, ts=512):
    S, D = y.shape
    ts = min(ts, S)

    def body(y_ref, t_ref, dy_ref, sq_ref):
        @pl.when(pl.program_id(0) == 0)
        def _():
            sq_ref[...] = jnp.zeros_like(sq_ref)

        e = y_ref[...] - t_ref[...]
        dy_ref[...] = e * (1.0 / D)
        sq_ref[...] += jnp.sum(e * e, axis=0, keepdims=True)

    cur = lambda i: (i, 0)
    return pl.pallas_call(
        body, name=name, grid=(S // ts,),
        in_specs=[pl.BlockSpec((ts, D), cur), pl.BlockSpec((ts, D), cur)],
        out_specs=[pl.BlockSpec((ts, D), cur), pl.BlockSpec((1, D), lambda i: (0, 0))],
        out_shape=(jax.ShapeDtypeStruct((S, D), F32), jax.ShapeDtypeStruct((1, D), F32)),
        compiler_params=_cp("arbitrary"),
    )(y, tgt)


def _adamw(w, g, m, v, *, name, tr=512):
    R, C = w.shape
    tr = _div_tile(R, tr)
    c1 = 1.0 - ADAM_B1 ** ADAM_STEP
    c2 = 1.0 - ADAM_B2 ** ADAM_STEP

    def body(w_ref, g_ref, m_ref, v_ref, d_ref, mo_ref, vo_ref):
        gv = g_ref[...]
        m2 = ADAM_B1 * m_ref[...] + (1.0 - ADAM_B1) * gv
        v2 = ADAM_B2 * v_ref[...] + (1.0 - ADAM_B2) * (gv * gv)
        m_hat = m2 / c1
        v_hat = v2 / c2
        d_ref[...] = -ADAM_LR * (m_hat / (jnp.sqrt(v_hat) + ADAM_EPS) + ADAM_WD * w_ref[...])
        mo_ref[...] = m2
        vo_ref[...] = v2

    spec = pl.BlockSpec((tr, C), lambda i: (i, 0))
    return pl.pallas_call(
        body, name=name, grid=(R // tr,),
        in_specs=[spec] * 4, out_specs=[spec] * 3,
        out_shape=(jax.ShapeDtypeStruct((R, C), F32),) * 3,
        compiler_params=_cp("parallel"),
    )(w, g, m, v)


def _interleave_cols(w, nj):
    lead, f2 = w.shape[:-1], w.shape[-1]
    tc = f2 // (2 * nj)
    w = w.reshape(lead + (2, nj, tc))
    return jnp.swapaxes(w, -3, -2).reshape(lead + (f2,))


def _deinterleave_cols(w, nj):
    lead, f2 = w.shape[:-1], w.shape[-1]
    tc = f2 // (2 * nj)
    w = w.reshape(lead + (nj, 2, tc))
    return jnp.swapaxes(w, -3, -2).reshape(lead + (f2,))


def _block_diag(w_pool):
    return jnp.concatenate([jnp.pad(w_pool[g], ((0, 0), (64 * g, 192 - 64 * g))) for g in range(4)], axis=0)


def _pad_rows(v, rows):
    return jnp.pad(v, (0, rows * LANES - v.shape[0])).reshape(rows, LANES)


def kernel(x, positions, w_in, w_pool, pool_scale, attn_sinks, conv_w, w_branch_a, w_branch_b, w_branch_c, w_o, ln1_g, ln1_b, w_up, ffn_conv_w, w_down, ln2_g, ln2_b, loss_target, m_w_in, m_w_pool, m_pool_scale, m_attn_sinks, m_conv_w, m_w_branch_a, m_w_branch_b, m_w_branch_c, m_w_o, m_ln1_g, m_ln1_b, m_w_up, m_ffn_conv_w, m_w_down, m_ln2_g, m_ln2_b, v_w_in, v_w_pool, v_pool_scale, v_attn_sinks, v_conv_w, v_w_branch_a, v_w_branch_b, v_w_branch_c, v_w_o, v_ln1_g, v_ln1_b, v_w_up, v_ffn_conv_w, v_w_down, v_ln2_g, v_ln2_b):
    L, D, in_shard = w_in.shape
    S = x.shape[1]
    IN = in_shard * N_DEV
    F2 = w_up.shape[2] * N_DEV
    F = F2 // 2
    assert D == 1024 and IN == 1792 + 3 * D and x.shape[0] == 1 and S % 512 == 0
    alpha = (2 * L) ** 0.25
    NJ = 2
    TC = F // NJ
    xs = x.reshape(S, D)
    tgt = loss_target.reshape(S, D)

    big = [w_in, w_branch_a, w_branch_b, w_branch_c, w_o, w_up, w_down]
    PART_A, PART_B = (0, 1, 2, 3, 4), (5, 6)
    rows_l = [a.size // L // LANES for a in big]
    offs_l = [sum(rows_l[:k]) for k in range(len(big) + 1)]

    def pack_part(l, part):
        return jnp.concatenate([big[k][l].reshape(-1, LANES).astype(BF16) for k in part], axis=0)

    n_cw, n_fw = conv_w.size, ffn_conv_w.size
    small_rows = -(-(n_cw + n_fw) // LANES)
    small = _pad_rows(jnp.concatenate([conv_w.reshape(-1), ffn_conv_w.reshape(-1)]), small_rows)
    gsmall = _all_gather(small, "ag_conv_weights").reshape(N_DEV, -1)
    conv_full = gsmall[:, :n_cw].reshape(N_DEV, L, 3, -1).transpose(1, 2, 0, 3).reshape(L, 3, 256)
    fcw_full = gsmall[:, n_cw:n_cw + n_fw].reshape(N_DEV, L, 3, -1).transpose(1, 2, 0, 3).reshape(L, 3, F2)
    fcw_full = _interleave_cols(fcw_full, NJ)

    def shard_of(g, part, k, shape):
        o = offs_l[k] - offs_l[part[0]]
        return g[:, o:o + rows_l[k], :].reshape((N_DEV,) + shape)

    def unpack_a(g):
        win = shard_of(g, PART_A, 0, (D, in_shard)).transpose(1, 0, 2).reshape(D, IN)
        wg = win[:, 1792:]
        wp = jnp.concatenate([win[:, 0:256], win[:, 1024:1792]], axis=1)
        wq = win[:, 256:1024]
        wa = shard_of(g, PART_A, 1, (256, D // N_DEV)).transpose(1, 0, 2).reshape(256, D)
        wb = shard_of(g, PART_A, 2, (512, D // N_DEV)).transpose(1, 0, 2).reshape(512, D)
        wc = shard_of(g, PART_A, 3, (256, D // N_DEV)).transpose(1, 0, 2).reshape(256, D)
        wbr = jnp.concatenate([wa, wb, wc], axis=0)
        wo = shard_of(g, PART_A, 4, (D // N_DEV, D)).reshape(D, D)
        return dict(wg=wg, wp=wp, wq=wq, wg_t=wg.T, wp_t=wp.T, wq_t=wq.T, wbr=wbr, wbr_t=wbr.T, wo=wo, wo_t=wo.T)

    def unpack_b(g):
        nh = N_DEV // (2 * NJ)
        wup = shard_of(g, PART_B, 5, (D, F2 // N_DEV)).reshape(2, NJ, nh, D, F2 // N_DEV)
        wup = wup.transpose(3, 1, 0, 2, 4).reshape(D, F2)
        wdn = shard_of(g, PART_B, 6, (F // N_DEV, D)).reshape(F, D)
        return dict(wup=wup, wup_t=wup.T, wdn=wdn, wdn_t=wdn.T)

    def local_weights(l):
        wbd = _block_diag(w_pool[l]).astype(BF16)
        return dict(wbd=wbd, wbd_t=wbd.T, scale=pool_scale[l].reshape(1, 256), conv=conv_full[l],
                    fcw=fcw_full[l], sinks=jnp.broadcast_to(attn_sinks[l].reshape(8, 1), (8, 128)),
                    g1=ln1_g[l].reshape(1, D), b1=ln1_b[l].reshape(1, D),
                    g2=ln2_g[l].reshape(1, D), b2=ln2_b[l].reshape(1, D))

    inv_freq = ROPE_THETA ** (-jnp.arange(0, ROT_DIM, 2, dtype=F32) / ROT_DIM)
    head_lane = jnp.concatenate([inv_freq, inv_freq, jnp.zeros((HEAD_DIM - ROT_DIM,), F32)])
    head_sign = jnp.concatenate([-jnp.ones((8,), F32), jnp.ones((8,), F32), jnp.zeros((HEAD_DIM - ROT_DIM,), F32)])
    inv_lane = jnp.tile(head_lane, 2).reshape(1, 128)
    sign_lane = jnp.tile(head_sign, 2).reshape(1, 128)
    cos_t, sin_t = _rope_tables(positions.reshape(S, 1), inv_lane, sign_lane, "rope_tables")

    saved, W = [], []
    h_in = xs
    gathered_a = _all_gather(pack_part(0, PART_A), "ag_weights_first")
    for l in range(L):
        w = {**unpack_a(gathered_a), **local_weights(l)}
        pg, gathered_b = _mm(h_in, w["wg"], out_dtype=BF16, name="proj_gate", gather=pack_part(l, PART_B))
        w.update(unpack_b(gathered_b))
        W.append(w)
        pp = _mm(h_in, w["wp"], out_dtype=BF16, name="proj_poolconv")
        pq = _mm(h_in, w["wq"], out_dtype=BF16, name="proj_qkv")
        o_a, o_c = _poolconv_fwd(pp, w["wbd"], w["scale"], w["conv"], name="poolconv_fwd")
        o_b, lse = _attn_fwd(pq, cos_t, sin_t, w["sinks"], name="attn_fwd")
        merged = _merge_fwd(o_a, o_b, o_c, pg, w["wbr"], name="merge_fwd")
        x1, xh1, rs1 = _mm_ln(merged, w["wo"], h_in, w["g1"], w["b1"], alpha=alpha, name="wo_ln1")
        if l + 1 < L:
            up_pre, gathered_a = _mm(x1, w["wup"], out_dtype=BF16, name="ffn_up", tn=2 * TC,
                                     gather=pack_part(l + 1, PART_A))
        else:
            up_pre = _mm(x1, w["wup"], out_dtype=BF16, name="ffn_up", tn=2 * TC)
        hact =_ffn_act_fwd(up_pre, w["fcw"], name="ffn_act_fwd", tc=TC, ts=256)
        x2, xh2, rs2 = _mm_ln(hact, w["wdn"], x1, w["g2"], w["b2"], alpha=alpha, name="down_ln2")
        saved.append(dict(x0=h_in, pg=pg, pp=pp, pq=pq, o_a=o_a, o_b=o_b, o_c=o_c, lse=lse, merged=merged,
                          x1=x1, xh1=xh1, rs1=rs1, up_pre=up_pre, hact=hact, xh2=xh2, rs2=rs2))
        h_in = x2

    dy, sq_lanes = _loss_head(h_in, tgt, name="loss_head")

    def pack_grads(g):
        col = lambda a, n: a.reshape(a.shape[0], N_DEV, n).transpose(1, 0, 2)
        row = lambda a, n: a.reshape(N_DEV, n, a.shape[1])
        nh = N_DEV // (2 * NJ)
        up = g["w_up"].reshape(D, NJ, 2, nh, F2 // N_DEV).transpose(2, 1, 3, 0, 4)
        parts = [col(g["w_in"], in_shard), col(g["a"], D // N_DEV), col(g["b"], D // N_DEV), col(g["c"], D // N_DEV),
                 row(g["w_o"], D // N_DEV), up, row(g["w_down"], F // N_DEV)]
        return jnp.concatenate([p.reshape(N_DEV, -1, LANES).astype(BF16) for p in parts], axis=1)

    my_c = lax.axis_index("c").astype(jnp.int32).reshape(1)
    my_chip = (2 * lax.axis_index("x") + lax.axis_index("y")).astype(jnp.int32).reshape(1)
    gw = [None] * L
    pair_sum = [None] * L
    from_chips = [None] * L
    for l in reversed(range(L)):
        w, sv = W[l], saved[l]
        dz2, dg2, db2 = _ln_bwd(dy, sv["xh2"], sv["rs2"], w["g2"], name="ln2_bwd")
        dh = _mm(dz2, w["wdn_t"], out_dtype=BF16, name="down_bwd_x")
        dw_dn = _mm_tn(sv["hact"], dz2, name="down_bwd_w", tka=TC, na=NJ, tn=D, ts=1024)
        if l + 1 < L:
            dpre, dfcw, from_chips[l + 1] = _ffn_act_bwd(sv["up_pre"], dh, w["fcw"], name="ffn_act_bwd", tc=TC, ts=256,
                                                         scatter=pair_sum[l + 1])
        else:
            dpre, dfcw = _ffn_act_bwd(sv["up_pre"], dh, w["fcw"], name="ffn_act_bwd", tc=TC, ts=256)
        dx1 = _mm(dpre, w["wup_t"], out_dtype=F32, name="up_bwd_x", tk=2 * TC, add=dz2, add_scale=alpha)
        dw_up = _mm_tn(sv["x1"], dpre, name="up_bwd_w", tka=D, tn=TC, nb=2 * NJ, ts=1024)
        dz1, dg1, db1 = _ln_bwd(dx1, sv["xh1"], sv["rs1"], w["g1"], name="ln1_bwd")
        dmerged = _mm(dz1, w["wo_t"], out_dtype=BF16, name="wo_bwd_x")
        dw_o = _mm_tn(sv["merged"], dz1, name="wo_bwd_w", tka=D, tn=D // 2, nb=2)
        dpg, dprod, do_a, do_b, do_c = _merge_bwd(dmerged, sv["o_a"], sv["o_b"], sv["o_c"], sv["pg"],
                                                  w["wbr"], w["wbr_t"], name="merge_bwd")
        dw_a = _mm_tn(sv["o_a"], dprod, name="branch_a_bwd_w", tka=256, tn=D, b_off=0)
        dw_b = _mm_tn(sv["o_b"], dprod, name="branch_b_bwd_w", tka=512, tn=D, b_off=1)
        dw_c = _mm_tn(sv["o_c"], dprod, name="branch_c_bwd_w", tka=256, tn=D, b_off=2)
        dpq, dsink = _attn_bwd(sv["pq"], cos_t, sin_t, w["sinks"], do_b, sv["o_b"], sv["lse"], name="attn_bwd")
        dpp, pooled, dmixed, dscale, dconv = _poolconv_bwd(sv["pp"], do_a, do_c, w["wbd"], w["wbd_t"], w["scale"],
                                                           w["conv"], name="poolconv_bwd")
        dwbd = _mm_tn(pooled, dmixed, name="pool_bwd_w", tka=256, tn=256)
        dx = _mm(dpg, w["wg_t"], out_dtype=F32, name="proj_gate_bwd_x", add=dz1, add_scale=alpha)
        dx = _mm(dpp, w["wp_t"], out_dtype=F32, name="proj_poolconv_bwd_x", add=dx)
        dx = _mm(dpq, w["wq_t"], out_dtype=F32, name="proj_qkv_bwd_x", add=dx)
        dw_g = _mm_tn(sv["x0"], dpg, name="proj_gate_bwd_w", tka=D, tn=512, nb=6)
        dw_p = _mm_tn(sv["x0"], dpp, name="proj_poolconv_bwd_w", tka=D, tn=512, nb=2)
        dw_q = _mm_tn(sv["x0"], dpq, name="proj_qkv_bwd_w", tka=D, tn=384, nb=2)
        dw_in = jnp.concatenate([dw_p[:, 0:256], dw_q, dw_p[:, 256:1024], dw_g], axis=1)
        dw_pool = jnp.stack([dwbd[64 * g:64 * (g + 1), 64 * g:64 * (g + 1)] for g in range(4)])
        gw[l] = dict(w_in=dw_in, a=dw_a, b=dw_b, c=dw_c, w_o=dw_o, w_up=dw_up, w_down=dw_dn,
                     w_pool=dw_pool, scale=dscale, sinks=dsink[:, 0], conv=dconv, fcw=_deinterleave_cols(dfcw, NJ),
                     g1=dg1, b1=db1, g2=dg2, b2=db2)
        p_l = pack_grads(gw[l])
        pair_sum[l] = _sum_sibling(p_l, _rs_sibling(p_l, "rs_sibling"), my_c, "rs_sum_sibling")
        dy = dx
    grad_x = dy.reshape(1, S, D)
    from_chips[0] = _rs_chips(pair_sum[0], "rs_chips_last")
    g_layers = [_sum_chips(pair_sum[l], from_chips[l], my_chip, "rs_sum_chips") for l in range(L)]

    def stack(k):
        return jnp.stack([gw[l][k] for l in range(L)])

    rep_vec = jnp.concatenate([
        stack("w_pool").reshape(-1), stack("scale").reshape(-1), stack("g1").reshape(-1), stack("b1").reshape(-1),
        stack("g2").reshape(-1), stack("b2").reshape(-1)])
    n_rep_full = -(-rep_vec.shape[0] // LANES)
    sinks_row = jnp.pad(stack("sinks").reshape(-1), (0, LANES - 8 * L))
    rep_vec = jnp.concatenate([_pad_rows(rep_vec, n_rep_full).reshape(-1), sinks_row, sq_lanes.reshape(-1)])
    loss_row = n_rep_full + 1
    n_rep = -(-(loss_row + 1) // 8) * 8
    rep_rows = _pad_rows(rep_vec, n_rep)
    dconv_by_dev = stack("conv").reshape(L, 3, N_DEV, -1).transpose(2, 0, 1, 3).reshape(N_DEV, -1)
    dfcw_by_dev = stack("fcw").reshape(L, 3, N_DEV, -1).transpose(2, 0, 1, 3).reshape(N_DEV, -1)
    n_mine = -(-(small_rows) // 8) * 8
    by_dev = jnp.concatenate([dconv_by_dev, dfcw_by_dev], axis=1)
    by_dev = jnp.pad(by_dev, ((0, 0), (0, n_mine * LANES - by_dev.shape[1]))).reshape(N_DEV * n_mine, LANES)
    small_g = _all_gather(jnp.concatenate([rep_rows, by_dev], axis=0), "ag_small_grads")
    rep_sum, mine_sum, loss11 = _small_reduce(small_g, n_rep, n_mine, 1.0 / D, loss_row, "small_reduce")
    loss = loss11[0, 0]

    names_big = ["w_in", "w_branch_a", "w_branch_b", "w_branch_c", "w_o", "w_up", "w_down"]
    ms_big = [m_w_in, m_w_branch_a, m_w_branch_b, m_w_branch_c, m_w_o, m_w_up, m_w_down]
    vs_big = [v_w_in, v_w_branch_a, v_w_branch_b, v_w_branch_c, v_w_o, v_w_up, v_w_down]
    out = {}
    for k, name in enumerate(names_big):
        wk = big[k]
        c2 = wk.shape[-1]
        as2d = lambda a: a.reshape(-1, c2)
        g_nat = jnp.concatenate([g[offs_l[k]:offs_l[k + 1]] for g in g_layers], axis=0).reshape(wk.shape)
        d, mo, vo = _adamw(as2d(wk), as2d(g_nat), as2d(ms_big[k]), as2d(vs_big[k]), name="adamw_" + name)
        out[name] = (g_nat, d.reshape(wk.shape), mo.reshape(wk.shape), vo.reshape(wk.shape))

    def rep_pack(wp_, sc_, g1_, b1_, g2_, b2_, sk_):
        v = jnp.concatenate([wp_.reshape(-1), sc_.reshape(-1), g1_.reshape(-1), b1_.reshape(-1), g2_.reshape(-1),
                             b2_.reshape(-1)])
        return _pad_rows(jnp.concatenate([_pad_rows(v, n_rep_full).reshape(-1), sk_.reshape(-1)]), n_rep)

    def mine_pack(cw_, fw_):
        return _pad_rows(jnp.concatenate([cw_.reshape(-1), fw_.reshape(-1)]), n_mine)

    w_rep = rep_pack(w_pool, pool_scale, ln1_g, ln1_b, ln2_g, ln2_b, attn_sinks)
    m_rep = rep_pack(m_w_pool, m_pool_scale, m_ln1_g, m_ln1_b, m_ln2_g, m_ln2_b, m_attn_sinks)
    v_rep = rep_pack(v_w_pool, v_pool_scale, v_ln1_g, v_ln1_b, v_ln2_g, v_ln2_b, v_attn_sinks)
    g_rep = jnp.concatenate([rep_sum[:loss_row], jnp.zeros((n_rep - loss_row, LANES), F32)], axis=0)
    rep_res = (g_rep,) + tuple(_adamw(w_rep, g_rep, m_rep, v_rep, name="adamw_replicated"))
    w_mine = mine_pack(conv_w, ffn_conv_w)
    mine_res = (mine_sum,) + tuple(_adamw(w_mine, mine_sum, mine_pack(m_conv_w, m_ffn_conv_w),
                                          mine_pack(v_conv_w, v_ffn_conv_w), name="adamw_conv"))

    def rep_unpack(buf):
        flat = buf.reshape(-1)
        res, o = {}, 0
        for nm, ref in (("w_pool", w_pool), ("pool_scale", pool_scale), ("ln1_g", ln1_g), ("ln1_b", ln1_b),
                        ("ln2_g", ln2_g), ("ln2_b", ln2_b)):
            res[nm] = flat[o:o + ref.size].reshape(ref.shape)
            o += ref.size
        o = n_rep_full * LANES
        res["attn_sinks"] = flat[o:o + attn_sinks.size].reshape(attn_sinks.shape)
        return res

    def mine_unpack(buf):
        flat = buf.reshape(-1)
        return {"conv_w": flat[:n_cw].reshape(conv_w.shape),
                "ffn_conv_w": flat[n_cw:n_cw + n_fw].reshape(ffn_conv_w.shape)}

    order = ["w_in", "w_pool", "pool_scale", "attn_sinks", "conv_w", "w_branch_a", "w_branch_b", "w_branch_c", "w_o",
             "ln1_g", "ln1_b", "w_up", "ffn_conv_w", "w_down", "ln2_g", "ln2_b"]
    results = [loss, grad_x]
    for kind in range(4):
        rep_k, mine_k = rep_unpack(rep_res[kind]), mine_unpack(mine_res[kind])
        for nm in order:
            if nm in out:
                results.append(out[nm][kind])
            elif nm in rep_k:
                results.append(rep_k[nm])
            else:
                results.append(mine_k[nm])
    return tuple(results)
```

```python
import functools

import jax
import jax.numpy as jnp
from jax import lax
from jax.experimental import pallas as pl
from jax.experimental.pallas import tpu as pltpu

F32 = jnp.float32
BF16 = jnp.bfloat16

HEAD_DIM = 64
N_Q_HEADS = 8
GROUP = 4
WINDOW = 128
ROT_DIM = 16
ROPE_THETA = 500000.0
POOL_WINDOWS = (2, 4, 8, 16)
LN_EPS = 1e-5
MASK_VALUE = -1e30
ADAM_LR, ADAM_B1, ADAM_B2, ADAM_EPS, ADAM_WD, ADAM_STEP = 0.001, 0.9, 0.999, 1e-08, 0.01, 10

N_DEV = 8
LANES = 1024
HALO = 16
MESH = pl.DeviceIdType.MESH
VMEM_LIMIT = 56 * 1024 * 1024


def _div_tile(n, want, mult=8):
    for t in range(min(want, n) // mult * mult, 0, -mult):
        if n % t == 0:
            return t
    return n


def _cp(*sem):
    return pltpu.CompilerParams(dimension_semantics=sem, vmem_limit_bytes=VMEM_LIMIT)


def _coords():
    return lax.axis_index("x"), lax.axis_index("y"), lax.axis_index("c")


def _all_gather(xs, name):
    R, C = xs.shape

    def body(x_ref, out_ref, send_sems, recv_sems, local_sem):
        _ag_start(x_ref, out_ref, send_sems, recv_sems, local_sem)
        _ag_finish(x_ref, out_ref, send_sems, recv_sems, local_sem)

    return pl.pallas_call(
        body, name=name,
        out_shape=jax.ShapeDtypeStruct((N_DEV, R, C), xs.dtype),
        in_specs=[pl.BlockSpec(memory_space=pl.ANY)],
        out_specs=pl.BlockSpec(memory_space=pl.ANY),
        scratch_shapes=_AG_SEMS,
    )(xs)


_AG_SEMS = [pltpu.SemaphoreType.DMA((7,)), pltpu.SemaphoreType.DMA((7,)), pltpu.SemaphoreType.DMA(())]


def _ag_copies(x_ref, out_ref, send_sems, recv_sems, local_sem):
    x, y, c = _coords()
    me, sibling = (x, y, c), (x, y, 1 - c)
    chips = [(1 - x, y), (x, 1 - y), (1 - x, 1 - y)]

    def slot(px, py, pc):
        return out_ref.at[4 * px + 2 * py + pc]

    def copy(k, block, to, src=None):
        return pltpu.make_async_remote_copy(
            src_ref=slot(*block) if src is None else src, dst_ref=slot(*block),
            send_sem=send_sems.at[k], recv_sem=recv_sems.at[k], device_id=to, device_id_type=MESH)

    mine = pltpu.make_async_copy(x_ref, slot(*me), local_sem)
    first = [copy(0, me, sibling, src=x_ref)]
    first += [copy(1 + j, me, (*chip, c), src=x_ref) for j, chip in enumerate(chips)]
    passed = [copy(4 + j, (*chip, c), sibling) for j, chip in enumerate(chips)]
    from_chips = [copy(1 + j, (*chip, c), me) for j, chip in enumerate(chips)]
    from_sibling = [copy(0, sibling, me)] + [copy(4 + j, (*chip, 1 - c), me) for j, chip in enumerate(chips)]
    return mine, first, passed, from_chips, from_sibling


def _ag_start(*refs):
    mine, first, _, _, _ = _ag_copies(*refs)
    mine.start()
    for cp in first:
        cp.start()


def _ag_finish(*refs):
    mine, first, passed, from_chips, from_sibling = _ag_copies(*refs)
    for j in range(3):
        from_chips[j].wait_recv()
        passed[j].start()
    for cp in from_sibling:
        cp.wait_recv()
    for cp in first + passed:
        cp.wait_send()
    mine.wait()


def _rs_sibling(p, name):
    _, R, C = p.shape

    def body(p_ref, out_ref, send_sems, recv_sems):
        x, y, c = _coords()
        copies = []
        for j in range(4):
            cx, cy = j // 2, j % 2
            copies.append(pltpu.make_async_remote_copy(
                src_ref=p_ref.at[4 * cx + 2 * cy + (1 - c)], dst_ref=out_ref.at[j],
                send_sem=send_sems.at[j], recv_sem=recv_sems.at[j], device_id=(x, y, 1 - c), device_id_type=MESH))
        for cp in copies:
            cp.start()
        for cp in copies:
            cp.wait_recv()
        for cp in copies:
            cp.wait_send()

    return pl.pallas_call(
        body, name=name,
        out_shape=jax.ShapeDtypeStruct((4, R, C), p.dtype),
        in_specs=[pl.BlockSpec(memory_space=pl.ANY)],
        out_specs=pl.BlockSpec(memory_space=pl.ANY),
        scratch_shapes=[pltpu.SemaphoreType.DMA((4,)), pltpu.SemaphoreType.DMA((4,))],
    )(p)


def _rs_chips(q, name):
    _, R, C = q.shape

    def body(q_ref, out_ref, send_sems, recv_sems):
        _rs_chips_start(q_ref, out_ref, send_sems, recv_sems)
        _rs_chips_finish(q_ref, out_ref, send_sems, recv_sems)

    return pl.pallas_call(
        body, name=name,
        out_shape=jax.ShapeDtypeStruct((3, R, C), q.dtype),
        in_specs=[pl.BlockSpec(memory_space=pl.ANY)],
        out_specs=pl.BlockSpec(memory_space=pl.ANY),
        scratch_shapes=_RS_SEMS,
    )(q)


_RS_SEMS = [pltpu.SemaphoreType.DMA((3,)), pltpu.SemaphoreType.DMA((3,))]


def _rs_chips_copies(q_ref, out_ref, send_sems, recv_sems):
    x, y, c = _coords()
    chips = [(1 - x, y), (x, 1 - y), (1 - x, 1 - y)]
    return [pltpu.make_async_remote_copy(
        src_ref=q_ref.at[2 * cx + cy], dst_ref=out_ref.at[k],
        send_sem=send_sems.at[k], recv_sem=recv_sems.at[k], device_id=(cx, cy, c), device_id_type=MESH)
        for k, (cx, cy) in enumerate(chips)]


def _rs_chips_start(*refs):
    for cp in _rs_chips_copies(*refs):
        cp.start()


def _rs_chips_finish(*refs):
    copies = _rs_chips_copies(*refs)
    for cp in copies:
        cp.wait_recv()
    for cp in copies:
        cp.wait_send()


def _sum_sibling(p, recv, my_c, name, tr=512):
    _, R, C = p.shape
    tr = _div_tile(R, tr, 16)

    def body(c_ref, p_ref, r_ref, o_ref):
        o_ref[...] = (p_ref[...].astype(F32) + r_ref[...].astype(F32)).astype(o_ref.dtype)

    grid_spec = pltpu.PrefetchScalarGridSpec(
        num_scalar_prefetch=1, grid=(4, R // tr),
        in_specs=[pl.BlockSpec((1, tr, C), lambda j, r, c_ref: (4 * (j // 2) + 2 * (j % 2) + c_ref[0], r, 0)),
                  pl.BlockSpec((1, tr, C), lambda j, r, c_ref: (j, r, 0))],
        out_specs=pl.BlockSpec((1, tr, C), lambda j, r, c_ref: (j, r, 0)))
    return pl.pallas_call(body, name=name, grid_spec=grid_spec,
                          out_shape=jax.ShapeDtypeStruct((4, R, C), p.dtype),
                          compiler_params=_cp("parallel", "parallel"))(my_c, p, recv)


def _sum_chips(q, recv, my_chip, name, tr=512):
    _, R, C = q.shape
    tr = _div_tile(R, tr, 16)

    def body(i_ref, q_ref, r_ref, o_ref):
        acc = q_ref[0].astype(F32)
        for k in range(3):
            acc = acc + r_ref[k].astype(F32)
        o_ref[...] = acc

    grid_spec = pltpu.PrefetchScalarGridSpec(
        num_scalar_prefetch=1, grid=(R // tr,),
        in_specs=[pl.BlockSpec((1, tr, C), lambda r, i_ref: (i_ref[0], r, 0)),
                  pl.BlockSpec((3, tr, C), lambda r, i_ref: (0, r, 0))],
        out_specs=pl.BlockSpec((tr, C), lambda r, i_ref: (r, 0)))
    return pl.pallas_call(body, name=name, grid_spec=grid_spec,
                          out_shape=jax.ShapeDtypeStruct((R, C), F32),
                          compiler_params=_cp("parallel"))(my_chip, q, recv)


def _small_reduce(g, n_rep, n_mine, inv_d, loss_row, name):
    _, R, C = g.shape

    def body(g_ref, rep_ref, mine_ref, loss_ref):
        x, y, c = _coords()
        start = pl.multiple_of(n_rep + (4 * x + 2 * y + c) * n_mine, 8)
        rep = g_ref[0, 0:n_rep, :]
        mine = g_ref[0, pl.ds(start, n_mine), :]
        sq = g_ref[0, loss_row:loss_row + 1, :]
        for d in range(1, N_DEV):
            rep = rep + g_ref[d, 0:n_rep, :]
            mine = mine + g_ref[d, pl.ds(start, n_mine), :]
            sq = sq + g_ref[d, loss_row:loss_row + 1, :]
        rep_ref[...] = rep
        mine_ref[...] = mine
        loss_ref[...] = (0.5 * inv_d) * jnp.sum(sq, axis=1, keepdims=True)

    return pl.pallas_call(
        body, name=name,
        out_shape=(jax.ShapeDtypeStruct((n_rep, C), F32), jax.ShapeDtypeStruct((n_mine, C), F32),
                   jax.ShapeDtypeStruct((1, 1), F32)),
        compiler_params=pltpu.CompilerParams(vmem_limit_bytes=VMEM_LIMIT),
    )(g)


def _mm(a, b, *, out_dtype, name, tm=512, tn=None, tk=None, add=None, add_scale=1.0, gather=None):
    M, K = a.shape
    N = b.shape[1]
    tm = min(tm, M)
    tn = N if tn is None else tn
    tk = K if tk is None else tk
    nk = K // tk
    has_add = add is not None
    has_ag = gather is not None
    n_i, n_j = M // tm, N // tn

    def body(*refs):
        a_ref, b_ref = refs[0], refs[1]
        add_ref = refs[2] if has_add else None
        n_in = 2 + has_add + has_ag
        o_ref = refs[n_in]
        if has_ag:
            ag_refs = (refs[n_in - 1], refs[n_in + 1]) + tuple(refs[n_in + 2:n_in + 5])
            pid = (pl.program_id(0), pl.program_id(1), pl.program_id(2))

            @pl.when((pid[0] == 0) & (pid[1] == 0) & (pid[2] == 0))
            def _():
                _ag_start(*ag_refs)

        part = jnp.dot(a_ref[...].astype(BF16), b_ref[...].astype(BF16), preferred_element_type=F32)

        def finish(r):
            if has_add:
                r = r + add_scale * add_ref[...].astype(F32)
            o_ref[...] = r.astype(out_dtype)

        if nk == 1:
            finish(part)
        else:
            acc_ref = refs[-1]
            k = pl.program_id(2)

            @pl.when(k == 0)
            def _():
                acc_ref[...] = part

            @pl.when(k > 0)
            def _():
                acc_ref[...] += part

            @pl.when(k == nk - 1)
            def _():
                finish(acc_ref[...])

        if has_ag:
            @pl.when((pid[0] == n_i - 1) & (pid[1] == n_j - 1) & (pid[2] == nk - 1))
            def _():
                _ag_finish(*ag_refs)

    in_specs = [pl.BlockSpec((tm, tk), lambda i, j, k: (i, k)), pl.BlockSpec((tk, tn), lambda i, j, k: (k, j))]
    args = [a, b]
    if has_add:
        in_specs.append(pl.BlockSpec((tm, tn), lambda i, j, k: (i, j)))
        args.append(add)
    out_specs = [pl.BlockSpec((tm, tn), lambda i, j, k: (i, j))]
    out_shape = [jax.ShapeDtypeStruct((M, N), out_dtype)]
    scratch = []
    if has_ag:
        in_specs.append(pl.BlockSpec(memory_space=pl.ANY))
        args.append(gather)
        out_specs.append(pl.BlockSpec(memory_space=pl.ANY))
        out_shape.append(jax.ShapeDtypeStruct((N_DEV,) + gather.shape, gather.dtype))
        scratch += _AG_SEMS
    if nk > 1:
        scratch.append(pltpu.VMEM((tm, tn), F32))
    sem = ("arbitrary",) * 3 if has_ag else ("parallel", "parallel", "arbitrary")
    res = pl.pallas_call(
        body, name=name, grid=(n_i, n_j, nk), in_specs=in_specs, out_specs=out_specs, out_shape=out_shape,
        scratch_shapes=scratch, compiler_params=_cp(*sem),
    )(*args)
    return tuple(res) if has_ag else res[0]


def _mm_ln(a, b, resid, gamma, beta, *, alpha, name, tm=512, tk=None):
    M, K = a.shape
    D = b.shape[1]
    tm = min(tm, M)
    tk = K if tk is None else tk
    nk = K // tk

    def body(a_ref, b_ref, r_ref, g_ref, be_ref, y_ref, xh_ref, rs_ref, *scratch):
        part = jnp.dot(a_ref[...].astype(BF16), b_ref[...].astype(BF16), preferred_element_type=F32)

        def finish(acc):
            z = alpha * r_ref[...] + acc
            mu = jnp.mean(z, axis=-1, keepdims=True)
            zc = z - mu
            var = jnp.mean(zc * zc, axis=-1, keepdims=True)
            rstd = lax.rsqrt(var + LN_EPS)
            xhat = zc * rstd
            y_ref[...] = xhat * g_ref[...] + be_ref[...]
            xh_ref[...] = xhat.astype(BF16)
            rs_ref[...] = rstd

        if nk == 1:
            finish(part)
        else:
            acc_ref = scratch[0]
            k = pl.program_id(1)

            @pl.when(k == 0)
            def _():
                acc_ref[...] = part

            @pl.when(k > 0)
            def _():
                acc_ref[...] += part

            @pl.when(k == nk - 1)
            def _():
                finish(acc_ref[...])

    row = lambda i, k: (i, 0)
    vec = lambda i, k: (0, 0)
    return pl.pallas_call(
        body, name=name, grid=(M // tm, nk),
        in_specs=[pl.BlockSpec((tm, tk), lambda i, k: (i, k)), pl.BlockSpec((tk, D), lambda i, k: (k, 0)),
                  pl.BlockSpec((tm, D), row), pl.BlockSpec((1, D), vec), pl.BlockSpec((1, D), vec)],
        out_specs=[pl.BlockSpec((tm, D), row), pl.BlockSpec((tm, D), row), pl.BlockSpec((tm, 1), row)],
        out_shape=(jax.ShapeDtypeStruct((M, D), F32), jax.ShapeDtypeStruct((M, D), BF16),
                   jax.ShapeDtypeStruct((M, 1), F32)),
        scratch_shapes=[pltpu.VMEM((tm, D), F32)] if nk > 1 else [],
        compiler_params=_cp("parallel", "arbitrary"),
    )(a, b, resid, gamma, beta)


def _mm_tn(a, b, *, name, tka, tn, a_off=0, na=1, b_off=0, nb=1, ts=2048):
    S = a.shape[0]
    ts = min(ts, S)

    def body(a_ref, b_ref, o_ref):
        s = pl.program_id(2)
        part = lax.dot_general(a_ref[...].astype(BF16), b_ref[...].astype(BF16),
                               (((0,), (0,)), ((), ())), preferred_element_type=F32)

        @pl.when(s == 0)
        def _():
            o_ref[...] = part

        @pl.when(s > 0)
        def _():
            o_ref[...] += part

    return pl.pallas_call(
        body, name=name, grid=(na, nb, S // ts),
        in_specs=[pl.BlockSpec((ts, tka), lambda i, j, s: (s, a_off + i)),
                  pl.BlockSpec((ts, tn), lambda i, j, s: (s, b_off + j))],
        out_specs=pl.BlockSpec((tka, tn), lambda i, j, s: (i, j)),
        out_shape=jax.ShapeDtypeStruct((na * tka, nb * tn), F32),
        compiler_params=_cp("parallel", "parallel", "arbitrary"),
    )(a, b)


def _rope_tables(pos, inv_lane, sign_lane, name, ts=512):
    S = pos.shape[0]
    ts = min(ts, S)

    def body(p_ref, inv_ref, sg_ref, cos_ref, sin_ref):
        ang = p_ref[...].astype(F32) * inv_ref[...]
        cos_ref[...] = jnp.cos(ang)
        sin_ref[...] = jnp.sin(ang) * sg_ref[...]

    return pl.pallas_call(
        body, name=name, grid=(S // ts,),
        in_specs=[pl.BlockSpec((ts, 1), lambda i: (i, 0)), pl.BlockSpec((1, 128), lambda i: (0, 0)),
                  pl.BlockSpec((1, 128), lambda i: (0, 0))],
        out_specs=[pl.BlockSpec((ts, 128), lambda i: (i, 0))] * 2,
        out_shape=(jax.ShapeDtypeStruct((S, 128), F32),) * 2,
        compiler_params=_cp("parallel"),
    )(pos, inv_lane, sign_lane)


def _rope_swap(t):
    lane = lax.broadcasted_iota(jnp.int32, (1, 128), 1)
    lo = (lane % HEAD_DIM) < (ROT_DIM // 2)
    return jnp.where(lo, pltpu.roll(t, 128 - ROT_DIM // 2, 1), pltpu.roll(t, ROT_DIM // 2, 1))


def _rope_fwd(t, cos, sin):
    return t * cos + _rope_swap(t) * sin


def _rope_bwd(d, cos, sin):
    lane = lax.broadcasted_iota(jnp.int32, (1, 128), 1)
    return d * cos + jnp.where((lane % HEAD_DIM) < ROT_DIM, _rope_swap(d * sin), 0.0)


def _tile_heads(t):
    lane = lax.broadcasted_iota(jnp.int32, (1, 128), 1)
    r = pltpu.roll(t, 64, 1)
    h0 = jnp.where(lane < 64, t, r)
    h1 = jnp.where(lane < 64, r, t)
    return jnp.concatenate([h0, h0], axis=1), jnp.concatenate([h1, h1], axis=1)


def _fold_heads(d0, d1):
    lane = lax.broadcasted_iota(jnp.int32, (1, 128), 1)

    def fold(d):
        s = d[:, 0:128] + d[:, 128:256]
        return s + pltpu.roll(s, 64, 1)

    return jnp.where(lane < 64, fold(d0), fold(d1))


def _band4(n_keys):
    row = lax.broadcasted_iota(jnp.int32, (GROUP * WINDOW, n_keys), 0) % WINDOW
    col = lax.broadcasted_iota(jnp.int32, (GROUP * WINDOW, n_keys), 1)
    return (col > row) & (col <= row + WINDOW), col


def _head_masks():
    lane = lax.broadcasted_iota(jnp.int32, (1, GROUP * HEAD_DIM), 1)
    return [(lane // HEAD_DIM) == hl for hl in range(GROUP)]


def _stack_heads(t):
    zero = jnp.zeros_like(t)
    return jnp.concatenate([jnp.where(hm, t, zero) for hm in _head_masks()], axis=0)


def _unstack_heads(t4):
    out = None
    for hl, hm in enumerate(_head_masks()):
        part = jnp.where(hm, t4[hl * WINDOW:(hl + 1) * WINDOW], 0.0)
        out = part if out is None else out + part
    return out


def _sink_block(sink_ref, g):
    return jnp.concatenate([jnp.broadcast_to(sink_ref[g * GROUP + hl:g * GROUP + hl + 1, 0:1], (WINDOW, 256))
                            for hl in range(GROUP)], axis=0)


def _sink_column(sink_ref, g):
    return jnp.concatenate([jnp.broadcast_to(sink_ref[g * GROUP + hl:g * GROUP + hl + 1, 0:1], (WINDOW, 1))
                            for hl in range(GROUP)], axis=0)


def _attn_fwd(pq, cos_t, sin_t, sinks_b, *, name, ts=256):
    S = pq.shape[0]
    ts = min(ts, S)
    nq = ts // WINDOW
    scale = HEAD_DIM ** -0.5

    def body(cur_ref, prev_ref, cosc_ref, sinc_ref, cosp_ref, sinp_ref, sink_ref, o_ref, lse_ref):
        i = pl.program_id(0)
        cosc, sinc = cosc_ref[...], sinc_ref[...]
        q = cur_ref[:, 0:512].astype(F32)
        qr = jnp.concatenate(
            [_rope_fwd(q[:, j * 128:(j + 1) * 128], cosc, sinc) for j in range(4)], axis=1) * scale
        qr = qr.astype(BF16)
        kc = _rope_fwd(cur_ref[:, 512:640].astype(F32), cosc, sinc)
        kp = _rope_fwd(prev_ref[:, 0:128].astype(F32), cosp_ref[...], sinp_ref[...])
        k_all = jnp.concatenate([kp, kc], axis=0)
        v_all = jnp.concatenate([prev_ref[:, 128:256].astype(F32), cur_ref[:, 640:768].astype(F32)], axis=0)
        kt = [t.astype(BF16) for t in _tile_heads(k_all)]
        vt = [t.astype(BF16) for t in _tile_heads(v_all)]
        band, col = _band4(2 * WINDOW)
        ones = jnp.ones((2 * WINDOW, 256), BF16)
        key_t = lax.broadcasted_iota(jnp.int32, (2 * WINDOW, GROUP * WINDOW), 0)
        qry_t = lax.broadcasted_iota(jnp.int32, (2 * WINDOW, GROUP * WINDOW), 1) % WINDOW
        band_t = (key_t > qry_t) & (key_t <= qry_t + WINDOW)
        NT = (((1,), (1,)), ((), ()))
        for qb in range(nq):
            rows = slice(qb * WINDOW, (qb + 1) * WINDOW)
            keys = slice(qb * WINDOW, (qb + 2) * WINDOW)
            valid = band & ((col >= WINDOW) | (i * nq + qb > 0))
            valid_t = band_t & ((key_t >= WINDOW) | (i * nq + qb > 0))
            for g in range(2):
                qs = _stack_heads(qr[rows, g * 256:(g + 1) * 256])
                sink = _sink_block(sink_ref, g)
                s = lax.dot_general(qs, kt[g][keys], NT, preferred_element_type=F32)
                s_t = lax.dot_general(kt[g][keys], qs, NT, preferred_element_type=F32)
                m_t = jnp.max(jnp.where(valid_t, s_t, MASK_VALUE), axis=0, keepdims=True)
                m_rep = jnp.broadcast_to(m_t, (WINDOW, GROUP * WINDOW)).T
                m = jnp.maximum(jnp.concatenate([m_rep, m_rep], axis=1), sink)
                e = jnp.exp(jnp.where(valid, s, MASK_VALUE) - m).astype(BF16)
                l = jnp.dot(e, ones, preferred_element_type=F32) + jnp.exp(sink - m)
                pv = jnp.dot(e, vt[g][keys], preferred_element_type=F32)
                o_ref[rows, g * 256:(g + 1) * 256] = (_unstack_heads(pv) / _unstack_heads(l)).astype(BF16)
                lse4 = (m + jnp.log(l))[:, 0:1]
                for hl in range(GROUP):
                    h = g * GROUP + hl
                    lse_ref[rows, h:h + 1] = lse4[hl * WINDOW:(hl + 1) * WINDOW]

    hb = ts // WINDOW
    cur = lambda i: (i, 0)
    prev = lambda i: (jnp.maximum(i * hb - 1, 0), 0)
    return pl.pallas_call(
        body, name=name, grid=(S // ts,),
        in_specs=[pl.BlockSpec((ts, 768), cur),
                  pl.BlockSpec((WINDOW, 256), lambda i: (jnp.maximum(i * hb - 1, 0), 2)),
                  pl.BlockSpec((ts, 128), cur), pl.BlockSpec((ts, 128), cur),
                  pl.BlockSpec((WINDOW, 128), prev), pl.BlockSpec((WINDOW, 128), prev),
                  pl.BlockSpec((8, 128), lambda i: (0, 0))],
        out_specs=[pl.BlockSpec((ts, 512), cur), pl.BlockSpec((ts, 8), cur)],
        out_shape=(jax.ShapeDtypeStruct((S, 512), BF16), jax.ShapeDtypeStruct((S, 8), F32)),
        compiler_params=_cp("parallel"),
    )(pq, pq, cos_t, sin_t, cos_t, sin_t, sinks_b)


def _attn_bwd(pq, cos_t, sin_t, sinks_b, do, o, lse, *, name, ts=256):
    S = pq.shape[0]
    ts = min(ts, S)
    nq = ts // WINDOW
    nt = S // ts
    scale = HEAD_DIM ** -0.5
    NT = (((1,), (1,)), ((), ()))
    TN = (((0,), (0,)), ((), ()))

    def body(cur_ref, prev_ref, nxt_ref, cosc_ref, sinc_ref, cosp_ref, sinp_ref, cosn_ref, sinn_ref, sink_ref,
             doc_ref, don_ref, oc_ref, on_ref, lsec_ref, lsen_ref, dpq_ref, dsink_ref):
        i = pl.program_id(0)
        last = i == nt - 1
        cosc, sinc = cosc_ref[...], sinc_ref[...]
        cose = jnp.concatenate([cosc, cosn_ref[...]], axis=0)
        sine = jnp.concatenate([sinc, sinn_ref[...]], axis=0)
        q = jnp.concatenate([cur_ref[:, 0:512], nxt_ref[:, 0:512]], axis=0).astype(F32)
        qr = jnp.concatenate(
            [_rope_fwd(q[:, j * 128:(j + 1) * 128], cose, sine) for j in range(4)], axis=1) * scale
        qr = qr.astype(BF16)
        kc = _rope_fwd(cur_ref[:, 512:640].astype(F32), cosc, sinc)
        kp = _rope_fwd(prev_ref[:, 0:128].astype(F32), cosp_ref[...], sinp_ref[...])
        k_all = jnp.concatenate([kp, kc], axis=0)
        v_all = jnp.concatenate([prev_ref[:, 128:256].astype(F32), cur_ref[:, 640:768].astype(F32)], axis=0)
        kt = [t.astype(BF16) for t in _tile_heads(k_all)]
        vt = [t.astype(BF16) for t in _tile_heads(v_all)]
        don = jnp.where(last, jnp.zeros_like(don_ref[...]), don_ref[...])
        do_e = jnp.concatenate([doc_ref[...], don], axis=0)
        o_e = jnp.concatenate([oc_ref[...], on_ref[...]], axis=0)
        band2, col2 = _band4(2 * WINDOW)
        band1, _ = _band4(WINDOW)
        ones = jnp.ones((256, 256), BF16)

        @pl.when(i == 0)
        def _():
            dsink_ref[...] = jnp.zeros_like(dsink_ref)

        dk_acc = [[None] * (nq + 1) for _ in range(2)]
        dv_acc = [[None] * (nq + 1) for _ in range(2)]

        def add(acc, g, e, val):
            acc[g][e] = val if acc[g][e] is None else acc[g][e] + val

        for qb in range(nq + 1):
            halo = qb == nq
            rows = slice(qb * WINDOW, (qb + 1) * WINDOW)
            if halo:
                keys = slice(qb * WINDOW, (qb + 1) * WINDOW)
                valid = band1 & jnp.logical_not(last)
            else:
                keys = slice(qb * WINDOW, (qb + 2) * WINDOW)
                valid = band2 & ((col2 >= WINDOW) | (i * nq + qb > 0))
            dq_parts = []
            for g in range(2):
                qs = _stack_heads(qr[rows, g * 256:(g + 1) * 256])
                dos = _stack_heads(do_e[rows, g * 256:(g + 1) * 256])
                o_g = o_e[rows, g * 256:(g + 1) * 256].astype(F32)
                kt_b, vt_b = kt[g][keys], vt[g][keys]
                lse_src = lsen_ref if halo else lsec_ref
                lse_rows = slice(0, WINDOW) if halo else rows
                big_l = jnp.concatenate([lse_src[lse_rows, g * GROUP + hl:g * GROUP + hl + 1] for hl in range(GROUP)],
                                        axis=0)
                delta = jnp.dot((dos.astype(F32) * jnp.concatenate([o_g] * GROUP, axis=0)).astype(BF16), ones,
                                preferred_element_type=F32)
                s = lax.dot_general(qs, kt_b, NT, preferred_element_type=F32)
                p = jnp.exp(jnp.where(valid, s, MASK_VALUE) - big_l)
                dp = lax.dot_general(dos, vt_b, NT, preferred_element_type=F32)
                ds = (p * (dp - delta[:, 0:p.shape[1]])).astype(BF16)
                dk_g = lax.dot_general(ds, qs, TN, preferred_element_type=F32)
                dv_g = lax.dot_general(p.astype(BF16), dos, TN, preferred_element_type=F32)
                if not halo:
                    dq_parts.append(_unstack_heads(jnp.dot(ds, kt_b, preferred_element_type=F32)))
                    dsink4 = jnp.exp(_sink_column(sink_ref, g) - big_l) * delta[:, 0:1]
                    for hl in range(GROUP):
                        h = g * GROUP + hl
                        dsink_h = -jnp.sum(dsink4[hl * WINDOW:(hl + 1) * WINDOW], axis=0, keepdims=True)
                        dsink_ref[h:h + 1, :] += jnp.broadcast_to(dsink_h, (1, 128))
                add(dk_acc, g, qb, dk_g[0:WINDOW])
                add(dv_acc, g, qb, dv_g[0:WINDOW])
                if not halo:
                    add(dk_acc, g, qb + 1, dk_g[WINDOW:2 * WINDOW])
                    add(dv_acc, g, qb + 1, dv_g[WINDOW:2 * WINDOW])
            if not halo:
                cs, sn = cosc[rows], sinc[rows]
                for g in range(2):
                    dq_g = dq_parts[g] * scale
                    for j in range(2):
                        c0 = g * 256 + j * 128
                        dpq_ref[rows, c0:c0 + 128] = _rope_bwd(dq_g[:, j * 128:(j + 1) * 128], cs, sn).astype(BF16)
        for e in range(1, nq + 1):
            rows = slice((e - 1) * WINDOW, e * WINDOW)
            dk = _fold_heads(dk_acc[0][e], dk_acc[1][e])
            dv = _fold_heads(dv_acc[0][e], dv_acc[1][e])
            dpq_ref[rows, 512:640] = _rope_bwd(dk, cosc[rows], sinc[rows]).astype(BF16)
            dpq_ref[rows, 640:768] = dv.astype(BF16)

    hb = ts // WINDOW
    nblk = S // WINDOW
    cur = lambda i: (i, 0)
    prev = lambda i: (jnp.maximum(i * hb - 1, 0), 0)
    nxt = lambda i: (jnp.minimum((i + 1) * hb, nblk - 1), 0)
    return pl.pallas_call(
        body, name=name, grid=(nt,),
        in_specs=[pl.BlockSpec((ts, 768), cur),
                  pl.BlockSpec((WINDOW, 256), lambda i: (jnp.maximum(i * hb - 1, 0), 2)),
                  pl.BlockSpec((WINDOW, 768), nxt),
                  pl.BlockSpec((ts, 128), cur), pl.BlockSpec((ts, 128), cur),
                  pl.BlockSpec((WINDOW, 128), prev), pl.BlockSpec((WINDOW, 128), prev),
                  pl.BlockSpec((WINDOW, 128), nxt), pl.BlockSpec((WINDOW, 128), nxt),
                  pl.BlockSpec((8, 128), lambda i: (0, 0)),
                  pl.BlockSpec((ts, 512), cur), pl.BlockSpec((WINDOW, 512), nxt),
                  pl.BlockSpec((ts, 512), cur), pl.BlockSpec((WINDOW, 512), nxt),
                  pl.BlockSpec((ts, 8), cur), pl.BlockSpec((WINDOW, 8), nxt)],
        out_specs=[pl.BlockSpec((ts, 768), cur), pl.BlockSpec((8, 128), lambda i: (0, 0))],
        out_shape=(jax.ShapeDtypeStruct((S, 768), BF16), jax.ShapeDtypeStruct((8, 128), F32)),
        compiler_params=_cp("arbitrary"),
    )(pq, pq, pq, cos_t, sin_t, cos_t, sin_t, cos_t, sin_t, sinks_b, do, do, o, o, lse, lse)


def _shift_dn(x, k):
    return pltpu.roll(x, k, 0)


def _shift_up(x, k):
    return pltpu.roll(x, x.shape[0] - k, 0)


def _pool_lane_select(vals):
    lane = lax.broadcasted_iota(jnp.int32, (1, 256), 1)
    out = vals[3]
    for g in (2, 1, 0):
        out = jnp.where(lane < 64 * (g + 1), vals[g], out)
    return out


def _pool_inv_count(t0, n):
    t = t0 + lax.broadcasted_iota(jnp.int32, (n, 256), 0)
    lane = lax.broadcasted_iota(jnp.int32, (n, 256), 1)
    w = jnp.where(lane < 64, 2, jnp.where(lane < 128, 4, jnp.where(lane < 192, 8, 16)))
    return 1.0 / jnp.minimum(t + 1, w).astype(F32)


def _pooled(u_ext, t0, n):
    s2 = u_ext + _shift_dn(u_ext, 1)
    s4 = s2 + _shift_dn(s2, 2)
    s8 = s4 + _shift_dn(s4, 4)
    s16 = s8 + _shift_dn(s8, 8)
    win = _pool_lane_select([s2, s4, s8, s16])[HALO:HALO + n]
    return win * _pool_inv_count(t0, n) - u_ext[HALO:HALO + n]


def _poolconv_fwd(pp, wbd, pool_scale, conv_w, *, name, ts=512):
    S = pp.shape[0]
    ts = min(ts, S)

    def body(cur_ref, prev_ref, wbd_ref, sc_ref, cw_ref, oa_ref, oc_ref):
        i = pl.program_id(0)
        prev = jnp.where(i > 0, prev_ref[...].astype(F32), 0.0)
        u_ext = jnp.concatenate([prev[:, 0:256], cur_ref[:, 0:256].astype(F32)], axis=0)
        pooled = _pooled(u_ext, i * ts, ts)
        mixed = jnp.dot(pooled.astype(BF16), wbd_ref[...], preferred_element_type=F32)
        oa_ref[...] = (mixed * sc_ref[...]).astype(BF16)
        v_ext = jnp.concatenate([prev[:, 256:512] * prev[:, 768:1024],
                                 cur_ref[:, 256:512].astype(F32) * cur_ref[:, 768:1024].astype(F32)], axis=0)
        cv = cw_ref[2:3, :] * v_ext + cw_ref[1:2, :] * _shift_dn(v_ext, 1) + cw_ref[0:1, :] * _shift_dn(v_ext, 2)
        oc_ref[...] = (cur_ref[:, 512:768].astype(F32) * cv[HALO:HALO + ts]).astype(BF16)

    hb = ts // HALO
    cur = lambda i: (i, 0)
    const = lambda i: (0, 0)
    return pl.pallas_call(
        body, name=name, grid=(S // ts,),
        in_specs=[pl.BlockSpec((ts, 1024), cur),
                  pl.BlockSpec((HALO, 1024), lambda i: (jnp.maximum(i * hb - 1, 0), 0)),
                  pl.BlockSpec((256, 256), const), pl.BlockSpec((1, 256), const), pl.BlockSpec((3, 256), const)],
        out_specs=[pl.BlockSpec((ts, 256), cur)] * 2,
        out_shape=(jax.ShapeDtypeStruct((S, 256), BF16),) * 2,
        compiler_params=_cp("parallel"),
    )(pp, pp, wbd, pool_scale, conv_w)


def _poolconv_bwd(pp, do_a, do_c, wbd, wbd_t, pool_scale, conv_w, *, name, ts=512):
    S = pp.shape[0]
    ts = min(ts, S)
    nt = S // ts
    n_e = ts + 2 * HALO

    def body(cur_ref, prev_ref, nxt_ref, dac_ref, dan_ref, dcc_ref, dcn_ref, wbd_ref, wbdt_ref, sc_ref, cw_ref,
             dpp_ref, pooled_ref, dmixed_ref, dsc_ref, dcw_ref):
        i = pl.program_id(0)

        @pl.when(i == 0)
        def _():
            dsc_ref[...] = jnp.zeros_like(dsc_ref)
            dcw_ref[...] = jnp.zeros_like(dcw_ref)

        prev = jnp.where(i > 0, prev_ref[...].astype(F32), 0.0)
        nxt = nxt_ref[...].astype(F32)
        cur = cur_ref[...].astype(F32)
        not_last = i < nt - 1
        da_n = jnp.where(not_last, dan_ref[...].astype(F32), 0.0)
        dc_n = jnp.where(not_last, dcn_ref[...].astype(F32), 0.0)
        zeros_h = jnp.zeros((HALO, 256), F32)
        sc = sc_ref[...]

        u_ext = jnp.concatenate([prev[:, 0:256], cur[:, 0:256]], axis=0)
        pooled = _pooled(u_ext, i * ts, ts)
        pooled_b = pooled.astype(BF16)
        pooled_ref[...] = pooled_b
        mixed = jnp.dot(pooled_b, wbd_ref[...], preferred_element_type=F32)
        da_c = dac_ref[...].astype(F32)
        dsc_ref[...] += jnp.sum(da_c * mixed, axis=0, keepdims=True)
        dmixed_e = jnp.concatenate([da_c, da_n], axis=0) * sc
        dmixed_ref[...] = dmixed_e[0:ts].astype(BF16)
        dpooled = jnp.dot(dmixed_e.astype(BF16), wbdt_ref[...], preferred_element_type=F32)
        qd = dpooled * _pool_inv_count(i * ts, ts + HALO)
        f2 = qd + _shift_up(qd, 1)
        f4 = f2 + _shift_up(f2, 2)
        f8 = f4 + _shift_up(f4, 4)
        f16 = f8 + _shift_up(f8, 8)
        du = (_pool_lane_select([f2, f4, f8, f16]) - dpooled)[0:ts]
        dpp_ref[:, 0:256] = du.astype(BF16)

        xc_e = jnp.concatenate([prev[:, 256:512], cur[:, 256:512], nxt[:, 256:512]], axis=0)
        gc_e = jnp.concatenate([prev[:, 768:1024], cur[:, 768:1024], nxt[:, 768:1024]], axis=0)
        gb_e = jnp.concatenate([zeros_h, cur[:, 512:768], nxt[:, 512:768]], axis=0)
        dc_e = jnp.concatenate([zeros_h, dcc_ref[...].astype(F32), dc_n], axis=0)
        v_e = xc_e * gc_e
        v1, v2 = _shift_dn(v_e, 1), _shift_dn(v_e, 2)
        w0, w1, w2 = cw_ref[0:1, :], cw_ref[1:2, :], cw_ref[2:3, :]
        cv = w2 * v_e + w1 * v1 + w0 * v2
        dcv = dc_e * gb_e
        dv = w2 * dcv + w1 * _shift_up(dcv, 1) + w0 * _shift_up(dcv, 2)
        tile = slice(HALO, HALO + ts)
        dpp_ref[:, 256:512] = (dv * gc_e)[tile].astype(BF16)
        dpp_ref[:, 512:768] = (dc_e * cv)[tile].astype(BF16)
        dpp_ref[:, 768:1024] = (dv * xc_e)[tile].astype(BF16)
        dcv_t = dcv[tile]
        dcw_ref[0:1, :] += jnp.sum(dcv_t * v2[tile], axis=0, keepdims=True)
        dcw_ref[1:2, :] += jnp.sum(dcv_t * v1[tile], axis=0, keepdims=True)
        dcw_ref[2:3, :] += jnp.sum(dcv_t * v_e[tile], axis=0, keepdims=True)

    hb = ts // HALO
    nblk = S // HALO
    cur = lambda i: (i, 0)
    const = lambda i: (0, 0)
    prev = lambda i: (jnp.maximum(i * hb - 1, 0), 0)
    nxt = lambda i: (jnp.minimum((i + 1) * hb, nblk - 1), 0)
    del n_e
    return pl.pallas_call(
        body, name=name, grid=(nt,),
        in_specs=[pl.BlockSpec((ts, 1024), cur), pl.BlockSpec((HALO, 1024), prev), pl.BlockSpec((HALO, 1024), nxt),
                  pl.BlockSpec((ts, 256), cur), pl.BlockSpec((HALO, 256), nxt),
                  pl.BlockSpec((ts, 256), cur), pl.BlockSpec((HALO, 256), nxt),
                  pl.BlockSpec((256, 256), const), pl.BlockSpec((256, 256), const),
                  pl.BlockSpec((1, 256), const), pl.BlockSpec((3, 256), const)],
        out_specs=[pl.BlockSpec((ts, 1024), cur), pl.BlockSpec((ts, 256), cur), pl.BlockSpec((ts, 256), cur),
                   pl.BlockSpec((1, 256), const), pl.BlockSpec((3, 256), const)],
        out_shape=(jax.ShapeDtypeStruct((S, 1024), BF16), jax.ShapeDtypeStruct((S, 256), BF16),
                   jax.ShapeDtypeStruct((S, 256), BF16), jax.ShapeDtypeStruct((1, 256), F32),
                   jax.ShapeDtypeStruct((3, 256), F32)),
        compiler_params=_cp("arbitrary"),
    )(pp, pp, pp, do_a, do_a, do_c, do_c, wbd, wbd_t, pool_scale, conv_w)


def _sigmoid(x):
    return 1.0 / (1.0 + jnp.exp(-x))


def _merge_fwd(o_a, o_b, o_c, glog, w_br, *, name, ts=512):
    S = o_a.shape[0]
    D = w_br.shape[1]
    ts = min(ts, S)

    def body(oa_ref, ob_ref, oc_ref, gl_ref, w_ref, m_ref):
        pa = jnp.dot(oa_ref[...], w_ref[0:256, :], preferred_element_type=F32)
        pb = jnp.dot(ob_ref[...], w_ref[256:768, :], preferred_element_type=F32)
        pc = jnp.dot(oc_ref[...], w_ref[768:1024, :], preferred_element_type=F32)
        m = _sigmoid(gl_ref[:, 0:D].astype(F32)) * pa
        m = m + _sigmoid(gl_ref[:, D:2 * D].astype(F32)) * pb
        m = m + _sigmoid(gl_ref[:, 2 * D:3 * D].astype(F32)) * pc
        m_ref[...] = m.astype(BF16)

    cur = lambda i: (i, 0)
    return pl.pallas_call(
        body, name=name, grid=(S // ts,),
        in_specs=[pl.BlockSpec((ts, 256), cur), pl.BlockSpec((ts, 512), cur), pl.BlockSpec((ts, 256), cur),
                  pl.BlockSpec((ts, 3 * D), cur), pl.BlockSpec((1024, D), lambda i: (0, 0))],
        out_specs=pl.BlockSpec((ts, D), cur),
        out_shape=jax.ShapeDtypeStruct((S, D), BF16),
        compiler_params=_cp("parallel"),
    )(o_a, o_b, o_c, glog, w_br)


def _merge_bwd(dm, o_a, o_b, o_c, glog, w_br, w_br_t, *, name, ts=256):
    S = o_a.shape[0]
    D = w_br.shape[1]
    ts = min(ts, S)

    def body(dm_ref, oa_ref, ob_ref, oc_ref, gl_ref, w_ref, wt_ref, dgl_ref, dp_ref, doa_ref, dob_ref, doc_ref):
        dmv = dm_ref[...].astype(F32)
        branches = ((oa_ref, 0, 256, doa_ref), (ob_ref, 256, 768, dob_ref), (oc_ref, 768, 1024, doc_ref))
        for b, (o_ref, r0, r1, do_ref) in enumerate(branches):
            prod = jnp.dot(o_ref[...], w_ref[r0:r1, :], preferred_element_type=F32)
            gate = _sigmoid(gl_ref[:, b * D:(b + 1) * D].astype(F32))
            dgl_ref[:, b * D:(b + 1) * D] = (dmv * prod * gate * (1.0 - gate)).astype(BF16)
            dprod = (dmv * gate).astype(BF16)
            dp_ref[:, b * D:(b + 1) * D] = dprod
            do_ref[...] = jnp.dot(dprod, wt_ref[:, r0:r1], preferred_element_type=F32).astype(BF16)

    cur = lambda i: (i, 0)
    const = lambda i: (0, 0)
    return pl.pallas_call(
        body, name=name, grid=(S // ts,),
        in_specs=[pl.BlockSpec((ts, D), cur), pl.BlockSpec((ts, 256), cur), pl.BlockSpec((ts, 512), cur),
                  pl.BlockSpec((ts, 256), cur), pl.BlockSpec((ts, 3 * D), cur),
                  pl.BlockSpec((1024, D), const), pl.BlockSpec((D, 1024), const)],
        out_specs=[pl.BlockSpec((ts, 3 * D), cur), pl.BlockSpec((ts, 3 * D), cur), pl.BlockSpec((ts, 256), cur),
                   pl.BlockSpec((ts, 512), cur), pl.BlockSpec((ts, 256), cur)],
        out_shape=(jax.ShapeDtypeStruct((S, 3 * D), BF16), jax.ShapeDtypeStruct((S, 3 * D), BF16),
                   jax.ShapeDtypeStruct((S, 256), BF16), jax.ShapeDtypeStruct((S, 512), BF16),
                   jax.ShapeDtypeStruct((S, 256), BF16)),
        compiler_params=_cp("parallel"),
    )(dm, o_a, o_b, o_c, glog, w_br, w_br_t)


def _ffn_act_fwd(up_pre, fcw, *, name, tc, ts=512):
    S, F2 = up_pre.shape
    ts = min(ts, S)
    nj = F2 // (2 * tc)

    def body(cur_ref, prev_ref, w_ref, h_ref):
        i = pl.program_id(1)
        prev = jnp.where(i > 0, prev_ref[...].astype(F32), 0.0)
        x = jnp.concatenate([prev, cur_ref[...].astype(F32)], axis=0)
        up = (w_ref[2:3, :] * x + w_ref[1:2, :] * _shift_dn(x, 1) + w_ref[0:1, :] * _shift_dn(x, 2))[HALO:HALO + ts]
        a, b = up[:, 0:tc], up[:, tc:2 * tc]
        h_ref[...] = (a * _sigmoid(a) * b).astype(BF16)

    hb = ts // HALO
    return pl.pallas_call(
        body, name=name, grid=(nj, S // ts),
        in_specs=[pl.BlockSpec((ts, 2 * tc), lambda j, i: (i, j)),
                  pl.BlockSpec((HALO, 2 * tc), lambda j, i: (jnp.maximum(i * hb - 1, 0), j)),
                  pl.BlockSpec((3, 2 * tc), lambda j, i: (0, j))],
        out_specs=pl.BlockSpec((ts, tc), lambda j, i: (i, j)),
        out_shape=jax.ShapeDtypeStruct((S, F2 // 2), BF16),
        compiler_params=_cp("parallel", "parallel"),
    )(up_pre, up_pre, fcw)


def _ffn_act_bwd(up_pre, dh, fcw, *, name, tc, ts=512, scatter=None):
    S, F2 = up_pre.shape
    ts = min(ts, S)
    nt = S // ts
    nj = F2 // (2 * tc)
    has_rs = scatter is not None

    def body(cur_ref, prev_ref, nxt_ref, dhc_ref, dhn_ref, w_ref, *rest):
        if has_rs:
            q_ref, dpre_ref, dw_ref, recv_ref, send_sems, recv_sems = rest
            rs_refs = (q_ref, recv_ref, send_sems, recv_sems)

            @pl.when((pl.program_id(0) == 0) & (pl.program_id(1) == 0))
            def _():
                _rs_chips_start(*rs_refs)
        else:
            dpre_ref, dw_ref = rest
        i = pl.program_id(1)

        @pl.when(i == 0)
        def _():
            dw_ref[...] = jnp.zeros_like(dw_ref)

        prev = jnp.where(i > 0, prev_ref[...].astype(F32), 0.0)
        x = jnp.concatenate([prev, cur_ref[...].astype(F32), nxt_ref[...].astype(F32)], axis=0)
        dh_n = jnp.where(i < nt - 1, dhn_ref[...].astype(F32), 0.0)
        dh_e = jnp.concatenate([jnp.zeros((HALO, tc), F32), dhc_ref[...].astype(F32), dh_n], axis=0)
        w0, w1, w2 = w_ref[0:1, :], w_ref[1:2, :], w_ref[2:3, :]
        x1, x2 = _shift_dn(x, 1), _shift_dn(x, 2)
        up = w2 * x + w1 * x1 + w0 * x2
        a, b = up[:, 0:tc], up[:, tc:2 * tc]
        sg = _sigmoid(a)
        da = dh_e * b * (sg * (1.0 + a * (1.0 - sg)))
        db = dh_e * (a * sg)
        dup = jnp.concatenate([da, db], axis=1)
        dpre = w2 * dup + w1 * _shift_up(dup, 1) + w0 * _shift_up(dup, 2)
        tile = slice(HALO, HALO + ts)
        dpre_ref[...] = dpre[tile].astype(BF16)
        dup_t = dup[tile]
        dw_ref[0:1, :] += jnp.sum(dup_t * x2[tile], axis=0, keepdims=True)
        dw_ref[1:2, :] += jnp.sum(dup_t * x1[tile], axis=0, keepdims=True)
        dw_ref[2:3, :] += jnp.sum(dup_t * x[tile], axis=0, keepdims=True)

        if has_rs:
            @pl.when((pl.program_id(0) == nj - 1) & (pl.program_id(1) == nt - 1))
            def _():
                _rs_chips_finish(*rs_refs)

    hb = ts // HALO
    nblk = S // HALO
    prev = lambda j, i: (jnp.maximum(i * hb - 1, 0), j)
    nxt = lambda j, i: (jnp.minimum((i + 1) * hb, nblk - 1), j)
    in_specs = [pl.BlockSpec((ts, 2 * tc), lambda j, i: (i, j)), pl.BlockSpec((HALO, 2 * tc), prev),
                pl.BlockSpec((HALO, 2 * tc), nxt),
                pl.BlockSpec((ts, tc), lambda j, i: (i, j)), pl.BlockSpec((HALO, tc), nxt),
                pl.BlockSpec((3, 2 * tc), lambda j, i: (0, j))]
    out_specs = [pl.BlockSpec((ts, 2 * tc), lambda j, i: (i, j)), pl.BlockSpec((3, 2 * tc), lambda j, i: (0, j))]
    out_shape = [jax.ShapeDtypeStruct((S, F2), BF16), jax.ShapeDtypeStruct((3, F2), F32)]
    args = [up_pre, up_pre, up_pre, dh, dh, fcw]
    if has_rs:
        in_specs.append(pl.BlockSpec(memory_space=pl.ANY))
        args.append(scatter)
        out_specs.append(pl.BlockSpec(memory_space=pl.ANY))
        out_shape.append(jax.ShapeDtypeStruct((3,) + scatter.shape[1:], scatter.dtype))
    return pl.pallas_call(
        body, name=name, grid=(nj, nt), in_specs=in_specs, out_specs=out_specs, out_shape=out_shape,
        scratch_shapes=_RS_SEMS if has_rs else [],
        compiler_params=_cp("arbitrary", "arbitrary") if has_rs else _cp("parallel", "arbitrary"),
    )(*args)


FFN_CHUNK = 128
FFN_DOT_CHUNKS = 4


def _conv3(x, w_ref, cols):
    x1, x2 = _shift_dn(x, 1), _shift_dn(x, 2)
    return w_ref[2:3, cols] * x + w_ref[1:2, cols] * x1 + w_ref[0:1, cols] * x2, x1, x2


def _ffn_down_fwd(up_pre, fcw, w_down3, resid, gamma, beta, *, alpha, name, tc, ts=256, gather=None):
    S, F2 = up_pre.shape
    D = resid.shape[1]
    ts = min(ts, S)
    nt = S // ts
    nj = F2 // (2 * tc)
    has_ag = gather is not None

    def body(cur_ref, prev_ref, w_ref, wd_ref, r_ref, g_ref, be_ref, *rest):
        if has_ag:
            src_ref, h_ref, y_ref, xh_ref, rs_ref, dst_ref, acc_ref = rest[:7]
            ag_refs = (src_ref, dst_ref) + tuple(rest[7:10])
        else:
            h_ref, y_ref, xh_ref, rs_ref, acc_ref = rest
        i, j = pl.program_id(0), pl.program_id(1)
        if has_ag:
            @pl.when((i == 0) & (j == 0))
            def _():
                _ag_start(*ag_refs)

        part = None
        for c in range(tc // FFN_CHUNK):
            halves = []
            for half in range(2):
                cols = slice(half * tc + c * FFN_CHUNK, half * tc + (c + 1) * FFN_CHUNK)
                prev = jnp.where(i > 0, prev_ref[:, cols].astype(F32), 0.0)
                x = jnp.concatenate([prev, cur_ref[:, cols].astype(F32)], axis=0)
                halves.append(_conv3(x, w_ref, cols)[0][HALO:HALO + ts])
            a, b = halves
            h_ref[:, c * FFN_CHUNK:(c + 1) * FFN_CHUNK] = (a * _sigmoid(a) * b).astype(BF16)
            if (c + 1) % FFN_DOT_CHUNKS == 0 or c + 1 == tc // FFN_CHUNK:
                k0 = (c // FFN_DOT_CHUNKS) * FFN_DOT_CHUNKS * FFN_CHUNK
                piece = jnp.dot(h_ref[:, k0:(c + 1) * FFN_CHUNK], wd_ref[j, k0:(c + 1) * FFN_CHUNK, :],
                                preferred_element_type=F32)
                part = piece if part is None else part + piece

        @pl.when(j == 0)
        def _():
            acc_ref[...] = part

        @pl.when(j > 0)
        def _():
            acc_ref[...] += part

        @pl.when(j == nj - 1)
        def _():
            z = alpha * r_ref[...] + acc_ref[...]
            mu = jnp.mean(z, axis=-1, keepdims=True)
            zc = z - mu
            var = jnp.mean(zc * zc, axis=-1, keepdims=True)
            rstd = lax.rsqrt(var + LN_EPS)
            xhat = zc * rstd
            y_ref[...] = xhat * g_ref[...] + be_ref[...]
            xh_ref[...] = xhat.astype(BF16)
            rs_ref[...] = rstd

        if has_ag:
            @pl.when((i == nt - 1) & (j == nj - 1))
            def _():
                _ag_finish(*ag_refs)

    hb = ts // HALO
    row = lambda i, j: (i, 0)
    vec = lambda i, j: (0, 0)
    in_specs = [pl.BlockSpec((ts, 2 * tc), lambda i, j: (i, j)),
                pl.BlockSpec((HALO, 2 * tc), lambda i, j: (jnp.maximum(i * hb - 1, 0), j)),
                pl.BlockSpec((3, 2 * tc), lambda i, j: (0, j)),
                pl.BlockSpec((nj, tc, D), lambda i, j: (0, 0, 0)),
                pl.BlockSpec((ts, D), row), pl.BlockSpec((1, D), vec), pl.BlockSpec((1, D), vec)]
    out_specs = [pl.BlockSpec((ts, tc), lambda i, j: (i, j)), pl.BlockSpec((ts, D), row), pl.BlockSpec((ts, D), row),
                 pl.BlockSpec((ts, 1), row)]
    out_shape = [jax.ShapeDtypeStruct((S, F2 // 2), BF16), jax.ShapeDtypeStruct((S, D), F32),
                 jax.ShapeDtypeStruct((S, D), BF16), jax.ShapeDtypeStruct((S, 1), F32)]
    args = [up_pre, up_pre, fcw, w_down3, resid, gamma, beta]
    scratch = [pltpu.VMEM((ts, D), F32)]
    if has_ag:
        in_specs.append(pl.BlockSpec(memory_space=pl.ANY))
        args.append(gather)
        out_specs.append(pl.BlockSpec(memory_space=pl.ANY))
        out_shape.append(jax.ShapeDtypeStruct((N_DEV,) + gather.shape, gather.dtype))
        scratch += _AG_SEMS
    return pl.pallas_call(
        body, name=name, grid=(nt, nj), in_specs=in_specs, out_specs=out_specs, out_shape=out_shape,
        scratch_shapes=scratch, compiler_params=_cp("arbitrary", "arbitrary"),
    )(*args)


def _ffn_up_bwd(up_pre, dh, fcw, w_up_t3, dz, *, alpha, name, tc, ts=256, scatter=None):
    S, F2 = up_pre.shape
    D = dz.shape[1]
    ts = min(ts, S)
    nt = S // ts
    nj = F2 // (2 * tc)
    has_rs = scatter is not None
    tile = slice(HALO, HALO + ts)

    def body(cur_ref, prev_ref, nxt_ref, dhc_ref, dhn_ref, w_ref, wt_ref, dz_ref, *rest):
        if has_rs:
            q_ref, dpre_ref, dx_ref, dw_ref, recv_ref, acc_ref, send_sems, recv_sems = rest
            rs_refs = (q_ref, recv_ref, send_sems, recv_sems)
        else:
            dpre_ref, dx_ref, dw_ref, acc_ref = rest
        i, j = pl.program_id(0), pl.program_id(1)

        @pl.when((i == 0) & (j == 0))
        def _():
            dw_ref[...] = jnp.zeros_like(dw_ref)
            if has_rs:
                _rs_chips_start(*rs_refs)

        part = None
        for c in range(tc // FFN_CHUNK):
            lanes = slice(c * FFN_CHUNK, (c + 1) * FFN_CHUNK)
            dh_n = jnp.where(i < nt - 1, dhn_ref[:, lanes].astype(F32), 0.0)
            dh_e = jnp.concatenate([jnp.zeros((HALO, FFN_CHUNK), F32), dhc_ref[:, lanes].astype(F32), dh_n], axis=0)
            ups, xs, cols_of = [], [], []
            for half in range(2):
                cols = slice(half * tc + c * FFN_CHUNK, half * tc + (c + 1) * FFN_CHUNK)
                prev = jnp.where(i > 0, prev_ref[:, cols].astype(F32), 0.0)
                x = jnp.concatenate([prev, cur_ref[:, cols].astype(F32), nxt_ref[:, cols].astype(F32)], axis=0)
                up, x1, x2 = _conv3(x, w_ref, cols)
                ups.append(up)
                xs.append((x, x1, x2))
                cols_of.append(cols)
            a, b = ups
            sg = _sigmoid(a)
            dups = [dh_e * b * (sg * (1.0 + a * (1.0 - sg))), dh_e * (a * sg)]
            for half in range(2):
                cols, dup, (x, x1, x2) = cols_of[half], dups[half], xs[half]
                dpre = (w_ref[2:3, cols] * dup + w_ref[1:2, cols] * _shift_up(dup, 1)
                        + w_ref[0:1, cols] * _shift_up(dup, 2))
                dpre_ref[:, cols] = dpre[tile].astype(BF16)
                dup_t = dup[tile]
                dw_ref[j, 0:1, cols] += jnp.sum(dup_t * x2[tile], axis=0, keepdims=True)
                dw_ref[j, 1:2, cols] += jnp.sum(dup_t * x1[tile], axis=0, keepdims=True)
                dw_ref[j, 2:3, cols] += jnp.sum(dup_t * x[tile], axis=0, keepdims=True)
            if (c + 1) % FFN_DOT_CHUNKS == 0 or c + 1 == tc // FFN_CHUNK:
                k0 = (c // FFN_DOT_CHUNKS) * FFN_DOT_CHUNKS * FFN_CHUNK
                for half in range(2):
                    ks = slice(half * tc + k0, half * tc + (c + 1) * FFN_CHUNK)
                    piece = jnp.dot(dpre_ref[:, ks], wt_ref[j, ks, :], preferred_element_type=F32)
                    part = piece if part is None else part + piece

        @pl.when(j == 0)
        def _():
            acc_ref[...] = part

        @pl.when(j > 0)
        def _():
            acc_ref[...] += part

        @pl.when(j == nj - 1)
        def _():
            dx_ref[...] = acc_ref[...] + alpha * dz_ref[...]

        if has_rs:
            @pl.when((i == nt - 1) & (j == nj - 1))
            def _():
                _rs_chips_finish(*rs_refs)

    hb = ts // HALO
    nblk = S // HALO
    prev = lambda i, j: (jnp.maximum(i * hb - 1, 0), j)
    nxt = lambda i, j: (jnp.minimum((i + 1) * hb, nblk - 1), j)
    row = lambda i, j: (i, 0)
    in_specs = [pl.BlockSpec((ts, 2 * tc), lambda i, j: (i, j)), pl.BlockSpec((HALO, 2 * tc), prev),
                pl.BlockSpec((HALO, 2 * tc), nxt),
                pl.BlockSpec((ts, tc), lambda i, j: (i, j)), pl.BlockSpec((HALO, tc), nxt),
                pl.BlockSpec((3, 2 * tc), lambda i, j: (0, j)),
                pl.BlockSpec((nj, 2 * tc, D), lambda i, j: (0, 0, 0)),
                pl.BlockSpec((ts, D), row)]
    out_specs = [pl.BlockSpec((ts, 2 * tc), lambda i, j: (i, j)), pl.BlockSpec((ts, D), row),
                 pl.BlockSpec((nj, 3, 2 * tc), lambda i, j: (0, 0, 0))]
    out_shape = [jax.ShapeDtypeStruct((S, F2), BF16), jax.ShapeDtypeStruct((S, D), F32),
                 jax.ShapeDtypeStruct((nj, 3, 2 * tc), F32)]
    args = [up_pre, up_pre, up_pre, dh, dh, fcw, w_up_t3, dz]
    scratch = [pltpu.VMEM((ts, D), F32)]
    if has_rs:
        in_specs.append(pl.BlockSpec(memory_space=pl.ANY))
        args.append(scatter)
        out_specs.append(pl.BlockSpec(memory_space=pl.ANY))
        out_shape.append(jax.ShapeDtypeStruct((3,) + scatter.shape[1:], scatter.dtype))
        scratch += _RS_SEMS
    return pl.pallas_call(
        body, name=name, grid=(nt, nj), in_specs=in_specs, out_specs=out_specs, out_shape=out_shape,
        scratch_shapes=scratch, compiler_params=_cp("arbitrary", "arbitrary"),
    )(*args)


def _ln_bwd(dy, xhat, rstd, gamma, *, name, ts=512):
    S, D = dy.shape
    ts = min(ts, S)

    def body(dy_ref, xh_ref, rs_ref, g_ref, dz_ref, dg_ref, db_ref):
        @pl.when(pl.program_id(0) == 0)
        def _():
            dg_ref[...] = jnp.zeros_like(dg_ref)
            db_ref[...] = jnp.zeros_like(db_ref)

        dyv = dy_ref[...]
        xh = xh_ref[...].astype(F32)
        dyg = dyv * g_ref[...]
        c1 = jnp.mean(dyg, axis=-1, keepdims=True)
        c2 = jnp.mean(dyg * xh, axis=-1, keepdims=True)
        dz_ref[...] = rs_ref[...] * (dyg - c1 - xh * c2)
        dg_ref[...] += jnp.sum(dyv * xh, axis=0, keepdims=True)
        db_ref[...] += jnp.sum(dyv, axis=0, keepdims=True)

    cur = lambda i: (i, 0)
    const = lambda i: (0, 0)
    return pl.pallas_call(
        body, name=name, grid=(S // ts,),
        in_specs=[pl.BlockSpec((ts, D), cur), pl.BlockSpec((ts, D), cur), pl.BlockSpec((ts, 1), cur),
                  pl.BlockSpec((1, D), const)],
        out_specs=[pl.BlockSpec((ts, D), cur), pl.BlockSpec((1, D), const), pl.BlockSpec((1, D), const)],
        out_shape=(jax.ShapeDtypeStruct((S, D), F32), jax.ShapeDtypeStruct((1, D), F32),
                   jax.ShapeDtypeStruct((1, D), F32)),
        compiler_params=_cp("arbitrary"),
    )(dy, xhat, rstd, gamma)


def _loss_head(y, tgt, *, name, ts=512):
    S, D = y.shape
    ts = min(ts, S)

    def body(y_ref, t_ref, dy_ref, sq_ref):
        @pl.when(pl.program_id(0) == 0)
        def _():
            sq_ref[...] = jnp.zeros_like(sq_ref)

        e = y_ref[...] - t_ref[...]
        dy_ref[...] = e * (1.0 / D)
        sq_ref[...] += jnp.sum(e * e, axis=0, keepdims=True)

    cur = lambda i: (i, 0)
    return pl.pallas_call(
        body, name=name, grid=(S // ts,),
        in_specs=[pl.BlockSpec((ts, D), cur), pl.BlockSpec((ts, D), cur)],
        out_specs=[pl.BlockSpec((ts, D), cur), pl.BlockSpec((1, D), lambda i: (0, 0))],
        out_shape=(jax.ShapeDtypeStruct((S, D), F32), jax.ShapeDtypeStruct((1, D), F32)),
        compiler_params=_cp("arbitrary"),
    )(y, tgt)


def _adamw(w, g, m, v, *, name, tr=512):
    R, C = w.shape
    tr = _div_tile(R, tr)
    c1 = 1.0 - ADAM_B1 ** ADAM_STEP
    c2 = 1.0 - ADAM_B2 ** ADAM_STEP

    def body(w_ref, g_ref, m_ref, v_ref, d_ref, mo_ref, vo_ref):
        gv = g_ref[...]
        m2 = ADAM_B1 * m_ref[...] + (1.0 - ADAM_B1) * gv
        v2 = ADAM_B2 * v_ref[...] + (1.0 - ADAM_B2) * (gv * gv)
        m_hat = m2 / c1
        v_hat = v2 / c2
        d_ref[...] = -ADAM_LR * (m_hat / (jnp.sqrt(v_hat) + ADAM_EPS) + ADAM_WD * w_ref[...])
        mo_ref[...] = m2
        vo_ref[...] = v2

    spec = pl.BlockSpec((tr, C), lambda i: (i, 0))
    return pl.pallas_call(
        body, name=name, grid=(R // tr,),
        in_specs=[spec] * 4, out_specs=[spec] * 3,
        out_shape=(jax.ShapeDtypeStruct((R, C), F32),) * 3,
        compiler_params=_cp("parallel"),
    )(w, g, m, v)


def _interleave_cols(w, nj):
    lead, f2 = w.shape[:-1], w.shape[-1]
    tc = f2 // (2 * nj)
    w = w.reshape(lead + (2, nj, tc))
    return jnp.swapaxes(w, -3, -2).reshape(lead + (f2,))


def _deinterleave_cols(w, nj):
    lead, f2 = w.shape[:-1], w.shape[-1]
    tc = f2 // (2 * nj)
    w = w.reshape(lead + (nj, 2, tc))
    return jnp.swapaxes(w, -3, -2).reshape(lead + (f2,))


def _block_diag(w_pool):
    return jnp.concatenate([jnp.pad(w_pool[g], ((0, 0), (64 * g, 192 - 64 * g))) for g in range(4)], axis=0)


def _pad_rows(v, rows):
    return jnp.pad(v, (0, rows * LANES - v.shape[0])).reshape(rows, LANES)


def kernel(x, positions, w_in, w_pool, pool_scale, attn_sinks, conv_w, w_branch_a, w_branch_b, w_branch_c, w_o, ln1_g, ln1_b, w_up, ffn_conv_w, w_down, ln2_g, ln2_b, loss_target, m_w_in, m_w_pool, m_pool_scale, m_attn_sinks, m_conv_w, m_w_branch_a, m_w_branch_b, m_w_branch_c, m_w_o, m_ln1_g, m_ln1_b, m_w_up, m_ffn_conv_w, m_w_down, m_ln2_g, m_ln2_b, v_w_in, v_w_pool, v_pool_scale, v_attn_sinks, v_conv_w, v_w_branch_a, v_w_branch_b, v_w_branch_c, v_w_o, v_ln1_g, v_ln1_b, v_w_up, v_ffn_conv_w, v_w_down, v_ln2_g, v_ln2_b):
    L, D, in_shard = w_in.shape
    S = x.shape[1]
    IN = in_shard * N_DEV
    F2 = w_up.shape[2] * N_DEV
    F = F2 // 2
    assert D == 1024 and IN == 1792 + 3 * D and x.shape[0] == 1 and S % 512 == 0
    alpha = (2 * L) ** 0.25
    NJ = 2
    TC = F // NJ
    xs = x.reshape(S, D)
    tgt = loss_target.reshape(S, D)

    big = [w_in, w_branch_a, w_branch_b, w_branch_c, w_o, w_up, w_down]
    PART_A, PART_B = (0, 1, 2, 3, 4), (5, 6)
    rows_l = [a.size // L // LANES for a in big]
    offs_l = [sum(rows_l[:k]) for k in range(len(big) + 1)]

    def pack_part(l, part):
        return jnp.concatenate([big[k][l].reshape(-1, LANES).astype(BF16) for k in part], axis=0)

    n_cw, n_fw = conv_w.size, ffn_conv_w.size
    small_rows = -(-(n_cw + n_fw) // LANES)
    small = _pad_rows(jnp.concatenate([conv_w.reshape(-1), ffn_conv_w.reshape(-1)]), small_rows)
    gsmall = _all_gather(small, "ag_conv_weights").reshape(N_DEV, -1)
    conv_full = gsmall[:, :n_cw].reshape(N_DEV, L, 3, -1).transpose(1, 2, 0, 3).reshape(L, 3, 256)
    fcw_full = gsmall[:, n_cw:n_cw + n_fw].reshape(N_DEV, L, 3, -1).transpose(1, 2, 0, 3).reshape(L, 3, F2)
    fcw_full = _interleave_cols(fcw_full, NJ)

    def shard_of(g, part, k, shape):
        o = offs_l[k] - offs_l[part[0]]
        return g[:, o:o + rows_l[k], :].reshape((N_DEV,) + shape)

    def unpack_a(g):
        win = shard_of(g, PART_A, 0, (D, in_shard)).transpose(1, 0, 2).reshape(D, IN)
        wg = win[:, 1792:]
        wp = jnp.concatenate([win[:, 0:256], win[:, 1024:1792]], axis=1)
        wq = win[:, 256:1024]
        wa = shard_of(g, PART_A, 1, (256, D // N_DEV)).transpose(1, 0, 2).reshape(256, D)
        wb = shard_of(g, PART_A, 2, (512, D // N_DEV)).transpose(1, 0, 2).reshape(512, D)
        wc = shard_of(g, PART_A, 3, (256, D // N_DEV)).transpose(1, 0, 2).reshape(256, D)
        wbr = jnp.concatenate([wa, wb, wc], axis=0)
        wo = shard_of(g, PART_A, 4, (D // N_DEV, D)).reshape(D, D)
        return dict(wg=wg, wp=wp, wq=wq, wg_t=wg.T, wp_t=wp.T, wq_t=wq.T, wbr=wbr, wbr_t=wbr.T, wo=wo, wo_t=wo.T)

    def unpack_b(g):
        nh = N_DEV // (2 * NJ)
        wup = shard_of(g, PART_B, 5, (D, F2 // N_DEV)).reshape(2, NJ, nh, D, F2 // N_DEV)
        wup = wup.transpose(3, 1, 0, 2, 4).reshape(D, F2)
        wdn = shard_of(g, PART_B, 6, (F // N_DEV, D)).reshape(F, D)
        return dict(wup=wup, wup_t=wup.T, wdn=wdn, wdn_t=wdn.T)

    def local_weights(l):
        wbd = _block_diag(w_pool[l]).astype(BF16)
        return dict(wbd=wbd, wbd_t=wbd.T, scale=pool_scale[l].reshape(1, 256), conv=conv_full[l],
                    fcw=fcw_full[l], sinks=jnp.broadcast_to(attn_sinks[l].reshape(8, 1), (8, 128)),
                    g1=ln1_g[l].reshape(1, D), b1=ln1_b[l].reshape(1, D),
                    g2=ln2_g[l].reshape(1, D), b2=ln2_b[l].reshape(1, D))

    inv_freq = ROPE_THETA ** (-jnp.arange(0, ROT_DIM, 2, dtype=F32) / ROT_DIM)
    head_lane = jnp.concatenate([inv_freq, inv_freq, jnp.zeros((HEAD_DIM - ROT_DIM,), F32)])
    head_sign = jnp.concatenate([-jnp.ones((8,), F32), jnp.ones((8,), F32), jnp.zeros((HEAD_DIM - ROT_DIM,), F32)])
    inv_lane = jnp.tile(head_lane, 2).reshape(1, 128)
    sign_lane = jnp.tile(head_sign, 2).reshape(1, 128)
    cos_t, sin_t = _rope_tables(positions.reshape(S, 1), inv_lane, sign_lane, "rope_tables")

    saved, W = [], []
    h_in = xs
    gathered_a = _all_gather(pack_part(0, PART_A), "ag_weights_first")
    for l in range(L):
        w = {**unpack_a(gathered_a), **local_weights(l)}
        if l == 0:
            pg, gathered_b = _mm(h_in, w["wg"], out_dtype=BF16, name="proj_gate", gather=pack_part(l, PART_B))
        else:
            pg = _mm(h_in, w["wg"], out_dtype=BF16, name="proj_gate")
        w.update(unpack_b(gathered_b))
        W.append(w)
        pp = _mm(h_in, w["wp"], out_dtype=BF16, name="proj_poolconv")
        pq = _mm(h_in, w["wq"], out_dtype=BF16, name="proj_qkv")
        o_a, o_c = _poolconv_fwd(pp, w["wbd"], w["scale"], w["conv"], name="poolconv_fwd")
        o_b, lse = _attn_fwd(pq, cos_t, sin_t, w["sinks"], name="attn_fwd")
        merged = _merge_fwd(o_a, o_b, o_c, pg, w["wbr"], name="merge_fwd")
        x1, xh1, rs1 = _mm_ln(merged, w["wo"], h_in, w["g1"], w["b1"], alpha=alpha, name="wo_ln1")
        if l + 1 < L:
            up_pre, gathered_a = _mm(x1, w["wup"], out_dtype=BF16, name="ffn_up", tn=2 * TC,
                                     gather=pack_part(l + 1, PART_A))
        else:
            up_pre = _mm(x1, w["wup"], out_dtype=BF16, name="ffn_up", tn=2 * TC)
        down = dict(alpha=alpha, name="ffn_down", tc=TC)
        wdn3 = w["wdn"].reshape(NJ, TC, D)
        if l + 1 < L:
            hact, x2, xh2, rs2, gathered_b = _ffn_down_fwd(up_pre, w["fcw"], wdn3, x1, w["g2"], w["b2"],
                                                           gather=pack_part(l + 1, PART_B), **down)
        else:
            hact, x2, xh2, rs2 = _ffn_down_fwd(up_pre, w["fcw"], wdn3, x1, w["g2"], w["b2"], **down)
        saved.append(dict(x0=h_in, pg=pg, pp=pp, pq=pq, o_a=o_a, o_b=o_b, o_c=o_c, lse=lse, merged=merged,
                          x1=x1, xh1=xh1, rs1=rs1, up_pre=up_pre, hact=hact, xh2=xh2, rs2=rs2))
        h_in = x2

    dy, sq_lanes = _loss_head(h_in, tgt, name="loss_head")

    def pack_grads(g):
        col = lambda a, n: a.reshape(a.shape[0], N_DEV, n).transpose(1, 0, 2)
        row = lambda a, n: a.reshape(N_DEV, n, a.shape[1])
        nh = N_DEV // (2 * NJ)
        up = g["w_up"].reshape(D, NJ, 2, nh, F2 // N_DEV).transpose(2, 1, 3, 0, 4)
        parts = [col(g["w_in"], in_shard), col(g["a"], D // N_DEV), col(g["b"], D // N_DEV), col(g["c"], D // N_DEV),
                 row(g["w_o"], D // N_DEV), up, row(g["w_down"], F // N_DEV)]
        return jnp.concatenate([p.reshape(N_DEV, -1, LANES).astype(BF16) for p in parts], axis=1)

    my_c = lax.axis_index("c").astype(jnp.int32).reshape(1)
    my_chip = (2 * lax.axis_index("x") + lax.axis_index("y")).astype(jnp.int32).reshape(1)
    gw = [None] * L
    pair_sum = [None] * L
    from_chips = [None] * L
    for l in reversed(range(L)):
        w, sv = W[l], saved[l]
        dz2, dg2, db2 = _ln_bwd(dy, sv["xh2"], sv["rs2"], w["g2"], name="ln2_bwd")
        dh = _mm(dz2, w["wdn_t"], out_dtype=BF16, name="down_bwd_x")
        dw_dn = _mm_tn(sv["hact"], dz2, name="down_bwd_w", tka=TC, na=NJ, tn=D, ts=1024)
        up_bwd = dict(alpha=alpha, name="ffn_up_bwd", tc=TC)
        wup_t3 = w["wup_t"].reshape(NJ, 2 * TC, D)
        if l + 1 < L:
            dpre, dx1, dfcw, from_chips[l + 1] = _ffn_up_bwd(sv["up_pre"], dh, w["fcw"], wup_t3, dz2,
                                                             scatter=pair_sum[l + 1], **up_bwd)
        else:
            dpre, dx1, dfcw = _ffn_up_bwd(sv["up_pre"], dh, w["fcw"], wup_t3, dz2, **up_bwd)
        dfcw = dfcw.transpose(1, 0, 2).reshape(3, F2)
        dw_up = _mm_tn(sv["x1"], dpre, name="up_bwd_w", tka=D, tn=TC, nb=2 * NJ, ts=1024)
        dz1, dg1, db1 = _ln_bwd(dx1, sv["xh1"], sv["rs1"], w["g1"], name="ln1_bwd")
        dmerged = _mm(dz1, w["wo_t"], out_dtype=BF16, name="wo_bwd_x")
        dw_o = _mm_tn(sv["merged"], dz1, name="wo_bwd_w", tka=D, tn=D // 2, nb=2)
        dpg, dprod, do_a, do_b, do_c = _merge_bwd(dmerged, sv["o_a"], sv["o_b"], sv["o_c"], sv["pg"],
                                                  w["wbr"], w["wbr_t"], name="merge_bwd")
        dw_a = _mm_tn(sv["o_a"], dprod, name="branch_a_bwd_w", tka=256, tn=D, b_off=0)
        dw_b = _mm_tn(sv["o_b"], dprod, name="branch_b_bwd_w", tka=512, tn=D, b_off=1)
        dw_c = _mm_tn(sv["o_c"], dprod, name="branch_c_bwd_w", tka=256, tn=D, b_off=2)
        dpq, dsink = _attn_bwd(sv["pq"], cos_t, sin_t, w["sinks"], do_b, sv["o_b"], sv["lse"], name="attn_bwd")
        dpp, pooled, dmixed, dscale, dconv = _poolconv_bwd(sv["pp"], do_a, do_c, w["wbd"], w["wbd_t"], w["scale"],
                                                           w["conv"], name="poolconv_bwd")
        dwbd = _mm_tn(pooled, dmixed, name="pool_bwd_w", tka=256, tn=256)
        dx = _mm(dpg, w["wg_t"], out_dtype=F32, name="proj_gate_bwd_x", add=dz1, add_scale=alpha)
        dx = _mm(dpp, w["wp_t"], out_dtype=F32, name="proj_poolconv_bwd_x", add=dx)
        dx = _mm(dpq, w["wq_t"], out_dtype=F32, name="proj_qkv_bwd_x", add=dx)
        dw_g = _mm_tn(sv["x0"], dpg, name="proj_gate_bwd_w", tka=D, tn=512, nb=6)
        dw_p = _mm_tn(sv["x0"], dpp, name="proj_poolconv_bwd_w", tka=D, tn=512, nb=2)
        dw_q = _mm_tn(sv["x0"], dpq, name="proj_qkv_bwd_w", tka=D, tn=384, nb=2)
        dw_in = jnp.concatenate([dw_p[:, 0:256], dw_q, dw_p[:, 256:1024], dw_g], axis=1)
        dw_pool = jnp.stack([dwbd[64 * g:64 * (g + 1), 64 * g:64 * (g + 1)] for g in range(4)])
        gw[l] = dict(w_in=dw_in, a=dw_a, b=dw_b, c=dw_c, w_o=dw_o, w_up=dw_up, w_down=dw_dn,
                     w_pool=dw_pool, scale=dscale, sinks=dsink[:, 0], conv=dconv, fcw=_deinterleave_cols(dfcw, NJ),
                     g1=dg1, b1=db1, g2=dg2, b2=db2)
        p_l = pack_grads(gw[l])
        pair_sum[l] = _sum_sibling(p_l, _rs_sibling(p_l, "rs_sibling"), my_c, "rs_sum_sibling")
        dy = dx
    grad_x = dy.reshape(1, S, D)
    from_chips[0] = _rs_chips(pair_sum[0], "rs_chips_last")
    g_layers = [_sum_chips(pair_sum[l], from_chips[l], my_chip, "rs_sum_chips") for l in range(L)]

    def stack(k):
        return jnp.stack([gw[l][k] for l in range(L)])

    rep_vec = jnp.concatenate([
        stack("w_pool").reshape(-1), stack("scale").reshape(-1), stack("g1").reshape(-1), stack("b1").reshape(-1),
        stack("g2").reshape(-1), stack("b2").reshape(-1)])
    n_rep_full = -(-rep_vec.shape[0] // LANES)
    sinks_row = jnp.pad(stack("sinks").reshape(-1), (0, LANES - 8 * L))
    rep_vec = jnp.concatenate([_pad_rows(rep_vec, n_rep_full).reshape(-1), sinks_row, sq_lanes.reshape(-1)])
    loss_row = n_rep_full + 1
    n_rep = -(-(loss_row + 1) // 8) * 8
    rep_rows = _pad_rows(rep_vec, n_rep)
    dconv_by_dev = stack("conv").reshape(L, 3, N_DEV, -1).transpose(2, 0, 1, 3).reshape(N_DEV, -1)
    dfcw_by_dev = stack("fcw").reshape(L, 3, N_DEV, -1).transpose(2, 0, 1, 3).reshape(N_DEV, -1)
    n_mine = -(-(small_rows) // 8) * 8
    by_dev = jnp.concatenate([dconv_by_dev, dfcw_by_dev], axis=1)
    by_dev = jnp.pad(by_dev, ((0, 0), (0, n_mine * LANES - by_dev.shape[1]))).reshape(N_DEV * n_mine, LANES)
    small_g = _all_gather(jnp.concatenate([rep_rows, by_dev], axis=0), "ag_small_grads")
    rep_sum, mine_sum, loss11 = _small_reduce(small_g, n_rep, n_mine, 1.0 / D, loss_row, "small_reduce")
    loss = loss11[0, 0]

    names_big = ["w_in", "w_branch_a", "w_branch_b", "w_branch_c", "w_o", "w_up", "w_down"]
    ms_big = [m_w_in, m_w_branch_a, m_w_branch_b, m_w_branch_c, m_w_o, m_w_up, m_w_down]
    vs_big = [v_w_in, v_w_branch_a, v_w_branch_b, v_w_branch_c, v_w_o, v_w_up, v_w_down]
    out = {}
    for k, name in enumerate(names_big):
        wk = big[k]
        c2 = wk.shape[-1]
        as2d = lambda a: a.reshape(-1, c2)
        g_nat = jnp.concatenate([g[offs_l[k]:offs_l[k + 1]] for g in g_layers], axis=0).reshape(wk.shape)
        d, mo, vo = _adamw(as2d(wk), as2d(g_nat), as2d(ms_big[k]), as2d(vs_big[k]), name="adamw_" + name)
        out[name] = (g_nat, d.reshape(wk.shape), mo.reshape(wk.shape), vo.reshape(wk.shape))

    def rep_pack(wp_, sc_, g1_, b1_, g2_, b2_, sk_):
        v = jnp.concatenate([wp_.reshape(-1), sc_.reshape(-1), g1_.reshape(-1), b1_.reshape(-1), g2_.reshape(-1),
                             b2_.reshape(-1)])
        return _pad_rows(jnp.concatenate([_pad_rows(v, n_rep_full).reshape(-1), sk_.reshape(-1)]), n_rep)

    def mine_pack(cw_, fw_):
        return _pad_rows(jnp.concatenate([cw_.reshape(-1), fw_.reshape(-1)]), n_mine)

    w_rep = rep_pack(w_pool, pool_scale, ln1_g, ln1_b, ln2_g, ln2_b, attn_sinks)
    m_rep = rep_pack(m_w_pool, m_pool_scale, m_ln1_g, m_ln1_b, m_ln2_g, m_ln2_b, m_attn_sinks)
    v_rep = rep_pack(v_w_pool, v_pool_scale, v_ln1_g, v_ln1_b, v_ln2_g, v_ln2_b, v_attn_sinks)
    g_rep = jnp.concatenate([rep_sum[:loss_row], jnp.zeros((n_rep - loss_row, LANES), F32)], axis=0)
    rep_res = (g_rep,) + tuple(_adamw(w_rep, g_rep, m_rep, v_rep, name="adamw_replicated"))
    w_mine = mine_pack(conv_w, ffn_conv_w)
    mine_res = (mine_sum,) + tuple(_adamw(w_mine, mine_sum, mine_pack(m_conv_w, m_ffn_conv_w),
                                          mine_pack(v_conv_w, v_ffn_conv_w), name="adamw_conv"))

    def rep_unpack(buf):
        flat = buf.reshape(-1)
        res, o = {}, 0
        for nm, ref in (("w_pool", w_pool), ("pool_scale", pool_scale), ("ln1_g", ln1_g), ("ln1_b", ln1_b),
                        ("ln2_g", ln2_g), ("ln2_b", ln2_b)):
            res[nm] = flat[o:o + ref.size].reshape(ref.shape)
            o += ref.size
        o = n_rep_full * LANES
        res["attn_sinks"] = flat[o:o + attn_sinks.size].reshape(attn_sinks.shape)
        return res

    def mine_unpack(buf):
        flat = buf.reshape(-1)
        return {"conv_w": flat[:n_cw].reshape(conv_w.shape),
                "ffn_conv_w": flat[n_cw:n_cw + n_fw].reshape(ffn_conv_w.shape)}

    order = ["w_in", "w_pool", "pool_scale", "attn_sinks", "conv_w", "w_branch_a", "w_branch_b", "w_branch_c", "w_o",
             "ln1_g", "ln1_b", "w_up", "ffn_conv_w", "w_down", "ln2_g", "ln2_b"]
    results = [loss, grad_x]
    for kind in range(4):
        rep_k, mine_k = rep_unpack(rep_res[kind]), mine_unpack(mine_res[kind])
        for nm in order:
            if nm in out:
                results.append(out[nm][kind])
            elif nm in rep_k:
                results.append(rep_k[nm])
            else:
                results.append(mine_k[nm])
    return tuple(results)
```

```python
import functools

import jax
import jax.numpy as jnp
from jax import lax
from jax.experimental import pallas as pl
from jax.experimental.pallas import tpu as pltpu

F32 = jnp.float32
BF16 = jnp.bfloat16

HEAD_DIM = 64
N_Q_HEADS = 8
GROUP = 4
WINDOW = 128
ROT_DIM = 16
ROPE_THETA = 500000.0
POOL_WINDOWS = (2, 4, 8, 16)
LN_EPS = 1e-5
MASK_VALUE = -1e30
ADAM_LR, ADAM_B1, ADAM_B2, ADAM_EPS, ADAM_WD, ADAM_STEP = 0.001, 0.9, 0.999, 1e-08, 0.01, 10

N_DEV = 8
LANES = 1024
HALO = 16
MESH = pl.DeviceIdType.MESH
VMEM_LIMIT = 56 * 1024 * 1024


def _div_tile(n, want, mult=8):
    for t in range(min(want, n) // mult * mult, 0, -mult):
        if n % t == 0:
            return t
    return n


def _cp(*sem):
    return pltpu.CompilerParams(dimension_semantics=sem, vmem_limit_bytes=VMEM_LIMIT)


def _coords():
    return lax.axis_index("x"), lax.axis_index("y"), lax.axis_index("c")


def _all_gather(xs, name):
    xs = list(xs) if isinstance(xs, (list, tuple)) else [xs]
    n = len(xs)

    def body(*refs):
        ag_refs = (refs[:n], refs[n:2 * n]) + tuple(refs[2 * n:])
        _ag_start(*ag_refs)
        _ag_finish(*ag_refs)

    res = pl.pallas_call(
        body, name=name,
        out_shape=[jax.ShapeDtypeStruct((N_DEV,) + a.shape, a.dtype) for a in xs],
        in_specs=[pl.BlockSpec(memory_space=pl.ANY)] * n,
        out_specs=[pl.BlockSpec(memory_space=pl.ANY)] * n,
        scratch_shapes=_ag_sems(n),
    )(*xs)
    return res if n > 1 else res[0]


def _ag_sems(n):
    return [pltpu.SemaphoreType.DMA((7 * n,)), pltpu.SemaphoreType.DMA((7 * n,)), pltpu.SemaphoreType.DMA((n,))]


def _ag_copies(x_refs, out_refs, send_sems, recv_sems, local_sems):
    x, y, c = _coords()
    me, sibling = (x, y, c), (x, y, 1 - c)
    chips = [(1 - x, y), (x, 1 - y), (1 - x, 1 - y)]
    per_array = []
    for a, (x_ref, out_ref) in enumerate(zip(x_refs, out_refs)):
        def slot(px, py, pc, out_ref=out_ref):
            return out_ref.at[4 * px + 2 * py + pc]

        def copy(k, block, to, src=None, a=a, slot=slot):
            return pltpu.make_async_remote_copy(
                src_ref=slot(*block) if src is None else src, dst_ref=slot(*block),
                send_sem=send_sems.at[7 * a + k], recv_sem=recv_sems.at[7 * a + k],
                device_id=to, device_id_type=MESH)

        mine = pltpu.make_async_copy(x_ref, slot(*me), local_sems.at[a])
        first = [copy(0, me, sibling, src=x_ref)]
        first += [copy(1 + j, me, (*chip, c), src=x_ref) for j, chip in enumerate(chips)]
        passed = [copy(4 + j, (*chip, c), sibling) for j, chip in enumerate(chips)]
        from_chips = [copy(1 + j, (*chip, c), me) for j, chip in enumerate(chips)]
        from_sibling = [copy(0, sibling, me)] + [copy(4 + j, (*chip, 1 - c), me) for j, chip in enumerate(chips)]
        per_array.append((mine, first, passed, from_chips, from_sibling))
    return per_array


def _ag_start(*refs):
    for mine, first, _, _, _ in _ag_copies(*refs):
        mine.start()
        for cp in first:
            cp.start()


def _ag_finish(*refs):
    per_array = _ag_copies(*refs)
    for j in range(3):
        for _, _, passed, from_chips, _ in per_array:
            from_chips[j].wait_recv()
            passed[j].start()
    for mine, first, passed, _, from_sibling in per_array:
        for cp in from_sibling:
            cp.wait_recv()
        for cp in first + passed:
            cp.wait_send()
        mine.wait()


def _rs_sibling(p, name):
    _, R, C = p.shape

    def body(p_ref, out_ref, send_sems, recv_sems):
        x, y, c = _coords()
        copies = []
        for j in range(4):
            cx, cy = j // 2, j % 2
            copies.append(pltpu.make_async_remote_copy(
                src_ref=p_ref.at[4 * cx + 2 * cy + (1 - c)], dst_ref=out_ref.at[j],
                send_sem=send_sems.at[j], recv_sem=recv_sems.at[j], device_id=(x, y, 1 - c), device_id_type=MESH))
        for cp in copies:
            cp.start()
        for cp in copies:
            cp.wait_recv()
        for cp in copies:
            cp.wait_send()

    return pl.pallas_call(
        body, name=name,
        out_shape=jax.ShapeDtypeStruct((4, R, C), p.dtype),
        in_specs=[pl.BlockSpec(memory_space=pl.ANY)],
        out_specs=pl.BlockSpec(memory_space=pl.ANY),
        scratch_shapes=[pltpu.SemaphoreType.DMA((4,)), pltpu.SemaphoreType.DMA((4,))],
    )(p)


def _rs_chips(q, name):
    _, R, C = q.shape

    def body(q_ref, out_ref, send_sems, recv_sems):
        _rs_chips_start(q_ref, out_ref, send_sems, recv_sems)
        _rs_chips_finish(q_ref, out_ref, send_sems, recv_sems)

    return pl.pallas_call(
        body, name=name,
        out_shape=jax.ShapeDtypeStruct((3, R, C), q.dtype),
        in_specs=[pl.BlockSpec(memory_space=pl.ANY)],
        out_specs=pl.BlockSpec(memory_space=pl.ANY),
        scratch_shapes=_RS_SEMS,
    )(q)


_RS_SEMS = [pltpu.SemaphoreType.DMA((3,)), pltpu.SemaphoreType.DMA((3,))]


def _rs_chips_copies(q_ref, out_ref, send_sems, recv_sems):
    x, y, c = _coords()
    chips = [(1 - x, y), (x, 1 - y), (1 - x, 1 - y)]
    return [pltpu.make_async_remote_copy(
        src_ref=q_ref.at[2 * cx + cy], dst_ref=out_ref.at[k],
        send_sem=send_sems.at[k], recv_sem=recv_sems.at[k], device_id=(cx, cy, c), device_id_type=MESH)
        for k, (cx, cy) in enumerate(chips)]


def _rs_chips_start(*refs):
    for cp in _rs_chips_copies(*refs):
        cp.start()


def _rs_chips_finish(*refs):
    copies = _rs_chips_copies(*refs)
    for cp in copies:
        cp.wait_recv()
    for cp in copies:
        cp.wait_send()


def _sum_sibling(p, recv, my_c, name, tr=512):
    _, R, C = p.shape
    tr = _div_tile(R, tr, 16)

    def body(c_ref, p_ref, r_ref, o_ref):
        o_ref[...] = (p_ref[...].astype(F32) + r_ref[...].astype(F32)).astype(o_ref.dtype)

    grid_spec = pltpu.PrefetchScalarGridSpec(
        num_scalar_prefetch=1, grid=(4, R // tr),
        in_specs=[pl.BlockSpec((1, tr, C), lambda j, r, c_ref: (4 * (j // 2) + 2 * (j % 2) + c_ref[0], r, 0)),
                  pl.BlockSpec((1, tr, C), lambda j, r, c_ref: (j, r, 0))],
        out_specs=pl.BlockSpec((1, tr, C), lambda j, r, c_ref: (j, r, 0)))
    return pl.pallas_call(body, name=name, grid_spec=grid_spec,
                          out_shape=jax.ShapeDtypeStruct((4, R, C), p.dtype),
                          compiler_params=_cp("parallel", "parallel"))(my_c, p, recv)


def _sum_chips(q, recv, my_chip, name, tr=512):
    _, R, C = q.shape
    tr = _div_tile(R, tr, 16)

    def body(i_ref, q_ref, r_ref, o_ref):
        acc = q_ref[0].astype(F32)
        for k in range(3):
            acc = acc + r_ref[k].astype(F32)
        o_ref[...] = acc

    grid_spec = pltpu.PrefetchScalarGridSpec(
        num_scalar_prefetch=1, grid=(R // tr,),
        in_specs=[pl.BlockSpec((1, tr, C), lambda r, i_ref: (i_ref[0], r, 0)),
                  pl.BlockSpec((3, tr, C), lambda r, i_ref: (0, r, 0))],
        out_specs=pl.BlockSpec((tr, C), lambda r, i_ref: (r, 0)))
    return pl.pallas_call(body, name=name, grid_spec=grid_spec,
                          out_shape=jax.ShapeDtypeStruct((R, C), F32),
                          compiler_params=_cp("parallel"))(my_chip, q, recv)


def _small_reduce(g, n_rep, n_mine, inv_d, loss_row, name):
    _, R, C = g.shape

    def body(g_ref, rep_ref, mine_ref, loss_ref):
        x, y, c = _coords()
        start = pl.multiple_of(n_rep + (4 * x + 2 * y + c) * n_mine, 8)
        rep = g_ref[0, 0:n_rep, :]
        mine = g_ref[0, pl.ds(start, n_mine), :]
        sq = g_ref[0, loss_row:loss_row + 1, :]
        for d in range(1, N_DEV):
            rep = rep + g_ref[d, 0:n_rep, :]
            mine = mine + g_ref[d, pl.ds(start, n_mine), :]
            sq = sq + g_ref[d, loss_row:loss_row + 1, :]
        rep_ref[...] = rep
        mine_ref[...] = mine
        loss_ref[...] = (0.5 * inv_d) * jnp.sum(sq, axis=1, keepdims=True)

    return pl.pallas_call(
        body, name=name,
        out_shape=(jax.ShapeDtypeStruct((n_rep, C), F32), jax.ShapeDtypeStruct((n_mine, C), F32),
                   jax.ShapeDtypeStruct((1, 1), F32)),
        compiler_params=pltpu.CompilerParams(vmem_limit_bytes=VMEM_LIMIT),
    )(g)


def _mm(a, b, *, out_dtype, name, tm=512, tn=None, tk=None, add=None, add_scale=1.0, gather=None):
    M, K = a.shape
    N = b.shape[1]
    tm = min(tm, M)
    tn = N if tn is None else tn
    tk = K if tk is None else tk
    nk = K // tk
    has_add = add is not None
    has_ag = gather is not None
    n_g = len(gather) if has_ag else 0
    n_i, n_j = M // tm, N // tn

    def body(*refs):
        a_ref, b_ref = refs[0], refs[1]
        add_ref = refs[2] if has_add else None
        n_in = 2 + has_add + n_g
        o_ref = refs[n_in]
        if has_ag:
            ag_refs = (refs[n_in - n_g:n_in], refs[n_in + 1:n_in + 1 + n_g]) + tuple(
                refs[n_in + 1 + n_g:n_in + 4 + n_g])
            pid = (pl.program_id(0), pl.program_id(1), pl.program_id(2))

            @pl.when((pid[0] == 0) & (pid[1] == 0) & (pid[2] == 0))
            def _():
                _ag_start(*ag_refs)

        part = jnp.dot(a_ref[...].astype(BF16), b_ref[...].astype(BF16), preferred_element_type=F32)

        def finish(r):
            if has_add:
                r = r + add_scale * add_ref[...].astype(F32)
            o_ref[...] = r.astype(out_dtype)

        if nk == 1:
            finish(part)
        else:
            acc_ref = refs[-1]
            k = pl.program_id(2)

            @pl.when(k == 0)
            def _():
                acc_ref[...] = part

            @pl.when(k > 0)
            def _():
                acc_ref[...] += part

            @pl.when(k == nk - 1)
            def _():
                finish(acc_ref[...])

        if has_ag:
            @pl.when((pid[0] == n_i - 1) & (pid[1] == n_j - 1) & (pid[2] == nk - 1))
            def _():
                _ag_finish(*ag_refs)

    in_specs = [pl.BlockSpec((tm, tk), lambda i, j, k: (i, k)), pl.BlockSpec((tk, tn), lambda i, j, k: (k, j))]
    args = [a, b]
    if has_add:
        in_specs.append(pl.BlockSpec((tm, tn), lambda i, j, k: (i, j)))
        args.append(add)
    out_specs = [pl.BlockSpec((tm, tn), lambda i, j, k: (i, j))]
    out_shape = [jax.ShapeDtypeStruct((M, N), out_dtype)]
    scratch = []
    if has_ag:
        in_specs += [pl.BlockSpec(memory_space=pl.ANY)] * n_g
        args += list(gather)
        out_specs += [pl.BlockSpec(memory_space=pl.ANY)] * n_g
        out_shape += [jax.ShapeDtypeStruct((N_DEV,) + g.shape, g.dtype) for g in gather]
        scratch += _ag_sems(n_g)
    if nk > 1:
        scratch.append(pltpu.VMEM((tm, tn), F32))
    sem = ("arbitrary",) * 3 if has_ag else ("parallel", "parallel", "arbitrary")
    res = pl.pallas_call(
        body, name=name, grid=(n_i, n_j, nk), in_specs=in_specs, out_specs=out_specs, out_shape=out_shape,
        scratch_shapes=scratch, compiler_params=_cp(*sem),
    )(*args)
    return (res[0], list(res[1:])) if has_ag else res[0]


def _mm_ln(a, b, resid, gamma, beta, *, alpha, name, tm=512, tk=None):
    M, K = a.shape
    D = b.shape[1]
    tm = min(tm, M)
    tk = K if tk is None else tk
    nk = K // tk

    def body(a_ref, b_ref, r_ref, g_ref, be_ref, y_ref, xh_ref, rs_ref, *scratch):
        part = jnp.dot(a_ref[...].astype(BF16), b_ref[...].astype(BF16), preferred_element_type=F32)

        def finish(acc):
            z = alpha * r_ref[...] + acc
            mu = jnp.mean(z, axis=-1, keepdims=True)
            zc = z - mu
            var = jnp.mean(zc * zc, axis=-1, keepdims=True)
            rstd = lax.rsqrt(var + LN_EPS)
            xhat = zc * rstd
            y_ref[...] = xhat * g_ref[...] + be_ref[...]
            xh_ref[...] = xhat.astype(BF16)
            rs_ref[...] = rstd

        if nk == 1:
            finish(part)
        else:
            acc_ref = scratch[0]
            k = pl.program_id(1)

            @pl.when(k == 0)
            def _():
                acc_ref[...] = part

            @pl.when(k > 0)
            def _():
                acc_ref[...] += part

            @pl.when(k == nk - 1)
            def _():
                finish(acc_ref[...])

    row = lambda i, k: (i, 0)
    vec = lambda i, k: (0, 0)
    return pl.pallas_call(
        body, name=name, grid=(M // tm, nk),
        in_specs=[pl.BlockSpec((tm, tk), lambda i, k: (i, k)), pl.BlockSpec((tk, D), lambda i, k: (k, 0)),
                  pl.BlockSpec((tm, D), row), pl.BlockSpec((1, D), vec), pl.BlockSpec((1, D), vec)],
        out_specs=[pl.BlockSpec((tm, D), row), pl.BlockSpec((tm, D), row), pl.BlockSpec((tm, 1), row)],
        out_shape=(jax.ShapeDtypeStruct((M, D), F32), jax.ShapeDtypeStruct((M, D), BF16),
                   jax.ShapeDtypeStruct((M, 1), F32)),
        scratch_shapes=[pltpu.VMEM((tm, D), F32)] if nk > 1 else [],
        compiler_params=_cp("parallel", "arbitrary"),
    )(a, b, resid, gamma, beta)


def _mm_tn(a, b, *, name, tka, tn, a_off=0, na=1, b_off=0, nb=1, ts=2048):
    S = a.shape[0]
    ts = min(ts, S)

    def body(a_ref, b_ref, o_ref):
        s = pl.program_id(2)
        part = lax.dot_general(a_ref[...].astype(BF16), b_ref[...].astype(BF16),
                               (((0,), (0,)), ((), ())), preferred_element_type=F32)

        @pl.when(s == 0)
        def _():
            o_ref[...] = part

        @pl.when(s > 0)
        def _():
            o_ref[...] += part

    return pl.pallas_call(
        body, name=name, grid=(na, nb, S // ts),
        in_specs=[pl.BlockSpec((ts, tka), lambda i, j, s: (s, a_off + i)),
                  pl.BlockSpec((ts, tn), lambda i, j, s: (s, b_off + j))],
        out_specs=pl.BlockSpec((tka, tn), lambda i, j, s: (i, j)),
        out_shape=jax.ShapeDtypeStruct((na * tka, nb * tn), F32),
        compiler_params=_cp("parallel", "parallel", "arbitrary"),
    )(a, b)


def _rope_tables(pos, inv_lane, sign_lane, name, ts=512):
    S = pos.shape[0]
    ts = min(ts, S)

    def body(p_ref, inv_ref, sg_ref, cos_ref, sin_ref):
        ang = p_ref[...].astype(F32) * inv_ref[...]
        cos_ref[...] = jnp.cos(ang)
        sin_ref[...] = jnp.sin(ang) * sg_ref[...]

    return pl.pallas_call(
        body, name=name, grid=(S // ts,),
        in_specs=[pl.BlockSpec((ts, 1), lambda i: (i, 0)), pl.BlockSpec((1, 128), lambda i: (0, 0)),
                  pl.BlockSpec((1, 128), lambda i: (0, 0))],
        out_specs=[pl.BlockSpec((ts, 128), lambda i: (i, 0))] * 2,
        out_shape=(jax.ShapeDtypeStruct((S, 128), F32),) * 2,
        compiler_params=_cp("parallel"),
    )(pos, inv_lane, sign_lane)


def _rope_swap(t):
    lane = lax.broadcasted_iota(jnp.int32, (1, 128), 1)
    lo = (lane % HEAD_DIM) < (ROT_DIM // 2)
    return jnp.where(lo, pltpu.roll(t, 128 - ROT_DIM // 2, 1), pltpu.roll(t, ROT_DIM // 2, 1))


def _rope_fwd(t, cos, sin):
    return t * cos + _rope_swap(t) * sin


def _rope_bwd(d, cos, sin):
    lane = lax.broadcasted_iota(jnp.int32, (1, 128), 1)
    return d * cos + jnp.where((lane % HEAD_DIM) < ROT_DIM, _rope_swap(d * sin), 0.0)


def _tile_heads(t):
    lane = lax.broadcasted_iota(jnp.int32, (1, 128), 1)
    r = pltpu.roll(t, 64, 1)
    h0 = jnp.where(lane < 64, t, r)
    h1 = jnp.where(lane < 64, r, t)
    return jnp.concatenate([h0, h0], axis=1), jnp.concatenate([h1, h1], axis=1)


def _fold_heads(d0, d1):
    lane = lax.broadcasted_iota(jnp.int32, (1, 128), 1)

    def fold(d):
        s = d[:, 0:128] + d[:, 128:256]
        return s + pltpu.roll(s, 64, 1)

    return jnp.where(lane < 64, fold(d0), fold(d1))


def _band4(n_keys):
    row = lax.broadcasted_iota(jnp.int32, (GROUP * WINDOW, n_keys), 0) % WINDOW
    col = lax.broadcasted_iota(jnp.int32, (GROUP * WINDOW, n_keys), 1)
    return (col > row) & (col <= row + WINDOW), col


def _head_masks():
    lane = lax.broadcasted_iota(jnp.int32, (1, GROUP * HEAD_DIM), 1)
    return [(lane // HEAD_DIM) == hl for hl in range(GROUP)]


def _stack_heads(t):
    zero = jnp.zeros_like(t)
    return jnp.concatenate([jnp.where(hm, t, zero) for hm in _head_masks()], axis=0)


def _unstack_heads(t4):
    out = None
    for hl, hm in enumerate(_head_masks()):
        part = jnp.where(hm, t4[hl * WINDOW:(hl + 1) * WINDOW], 0.0)
        out = part if out is None else out + part
    return out


def _sink_block(sink_ref, g):
    return jnp.concatenate([jnp.broadcast_to(sink_ref[g * GROUP + hl:g * GROUP + hl + 1, 0:1], (WINDOW, 256))
                            for hl in range(GROUP)], axis=0)


def _sink_column(sink_ref, g):
    return jnp.concatenate([jnp.broadcast_to(sink_ref[g * GROUP + hl:g * GROUP + hl + 1, 0:1], (WINDOW, 1))
                            for hl in range(GROUP)], axis=0)


def _attn_fwd(pq, cos_t, sin_t, sinks_b, *, name, ts=256):
    S = pq.shape[0]
    ts = min(ts, S)
    nq = ts // WINDOW
    scale = HEAD_DIM ** -0.5

    def body(cur_ref, prev_ref, cosc_ref, sinc_ref, cosp_ref, sinp_ref, sink_ref, o_ref, lse_ref):
        i = pl.program_id(0)
        cosc, sinc = cosc_ref[...], sinc_ref[...]
        q = cur_ref[:, 0:512].astype(F32)
        qr = jnp.concatenate(
            [_rope_fwd(q[:, j * 128:(j + 1) * 128], cosc, sinc) for j in range(4)], axis=1) * scale
        qr = qr.astype(BF16)
        kc = _rope_fwd(cur_ref[:, 512:640].astype(F32), cosc, sinc)
        kp = _rope_fwd(prev_ref[:, 0:128].astype(F32), cosp_ref[...], sinp_ref[...])
        k_all = jnp.concatenate([kp, kc], axis=0)
        v_all = jnp.concatenate([prev_ref[:, 128:256].astype(F32), cur_ref[:, 640:768].astype(F32)], axis=0)
        kt = [t.astype(BF16) for t in _tile_heads(k_all)]
        vt = [t.astype(BF16) for t in _tile_heads(v_all)]
        band, col = _band4(2 * WINDOW)
        ones = jnp.ones((2 * WINDOW, 256), BF16)
        key_t = lax.broadcasted_iota(jnp.int32, (2 * WINDOW, GROUP * WINDOW), 0)
        qry_t = lax.broadcasted_iota(jnp.int32, (2 * WINDOW, GROUP * WINDOW), 1) % WINDOW
        band_t = (key_t > qry_t) & (key_t <= qry_t + WINDOW)
        NT = (((1,), (1,)), ((), ()))
        for qb in range(nq):
            rows = slice(qb * WINDOW, (qb + 1) * WINDOW)
            keys = slice(qb * WINDOW, (qb + 2) * WINDOW)
            valid = band & ((col >= WINDOW) | (i * nq + qb > 0))
            valid_t = band_t & ((key_t >= WINDOW) | (i * nq + qb > 0))
            for g in range(2):
                qs = _stack_heads(qr[rows, g * 256:(g + 1) * 256])
                sink = _sink_block(sink_ref, g)
                s = lax.dot_general(qs, kt[g][keys], NT, preferred_element_type=F32)
                s_t = lax.dot_general(kt[g][keys], qs, NT, preferred_element_type=F32)
                m_t = jnp.max(jnp.where(valid_t, s_t, MASK_VALUE), axis=0, keepdims=True)
                m_rep = jnp.broadcast_to(m_t, (WINDOW, GROUP * WINDOW)).T
                m = jnp.maximum(jnp.concatenate([m_rep, m_rep], axis=1), sink)
                e = jnp.exp(jnp.where(valid, s, MASK_VALUE) - m).astype(BF16)
                l = jnp.dot(e, ones, preferred_element_type=F32) + jnp.exp(sink - m)
                pv = jnp.dot(e, vt[g][keys], preferred_element_type=F32)
                o_ref[rows, g * 256:(g + 1) * 256] = (_unstack_heads(pv) / _unstack_heads(l)).astype(BF16)
                lse4 = (m + jnp.log(l))[:, 0:1]
                for hl in range(GROUP):
                    h = g * GROUP + hl
                    lse_ref[rows, h:h + 1] = lse4[hl * WINDOW:(hl + 1) * WINDOW]

    hb = ts // WINDOW
    cur = lambda i: (i, 0)
    prev = lambda i: (jnp.maximum(i * hb - 1, 0), 0)
    return pl.pallas_call(
        body, name=name, grid=(S // ts,),
        in_specs=[pl.BlockSpec((ts, 768), cur),
                  pl.BlockSpec((WINDOW, 256), lambda i: (jnp.maximum(i * hb - 1, 0), 2)),
                  pl.BlockSpec((ts, 128), cur), pl.BlockSpec((ts, 128), cur),
                  pl.BlockSpec((WINDOW, 128), prev), pl.BlockSpec((WINDOW, 128), prev),
                  pl.BlockSpec((8, 128), lambda i: (0, 0))],
        out_specs=[pl.BlockSpec((ts, 512), cur), pl.BlockSpec((ts, 8), cur)],
        out_shape=(jax.ShapeDtypeStruct((S, 512), BF16), jax.ShapeDtypeStruct((S, 8), F32)),
        compiler_params=_cp("parallel"),
    )(pq, pq, cos_t, sin_t, cos_t, sin_t, sinks_b)


def _attn_bwd(pq, cos_t, sin_t, sinks_b, do, o, lse, *, name, ts=256):
    S = pq.shape[0]
    ts = min(ts, S)
    nq = ts // WINDOW
    nt = S // ts
    scale = HEAD_DIM ** -0.5
    NT = (((1,), (1,)), ((), ()))
    TN = (((0,), (0,)), ((), ()))

    def body(cur_ref, prev_ref, nxt_ref, cosc_ref, sinc_ref, cosp_ref, sinp_ref, cosn_ref, sinn_ref, sink_ref,
             doc_ref, don_ref, oc_ref, on_ref, lsec_ref, lsen_ref, dpq_ref, dsink_ref):
        i = pl.program_id(0)
        last = i == nt - 1
        cosc, sinc = cosc_ref[...], sinc_ref[...]
        cose = jnp.concatenate([cosc, cosn_ref[...]], axis=0)
        sine = jnp.concatenate([sinc, sinn_ref[...]], axis=0)
        q = jnp.concatenate([cur_ref[:, 0:512], nxt_ref[:, 0:512]], axis=0).astype(F32)
        qr = jnp.concatenate(
            [_rope_fwd(q[:, j * 128:(j + 1) * 128], cose, sine) for j in range(4)], axis=1) * scale
        qr = qr.astype(BF16)
        kc = _rope_fwd(cur_ref[:, 512:640].astype(F32), cosc, sinc)
        kp = _rope_fwd(prev_ref[:, 0:128].astype(F32), cosp_ref[...], sinp_ref[...])
        k_all = jnp.concatenate([kp, kc], axis=0)
        v_all = jnp.concatenate([prev_ref[:, 128:256].astype(F32), cur_ref[:, 640:768].astype(F32)], axis=0)
        kt = [t.astype(BF16) for t in _tile_heads(k_all)]
        vt = [t.astype(BF16) for t in _tile_heads(v_all)]
        don = jnp.where(last, jnp.zeros_like(don_ref[...]), don_ref[...])
        do_e = jnp.concatenate([doc_ref[...], don], axis=0)
        o_e = jnp.concatenate([oc_ref[...], on_ref[...]], axis=0)
        band2, col2 = _band4(2 * WINDOW)
        band1, _ = _band4(WINDOW)
        ones = jnp.ones((256, 256), BF16)

        @pl.when(i == 0)
        def _():
            dsink_ref[...] = jnp.zeros_like(dsink_ref)

        dk_acc = [[None] * (nq + 1) for _ in range(2)]
        dv_acc = [[None] * (nq + 1) for _ in range(2)]

        def add(acc, g, e, val):
            acc[g][e] = val if acc[g][e] is None else acc[g][e] + val

        for qb in range(nq + 1):
            halo = qb == nq
            rows = slice(qb * WINDOW, (qb + 1) * WINDOW)
            if halo:
                keys = slice(qb * WINDOW, (qb + 1) * WINDOW)
                valid = band1 & jnp.logical_not(last)
            else:
                keys = slice(qb * WINDOW, (qb + 2) * WINDOW)
                valid = band2 & ((col2 >= WINDOW) | (i * nq + qb > 0))
            dq_parts = []
            for g in range(2):
                qs = _stack_heads(qr[rows, g * 256:(g + 1) * 256])
                dos = _stack_heads(do_e[rows, g * 256:(g + 1) * 256])
                o_g = o_e[rows, g * 256:(g + 1) * 256].astype(F32)
                kt_b, vt_b = kt[g][keys], vt[g][keys]
                lse_src = lsen_ref if halo else lsec_ref
                lse_rows = slice(0, WINDOW) if halo else rows
                big_l = jnp.concatenate([lse_src[lse_rows, g * GROUP + hl:g * GROUP + hl + 1] for hl in range(GROUP)],
                                        axis=0)
                delta = jnp.dot((dos.astype(F32) * jnp.concatenate([o_g] * GROUP, axis=0)).astype(BF16), ones,
                                preferred_element_type=F32)
                s = lax.dot_general(qs, kt_b, NT, preferred_element_type=F32)
                p = jnp.exp(jnp.where(valid, s, MASK_VALUE) - big_l)
                dp = lax.dot_general(dos, vt_b, NT, preferred_element_type=F32)
                ds = (p * (dp - delta[:, 0:p.shape[1]])).astype(BF16)
                dk_g = lax.dot_general(ds, qs, TN, preferred_element_type=F32)
                dv_g = lax.dot_general(p.astype(BF16), dos, TN, preferred_element_type=F32)
                if not halo:
                    dq_parts.append(_unstack_heads(jnp.dot(ds, kt_b, preferred_element_type=F32)))
                    dsink4 = jnp.exp(_sink_column(sink_ref, g) - big_l) * delta[:, 0:1]
                    for hl in range(GROUP):
                        h = g * GROUP + hl
                        dsink_h = -jnp.sum(dsink4[hl * WINDOW:(hl + 1) * WINDOW], axis=0, keepdims=True)
                        dsink_ref[h:h + 1, :] += jnp.broadcast_to(dsink_h, (1, 128))
                add(dk_acc, g, qb, dk_g[0:WINDOW])
                add(dv_acc, g, qb, dv_g[0:WINDOW])
                if not halo:
                    add(dk_acc, g, qb + 1, dk_g[WINDOW:2 * WINDOW])
                    add(dv_acc, g, qb + 1, dv_g[WINDOW:2 * WINDOW])
            if not halo:
                cs, sn = cosc[rows], sinc[rows]
                for g in range(2):
                    dq_g = dq_parts[g] * scale
                    for j in range(2):
                        c0 = g * 256 + j * 128
                        dpq_ref[rows, c0:c0 + 128] = _rope_bwd(dq_g[:, j * 128:(j + 1) * 128], cs, sn).astype(BF16)
        for e in range(1, nq + 1):
            rows = slice((e - 1) * WINDOW, e * WINDOW)
            dk = _fold_heads(dk_acc[0][e], dk_acc[1][e])
            dv = _fold_heads(dv_acc[0][e], dv_acc[1][e])
            dpq_ref[rows, 512:640] = _rope_bwd(dk, cosc[rows], sinc[rows]).astype(BF16)
            dpq_ref[rows, 640:768] = dv.astype(BF16)

    hb = ts // WINDOW
    nblk = S // WINDOW
    cur = lambda i: (i, 0)
    prev = lambda i: (jnp.maximum(i * hb - 1, 0), 0)
    nxt = lambda i: (jnp.minimum((i + 1) * hb, nblk - 1), 0)
    return pl.pallas_call(
        body, name=name, grid=(nt,),
        in_specs=[pl.BlockSpec((ts, 768), cur),
                  pl.BlockSpec((WINDOW, 256), lambda i: (jnp.maximum(i * hb - 1, 0), 2)),
                  pl.BlockSpec((WINDOW, 768), nxt),
                  pl.BlockSpec((ts, 128), cur), pl.BlockSpec((ts, 128), cur),
                  pl.BlockSpec((WINDOW, 128), prev), pl.BlockSpec((WINDOW, 128), prev),
                  pl.BlockSpec((WINDOW, 128), nxt), pl.BlockSpec((WINDOW, 128), nxt),
                  pl.BlockSpec((8, 128), lambda i: (0, 0)),
                  pl.BlockSpec((ts, 512), cur), pl.BlockSpec((WINDOW, 512), nxt),
                  pl.BlockSpec((ts, 512), cur), pl.BlockSpec((WINDOW, 512), nxt),
                  pl.BlockSpec((ts, 8), cur), pl.BlockSpec((WINDOW, 8), nxt)],
        out_specs=[pl.BlockSpec((ts, 768), cur), pl.BlockSpec((8, 128), lambda i: (0, 0))],
        out_shape=(jax.ShapeDtypeStruct((S, 768), BF16), jax.ShapeDtypeStruct((8, 128), F32)),
        compiler_params=_cp("arbitrary"),
    )(pq, pq, pq, cos_t, sin_t, cos_t, sin_t, cos_t, sin_t, sinks_b, do, do, o, o, lse, lse)


def _shift_dn(x, k):
    return pltpu.roll(x, k, 0)


def _shift_up(x, k):
    return pltpu.roll(x, x.shape[0] - k, 0)


def _pool_lane_select(vals):
    lane = lax.broadcasted_iota(jnp.int32, (1, 256), 1)
    out = vals[3]
    for g in (2, 1, 0):
        out = jnp.where(lane < 64 * (g + 1), vals[g], out)
    return out


def _pool_inv_count(t0, n):
    t = t0 + lax.broadcasted_iota(jnp.int32, (n, 256), 0)
    lane = lax.broadcasted_iota(jnp.int32, (n, 256), 1)
    w = jnp.where(lane < 64, 2, jnp.where(lane < 128, 4, jnp.where(lane < 192, 8, 16)))
    return 1.0 / jnp.minimum(t + 1, w).astype(F32)


def _pooled(u_ext, t0, n):
    s2 = u_ext + _shift_dn(u_ext, 1)
    s4 = s2 + _shift_dn(s2, 2)
    s8 = s4 + _shift_dn(s4, 4)
    s16 = s8 + _shift_dn(s8, 8)
    win = _pool_lane_select([s2, s4, s8, s16])[HALO:HALO + n]
    return win * _pool_inv_count(t0, n) - u_ext[HALO:HALO + n]


def _poolconv_fwd(pp, wbd, pool_scale, conv_w, *, name, ts=512):
    S = pp.shape[0]
    ts = min(ts, S)

    def body(cur_ref, prev_ref, wbd_ref, sc_ref, cw_ref, oa_ref, oc_ref):
        i = pl.program_id(0)
        prev = jnp.where(i > 0, prev_ref[...].astype(F32), 0.0)
        u_ext = jnp.concatenate([prev[:, 0:256], cur_ref[:, 0:256].astype(F32)], axis=0)
        pooled = _pooled(u_ext, i * ts, ts)
        mixed = jnp.dot(pooled.astype(BF16), wbd_ref[...], preferred_element_type=F32)
        oa_ref[...] = (mixed * sc_ref[...]).astype(BF16)
        v_ext = jnp.concatenate([prev[:, 256:512] * prev[:, 768:1024],
                                 cur_ref[:, 256:512].astype(F32) * cur_ref[:, 768:1024].astype(F32)], axis=0)
        cv = cw_ref[2:3, :] * v_ext + cw_ref[1:2, :] * _shift_dn(v_ext, 1) + cw_ref[0:1, :] * _shift_dn(v_ext, 2)
        oc_ref[...] = (cur_ref[:, 512:768].astype(F32) * cv[HALO:HALO + ts]).astype(BF16)

    hb = ts // HALO
    cur = lambda i: (i, 0)
    const = lambda i: (0, 0)
    return pl.pallas_call(
        body, name=name, grid=(S // ts,),
        in_specs=[pl.BlockSpec((ts, 1024), cur),
                  pl.BlockSpec((HALO, 1024), lambda i: (jnp.maximum(i * hb - 1, 0), 0)),
                  pl.BlockSpec((256, 256), const), pl.BlockSpec((1, 256), const), pl.BlockSpec((3, 256), const)],
        out_specs=[pl.BlockSpec((ts, 256), cur)] * 2,
        out_shape=(jax.ShapeDtypeStruct((S, 256), BF16),) * 2,
        compiler_params=_cp("parallel"),
    )(pp, pp, wbd, pool_scale, conv_w)


def _poolconv_bwd(pp, do_a, do_c, wbd, wbd_t, pool_scale, conv_w, *, name, ts=512):
    S = pp.shape[0]
    ts = min(ts, S)
    nt = S // ts
    n_e = ts + 2 * HALO

    def body(cur_ref, prev_ref, nxt_ref, dac_ref, dan_ref, dcc_ref, dcn_ref, wbd_ref, wbdt_ref, sc_ref, cw_ref,
             dpp_ref, pooled_ref, dmixed_ref, dsc_ref, dcw_ref):
        i = pl.program_id(0)

        @pl.when(i == 0)
        def _():
            dsc_ref[...] = jnp.zeros_like(dsc_ref)
            dcw_ref[...] = jnp.zeros_like(dcw_ref)

        prev = jnp.where(i > 0, prev_ref[...].astype(F32), 0.0)
        nxt = nxt_ref[...].astype(F32)
        cur = cur_ref[...].astype(F32)
        not_last = i < nt - 1
        da_n = jnp.where(not_last, dan_ref[...].astype(F32), 0.0)
        dc_n = jnp.where(not_last, dcn_ref[...].astype(F32), 0.0)
        zeros_h = jnp.zeros((HALO, 256), F32)
        sc = sc_ref[...]

        u_ext = jnp.concatenate([prev[:, 0:256], cur[:, 0:256]], axis=0)
        pooled = _pooled(u_ext, i * ts, ts)
        pooled_b = pooled.astype(BF16)
        pooled_ref[...] = pooled_b
        mixed = jnp.dot(pooled_b, wbd_ref[...], preferred_element_type=F32)
        da_c = dac_ref[...].astype(F32)
        dsc_ref[...] += jnp.sum(da_c * mixed, axis=0, keepdims=True)
        dmixed_e = jnp.concatenate([da_c, da_n], axis=0) * sc
        dmixed_ref[...] = dmixed_e[0:ts].astype(BF16)
        dpooled = jnp.dot(dmixed_e.astype(BF16), wbdt_ref[...], preferred_element_type=F32)
        qd = dpooled * _pool_inv_count(i * ts, ts + HALO)
        f2 = qd + _shift_up(qd, 1)
        f4 = f2 + _shift_up(f2, 2)
        f8 = f4 + _shift_up(f4, 4)
        f16 = f8 + _shift_up(f8, 8)
        du = (_pool_lane_select([f2, f4, f8, f16]) - dpooled)[0:ts]
        dpp_ref[:, 0:256] = du.astype(BF16)

        xc_e = jnp.concatenate([prev[:, 256:512], cur[:, 256:512], nxt[:, 256:512]], axis=0)
        gc_e = jnp.concatenate([prev[:, 768:1024], cur[:, 768:1024], nxt[:, 768:1024]], axis=0)
        gb_e = jnp.concatenate([zeros_h, cur[:, 512:768], nxt[:, 512:768]], axis=0)
        dc_e = jnp.concatenate([zeros_h, dcc_ref[...].astype(F32), dc_n], axis=0)
        v_e = xc_e * gc_e
        v1, v2 = _shift_dn(v_e, 1), _shift_dn(v_e, 2)
        w0, w1, w2 = cw_ref[0:1, :], cw_ref[1:2, :], cw_ref[2:3, :]
        cv = w2 * v_e + w1 * v1 + w0 * v2
        dcv = dc_e * gb_e
        dv = w2 * dcv + w1 * _shift_up(dcv, 1) + w0 * _shift_up(dcv, 2)
        tile = slice(HALO, HALO + ts)
        dpp_ref[:, 256:512] = (dv * gc_e)[tile].astype(BF16)
        dpp_ref[:, 512:768] = (dc_e * cv)[tile].astype(BF16)
        dpp_ref[:, 768:1024] = (dv * xc_e)[tile].astype(BF16)
        dcv_t = dcv[tile]
        dcw_ref[0:1, :] += jnp.sum(dcv_t * v2[tile], axis=0, keepdims=True)
        dcw_ref[1:2, :] += jnp.sum(dcv_t * v1[tile], axis=0, keepdims=True)
        dcw_ref[2:3, :] += jnp.sum(dcv_t * v_e[tile], axis=0, keepdims=True)

    hb = ts // HALO
    nblk = S // HALO
    cur = lambda i: (i, 0)
    const = lambda i: (0, 0)
    prev = lambda i: (jnp.maximum(i * hb - 1, 0), 0)
    nxt = lambda i: (jnp.minimum((i + 1) * hb, nblk - 1), 0)
    del n_e
    return pl.pallas_call(
        body, name=name, grid=(nt,),
        in_specs=[pl.BlockSpec((ts, 1024), cur), pl.BlockSpec((HALO, 1024), prev), pl.BlockSpec((HALO, 1024), nxt),
                  pl.BlockSpec((ts, 256), cur), pl.BlockSpec((HALO, 256), nxt),
                  pl.BlockSpec((ts, 256), cur), pl.BlockSpec((HALO, 256), nxt),
                  pl.BlockSpec((256, 256), const), pl.BlockSpec((256, 256), const),
                  pl.BlockSpec((1, 256), const), pl.BlockSpec((3, 256), const)],
        out_specs=[pl.BlockSpec((ts, 1024), cur), pl.BlockSpec((ts, 256), cur), pl.BlockSpec((ts, 256), cur),
                   pl.BlockSpec((1, 256), const), pl.BlockSpec((3, 256), const)],
        out_shape=(jax.ShapeDtypeStruct((S, 1024), BF16), jax.ShapeDtypeStruct((S, 256), BF16),
                   jax.ShapeDtypeStruct((S, 256), BF16), jax.ShapeDtypeStruct((1, 256), F32),
                   jax.ShapeDtypeStruct((3, 256), F32)),
        compiler_params=_cp("arbitrary"),
    )(pp, pp, pp, do_a, do_a, do_c, do_c, wbd, wbd_t, pool_scale, conv_w)


def _sigmoid(x):
    return 0.5 * jnp.tanh(0.5 * x) + 0.5


def _merge_fwd(o_a, o_b, o_c, glog, w_br, *, name, ts=512):
    S = o_a.shape[0]
    D = w_br.shape[1]
    ts = min(ts, S)

    def body(oa_ref, ob_ref, oc_ref, gl_ref, w_ref, m_ref):
        pa = jnp.dot(oa_ref[...], w_ref[0:256, :], preferred_element_type=F32)
        pb = jnp.dot(ob_ref[...], w_ref[256:768, :], preferred_element_type=F32)
        pc = jnp.dot(oc_ref[...], w_ref[768:1024, :], preferred_element_type=F32)
        m = _sigmoid(gl_ref[:, 0:D].astype(F32)) * pa
        m = m + _sigmoid(gl_ref[:, D:2 * D].astype(F32)) * pb
        m = m + _sigmoid(gl_ref[:, 2 * D:3 * D].astype(F32)) * pc
        m_ref[...] = m.astype(BF16)

    cur = lambda i: (i, 0)
    return pl.pallas_call(
        body, name=name, grid=(S // ts,),
        in_specs=[pl.BlockSpec((ts, 256), cur), pl.BlockSpec((ts, 512), cur), pl.BlockSpec((ts, 256), cur),
                  pl.BlockSpec((ts, 3 * D), cur), pl.BlockSpec((1024, D), lambda i: (0, 0))],
        out_specs=pl.BlockSpec((ts, D), cur),
        out_shape=jax.ShapeDtypeStruct((S, D), BF16),
        compiler_params=_cp("parallel"),
    )(o_a, o_b, o_c, glog, w_br)


def _merge_bwd(dm, o_a, o_b, o_c, glog, w_br, w_br_t, *, name, ts=256):
    S = o_a.shape[0]
    D = w_br.shape[1]
    ts = min(ts, S)

    def body(dm_ref, oa_ref, ob_ref, oc_ref, gl_ref, w_ref, wt_ref, dgl_ref, dp_ref, doa_ref, dob_ref, doc_ref):
        dmv = dm_ref[...].astype(F32)
        branches = ((oa_ref, 0, 256, doa_ref), (ob_ref, 256, 768, dob_ref), (oc_ref, 768, 1024, doc_ref))
        for b, (o_ref, r0, r1, do_ref) in enumerate(branches):
            prod = jnp.dot(o_ref[...], w_ref[r0:r1, :], preferred_element_type=F32)
            gate = _sigmoid(gl_ref[:, b * D:(b + 1) * D].astype(F32))
            dgl_ref[:, b * D:(b + 1) * D] = (dmv * prod * gate * (1.0 - gate)).astype(BF16)
            dprod = (dmv * gate).astype(BF16)
            dp_ref[:, b * D:(b + 1) * D] = dprod
            do_ref[...] = jnp.dot(dprod, wt_ref[:, r0:r1], preferred_element_type=F32).astype(BF16)

    cur = lambda i: (i, 0)
    const = lambda i: (0, 0)
    return pl.pallas_call(
        body, name=name, grid=(S // ts,),
        in_specs=[pl.BlockSpec((ts, D), cur), pl.BlockSpec((ts, 256), cur), pl.BlockSpec((ts, 512), cur),
                  pl.BlockSpec((ts, 256), cur), pl.BlockSpec((ts, 3 * D), cur),
                  pl.BlockSpec((1024, D), const), pl.BlockSpec((D, 1024), const)],
        out_specs=[pl.BlockSpec((ts, 3 * D), cur), pl.BlockSpec((ts, 3 * D), cur), pl.BlockSpec((ts, 256), cur),
                   pl.BlockSpec((ts, 512), cur), pl.BlockSpec((ts, 256), cur)],
        out_shape=(jax.ShapeDtypeStruct((S, 3 * D), BF16), jax.ShapeDtypeStruct((S, 3 * D), BF16),
                   jax.ShapeDtypeStruct((S, 256), BF16), jax.ShapeDtypeStruct((S, 512), BF16),
                   jax.ShapeDtypeStruct((S, 256), BF16)),
        compiler_params=_cp("parallel"),
    )(dm, o_a, o_b, o_c, glog, w_br, w_br_t)


def _ffn_act_fwd(up_pre, fcw, *, name, tc, ts=512):
    S, F2 = up_pre.shape
    ts = min(ts, S)
    nj = F2 // (2 * tc)

    def body(cur_ref, prev_ref, w_ref, h_ref):
        i = pl.program_id(1)
        prev = jnp.where(i > 0, prev_ref[...].astype(F32), 0.0)
        x = jnp.concatenate([prev, cur_ref[...].astype(F32)], axis=0)
        up = (w_ref[2:3, :] * x + w_ref[1:2, :] * _shift_dn(x, 1) + w_ref[0:1, :] * _shift_dn(x, 2))[HALO:HALO + ts]
        a, b = up[:, 0:tc], up[:, tc:2 * tc]
        h_ref[...] = (a * _sigmoid(a) * b).astype(BF16)

    hb = ts // HALO
    return pl.pallas_call(
        body, name=name, grid=(nj, S // ts),
        in_specs=[pl.BlockSpec((ts, 2 * tc), lambda j, i: (i, j)),
                  pl.BlockSpec((HALO, 2 * tc), lambda j, i: (jnp.maximum(i * hb - 1, 0), j)),
                  pl.BlockSpec((3, 2 * tc), lambda j, i: (0, j))],
        out_specs=pl.BlockSpec((ts, tc), lambda j, i: (i, j)),
        out_shape=jax.ShapeDtypeStruct((S, F2 // 2), BF16),
        compiler_params=_cp("parallel", "parallel"),
    )(up_pre, up_pre, fcw)


def _ffn_act_bwd(up_pre, dh, fcw, *, name, tc, ts=512, scatter=None):
    S, F2 = up_pre.shape
    ts = min(ts, S)
    nt = S // ts
    nj = F2 // (2 * tc)
    has_rs = scatter is not None

    def body(cur_ref, prev_ref, nxt_ref, dhc_ref, dhn_ref, w_ref, *rest):
        if has_rs:
            q_ref, dpre_ref, dw_ref, recv_ref, send_sems, recv_sems = rest
            rs_refs = (q_ref, recv_ref, send_sems, recv_sems)

            @pl.when((pl.program_id(0) == 0) & (pl.program_id(1) == 0))
            def _():
                _rs_chips_start(*rs_refs)
        else:
            dpre_ref, dw_ref = rest
        i = pl.program_id(1)

        @pl.when(i == 0)
        def _():
            dw_ref[...] = jnp.zeros_like(dw_ref)

        prev = jnp.where(i > 0, prev_ref[...].astype(F32), 0.0)
        x = jnp.concatenate([prev, cur_ref[...].astype(F32), nxt_ref[...].astype(F32)], axis=0)
        dh_n = jnp.where(i < nt - 1, dhn_ref[...].astype(F32), 0.0)
        dh_e = jnp.concatenate([jnp.zeros((HALO, tc), F32), dhc_ref[...].astype(F32), dh_n], axis=0)
        w0, w1, w2 = w_ref[0:1, :], w_ref[1:2, :], w_ref[2:3, :]
        x1, x2 = _shift_dn(x, 1), _shift_dn(x, 2)
        up = w2 * x + w1 * x1 + w0 * x2
        a, b = up[:, 0:tc], up[:, tc:2 * tc]
        sg = _sigmoid(a)
        da = dh_e * b * (sg * (1.0 + a * (1.0 - sg)))
        db = dh_e * (a * sg)
        dup = jnp.concatenate([da, db], axis=1)
        dpre = w2 * dup + w1 * _shift_up(dup, 1) + w0 * _shift_up(dup, 2)
        tile = slice(HALO, HALO + ts)
        dpre_ref[...] = dpre[tile].astype(BF16)
        dup_t = dup[tile]
        dw_ref[0:1, :] += jnp.sum(dup_t * x2[tile], axis=0, keepdims=True)
        dw_ref[1:2, :] += jnp.sum(dup_t * x1[tile], axis=0, keepdims=True)
        dw_ref[2:3, :] += jnp.sum(dup_t * x[tile], axis=0, keepdims=True)

        if has_rs:
            @pl.when((pl.program_id(0) == nj - 1) & (pl.program_id(1) == nt - 1))
            def _():
                _rs_chips_finish(*rs_refs)

    hb = ts // HALO
    nblk = S // HALO
    prev = lambda j, i: (jnp.maximum(i * hb - 1, 0), j)
    nxt = lambda j, i: (jnp.minimum((i + 1) * hb, nblk - 1), j)
    in_specs = [pl.BlockSpec((ts, 2 * tc), lambda j, i: (i, j)), pl.BlockSpec((HALO, 2 * tc), prev),
                pl.BlockSpec((HALO, 2 * tc), nxt),
                pl.BlockSpec((ts, tc), lambda j, i: (i, j)), pl.BlockSpec((HALO, tc), nxt),
                pl.BlockSpec((3, 2 * tc), lambda j, i: (0, j))]
    out_specs = [pl.BlockSpec((ts, 2 * tc), lambda j, i: (i, j)), pl.BlockSpec((3, 2 * tc), lambda j, i: (0, j))]
    out_shape = [jax.ShapeDtypeStruct((S, F2), BF16), jax.ShapeDtypeStruct((3, F2), F32)]
    args = [up_pre, up_pre, up_pre, dh, dh, fcw]
    if has_rs:
        in_specs.append(pl.BlockSpec(memory_space=pl.ANY))
        args.append(scatter)
        out_specs.append(pl.BlockSpec(memory_space=pl.ANY))
        out_shape.append(jax.ShapeDtypeStruct((3,) + scatter.shape[1:], scatter.dtype))
    return pl.pallas_call(
        body, name=name, grid=(nj, nt), in_specs=in_specs, out_specs=out_specs, out_shape=out_shape,
        scratch_shapes=_RS_SEMS if has_rs else [],
        compiler_params=_cp("arbitrary", "arbitrary") if has_rs else _cp("parallel", "arbitrary"),
    )(*args)


FFN_CHUNK = 128
FFN_DOT_CHUNKS = 4


def _conv3(x, w_ref, cols):
    x1, x2 = _shift_dn(x, 1), _shift_dn(x, 2)
    return w_ref[2:3, cols] * x + w_ref[1:2, cols] * x1 + w_ref[0:1, cols] * x2, x1, x2


def _ffn_down_fwd(up_pre, fcw, w_down3, resid, gamma, beta, *, alpha, name, tc, ts=256, gather=None):
    S, F2 = up_pre.shape
    D = resid.shape[1]
    ts = min(ts, S)
    nt = S // ts
    nj = F2 // (2 * tc)
    has_ag = gather is not None
    n_g = len(gather) if has_ag else 0

    def body(cur_ref, prev_ref, w_ref, wd_ref, r_ref, g_ref, be_ref, *rest):
        h_ref, y_ref, xh_ref, rs_ref, up_ref = rest[n_g:n_g + 5]
        acc_ref = rest[2 * n_g + 5]
        if has_ag:
            ag_refs = (rest[:n_g], rest[n_g + 5:2 * n_g + 5]) + tuple(rest[2 * n_g + 6:2 * n_g + 9])
        i, j = pl.program_id(0), pl.program_id(1)
        if has_ag:
            @pl.when((i == 0) & (j == 0))
            def _():
                _ag_start(*ag_refs)

        part = None
        for c in range(tc // FFN_CHUNK):
            halves = []
            for half in range(2):
                cols = slice(half * tc + c * FFN_CHUNK, half * tc + (c + 1) * FFN_CHUNK)
                prev = jnp.where(i > 0, prev_ref[:, cols].astype(F32), 0.0)
                x = jnp.concatenate([prev, cur_ref[:, cols].astype(F32)], axis=0)
                halves.append(_conv3(x, w_ref, cols)[0][HALO:HALO + ts])
                up_ref[:, cols] = halves[-1].astype(BF16)
            a, b = halves
            h_ref[:, c * FFN_CHUNK:(c + 1) * FFN_CHUNK] = (a * _sigmoid(a) * b).astype(BF16)
            if (c + 1) % FFN_DOT_CHUNKS == 0 or c + 1 == tc // FFN_CHUNK:
                k0 = (c // FFN_DOT_CHUNKS) * FFN_DOT_CHUNKS * FFN_CHUNK
                piece = jnp.dot(h_ref[:, k0:(c + 1) * FFN_CHUNK], wd_ref[j, k0:(c + 1) * FFN_CHUNK, :],
                                preferred_element_type=F32)
                part = piece if part is None else part + piece

        @pl.when(j == 0)
        def _():
            acc_ref[...] = part

        @pl.when(j > 0)
        def _():
            acc_ref[...] += part

        @pl.when(j == nj - 1)
        def _():
            z = alpha * r_ref[...] + acc_ref[...]
            mu = jnp.mean(z, axis=-1, keepdims=True)
            zc = z - mu
            var = jnp.mean(zc * zc, axis=-1, keepdims=True)
            rstd = lax.rsqrt(var + LN_EPS)
            xhat = zc * rstd
            y_ref[...] = xhat * g_ref[...] + be_ref[...]
            xh_ref[...] = xhat.astype(BF16)
            rs_ref[...] = rstd

        if has_ag:
            @pl.when((i == nt - 1) & (j == nj - 1))
            def _():
                _ag_finish(*ag_refs)

    hb = ts // HALO
    row = lambda i, j: (i, 0)
    vec = lambda i, j: (0, 0)
    in_specs = [pl.BlockSpec((ts, 2 * tc), lambda i, j: (i, j)),
                pl.BlockSpec((HALO, 2 * tc), lambda i, j: (jnp.maximum(i * hb - 1, 0), j)),
                pl.BlockSpec((3, 2 * tc), lambda i, j: (0, j)),
                pl.BlockSpec((nj, tc, D), lambda i, j: (0, 0, 0)),
                pl.BlockSpec((ts, D), row), pl.BlockSpec((1, D), vec), pl.BlockSpec((1, D), vec)]
    out_specs = [pl.BlockSpec((ts, tc), lambda i, j: (i, j)), pl.BlockSpec((ts, D), row), pl.BlockSpec((ts, D), row),
                 pl.BlockSpec((ts, 1), row), pl.BlockSpec((ts, 2 * tc), lambda i, j: (i, j))]
    out_shape = [jax.ShapeDtypeStruct((S, F2 // 2), BF16), jax.ShapeDtypeStruct((S, D), F32),
                 jax.ShapeDtypeStruct((S, D), BF16), jax.ShapeDtypeStruct((S, 1), F32),
                 jax.ShapeDtypeStruct((S, F2), BF16)]
    args = [up_pre, up_pre, fcw, w_down3, resid, gamma, beta]
    scratch = [pltpu.VMEM((ts, D), F32)]
    if has_ag:
        in_specs += [pl.BlockSpec(memory_space=pl.ANY)] * n_g
        args += list(gather)
        out_specs += [pl.BlockSpec(memory_space=pl.ANY)] * n_g
        out_shape += [jax.ShapeDtypeStruct((N_DEV,) + g.shape, g.dtype) for g in gather]
        scratch += _ag_sems(n_g)
    res = pl.pallas_call(
        body, name=name, grid=(nt, nj), in_specs=in_specs, out_specs=out_specs, out_shape=out_shape,
        scratch_shapes=scratch, compiler_params=_cp("arbitrary", "arbitrary"),
    )(*args)
    return tuple(res[:5]) + ((list(res[5:]),) if has_ag else ())


def _ffn_up_bwd(up_pre, up, dh, fcw, w_up_t3, dz, *, alpha, name, tc, ts=256, scatter=None):
    S, F2 = up_pre.shape
    D = dz.shape[1]
    ts = min(ts, S)
    nt = S // ts
    nj = F2 // (2 * tc)
    has_rs = scatter is not None
    tile = slice(0, ts)

    def body(x_ref, upc_ref, upn_ref, dhc_ref, dhn_ref, w_ref, wt_ref, dz_ref, *rest):
        if has_rs:
            q_ref, dpre_ref, dx_ref, dw_ref, recv_ref, acc_ref, send_sems, recv_sems = rest
            rs_refs = (q_ref, recv_ref, send_sems, recv_sems)
        else:
            dpre_ref, dx_ref, dw_ref, acc_ref = rest
        i, j = pl.program_id(0), pl.program_id(1)

        @pl.when((i == 0) & (j == 0))
        def _():
            dw_ref[...] = jnp.zeros_like(dw_ref)
            if has_rs:
                _rs_chips_start(*rs_refs)

        part = None
        for c in range(tc // FFN_CHUNK):
            lanes = slice(c * FFN_CHUNK, (c + 1) * FFN_CHUNK)
            dh_n = jnp.where(i < nt - 1, dhn_ref[:, lanes].astype(F32), 0.0)
            dh_e = jnp.concatenate([dhc_ref[:, lanes].astype(F32), dh_n], axis=0)
            cols_of = [slice(half * tc + c * FFN_CHUNK, half * tc + (c + 1) * FFN_CHUNK) for half in range(2)]
            a, b = [jnp.concatenate([upc_ref[:, cols].astype(F32), upn_ref[:, cols].astype(F32)], axis=0)
                    for cols in cols_of]
            sg = _sigmoid(a)
            dups = [dh_e * b * (sg * (1.0 + a * (1.0 - sg))), dh_e * (a * sg)]
            for half in range(2):
                cols, dup = cols_of[half], dups[half]
                dup1, dup2 = _shift_up(dup, 1), _shift_up(dup, 2)
                dpre = w_ref[2:3, cols] * dup + w_ref[1:2, cols] * dup1 + w_ref[0:1, cols] * dup2
                dpre_ref[:, cols] = dpre[tile].astype(BF16)
                x = x_ref[:, cols].astype(F32)
                dw_ref[j, 0:1, cols] += jnp.sum(dup2[tile] * x, axis=0, keepdims=True)
                dw_ref[j, 1:2, cols] += jnp.sum(dup1[tile] * x, axis=0, keepdims=True)
                dw_ref[j, 2:3, cols] += jnp.sum(dup[tile] * x, axis=0, keepdims=True)
            if (c + 1) % FFN_DOT_CHUNKS == 0 or c + 1 == tc // FFN_CHUNK:
                k0 = (c // FFN_DOT_CHUNKS) * FFN_DOT_CHUNKS * FFN_CHUNK
                for half in range(2):
                    ks = slice(half * tc + k0, half * tc + (c + 1) * FFN_CHUNK)
                    piece = jnp.dot(dpre_ref[:, ks], wt_ref[j, ks, :], preferred_element_type=F32)
                    part = piece if part is None else part + piece

        @pl.when(j == 0)
        def _():
            acc_ref[...] = part

        @pl.when(j > 0)
        def _():
            acc_ref[...] += part

        @pl.when(j == nj - 1)
        def _():
            dx_ref[...] = acc_ref[...] + alpha * dz_ref[...]

        if has_rs:
            @pl.when((i == nt - 1) & (j == nj - 1))
            def _():
                _rs_chips_finish(*rs_refs)

    hb = ts // HALO
    nblk = S // HALO
    nxt = lambda i, j: (jnp.minimum((i + 1) * hb, nblk - 1), j)
    row = lambda i, j: (i, 0)
    in_specs = [pl.BlockSpec((ts, 2 * tc), lambda i, j: (i, j)),
                pl.BlockSpec((ts, 2 * tc), lambda i, j: (i, j)), pl.BlockSpec((HALO, 2 * tc), nxt),
                pl.BlockSpec((ts, tc), lambda i, j: (i, j)), pl.BlockSpec((HALO, tc), nxt),
                pl.BlockSpec((3, 2 * tc), lambda i, j: (0, j)),
                pl.BlockSpec((nj, 2 * tc, D), lambda i, j: (0, 0, 0)),
                pl.BlockSpec((ts, D), row)]
    out_specs = [pl.BlockSpec((ts, 2 * tc), lambda i, j: (i, j)), pl.BlockSpec((ts, D), row),
                 pl.BlockSpec((nj, 3, 2 * tc), lambda i, j: (0, 0, 0))]
    out_shape = [jax.ShapeDtypeStruct((S, F2), BF16), jax.ShapeDtypeStruct((S, D), F32),
                 jax.ShapeDtypeStruct((nj, 3, 2 * tc), F32)]
    args = [up_pre, up, up, dh, dh, fcw, w_up_t3, dz]
    scratch = [pltpu.VMEM((ts, D), F32)]
    if has_rs:
        in_specs.append(pl.BlockSpec(memory_space=pl.ANY))
        args.append(scatter)
        out_specs.append(pl.BlockSpec(memory_space=pl.ANY))
        out_shape.append(jax.ShapeDtypeStruct((3,) + scatter.shape[1:], scatter.dtype))
        scratch += _RS_SEMS
    return pl.pallas_call(
        body, name=name, grid=(nt, nj), in_specs=in_specs, out_specs=out_specs, out_shape=out_shape,
        scratch_shapes=scratch, compiler_params=_cp("arbitrary", "arbitrary"),
    )(*args)


def _ln_bwd(dy, xhat, rstd, gamma, *, name, ts=512):
    S, D = dy.shape
    ts = min(ts, S)

    def body(dy_ref, xh_ref, rs_ref, g_ref, dz_ref, dg_ref, db_ref):
        @pl.when(pl.program_id(0) == 0)
        def _():
            dg_ref[...] = jnp.zeros_like(dg_ref)
            db_ref[...] = jnp.zeros_like(db_ref)

        dyv = dy_ref[...]
        xh = xh_ref[...].astype(F32)
        dyg = dyv * g_ref[...]
        c1 = jnp.mean(dyg, axis=-1, keepdims=True)
        c2 = jnp.mean(dyg * xh, axis=-1, keepdims=True)
        dz_ref[...] = rs_ref[...] * (dyg - c1 - xh * c2)
        dg_ref[...] += jnp.sum(dyv * xh, axis=0, keepdims=True)
        db_ref[...] += jnp.sum(dyv, axis=0, keepdims=True)

    cur = lambda i: (i, 0)
    const = lambda i: (0, 0)
    return pl.pallas_call(
        body, name=name, grid=(S // ts,),
        in_specs=[pl.BlockSpec((ts, D), cur), pl.BlockSpec((ts, D), cur), pl.BlockSpec((ts, 1), cur),
                  pl.BlockSpec((1, D), const)],
        out_specs=[pl.BlockSpec((ts, D), cur), pl.BlockSpec((1, D), const), pl.BlockSpec((1, D), const)],
        out_shape=(jax.ShapeDtypeStruct((S, D), F32), jax.ShapeDtypeStruct((1, D), F32),
                   jax.ShapeDtypeStruct((1, D), F32)),
        compiler_params=_cp("arbitrary"),
    )(dy, xhat, rstd, gamma)


def _loss_head(y, tgt, *, name, ts=512):
    S, D = y.shape
    ts = min(ts, S)

    def body(y_ref, t_ref, dy_ref, sq_ref):
        @pl.when(pl.program_id(0) == 0)
        def _():
            sq_ref[...] = jnp.zeros_like(sq_ref)

        e = y_ref[...] - t_ref[...]
        dy_ref[...] = e * (1.0 / D)
        sq_ref[...] += jnp.sum(e * e, axis=0, keepdims=True)

    cur = lambda i: (i, 0)
    return pl.pallas_call(
        body, name=name, grid=(S // ts,),
        in_specs=[pl.BlockSpec((ts, D), cur), pl.BlockSpec((ts, D), cur)],
        out_specs=[pl.BlockSpec((ts, D), cur), pl.BlockSpec((1, D), lambda i: (0, 0))],
        out_shape=(jax.ShapeDtypeStruct((S, D), F32), jax.ShapeDtypeStruct((1, D), F32)),
        compiler_params=_cp("arbitrary"),
    )(y, tgt)


def _adamw(w, g, m, v, *, name, tr=512):
    R, C = w.shape
    tr = _div_tile(R, tr)
    c1 = 1.0 - ADAM_B1 ** ADAM_STEP
    c2 = 1.0 - ADAM_B2 ** ADAM_STEP

    def body(w_ref, g_ref, m_ref, v_ref, d_ref, mo_ref, vo_ref):
        gv = g_ref[...]
        m2 = ADAM_B1 * m_ref[...] + (1.0 - ADAM_B1) * gv
        v2 = ADAM_B2 * v_ref[...] + (1.0 - ADAM_B2) * (gv * gv)
        m_hat = m2 / c1
        v_hat = v2 / c2
        d_ref[...] = -ADAM_LR * (m_hat / (jnp.sqrt(v_hat) + ADAM_EPS) + ADAM_WD * w_ref[...])
        mo_ref[...] = m2
        vo_ref[...] = v2

    spec = pl.BlockSpec((tr, C), lambda i: (i, 0))
    return pl.pallas_call(
        body, name=name, grid=(R // tr,),
        in_specs=[spec] * 4, out_specs=[spec] * 3,
        out_shape=(jax.ShapeDtypeStruct((R, C), F32),) * 3,
        compiler_params=_cp("parallel"),
    )(w, g, m, v)


def _interleave_cols(w, nj):
    lead, f2 = w.shape[:-1], w.shape[-1]
    tc = f2 // (2 * nj)
    w = w.reshape(lead + (2, nj, tc))
    return jnp.swapaxes(w, -3, -2).reshape(lead + (f2,))


def _deinterleave_cols(w, nj):
    lead, f2 = w.shape[:-1], w.shape[-1]
    tc = f2 // (2 * nj)
    w = w.reshape(lead + (nj, 2, tc))
    return jnp.swapaxes(w, -3, -2).reshape(lead + (f2,))


def _block_diag(w_pool):
    return jnp.concatenate([jnp.pad(w_pool[g], ((0, 0), (64 * g, 192 - 64 * g))) for g in range(4)], axis=0)


def _pad_rows(v, rows):
    return jnp.pad(v, (0, rows * LANES - v.shape[0])).reshape(rows, LANES)


def kernel(x, positions, w_in, w_pool, pool_scale, attn_sinks, conv_w, w_branch_a, w_branch_b, w_branch_c, w_o, ln1_g, ln1_b, w_up, ffn_conv_w, w_down, ln2_g, ln2_b, loss_target, m_w_in, m_w_pool, m_pool_scale, m_attn_sinks, m_conv_w, m_w_branch_a, m_w_branch_b, m_w_branch_c, m_w_o, m_ln1_g, m_ln1_b, m_w_up, m_ffn_conv_w, m_w_down, m_ln2_g, m_ln2_b, v_w_in, v_w_pool, v_pool_scale, v_attn_sinks, v_conv_w, v_w_branch_a, v_w_branch_b, v_w_branch_c, v_w_o, v_ln1_g, v_ln1_b, v_w_up, v_ffn_conv_w, v_w_down, v_ln2_g, v_ln2_b):
    L, D, in_shard = w_in.shape
    S = x.shape[1]
    IN = in_shard * N_DEV
    F2 = w_up.shape[2] * N_DEV
    F = F2 // 2
    assert D == 1024 and IN == 1792 + 3 * D and x.shape[0] == 1 and S % 512 == 0
    alpha = (2 * L) ** 0.25
    NJ = 2
    TC = F // NJ
    xs = x.reshape(S, D)
    tgt = loss_target.reshape(S, D)

    big = [w_in, w_branch_a, w_branch_b, w_branch_c, w_o, w_up, w_down]
    PART_A, PART_B = (0, 1, 2, 3, 4), (5, 6)
    rows_l = [a.size // L // LANES for a in big]
    offs_l = [sum(rows_l[:k]) for k in range(len(big) + 1)]

    def pack_part(l, part):
        return [big[k][l].astype(BF16) for k in part]

    n_cw, n_fw = conv_w.size, ffn_conv_w.size
    small_rows = -(-(n_cw + n_fw) // LANES)
    small = _pad_rows(jnp.concatenate([conv_w.reshape(-1), ffn_conv_w.reshape(-1)]), small_rows)
    gsmall = _all_gather(small, "ag_conv_weights").reshape(N_DEV, -1)
    conv_full = gsmall[:, :n_cw].reshape(N_DEV, L, 3, -1).transpose(1, 2, 0, 3).reshape(L, 3, 256)
    fcw_full = gsmall[:, n_cw:n_cw + n_fw].reshape(N_DEV, L, 3, -1).transpose(1, 2, 0, 3).reshape(L, 3, F2)
    fcw_full = _interleave_cols(fcw_full, NJ)

    def shard_of(g, part, k, shape):
        assert g[part.index(k)].shape == (N_DEV,) + shape
        return g[part.index(k)]

    def unpack_a(g):
        win = shard_of(g, PART_A, 0, (D, in_shard)).transpose(1, 0, 2).reshape(D, IN)
        wg = win[:, 1792:]
        wp = jnp.concatenate([win[:, 0:256], win[:, 1024:1792]], axis=1)
        wq = win[:, 256:1024]
        wa = shard_of(g, PART_A, 1, (256, D // N_DEV)).transpose(1, 0, 2).reshape(256, D)
        wb = shard_of(g, PART_A, 2, (512, D // N_DEV)).transpose(1, 0, 2).reshape(512, D)
        wc = shard_of(g, PART_A, 3, (256, D // N_DEV)).transpose(1, 0, 2).reshape(256, D)
        wbr = jnp.concatenate([wa, wb, wc], axis=0)
        wo = shard_of(g, PART_A, 4, (D // N_DEV, D)).reshape(D, D)
        return dict(wg=wg, wp=wp, wq=wq, wg_t=wg.T, wp_t=wp.T, wq_t=wq.T, wbr=wbr, wbr_t=wbr.T, wo=wo, wo_t=wo.T)

    def unpack_b(g):
        nh = N_DEV // (2 * NJ)
        wup = shard_of(g, PART_B, 5, (D, F2 // N_DEV)).reshape(2, NJ, nh, D, F2 // N_DEV)
        wup = wup.transpose(3, 1, 0, 2, 4).reshape(D, F2)
        wdn = shard_of(g, PART_B, 6, (F // N_DEV, D)).reshape(F, D)
        return dict(wup=wup, wup_t=wup.T, wdn=wdn, wdn_t=wdn.T)

    def local_weights(l):
        wbd = _block_diag(w_pool[l]).astype(BF16)
        return dict(wbd=wbd, wbd_t=wbd.T, scale=pool_scale[l].reshape(1, 256), conv=conv_full[l],
                    fcw=fcw_full[l], sinks=jnp.broadcast_to(attn_sinks[l].reshape(8, 1), (8, 128)),
                    g1=ln1_g[l].reshape(1, D), b1=ln1_b[l].reshape(1, D),
                    g2=ln2_g[l].reshape(1, D), b2=ln2_b[l].reshape(1, D))

    inv_freq = ROPE_THETA ** (-jnp.arange(0, ROT_DIM, 2, dtype=F32) / ROT_DIM)
    head_lane = jnp.concatenate([inv_freq, inv_freq, jnp.zeros((HEAD_DIM - ROT_DIM,), F32)])
    head_sign = jnp.concatenate([-jnp.ones((8,), F32), jnp.ones((8,), F32), jnp.zeros((HEAD_DIM - ROT_DIM,), F32)])
    inv_lane = jnp.tile(head_lane, 2).reshape(1, 128)
    sign_lane = jnp.tile(head_sign, 2).reshape(1, 128)
    cos_t, sin_t = _rope_tables(positions.reshape(S, 1), inv_lane, sign_lane, "rope_tables")

    saved, W = [], []
    h_in = xs
    gathered_a = _all_gather(pack_part(0, PART_A), "ag_weights_first")
    for l in range(L):
        w = {**unpack_a(gathered_a), **local_weights(l)}
        if l == 0:
            pg, gathered_b = _mm(h_in, w["wg"], out_dtype=BF16, name="proj_gate", gather=pack_part(l, PART_B))
        else:
            pg = _mm(h_in, w["wg"], out_dtype=BF16, name="proj_gate")
        w.update(unpack_b(gathered_b))
        W.append(w)
        pp = _mm(h_in, w["wp"], out_dtype=BF16, name="proj_poolconv")
        pq = _mm(h_in, w["wq"], out_dtype=BF16, name="proj_qkv")
        o_a, o_c = _poolconv_fwd(pp, w["wbd"], w["scale"], w["conv"], name="poolconv_fwd")
        o_b, lse = _attn_fwd(pq, cos_t, sin_t, w["sinks"], name="attn_fwd")
        merged = _merge_fwd(o_a, o_b, o_c, pg, w["wbr"], name="merge_fwd")
        x1, xh1, rs1 = _mm_ln(merged, w["wo"], h_in, w["g1"], w["b1"], alpha=alpha, name="wo_ln1")
        if l + 1 < L:
            up_pre, gathered_a = _mm(x1, w["wup"], out_dtype=BF16, name="ffn_up", tn=2 * TC,
                                     gather=pack_part(l + 1, PART_A))
        else:
            up_pre = _mm(x1, w["wup"], out_dtype=BF16, name="ffn_up", tn=2 * TC)
        down = dict(alpha=alpha, name="ffn_down", tc=TC)
        wdn3 = w["wdn"].reshape(NJ, TC, D)
        if l + 1 < L:
            hact, x2, xh2, rs2, up, gathered_b = _ffn_down_fwd(up_pre, w["fcw"], wdn3, x1, w["g2"], w["b2"],
                                                               gather=pack_part(l + 1, PART_B), **down)
        else:
            hact, x2, xh2, rs2, up = _ffn_down_fwd(up_pre, w["fcw"], wdn3, x1, w["g2"], w["b2"], **down)
        saved.append(dict(up=up,x0=h_in, pg=pg, pp=pp, pq=pq, o_a=o_a, o_b=o_b, o_c=o_c, lse=lse, merged=merged,
                          x1=x1, xh1=xh1, rs1=rs1, up_pre=up_pre, hact=hact, xh2=xh2, rs2=rs2))
        h_in = x2

    dy, sq_lanes = _loss_head(h_in, tgt, name="loss_head")

    def pack_grads(g):
        col = lambda a, n: a.reshape(a.shape[0], N_DEV, n).transpose(1, 0, 2)
        row = lambda a, n: a.reshape(N_DEV, n, a.shape[1])
        nh = N_DEV // (2 * NJ)
        up = g["w_up"].reshape(D, NJ, 2, nh, F2 // N_DEV).transpose(2, 1, 3, 0, 4)
        parts = [col(g["w_in"], in_shard), col(g["a"], D // N_DEV), col(g["b"], D // N_DEV), col(g["c"], D // N_DEV),
                 row(g["w_o"], D // N_DEV), up, row(g["w_down"], F // N_DEV)]
        return jnp.concatenate([p.reshape(N_DEV, -1, LANES).astype(BF16) for p in parts], axis=1)

    my_c = lax.axis_index("c").astype(jnp.int32).reshape(1)
    my_chip = (2 * lax.axis_index("x") + lax.axis_index("y")).astype(jnp.int32).reshape(1)
    gw = [None] * L
    pair_sum = [None] * L
    from_chips = [None] * L
    for l in reversed(range(L)):
        w, sv = W[l], saved[l]
        dz2, dg2, db2 = _ln_bwd(dy, sv["xh2"], sv["rs2"], w["g2"], name="ln2_bwd")
        dh = _mm(dz2, w["wdn_t"], out_dtype=BF16, name="down_bwd_x")
        dw_dn = _mm_tn(sv["hact"], dz2, name="down_bwd_w", tka=TC, na=NJ, tn=D, ts=1024)
        up_bwd = dict(alpha=alpha, name="ffn_up_bwd", tc=TC)
        wup_t3 = w["wup_t"].reshape(NJ, 2 * TC, D)
        if l + 1 < L:
            dpre, dx1, dfcw, from_chips[l + 1] = _ffn_up_bwd(sv["up_pre"], sv["up"], dh, w["fcw"], wup_t3, dz2,
                                                             scatter=pair_sum[l + 1], **up_bwd)
        else:
            dpre, dx1, dfcw = _ffn_up_bwd(sv["up_pre"], sv["up"], dh, w["fcw"], wup_t3, dz2, **up_bwd)
        dfcw = dfcw.transpose(1, 0, 2).reshape(3, F2)
        dw_up = _mm_tn(sv["x1"], dpre, name="up_bwd_w", tka=D, tn=TC, nb=2 * NJ, ts=1024)
        dz1, dg1, db1 = _ln_bwd(dx1, sv["xh1"], sv["rs1"], w["g1"], name="ln1_bwd")
        dmerged = _mm(dz1, w["wo_t"], out_dtype=BF16, name="wo_bwd_x")
        dw_o = _mm_tn(sv["merged"], dz1, name="wo_bwd_w", tka=D, tn=D // 2, nb=2)
        dpg, dprod, do_a, do_b, do_c = _merge_bwd(dmerged, sv["o_a"], sv["o_b"], sv["o_c"], sv["pg"],
                                                  w["wbr"], w["wbr_t"], name="merge_bwd")
        dw_a = _mm_tn(sv["o_a"], dprod, name="branch_a_bwd_w", tka=256, tn=D, b_off=0)
        dw_b = _mm_tn(sv["o_b"], dprod, name="branch_b_bwd_w", tka=512, tn=D, b_off=1)
        dw_c = _mm_tn(sv["o_c"], dprod, name="branch_c_bwd_w", tka=256, tn=D, b_off=2)
        dpq, dsink = _attn_bwd(sv["pq"], cos_t, sin_t, w["sinks"], do_b, sv["o_b"], sv["lse"], name="attn_bwd")
        dpp, pooled, dmixed, dscale, dconv = _poolconv_bwd(sv["pp"], do_a, do_c, w["wbd"], w["wbd_t"], w["scale"],
                                                           w["conv"], name="poolconv_bwd")
        dwbd = _mm_tn(pooled, dmixed, name="pool_bwd_w", tka=256, tn=256)
        dx = _mm(dpg, w["wg_t"], out_dtype=F32, name="proj_gate_bwd_x", add=dz1, add_scale=alpha)
        dx = _mm(dpp, w["wp_t"], out_dtype=F32, name="proj_poolconv_bwd_x", add=dx)
        dx = _mm(dpq, w["wq_t"], out_dtype=F32, name="proj_qkv_bwd_x", add=dx)
        dw_g = _mm_tn(sv["x0"], dpg, name="proj_gate_bwd_w", tka=D, tn=512, nb=6)
        dw_p = _mm_tn(sv["x0"], dpp, name="proj_poolconv_bwd_w", tka=D, tn=512, nb=2)
        dw_q = _mm_tn(sv["x0"], dpq, name="proj_qkv_bwd_w", tka=D, tn=384, nb=2)
        dw_in = jnp.concatenate([dw_p[:, 0:256], dw_q, dw_p[:, 256:1024], dw_g], axis=1)
        dw_pool = jnp.stack([dwbd[64 * g:64 * (g + 1), 64 * g:64 * (g + 1)] for g in range(4)])
        gw[l] = dict(w_in=dw_in, a=dw_a, b=dw_b, c=dw_c, w_o=dw_o, w_up=dw_up, w_down=dw_dn,
                     w_pool=dw_pool, scale=dscale, sinks=dsink[:, 0], conv=dconv, fcw=_deinterleave_cols(dfcw, NJ),
                     g1=dg1, b1=db1, g2=dg2, b2=db2)
        p_l = pack_grads(gw[l])
        pair_sum[l] = _sum_sibling(p_l, _rs_sibling(p_l, "rs_sibling"), my_c, "rs_sum_sibling")
        dy = dx
    grad_x = dy.reshape(1, S, D)
    from_chips[0] = _rs_chips(pair_sum[0], "rs_chips_last")
    g_layers = [_sum_chips(pair_sum[l], from_chips[l], my_chip, "rs_sum_chips") for l in range(L)]

    def stack(k):
        return jnp.stack([gw[l][k] for l in range(L)])

    rep_vec = jnp.concatenate([
        stack("w_pool").reshape(-1), stack("scale").reshape(-1), stack("g1").reshape(-1), stack("b1").reshape(-1),
        stack("g2").reshape(-1), stack("b2").reshape(-1)])
    n_rep_full = -(-rep_vec.shape[0] // LANES)
    sinks_row = jnp.pad(stack("sinks").reshape(-1), (0, LANES - 8 * L))
    rep_vec = jnp.concatenate([_pad_rows(rep_vec, n_rep_full).reshape(-1), sinks_row, sq_lanes.reshape(-1)])
    loss_row = n_rep_full + 1
    n_rep = -(-(loss_row + 1) // 8) * 8
    rep_rows = _pad_rows(rep_vec, n_rep)
    dconv_by_dev = stack("conv").reshape(L, 3, N_DEV, -1).transpose(2, 0, 1, 3).reshape(N_DEV, -1)
    dfcw_by_dev = stack("fcw").reshape(L, 3, N_DEV, -1).transpose(2, 0, 1, 3).reshape(N_DEV, -1)
    n_mine = -(-(small_rows) // 8) * 8
    by_dev = jnp.concatenate([dconv_by_dev, dfcw_by_dev], axis=1)
    by_dev = jnp.pad(by_dev, ((0, 0), (0, n_mine * LANES - by_dev.shape[1]))).reshape(N_DEV * n_mine, LANES)
    small_g = _all_gather(jnp.concatenate([rep_rows, by_dev], axis=0), "ag_small_grads")
    rep_sum, mine_sum, loss11 = _small_reduce(small_g, n_rep, n_mine, 1.0 / D, loss_row, "small_reduce")
    loss = loss11[0, 0]

    names_big = ["w_in", "w_branch_a", "w_branch_b", "w_branch_c", "w_o", "w_up", "w_down"]
    ms_big = [m_w_in, m_w_branch_a, m_w_branch_b, m_w_branch_c, m_w_o, m_w_up, m_w_down]
    vs_big = [v_w_in, v_w_branch_a, v_w_branch_b, v_w_branch_c, v_w_o, v_w_up, v_w_down]
    out = {}
    for k, name in enumerate(names_big):
        wk = big[k]
        c2 = wk.shape[-1]
        as2d = lambda a: a.reshape(-1, c2)
        g_nat = jnp.concatenate([g[offs_l[k]:offs_l[k + 1]] for g in g_layers], axis=0).reshape(wk.shape)
        d, mo, vo = _adamw(as2d(wk), as2d(g_nat), as2d(ms_big[k]), as2d(vs_big[k]), name="adamw_" + name)
        out[name] = (g_nat, d.reshape(wk.shape), mo.reshape(wk.shape), vo.reshape(wk.shape))

    def rep_pack(wp_, sc_, g1_, b1_, g2_, b2_, sk_):
        v = jnp.concatenate([wp_.reshape(-1), sc_.reshape(-1), g1_.reshape(-1), b1_.reshape(-1), g2_.reshape(-1),
                             b2_.reshape(-1)])
        return _pad_rows(jnp.concatenate([_pad_rows(v, n_rep_full).reshape(-1), sk_.reshape(-1)]), n_rep)

    def mine_pack(cw_, fw_):
        return _pad_rows(jnp.concatenate([cw_.reshape(-1), fw_.reshape(-1)]), n_mine)

    w_rep = rep_pack(w_pool, pool_scale, ln1_g, ln1_b, ln2_g, ln2_b, attn_sinks)
    m_rep = rep_pack(m_w_pool, m_pool_scale, m_ln1_g, m_ln1_b, m_ln2_g, m_ln2_b, m_attn_sinks)
    v_rep = rep_pack(v_w_pool, v_pool_scale, v_ln1_g, v_ln1_b, v_ln2_g, v_ln2_b, v_attn_sinks)
    g_rep = jnp.concatenate([rep_sum[:loss_row], jnp.zeros((n_rep - loss_row, LANES), F32)], axis=0)
    rep_res = (g_rep,) + tuple(_adamw(w_rep, g_rep, m_rep, v_rep, name="adamw_replicated"))
    w_mine = mine_pack(conv_w, ffn_conv_w)
    mine_res = (mine_sum,) + tuple(_adamw(w_mine, mine_sum, mine_pack(m_conv_w, m_ffn_conv_w),
                                          mine_pack(v_conv_w, v_ffn_conv_w), name="adamw_conv"))

    def rep_unpack(buf):
        flat = buf.reshape(-1)
        res, o = {}, 0
        for nm, ref in (("w_pool", w_pool), ("pool_scale", pool_scale), ("ln1_g", ln1_g), ("ln1_b", ln1_b),
                        ("ln2_g", ln2_g), ("ln2_b", ln2_b)):
            res[nm] = flat[o:o + ref.size].reshape(ref.shape)
            o += ref.size
        o = n_rep_full * LANES
        res["attn_sinks"] = flat[o:o + attn_sinks.size].reshape(attn_sinks.shape)
        return res

    def mine_unpack(buf):
        flat = buf.reshape(-1)
        return {"conv_w": flat[:n_cw].reshape(conv_w.shape),
                "ffn_conv_w": flat[n_cw:n_cw + n_fw].reshape(ffn_conv_w.shape)}

    order = ["w_in", "w_pool", "pool_scale", "attn_sinks", "conv_w", "w_branch_a", "w_branch_b", "w_branch_c", "w_o",
             "ln1_g", "ln1_b", "w_up", "ffn_conv_w", "w_down", "ln2_g", "ln2_b"]
    results = [loss, grad_x]
    for kind in range(4):
        rep_k, mine_k = rep_unpack(rep_res[kind]), mine_unpack(mine_res[kind])
        for nm in order:
            if nm in out:
                results.append(out[nm][kind])
            elif nm in rep_k:
                results.append(rep_k[nm])
            else:
                results.append(mine_k[nm])
    return tuple(results)
```

```python
import functools

import jax
import jax.numpy as jnp
from jax import lax
from jax.experimental import pallas as pl
from jax.experimental.pallas import tpu as pltpu

F32 = jnp.float32
BF16 = jnp.bfloat16

HEAD_DIM = 64
N_Q_HEADS = 8
GROUP = 4
WINDOW = 128
ROT_DIM = 16
ROPE_THETA = 500000.0
POOL_WINDOWS = (2, 4, 8, 16)
LN_EPS = 1e-5
MASK_VALUE = -1e30
ADAM_LR, ADAM_B1, ADAM_B2, ADAM_EPS, ADAM_WD, ADAM_STEP = 0.001, 0.9, 0.999, 1e-08, 0.01, 10

N_DEV = 8
LANES = 1024
HALO = 16
MESH = pl.DeviceIdType.MESH
VMEM_LIMIT = 56 * 1024 * 1024


def _div_tile(n, want, mult=8):
    for t in range(min(want, n) // mult * mult, 0, -mult):
        if n % t == 0:
            return t
    return n


def _cp(*sem):
    return pltpu.CompilerParams(dimension_semantics=sem, vmem_limit_bytes=VMEM_LIMIT)


def _coords():
    return lax.axis_index("x"), lax.axis_index("y"), lax.axis_index("c")


def _all_gather(xs, name):
    xs = list(xs) if isinstance(xs, (list, tuple)) else [xs]
    n = len(xs)

    def body(*refs):
        ag_refs = (refs[:n], refs[n:2 * n]) + tuple(refs[2 * n:])
        _ag_start(*ag_refs)
        _ag_finish(*ag_refs)

    res = pl.pallas_call(
        body, name=name,
        out_shape=[jax.ShapeDtypeStruct((N_DEV,) + a.shape, a.dtype) for a in xs],
        in_specs=[pl.BlockSpec(memory_space=pl.ANY)] * n,
        out_specs=[pl.BlockSpec(memory_space=pl.ANY)] * n,
        scratch_shapes=_ag_sems(n),
    )(*xs)
    return res if n > 1 else res[0]


def _ag_sems(n):
    return [pltpu.SemaphoreType.DMA((7 * n,)), pltpu.SemaphoreType.DMA((7 * n,)), pltpu.SemaphoreType.DMA((n,))]


def _ag_copies(x_refs, out_refs, send_sems, recv_sems, local_sems):
    x, y, c = _coords()
    me, sibling = (x, y, c), (x, y, 1 - c)
    chips = [(1 - x, y), (x, 1 - y), (1 - x, 1 - y)]
    per_array = []
    for a, (x_ref, out_ref) in enumerate(zip(x_refs, out_refs)):
        def slot(px, py, pc, out_ref=out_ref):
            return out_ref.at[4 * px + 2 * py + pc]

        def copy(k, block, to, src=None, a=a, slot=slot):
            return pltpu.make_async_remote_copy(
                src_ref=slot(*block) if src is None else src, dst_ref=slot(*block),
                send_sem=send_sems.at[7 * a + k], recv_sem=recv_sems.at[7 * a + k],
                device_id=to, device_id_type=MESH)

        mine = pltpu.make_async_copy(x_ref, slot(*me), local_sems.at[a])
        first = [copy(0, me, sibling, src=x_ref)]
        first += [copy(1 + j, me, (*chip, c), src=x_ref) for j, chip in enumerate(chips)]
        passed = [copy(4 + j, (*chip, c), sibling) for j, chip in enumerate(chips)]
        from_chips = [copy(1 + j, (*chip, c), me) for j, chip in enumerate(chips)]
        from_sibling = [copy(0, sibling, me)] + [copy(4 + j, (*chip, 1 - c), me) for j, chip in enumerate(chips)]
        per_array.append((mine, first, passed, from_chips, from_sibling))
    return per_array


def _ag_start(*refs):
    for mine, first, _, _, _ in _ag_copies(*refs):
        mine.start()
        for cp in first:
            cp.start()


def _ag_finish(*refs):
    per_array = _ag_copies(*refs)
    for j in range(3):
        for _, _, passed, from_chips, _ in per_array:
            from_chips[j].wait_recv()
            passed[j].start()
    for mine, first, passed, _, from_sibling in per_array:
        for cp in from_sibling:
            cp.wait_recv()
        for cp in first + passed:
            cp.wait_send()
        mine.wait()


def _rs_sibling(ps, name):
    n = len(ps)

    def body(*refs):
        p_refs, out_refs, send_sems, recv_sems = refs[:n], refs[n:2 * n], refs[2 * n], refs[2 * n + 1]
        x, y, c = _coords()
        copies = []
        for a in range(n):
            for j in range(4):
                cx, cy = j // 2, j % 2
                copies.append(pltpu.make_async_remote_copy(
                    src_ref=p_refs[a].at[4 * cx + 2 * cy + (1 - c)], dst_ref=out_refs[a].at[j],
                    send_sem=send_sems.at[4 * a + j], recv_sem=recv_sems.at[4 * a + j],
                    device_id=(x, y, 1 - c), device_id_type=MESH))
        for cp in copies:
            cp.start()
        for cp in copies:
            cp.wait_recv()
        for cp in copies:
            cp.wait_send()

    return pl.pallas_call(
        body, name=name,
        out_shape=[jax.ShapeDtypeStruct((4,) + p.shape[1:], p.dtype) for p in ps],
        in_specs=[pl.BlockSpec(memory_space=pl.ANY)] * n,
        out_specs=[pl.BlockSpec(memory_space=pl.ANY)] * n,
        scratch_shapes=[pltpu.SemaphoreType.DMA((4 * n,)), pltpu.SemaphoreType.DMA((4 * n,))],
    )(*ps)


def _rs_chips(qs, name):
    n = len(qs)

    def body(*refs):
        rs_refs = (refs[:n], refs[n:2 * n], refs[2 * n], refs[2 * n + 1])
        _rs_chips_start(*rs_refs)
        _rs_chips_finish(*rs_refs)

    return pl.pallas_call(
        body, name=name,
        out_shape=[jax.ShapeDtypeStruct((3,) + q.shape[1:], q.dtype) for q in qs],
        in_specs=[pl.BlockSpec(memory_space=pl.ANY)] * n,
        out_specs=[pl.BlockSpec(memory_space=pl.ANY)] * n,
        scratch_shapes=_rs_sems(n),
    )(*qs)


def _rs_sems(n):
    return [pltpu.SemaphoreType.DMA((3 * n,)), pltpu.SemaphoreType.DMA((3 * n,))]


def _rs_chips_copies(q_refs, out_refs, send_sems, recv_sems):
    x, y, c = _coords()
    chips = [(1 - x, y), (x, 1 - y), (1 - x, 1 - y)]
    return [pltpu.make_async_remote_copy(
        src_ref=q_ref.at[2 * cx + cy], dst_ref=out_ref.at[k],
        send_sem=send_sems.at[3 * a + k], recv_sem=recv_sems.at[3 * a + k], device_id=(cx, cy, c),
        device_id_type=MESH)
        for a, (q_ref, out_ref) in enumerate(zip(q_refs, out_refs)) for k, (cx, cy) in enumerate(chips)]


def _rs_chips_start(*refs):
    for cp in _rs_chips_copies(*refs):
        cp.start()


def _rs_chips_finish(*refs):
    copies = _rs_chips_copies(*refs)
    for cp in copies:
        cp.wait_recv()
    for cp in copies:
        cp.wait_send()


def _sum_sibling(p, recv, my_c, name, tr=512):
    _, R, C = p.shape
    tr = _div_tile(R, tr, 16)

    def body(c_ref, p_ref, r_ref, o_ref):
        o_ref[...] = (p_ref[...].astype(F32) + r_ref[...].astype(F32)).astype(o_ref.dtype)

    grid_spec = pltpu.PrefetchScalarGridSpec(
        num_scalar_prefetch=1, grid=(4, R // tr),
        in_specs=[pl.BlockSpec((1, tr, C), lambda j, r, c_ref: (4 * (j // 2) + 2 * (j % 2) + c_ref[0], r, 0)),
                  pl.BlockSpec((1, tr, C), lambda j, r, c_ref: (j, r, 0))],
        out_specs=pl.BlockSpec((1, tr, C), lambda j, r, c_ref: (j, r, 0)))
    return pl.pallas_call(body, name=name, grid_spec=grid_spec,
                          out_shape=jax.ShapeDtypeStruct((4, R, C), p.dtype),
                          compiler_params=_cp("parallel", "parallel"))(my_c, p, recv)


def _sum_chips(q, recv, my_chip, name, tr=512):
    _, R, C = q.shape
    tr = _div_tile(R, tr, 16)

    def body(i_ref, q_ref, r_ref, o_ref):
        acc = q_ref[0].astype(F32)
        for k in range(3):
            acc = acc + r_ref[k].astype(F32)
        o_ref[...] = acc

    grid_spec = pltpu.PrefetchScalarGridSpec(
        num_scalar_prefetch=1, grid=(R // tr,),
        in_specs=[pl.BlockSpec((1, tr, C), lambda r, i_ref: (i_ref[0], r, 0)),
                  pl.BlockSpec((3, tr, C), lambda r, i_ref: (0, r, 0))],
        out_specs=pl.BlockSpec((tr, C), lambda r, i_ref: (r, 0)))
    return pl.pallas_call(body, name=name, grid_spec=grid_spec,
                          out_shape=jax.ShapeDtypeStruct((R, C), F32),
                          compiler_params=_cp("parallel"))(my_chip, q, recv)


def _small_reduce(g, n_rep, n_mine, inv_d, loss_row, name):
    _, R, C = g.shape

    def body(g_ref, rep_ref, mine_ref, loss_ref):
        x, y, c = _coords()
        start = pl.multiple_of(n_rep + (4 * x + 2 * y + c) * n_mine, 8)
        rep = g_ref[0, 0:n_rep, :]
        mine = g_ref[0, pl.ds(start, n_mine), :]
        sq = g_ref[0, loss_row:loss_row + 1, :]
        for d in range(1, N_DEV):
            rep = rep + g_ref[d, 0:n_rep, :]
            mine = mine + g_ref[d, pl.ds(start, n_mine), :]
            sq = sq + g_ref[d, loss_row:loss_row + 1, :]
        rep_ref[...] = rep
        mine_ref[...] = mine
        loss_ref[...] = (0.5 * inv_d) * jnp.sum(sq, axis=1, keepdims=True)

    return pl.pallas_call(
        body, name=name,
        out_shape=(jax.ShapeDtypeStruct((n_rep, C), F32), jax.ShapeDtypeStruct((n_mine, C), F32),
                   jax.ShapeDtypeStruct((1, 1), F32)),
        compiler_params=pltpu.CompilerParams(vmem_limit_bytes=VMEM_LIMIT),
    )(g)


def _mm(a, b, *, out_dtype, name, tm=512, tn=None, tk=None, add=None, add_scale=1.0, gather=None):
    M, K = a.shape
    N = b.shape[1]
    tm = min(tm, M)
    tn = N if tn is None else tn
    tk = K if tk is None else tk
    nk = K // tk
    has_add = add is not None
    has_ag = gather is not None
    n_g = len(gather) if has_ag else 0
    n_i, n_j = M // tm, N // tn

    def body(*refs):
        a_ref, b_ref = refs[0], refs[1]
        add_ref = refs[2] if has_add else None
        n_in = 2 + has_add + n_g
        o_ref = refs[n_in]
        if has_ag:
            ag_refs = (refs[n_in - n_g:n_in], refs[n_in + 1:n_in + 1 + n_g]) + tuple(
                refs[n_in + 1 + n_g:n_in + 4 + n_g])
            pid = (pl.program_id(0), pl.program_id(1), pl.program_id(2))

            @pl.when((pid[0] == 0) & (pid[1] == 0) & (pid[2] == 0))
            def _():
                _ag_start(*ag_refs)

        part = jnp.dot(a_ref[...].astype(BF16), b_ref[...].astype(BF16), preferred_element_type=F32)

        def finish(r):
            if has_add:
                r = r + add_scale * add_ref[...].astype(F32)
            o_ref[...] = r.astype(out_dtype)

        if nk == 1:
            finish(part)
        else:
            acc_ref = refs[-1]
            k = pl.program_id(2)

            @pl.when(k == 0)
            def _():
                acc_ref[...] = part

            @pl.when(k > 0)
            def _():
                acc_ref[...] += part

            @pl.when(k == nk - 1)
            def _():
                finish(acc_ref[...])

        if has_ag:
            @pl.when((pid[0] == n_i - 1) & (pid[1] == n_j - 1) & (pid[2] == nk - 1))
            def _():
                _ag_finish(*ag_refs)

    in_specs = [pl.BlockSpec((tm, tk), lambda i, j, k: (i, k)), pl.BlockSpec((tk, tn), lambda i, j, k: (k, j))]
    args = [a, b]
    if has_add:
        in_specs.append(pl.BlockSpec((tm, tn), lambda i, j, k: (i, j)))
        args.append(add)
    out_specs = [pl.BlockSpec((tm, tn), lambda i, j, k: (i, j))]
    out_shape = [jax.ShapeDtypeStruct((M, N), out_dtype)]
    scratch = []
    if has_ag:
        in_specs += [pl.BlockSpec(memory_space=pl.ANY)] * n_g
        args += list(gather)
        out_specs += [pl.BlockSpec(memory_space=pl.ANY)] * n_g
        out_shape += [jax.ShapeDtypeStruct((N_DEV,) + g.shape, g.dtype) for g in gather]
        scratch += _ag_sems(n_g)
    if nk > 1:
        scratch.append(pltpu.VMEM((tm, tn), F32))
    sem = ("arbitrary",) * 3 if has_ag else ("parallel", "parallel", "arbitrary")
    res = pl.pallas_call(
        body, name=name, grid=(n_i, n_j, nk), in_specs=in_specs, out_specs=out_specs, out_shape=out_shape,
        scratch_shapes=scratch, compiler_params=_cp(*sem),
    )(*args)
    return (res[0], list(res[1:])) if has_ag else res[0]


def _mm_fan(a, bs, *, out_dtype, name, tm=512):
    M, K = a.shape
    tm = min(tm, M)
    n = len(bs)

    def body(*refs):
        a_v = refs[0][...].astype(BF16)
        for k in range(n):
            refs[1 + n + k][...] = jnp.dot(a_v, refs[1 + k][...].astype(BF16),
                                           preferred_element_type=F32).astype(out_dtype)

    row = lambda i: (i, 0)
    return pl.pallas_call(
        body, name=name, grid=(M // tm,),
        in_specs=[pl.BlockSpec((tm, K), row)] + [pl.BlockSpec(b.shape, lambda i: (0, 0)) for b in bs],
        out_specs=[pl.BlockSpec((tm, b.shape[1]), row) for b in bs],
        out_shape=[jax.ShapeDtypeStruct((M, b.shape[1]), out_dtype) for b in bs],
        compiler_params=_cp("parallel"),
    )(a, *bs)


def _mm_sum(xs, bs, add, *, add_scale, name, tm=512, ln=None):
    M = xs[0].shape[0]
    N = bs[0].shape[1]
    tm = min(tm, M)
    n = len(xs)

    def body(*refs):
        acc = add_scale * refs[2 * n][...]
        for k in range(n):
            acc = acc + jnp.dot(refs[k][...].astype(BF16), refs[n + k][...].astype(BF16), preferred_element_type=F32)
        if ln is None:
            refs[2 * n + 1][...] = acc
        else:
            xh_ref, rs_ref, g_ref, dz_ref, dg_ref, db_ref = refs[2 * n + 1:]
            _ln_bwd_tile(acc, xh_ref, rs_ref, g_ref, dz_ref, dg_ref, db_ref, pl.program_id(0) == 0)

    row = lambda i: (i, 0)
    vec = lambda i: (0, 0)
    in_specs = ([pl.BlockSpec((tm, x.shape[1]), row) for x in xs]
                + [pl.BlockSpec(b.shape, vec) for b in bs] + [pl.BlockSpec((tm, N), row)])
    if ln is None:
        return pl.pallas_call(
            body, name=name, grid=(M // tm,), in_specs=in_specs, out_specs=pl.BlockSpec((tm, N), row),
            out_shape=jax.ShapeDtypeStruct((M, N), F32), compiler_params=_cp("parallel"),
        )(*xs, *bs, add)
    in_specs += [pl.BlockSpec((tm, N), row), pl.BlockSpec((tm, 1), row), pl.BlockSpec((1, N), vec)]
    return pl.pallas_call(
        body, name=name, grid=(M // tm,), in_specs=in_specs,
        out_specs=[pl.BlockSpec((tm, N), row), pl.BlockSpec((1, N), vec), pl.BlockSpec((1, N), vec)],
        out_shape=(jax.ShapeDtypeStruct((M, N), F32), jax.ShapeDtypeStruct((1, N), F32),
                   jax.ShapeDtypeStruct((1, N), F32)),
        compiler_params=_cp("arbitrary"),
    )(*xs, *bs, add, *ln)


def _ln_bwd_tile(dyv, xh_ref, rs_ref, g_ref, dz_ref, dg_ref, db_ref, first):
    @pl.when(first)
    def _():
        dg_ref[...] = jnp.zeros_like(dg_ref)
        db_ref[...] = jnp.zeros_like(db_ref)

    xh = xh_ref[...].astype(F32)
    dyg = dyv * g_ref[...]
    c1 = jnp.mean(dyg, axis=-1, keepdims=True)
    c2 = jnp.mean(dyg * xh, axis=-1, keepdims=True)
    dz_ref[...] = rs_ref[...] * (dyg - c1 - xh * c2)
    dg_ref[...] += jnp.sum(dyv * xh, axis=0, keepdims=True)
    db_ref[...] += jnp.sum(dyv, axis=0, keepdims=True)


def _mm_ln(a, b, resid, gamma, beta, *, alpha, name, tm=512, tk=None):
    M, K = a.shape
    D = b.shape[1]
    tm = min(tm, M)
    tk = K if tk is None else tk
    nk = K // tk

    def body(a_ref, b_ref, r_ref, g_ref, be_ref, y_ref, xh_ref, rs_ref, *scratch):
        part = jnp.dot(a_ref[...].astype(BF16), b_ref[...].astype(BF16), preferred_element_type=F32)

        def finish(acc):
            z = alpha * r_ref[...] + acc
            mu = jnp.mean(z, axis=-1, keepdims=True)
            zc = z - mu
            var = jnp.mean(zc * zc, axis=-1, keepdims=True)
            rstd = lax.rsqrt(var + LN_EPS)
            xhat = zc * rstd
            y_ref[...] = xhat * g_ref[...] + be_ref[...]
            xh_ref[...] = xhat.astype(BF16)
            rs_ref[...] = rstd

        if nk == 1:
            finish(part)
        else:
            acc_ref = scratch[0]
            k = pl.program_id(1)

            @pl.when(k == 0)
            def _():
                acc_ref[...] = part

            @pl.when(k > 0)
            def _():
                acc_ref[...] += part

            @pl.when(k == nk - 1)
            def _():
                finish(acc_ref[...])

    row = lambda i, k: (i, 0)
    vec = lambda i, k: (0, 0)
    return pl.pallas_call(
        body, name=name, grid=(M // tm, nk),
        in_specs=[pl.BlockSpec((tm, tk), lambda i, k: (i, k)), pl.BlockSpec((tk, D), lambda i, k: (k, 0)),
                  pl.BlockSpec((tm, D), row), pl.BlockSpec((1, D), vec), pl.BlockSpec((1, D), vec)],
        out_specs=[pl.BlockSpec((tm, D), row), pl.BlockSpec((tm, D), row), pl.BlockSpec((tm, 1), row)],
        out_shape=(jax.ShapeDtypeStruct((M, D), F32), jax.ShapeDtypeStruct((M, D), BF16),
                   jax.ShapeDtypeStruct((M, 1), F32)),
        scratch_shapes=[pltpu.VMEM((tm, D), F32)] if nk > 1 else [],
        compiler_params=_cp("parallel", "arbitrary"),
    )(a, b, resid, gamma, beta)


def _mm_tn(a, b, *, name, tka, tn, a_off=0, na=1, b_off=0, nb=1, ts=2048):
    S = a.shape[0]
    ts = min(ts, S)

    def body(a_ref, b_ref, o_ref):
        s = pl.program_id(2)
        part = lax.dot_general(a_ref[...].astype(BF16), b_ref[...].astype(BF16),
                               (((0,), (0,)), ((), ())), preferred_element_type=F32)

        @pl.when(s == 0)
        def _():
            o_ref[...] = part

        @pl.when(s > 0)
        def _():
            o_ref[...] += part

    return pl.pallas_call(
        body, name=name, grid=(na, nb, S // ts),
        in_specs=[pl.BlockSpec((ts, tka), lambda i, j, s: (s, a_off + i)),
                  pl.BlockSpec((ts, tn), lambda i, j, s: (s, b_off + j))],
        out_specs=pl.BlockSpec((tka, tn), lambda i, j, s: (i, j)),
        out_shape=jax.ShapeDtypeStruct((na * tka, nb * tn), F32),
        compiler_params=_cp("parallel", "parallel", "arbitrary"),
    )(a, b)


def _rope_tables(pos, inv_lane, sign_lane, name, ts=512):
    S = pos.shape[0]
    ts = min(ts, S)

    def body(p_ref, inv_ref, sg_ref, cos_ref, sin_ref):
        ang = p_ref[...].astype(F32) * inv_ref[...]
        cos_ref[...] = jnp.cos(ang)
        sin_ref[...] = jnp.sin(ang) * sg_ref[...]

    return pl.pallas_call(
        body, name=name, grid=(S // ts,),
        in_specs=[pl.BlockSpec((ts, 1), lambda i: (i, 0)), pl.BlockSpec((1, 128), lambda i: (0, 0)),
                  pl.BlockSpec((1, 128), lambda i: (0, 0))],
        out_specs=[pl.BlockSpec((ts, 128), lambda i: (i, 0))] * 2,
        out_shape=(jax.ShapeDtypeStruct((S, 128), F32),) * 2,
        compiler_params=_cp("parallel"),
    )(pos, inv_lane, sign_lane)


def _rope_swap(t):
    lane = lax.broadcasted_iota(jnp.int32, (1, 128), 1)
    lo = (lane % HEAD_DIM) < (ROT_DIM // 2)
    return jnp.where(lo, pltpu.roll(t, 128 - ROT_DIM // 2, 1), pltpu.roll(t, ROT_DIM // 2, 1))


def _rope_fwd(t, cos, sin):
    return t * cos + _rope_swap(t) * sin


def _rope_bwd(d, cos, sin):
    lane = lax.broadcasted_iota(jnp.int32, (1, 128), 1)
    return d * cos + jnp.where((lane % HEAD_DIM) < ROT_DIM, _rope_swap(d * sin), 0.0)


def _tile_heads(t):
    lane = lax.broadcasted_iota(jnp.int32, (1, 128), 1)
    r = pltpu.roll(t, 64, 1)
    h0 = jnp.where(lane < 64, t, r)
    h1 = jnp.where(lane < 64, r, t)
    return jnp.concatenate([h0, h0], axis=1), jnp.concatenate([h1, h1], axis=1)


def _fold_heads(d0, d1):
    lane = lax.broadcasted_iota(jnp.int32, (1, 128), 1)

    def fold(d):
        s = d[:, 0:128] + d[:, 128:256]
        return s + pltpu.roll(s, 64, 1)

    return jnp.where(lane < 64, fold(d0), fold(d1))


def _band4(n_keys):
    row = lax.broadcasted_iota(jnp.int32, (GROUP * WINDOW, n_keys), 0) % WINDOW
    col = lax.broadcasted_iota(jnp.int32, (GROUP * WINDOW, n_keys), 1)
    return (col > row) & (col <= row + WINDOW), col


def _head_masks():
    lane = lax.broadcasted_iota(jnp.int32, (1, GROUP * HEAD_DIM), 1)
    return [(lane // HEAD_DIM) == hl for hl in range(GROUP)]


def _stack_heads(t):
    zero = jnp.zeros_like(t)
    return jnp.concatenate([jnp.where(hm, t, zero) for hm in _head_masks()], axis=0)


def _unstack_heads(t4):
    out = None
    for hl, hm in enumerate(_head_masks()):
        part = jnp.where(hm, t4[hl * WINDOW:(hl + 1) * WINDOW], 0.0)
        out = part if out is None else out + part
    return out


def _sink_block(sink_ref, g):
    return jnp.concatenate([jnp.broadcast_to(sink_ref[g * GROUP + hl:g * GROUP + hl + 1, 0:1], (WINDOW, 256))
                            for hl in range(GROUP)], axis=0)


def _sink_column(sink_ref, g):
    return jnp.concatenate([jnp.broadcast_to(sink_ref[g * GROUP + hl:g * GROUP + hl + 1, 0:1], (WINDOW, 1))
                            for hl in range(GROUP)], axis=0)


def _attn_fwd(pq, cos_t, sin_t, sinks_b, *, name, ts=256):
    S = pq.shape[0]
    ts = min(ts, S)
    nq = ts // WINDOW
    scale = HEAD_DIM ** -0.5

    def body(cur_ref, prev_ref, cosc_ref, sinc_ref, cosp_ref, sinp_ref, sink_ref, o_ref, lse_ref):
        i = pl.program_id(0)
        cosc, sinc = cosc_ref[...], sinc_ref[...]
        q = cur_ref[:, 0:512].astype(F32)
        qr = jnp.concatenate(
            [_rope_fwd(q[:, j * 128:(j + 1) * 128], cosc, sinc) for j in range(4)], axis=1) * scale
        qr = qr.astype(BF16)
        kc = _rope_fwd(cur_ref[:, 512:640].astype(F32), cosc, sinc)
        kp = _rope_fwd(prev_ref[:, 0:128].astype(F32), cosp_ref[...], sinp_ref[...])
        k_all = jnp.concatenate([kp, kc], axis=0)
        v_all = jnp.concatenate([prev_ref[:, 128:256].astype(F32), cur_ref[:, 640:768].astype(F32)], axis=0)
        kt = [t.astype(BF16) for t in _tile_heads(k_all)]
        vt = [t.astype(BF16) for t in _tile_heads(v_all)]
        band, col = _band4(2 * WINDOW)
        ones = jnp.ones((2 * WINDOW, 256), BF16)
        key_t = lax.broadcasted_iota(jnp.int32, (2 * WINDOW, GROUP * WINDOW), 0)
        qry_t = lax.broadcasted_iota(jnp.int32, (2 * WINDOW, GROUP * WINDOW), 1) % WINDOW
        band_t = (key_t > qry_t) & (key_t <= qry_t + WINDOW)
        NT = (((1,), (1,)), ((), ()))
        for qb in range(nq):
            rows = slice(qb * WINDOW, (qb + 1) * WINDOW)
            keys = slice(qb * WINDOW, (qb + 2) * WINDOW)
            valid = band & ((col >= WINDOW) | (i * nq + qb > 0))
            valid_t = band_t & ((key_t >= WINDOW) | (i * nq + qb > 0))
            for g in range(2):
                qs = _stack_heads(qr[rows, g * 256:(g + 1) * 256])
                sink = _sink_block(sink_ref, g)
                s = lax.dot_general(qs, kt[g][keys], NT, preferred_element_type=F32)
                s_t = lax.dot_general(kt[g][keys], qs, NT, preferred_element_type=F32)
                m_t = jnp.max(jnp.where(valid_t, s_t, MASK_VALUE), axis=0, keepdims=True)
                m_rep = jnp.broadcast_to(m_t, (WINDOW, GROUP * WINDOW)).T
                m = jnp.maximum(jnp.concatenate([m_rep, m_rep], axis=1), sink)
                e = jnp.exp(jnp.where(valid, s, MASK_VALUE) - m).astype(BF16)
                l = jnp.dot(e, ones, preferred_element_type=F32) + jnp.exp(sink - m)
                pv = jnp.dot(e, vt[g][keys], preferred_element_type=F32)
                o_ref[rows, g * 256:(g + 1) * 256] = (_unstack_heads(pv) / _unstack_heads(l)).astype(BF16)
                lse4 = (m + jnp.log(l))[:, 0:1]
                for hl in range(GROUP):
                    h = g * GROUP + hl
                    lse_ref[rows, h:h + 1] = lse4[hl * WINDOW:(hl + 1) * WINDOW]

    hb = ts // WINDOW
    cur = lambda i: (i, 0)
    prev = lambda i: (jnp.maximum(i * hb - 1, 0), 0)
    return pl.pallas_call(
        body, name=name, grid=(S // ts,),
        in_specs=[pl.BlockSpec((ts, 768), cur),
                  pl.BlockSpec((WINDOW, 256), lambda i: (jnp.maximum(i * hb - 1, 0), 2)),
                  pl.BlockSpec((ts, 128), cur), pl.BlockSpec((ts, 128), cur),
                  pl.BlockSpec((WINDOW, 128), prev), pl.BlockSpec((WINDOW, 128), prev),
                  pl.BlockSpec((8, 128), lambda i: (0, 0))],
        out_specs=[pl.BlockSpec((ts, 512), cur), pl.BlockSpec((ts, 8), cur)],
        out_shape=(jax.ShapeDtypeStruct((S, 512), BF16), jax.ShapeDtypeStruct((S, 8), F32)),
        compiler_params=_cp("parallel"),
    )(pq, pq, cos_t, sin_t, cos_t, sin_t, sinks_b)


def _attn_bwd(pq, cos_t, sin_t, sinks_b, do, o, lse, *, name, ts=256):
    S = pq.shape[0]
    ts = min(ts, S)
    nq = ts // WINDOW
    nt = S // ts
    scale = HEAD_DIM ** -0.5
    NT = (((1,), (1,)), ((), ()))
    TN = (((0,), (0,)), ((), ()))

    def body(cur_ref, prev_ref, nxt_ref, cosc_ref, sinc_ref, cosp_ref, sinp_ref, cosn_ref, sinn_ref, sink_ref,
             doc_ref, don_ref, oc_ref, on_ref, lsec_ref, lsen_ref, dpq_ref, dsink_ref):
        i = pl.program_id(0)
        last = i == nt - 1
        cosc, sinc = cosc_ref[...], sinc_ref[...]
        cose = jnp.concatenate([cosc, cosn_ref[...]], axis=0)
        sine = jnp.concatenate([sinc, sinn_ref[...]], axis=0)
        q = jnp.concatenate([cur_ref[:, 0:512], nxt_ref[:, 0:512]], axis=0).astype(F32)
        qr = jnp.concatenate(
            [_rope_fwd(q[:, j * 128:(j + 1) * 128], cose, sine) for j in range(4)], axis=1) * scale
        qr = qr.astype(BF16)
        kc = _rope_fwd(cur_ref[:, 512:640].astype(F32), cosc, sinc)
        kp = _rope_fwd(prev_ref[:, 0:128].astype(F32), cosp_ref[...], sinp_ref[...])
        k_all = jnp.concatenate([kp, kc], axis=0)
        v_all = jnp.concatenate([prev_ref[:, 128:256].astype(F32), cur_ref[:, 640:768].astype(F32)], axis=0)
        kt = [t.astype(BF16) for t in _tile_heads(k_all)]
        vt = [t.astype(BF16) for t in _tile_heads(v_all)]
        don = jnp.where(last, jnp.zeros_like(don_ref[...]), don_ref[...])
        do_e = jnp.concatenate([doc_ref[...], don], axis=0)
        o_e = jnp.concatenate([oc_ref[...], on_ref[...]], axis=0)
        band2, col2 = _band4(2 * WINDOW)
        band1, _ = _band4(WINDOW)
        ones = jnp.ones((256, 256), BF16)

        @pl.when(i == 0)
        def _():
            dsink_ref[...] = jnp.zeros_like(dsink_ref)

        dk_acc = [[None] * (nq + 1) for _ in range(2)]
        dv_acc = [[None] * (nq + 1) for _ in range(2)]

        def add(acc, g, e, val):
            acc[g][e] = val if acc[g][e] is None else acc[g][e] + val

        for qb in range(nq + 1):
            halo = qb == nq
            rows = slice(qb * WINDOW, (qb + 1) * WINDOW)
            if halo:
                keys = slice(qb * WINDOW, (qb + 1) * WINDOW)
                valid = band1 & jnp.logical_not(last)
            else:
                keys = slice(qb * WINDOW, (qb + 2) * WINDOW)
                valid = band2 & ((col2 >= WINDOW) | (i * nq + qb > 0))
            dq_parts = []
            for g in range(2):
                qs = _stack_heads(qr[rows, g * 256:(g + 1) * 256])
                dos = _stack_heads(do_e[rows, g * 256:(g + 1) * 256])
                o_g = o_e[rows, g * 256:(g + 1) * 256].astype(F32)
                kt_b, vt_b = kt[g][keys], vt[g][keys]
                lse_src = lsen_ref if halo else lsec_ref
                lse_rows = slice(0, WINDOW) if halo else rows
                big_l = jnp.concatenate([lse_src[lse_rows, g * GROUP + hl:g * GROUP + hl + 1] for hl in range(GROUP)],
                                        axis=0)
                delta = jnp.dot((dos.astype(F32) * jnp.concatenate([o_g] * GROUP, axis=0)).astype(BF16), ones,
                                preferred_element_type=F32)
                s = lax.dot_general(qs, kt_b, NT, preferred_element_type=F32)
                p = jnp.exp(jnp.where(valid, s, MASK_VALUE) - big_l)
                dp = lax.dot_general(dos, vt_b, NT, preferred_element_type=F32)
                ds = (p * (dp - delta[:, 0:p.shape[1]])).astype(BF16)
                dk_g = lax.dot_general(ds, qs, TN, preferred_element_type=F32)
                dv_g = lax.dot_general(p.astype(BF16), dos, TN, preferred_element_type=F32)
                if not halo:
                    dq_parts.append(_unstack_heads(jnp.dot(ds, kt_b, preferred_element_type=F32)))
                    dsink4 = jnp.exp(_sink_column(sink_ref, g) - big_l) * delta[:, 0:1]
                    for hl in range(GROUP):
                        h = g * GROUP + hl
                        dsink_h = -jnp.sum(dsink4[hl * WINDOW:(hl + 1) * WINDOW], axis=0, keepdims=True)
                        dsink_ref[h:h + 1, :] += jnp.broadcast_to(dsink_h, (1, 128))
                add(dk_acc, g, qb, dk_g[0:WINDOW])
                add(dv_acc, g, qb, dv_g[0:WINDOW])
                if not halo:
                    add(dk_acc, g, qb + 1, dk_g[WINDOW:2 * WINDOW])
                    add(dv_acc, g, qb + 1, dv_g[WINDOW:2 * WINDOW])
            if not halo:
                cs, sn = cosc[rows], sinc[rows]
                for g in range(2):
                    dq_g = dq_parts[g] * scale
                    for j in range(2):
                        c0 = g * 256 + j * 128
                        dpq_ref[rows, c0:c0 + 128] = _rope_bwd(dq_g[:, j * 128:(j + 1) * 128], cs, sn).astype(BF16)
        for e in range(1, nq + 1):
            rows = slice((e - 1) * WINDOW, e * WINDOW)
            dk = _fold_heads(dk_acc[0][e], dk_acc[1][e])
            dv = _fold_heads(dv_acc[0][e], dv_acc[1][e])
            dpq_ref[rows, 512:640] = _rope_bwd(dk, cosc[rows], sinc[rows]).astype(BF16)
            dpq_ref[rows, 640:768] = dv.astype(BF16)

    hb = ts // WINDOW
    nblk = S // WINDOW
    cur = lambda i: (i, 0)
    prev = lambda i: (jnp.maximum(i * hb - 1, 0), 0)
    nxt = lambda i: (jnp.minimum((i + 1) * hb, nblk - 1), 0)
    return pl.pallas_call(
        body, name=name, grid=(nt,),
        in_specs=[pl.BlockSpec((ts, 768), cur),
                  pl.BlockSpec((WINDOW, 256), lambda i: (jnp.maximum(i * hb - 1, 0), 2)),
                  pl.BlockSpec((WINDOW, 768), nxt),
                  pl.BlockSpec((ts, 128), cur), pl.BlockSpec((ts, 128), cur),
                  pl.BlockSpec((WINDOW, 128), prev), pl.BlockSpec((WINDOW, 128), prev),
                  pl.BlockSpec((WINDOW, 128), nxt), pl.BlockSpec((WINDOW, 128), nxt),
                  pl.BlockSpec((8, 128), lambda i: (0, 0)),
                  pl.BlockSpec((ts, 512), cur), pl.BlockSpec((WINDOW, 512), nxt),
                  pl.BlockSpec((ts, 512), cur), pl.BlockSpec((WINDOW, 512), nxt),
                  pl.BlockSpec((ts, 8), cur), pl.BlockSpec((WINDOW, 8), nxt)],
        out_specs=[pl.BlockSpec((ts, 768), cur), pl.BlockSpec((8, 128), lambda i: (0, 0))],
        out_shape=(jax.ShapeDtypeStruct((S, 768), BF16), jax.ShapeDtypeStruct((8, 128), F32)),
        compiler_params=_cp("arbitrary"),
    )(pq, pq, pq, cos_t, sin_t, cos_t, sin_t, cos_t, sin_t, sinks_b, do, do, o, o, lse, lse)


def _shift_dn(x, k):
    return pltpu.roll(x, k, 0)


def _shift_up(x, k):
    return pltpu.roll(x, x.shape[0] - k, 0)


def _pool_lane_select(vals):
    lane = lax.broadcasted_iota(jnp.int32, (1, 256), 1)
    out = vals[3]
    for g in (2, 1, 0):
        out = jnp.where(lane < 64 * (g + 1), vals[g], out)
    return out


def _pool_inv_count(t0, n):
    t = t0 + lax.broadcasted_iota(jnp.int32, (n, 256), 0)
    lane = lax.broadcasted_iota(jnp.int32, (n, 256), 1)
    w = jnp.where(lane < 64, 2, jnp.where(lane < 128, 4, jnp.where(lane < 192, 8, 16)))
    return 1.0 / jnp.minimum(t + 1, w).astype(F32)


def _pooled(u_ext, t0, n):
    s2 = u_ext + _shift_dn(u_ext, 1)
    s4 = s2 + _shift_dn(s2, 2)
    s8 = s4 + _shift_dn(s4, 4)
    s16 = s8 + _shift_dn(s8, 8)
    win = _pool_lane_select([s2, s4, s8, s16])[HALO:HALO + n]
    return win * _pool_inv_count(t0, n) - u_ext[HALO:HALO + n]


def _poolconv_fwd(pp, wbd, pool_scale, conv_w, *, name, ts=512):
    S = pp.shape[0]
    ts = min(ts, S)

    def body(cur_ref, prev_ref, wbd_ref, sc_ref, cw_ref, oa_ref, oc_ref):
        i = pl.program_id(0)
        prev = jnp.where(i > 0, prev_ref[...].astype(F32), 0.0)
        u_ext = jnp.concatenate([prev[:, 0:256], cur_ref[:, 0:256].astype(F32)], axis=0)
        pooled = _pooled(u_ext, i * ts, ts)
        mixed = jnp.dot(pooled.astype(BF16), wbd_ref[...], preferred_element_type=F32)
        oa_ref[...] = (mixed * sc_ref[...]).astype(BF16)
        v_ext = jnp.concatenate([prev[:, 256:512] * prev[:, 768:1024],
                                 cur_ref[:, 256:512].astype(F32) * cur_ref[:, 768:1024].astype(F32)], axis=0)
        cv = cw_ref[2:3, :] * v_ext + cw_ref[1:2, :] * _shift_dn(v_ext, 1) + cw_ref[0:1, :] * _shift_dn(v_ext, 2)
        oc_ref[...] = (cur_ref[:, 512:768].astype(F32) * cv[HALO:HALO + ts]).astype(BF16)

    hb = ts // HALO
    cur = lambda i: (i, 0)
    const = lambda i: (0, 0)
    return pl.pallas_call(
        body, name=name, grid=(S // ts,),
        in_specs=[pl.BlockSpec((ts, 1024), cur),
                  pl.BlockSpec((HALO, 1024), lambda i: (jnp.maximum(i * hb - 1, 0), 0)),
                  pl.BlockSpec((256, 256), const), pl.BlockSpec((1, 256), const), pl.BlockSpec((3, 256), const)],
        out_specs=[pl.BlockSpec((ts, 256), cur)] * 2,
        out_shape=(jax.ShapeDtypeStruct((S, 256), BF16),) * 2,
        compiler_params=_cp("parallel"),
    )(pp, pp, wbd, pool_scale, conv_w)


def _poolconv_bwd(pp, do_a, do_c, wbd, wbd_t, pool_scale, conv_w, *, name, ts=512):
    S = pp.shape[0]
    ts = min(ts, S)
    nt = S // ts
    n_e = ts + 2 * HALO

    def body(cur_ref, prev_ref, nxt_ref, dac_ref, dan_ref, dcc_ref, dcn_ref, wbd_ref, wbdt_ref, sc_ref, cw_ref,
             dpp_ref, pooled_ref, dmixed_ref, dsc_ref, dcw_ref):
        i = pl.program_id(0)

        @pl.when(i == 0)
        def _():
            dsc_ref[...] = jnp.zeros_like(dsc_ref)
            dcw_ref[...] = jnp.zeros_like(dcw_ref)

        prev = jnp.where(i > 0, prev_ref[...].astype(F32), 0.0)
        nxt = nxt_ref[...].astype(F32)
        cur = cur_ref[...].astype(F32)
        not_last = i < nt - 1
        da_n = jnp.where(not_last, dan_ref[...].astype(F32), 0.0)
        dc_n = jnp.where(not_last, dcn_ref[...].astype(F32), 0.0)
        zeros_h = jnp.zeros((HALO, 256), F32)
        sc = sc_ref[...]

        u_ext = jnp.concatenate([prev[:, 0:256], cur[:, 0:256]], axis=0)
        pooled = _pooled(u_ext, i * ts, ts)
        pooled_b = pooled.astype(BF16)
        pooled_ref[...] = pooled_b
        mixed = jnp.dot(pooled_b, wbd_ref[...], preferred_element_type=F32)
        da_c = dac_ref[...].astype(F32)
        dsc_ref[...] += jnp.sum(da_c * mixed, axis=0, keepdims=True)
        dmixed_e = jnp.concatenate([da_c, da_n], axis=0) * sc
        dmixed_ref[...] = dmixed_e[0:ts].astype(BF16)
        dpooled = jnp.dot(dmixed_e.astype(BF16), wbdt_ref[...], preferred_element_type=F32)
        qd = dpooled * _pool_inv_count(i * ts, ts + HALO)
        f2 = qd + _shift_up(qd, 1)
        f4 = f2 + _shift_up(f2, 2)
        f8 = f4 + _shift_up(f4, 4)
        f16 = f8 + _shift_up(f8, 8)
        du = (_pool_lane_select([f2, f4, f8, f16]) - dpooled)[0:ts]
        dpp_ref[:, 0:256] = du.astype(BF16)

        xc_e = jnp.concatenate([prev[:, 256:512], cur[:, 256:512], nxt[:, 256:512]], axis=0)
        gc_e = jnp.concatenate([prev[:, 768:1024], cur[:, 768:1024], nxt[:, 768:1024]], axis=0)
        gb_e = jnp.concatenate([zeros_h, cur[:, 512:768], nxt[:, 512:768]], axis=0)
        dc_e = jnp.concatenate([zeros_h, dcc_ref[...].astype(F32), dc_n], axis=0)
        v_e = xc_e * gc_e
        v1, v2 = _shift_dn(v_e, 1), _shift_dn(v_e, 2)
        w0, w1, w2 = cw_ref[0:1, :], cw_ref[1:2, :], cw_ref[2:3, :]
        cv = w2 * v_e + w1 * v1 + w0 * v2
        dcv = dc_e * gb_e
        dv = w2 * dcv + w1 * _shift_up(dcv, 1) + w0 * _shift_up(dcv, 2)
        tile = slice(HALO, HALO + ts)
        dpp_ref[:, 256:512] = (dv * gc_e)[tile].astype(BF16)
        dpp_ref[:, 512:768] = (dc_e * cv)[tile].astype(BF16)
        dpp_ref[:, 768:1024] = (dv * xc_e)[tile].astype(BF16)
        dcv_t = dcv[tile]
        dcw_ref[0:1, :] += jnp.sum(dcv_t * v2[tile], axis=0, keepdims=True)
        dcw_ref[1:2, :] += jnp.sum(dcv_t * v1[tile], axis=0, keepdims=True)
        dcw_ref[2:3, :] += jnp.sum(dcv_t * v_e[tile], axis=0, keepdims=True)

    hb = ts // HALO
    nblk = S // HALO
    cur = lambda i: (i, 0)
    const = lambda i: (0, 0)
    prev = lambda i: (jnp.maximum(i * hb - 1, 0), 0)
    nxt = lambda i: (jnp.minimum((i + 1) * hb, nblk - 1), 0)
    del n_e
    return pl.pallas_call(
        body, name=name, grid=(nt,),
        in_specs=[pl.BlockSpec((ts, 1024), cur), pl.BlockSpec((HALO, 1024), prev), pl.BlockSpec((HALO, 1024), nxt),
                  pl.BlockSpec((ts, 256), cur), pl.BlockSpec((HALO, 256), nxt),
                  pl.BlockSpec((ts, 256), cur), pl.BlockSpec((HALO, 256), nxt),
                  pl.BlockSpec((256, 256), const), pl.BlockSpec((256, 256), const),
                  pl.BlockSpec((1, 256), const), pl.BlockSpec((3, 256), const)],
        out_specs=[pl.BlockSpec((ts, 1024), cur), pl.BlockSpec((ts, 256), cur), pl.BlockSpec((ts, 256), cur),
                   pl.BlockSpec((1, 256), const), pl.BlockSpec((3, 256), const)],
        out_shape=(jax.ShapeDtypeStruct((S, 1024), BF16), jax.ShapeDtypeStruct((S, 256), BF16),
                   jax.ShapeDtypeStruct((S, 256), BF16), jax.ShapeDtypeStruct((1, 256), F32),
                   jax.ShapeDtypeStruct((3, 256), F32)),
        compiler_params=_cp("arbitrary"),
    )(pp, pp, pp, do_a, do_a, do_c, do_c, wbd, wbd_t, pool_scale, conv_w)


def _sigmoid(x):
    return 0.5 * jnp.tanh(0.5 * x) + 0.5


def _merge_fwd(o_a, o_b, o_c, glog, w_br, *, name, ts=512):
    S = o_a.shape[0]
    D = w_br.shape[1]
    ts = min(ts, S)

    def body(oa_ref, ob_ref, oc_ref, gl_ref, w_ref, m_ref):
        pa = jnp.dot(oa_ref[...], w_ref[0:256, :], preferred_element_type=F32)
        pb = jnp.dot(ob_ref[...], w_ref[256:768, :], preferred_element_type=F32)
        pc = jnp.dot(oc_ref[...], w_ref[768:1024, :], preferred_element_type=F32)
        m = _sigmoid(gl_ref[:, 0:D].astype(F32)) * pa
        m = m + _sigmoid(gl_ref[:, D:2 * D].astype(F32)) * pb
        m = m + _sigmoid(gl_ref[:, 2 * D:3 * D].astype(F32)) * pc
        m_ref[...] = m.astype(BF16)

    cur = lambda i: (i, 0)
    return pl.pallas_call(
        body, name=name, grid=(S // ts,),
        in_specs=[pl.BlockSpec((ts, 256), cur), pl.BlockSpec((ts, 512), cur), pl.BlockSpec((ts, 256), cur),
                  pl.BlockSpec((ts, 3 * D), cur), pl.BlockSpec((1024, D), lambda i: (0, 0))],
        out_specs=pl.BlockSpec((ts, D), cur),
        out_shape=jax.ShapeDtypeStruct((S, D), BF16),
        compiler_params=_cp("parallel"),
    )(o_a, o_b, o_c, glog, w_br)


def _merge_bwd(dm, o_a, o_b, o_c, glog, w_br, w_br_t, *, name, ts=256):
    S = o_a.shape[0]
    D = w_br.shape[1]
    ts = min(ts, S)

    def body(dm_ref, oa_ref, ob_ref, oc_ref, gl_ref, w_ref, wt_ref, dgl_ref, dp_ref, doa_ref, dob_ref, doc_ref):
        dmv = dm_ref[...].astype(F32)
        branches = ((oa_ref, 0, 256, doa_ref), (ob_ref, 256, 768, dob_ref), (oc_ref, 768, 1024, doc_ref))
        for b, (o_ref, r0, r1, do_ref) in enumerate(branches):
            prod = jnp.dot(o_ref[...], w_ref[r0:r1, :], preferred_element_type=F32)
            gate = _sigmoid(gl_ref[:, b * D:(b + 1) * D].astype(F32))
            dgl_ref[:, b * D:(b + 1) * D] = (dmv * prod * gate * (1.0 - gate)).astype(BF16)
            dprod = (dmv * gate).astype(BF16)
            dp_ref[:, b * D:(b + 1) * D] = dprod
            do_ref[...] = jnp.dot(dprod, wt_ref[:, r0:r1], preferred_element_type=F32).astype(BF16)

    cur = lambda i: (i, 0)
    const = lambda i: (0, 0)
    return pl.pallas_call(
        body, name=name, grid=(S // ts,),
        in_specs=[pl.BlockSpec((ts, D), cur), pl.BlockSpec((ts, 256), cur), pl.BlockSpec((ts, 512), cur),
                  pl.BlockSpec((ts, 256), cur), pl.BlockSpec((ts, 3 * D), cur),
                  pl.BlockSpec((1024, D), const), pl.BlockSpec((D, 1024), const)],
        out_specs=[pl.BlockSpec((ts, 3 * D), cur), pl.BlockSpec((ts, 3 * D), cur), pl.BlockSpec((ts, 256), cur),
                   pl.BlockSpec((ts, 512), cur), pl.BlockSpec((ts, 256), cur)],
        out_shape=(jax.ShapeDtypeStruct((S, 3 * D), BF16), jax.ShapeDtypeStruct((S, 3 * D), BF16),
                   jax.ShapeDtypeStruct((S, 256), BF16), jax.ShapeDtypeStruct((S, 512), BF16),
                   jax.ShapeDtypeStruct((S, 256), BF16)),
        compiler_params=_cp("parallel"),
    )(dm, o_a, o_b, o_c, glog, w_br, w_br_t)


FFN_CHUNK = 128
FFN_DOT_CHUNKS = 4


def _conv3(x, w_ref, cols):
    x1, x2 = _shift_dn(x, 1), _shift_dn(x, 2)
    return w_ref[2:3, cols] * x + w_ref[1:2, cols] * x1 + w_ref[0:1, cols] * x2, x1, x2


def _ffn_down_fwd(up_pre, fcw, w_down3, resid, gamma, beta, *, alpha, name, tc, ts=256, gather=None):
    S, F2 = up_pre.shape
    D = resid.shape[1]
    ts = min(ts, S)
    nt = S // ts
    nj = F2 // (2 * tc)
    has_ag = gather is not None
    n_g = len(gather) if has_ag else 0

    def body(cur_ref, prev_ref, w_ref, wd_ref, r_ref, g_ref, be_ref, *rest):
        h_ref, y_ref, xh_ref, rs_ref, up_ref = rest[n_g:n_g + 5]
        acc_ref = rest[2 * n_g + 5]
        if has_ag:
            ag_refs = (rest[:n_g], rest[n_g + 5:2 * n_g + 5]) + tuple(rest[2 * n_g + 6:2 * n_g + 9])
        i, j = pl.program_id(0), pl.program_id(1)
        if has_ag:
            @pl.when((i == 0) & (j == 0))
            def _():
                _ag_start(*ag_refs)

        part = None
        for c in range(tc // FFN_CHUNK):
            halves = []
            for half in range(2):
                cols = slice(half * tc + c * FFN_CHUNK, half * tc + (c + 1) * FFN_CHUNK)
                prev = jnp.where(i > 0, prev_ref[:, cols].astype(F32), 0.0)
                x = jnp.concatenate([prev, cur_ref[:, cols].astype(F32)], axis=0)
                halves.append(_conv3(x, w_ref, cols)[0][HALO:HALO + ts])
                up_ref[:, cols] = halves[-1].astype(BF16)
            a, b = halves
            h_ref[:, c * FFN_CHUNK:(c + 1) * FFN_CHUNK] = (a * _sigmoid(a) * b).astype(BF16)
            if (c + 1) % FFN_DOT_CHUNKS == 0 or c + 1 == tc // FFN_CHUNK:
                k0 = (c // FFN_DOT_CHUNKS) * FFN_DOT_CHUNKS * FFN_CHUNK
                piece = jnp.dot(h_ref[:, k0:(c + 1) * FFN_CHUNK], wd_ref[j, k0:(c + 1) * FFN_CHUNK, :],
                                preferred_element_type=F32)
                part = piece if part is None else part + piece

        @pl.when(j == 0)
        def _():
            acc_ref[...] = part

        @pl.when(j > 0)
        def _():
            acc_ref[...] += part

        @pl.when(j == nj - 1)
        def _():
            z = alpha * r_ref[...] + acc_ref[...]
            mu = jnp.mean(z, axis=-1, keepdims=True)
            zc = z - mu
            var = jnp.mean(zc * zc, axis=-1, keepdims=True)
            rstd = lax.rsqrt(var + LN_EPS)
            xhat = zc * rstd
            y_ref[...] = xhat * g_ref[...] + be_ref[...]
            xh_ref[...] = xhat.astype(BF16)
            rs_ref[...] = rstd

        if has_ag:
            @pl.when((i == nt - 1) & (j == nj - 1))
            def _():
                _ag_finish(*ag_refs)

    hb = ts // HALO
    row = lambda i, j: (i, 0)
    vec = lambda i, j: (0, 0)
    in_specs = [pl.BlockSpec((ts, 2 * tc), lambda i, j: (i, j)),
                pl.BlockSpec((HALO, 2 * tc), lambda i, j: (jnp.maximum(i * hb - 1, 0), j)),
                pl.BlockSpec((3, 2 * tc), lambda i, j: (0, j)),
                pl.BlockSpec((nj, tc, D), lambda i, j: (0, 0, 0)),
                pl.BlockSpec((ts, D), row), pl.BlockSpec((1, D), vec), pl.BlockSpec((1, D), vec)]
    out_specs = [pl.BlockSpec((ts, tc), lambda i, j: (i, j)), pl.BlockSpec((ts, D), row), pl.BlockSpec((ts, D), row),
                 pl.BlockSpec((ts, 1), row), pl.BlockSpec((ts, 2 * tc), lambda i, j: (i, j))]
    out_shape = [jax.ShapeDtypeStruct((S, F2 // 2), BF16), jax.ShapeDtypeStruct((S, D), F32),
                 jax.ShapeDtypeStruct((S, D), BF16), jax.ShapeDtypeStruct((S, 1), F32),
                 jax.ShapeDtypeStruct((S, F2), BF16)]
    args = [up_pre, up_pre, fcw, w_down3, resid, gamma, beta]
    scratch = [pltpu.VMEM((ts, D), F32)]
    if has_ag:
        in_specs += [pl.BlockSpec(memory_space=pl.ANY)] * n_g
        args += list(gather)
        out_specs += [pl.BlockSpec(memory_space=pl.ANY)] * n_g
        out_shape += [jax.ShapeDtypeStruct((N_DEV,) + g.shape, g.dtype) for g in gather]
        scratch += _ag_sems(n_g)
    res = pl.pallas_call(
        body, name=name, grid=(nt, nj), in_specs=in_specs, out_specs=out_specs, out_shape=out_shape,
        scratch_shapes=scratch, compiler_params=_cp("arbitrary", "arbitrary"),
    )(*args)
    return tuple(res[:5]) + ((list(res[5:]),) if has_ag else ())


def _ffn_up_bwd(up_pre, up, dh, fcw, w_up_t3, dz, *, alpha, name, tc, ts=256, scatter=None):
    S, F2 = up_pre.shape
    D = dz.shape[1]
    ts = min(ts, S)
    nt = S // ts
    nj = F2 // (2 * tc)
    has_rs = scatter is not None
    n_s = len(scatter) if has_rs else 0
    tile = slice(0, ts)

    def body(x_ref, upc_ref, upn_ref, dhc_ref, dhn_ref, w_ref, wt_ref, dz_ref, *rest):
        dpre_ref, dx_ref, dw_ref = rest[n_s:n_s + 3]
        acc_ref = rest[2 * n_s + 3]
        if has_rs:
            rs_refs = (rest[:n_s], rest[n_s + 3:2 * n_s + 3], rest[2 * n_s + 4], rest[2 * n_s + 5])
        i, j = pl.program_id(0), pl.program_id(1)

        @pl.when((i == 0) & (j == 0))
        def _():
            dw_ref[...] = jnp.zeros_like(dw_ref)
            if has_rs:
                _rs_chips_start(*rs_refs)

        part = None
        for c in range(tc // FFN_CHUNK):
            lanes = slice(c * FFN_CHUNK, (c + 1) * FFN_CHUNK)
            dh_n = jnp.where(i < nt - 1, dhn_ref[:, lanes].astype(F32), 0.0)
            dh_e = jnp.concatenate([dhc_ref[:, lanes].astype(F32), dh_n], axis=0)
            cols_of = [slice(half * tc + c * FFN_CHUNK, half * tc + (c + 1) * FFN_CHUNK) for half in range(2)]
            a, b = [jnp.concatenate([upc_ref[:, cols].astype(F32), upn_ref[:, cols].astype(F32)], axis=0)
                    for cols in cols_of]
            sg = _sigmoid(a)
            dups = [dh_e * b * (sg * (1.0 + a * (1.0 - sg))), dh_e * (a * sg)]
            for half in range(2):
                cols, dup = cols_of[half], dups[half]
                dup1, dup2 = _shift_up(dup, 1), _shift_up(dup, 2)
                dpre = w_ref[2:3, cols] * dup + w_ref[1:2, cols] * dup1 + w_ref[0:1, cols] * dup2
                dpre_ref[:, cols] = dpre[tile].astype(BF16)
                x = x_ref[:, cols].astype(F32)
                dw_ref[j, 0:1, cols] += jnp.sum(dup2[tile] * x, axis=0, keepdims=True)
                dw_ref[j, 1:2, cols] += jnp.sum(dup1[tile] * x, axis=0, keepdims=True)
                dw_ref[j, 2:3, cols] += jnp.sum(dup[tile] * x, axis=0, keepdims=True)
            if (c + 1) % FFN_DOT_CHUNKS == 0 or c + 1 == tc // FFN_CHUNK:
                k0 = (c // FFN_DOT_CHUNKS) * FFN_DOT_CHUNKS * FFN_CHUNK
                for half in range(2):
                    ks = slice(half * tc + k0, half * tc + (c + 1) * FFN_CHUNK)
                    piece = jnp.dot(dpre_ref[:, ks], wt_ref[j, ks, :], preferred_element_type=F32)
                    part = piece if part is None else part + piece

        @pl.when(j == 0)
        def _():
            acc_ref[...] = part

        @pl.when(j > 0)
        def _():
            acc_ref[...] += part

        @pl.when(j == nj - 1)
        def _():
            dx_ref[...] = acc_ref[...] + alpha * dz_ref[...]

        if has_rs:
            @pl.when((i == nt - 1) & (j == nj - 1))
            def _():
                _rs_chips_finish(*rs_refs)

    hb = ts // HALO
    nblk = S // HALO
    nxt = lambda i, j: (jnp.minimum((i + 1) * hb, nblk - 1), j)
    row = lambda i, j: (i, 0)
    in_specs = [pl.BlockSpec((ts, 2 * tc), lambda i, j: (i, j)),
                pl.BlockSpec((ts, 2 * tc), lambda i, j: (i, j)), pl.BlockSpec((HALO, 2 * tc), nxt),
                pl.BlockSpec((ts, tc), lambda i, j: (i, j)), pl.BlockSpec((HALO, tc), nxt),
                pl.BlockSpec((3, 2 * tc), lambda i, j: (0, j)),
                pl.BlockSpec((nj, 2 * tc, D), lambda i, j: (0, 0, 0)),
                pl.BlockSpec((ts, D), row)]
    out_specs = [pl.BlockSpec((ts, 2 * tc), lambda i, j: (i, j)), pl.BlockSpec((ts, D), row),
                 pl.BlockSpec((nj, 3, 2 * tc), lambda i, j: (0, 0, 0))]
    out_shape = [jax.ShapeDtypeStruct((S, F2), BF16), jax.ShapeDtypeStruct((S, D), F32),
                 jax.ShapeDtypeStruct((nj, 3, 2 * tc), F32)]
    args = [up_pre, up, up, dh, dh, fcw, w_up_t3, dz]
    scratch = [pltpu.VMEM((ts, D), F32)]
    if has_rs:
        in_specs += [pl.BlockSpec(memory_space=pl.ANY)] * n_s
        args += list(scatter)
        out_specs += [pl.BlockSpec(memory_space=pl.ANY)] * n_s
        out_shape += [jax.ShapeDtypeStruct((3,) + q.shape[1:], q.dtype) for q in scatter]
        scratch += _rs_sems(n_s)
    res = pl.pallas_call(
        body, name=name, grid=(nt, nj), in_specs=in_specs, out_specs=out_specs, out_shape=out_shape,
        scratch_shapes=scratch, compiler_params=_cp("arbitrary", "arbitrary"),
    )(*args)
    return tuple(res[:3]) + ((list(res[3:]),) if has_rs else ())


def _ln_bwd(dy, xhat, rstd, gamma, *, name, ts=512):
    S, D = dy.shape
    ts = min(ts, S)

    def body(dy_ref, xh_ref, rs_ref, g_ref, dz_ref, dg_ref, db_ref):
        _ln_bwd_tile(dy_ref[...], xh_ref, rs_ref, g_ref, dz_ref, dg_ref, db_ref, pl.program_id(0) == 0)

    cur = lambda i: (i, 0)
    const = lambda i: (0, 0)
    return pl.pallas_call(
        body, name=name, grid=(S // ts,),
        in_specs=[pl.BlockSpec((ts, D), cur), pl.BlockSpec((ts, D), cur), pl.BlockSpec((ts, 1), cur),
                  pl.BlockSpec((1, D), const)],
        out_specs=[pl.BlockSpec((ts, D), cur), pl.BlockSpec((1, D), const), pl.BlockSpec((1, D), const)],
        out_shape=(jax.ShapeDtypeStruct((S, D), F32), jax.ShapeDtypeStruct((1, D), F32),
                   jax.ShapeDtypeStruct((1, D), F32)),
        compiler_params=_cp("arbitrary"),
    )(dy, xhat, rstd, gamma)


def _loss_head(y, tgt, *, name, ts=512):
    S, D = y.shape
    ts = min(ts, S)

    def body(y_ref, t_ref, dy_ref, sq_ref):
        @pl.when(pl.program_id(0) == 0)
        def _():
            sq_ref[...] = jnp.zeros_like(sq_ref)

        e = y_ref[...] - t_ref[...]
        dy_ref[...] = e * (1.0 / D)
        sq_ref[...] += jnp.sum(e * e, axis=0, keepdims=True)

    cur = lambda i: (i, 0)
    return pl.pallas_call(
        body, name=name, grid=(S // ts,),
        in_specs=[pl.BlockSpec((ts, D), cur), pl.BlockSpec((ts, D), cur)],
        out_specs=[pl.BlockSpec((ts, D), cur), pl.BlockSpec((1, D), lambda i: (0, 0))],
        out_shape=(jax.ShapeDtypeStruct((S, D), F32), jax.ShapeDtypeStruct((1, D), F32)),
        compiler_params=_cp("arbitrary"),
    )(y, tgt)


def _adamw(w, g, m, v, *, name, tr=512):
    R, C = w.shape
    tr = _div_tile(R, tr)
    c1 = 1.0 - ADAM_B1 ** ADAM_STEP
    c2 = 1.0 - ADAM_B2 ** ADAM_STEP

    def body(w_ref, g_ref, m_ref, v_ref, d_ref, mo_ref, vo_ref):
        gv = g_ref[...]
        m2 = ADAM_B1 * m_ref[...] + (1.0 - ADAM_B1) * gv
        v2 = ADAM_B2 * v_ref[...] + (1.0 - ADAM_B2) * (gv * gv)
        m_hat = m2 / c1
        v_hat = v2 / c2
        d_ref[...] = -ADAM_LR * (m_hat / (jnp.sqrt(v_hat) + ADAM_EPS) + ADAM_WD * w_ref[...])
        mo_ref[...] = m2
        vo_ref[...] = v2

    spec = pl.BlockSpec((tr, C), lambda i: (i, 0))
    return pl.pallas_call(
        body, name=name, grid=(R // tr,),
        in_specs=[spec] * 4, out_specs=[spec] * 3,
        out_shape=(jax.ShapeDtypeStruct((R, C), F32),) * 3,
        compiler_params=_cp("parallel"),
    )(w, g, m, v)


def _interleave_cols(w, nj):
    lead, f2 = w.shape[:-1], w.shape[-1]
    tc = f2 // (2 * nj)
    w = w.reshape(lead + (2, nj, tc))
    return jnp.swapaxes(w, -3, -2).reshape(lead + (f2,))


def _deinterleave_cols(w, nj):
    lead, f2 = w.shape[:-1], w.shape[-1]
    tc = f2 // (2 * nj)
    w = w.reshape(lead + (nj, 2, tc))
    return jnp.swapaxes(w, -3, -2).reshape(lead + (f2,))


def _block_diag(w_pool):
    return jnp.concatenate([jnp.pad(w_pool[g], ((0, 0), (64 * g, 192 - 64 * g))) for g in range(4)], axis=0)


def _pad_rows(v, rows):
    return jnp.pad(v, (0, rows * LANES - v.shape[0])).reshape(rows, LANES)


def kernel(x, positions, w_in, w_pool, pool_scale, attn_sinks, conv_w, w_branch_a, w_branch_b, w_branch_c, w_o, ln1_g, ln1_b, w_up, ffn_conv_w, w_down, ln2_g, ln2_b, loss_target, m_w_in, m_w_pool, m_pool_scale, m_attn_sinks, m_conv_w, m_w_branch_a, m_w_branch_b, m_w_branch_c, m_w_o, m_ln1_g, m_ln1_b, m_w_up, m_ffn_conv_w, m_w_down, m_ln2_g, m_ln2_b, v_w_in, v_w_pool, v_pool_scale, v_attn_sinks, v_conv_w, v_w_branch_a, v_w_branch_b, v_w_branch_c, v_w_o, v_ln1_g, v_ln1_b, v_w_up, v_ffn_conv_w, v_w_down, v_ln2_g, v_ln2_b):
    L, D, in_shard = w_in.shape
    S = x.shape[1]
    IN = in_shard * N_DEV
    F2 = w_up.shape[2] * N_DEV
    F = F2 // 2
    assert D == 1024 and IN == 1792 + 3 * D and x.shape[0] == 1 and S % 512 == 0
    alpha = (2 * L) ** 0.25
    NJ = 2
    TC = F // NJ
    xs = x.reshape(S, D)
    tgt = loss_target.reshape(S, D)

    big = [w_in, w_branch_a, w_branch_b, w_branch_c, w_o, w_up, w_down]
    PART_A, PART_B = (0, 1, 2, 3, 4), (5, 6)
    rows_l = [a.size // L // LANES for a in big]
    offs_l = [sum(rows_l[:k]) for k in range(len(big) + 1)]

    def pack_part(l, part):
        return [big[k][l].astype(BF16) for k in part]

    n_cw, n_fw = conv_w.size, ffn_conv_w.size
    small_rows = -(-(n_cw + n_fw) // LANES)
    small = _pad_rows(jnp.concatenate([conv_w.reshape(-1), ffn_conv_w.reshape(-1)]), small_rows)
    gsmall = _all_gather(small, "ag_conv_weights").reshape(N_DEV, -1)
    conv_full = gsmall[:, :n_cw].reshape(N_DEV, L, 3, -1).transpose(1, 2, 0, 3).reshape(L, 3, 256)
    fcw_full = gsmall[:, n_cw:n_cw + n_fw].reshape(N_DEV, L, 3, -1).transpose(1, 2, 0, 3).reshape(L, 3, F2)
    fcw_full = _interleave_cols(fcw_full, NJ)

    def shard_of(g, part, k, shape):
        assert g[part.index(k)].shape == (N_DEV,) + shape
        return g[part.index(k)]

    def unpack_a(g):
        win = shard_of(g, PART_A, 0, (D, in_shard)).transpose(1, 0, 2).reshape(D, IN)
        wg = win[:, 1792:]
        wp = jnp.concatenate([win[:, 0:256], win[:, 1024:1792]], axis=1)
        wq = win[:, 256:1024]
        wa = shard_of(g, PART_A, 1, (256, D // N_DEV)).transpose(1, 0, 2).reshape(256, D)
        wb = shard_of(g, PART_A, 2, (512, D // N_DEV)).transpose(1, 0, 2).reshape(512, D)
        wc = shard_of(g, PART_A, 3, (256, D // N_DEV)).transpose(1, 0, 2).reshape(256, D)
        wbr = jnp.concatenate([wa, wb, wc], axis=0)
        wo = shard_of(g, PART_A, 4, (D // N_DEV, D)).reshape(D, D)
        return dict(wg=wg, wp=wp, wq=wq, wg_t=wg.T, wp_t=wp.T, wq_t=wq.T, wbr=wbr, wbr_t=wbr.T, wo=wo, wo_t=wo.T)

    def unpack_b(g):
        nh = N_DEV // (2 * NJ)
        wup = shard_of(g, PART_B, 5, (D, F2 // N_DEV)).reshape(2, NJ, nh, D, F2 // N_DEV)
        wup = wup.transpose(3, 1, 0, 2, 4).reshape(D, F2)
        wdn = shard_of(g, PART_B, 6, (F // N_DEV, D)).reshape(F, D)
        return dict(wup=wup, wup_t=wup.T, wdn=wdn, wdn_t=wdn.T)

    def local_weights(l):
        wbd = _block_diag(w_pool[l]).astype(BF16)
        return dict(wbd=wbd, wbd_t=wbd.T, scale=pool_scale[l].reshape(1, 256), conv=conv_full[l],
                    fcw=fcw_full[l], sinks=jnp.broadcast_to(attn_sinks[l].reshape(8, 1), (8, 128)),
                    g1=ln1_g[l].reshape(1, D), b1=ln1_b[l].reshape(1, D),
                    g2=ln2_g[l].reshape(1, D), b2=ln2_b[l].reshape(1, D))

    inv_freq = ROPE_THETA ** (-jnp.arange(0, ROT_DIM, 2, dtype=F32) / ROT_DIM)
    head_lane = jnp.concatenate([inv_freq, inv_freq, jnp.zeros((HEAD_DIM - ROT_DIM,), F32)])
    head_sign = jnp.concatenate([-jnp.ones((8,), F32), jnp.ones((8,), F32), jnp.zeros((HEAD_DIM - ROT_DIM,), F32)])
    inv_lane = jnp.tile(head_lane, 2).reshape(1, 128)
    sign_lane = jnp.tile(head_sign, 2).reshape(1, 128)
    cos_t, sin_t = _rope_tables(positions.reshape(S, 1), inv_lane, sign_lane, "rope_tables")

    saved, W = [], []
    h_in = xs
    first = _all_gather(pack_part(0, PART_A) + pack_part(0, PART_B), "ag_weights_first")
    gathered_a, gathered_b = first[:len(PART_A)], first[len(PART_A):]
    for l in range(L):
        w = {**unpack_a(gathered_a), **unpack_b(gathered_b), **local_weights(l)}
        W.append(w)
        pg, pp, pq = _mm_fan(h_in, [w["wg"], w["wp"], w["wq"]], out_dtype=BF16, name="proj_in")
        o_a, o_c = _poolconv_fwd(pp, w["wbd"], w["scale"], w["conv"], name="poolconv_fwd")
        o_b, lse = _attn_fwd(pq, cos_t, sin_t, w["sinks"], name="attn_fwd")
        merged = _merge_fwd(o_a, o_b, o_c, pg, w["wbr"], name="merge_fwd")
        x1, xh1, rs1 = _mm_ln(merged, w["wo"], h_in, w["g1"], w["b1"], alpha=alpha, name="wo_ln1")
        if l + 1 < L:
            up_pre, gathered_a = _mm(x1, w["wup"], out_dtype=BF16, name="ffn_up", tn=2 * TC,
                                     gather=pack_part(l + 1, PART_A))
        else:
            up_pre = _mm(x1, w["wup"], out_dtype=BF16, name="ffn_up", tn=2 * TC)
        down = dict(alpha=alpha, name="ffn_down", tc=TC)
        wdn3 = w["wdn"].reshape(NJ, TC, D)
        if l + 1 < L:
            hact, x2, xh2, rs2, up, gathered_b = _ffn_down_fwd(up_pre, w["fcw"], wdn3, x1, w["g2"], w["b2"],
                                                               gather=pack_part(l + 1, PART_B), **down)
        else:
            hact, x2, xh2, rs2, up = _ffn_down_fwd(up_pre, w["fcw"], wdn3, x1, w["g2"], w["b2"], **down)
        saved.append(dict(up=up,x0=h_in, pg=pg, pp=pp, pq=pq, o_a=o_a, o_b=o_b, o_c=o_c, lse=lse, merged=merged,
                          x1=x1, xh1=xh1, rs1=rs1, up_pre=up_pre, hact=hact, xh2=xh2, rs2=rs2))
        h_in = x2

    dy, sq_lanes = _loss_head(h_in, tgt, name="loss_head")

    def pack_grads(g):
        col = lambda a, n: a.reshape(a.shape[0], N_DEV, n).transpose(1, 0, 2)
        row = lambda a, n: a.reshape(N_DEV, n, a.shape[1])
        nh = N_DEV // (2 * NJ)
        up = g["w_up"].reshape(D, NJ, 2, nh, F2 // N_DEV).transpose(2, 1, 3, 0, 4)
        rest = [col(g["a"], D // N_DEV), col(g["b"], D // N_DEV), col(g["c"], D // N_DEV),
                row(g["w_o"], D // N_DEV), row(g["w_down"], F // N_DEV)]
        return [col(g["w_in"], in_shard).astype(BF16), up.reshape(N_DEV, D, F2 // N_DEV).astype(BF16),
                jnp.concatenate([p.reshape(N_DEV, -1, LANES).astype(BF16) for p in rest], axis=1)]

    my_c = lax.axis_index("c").astype(jnp.int32).reshape(1)
    my_chip = (2 * lax.axis_index("x") + lax.axis_index("y")).astype(jnp.int32).reshape(1)
    gw = [None] * L
    pair_sum = [None] * L
    from_chips = [None] * L
    for l in reversed(range(L)):
        w, sv = W[l], saved[l]
        if l == L - 1:
            dz2, dg2, db2 = _ln_bwd(dy, sv["xh2"], sv["rs2"], w["g2"], name="ln2_bwd")
        else:
            dz2, dg2, db2 = ln2_out
        dw_dn = _mm_tn(sv["hact"], dz2, name="down_bwd_w", tka=TC, na=NJ, tn=D, ts=1024)
        up_bwd = dict(alpha=alpha, name="ffn_up_bwd", tc=TC)
        dh = _mm(dz2, w["wdn_t"], out_dtype=BF16, name="down_bwd_x")
        wup_t3 = w["wup_t"].reshape(NJ, 2 * TC, D)
        if l + 1 < L:
            dpre, dx1, dfcw, from_chips[l + 1] = _ffn_up_bwd(sv["up_pre"], sv["up"], dh, w["fcw"], wup_t3, dz2,
                                                             scatter=pair_sum[l + 1], **up_bwd)
        else:
            dpre, dx1, dfcw = _ffn_up_bwd(sv["up_pre"], sv["up"], dh, w["fcw"], wup_t3, dz2, **up_bwd)
        dfcw = dfcw.transpose(1, 0, 2).reshape(3, F2)
        dw_up = _mm_tn(sv["x1"], dpre, name="up_bwd_w", tka=D, tn=TC, nb=2 * NJ, ts=1024)
        dz1, dg1, db1 = _ln_bwd(dx1, sv["xh1"], sv["rs1"], w["g1"], name="ln1_bwd")
        dmerged = _mm(dz1, w["wo_t"], out_dtype=BF16, name="wo_bwd_x")
        dw_o = _mm_tn(sv["merged"], dz1, name="wo_bwd_w", tka=D, tn=D // 2, nb=2)
        dpg, dprod, do_a, do_b, do_c = _merge_bwd(dmerged, sv["o_a"], sv["o_b"], sv["o_c"], sv["pg"],
                                                  w["wbr"], w["wbr_t"], name="merge_bwd")
        dw_a = _mm_tn(sv["o_a"], dprod, name="branch_a_bwd_w", tka=256, tn=D, b_off=0)
        dw_b = _mm_tn(sv["o_b"], dprod, name="branch_b_bwd_w", tka=512, tn=D, b_off=1)
        dw_c = _mm_tn(sv["o_c"], dprod, name="branch_c_bwd_w", tka=256, tn=D, b_off=2)
        dpq, dsink = _attn_bwd(sv["pq"], cos_t, sin_t, w["sinks"], do_b, sv["o_b"], sv["lse"], name="attn_bwd")
        dpp, pooled, dmixed, dscale, dconv = _poolconv_bwd(sv["pp"], do_a, do_c, w["wbd"], w["wbd_t"], w["scale"],
                                                           w["conv"], name="poolconv_bwd")
        dwbd = _mm_tn(pooled, dmixed, name="pool_bwd_w", tka=256, tn=256)
        dx_args = ([dpg, dpp, dpq], [w["wg_t"], w["wp_t"], w["wq_t"]], dz1)
        if l > 0:
            below = saved[l - 1]
            ln2_out = _mm_sum(*dx_args, add_scale=alpha, name="proj_in_bwd_x",
                              ln=(below["xh2"], below["rs2"], W[l - 1]["g2"]))
        else:
            dx = _mm_sum(*dx_args, add_scale=alpha, name="proj_in_bwd_x")
        dw_g = _mm_tn(sv["x0"], dpg, name="proj_gate_bwd_w", tka=D, tn=512, nb=6)
        dw_p = _mm_tn(sv["x0"], dpp, name="proj_poolconv_bwd_w", tka=D, tn=512, nb=2)
        dw_q = _mm_tn(sv["x0"], dpq, name="proj_qkv_bwd_w", tka=D, tn=384, nb=2)
        dw_in = jnp.concatenate([dw_p[:, 0:256], dw_q, dw_p[:, 256:1024], dw_g], axis=1)
        dw_pool = jnp.stack([dwbd[64 * g:64 * (g + 1), 64 * g:64 * (g + 1)] for g in range(4)])
        gw[l] = dict(w_in=dw_in, a=dw_a, b=dw_b, c=dw_c, w_o=dw_o, w_up=dw_up, w_down=dw_dn,
                     w_pool=dw_pool, scale=dscale, sinks=dsink[:, 0], conv=dconv, fcw=_deinterleave_cols(dfcw, NJ),
                     g1=dg1, b1=db1, g2=dg2, b2=db2)
        p_l = pack_grads(gw[l])
        from_sibling = _rs_sibling(p_l, "rs_sibling")
        pair_sum[l] = [_sum_sibling(p, r, my_c, "rs_sum_sibling") for p, r in zip(p_l, from_sibling)]
    grad_x = dx.reshape(1, S, D)
    from_chips[0] = _rs_chips(pair_sum[0], "rs_chips_last")
    g_layers = [[_sum_chips(q, r, my_chip, "rs_sum_chips") for q, r in zip(pair_sum[l], from_chips[l])]
                for l in range(L)]

    def stack(k):
        return jnp.stack([gw[l][k] for l in range(L)])

    rep_vec = jnp.concatenate([
        stack("w_pool").reshape(-1), stack("scale").reshape(-1), stack("g1").reshape(-1), stack("b1").reshape(-1),
        stack("g2").reshape(-1), stack("b2").reshape(-1)])
    n_rep_full = -(-rep_vec.shape[0] // LANES)
    sinks_row = jnp.pad(stack("sinks").reshape(-1), (0, LANES - 8 * L))
    rep_vec = jnp.concatenate([_pad_rows(rep_vec, n_rep_full).reshape(-1), sinks_row, sq_lanes.reshape(-1)])
    loss_row = n_rep_full + 1
    n_rep = -(-(loss_row + 1) // 8) * 8
    rep_rows = _pad_rows(rep_vec, n_rep)
    dconv_by_dev = stack("conv").reshape(L, 3, N_DEV, -1).transpose(2, 0, 1, 3).reshape(N_DEV, -1)
    dfcw_by_dev = stack("fcw").reshape(L, 3, N_DEV, -1).transpose(2, 0, 1, 3).reshape(N_DEV, -1)
    n_mine = -(-(small_rows) // 8) * 8
    by_dev = jnp.concatenate([dconv_by_dev, dfcw_by_dev], axis=1)
    by_dev = jnp.pad(by_dev, ((0, 0), (0, n_mine * LANES - by_dev.shape[1]))).reshape(N_DEV * n_mine, LANES)
    small_g = _all_gather(jnp.concatenate([rep_rows, by_dev], axis=0), "ag_small_grads")
    rep_sum, mine_sum, loss11 = _small_reduce(small_g, n_rep, n_mine, 1.0 / D, loss_row, "small_reduce")
    loss = loss11[0, 0]

    names_big = ["w_in", "w_branch_a", "w_branch_b", "w_branch_c", "w_o", "w_up", "w_down"]
    ms_big = [m_w_in, m_w_branch_a, m_w_branch_b, m_w_branch_c, m_w_o, m_w_up, m_w_down]
    vs_big = [v_w_in, v_w_branch_a, v_w_branch_b, v_w_branch_c, v_w_o, v_w_up, v_w_down]
    out = {}
    for k, name in enumerate(names_big):
        wk = big[k]
        c2 = wk.shape[-1]
        as2d = lambda a: a.reshape(-1, c2)
        if k in (0, 5):
            g_nat = jnp.stack([g[0 if k == 0 else 1] for g in g_layers])
        else:
            rest_ks = (1, 2, 3, 4, 6)
            o = sum(rows_l[q] for q in rest_ks[:rest_ks.index(k)])
            g_nat = jnp.concatenate([g[2][o:o + rows_l[k]] for g in g_layers], axis=0).reshape(wk.shape)
        d, mo, vo = _adamw(as2d(wk), as2d(g_nat), as2d(ms_big[k]), as2d(vs_big[k]), name="adamw_" + name)
        out[name] = (g_nat, d.reshape(wk.shape), mo.reshape(wk.shape), vo.reshape(wk.shape))

    def rep_pack(wp_, sc_, g1_, b1_, g2_, b2_, sk_):
        v = jnp.concatenate([wp_.reshape(-1), sc_.reshape(-1), g1_.reshape(-1), b1_.reshape(-1), g2_.reshape(-1),
                             b2_.reshape(-1)])
        return _pad_rows(jnp.concatenate([_pad_rows(v, n_rep_full).reshape(-1), sk_.reshape(-1)]), n_rep)

    def mine_pack(cw_, fw_):
        return _pad_rows(jnp.concatenate([cw_.reshape(-1), fw_.reshape(-1)]), n_mine)

    w_rep = rep_pack(w_pool, pool_scale, ln1_g, ln1_b, ln2_g, ln2_b, attn_sinks)
    m_rep = rep_pack(m_w_pool, m_pool_scale, m_ln1_g, m_ln1_b, m_ln2_g, m_ln2_b, m_attn_sinks)
    v_rep = rep_pack(v_w_pool, v_pool_scale, v_ln1_g, v_ln1_b, v_ln2_g, v_ln2_b, v_attn_sinks)
    g_rep = jnp.concatenate([rep_sum[:loss_row], jnp.zeros((n_rep - loss_row, LANES), F32)], axis=0)
    rep_res = (g_rep,) + tuple(_adamw(w_rep, g_rep, m_rep, v_rep, name="adamw_replicated"))
    w_mine = mine_pack(conv_w, ffn_conv_w)
    mine_res = (mine_sum,) + tuple(_adamw(w_mine, mine_sum, mine_pack(m_conv_w, m_ffn_conv_w),
                                          mine_pack(v_conv_w, v_ffn_conv_w), name="adamw_conv"))

    def rep_unpack(buf):
        flat = buf.reshape(-1)
        res, o = {}, 0
        for nm, ref in (("w_pool", w_pool), ("pool_scale", pool_scale), ("ln1_g", ln1_g), ("ln1_b", ln1_b),
                        ("ln2_g", ln2_g), ("ln2_b", ln2_b)):
            res[nm] = flat[o:o + ref.size].reshape(ref.shape)
            o += ref.size
        o = n_rep_full * LANES
        res["attn_sinks"] = flat[o:o + attn_sinks.size].reshape(attn_sinks.shape)
        return res

    def mine_unpack(buf):
        flat = buf.reshape(-1)
        return {"conv_w": flat[:n_cw].reshape(conv_w.shape),
                "ffn_conv_w": flat[n_cw:n_cw + n_fw].reshape(ffn_conv_w.shape)}

    order = ["w_in", "w_pool", "pool_scale", "attn_sinks", "conv_w", "w_branch_a", "w_branch_b", "w_branch_c", "w_o",
             "ln1_g", "ln1_b", "w_up", "ffn_conv_w", "w_down", "ln2_g", "ln2_b"]
    results = [loss, grad_x]
    for kind in range(4):
        rep_k, mine_k = rep_unpack(rep_res[kind]), mine_unpack(mine_res[kind])
        for nm in order:
            if nm in out:
                results.append(out[nm][kind])
            elif nm in rep_k:
                results.append(rep_k[nm])
            else:
                results.append(mine_k[nm])
    return tuple(results)
```

```python
import functools

import jax
import jax.numpy as jnp
from jax import lax
from jax.experimental import pallas as pl
from jax.experimental.pallas import tpu as pltpu

F32 = jnp.float32
BF16 = jnp.bfloat16

HEAD_DIM = 64
N_Q_HEADS = 8
GROUP = 4
WINDOW = 128
ROT_DIM = 16
ROPE_THETA = 500000.0
POOL_WINDOWS = (2, 4, 8, 16)
LN_EPS = 1e-5
MASK_VALUE = -1e30
ADAM_LR, ADAM_B1, ADAM_B2, ADAM_EPS, ADAM_WD, ADAM_STEP = 0.001, 0.9, 0.999, 1e-08, 0.01, 10

N_DEV = 8
LANES = 1024
HALO = 16
MESH = pl.DeviceIdType.MESH
VMEM_LIMIT = 56 * 1024 * 1024


def _div_tile(n, want, mult=8):
    for t in range(min(want, n) // mult * mult, 0, -mult):
        if n % t == 0:
            return t
    return n


def _cp(*sem):
    return pltpu.CompilerParams(dimension_semantics=sem, vmem_limit_bytes=VMEM_LIMIT)


def _coords():
    return lax.axis_index("x"), lax.axis_index("y"), lax.axis_index("c")


def _all_gather(xs, name):
    xs = list(xs) if isinstance(xs, (list, tuple)) else [xs]
    n = len(xs)

    def body(*refs):
        ag_refs = (refs[:n], refs[n:2 * n]) + tuple(refs[2 * n:])
        _ag_start(*ag_refs)
        _ag_finish(*ag_refs)

    res = pl.pallas_call(
        body, name=name,
        out_shape=[jax.ShapeDtypeStruct((N_DEV,) + a.shape, a.dtype) for a in xs],
        in_specs=[pl.BlockSpec(memory_space=pl.ANY)] * n,
        out_specs=[pl.BlockSpec(memory_space=pl.ANY)] * n,
        scratch_shapes=_ag_sems(n),
    )(*xs)
    return res if n > 1 else res[0]


def _ag_sems(n):
    return [pltpu.SemaphoreType.DMA((7 * n,)), pltpu.SemaphoreType.DMA((7 * n,)), pltpu.SemaphoreType.DMA((n,))]


def _ag_copies(x_refs, out_refs, send_sems, recv_sems, local_sems):
    x, y, c = _coords()
    me, sibling = (x, y, c), (x, y, 1 - c)
    chips = [(1 - x, y), (x, 1 - y), (1 - x, 1 - y)]
    per_array = []
    for a, (x_ref, out_ref) in enumerate(zip(x_refs, out_refs)):
        def slot(px, py, pc, out_ref=out_ref):
            return out_ref.at[4 * px + 2 * py + pc]

        def copy(k, block, to, src=None, a=a, slot=slot):
            return pltpu.make_async_remote_copy(
                src_ref=slot(*block) if src is None else src, dst_ref=slot(*block),
                send_sem=send_sems.at[7 * a + k], recv_sem=recv_sems.at[7 * a + k],
                device_id=to, device_id_type=MESH)

        mine = pltpu.make_async_copy(x_ref, slot(*me), local_sems.at[a])
        first = [copy(0, me, sibling, src=x_ref)]
        first += [copy(1 + j, me, (*chip, c), src=x_ref) for j, chip in enumerate(chips)]
        passed = [copy(4 + j, (*chip, c), sibling) for j, chip in enumerate(chips)]
        from_chips = [copy(1 + j, (*chip, c), me) for j, chip in enumerate(chips)]
        from_sibling = [copy(0, sibling, me)] + [copy(4 + j, (*chip, 1 - c), me) for j, chip in enumerate(chips)]
        per_array.append((mine, first, passed, from_chips, from_sibling))
    return per_array


def _ag_start(*refs):
    for mine, first, _, _, _ in _ag_copies(*refs):
        mine.start()
        for cp in first:
            cp.start()


def _ag_finish(*refs):
    per_array = _ag_copies(*refs)
    for j in range(3):
        for _, _, passed, from_chips, _ in per_array:
            from_chips[j].wait_recv()
            passed[j].start()
    for mine, first, passed, _, from_sibling in per_array:
        for cp in from_sibling:
            cp.wait_recv()
        for cp in first + passed:
            cp.wait_send()
        mine.wait()


def _rs_sibling(ps, name):
    n = len(ps)

    def body(*refs):
        p_refs, out_refs, send_sems, recv_sems = refs[:n], refs[n:2 * n], refs[2 * n], refs[2 * n + 1]
        x, y, c = _coords()
        copies = []
        for a in range(n):
            for j in range(4):
                cx, cy = j // 2, j % 2
                copies.append(pltpu.make_async_remote_copy(
                    src_ref=p_refs[a].at[4 * cx + 2 * cy + (1 - c)], dst_ref=out_refs[a].at[j],
                    send_sem=send_sems.at[4 * a + j], recv_sem=recv_sems.at[4 * a + j],
                    device_id=(x, y, 1 - c), device_id_type=MESH))
        for cp in copies:
            cp.start()
        for cp in copies:
            cp.wait_recv()
        for cp in copies:
            cp.wait_send()

    return pl.pallas_call(
        body, name=name,
        out_shape=[jax.ShapeDtypeStruct((4,) + p.shape[1:], p.dtype) for p in ps],
        in_specs=[pl.BlockSpec(memory_space=pl.ANY)] * n,
        out_specs=[pl.BlockSpec(memory_space=pl.ANY)] * n,
        scratch_shapes=[pltpu.SemaphoreType.DMA((4 * n,)), pltpu.SemaphoreType.DMA((4 * n,))],
    )(*ps)


def _rs_chips(qs, name):
    n = len(qs)

    def body(*refs):
        rs_refs = (refs[:n], refs[n:2 * n], refs[2 * n], refs[2 * n + 1])
        _rs_chips_start(*rs_refs)
        _rs_chips_finish(*rs_refs)

    return pl.pallas_call(
        body, name=name,
        out_shape=[jax.ShapeDtypeStruct((3,) + q.shape[1:], q.dtype) for q in qs],
        in_specs=[pl.BlockSpec(memory_space=pl.ANY)] * n,
        out_specs=[pl.BlockSpec(memory_space=pl.ANY)] * n,
        scratch_shapes=_rs_sems(n),
    )(*qs)


def _rs_sems(n):
    return [pltpu.SemaphoreType.DMA((3 * n,)), pltpu.SemaphoreType.DMA((3 * n,))]


def _rs_chips_copies(q_refs, out_refs, send_sems, recv_sems):
    x, y, c = _coords()
    chips = [(1 - x, y), (x, 1 - y), (1 - x, 1 - y)]
    return [pltpu.make_async_remote_copy(
        src_ref=q_ref.at[2 * cx + cy], dst_ref=out_ref.at[k],
        send_sem=send_sems.at[3 * a + k], recv_sem=recv_sems.at[3 * a + k], device_id=(cx, cy, c),
        device_id_type=MESH)
        for a, (q_ref, out_ref) in enumerate(zip(q_refs, out_refs)) for k, (cx, cy) in enumerate(chips)]


def _rs_chips_start(*refs):
    for cp in _rs_chips_copies(*refs):
        cp.start()


def _rs_chips_finish(*refs):
    copies = _rs_chips_copies(*refs)
    for cp in copies:
        cp.wait_recv()
    for cp in copies:
        cp.wait_send()


def _sum_sibling(p, recv, my_c, name, tr=512):
    _, R, C = p.shape
    tr = _div_tile(R, tr, 16)

    def body(c_ref, p_ref, r_ref, o_ref):
        o_ref[...] = (p_ref[...].astype(F32) + r_ref[...].astype(F32)).astype(o_ref.dtype)

    grid_spec = pltpu.PrefetchScalarGridSpec(
        num_scalar_prefetch=1, grid=(4, R // tr),
        in_specs=[pl.BlockSpec((1, tr, C), lambda j, r, c_ref: (4 * (j // 2) + 2 * (j % 2) + c_ref[0], r, 0)),
                  pl.BlockSpec((1, tr, C), lambda j, r, c_ref: (j, r, 0))],
        out_specs=pl.BlockSpec((1, tr, C), lambda j, r, c_ref: (j, r, 0)))
    return pl.pallas_call(body, name=name, grid_spec=grid_spec,
                          out_shape=jax.ShapeDtypeStruct((4, R, C), p.dtype),
                          compiler_params=_cp("parallel", "parallel"))(my_c, p, recv)


def _sum_chips(q, recv, my_chip, name, tr=512):
    _, R, C = q.shape
    tr = _div_tile(R, tr, 16)

    def body(i_ref, q_ref, r_ref, o_ref):
        acc = q_ref[0].astype(F32)
        for k in range(3):
            acc = acc + r_ref[k].astype(F32)
        o_ref[...] = acc

    grid_spec = pltpu.PrefetchScalarGridSpec(
        num_scalar_prefetch=1, grid=(R // tr,),
        in_specs=[pl.BlockSpec((1, tr, C), lambda r, i_ref: (i_ref[0], r, 0)),
                  pl.BlockSpec((3, tr, C), lambda r, i_ref: (0, r, 0))],
        out_specs=pl.BlockSpec((tr, C), lambda r, i_ref: (r, 0)))
    return pl.pallas_call(body, name=name, grid_spec=grid_spec,
                          out_shape=jax.ShapeDtypeStruct((R, C), F32),
                          compiler_params=_cp("parallel"))(my_chip, q, recv)


def _small_reduce(g, n_rep, n_mine, inv_d, loss_row, name):
    _, R, C = g.shape

    def body(g_ref, rep_ref, mine_ref, loss_ref):
        x, y, c = _coords()
        start = pl.multiple_of(n_rep + (4 * x + 2 * y + c) * n_mine, 8)
        rep = g_ref[0, 0:n_rep, :]
        mine = g_ref[0, pl.ds(start, n_mine), :]
        sq = g_ref[0, loss_row:loss_row + 1, :]
        for d in range(1, N_DEV):
            rep = rep + g_ref[d, 0:n_rep, :]
            mine = mine + g_ref[d, pl.ds(start, n_mine), :]
            sq = sq + g_ref[d, loss_row:loss_row + 1, :]
        rep_ref[...] = rep
        mine_ref[...] = mine
        loss_ref[...] = (0.5 * inv_d) * jnp.sum(sq, axis=1, keepdims=True)

    return pl.pallas_call(
        body, name=name,
        out_shape=(jax.ShapeDtypeStruct((n_rep, C), F32), jax.ShapeDtypeStruct((n_mine, C), F32),
                   jax.ShapeDtypeStruct((1, 1), F32)),
        compiler_params=pltpu.CompilerParams(vmem_limit_bytes=VMEM_LIMIT),
    )(g)


def _mm(a, b, *, out_dtype, name, tm=512, tn=None, tk=None, add=None, add_scale=1.0, gather=None):
    M, K = a.shape
    N = b.shape[1]
    tm = min(tm, M)
    tn = N if tn is None else tn
    tk = K if tk is None else tk
    nk = K // tk
    has_add = add is not None
    has_ag = gather is not None
    n_g = len(gather) if has_ag else 0
    n_i, n_j = M // tm, N // tn

    def body(*refs):
        a_ref, b_ref = refs[0], refs[1]
        add_ref = refs[2] if has_add else None
        n_in = 2 + has_add + n_g
        o_ref = refs[n_in]
        if has_ag:
            ag_refs = (refs[n_in - n_g:n_in], refs[n_in + 1:n_in + 1 + n_g]) + tuple(
                refs[n_in + 1 + n_g:n_in + 4 + n_g])
            pid = (pl.program_id(0), pl.program_id(1), pl.program_id(2))

            @pl.when((pid[0] == 0) & (pid[1] == 0) & (pid[2] == 0))
            def _():
                _ag_start(*ag_refs)

        part = jnp.dot(a_ref[...].astype(BF16), b_ref[...].astype(BF16), preferred_element_type=F32)

        def finish(r):
            if has_add:
                r = r + add_scale * add_ref[...].astype(F32)
            o_ref[...] = r.astype(out_dtype)

        if nk == 1:
            finish(part)
        else:
            acc_ref = refs[-1]
            k = pl.program_id(2)

            @pl.when(k == 0)
            def _():
                acc_ref[...] = part

            @pl.when(k > 0)
            def _():
                acc_ref[...] += part

            @pl.when(k == nk - 1)
            def _():
                finish(acc_ref[...])

        if has_ag:
            @pl.when((pid[0] == n_i - 1) & (pid[1] == n_j - 1) & (pid[2] == nk - 1))
            def _():
                _ag_finish(*ag_refs)

    b_mode = dict(pipeline_mode=pl.Buffered(1)) if (n_j == 1 and nk == 1) else {}
    in_specs = [pl.BlockSpec((tm, tk), lambda i, j, k: (i, k)),
                pl.BlockSpec((tk, tn), lambda i, j, k: (k, j), **b_mode)]
    args = [a, b]
    if has_add:
        in_specs.append(pl.BlockSpec((tm, tn), lambda i, j, k: (i, j)))
        args.append(add)
    out_specs = [pl.BlockSpec((tm, tn), lambda i, j, k: (i, j))]
    out_shape = [jax.ShapeDtypeStruct((M, N), out_dtype)]
    scratch = []
    if has_ag:
        in_specs += [pl.BlockSpec(memory_space=pl.ANY)] * n_g
        args += list(gather)
        out_specs += [pl.BlockSpec(memory_space=pl.ANY)] * n_g
        out_shape += [jax.ShapeDtypeStruct((N_DEV,) + g.shape, g.dtype) for g in gather]
        scratch += _ag_sems(n_g)
    if nk > 1:
        scratch.append(pltpu.VMEM((tm, tn), F32))
    sem = ("arbitrary",) * 3 if has_ag else ("parallel", "parallel", "arbitrary")
    res = pl.pallas_call(
        body, name=name, grid=(n_i, n_j, nk), in_specs=in_specs, out_specs=out_specs, out_shape=out_shape,
        scratch_shapes=scratch, compiler_params=_cp(*sem),
    )(*args)
    return (res[0], list(res[1:])) if has_ag else res[0]


def _mm_fan(a, bs, *, out_dtype, name, tm=512, gather=None):
    M, K = a.shape
    tm = min(tm, M)
    n = len(bs)
    n_i = M // tm
    n_g = len(gather) if gather is not None else 0

    def body(*refs):
        outs = refs[1 + n + n_g:1 + 2 * n + n_g]
        if n_g:
            ag_refs = (refs[1 + n:1 + n + n_g], refs[1 + 2 * n + n_g:1 + 2 * n + 2 * n_g]) + tuple(
                refs[1 + 2 * n + 2 * n_g:])

            @pl.when(pl.program_id(0) == 0)
            def _():
                _ag_start(*ag_refs)

        a_v = refs[0][...].astype(BF16)
        for k in range(n):
            outs[k][...] = jnp.dot(a_v, refs[1 + k][...].astype(BF16), preferred_element_type=F32).astype(out_dtype)

        if n_g:
            @pl.when(pl.program_id(0) == n_i - 1)
            def _():
                _ag_finish(*ag_refs)

    row = lambda i: (i, 0)
    hbm = pl.BlockSpec(memory_space=pl.ANY)
    res = pl.pallas_call(
        body, name=name, grid=(n_i,),
        in_specs=[pl.BlockSpec((tm, K), row)] + [pl.BlockSpec(b.shape, lambda i: (0, 0)) for b in bs] + [hbm] * n_g,
        out_specs=[pl.BlockSpec((tm, b.shape[1]), row) for b in bs] + [hbm] * n_g,
        out_shape=([jax.ShapeDtypeStruct((M, b.shape[1]), out_dtype) for b in bs]
                   + [jax.ShapeDtypeStruct((N_DEV,) + g.shape, g.dtype) for g in (gather or [])]),
        scratch_shapes=_ag_sems(n_g) if n_g else [],
        compiler_params=_cp("arbitrary" if n_g else "parallel"),
    )(a, *bs, *(gather or []))
    return (list(res[:n]), list(res[n:])) if n_g else list(res)


def _mm_sum(xs, bs, add, *, add_scale, name, tm=512, ln=None):
    M = xs[0].shape[0]
    N = bs[0].shape[1]
    tm = min(tm, M)
    n = len(xs)

    def body(*refs):
        acc = add_scale * refs[2 * n][...]
        for k in range(n):
            acc = acc + jnp.dot(refs[k][...].astype(BF16), refs[n + k][...].astype(BF16), preferred_element_type=F32)
        if ln is None:
            refs[2 * n + 1][...] = acc
        else:
            xh_ref, rs_ref, g_ref, dz_ref, dg_ref, db_ref = refs[2 * n + 1:]
            _ln_bwd_tile(acc, xh_ref, rs_ref, g_ref, dz_ref, dg_ref, db_ref, pl.program_id(0) == 0)

    row = lambda i: (i, 0)
    vec = lambda i: (0, 0)
    in_specs = ([pl.BlockSpec((tm, x.shape[1]), row) for x in xs]
                + [pl.BlockSpec(b.shape, vec) for b in bs] + [pl.BlockSpec((tm, N), row)])
    if ln is None:
        return pl.pallas_call(
            body, name=name, grid=(M // tm,), in_specs=in_specs, out_specs=pl.BlockSpec((tm, N), row),
            out_shape=jax.ShapeDtypeStruct((M, N), F32), compiler_params=_cp("parallel"),
        )(*xs, *bs, add)
    in_specs += [pl.BlockSpec((tm, N), row), pl.BlockSpec((tm, 1), row), pl.BlockSpec((1, N), vec)]
    return pl.pallas_call(
        body, name=name, grid=(M // tm,), in_specs=in_specs,
        out_specs=[pl.BlockSpec((tm, N), row), pl.BlockSpec((1, N), vec), pl.BlockSpec((1, N), vec)],
        out_shape=(jax.ShapeDtypeStruct((M, N), F32), jax.ShapeDtypeStruct((1, N), F32),
                   jax.ShapeDtypeStruct((1, N), F32)),
        compiler_params=_cp("arbitrary"),
    )(*xs, *bs, add, *ln)


def _ln_bwd_tile(dyv, xh_ref, rs_ref, g_ref, dz_ref, dg_ref, db_ref, first):
    @pl.when(first)
    def _():
        dg_ref[...] = jnp.zeros_like(dg_ref)
        db_ref[...] = jnp.zeros_like(db_ref)

    xh = xh_ref[...].astype(F32)
    dyg = dyv * g_ref[...]
    c1 = jnp.mean(dyg, axis=-1, keepdims=True)
    c2 = jnp.mean(dyg * xh, axis=-1, keepdims=True)
    dz_ref[...] = rs_ref[...] * (dyg - c1 - xh * c2)
    dg_ref[...] += jnp.sum(dyv * xh, axis=0, keepdims=True)
    db_ref[...] += jnp.sum(dyv, axis=0, keepdims=True)


def _mm_ln(a, b, resid, gamma, beta, *, alpha, name, tm=512, tk=None):
    M, K = a.shape
    D = b.shape[1]
    tm = min(tm, M)
    tk = K if tk is None else tk
    nk = K // tk

    def body(a_ref, b_ref, r_ref, g_ref, be_ref, y_ref, xh_ref, rs_ref, *scratch):
        part = jnp.dot(a_ref[...].astype(BF16), b_ref[...].astype(BF16), preferred_element_type=F32)

        def finish(acc):
            z = alpha * r_ref[...] + acc
            mu = jnp.mean(z, axis=-1, keepdims=True)
            zc = z - mu
            var = jnp.mean(zc * zc, axis=-1, keepdims=True)
            rstd = lax.rsqrt(var + LN_EPS)
            xhat = zc * rstd
            y_ref[...] = xhat * g_ref[...] + be_ref[...]
            xh_ref[...] = xhat.astype(BF16)
            rs_ref[...] = rstd

        if nk == 1:
            finish(part)
        else:
            acc_ref = scratch[0]
            k = pl.program_id(1)

            @pl.when(k == 0)
            def _():
                acc_ref[...] = part

            @pl.when(k > 0)
            def _():
                acc_ref[...] += part

            @pl.when(k == nk - 1)
            def _():
                finish(acc_ref[...])

    row = lambda i, k: (i, 0)
    vec = lambda i, k: (0, 0)
    return pl.pallas_call(
        body, name=name, grid=(M // tm, nk),
        in_specs=[pl.BlockSpec((tm, tk), lambda i, k: (i, k)), pl.BlockSpec((tk, D), lambda i, k: (k, 0)),
                  pl.BlockSpec((tm, D), row), pl.BlockSpec((1, D), vec), pl.BlockSpec((1, D), vec)],
        out_specs=[pl.BlockSpec((tm, D), row), pl.BlockSpec((tm, D), row), pl.BlockSpec((tm, 1), row)],
        out_shape=(jax.ShapeDtypeStruct((M, D), F32), jax.ShapeDtypeStruct((M, D), BF16),
                   jax.ShapeDtypeStruct((M, 1), F32)),
        scratch_shapes=[pltpu.VMEM((tm, D), F32)] if nk > 1 else [],
        compiler_params=_cp("parallel", "arbitrary"),
    )(a, b, resid, gamma, beta)


def _mm_tn(a, b, *, name, tka, tn, a_off=0, na=1, b_off=0, nb=1, ts=2048):
    S = a.shape[0]
    ts = min(ts, S)

    def body(a_ref, b_ref, o_ref):
        s = pl.program_id(2)
        part = lax.dot_general(a_ref[...].astype(BF16), b_ref[...].astype(BF16),
                               (((0,), (0,)), ((), ())), preferred_element_type=F32)

        @pl.when(s == 0)
        def _():
            o_ref[...] = part

        @pl.when(s > 0)
        def _():
            o_ref[...] += part

    return pl.pallas_call(
        body, name=name, grid=(na, nb, S // ts),
        in_specs=[pl.BlockSpec((ts, tka), lambda i, j, s: (s, a_off + i)),
                  pl.BlockSpec((ts, tn), lambda i, j, s: (s, b_off + j))],
        out_specs=pl.BlockSpec((tka, tn), lambda i, j, s: (i, j)),
        out_shape=jax.ShapeDtypeStruct((na * tka, nb * tn), F32),
        compiler_params=_cp("parallel", "parallel", "arbitrary"),
    )(a, b)


def _rope_tables(pos, inv_lane, sign_lane, name, ts=512):
    S = pos.shape[0]
    ts = min(ts, S)

    def body(p_ref, inv_ref, sg_ref, cos_ref, sin_ref):
        ang = p_ref[...].astype(F32) * inv_ref[...]
        cos_ref[...] = jnp.cos(ang)
        sin_ref[...] = jnp.sin(ang) * sg_ref[...]

    return pl.pallas_call(
        body, name=name, grid=(S // ts,),
        in_specs=[pl.BlockSpec((ts, 1), lambda i: (i, 0)), pl.BlockSpec((1, 128), lambda i: (0, 0)),
                  pl.BlockSpec((1, 128), lambda i: (0, 0))],
        out_specs=[pl.BlockSpec((ts, 128), lambda i: (i, 0))] * 2,
        out_shape=(jax.ShapeDtypeStruct((S, 128), F32),) * 2,
        compiler_params=_cp("parallel"),
    )(pos, inv_lane, sign_lane)


def _rope_swap(t):
    lane = lax.broadcasted_iota(jnp.int32, (1, 128), 1)
    lo = (lane % HEAD_DIM) < (ROT_DIM // 2)
    return jnp.where(lo, pltpu.roll(t, 128 - ROT_DIM // 2, 1), pltpu.roll(t, ROT_DIM // 2, 1))


def _rope_fwd(t, cos, sin):
    return t * cos + _rope_swap(t) * sin


def _rope_bwd(d, cos, sin):
    lane = lax.broadcasted_iota(jnp.int32, (1, 128), 1)
    return d * cos + jnp.where((lane % HEAD_DIM) < ROT_DIM, _rope_swap(d * sin), 0.0)


def _tile_heads(t):
    lane = lax.broadcasted_iota(jnp.int32, (1, 128), 1)
    r = pltpu.roll(t, 64, 1)
    h0 = jnp.where(lane < 64, t, r)
    h1 = jnp.where(lane < 64, r, t)
    return jnp.concatenate([h0, h0], axis=1), jnp.concatenate([h1, h1], axis=1)


def _fold_heads(d0, d1):
    lane = lax.broadcasted_iota(jnp.int32, (1, 128), 1)

    def fold(d):
        s = d[:, 0:128] + d[:, 128:256]
        return s + pltpu.roll(s, 64, 1)

    return jnp.where(lane < 64, fold(d0), fold(d1))


def _band4(n_keys):
    row = lax.broadcasted_iota(jnp.int32, (GROUP * WINDOW, n_keys), 0) % WINDOW
    col = lax.broadcasted_iota(jnp.int32, (GROUP * WINDOW, n_keys), 1)
    return (col > row) & (col <= row + WINDOW), col


def _head_masks():
    lane = lax.broadcasted_iota(jnp.int32, (1, GROUP * HEAD_DIM), 1)
    return [(lane // HEAD_DIM) == hl for hl in range(GROUP)]


def _stack_heads(t):
    zero = jnp.zeros_like(t)
    return jnp.concatenate([jnp.where(hm, t, zero) for hm in _head_masks()], axis=0)


def _unstack_heads(t4):
    out = None
    for hl, hm in enumerate(_head_masks()):
        part = jnp.where(hm, t4[hl * WINDOW:(hl + 1) * WINDOW], 0.0)
        out = part if out is None else out + part
    return out


def _sink_block(sink_ref, g):
    return jnp.concatenate([jnp.broadcast_to(sink_ref[g * GROUP + hl:g * GROUP + hl + 1, 0:1], (WINDOW, 256))
                            for hl in range(GROUP)], axis=0)


def _sink_column(sink_ref, g):
    return jnp.concatenate([jnp.broadcast_to(sink_ref[g * GROUP + hl:g * GROUP + hl + 1, 0:1], (WINDOW, 1))
                            for hl in range(GROUP)], axis=0)


def _attn_fwd(pq, cos_t, sin_t, sinks_b, *, name, ts=256):
    S = pq.shape[0]
    ts = min(ts, S)
    nq = ts // WINDOW
    scale = HEAD_DIM ** -0.5

    def body(cur_ref, prev_ref, cosc_ref, sinc_ref, cosp_ref, sinp_ref, sink_ref, o_ref, lse_ref):
        i = pl.program_id(0)
        cosc, sinc = cosc_ref[...], sinc_ref[...]
        q = cur_ref[:, 0:512].astype(F32)
        qr = jnp.concatenate(
            [_rope_fwd(q[:, j * 128:(j + 1) * 128], cosc, sinc) for j in range(4)], axis=1) * scale
        qr = qr.astype(BF16)
        kc = _rope_fwd(cur_ref[:, 512:640].astype(F32), cosc, sinc)
        kp = _rope_fwd(prev_ref[:, 0:128].astype(F32), cosp_ref[...], sinp_ref[...])
        k_all = jnp.concatenate([kp, kc], axis=0)
        v_all = jnp.concatenate([prev_ref[:, 128:256].astype(F32), cur_ref[:, 640:768].astype(F32)], axis=0)
        kt = [t.astype(BF16) for t in _tile_heads(k_all)]
        vt = [t.astype(BF16) for t in _tile_heads(v_all)]
        band, col = _band4(2 * WINDOW)
        ones = jnp.ones((2 * WINDOW, 256), BF16)
        key_t = lax.broadcasted_iota(jnp.int32, (2 * WINDOW, GROUP * WINDOW), 0)
        qry_t = lax.broadcasted_iota(jnp.int32, (2 * WINDOW, GROUP * WINDOW), 1) % WINDOW
        band_t = (key_t > qry_t) & (key_t <= qry_t + WINDOW)
        NT = (((1,), (1,)), ((), ()))
        for qb in range(nq):
            rows = slice(qb * WINDOW, (qb + 1) * WINDOW)
            keys = slice(qb * WINDOW, (qb + 2) * WINDOW)
            valid = band & ((col >= WINDOW) | (i * nq + qb > 0))
            valid_t = band_t & ((key_t >= WINDOW) | (i * nq + qb > 0))
            for g in range(2):
                qs = _stack_heads(qr[rows, g * 256:(g + 1) * 256])
                sink = _sink_block(sink_ref, g)
                s = lax.dot_general(qs, kt[g][keys], NT, preferred_element_type=F32)
                s_t = lax.dot_general(kt[g][keys], qs, NT, preferred_element_type=F32)
                m_t = jnp.max(jnp.where(valid_t, s_t, MASK_VALUE), axis=0, keepdims=True)
                m_rep = jnp.broadcast_to(m_t, (WINDOW, GROUP * WINDOW)).T
                m = jnp.maximum(jnp.concatenate([m_rep, m_rep], axis=1), sink)
                e = jnp.exp(jnp.where(valid, s, MASK_VALUE) - m).astype(BF16)
                l = jnp.dot(e, ones, preferred_element_type=F32) + jnp.exp(sink - m)
                pv = jnp.dot(e, vt[g][keys], preferred_element_type=F32)
                o_ref[rows, g * 256:(g + 1) * 256] = (_unstack_heads(pv) / _unstack_heads(l)).astype(BF16)
                lse4 = (m + jnp.log(l))[:, 0:1]
                for hl in range(GROUP):
                    h = g * GROUP + hl
                    lse_ref[rows, h:h + 1] = lse4[hl * WINDOW:(hl + 1) * WINDOW]

    hb = ts // WINDOW
    cur = lambda i: (i, 0)
    prev = lambda i: (jnp.maximum(i * hb - 1, 0), 0)
    return pl.pallas_call(
        body, name=name, grid=(S // ts,),
        in_specs=[pl.BlockSpec((ts, 768), cur),
                  pl.BlockSpec((WINDOW, 256), lambda i: (jnp.maximum(i * hb - 1, 0), 2)),
                  pl.BlockSpec((ts, 128), cur), pl.BlockSpec((ts, 128), cur),
                  pl.BlockSpec((WINDOW, 128), prev), pl.BlockSpec((WINDOW, 128), prev),
                  pl.BlockSpec((8, 128), lambda i: (0, 0))],
        out_specs=[pl.BlockSpec((ts, 512), cur), pl.BlockSpec((ts, 8), cur)],
        out_shape=(jax.ShapeDtypeStruct((S, 512), BF16), jax.ShapeDtypeStruct((S, 8), F32)),
        compiler_params=_cp("parallel"),
    )(pq, pq, cos_t, sin_t, cos_t, sin_t, sinks_b)


def _attn_bwd(pq, cos_t, sin_t, sinks_b, do, o, lse, *, name, ts=256):
    S = pq.shape[0]
    ts = min(ts, S)
    nq = ts // WINDOW
    nt = S // ts
    scale = HEAD_DIM ** -0.5
    NT = (((1,), (1,)), ((), ()))
    TN = (((0,), (0,)), ((), ()))

    def body(cur_ref, prev_ref, nxt_ref, cosc_ref, sinc_ref, cosp_ref, sinp_ref, cosn_ref, sinn_ref, sink_ref,
             doc_ref, don_ref, oc_ref, on_ref, lsec_ref, lsen_ref, dpq_ref, dsink_ref):
        i = pl.program_id(0)
        last = i == nt - 1
        cosc, sinc = cosc_ref[...], sinc_ref[...]
        cose = jnp.concatenate([cosc, cosn_ref[...]], axis=0)
        sine = jnp.concatenate([sinc, sinn_ref[...]], axis=0)
        q = jnp.concatenate([cur_ref[:, 0:512], nxt_ref[:, 0:512]], axis=0).astype(F32)
        qr = jnp.concatenate(
            [_rope_fwd(q[:, j * 128:(j + 1) * 128], cose, sine) for j in range(4)], axis=1) * scale
        qr = qr.astype(BF16)
        kc = _rope_fwd(cur_ref[:, 512:640].astype(F32), cosc, sinc)
        kp = _rope_fwd(prev_ref[:, 0:128].astype(F32), cosp_ref[...], sinp_ref[...])
        k_all = jnp.concatenate([kp, kc], axis=0)
        v_all = jnp.concatenate([prev_ref[:, 128:256].astype(F32), cur_ref[:, 640:768].astype(F32)], axis=0)
        kt = [t.astype(BF16) for t in _tile_heads(k_all)]
        vt = [t.astype(BF16) for t in _tile_heads(v_all)]
        don = jnp.where(last, jnp.zeros_like(don_ref[...]), don_ref[...])
        do_e = jnp.concatenate([doc_ref[...], don], axis=0)
        o_e = jnp.concatenate([oc_ref[...], on_ref[...]], axis=0)
        band2, col2 = _band4(2 * WINDOW)
        band1, _ = _band4(WINDOW)
        ones = jnp.ones((256, 256), BF16)

        @pl.when(i == 0)
        def _():
            dsink_ref[...] = jnp.zeros_like(dsink_ref)

        dk_acc = [[None] * (nq + 1) for _ in range(2)]
        dv_acc = [[None] * (nq + 1) for _ in range(2)]

        def add(acc, g, e, val):
            acc[g][e] = val if acc[g][e] is None else acc[g][e] + val

        for qb in range(nq + 1):
            halo = qb == nq
            rows = slice(qb * WINDOW, (qb + 1) * WINDOW)
            if halo:
                keys = slice(qb * WINDOW, (qb + 1) * WINDOW)
                valid = band1 & jnp.logical_not(last)
            else:
                keys = slice(qb * WINDOW, (qb + 2) * WINDOW)
                valid = band2 & ((col2 >= WINDOW) | (i * nq + qb > 0))
            dq_parts = []
            for g in range(2):
                qs = _stack_heads(qr[rows, g * 256:(g + 1) * 256])
                dos = _stack_heads(do_e[rows, g * 256:(g + 1) * 256])
                o_g = o_e[rows, g * 256:(g + 1) * 256].astype(F32)
                kt_b, vt_b = kt[g][keys], vt[g][keys]
                lse_src = lsen_ref if halo else lsec_ref
                lse_rows = slice(0, WINDOW) if halo else rows
                big_l = jnp.concatenate([lse_src[lse_rows, g * GROUP + hl:g * GROUP + hl + 1] for hl in range(GROUP)],
                                        axis=0)
                delta = jnp.dot((dos.astype(F32) * jnp.concatenate([o_g] * GROUP, axis=0)).astype(BF16), ones,
                                preferred_element_type=F32)
                s = lax.dot_general(qs, kt_b, NT, preferred_element_type=F32)
                p = jnp.exp(jnp.where(valid, s, MASK_VALUE) - big_l)
                dp = lax.dot_general(dos, vt_b, NT, preferred_element_type=F32)
                ds = (p * (dp - delta[:, 0:p.shape[1]])).astype(BF16)
                dk_g = lax.dot_general(ds, qs, TN, preferred_element_type=F32)
                dv_g = lax.dot_general(p.astype(BF16), dos, TN, preferred_element_type=F32)
                if not halo:
                    dq_parts.append(_unstack_heads(jnp.dot(ds, kt_b, preferred_element_type=F32)))
                    dsink4 = jnp.exp(_sink_column(sink_ref, g) - big_l) * delta[:, 0:1]
                    for hl in range(GROUP):
                        h = g * GROUP + hl
                        dsink_h = -jnp.sum(dsink4[hl * WINDOW:(hl + 1) * WINDOW], axis=0, keepdims=True)
                        dsink_ref[h:h + 1, :] += jnp.broadcast_to(dsink_h, (1, 128))
                add(dk_acc, g, qb, dk_g[0:WINDOW])
                add(dv_acc, g, qb, dv_g[0:WINDOW])
                if not halo:
                    add(dk_acc, g, qb + 1, dk_g[WINDOW:2 * WINDOW])
                    add(dv_acc, g, qb + 1, dv_g[WINDOW:2 * WINDOW])
            if not halo:
                cs, sn = cosc[rows], sinc[rows]
                for g in range(2):
                    dq_g = dq_parts[g] * scale
                    for j in range(2):
                        c0 = g * 256 + j * 128
                        dpq_ref[rows, c0:c0 + 128] = _rope_bwd(dq_g[:, j * 128:(j + 1) * 128], cs, sn).astype(BF16)
        for e in range(1, nq + 1):
            rows = slice((e - 1) * WINDOW, e * WINDOW)
            dk = _fold_heads(dk_acc[0][e], dk_acc[1][e])
            dv = _fold_heads(dv_acc[0][e], dv_acc[1][e])
            dpq_ref[rows, 512:640] = _rope_bwd(dk, cosc[rows], sinc[rows]).astype(BF16)
            dpq_ref[rows, 640:768] = dv.astype(BF16)

    hb = ts // WINDOW
    nblk = S // WINDOW
    cur = lambda i: (i, 0)
    prev = lambda i: (jnp.maximum(i * hb - 1, 0), 0)
    nxt = lambda i: (jnp.minimum((i + 1) * hb, nblk - 1), 0)
    return pl.pallas_call(
        body, name=name, grid=(nt,),
        in_specs=[pl.BlockSpec((ts, 768), cur),
                  pl.BlockSpec((WINDOW, 256), lambda i: (jnp.maximum(i * hb - 1, 0), 2)),
                  pl.BlockSpec((WINDOW, 768), nxt),
                  pl.BlockSpec((ts, 128), cur), pl.BlockSpec((ts, 128), cur),
                  pl.BlockSpec((WINDOW, 128), prev), pl.BlockSpec((WINDOW, 128), prev),
                  pl.BlockSpec((WINDOW, 128), nxt), pl.BlockSpec((WINDOW, 128), nxt),
                  pl.BlockSpec((8, 128), lambda i: (0, 0)),
                  pl.BlockSpec((ts, 512), cur), pl.BlockSpec((WINDOW, 512), nxt),
                  pl.BlockSpec((ts, 512), cur), pl.BlockSpec((WINDOW, 512), nxt),
                  pl.BlockSpec((ts, 8), cur), pl.BlockSpec((WINDOW, 8), nxt)],
        out_specs=[pl.BlockSpec((ts, 768), cur), pl.BlockSpec((8, 128), lambda i: (0, 0))],
        out_shape=(jax.ShapeDtypeStruct((S, 768), BF16), jax.ShapeDtypeStruct((8, 128), F32)),
        compiler_params=_cp("arbitrary"),
    )(pq, pq, pq, cos_t, sin_t, cos_t, sin_t, cos_t, sin_t, sinks_b, do, do, o, o, lse, lse)


def _shift_dn(x, k):
    return pltpu.roll(x, k, 0)


def _shift_up(x, k):
    return pltpu.roll(x, x.shape[0] - k, 0)


def _pool_lane_select(vals):
    lane = lax.broadcasted_iota(jnp.int32, (1, 256), 1)
    out = vals[3]
    for g in (2, 1, 0):
        out = jnp.where(lane < 64 * (g + 1), vals[g], out)
    return out


def _pool_inv_count(t0, n):
    t = t0 + lax.broadcasted_iota(jnp.int32, (n, 256), 0)
    lane = lax.broadcasted_iota(jnp.int32, (n, 256), 1)
    w = jnp.where(lane < 64, 2, jnp.where(lane < 128, 4, jnp.where(lane < 192, 8, 16)))
    return 1.0 / jnp.minimum(t + 1, w).astype(F32)


def _pooled(u_ext, t0, n):
    s2 = u_ext + _shift_dn(u_ext, 1)
    s4 = s2 + _shift_dn(s2, 2)
    s8 = s4 + _shift_dn(s4, 4)
    s16 = s8 + _shift_dn(s8, 8)
    win = _pool_lane_select([s2, s4, s8, s16])[HALO:HALO + n]
    return win * _pool_inv_count(t0, n) - u_ext[HALO:HALO + n]


def _poolconv_fwd(pp, wbd, pool_scale, conv_w, *, name, ts=512):
    S = pp.shape[0]
    ts = min(ts, S)

    def body(cur_ref, prev_ref, wbd_ref, sc_ref, cw_ref, oa_ref, oc_ref):
        i = pl.program_id(0)
        prev = jnp.where(i > 0, prev_ref[...].astype(F32), 0.0)
        u_ext = jnp.concatenate([prev[:, 0:256], cur_ref[:, 0:256].astype(F32)], axis=0)
        pooled = _pooled(u_ext, i * ts, ts)
        mixed = jnp.dot(pooled.astype(BF16), wbd_ref[...], preferred_element_type=F32)
        oa_ref[...] = (mixed * sc_ref[...]).astype(BF16)
        v_ext = jnp.concatenate([prev[:, 256:512] * prev[:, 768:1024],
                                 cur_ref[:, 256:512].astype(F32) * cur_ref[:, 768:1024].astype(F32)], axis=0)
        cv = cw_ref[2:3, :] * v_ext + cw_ref[1:2, :] * _shift_dn(v_ext, 1) + cw_ref[0:1, :] * _shift_dn(v_ext, 2)
        oc_ref[...] = (cur_ref[:, 512:768].astype(F32) * cv[HALO:HALO + ts]).astype(BF16)

    hb = ts // HALO
    cur = lambda i: (i, 0)
    const = lambda i: (0, 0)
    return pl.pallas_call(
        body, name=name, grid=(S // ts,),
        in_specs=[pl.BlockSpec((ts, 1024), cur),
                  pl.BlockSpec((HALO, 1024), lambda i: (jnp.maximum(i * hb - 1, 0), 0)),
                  pl.BlockSpec((256, 256), const), pl.BlockSpec((1, 256), const), pl.BlockSpec((3, 256), const)],
        out_specs=[pl.BlockSpec((ts, 256), cur)] * 2,
        out_shape=(jax.ShapeDtypeStruct((S, 256), BF16),) * 2,
        compiler_params=_cp("parallel"),
    )(pp, pp, wbd, pool_scale, conv_w)


def _poolconv_bwd(pp, do_a, do_c, wbd, wbd_t, pool_scale, conv_w, *, name, ts=512):
    S = pp.shape[0]
    ts = min(ts, S)
    nt = S // ts
    n_e = ts + 2 * HALO

    def body(cur_ref, prev_ref, nxt_ref, dac_ref, dan_ref, dcc_ref, dcn_ref, wbd_ref, wbdt_ref, sc_ref, cw_ref,
             dpp_ref, pooled_ref, dmixed_ref, dsc_ref, dcw_ref):
        i = pl.program_id(0)

        @pl.when(i == 0)
        def _():
            dsc_ref[...] = jnp.zeros_like(dsc_ref)
            dcw_ref[...] = jnp.zeros_like(dcw_ref)

        prev = jnp.where(i > 0, prev_ref[...].astype(F32), 0.0)
        nxt = nxt_ref[...].astype(F32)
        cur = cur_ref[...].astype(F32)
        not_last = i < nt - 1
        da_n = jnp.where(not_last, dan_ref[...].astype(F32), 0.0)
        dc_n = jnp.where(not_last, dcn_ref[...].astype(F32), 0.0)
        zeros_h = jnp.zeros((HALO, 256), F32)
        sc = sc_ref[...]

        u_ext = jnp.concatenate([prev[:, 0:256], cur[:, 0:256]], axis=0)
        pooled = _pooled(u_ext, i * ts, ts)
        pooled_b = pooled.astype(BF16)
        pooled_ref[...] = pooled_b
        mixed = jnp.dot(pooled_b, wbd_ref[...], preferred_element_type=F32)
        da_c = dac_ref[...].astype(F32)
        dsc_ref[...] += jnp.sum(da_c * mixed, axis=0, keepdims=True)
        dmixed_e = jnp.concatenate([da_c, da_n], axis=0) * sc
        dmixed_ref[...] = dmixed_e[0:ts].astype(BF16)
        dpooled = jnp.dot(dmixed_e.astype(BF16), wbdt_ref[...], preferred_element_type=F32)
        qd = dpooled * _pool_inv_count(i * ts, ts + HALO)
        f2 = qd + _shift_up(qd, 1)
        f4 = f2 + _shift_up(f2, 2)
        f8 = f4 + _shift_up(f4, 4)
        f16 = f8 + _shift_up(f8, 8)
        du = (_pool_lane_select([f2, f4, f8, f16]) - dpooled)[0:ts]
        dpp_ref[:, 0:256] = du.astype(BF16)

        xc_e = jnp.concatenate([prev[:, 256:512], cur[:, 256:512], nxt[:, 256:512]], axis=0)
        gc_e = jnp.concatenate([prev[:, 768:1024], cur[:, 768:1024], nxt[:, 768:1024]], axis=0)
        gb_e = jnp.concatenate([zeros_h, cur[:, 512:768], nxt[:, 512:768]], axis=0)
        dc_e = jnp.concatenate([zeros_h, dcc_ref[...].astype(F32), dc_n], axis=0)
        v_e = xc_e * gc_e
        v1, v2 = _shift_dn(v_e, 1), _shift_dn(v_e, 2)
        w0, w1, w2 = cw_ref[0:1, :], cw_ref[1:2, :], cw_ref[2:3, :]
        cv = w2 * v_e + w1 * v1 + w0 * v2
        dcv = dc_e * gb_e
        dv = w2 * dcv + w1 * _shift_up(dcv, 1) + w0 * _shift_up(dcv, 2)
        tile = slice(HALO, HALO + ts)
        dpp_ref[:, 256:512] = (dv * gc_e)[tile].astype(BF16)
        dpp_ref[:, 512:768] = (dc_e * cv)[tile].astype(BF16)
        dpp_ref[:, 768:1024] = (dv * xc_e)[tile].astype(BF16)
        dcv_t = dcv[tile]
        dcw_ref[0:1, :] += jnp.sum(dcv_t * v2[tile], axis=0, keepdims=True)
        dcw_ref[1:2, :] += jnp.sum(dcv_t * v1[tile], axis=0, keepdims=True)
        dcw_ref[2:3, :] += jnp.sum(dcv_t * v_e[tile], axis=0, keepdims=True)

    hb = ts // HALO
    nblk = S // HALO
    cur = lambda i: (i, 0)
    const = lambda i: (0, 0)
    prev = lambda i: (jnp.maximum(i * hb - 1, 0), 0)
    nxt = lambda i: (jnp.minimum((i + 1) * hb, nblk - 1), 0)
    del n_e
    return pl.pallas_call(
        body, name=name, grid=(nt,),
        in_specs=[pl.BlockSpec((ts, 1024), cur), pl.BlockSpec((HALO, 1024), prev), pl.BlockSpec((HALO, 1024), nxt),
                  pl.BlockSpec((ts, 256), cur), pl.BlockSpec((HALO, 256), nxt),
                  pl.BlockSpec((ts, 256), cur), pl.BlockSpec((HALO, 256), nxt),
                  pl.BlockSpec((256, 256), const), pl.BlockSpec((256, 256), const),
                  pl.BlockSpec((1, 256), const), pl.BlockSpec((3, 256), const)],
        out_specs=[pl.BlockSpec((ts, 1024), cur), pl.BlockSpec((ts, 256), cur), pl.BlockSpec((ts, 256), cur),
                   pl.BlockSpec((1, 256), const), pl.BlockSpec((3, 256), const)],
        out_shape=(jax.ShapeDtypeStruct((S, 1024), BF16), jax.ShapeDtypeStruct((S, 256), BF16),
                   jax.ShapeDtypeStruct((S, 256), BF16), jax.ShapeDtypeStruct((1, 256), F32),
                   jax.ShapeDtypeStruct((3, 256), F32)),
        compiler_params=_cp("arbitrary"),
    )(pp, pp, pp, do_a, do_a, do_c, do_c, wbd, wbd_t, pool_scale, conv_w)


def _sigmoid(x):
    return 0.5 * jnp.tanh(0.5 * x) + 0.5


def _merge_fwd(o_a, o_b, o_c, glog, w_br, *, name, ts=512):
    S = o_a.shape[0]
    D = w_br.shape[1]
    ts = min(ts, S)

    def body(oa_ref, ob_ref, oc_ref, gl_ref, w_ref, m_ref):
        pa = jnp.dot(oa_ref[...], w_ref[0:256, :], preferred_element_type=F32)
        pb = jnp.dot(ob_ref[...], w_ref[256:768, :], preferred_element_type=F32)
        pc = jnp.dot(oc_ref[...], w_ref[768:1024, :], preferred_element_type=F32)
        m = _sigmoid(gl_ref[:, 0:D].astype(F32)) * pa
        m = m + _sigmoid(gl_ref[:, D:2 * D].astype(F32)) * pb
        m = m + _sigmoid(gl_ref[:, 2 * D:3 * D].astype(F32)) * pc
        m_ref[...] = m.astype(BF16)

    cur = lambda i: (i, 0)
    return pl.pallas_call(
        body, name=name, grid=(S // ts,),
        in_specs=[pl.BlockSpec((ts, 256), cur), pl.BlockSpec((ts, 512), cur), pl.BlockSpec((ts, 256), cur),
                  pl.BlockSpec((ts, 3 * D), cur), pl.BlockSpec((1024, D), lambda i: (0, 0))],
        out_specs=pl.BlockSpec((ts, D), cur),
        out_shape=jax.ShapeDtypeStruct((S, D), BF16),
        compiler_params=_cp("parallel"),
    )(o_a, o_b, o_c, glog, w_br)


def _merge_bwd(dm, o_a, o_b, o_c, glog, w_br, w_br_t, *, name, ts=256):
    S = o_a.shape[0]
    D = w_br.shape[1]
    ts = min(ts, S)

    def body(dm_ref, oa_ref, ob_ref, oc_ref, gl_ref, w_ref, wt_ref, dgl_ref, dp_ref, doa_ref, dob_ref, doc_ref):
        dmv = dm_ref[...].astype(F32)
        branches = ((oa_ref, 0, 256, doa_ref), (ob_ref, 256, 768, dob_ref), (oc_ref, 768, 1024, doc_ref))
        for b, (o_ref, r0, r1, do_ref) in enumerate(branches):
            prod = jnp.dot(o_ref[...], w_ref[r0:r1, :], preferred_element_type=F32)
            gate = _sigmoid(gl_ref[:, b * D:(b + 1) * D].astype(F32))
            dgl_ref[:, b * D:(b + 1) * D] = (dmv * prod * gate * (1.0 - gate)).astype(BF16)
            dprod = (dmv * gate).astype(BF16)
            dp_ref[:, b * D:(b + 1) * D] = dprod
            do_ref[...] = jnp.dot(dprod, wt_ref[:, r0:r1], preferred_element_type=F32).astype(BF16)

    cur = lambda i: (i, 0)
    const = lambda i: (0, 0)
    return pl.pallas_call(
        body, name=name, grid=(S // ts,),
        in_specs=[pl.BlockSpec((ts, D), cur), pl.BlockSpec((ts, 256), cur), pl.BlockSpec((ts, 512), cur),
                  pl.BlockSpec((ts, 256), cur), pl.BlockSpec((ts, 3 * D), cur),
                  pl.BlockSpec((1024, D), const), pl.BlockSpec((D, 1024), const)],
        out_specs=[pl.BlockSpec((ts, 3 * D), cur), pl.BlockSpec((ts, 3 * D), cur), pl.BlockSpec((ts, 256), cur),
                   pl.BlockSpec((ts, 512), cur), pl.BlockSpec((ts, 256), cur)],
        out_shape=(jax.ShapeDtypeStruct((S, 3 * D), BF16), jax.ShapeDtypeStruct((S, 3 * D), BF16),
                   jax.ShapeDtypeStruct((S, 256), BF16), jax.ShapeDtypeStruct((S, 512), BF16),
                   jax.ShapeDtypeStruct((S, 256), BF16)),
        compiler_params=_cp("parallel"),
    )(dm, o_a, o_b, o_c, glog, w_br, w_br_t)


FFN_CHUNK = 128
FFN_DOT_CHUNKS = 4


def _conv3(x, w_ref, cols):
    x1, x2 = _shift_dn(x, 1), _shift_dn(x, 2)
    return w_ref[2:3, cols] * x + w_ref[1:2, cols] * x1 + w_ref[0:1, cols] * x2, x1, x2


def _ffn_down_fwd(up_pre, fcw, w_down3, resid, gamma, beta, *, alpha, name, tc, ts=256, gather=None):
    S, F2 = up_pre.shape
    D = resid.shape[1]
    ts = min(ts, S)
    nt = S // ts
    nj = F2 // (2 * tc)
    has_ag = gather is not None
    n_g = len(gather) if has_ag else 0

    def body(cur_ref, prev_ref, w_ref, wd_ref, r_ref, g_ref, be_ref, *rest):
        h_ref, y_ref, xh_ref, rs_ref, up_ref = rest[n_g:n_g + 5]
        acc_ref = rest[2 * n_g + 5]
        if has_ag:
            ag_refs = (rest[:n_g], rest[n_g + 5:2 * n_g + 5]) + tuple(rest[2 * n_g + 6:2 * n_g + 9])
        i, j = pl.program_id(0), pl.program_id(1)
        if has_ag:
            @pl.when((i == 0) & (j == 0))
            def _():
                _ag_start(*ag_refs)

        part = None
        for c in range(tc // FFN_CHUNK):
            halves = []
            for half in range(2):
                cols = slice(half * tc + c * FFN_CHUNK, half * tc + (c + 1) * FFN_CHUNK)
                prev = jnp.where(i > 0, prev_ref[:, cols].astype(F32), 0.0)
                x = jnp.concatenate([prev, cur_ref[:, cols].astype(F32)], axis=0)
                halves.append(_conv3(x, w_ref, cols)[0][HALO:HALO + ts])
                up_ref[:, cols] = halves[-1].astype(BF16)
            a, b = halves
            h_ref[:, c * FFN_CHUNK:(c + 1) * FFN_CHUNK] = (a * _sigmoid(a) * b).astype(BF16)
            if (c + 1) % FFN_DOT_CHUNKS == 0 or c + 1 == tc // FFN_CHUNK:
                k0 = (c // FFN_DOT_CHUNKS) * FFN_DOT_CHUNKS * FFN_CHUNK
                piece = jnp.dot(h_ref[:, k0:(c + 1) * FFN_CHUNK], wd_ref[j, k0:(c + 1) * FFN_CHUNK, :],
                                preferred_element_type=F32)
                part = piece if part is None else part + piece

        @pl.when(j == 0)
        def _():
            acc_ref[...] = part

        @pl.when(j > 0)
        def _():
            acc_ref[...] += part

        @pl.when(j == nj - 1)
        def _():
            z = alpha * r_ref[...] + acc_ref[...]
            mu = jnp.mean(z, axis=-1, keepdims=True)
            zc = z - mu
            var = jnp.mean(zc * zc, axis=-1, keepdims=True)
            rstd = lax.rsqrt(var + LN_EPS)
            xhat = zc * rstd
            y_ref[...] = xhat * g_ref[...] + be_ref[...]
            xh_ref[...] = xhat.astype(BF16)
            rs_ref[...] = rstd

        if has_ag:
            @pl.when((i == nt - 1) & (j == nj - 1))
            def _():
                _ag_finish(*ag_refs)

    hb = ts // HALO
    row = lambda i, j: (i, 0)
    vec = lambda i, j: (0, 0)
    in_specs = [pl.BlockSpec((ts, 2 * tc), lambda i, j: (i, j)),
                pl.BlockSpec((HALO, 2 * tc), lambda i, j: (jnp.maximum(i * hb - 1, 0), j)),
                pl.BlockSpec((3, 2 * tc), lambda i, j: (0, j)),
                pl.BlockSpec((nj, tc, D), lambda i, j: (0, 0, 0)),
                pl.BlockSpec((ts, D), row), pl.BlockSpec((1, D), vec), pl.BlockSpec((1, D), vec)]
    out_specs = [pl.BlockSpec((ts, tc), lambda i, j: (i, j)), pl.BlockSpec((ts, D), row), pl.BlockSpec((ts, D), row),
                 pl.BlockSpec((ts, 1), row), pl.BlockSpec((ts, 2 * tc), lambda i, j: (i, j))]
    out_shape = [jax.ShapeDtypeStruct((S, F2 // 2), BF16), jax.ShapeDtypeStruct((S, D), F32),
                 jax.ShapeDtypeStruct((S, D), BF16), jax.ShapeDtypeStruct((S, 1), F32),
                 jax.ShapeDtypeStruct((S, F2), BF16)]
    args = [up_pre, up_pre, fcw, w_down3, resid, gamma, beta]
    scratch = [pltpu.VMEM((ts, D), F32)]
    if has_ag:
        in_specs += [pl.BlockSpec(memory_space=pl.ANY)] * n_g
        args += list(gather)
        out_specs += [pl.BlockSpec(memory_space=pl.ANY)] * n_g
        out_shape += [jax.ShapeDtypeStruct((N_DEV,) + g.shape, g.dtype) for g in gather]
        scratch += _ag_sems(n_g)
    res = pl.pallas_call(
        body, name=name, grid=(nt, nj), in_specs=in_specs, out_specs=out_specs, out_shape=out_shape,
        scratch_shapes=scratch, compiler_params=_cp("arbitrary", "arbitrary"),
    )(*args)
    return tuple(res[:5]) + ((list(res[5:]),) if has_ag else ())


def _ffn_up_bwd(up_pre, up, dh, fcw, w_up_t3, dz, *, alpha, name, tc, ts=256, scatter=None):
    S, F2 = up_pre.shape
    D = dz.shape[1]
    ts = min(ts, S)
    nt = S // ts
    nj = F2 // (2 * tc)
    has_rs = scatter is not None
    n_s = len(scatter) if has_rs else 0
    tile = slice(0, ts)

    def body(x_ref, upc_ref, upn_ref, dhc_ref, dhn_ref, w_ref, wt_ref, dz_ref, *rest):
        dpre_ref, dx_ref, dw_ref = rest[n_s:n_s + 3]
        acc_ref = rest[2 * n_s + 3]
        if has_rs:
            rs_refs = (rest[:n_s], rest[n_s + 3:2 * n_s + 3], rest[2 * n_s + 4], rest[2 * n_s + 5])
        i, j = pl.program_id(0), pl.program_id(1)

        @pl.when((i == 0) & (j == 0))
        def _():
            dw_ref[...] = jnp.zeros_like(dw_ref)
            if has_rs:
                _rs_chips_start(*rs_refs)

        part = None
        for c in range(tc // FFN_CHUNK):
            lanes = slice(c * FFN_CHUNK, (c + 1) * FFN_CHUNK)
            dh_n = jnp.where(i < nt - 1, dhn_ref[:, lanes].astype(F32), 0.0)
            dh_e = jnp.concatenate([dhc_ref[:, lanes].astype(F32), dh_n], axis=0)
            cols_of = [slice(half * tc + c * FFN_CHUNK, half * tc + (c + 1) * FFN_CHUNK) for half in range(2)]
            a, b = [jnp.concatenate([upc_ref[:, cols].astype(F32), upn_ref[:, cols].astype(F32)], axis=0)
                    for cols in cols_of]
            sg = _sigmoid(a)
            dups = [dh_e * b * (sg * (1.0 + a * (1.0 - sg))), dh_e * (a * sg)]
            for half in range(2):
                cols, dup = cols_of[half], dups[half]
                dup1, dup2 = _shift_up(dup, 1), _shift_up(dup, 2)
                dpre = w_ref[2:3, cols] * dup + w_ref[1:2, cols] * dup1 + w_ref[0:1, cols] * dup2
                dpre_ref[:, cols] = dpre[tile].astype(BF16)
                x = x_ref[:, cols].astype(F32)
                dw_ref[j, 0:1, cols] += jnp.sum(dup2[tile] * x, axis=0, keepdims=True)
                dw_ref[j, 1:2, cols] += jnp.sum(dup1[tile] * x, axis=0, keepdims=True)
                dw_ref[j, 2:3, cols] += jnp.sum(dup[tile] * x, axis=0, keepdims=True)
            if (c + 1) % FFN_DOT_CHUNKS == 0 or c + 1 == tc // FFN_CHUNK:
                k0 = (c // FFN_DOT_CHUNKS) * FFN_DOT_CHUNKS * FFN_CHUNK
                for half in range(2):
                    ks = slice(half * tc + k0, half * tc + (c + 1) * FFN_CHUNK)
                    piece = jnp.dot(dpre_ref[:, ks], wt_ref[j, ks, :], preferred_element_type=F32)
                    part = piece if part is None else part + piece

        @pl.when(j == 0)
        def _():
            acc_ref[...] = part

        @pl.when(j > 0)
        def _():
            acc_ref[...] += part

        @pl.when(j == nj - 1)
        def _():
            dx_ref[...] = acc_ref[...] + alpha * dz_ref[...]

        if has_rs:
            @pl.when((i == nt - 1) & (j == nj - 1))
            def _():
                _rs_chips_finish(*rs_refs)

    hb = ts // HALO
    nblk = S // HALO
    nxt = lambda i, j: (jnp.minimum((i + 1) * hb, nblk - 1), j)
    row = lambda i, j: (i, 0)
    in_specs = [pl.BlockSpec((ts, 2 * tc), lambda i, j: (i, j)),
                pl.BlockSpec((ts, 2 * tc), lambda i, j: (i, j)), pl.BlockSpec((HALO, 2 * tc), nxt),
                pl.BlockSpec((ts, tc), lambda i, j: (i, j)), pl.BlockSpec((HALO, tc), nxt),
                pl.BlockSpec((3, 2 * tc), lambda i, j: (0, j)),
                pl.BlockSpec((nj, 2 * tc, D), lambda i, j: (0, 0, 0)),
                pl.BlockSpec((ts, D), row)]
    out_specs = [pl.BlockSpec((ts, 2 * tc), lambda i, j: (i, j)), pl.BlockSpec((ts, D), row),
                 pl.BlockSpec((nj, 3, 2 * tc), lambda i, j: (0, 0, 0))]
    out_shape = [jax.ShapeDtypeStruct((S, F2), BF16), jax.ShapeDtypeStruct((S, D), F32),
                 jax.ShapeDtypeStruct((nj, 3, 2 * tc), F32)]
    args = [up_pre, up, up, dh, dh, fcw, w_up_t3, dz]
    scratch = [pltpu.VMEM((ts, D), F32)]
    if has_rs:
        in_specs += [pl.BlockSpec(memory_space=pl.ANY)] * n_s
        args += list(scatter)
        out_specs += [pl.BlockSpec(memory_space=pl.ANY)] * n_s
        out_shape += [jax.ShapeDtypeStruct((3,) + q.shape[1:], q.dtype) for q in scatter]
        scratch += _rs_sems(n_s)
    res = pl.pallas_call(
        body, name=name, grid=(nt, nj), in_specs=in_specs, out_specs=out_specs, out_shape=out_shape,
        scratch_shapes=scratch, compiler_params=_cp("arbitrary", "arbitrary"),
    )(*args)
    return tuple(res[:3]) + ((list(res[3:]),) if has_rs else ())


def _ln_bwd(dy, xhat, rstd, gamma, *, name, ts=512):
    S, D = dy.shape
    ts = min(ts, S)

    def body(dy_ref, xh_ref, rs_ref, g_ref, dz_ref, dg_ref, db_ref):
        _ln_bwd_tile(dy_ref[...], xh_ref, rs_ref, g_ref, dz_ref, dg_ref, db_ref, pl.program_id(0) == 0)

    cur = lambda i: (i, 0)
    const = lambda i: (0, 0)
    return pl.pallas_call(
        body, name=name, grid=(S // ts,),
        in_specs=[pl.BlockSpec((ts, D), cur), pl.BlockSpec((ts, D), cur), pl.BlockSpec((ts, 1), cur),
                  pl.BlockSpec((1, D), const)],
        out_specs=[pl.BlockSpec((ts, D), cur), pl.BlockSpec((1, D), const), pl.BlockSpec((1, D), const)],
        out_shape=(jax.ShapeDtypeStruct((S, D), F32), jax.ShapeDtypeStruct((1, D), F32),
                   jax.ShapeDtypeStruct((1, D), F32)),
        compiler_params=_cp("arbitrary"),
    )(dy, xhat, rstd, gamma)


def _loss_head(y, tgt, *, name, ts=512):
    S, D = y.shape
    ts = min(ts, S)

    def body(y_ref, t_ref, dy_ref, sq_ref):
        @pl.when(pl.program_id(0) == 0)
        def _():
            sq_ref[...] = jnp.zeros_like(sq_ref)

        e = y_ref[...] - t_ref[...]
        dy_ref[...] = e * (1.0 / D)
        sq_ref[...] += jnp.sum(e * e, axis=0, keepdims=True)

    cur = lambda i: (i, 0)
    return pl.pallas_call(
        body, name=name, grid=(S // ts,),
        in_specs=[pl.BlockSpec((ts, D), cur), pl.BlockSpec((ts, D), cur)],
        out_specs=[pl.BlockSpec((ts, D), cur), pl.BlockSpec((1, D), lambda i: (0, 0))],
        out_shape=(jax.ShapeDtypeStruct((S, D), F32), jax.ShapeDtypeStruct((1, D), F32)),
        compiler_params=_cp("arbitrary"),
    )(y, tgt)


def _adamw(w, g, m, v, *, name, tr=512):
    lead = w.shape[:-2]
    R, C = w.shape[-2:]
    tr = _div_tile(R, tr)
    c1 = 1.0 - ADAM_B1 ** ADAM_STEP
    c2 = 1.0 - ADAM_B2 ** ADAM_STEP

    def body(w_ref, g_ref, m_ref, v_ref, d_ref, mo_ref, vo_ref):
        gv = g_ref[...]
        m2 = ADAM_B1 * m_ref[...] + (1.0 - ADAM_B1) * gv
        v2 = ADAM_B2 * v_ref[...] + (1.0 - ADAM_B2) * (gv * gv)
        m_hat = m2 / c1
        v_hat = v2 / c2
        d_ref[...] = -ADAM_LR * (m_hat / (jnp.sqrt(v_hat) + ADAM_EPS) + ADAM_WD * w_ref[...])
        mo_ref[...] = m2
        vo_ref[...] = v2

    if lead:
        spec = pl.BlockSpec((1, tr, C), lambda l, i: (l, i, 0))
        grid = (lead[0], R // tr)
    else:
        spec = pl.BlockSpec((tr, C), lambda i: (i, 0))
        grid = (R // tr,)
    return pl.pallas_call(
        body, name=name, grid=grid,
        in_specs=[spec] * 4, out_specs=[spec] * 3,
        out_shape=(jax.ShapeDtypeStruct(w.shape, F32),) * 3,
        compiler_params=_cp(*(("parallel",) * len(grid))),
    )(w, g, m, v)


def _interleave_cols(w, nj):
    lead, f2 = w.shape[:-1], w.shape[-1]
    tc = f2 // (2 * nj)
    w = w.reshape(lead + (2, nj, tc))
    return jnp.swapaxes(w, -3, -2).reshape(lead + (f2,))


def _deinterleave_cols(w, nj):
    lead, f2 = w.shape[:-1], w.shape[-1]
    tc = f2 // (2 * nj)
    w = w.reshape(lead + (nj, 2, tc))
    return jnp.swapaxes(w, -3, -2).reshape(lead + (f2,))


def _block_diag(w_pool):
    return jnp.concatenate([jnp.pad(w_pool[g], ((0, 0), (64 * g, 192 - 64 * g))) for g in range(4)], axis=0)


def _pad_rows(v, rows):
    return jnp.pad(v, (0, rows * LANES - v.shape[0])).reshape(rows, LANES)


def kernel(x, positions, w_in, w_pool, pool_scale, attn_sinks, conv_w, w_branch_a, w_branch_b, w_branch_c, w_o, ln1_g, ln1_b, w_up, ffn_conv_w, w_down, ln2_g, ln2_b, loss_target, m_w_in, m_w_pool, m_pool_scale, m_attn_sinks, m_conv_w, m_w_branch_a, m_w_branch_b, m_w_branch_c, m_w_o, m_ln1_g, m_ln1_b, m_w_up, m_ffn_conv_w, m_w_down, m_ln2_g, m_ln2_b, v_w_in, v_w_pool, v_pool_scale, v_attn_sinks, v_conv_w, v_w_branch_a, v_w_branch_b, v_w_branch_c, v_w_o, v_ln1_g, v_ln1_b, v_w_up, v_ffn_conv_w, v_w_down, v_ln2_g, v_ln2_b):
    L, D, in_shard = w_in.shape
    S = x.shape[1]
    IN = in_shard * N_DEV
    F2 = w_up.shape[2] * N_DEV
    F = F2 // 2
    assert D == 1024 and IN == 1792 + 3 * D and x.shape[0] == 1 and S % 512 == 0
    alpha = (2 * L) ** 0.25
    NJ = 2
    TC = F // NJ
    xs = x.reshape(S, D)
    tgt = loss_target.reshape(S, D)

    big = [w_in, w_branch_a, w_branch_b, w_branch_c, w_o, w_up, w_down]
    PART_A, PART_B = (0, 1, 2, 3, 4), (5, 6)
    rows_l = [a.size // L // LANES for a in big]
    offs_l = [sum(rows_l[:k]) for k in range(len(big) + 1)]

    def pack_part(l, part):
        return [big[k][l].astype(BF16) for k in part]

    n_cw, n_fw = conv_w.size, ffn_conv_w.size
    small_rows = -(-(n_cw + n_fw) // LANES)
    small = _pad_rows(jnp.concatenate([conv_w.reshape(-1), ffn_conv_w.reshape(-1)]), small_rows)
    gsmall = _all_gather(small, "ag_conv_weights").reshape(N_DEV, -1)
    conv_full = gsmall[:, :n_cw].reshape(N_DEV, L, 3, -1).transpose(1, 2, 0, 3).reshape(L, 3, 256)
    fcw_full = gsmall[:, n_cw:n_cw + n_fw].reshape(N_DEV, L, 3, -1).transpose(1, 2, 0, 3).reshape(L, 3, F2)
    fcw_full = _interleave_cols(fcw_full, NJ)

    def shard_of(g, part, k, shape):
        assert g[part.index(k)].shape == (N_DEV,) + shape
        return g[part.index(k)]

    def unpack_a(g):
        win = shard_of(g, PART_A, 0, (D, in_shard)).transpose(1, 0, 2).reshape(D, IN)
        wg = win[:, 1792:]
        wp = jnp.concatenate([win[:, 0:256], win[:, 1024:1792]], axis=1)
        wq = win[:, 256:1024]
        if g[1] is None:
            return dict(wg=wg, wp=wp, wq=wq)
        wa = shard_of(g, PART_A, 1, (256, D // N_DEV)).transpose(1, 0, 2).reshape(256, D)
        wb = shard_of(g, PART_A, 2, (512, D // N_DEV)).transpose(1, 0, 2).reshape(512, D)
        wc = shard_of(g, PART_A, 3, (256, D // N_DEV)).transpose(1, 0, 2).reshape(256, D)
        wbr = jnp.concatenate([wa, wb, wc], axis=0)
        wo = shard_of(g, PART_A, 4, (D // N_DEV, D)).reshape(D, D)
        return dict(wg=wg, wp=wp, wq=wq, wg_t=wg.T, wp_t=wp.T, wq_t=wq.T, wbr=wbr, wbr_t=wbr.T, wo=wo, wo_t=wo.T)

    def unpack_b(g):
        nh = N_DEV // (2 * NJ)
        wup = shard_of(g, PART_B, 5, (D, F2 // N_DEV)).reshape(2, NJ, nh, D, F2 // N_DEV)
        wup = wup.transpose(3, 1, 0, 2, 4).reshape(D, F2)
        wdn = shard_of(g, PART_B, 6, (F // N_DEV, D)).reshape(F, D)
        return dict(wup=wup, wup_t=wup.T, wdn=wdn, wdn_t=wdn.T)

    def local_weights(l):
        wbd = _block_diag(w_pool[l]).astype(BF16)
        return dict(wbd=wbd, wbd_t=wbd.T, scale=pool_scale[l].reshape(1, 256), conv=conv_full[l],
                    fcw=fcw_full[l], sinks=jnp.broadcast_to(attn_sinks[l].reshape(8, 1), (8, 128)),
                    g1=ln1_g[l].reshape(1, D), b1=ln1_b[l].reshape(1, D),
                    g2=ln2_g[l].reshape(1, D), b2=ln2_b[l].reshape(1, D))

    inv_freq = ROPE_THETA ** (-jnp.arange(0, ROT_DIM, 2, dtype=F32) / ROT_DIM)
    head_lane = jnp.concatenate([inv_freq, inv_freq, jnp.zeros((HEAD_DIM - ROT_DIM,), F32)])
    head_sign = jnp.concatenate([-jnp.ones((8,), F32), jnp.ones((8,), F32), jnp.zeros((HEAD_DIM - ROT_DIM,), F32)])
    inv_lane = jnp.tile(head_lane, 2).reshape(1, 128)
    sign_lane = jnp.tile(head_sign, 2).reshape(1, 128)
    cos_t, sin_t = _rope_tables(positions.reshape(S, 1), inv_lane, sign_lane, "rope_tables")

    saved, W = [], []
    h_in = xs
    gathered_a = [_all_gather(pack_part(0, PART_A[:1]), "ag_weights_first")]
    for l in range(L):
        if l == 0:
            w_in_only = unpack_a(gathered_a + [None] * 4)
            (pg, pp, pq), later = _mm_fan(h_in, [w_in_only["wg"], w_in_only["wp"], w_in_only["wq"]], out_dtype=BF16,
                                          name="proj_in", gather=pack_part(0, PART_A[1:]) + pack_part(0, PART_B))
            gathered_a, gathered_b = gathered_a + later[:4], later[4:]
        w = {**unpack_a(gathered_a), **unpack_b(gathered_b), **local_weights(l)}
        W.append(w)
        if l > 0:
            pg, pp, pq = _mm_fan(h_in, [w["wg"], w["wp"], w["wq"]], out_dtype=BF16, name="proj_in")
        o_a, o_c = _poolconv_fwd(pp, w["wbd"], w["scale"], w["conv"], name="poolconv_fwd")
        o_b, lse = _attn_fwd(pq, cos_t, sin_t, w["sinks"], name="attn_fwd")
        merged = _merge_fwd(o_a, o_b, o_c, pg, w["wbr"], name="merge_fwd")
        x1, xh1, rs1 = _mm_ln(merged, w["wo"], h_in, w["g1"], w["b1"], alpha=alpha, name="wo_ln1")
        if l + 1 < L:
            up_pre, gathered_a = _mm(x1, w["wup"], out_dtype=BF16, name="ffn_up",
                                     gather=pack_part(l + 1, PART_A))
        else:
            up_pre = _mm(x1, w["wup"], out_dtype=BF16, name="ffn_up")
        down = dict(alpha=alpha, name="ffn_down", tc=TC)
        wdn3 = w["wdn"].reshape(NJ, TC, D)
        if l + 1 < L:
            hact, x2, xh2, rs2, up, gathered_b = _ffn_down_fwd(up_pre, w["fcw"], wdn3, x1, w["g2"], w["b2"],
                                                               gather=pack_part(l + 1, PART_B), **down)
        else:
            hact, x2, xh2, rs2, up = _ffn_down_fwd(up_pre, w["fcw"], wdn3, x1, w["g2"], w["b2"], **down)
        saved.append(dict(up=up,x0=h_in, pg=pg, pp=pp, pq=pq, o_a=o_a, o_b=o_b, o_c=o_c, lse=lse, merged=merged,
                          x1=x1, xh1=xh1, rs1=rs1, up_pre=up_pre, hact=hact, xh2=xh2, rs2=rs2))
        h_in = x2

    dy, sq_lanes = _loss_head(h_in, tgt, name="loss_head")

    def pack_grads(g):
        col = lambda a, n: a.reshape(a.shape[0], N_DEV, n).transpose(1, 0, 2)
        row = lambda a, n: a.reshape(N_DEV, n, a.shape[1])
        nh = N_DEV // (2 * NJ)
        up = g["w_up"].reshape(D, NJ, 2, nh, F2 // N_DEV).transpose(2, 1, 3, 0, 4)
        rest = [col(g["a"], D // N_DEV), col(g["b"], D // N_DEV), col(g["c"], D // N_DEV),
                row(g["w_o"], D // N_DEV), row(g["w_down"], F // N_DEV)]
        return [col(g["w_in"], in_shard).astype(BF16), up.reshape(N_DEV, D, F2 // N_DEV).astype(BF16),
                jnp.concatenate([p.reshape(N_DEV, -1, LANES).astype(BF16) for p in rest], axis=1)]

    my_c = lax.axis_index("c").astype(jnp.int32).reshape(1)
    my_chip = (2 * lax.axis_index("x") + lax.axis_index("y")).astype(jnp.int32).reshape(1)
    gw = [None] * L
    pair_sum = [None] * L
    from_chips = [None] * L
    for l in reversed(range(L)):
        w, sv = W[l], saved[l]
        if l == L - 1:
            dz2, dg2, db2 = _ln_bwd(dy, sv["xh2"], sv["rs2"], w["g2"], name="ln2_bwd")
        else:
            dz2, dg2, db2 = ln2_out
        dw_dn = _mm_tn(sv["hact"], dz2, name="down_bwd_w", tka=TC, na=NJ, tn=D, ts=1024)
        up_bwd = dict(alpha=alpha, name="ffn_up_bwd", tc=TC)
        dh = _mm(dz2, w["wdn_t"], out_dtype=BF16, name="down_bwd_x")
        wup_t3 = w["wup_t"].reshape(NJ, 2 * TC, D)
        if l + 1 < L:
            dpre, dx1, dfcw, from_chips[l + 1] = _ffn_up_bwd(sv["up_pre"], sv["up"], dh, w["fcw"], wup_t3, dz2,
                                                             scatter=pair_sum[l + 1], **up_bwd)
        else:
            dpre, dx1, dfcw = _ffn_up_bwd(sv["up_pre"], sv["up"], dh, w["fcw"], wup_t3, dz2, **up_bwd)
        dfcw = dfcw.transpose(1, 0, 2).reshape(3, F2)
        dw_up = _mm_tn(sv["x1"], dpre, name="up_bwd_w", tka=D, tn=TC, nb=2 * NJ, ts=1024)
        dz1, dg1, db1 = _ln_bwd(dx1, sv["xh1"], sv["rs1"], w["g1"], name="ln1_bwd")
        dmerged = _mm(dz1, w["wo_t"], out_dtype=BF16, name="wo_bwd_x")
        dw_o = _mm_tn(sv["merged"], dz1, name="wo_bwd_w", tka=D, tn=D // 2, nb=2)
        dpg, dprod, do_a, do_b, do_c = _merge_bwd(dmerged, sv["o_a"], sv["o_b"], sv["o_c"], sv["pg"],
                                                  w["wbr"], w["wbr_t"], name="merge_bwd")
        dw_a = _mm_tn(sv["o_a"], dprod, name="branch_a_bwd_w", tka=256, tn=D, b_off=0)
        dw_b = _mm_tn(sv["o_b"], dprod, name="branch_b_bwd_w", tka=512, tn=D, b_off=1)
        dw_c = _mm_tn(sv["o_c"], dprod, name="branch_c_bwd_w", tka=256, tn=D, b_off=2)
        dpq, dsink = _attn_bwd(sv["pq"], cos_t, sin_t, w["sinks"], do_b, sv["o_b"], sv["lse"], name="attn_bwd")
        dpp, pooled, dmixed, dscale, dconv = _poolconv_bwd(sv["pp"], do_a, do_c, w["wbd"], w["wbd_t"], w["scale"],
                                                           w["conv"], name="poolconv_bwd")
        dwbd = _mm_tn(pooled, dmixed, name="pool_bwd_w", tka=256, tn=256)
        dx_args = ([dpg, dpp, dpq], [w["wg_t"], w["wp_t"], w["wq_t"]], dz1)
        if l > 0:
            below = saved[l - 1]
            ln2_out = _mm_sum(*dx_args, add_scale=alpha, name="proj_in_bwd_x",
                              ln=(below["xh2"], below["rs2"], W[l - 1]["g2"]))
        else:
            dx = _mm_sum(*dx_args, add_scale=alpha, name="proj_in_bwd_x")
        dw_g = _mm_tn(sv["x0"], dpg, name="proj_gate_bwd_w", tka=D, tn=512, nb=6)
        dw_p = _mm_tn(sv["x0"], dpp, name="proj_poolconv_bwd_w", tka=D, tn=512, nb=2)
        dw_q = _mm_tn(sv["x0"], dpq, name="proj_qkv_bwd_w", tka=D, tn=384, nb=2)
        dw_in = jnp.concatenate([dw_p[:, 0:256], dw_q, dw_p[:, 256:1024], dw_g], axis=1)
        dw_pool = jnp.stack([dwbd[64 * g:64 * (g + 1), 64 * g:64 * (g + 1)] for g in range(4)])
        gw[l] = dict(w_in=dw_in, a=dw_a, b=dw_b, c=dw_c, w_o=dw_o, w_up=dw_up, w_down=dw_dn,
                     w_pool=dw_pool, scale=dscale, sinks=dsink[:, 0], conv=dconv, fcw=_deinterleave_cols(dfcw, NJ),
                     g1=dg1, b1=db1, g2=dg2, b2=db2)
        p_l = pack_grads(gw[l])
        from_sibling = _rs_sibling(p_l, "rs_sibling")
        pair_sum[l] = [_sum_sibling(p, r, my_c, "rs_sum_sibling") for p, r in zip(p_l, from_sibling)]
    grad_x = dx.reshape(1, S, D)
    from_chips[0] = _rs_chips(pair_sum[0], "rs_chips_last")
    g_layers = [[_sum_chips(q, r, my_chip, "rs_sum_chips") for q, r in zip(pair_sum[l], from_chips[l])]
                for l in range(L)]

    def stack(k):
        return jnp.stack([gw[l][k] for l in range(L)])

    rep_vec = jnp.concatenate([
        stack("w_pool").reshape(-1), stack("scale").reshape(-1), stack("g1").reshape(-1), stack("b1").reshape(-1),
        stack("g2").reshape(-1), stack("b2").reshape(-1)])
    n_rep_full = -(-rep_vec.shape[0] // LANES)
    sinks_row = jnp.pad(stack("sinks").reshape(-1), (0, LANES - 8 * L))
    rep_vec = jnp.concatenate([_pad_rows(rep_vec, n_rep_full).reshape(-1), sinks_row, sq_lanes.reshape(-1)])
    loss_row = n_rep_full + 1
    n_rep = -(-(loss_row + 1) // 8) * 8
    rep_rows = _pad_rows(rep_vec, n_rep)
    dconv_by_dev = stack("conv").reshape(L, 3, N_DEV, -1).transpose(2, 0, 1, 3).reshape(N_DEV, -1)
    dfcw_by_dev = stack("fcw").reshape(L, 3, N_DEV, -1).transpose(2, 0, 1, 3).reshape(N_DEV, -1)
    n_mine = -(-(small_rows) // 8) * 8
    by_dev = jnp.concatenate([dconv_by_dev, dfcw_by_dev], axis=1)
    by_dev = jnp.pad(by_dev, ((0, 0), (0, n_mine * LANES - by_dev.shape[1]))).reshape(N_DEV * n_mine, LANES)
    small_g = _all_gather(jnp.concatenate([rep_rows, by_dev], axis=0), "ag_small_grads")
    rep_sum, mine_sum, loss11 = _small_reduce(small_g, n_rep, n_mine, 1.0 / D, loss_row, "small_reduce")
    loss = loss11[0, 0]

    names_big = ["w_in", "w_branch_a", "w_branch_b", "w_branch_c", "w_o", "w_up", "w_down"]
    ms_big = [m_w_in, m_w_branch_a, m_w_branch_b, m_w_branch_c, m_w_o, m_w_up, m_w_down]
    vs_big = [v_w_in, v_w_branch_a, v_w_branch_b, v_w_branch_c, v_w_o, v_w_up, v_w_down]
    out = {}
    for k, name in enumerate(names_big):
        wk = big[k]
        if k in (0, 5):
            g_nat = jnp.stack([g[0 if k == 0 else 1] for g in g_layers])
        else:
            rest_ks = (1, 2, 3, 4, 6)
            o = sum(rows_l[q] for q in rest_ks[:rest_ks.index(k)])
            g_nat = jnp.concatenate([g[2][o:o + rows_l[k]] for g in g_layers], axis=0).reshape(wk.shape)
        d, mo, vo = _adamw(wk, g_nat, ms_big[k], vs_big[k], name="adamw_" + name)
        out[name] = (g_nat, d, mo, vo)

    def rep_pack(wp_, sc_, g1_, b1_, g2_, b2_, sk_):
        v = jnp.concatenate([wp_.reshape(-1), sc_.reshape(-1), g1_.reshape(-1), b1_.reshape(-1), g2_.reshape(-1),
                             b2_.reshape(-1)])
        return _pad_rows(jnp.concatenate([_pad_rows(v, n_rep_full).reshape(-1), sk_.reshape(-1)]), n_rep)

    def mine_pack(cw_, fw_):
        return _pad_rows(jnp.concatenate([cw_.reshape(-1), fw_.reshape(-1)]), n_mine)

    w_rep = rep_pack(w_pool, pool_scale, ln1_g, ln1_b, ln2_g, ln2_b, attn_sinks)
    m_rep = rep_pack(m_w_pool, m_pool_scale, m_ln1_g, m_ln1_b, m_ln2_g, m_ln2_b, m_attn_sinks)
    v_rep = rep_pack(v_w_pool, v_pool_scale, v_ln1_g, v_ln1_b, v_ln2_g, v_ln2_b, v_attn_sinks)
    g_rep = jnp.concatenate([rep_sum[:loss_row], jnp.zeros((n_rep - loss_row, LANES), F32)], axis=0)
    rep_res = (g_rep,) + tuple(_adamw(w_rep, g_rep, m_rep, v_rep, name="adamw_replicated"))
    w_mine = mine_pack(conv_w, ffn_conv_w)
    mine_res = (mine_sum,) + tuple(_adamw(w_mine, mine_sum, mine_pack(m_conv_w, m_ffn_conv_w),
                                          mine_pack(v_conv_w, v_ffn_conv_w), name="adamw_conv"))

    def rep_unpack(buf):
        flat = buf.reshape(-1)
        res, o = {}, 0
        for nm, ref in (("w_pool", w_pool), ("pool_scale", pool_scale), ("ln1_g", ln1_g), ("ln1_b", ln1_b),
                        ("ln2_g", ln2_g), ("ln2_b", ln2_b)):
            res[nm] = flat[o:o + ref.size].reshape(ref.shape)
            o += ref.size
        o = n_rep_full * LANES
        res["attn_sinks"] = flat[o:o + attn_sinks.size].reshape(attn_sinks.shape)
        return res

    def mine_unpack(buf):
        flat = buf.reshape(-1)
        return {"conv_w": flat[:n_cw].reshape(conv_w.shape),
                "ffn_conv_w": flat[n_cw:n_cw + n_fw].reshape(ffn_conv_w.shape)}

    order = ["w_in", "w_pool", "pool_scale", "attn_sinks", "conv_w", "w_branch_a", "w_branch_b", "w_branch_c", "w_o",
             "ln1_g", "ln1_b", "w_up", "ffn_conv_w", "w_down", "ln2_g", "ln2_b"]
    results = [loss, grad_x]
    for kind in range(4):
        rep_k, mine_k = rep_unpack(rep_res[kind]), mine_unpack(mine_res[kind])
        for nm in order:
            if nm in out:
                results.append(out[nm][kind])
            elif nm in rep_k:
                results.append(rep_k[nm])
            else:
                results.append(mine_k[nm])
    return tuple(results)
```

```python
import functools

import jax
import jax.numpy as jnp
from jax import lax
from jax.experimental import pallas as pl
from jax.experimental.pallas import tpu as pltpu

F32 = jnp.float32
BF16 = jnp.bfloat16

HEAD_DIM = 64
N_Q_HEADS = 8
GROUP = 4
WINDOW = 128
ROT_DIM = 16
ROPE_THETA = 500000.0
POOL_WINDOWS = (2, 4, 8, 16)
LN_EPS = 1e-5
MASK_VALUE = -1e30
ADAM_LR, ADAM_B1, ADAM_B2, ADAM_EPS, ADAM_WD, ADAM_STEP = 0.001, 0.9, 0.999, 1e-08, 0.01, 10

N_DEV = 8
LANES = 1024
HALO = 16
MESH = pl.DeviceIdType.MESH
VMEM_LIMIT = 56 * 1024 * 1024


def _div_tile(n, want, mult=8):
    for t in range(min(want, n) // mult * mult, 0, -mult):
        if n % t == 0:
            return t
    return n


def _cp(*sem):
    return pltpu.CompilerParams(dimension_semantics=sem, vmem_limit_bytes=VMEM_LIMIT)


def _coords():
    return lax.axis_index("x"), lax.axis_index("y"), lax.axis_index("c")


def _all_gather(xs, name):
    xs = list(xs) if isinstance(xs, (list, tuple)) else [xs]
    n = len(xs)

    def body(*refs):
        ag_refs = (refs[:n], refs[n:2 * n]) + tuple(refs[2 * n:])
        _ag_start(*ag_refs)
        _ag_finish(*ag_refs)

    res = pl.pallas_call(
        body, name=name,
        out_shape=[jax.ShapeDtypeStruct((N_DEV,) + a.shape, a.dtype) for a in xs],
        in_specs=[pl.BlockSpec(memory_space=pl.ANY)] * n,
        out_specs=[pl.BlockSpec(memory_space=pl.ANY)] * n,
        scratch_shapes=_ag_sems(n),
    )(*xs)
    return res if n > 1 else res[0]


def _ag_sems(n):
    return [pltpu.SemaphoreType.DMA((7 * n,)), pltpu.SemaphoreType.DMA((7 * n,)), pltpu.SemaphoreType.DMA((n,))]


def _ag_copies(x_refs, out_refs, send_sems, recv_sems, local_sems):
    x, y, c = _coords()
    me, sibling = (x, y, c), (x, y, 1 - c)
    chips = [(1 - x, y), (x, 1 - y), (1 - x, 1 - y)]
    per_array = []
    for a, (x_ref, out_ref) in enumerate(zip(x_refs, out_refs)):
        def slot(px, py, pc, out_ref=out_ref):
            return out_ref.at[4 * px + 2 * py + pc]

        def copy(k, block, to, src=None, a=a, slot=slot):
            return pltpu.make_async_remote_copy(
                src_ref=slot(*block) if src is None else src, dst_ref=slot(*block),
                send_sem=send_sems.at[7 * a + k], recv_sem=recv_sems.at[7 * a + k],
                device_id=to, device_id_type=MESH)

        mine = pltpu.make_async_copy(x_ref, slot(*me), local_sems.at[a])
        first = [copy(0, me, sibling, src=x_ref)]
        first += [copy(1 + j, me, (*chip, c), src=x_ref) for j, chip in enumerate(chips)]
        passed = [copy(4 + j, (*chip, c), sibling) for j, chip in enumerate(chips)]
        from_chips = [copy(1 + j, (*chip, c), me) for j, chip in enumerate(chips)]
        from_sibling = [copy(0, sibling, me)] + [copy(4 + j, (*chip, 1 - c), me) for j, chip in enumerate(chips)]
        per_array.append((mine, first, passed, from_chips, from_sibling))
    return per_array


def _ag_start(*refs):
    for mine, first, _, _, _ in _ag_copies(*refs):
        mine.start()
        for cp in first:
            cp.start()


def _ag_finish(*refs):
    per_array = _ag_copies(*refs)
    for j in range(3):
        for _, _, passed, from_chips, _ in per_array:
            from_chips[j].wait_recv()
            passed[j].start()
    for mine, first, passed, _, from_sibling in per_array:
        for cp in from_sibling:
            cp.wait_recv()
        for cp in first + passed:
            cp.wait_send()
        mine.wait()


def _rs_sibling(ps, name):
    n = len(ps)

    def body(*refs):
        p_refs, out_refs, send_sems, recv_sems = refs[:n], refs[n:2 * n], refs[2 * n], refs[2 * n + 1]
        x, y, c = _coords()
        copies = []
        for a in range(n):
            for j in range(4):
                cx, cy = j // 2, j % 2
                copies.append(pltpu.make_async_remote_copy(
                    src_ref=p_refs[a].at[4 * cx + 2 * cy + (1 - c)], dst_ref=out_refs[a].at[j],
                    send_sem=send_sems.at[4 * a + j], recv_sem=recv_sems.at[4 * a + j],
                    device_id=(x, y, 1 - c), device_id_type=MESH))
        for cp in copies:
            cp.start()
        for cp in copies:
            cp.wait_recv()
        for cp in copies:
            cp.wait_send()

    return pl.pallas_call(
        body, name=name,
        out_shape=[jax.ShapeDtypeStruct((4,) + p.shape[1:], p.dtype) for p in ps],
        in_specs=[pl.BlockSpec(memory_space=pl.ANY)] * n,
        out_specs=[pl.BlockSpec(memory_space=pl.ANY)] * n,
        scratch_shapes=[pltpu.SemaphoreType.DMA((4 * n,)), pltpu.SemaphoreType.DMA((4 * n,))],
    )(*ps)


def _rs_chips(qs, name):
    n = len(qs)

    def body(*refs):
        rs_refs = (refs[:n], refs[n:2 * n], refs[2 * n], refs[2 * n + 1])
        _rs_chips_start(*rs_refs)
        _rs_chips_finish(*rs_refs)

    return pl.pallas_call(
        body, name=name,
        out_shape=[jax.ShapeDtypeStruct((3,) + q.shape[1:], q.dtype) for q in qs],
        in_specs=[pl.BlockSpec(memory_space=pl.ANY)] * n,
        out_specs=[pl.BlockSpec(memory_space=pl.ANY)] * n,
        scratch_shapes=_rs_sems(n),
    )(*qs)


def _rs_sems(n):
    return [pltpu.SemaphoreType.DMA((3 * n,)), pltpu.SemaphoreType.DMA((3 * n,))]


def _rs_chips_copies(q_refs, out_refs, send_sems, recv_sems):
    x, y, c = _coords()
    chips = [(1 - x, y), (x, 1 - y), (1 - x, 1 - y)]
    return [pltpu.make_async_remote_copy(
        src_ref=q_ref.at[2 * cx + cy], dst_ref=out_ref.at[k],
        send_sem=send_sems.at[3 * a + k], recv_sem=recv_sems.at[3 * a + k], device_id=(cx, cy, c),
        device_id_type=MESH)
        for a, (q_ref, out_ref) in enumerate(zip(q_refs, out_refs)) for k, (cx, cy) in enumerate(chips)]


def _rs_chips_start(*refs):
    for cp in _rs_chips_copies(*refs):
        cp.start()


def _rs_chips_finish(*refs):
    copies = _rs_chips_copies(*refs)
    for cp in copies:
        cp.wait_recv()
    for cp in copies:
        cp.wait_send()


def _sum_sibling(p, recv, my_c, name, tr=512):
    _, R, C = p.shape
    tr = _div_tile(R, tr, 16)

    def body(c_ref, p_ref, r_ref, o_ref):
        o_ref[...] = (p_ref[...].astype(F32) + r_ref[...].astype(F32)).astype(o_ref.dtype)

    grid_spec = pltpu.PrefetchScalarGridSpec(
        num_scalar_prefetch=1, grid=(4, R // tr),
        in_specs=[pl.BlockSpec((1, tr, C), lambda j, r, c_ref: (4 * (j // 2) + 2 * (j % 2) + c_ref[0], r, 0)),
                  pl.BlockSpec((1, tr, C), lambda j, r, c_ref: (j, r, 0))],
        out_specs=pl.BlockSpec((1, tr, C), lambda j, r, c_ref: (j, r, 0)))
    return pl.pallas_call(body, name=name, grid_spec=grid_spec,
                          out_shape=jax.ShapeDtypeStruct((4, R, C), p.dtype),
                          compiler_params=_cp("parallel", "parallel"))(my_c, p, recv)


def _sum_chips(q, recv, my_chip, name, tr=512):
    _, R, C = q.shape
    tr = _div_tile(R, tr, 16)

    def body(i_ref, q_ref, r_ref, o_ref):
        acc = q_ref[0].astype(F32)
        for k in range(3):
            acc = acc + r_ref[k].astype(F32)
        o_ref[...] = acc

    grid_spec = pltpu.PrefetchScalarGridSpec(
        num_scalar_prefetch=1, grid=(R // tr,),
        in_specs=[pl.BlockSpec((1, tr, C), lambda r, i_ref: (i_ref[0], r, 0)),
                  pl.BlockSpec((3, tr, C), lambda r, i_ref: (0, r, 0))],
        out_specs=pl.BlockSpec((tr, C), lambda r, i_ref: (r, 0)))
    return pl.pallas_call(body, name=name, grid_spec=grid_spec,
                          out_shape=jax.ShapeDtypeStruct((R, C), F32),
                          compiler_params=_cp("parallel"))(my_chip, q, recv)


def _small_reduce(g, n_rep, n_mine, inv_d, loss_row, name):
    _, R, C = g.shape

    def body(g_ref, rep_ref, mine_ref, loss_ref):
        x, y, c = _coords()
        start = pl.multiple_of(n_rep + (4 * x + 2 * y + c) * n_mine, 8)
        rep = g_ref[0, 0:n_rep, :]
        mine = g_ref[0, pl.ds(start, n_mine), :]
        sq = g_ref[0, loss_row:loss_row + 1, :]
        for d in range(1, N_DEV):
            rep = rep + g_ref[d, 0:n_rep, :]
            mine = mine + g_ref[d, pl.ds(start, n_mine), :]
            sq = sq + g_ref[d, loss_row:loss_row + 1, :]
        rep_ref[...] = rep
        mine_ref[...] = mine
        loss_ref[...] = (0.5 * inv_d) * jnp.sum(sq, axis=1, keepdims=True)

    return pl.pallas_call(
        body, name=name,
        out_shape=(jax.ShapeDtypeStruct((n_rep, C), F32), jax.ShapeDtypeStruct((n_mine, C), F32),
                   jax.ShapeDtypeStruct((1, 1), F32)),
        compiler_params=pltpu.CompilerParams(vmem_limit_bytes=VMEM_LIMIT),
    )(g)


def _mm(a, b, *, out_dtype, name, tm=512, tn=None, tk=None, add=None, add_scale=1.0, gather=None):
    M, K = a.shape
    N = b.shape[1]
    tm = min(tm, M)
    tn = N if tn is None else tn
    tk = K if tk is None else tk
    nk = K // tk
    has_add = add is not None
    has_ag = gather is not None
    n_g = len(gather) if has_ag else 0
    n_i, n_j = M // tm, N // tn

    def body(*refs):
        a_ref, b_ref = refs[0], refs[1]
        add_ref = refs[2] if has_add else None
        n_in = 2 + has_add + n_g
        o_ref = refs[n_in]
        if has_ag:
            ag_refs = (refs[n_in - n_g:n_in], refs[n_in + 1:n_in + 1 + n_g]) + tuple(
                refs[n_in + 1 + n_g:n_in + 4 + n_g])
            pid = (pl.program_id(0), pl.program_id(1), pl.program_id(2))

            @pl.when((pid[0] == 0) & (pid[1] == 0) & (pid[2] == 0))
            def _():
                _ag_start(*ag_refs)

        part = jnp.dot(a_ref[...].astype(BF16), b_ref[...].astype(BF16), preferred_element_type=F32)

        def finish(r):
            if has_add:
                r = r + add_scale * add_ref[...].astype(F32)
            o_ref[...] = r.astype(out_dtype)

        if nk == 1:
            finish(part)
        else:
            acc_ref = refs[-1]
            k = pl.program_id(2)

            @pl.when(k == 0)
            def _():
                acc_ref[...] = part

            @pl.when(k > 0)
            def _():
                acc_ref[...] += part

            @pl.when(k == nk - 1)
            def _():
                finish(acc_ref[...])

        if has_ag:
            @pl.when((pid[0] == n_i - 1) & (pid[1] == n_j - 1) & (pid[2] == nk - 1))
            def _():
                _ag_finish(*ag_refs)

    b_mode = dict(pipeline_mode=pl.Buffered(1)) if (n_j == 1 and nk == 1) else {}
    in_specs = [pl.BlockSpec((tm, tk), lambda i, j, k: (i, k)),
                pl.BlockSpec((tk, tn), lambda i, j, k: (k, j), **b_mode)]
    args = [a, b]
    if has_add:
        in_specs.append(pl.BlockSpec((tm, tn), lambda i, j, k: (i, j)))
        args.append(add)
    out_specs = [pl.BlockSpec((tm, tn), lambda i, j, k: (i, j))]
    out_shape = [jax.ShapeDtypeStruct((M, N), out_dtype)]
    scratch = []
    if has_ag:
        in_specs += [pl.BlockSpec(memory_space=pl.ANY)] * n_g
        args += list(gather)
        out_specs += [pl.BlockSpec(memory_space=pl.ANY)] * n_g
        out_shape += [jax.ShapeDtypeStruct((N_DEV,) + g.shape, g.dtype) for g in gather]
        scratch += _ag_sems(n_g)
    if nk > 1:
        scratch.append(pltpu.VMEM((tm, tn), F32))
    sem = ("arbitrary",) * 3 if has_ag else ("parallel", "parallel", "arbitrary")
    res = pl.pallas_call(
        body, name=name, grid=(n_i, n_j, nk), in_specs=in_specs, out_specs=out_specs, out_shape=out_shape,
        scratch_shapes=scratch, compiler_params=_cp(*sem),
    )(*args)
    return (res[0], list(res[1:])) if has_ag else res[0]


def _mm_fan(a, bs, *, out_dtype, name, tm=512, gather=None):
    M, K = a.shape
    tm = min(tm, M)
    n = len(bs)
    n_i = M // tm
    n_g = len(gather) if gather is not None else 0

    def body(*refs):
        outs = refs[1 + n + n_g:1 + 2 * n + n_g]
        if n_g:
            ag_refs = (refs[1 + n:1 + n + n_g], refs[1 + 2 * n + n_g:1 + 2 * n + 2 * n_g]) + tuple(
                refs[1 + 2 * n + 2 * n_g:])

            @pl.when(pl.program_id(0) == 0)
            def _():
                _ag_start(*ag_refs)

        a_v = refs[0][...].astype(BF16)
        for k in range(n):
            outs[k][...] = jnp.dot(a_v, refs[1 + k][...].astype(BF16), preferred_element_type=F32).astype(out_dtype)

        if n_g:
            @pl.when(pl.program_id(0) == n_i - 1)
            def _():
                _ag_finish(*ag_refs)

    row = lambda i: (i, 0)
    hbm = pl.BlockSpec(memory_space=pl.ANY)
    res = pl.pallas_call(
        body, name=name, grid=(n_i,),
        in_specs=[pl.BlockSpec((tm, K), row)] + [pl.BlockSpec(b.shape, lambda i: (0, 0)) for b in bs] + [hbm] * n_g,
        out_specs=[pl.BlockSpec((tm, b.shape[1]), row) for b in bs] + [hbm] * n_g,
        out_shape=([jax.ShapeDtypeStruct((M, b.shape[1]), out_dtype) for b in bs]
                   + [jax.ShapeDtypeStruct((N_DEV,) + g.shape, g.dtype) for g in (gather or [])]),
        scratch_shapes=_ag_sems(n_g) if n_g else [],
        compiler_params=_cp("arbitrary" if n_g else "parallel"),
    )(a, *bs, *(gather or []))
    return (list(res[:n]), list(res[n:])) if n_g else list(res)


def _mm_sum(xs, bs, add, *, add_scale, name, tm=512, ln=None):
    M = xs[0].shape[0]
    N = bs[0].shape[1]
    tm = min(tm, M)
    n = len(xs)

    def body(*refs):
        acc = add_scale * refs[2 * n][...]
        for k in range(n):
            acc = acc + jnp.dot(refs[k][...].astype(BF16), refs[n + k][...].astype(BF16), preferred_element_type=F32)
        if ln is None:
            refs[2 * n + 1][...] = acc
        else:
            xh_ref, rs_ref, g_ref, dz_ref, dg_ref, db_ref = refs[2 * n + 1:]
            _ln_bwd_tile(acc, xh_ref, rs_ref, g_ref, dz_ref, dg_ref, db_ref, pl.program_id(0) == 0)

    row = lambda i: (i, 0)
    vec = lambda i: (0, 0)
    in_specs = ([pl.BlockSpec((tm, x.shape[1]), row) for x in xs]
                + [pl.BlockSpec(b.shape, vec) for b in bs] + [pl.BlockSpec((tm, N), row)])
    if ln is None:
        return pl.pallas_call(
            body, name=name, grid=(M // tm,), in_specs=in_specs, out_specs=pl.BlockSpec((tm, N), row),
            out_shape=jax.ShapeDtypeStruct((M, N), F32), compiler_params=_cp("parallel"),
        )(*xs, *bs, add)
    in_specs += [pl.BlockSpec((tm, N), row), pl.BlockSpec((tm, 1), row), pl.BlockSpec((1, N), vec)]
    return pl.pallas_call(
        body, name=name, grid=(M // tm,), in_specs=in_specs,
        out_specs=[pl.BlockSpec((tm, N), row), pl.BlockSpec((1, N), vec), pl.BlockSpec((1, N), vec)],
        out_shape=(jax.ShapeDtypeStruct((M, N), F32), jax.ShapeDtypeStruct((1, N), F32),
                   jax.ShapeDtypeStruct((1, N), F32)),
        compiler_params=_cp("arbitrary"),
    )(*xs, *bs, add, *ln)


def _ln_bwd_tile(dyv, xh_ref, rs_ref, g_ref, dz_ref, dg_ref, db_ref, first):
    @pl.when(first)
    def _():
        dg_ref[...] = jnp.zeros_like(dg_ref)
        db_ref[...] = jnp.zeros_like(db_ref)

    xh = xh_ref[...].astype(F32)
    dyg = dyv * g_ref[...]
    c1 = jnp.mean(dyg, axis=-1, keepdims=True)
    c2 = jnp.mean(dyg * xh, axis=-1, keepdims=True)
    dz_ref[...] = rs_ref[...] * (dyg - c1 - xh * c2)
    dg_ref[...] += jnp.sum(dyv * xh, axis=0, keepdims=True)
    db_ref[...] += jnp.sum(dyv, axis=0, keepdims=True)


def _mm_ln(a, b, resid, gamma, beta, *, alpha, name, tm=512, tk=None):
    M, K = a.shape
    D = b.shape[1]
    tm = min(tm, M)
    tk = K if tk is None else tk
    nk = K // tk

    def body(a_ref, b_ref, r_ref, g_ref, be_ref, y_ref, xh_ref, rs_ref, *scratch):
        part = jnp.dot(a_ref[...].astype(BF16), b_ref[...].astype(BF16), preferred_element_type=F32)

        def finish(acc):
            z = alpha * r_ref[...] + acc
            mu = jnp.mean(z, axis=-1, keepdims=True)
            zc = z - mu
            var = jnp.mean(zc * zc, axis=-1, keepdims=True)
            rstd = lax.rsqrt(var + LN_EPS)
            xhat = zc * rstd
            y_ref[...] = xhat * g_ref[...] + be_ref[...]
            xh_ref[...] = xhat.astype(BF16)
            rs_ref[...] = rstd

        if nk == 1:
            finish(part)
        else:
            acc_ref = scratch[0]
            k = pl.program_id(1)

            @pl.when(k == 0)
            def _():
                acc_ref[...] = part

            @pl.when(k > 0)
            def _():
                acc_ref[...] += part

            @pl.when(k == nk - 1)
            def _():
                finish(acc_ref[...])

    row = lambda i, k: (i, 0)
    vec = lambda i, k: (0, 0)
    return pl.pallas_call(
        body, name=name, grid=(M // tm, nk),
        in_specs=[pl.BlockSpec((tm, tk), lambda i, k: (i, k)), pl.BlockSpec((tk, D), lambda i, k: (k, 0)),
                  pl.BlockSpec((tm, D), row), pl.BlockSpec((1, D), vec), pl.BlockSpec((1, D), vec)],
        out_specs=[pl.BlockSpec((tm, D), row), pl.BlockSpec((tm, D), row), pl.BlockSpec((tm, 1), row)],
        out_shape=(jax.ShapeDtypeStruct((M, D), F32), jax.ShapeDtypeStruct((M, D), BF16),
                   jax.ShapeDtypeStruct((M, 1), F32)),
        scratch_shapes=[pltpu.VMEM((tm, D), F32)] if nk > 1 else [],
        compiler_params=_cp("parallel", "arbitrary"),
    )(a, b, resid, gamma, beta)


def _mm_tn(a, b, *, name, tka, tn, a_off=0, na=1, b_off=0, nb=1, ts=2048):
    S = a.shape[0]
    ts = min(ts, S)

    def body(a_ref, b_ref, o_ref):
        s = pl.program_id(2)
        part = lax.dot_general(a_ref[...].astype(BF16), b_ref[...].astype(BF16),
                               (((0,), (0,)), ((), ())), preferred_element_type=F32)

        @pl.when(s == 0)
        def _():
            o_ref[...] = part

        @pl.when(s > 0)
        def _():
            o_ref[...] += part

    return pl.pallas_call(
        body, name=name, grid=(na, nb, S // ts),
        in_specs=[pl.BlockSpec((ts, tka), lambda i, j, s: (s, a_off + i)),
                  pl.BlockSpec((ts, tn), lambda i, j, s: (s, b_off + j))],
        out_specs=pl.BlockSpec((tka, tn), lambda i, j, s: (i, j)),
        out_shape=jax.ShapeDtypeStruct((na * tka, nb * tn), F32),
        compiler_params=_cp("parallel", "parallel", "arbitrary"),
    )(a, b)


def _rope_tables(pos, inv_lane, sign_lane, name, ts=512):
    S = pos.shape[0]
    ts = min(ts, S)

    def body(p_ref, inv_ref, sg_ref, cos_ref, sin_ref):
        ang = p_ref[...].astype(F32) * inv_ref[...]
        cos_ref[...] = jnp.cos(ang)
        sin_ref[...] = jnp.sin(ang) * sg_ref[...]

    return pl.pallas_call(
        body, name=name, grid=(S // ts,),
        in_specs=[pl.BlockSpec((ts, 1), lambda i: (i, 0)), pl.BlockSpec((1, 128), lambda i: (0, 0)),
                  pl.BlockSpec((1, 128), lambda i: (0, 0))],
        out_specs=[pl.BlockSpec((ts, 128), lambda i: (i, 0))] * 2,
        out_shape=(jax.ShapeDtypeStruct((S, 128), F32),) * 2,
        compiler_params=_cp("parallel"),
    )(pos, inv_lane, sign_lane)


def _rope_swap(t):
    lane = lax.broadcasted_iota(jnp.int32, (1, 128), 1)
    lo = (lane % HEAD_DIM) < (ROT_DIM // 2)
    return jnp.where(lo, pltpu.roll(t, 128 - ROT_DIM // 2, 1), pltpu.roll(t, ROT_DIM // 2, 1))


def _rope_fwd(t, cos, sin):
    return t * cos + _rope_swap(t) * sin


def _rope_bwd(d, cos, sin):
    lane = lax.broadcasted_iota(jnp.int32, (1, 128), 1)
    return d * cos + jnp.where((lane % HEAD_DIM) < ROT_DIM, _rope_swap(d * sin), 0.0)


def _tile_heads(t):
    lane = lax.broadcasted_iota(jnp.int32, (1, 128), 1)
    r = pltpu.roll(t, 64, 1)
    h0 = jnp.where(lane < 64, t, r)
    h1 = jnp.where(lane < 64, r, t)
    return jnp.concatenate([h0, h0], axis=1), jnp.concatenate([h1, h1], axis=1)


def _fold_heads(d0, d1):
    lane = lax.broadcasted_iota(jnp.int32, (1, 128), 1)

    def fold(d):
        s = d[:, 0:128] + d[:, 128:256]
        return s + pltpu.roll(s, 64, 1)

    return jnp.where(lane < 64, fold(d0), fold(d1))


def _band4(n_keys):
    row = lax.broadcasted_iota(jnp.int32, (GROUP * WINDOW, n_keys), 0) % WINDOW
    col = lax.broadcasted_iota(jnp.int32, (GROUP * WINDOW, n_keys), 1)
    return (col > row) & (col <= row + WINDOW), col


def _head_masks():
    lane = lax.broadcasted_iota(jnp.int32, (1, GROUP * HEAD_DIM), 1)
    return [(lane // HEAD_DIM) == hl for hl in range(GROUP)]


def _stack_heads(t):
    zero = jnp.zeros_like(t)
    return jnp.concatenate([jnp.where(hm, t, zero) for hm in _head_masks()], axis=0)


def _unstack_heads(t4):
    out = None
    for hl, hm in enumerate(_head_masks()):
        part = jnp.where(hm, t4[hl * WINDOW:(hl + 1) * WINDOW], 0.0)
        out = part if out is None else out + part
    return out


def _sink_block(sink_ref, g):
    return jnp.concatenate([jnp.broadcast_to(sink_ref[g * GROUP + hl:g * GROUP + hl + 1, 0:1], (WINDOW, 256))
                            for hl in range(GROUP)], axis=0)


def _sink_column(sink_ref, g):
    return jnp.concatenate([jnp.broadcast_to(sink_ref[g * GROUP + hl:g * GROUP + hl + 1, 0:1], (WINDOW, 1))
                            for hl in range(GROUP)], axis=0)


def _attn_fwd(pq, cos_t, sin_t, sinks_b, *, name, ts=256):
    S = pq.shape[0]
    ts = min(ts, S)
    nq = ts // WINDOW
    scale = HEAD_DIM ** -0.5

    def body(cur_ref, prev_ref, cosc_ref, sinc_ref, cosp_ref, sinp_ref, sink_ref, o_ref, lse_ref):
        i = pl.program_id(0)
        cosc, sinc = cosc_ref[...], sinc_ref[...]
        q = cur_ref[:, 0:512].astype(F32)
        qr = jnp.concatenate(
            [_rope_fwd(q[:, j * 128:(j + 1) * 128], cosc, sinc) for j in range(4)], axis=1) * scale
        qr = qr.astype(BF16)
        kc = _rope_fwd(cur_ref[:, 512:640].astype(F32), cosc, sinc)
        kp = _rope_fwd(prev_ref[:, 0:128].astype(F32), cosp_ref[...], sinp_ref[...])
        k_all = jnp.concatenate([kp, kc], axis=0)
        v_all = jnp.concatenate([prev_ref[:, 128:256].astype(F32), cur_ref[:, 640:768].astype(F32)], axis=0)
        kt = [t.astype(BF16) for t in _tile_heads(k_all)]
        vt = [t.astype(BF16) for t in _tile_heads(v_all)]
        band, col = _band4(2 * WINDOW)
        ones = jnp.ones((2 * WINDOW, 256), BF16)
        key_t = lax.broadcasted_iota(jnp.int32, (2 * WINDOW, GROUP * WINDOW), 0)
        qry_t = lax.broadcasted_iota(jnp.int32, (2 * WINDOW, GROUP * WINDOW), 1) % WINDOW
        band_t = (key_t > qry_t) & (key_t <= qry_t + WINDOW)
        NT = (((1,), (1,)), ((), ()))
        for qb in range(nq):
            rows = slice(qb * WINDOW, (qb + 1) * WINDOW)
            keys = slice(qb * WINDOW, (qb + 2) * WINDOW)
            valid = band & ((col >= WINDOW) | (i * nq + qb > 0))
            valid_t = band_t & ((key_t >= WINDOW) | (i * nq + qb > 0))
            for g in range(2):
                qs = _stack_heads(qr[rows, g * 256:(g + 1) * 256])
                sink = _sink_block(sink_ref, g)
                s = lax.dot_general(qs, kt[g][keys], NT, preferred_element_type=F32)
                s_t = lax.dot_general(kt[g][keys], qs, NT, preferred_element_type=F32)
                m_t = jnp.max(jnp.where(valid_t, s_t, MASK_VALUE), axis=0, keepdims=True)
                m_rep = jnp.broadcast_to(m_t, (WINDOW, GROUP * WINDOW)).T
                m = jnp.maximum(jnp.concatenate([m_rep, m_rep], axis=1), sink)
                e = jnp.exp(jnp.where(valid, s, MASK_VALUE) - m).astype(BF16)
                l = jnp.dot(e, ones, preferred_element_type=F32) + jnp.exp(sink - m)
                pv = jnp.dot(e, vt[g][keys], preferred_element_type=F32)
                o_ref[rows, g * 256:(g + 1) * 256] = (_unstack_heads(pv) / _unstack_heads(l)).astype(BF16)
                lse4 = (m + jnp.log(l))[:, 0:1]
                for hl in range(GROUP):
                    h = g * GROUP + hl
                    lse_ref[rows, h:h + 1] = lse4[hl * WINDOW:(hl + 1) * WINDOW]

    hb = ts // WINDOW
    cur = lambda i: (i, 0)
    prev = lambda i: (jnp.maximum(i * hb - 1, 0), 0)
    return pl.pallas_call(
        body, name=name, grid=(S // ts,),
        in_specs=[pl.BlockSpec((ts, 768), cur),
                  pl.BlockSpec((WINDOW, 256), lambda i: (jnp.maximum(i * hb - 1, 0), 2)),
                  pl.BlockSpec((ts, 128), cur), pl.BlockSpec((ts, 128), cur),
                  pl.BlockSpec((WINDOW, 128), prev), pl.BlockSpec((WINDOW, 128), prev),
                  pl.BlockSpec((8, 128), lambda i: (0, 0))],
        out_specs=[pl.BlockSpec((ts, 512), cur), pl.BlockSpec((ts, 8), cur)],
        out_shape=(jax.ShapeDtypeStruct((S, 512), BF16), jax.ShapeDtypeStruct((S, 8), F32)),
        compiler_params=_cp("parallel"),
    )(pq, pq, cos_t, sin_t, cos_t, sin_t, sinks_b)


def _attn_bwd(pq, cos_t, sin_t, sinks_b, do, o, lse, *, name, ts=256):
    S = pq.shape[0]
    ts = min(ts, S)
    nq = ts // WINDOW
    nt = S // ts
    scale = HEAD_DIM ** -0.5
    NT = (((1,), (1,)), ((), ()))
    TN = (((0,), (0,)), ((), ()))

    def body(cur_ref, prev_ref, nxt_ref, cosc_ref, sinc_ref, cosp_ref, sinp_ref, cosn_ref, sinn_ref, sink_ref,
             doc_ref, don_ref, oc_ref, on_ref, lsec_ref, lsen_ref, dpq_ref, dsink_ref):
        i = pl.program_id(0)
        last = i == nt - 1
        cosc, sinc = cosc_ref[...], sinc_ref[...]
        cose = jnp.concatenate([cosc, cosn_ref[...]], axis=0)
        sine = jnp.concatenate([sinc, sinn_ref[...]], axis=0)
        q = jnp.concatenate([cur_ref[:, 0:512], nxt_ref[:, 0:512]], axis=0).astype(F32)
        qr = jnp.concatenate(
            [_rope_fwd(q[:, j * 128:(j + 1) * 128], cose, sine) for j in range(4)], axis=1) * scale
        qr = qr.astype(BF16)
        kc = _rope_fwd(cur_ref[:, 512:640].astype(F32), cosc, sinc)
        kp = _rope_fwd(prev_ref[:, 0:128].astype(F32), cosp_ref[...], sinp_ref[...])
        k_all = jnp.concatenate([kp, kc], axis=0)
        v_all = jnp.concatenate([prev_ref[:, 128:256].astype(F32), cur_ref[:, 640:768].astype(F32)], axis=0)
        kt = [t.astype(BF16) for t in _tile_heads(k_all)]
        vt = [t.astype(BF16) for t in _tile_heads(v_all)]
        don = jnp.where(last, jnp.zeros_like(don_ref[...]), don_ref[...])
        do_e = jnp.concatenate([doc_ref[...], don], axis=0)
        o_e = jnp.concatenate([oc_ref[...], on_ref[...]], axis=0)
        band2, col2 = _band4(2 * WINDOW)
        band1, _ = _band4(WINDOW)
        ones = jnp.ones((256, 256), BF16)

        @pl.when(i == 0)
        def _():
            dsink_ref[...] = jnp.zeros_like(dsink_ref)

        dk_acc = [[None] * (nq + 1) for _ in range(2)]
        dv_acc = [[None] * (nq + 1) for _ in range(2)]

        def add(acc, g, e, val):
            acc[g][e] = val if acc[g][e] is None else acc[g][e] + val

        for qb in range(nq + 1):
            halo = qb == nq
            rows = slice(qb * WINDOW, (qb + 1) * WINDOW)
            if halo:
                keys = slice(qb * WINDOW, (qb + 1) * WINDOW)
                valid = band1 & jnp.logical_not(last)
            else:
                keys = slice(qb * WINDOW, (qb + 2) * WINDOW)
                valid = band2 & ((col2 >= WINDOW) | (i * nq + qb > 0))
            dq_parts = []
            for g in range(2):
                qs = _stack_heads(qr[rows, g * 256:(g + 1) * 256])
                dos = _stack_heads(do_e[rows, g * 256:(g + 1) * 256])
                o_g = o_e[rows, g * 256:(g + 1) * 256].astype(F32)
                kt_b, vt_b = kt[g][keys], vt[g][keys]
                lse_src = lsen_ref if halo else lsec_ref
                lse_rows = slice(0, WINDOW) if halo else rows
                big_l = jnp.concatenate([lse_src[lse_rows, g * GROUP + hl:g * GROUP + hl + 1] for hl in range(GROUP)],
                                        axis=0)
                delta = jnp.dot((dos.astype(F32) * jnp.concatenate([o_g] * GROUP, axis=0)).astype(BF16), ones,
                                preferred_element_type=F32)
                s = lax.dot_general(qs, kt_b, NT, preferred_element_type=F32)
                p = jnp.exp(jnp.where(valid, s, MASK_VALUE) - big_l)
                dp = lax.dot_general(dos, vt_b, NT, preferred_element_type=F32)
                ds = (p * (dp - delta[:, 0:p.shape[1]])).astype(BF16)
                dk_g = lax.dot_general(ds, qs, TN, preferred_element_type=F32)
                dv_g = lax.dot_general(p.astype(BF16), dos, TN, preferred_element_type=F32)
                if not halo:
                    dq_parts.append(_unstack_heads(jnp.dot(ds, kt_b, preferred_element_type=F32)))
                    dsink4 = jnp.exp(_sink_column(sink_ref, g) - big_l) * delta[:, 0:1]
                    for hl in range(GROUP):
                        h = g * GROUP + hl
                        dsink_h = -jnp.sum(dsink4[hl * WINDOW:(hl + 1) * WINDOW], axis=0, keepdims=True)
                        dsink_ref[h:h + 1, :] += jnp.broadcast_to(dsink_h, (1, 128))
                add(dk_acc, g, qb, dk_g[0:WINDOW])
                add(dv_acc, g, qb, dv_g[0:WINDOW])
                if not halo:
                    add(dk_acc, g, qb + 1, dk_g[WINDOW:2 * WINDOW])
                    add(dv_acc, g, qb + 1, dv_g[WINDOW:2 * WINDOW])
            if not halo:
                cs, sn = cosc[rows], sinc[rows]
                for g in range(2):
                    dq_g = dq_parts[g] * scale
                    for j in range(2):
                        c0 = g * 256 + j * 128
                        dpq_ref[rows, c0:c0 + 128] = _rope_bwd(dq_g[:, j * 128:(j + 1) * 128], cs, sn).astype(BF16)
        for e in range(1, nq + 1):
            rows = slice((e - 1) * WINDOW, e * WINDOW)
            dk = _fold_heads(dk_acc[0][e], dk_acc[1][e])
            dv = _fold_heads(dv_acc[0][e], dv_acc[1][e])
            dpq_ref[rows, 512:640] = _rope_bwd(dk, cosc[rows], sinc[rows]).astype(BF16)
            dpq_ref[rows, 640:768] = dv.astype(BF16)

    hb = ts // WINDOW
    nblk = S // WINDOW
    cur = lambda i: (i, 0)
    prev = lambda i: (jnp.maximum(i * hb - 1, 0), 0)
    nxt = lambda i: (jnp.minimum((i + 1) * hb, nblk - 1), 0)
    return pl.pallas_call(
        body, name=name, grid=(nt,),
        in_specs=[pl.BlockSpec((ts, 768), cur),
                  pl.BlockSpec((WINDOW, 256), lambda i: (jnp.maximum(i * hb - 1, 0), 2)),
                  pl.BlockSpec((WINDOW, 768), nxt),
                  pl.BlockSpec((ts, 128), cur), pl.BlockSpec((ts, 128), cur),
                  pl.BlockSpec((WINDOW, 128), prev), pl.BlockSpec((WINDOW, 128), prev),
                  pl.BlockSpec((WINDOW, 128), nxt), pl.BlockSpec((WINDOW, 128), nxt),
                  pl.BlockSpec((8, 128), lambda i: (0, 0)),
                  pl.BlockSpec((ts, 512), cur), pl.BlockSpec((WINDOW, 512), nxt),
                  pl.BlockSpec((ts, 512), cur), pl.BlockSpec((WINDOW, 512), nxt),
                  pl.BlockSpec((ts, 8), cur), pl.BlockSpec((WINDOW, 8), nxt)],
        out_specs=[pl.BlockSpec((ts, 768), cur), pl.BlockSpec((8, 128), lambda i: (0, 0))],
        out_shape=(jax.ShapeDtypeStruct((S, 768), BF16), jax.ShapeDtypeStruct((8, 128), F32)),
        compiler_params=_cp("arbitrary"),
    )(pq, pq, pq, cos_t, sin_t, cos_t, sin_t, cos_t, sin_t, sinks_b, do, do, o, o, lse, lse)


def _shift_dn(x, k):
    return pltpu.roll(x, k, 0)


def _shift_up(x, k):
    return pltpu.roll(x, x.shape[0] - k, 0)


def _pool_lane_select(vals):
    lane = lax.broadcasted_iota(jnp.int32, (1, 256), 1)
    out = vals[3]
    for g in (2, 1, 0):
        out = jnp.where(lane < 64 * (g + 1), vals[g], out)
    return out


def _pool_inv_count(t0, n):
    t = t0 + lax.broadcasted_iota(jnp.int32, (n, 256), 0)
    lane = lax.broadcasted_iota(jnp.int32, (n, 256), 1)
    w = jnp.where(lane < 64, 2, jnp.where(lane < 128, 4, jnp.where(lane < 192, 8, 16)))
    return 1.0 / jnp.minimum(t + 1, w).astype(F32)


def _pooled(u_ext, t0, n):
    s2 = u_ext + _shift_dn(u_ext, 1)
    s4 = s2 + _shift_dn(s2, 2)
    s8 = s4 + _shift_dn(s4, 4)
    s16 = s8 + _shift_dn(s8, 8)
    win = _pool_lane_select([s2, s4, s8, s16])[HALO:HALO + n]
    return win * _pool_inv_count(t0, n) - u_ext[HALO:HALO + n]


def _poolconv_fwd(pp, wbd, pool_scale, conv_w, *, name, ts=512):
    S = pp.shape[0]
    ts = min(ts, S)

    def body(cur_ref, prev_ref, wbd_ref, sc_ref, cw_ref, oa_ref, oc_ref):
        i = pl.program_id(0)
        prev = jnp.where(i > 0, prev_ref[...].astype(F32), 0.0)
        u_ext = jnp.concatenate([prev[:, 0:256], cur_ref[:, 0:256].astype(F32)], axis=0)
        pooled = _pooled(u_ext, i * ts, ts)
        mixed = jnp.dot(pooled.astype(BF16), wbd_ref[...], preferred_element_type=F32)
        oa_ref[...] = (mixed * sc_ref[...]).astype(BF16)
        v_ext = jnp.concatenate([prev[:, 256:512] * prev[:, 768:1024],
                                 cur_ref[:, 256:512].astype(F32) * cur_ref[:, 768:1024].astype(F32)], axis=0)
        cv = cw_ref[2:3, :] * v_ext + cw_ref[1:2, :] * _shift_dn(v_ext, 1) + cw_ref[0:1, :] * _shift_dn(v_ext, 2)
        oc_ref[...] = (cur_ref[:, 512:768].astype(F32) * cv[HALO:HALO + ts]).astype(BF16)

    hb = ts // HALO
    cur = lambda i: (i, 0)
    const = lambda i: (0, 0)
    return pl.pallas_call(
        body, name=name, grid=(S // ts,),
        in_specs=[pl.BlockSpec((ts, 1024), cur),
                  pl.BlockSpec((HALO, 1024), lambda i: (jnp.maximum(i * hb - 1, 0), 0)),
                  pl.BlockSpec((256, 256), const), pl.BlockSpec((1, 256), const), pl.BlockSpec((3, 256), const)],
        out_specs=[pl.BlockSpec((ts, 256), cur)] * 2,
        out_shape=(jax.ShapeDtypeStruct((S, 256), BF16),) * 2,
        compiler_params=_cp("parallel"),
    )(pp, pp, wbd, pool_scale, conv_w)


def _poolconv_bwd(pp, do_a, do_c, wbd, wbd_t, pool_scale, conv_w, *, name, ts=512):
    S = pp.shape[0]
    ts = min(ts, S)
    nt = S // ts
    n_e = ts + 2 * HALO

    def body(cur_ref, prev_ref, nxt_ref, dac_ref, dan_ref, dcc_ref, dcn_ref, wbd_ref, wbdt_ref, sc_ref, cw_ref,
             dpp_ref, pooled_ref, dmixed_ref, dsc_ref, dcw_ref):
        i = pl.program_id(0)

        @pl.when(i == 0)
        def _():
            dsc_ref[...] = jnp.zeros_like(dsc_ref)
            dcw_ref[...] = jnp.zeros_like(dcw_ref)

        prev = jnp.where(i > 0, prev_ref[...].astype(F32), 0.0)
        nxt = nxt_ref[...].astype(F32)
        cur = cur_ref[...].astype(F32)
        not_last = i < nt - 1
        da_n = jnp.where(not_last, dan_ref[...].astype(F32), 0.0)
        dc_n = jnp.where(not_last, dcn_ref[...].astype(F32), 0.0)
        zeros_h = jnp.zeros((HALO, 256), F32)
        sc = sc_ref[...]

        u_ext = jnp.concatenate([prev[:, 0:256], cur[:, 0:256]], axis=0)
        pooled = _pooled(u_ext, i * ts, ts)
        pooled_b = pooled.astype(BF16)
        pooled_ref[...] = pooled_b
        mixed = jnp.dot(pooled_b, wbd_ref[...], preferred_element_type=F32)
        da_c = dac_ref[...].astype(F32)
        dsc_ref[...] += jnp.sum(da_c * mixed, axis=0, keepdims=True)
        dmixed_e = jnp.concatenate([da_c, da_n], axis=0) * sc
        dmixed_ref[...] = dmixed_e[0:ts].astype(BF16)
        dpooled = jnp.dot(dmixed_e.astype(BF16), wbdt_ref[...], preferred_element_type=F32)
        qd = dpooled * _pool_inv_count(i * ts, ts + HALO)
        f2 = qd + _shift_up(qd, 1)
        f4 = f2 + _shift_up(f2, 2)
        f8 = f4 + _shift_up(f4, 4)
        f16 = f8 + _shift_up(f8, 8)
        du = (_pool_lane_select([f2, f4, f8, f16]) - dpooled)[0:ts]
        dpp_ref[:, 0:256] = du.astype(BF16)

        xc_e = jnp.concatenate([prev[:, 256:512], cur[:, 256:512], nxt[:, 256:512]], axis=0)
        gc_e = jnp.concatenate([prev[:, 768:1024], cur[:, 768:1024], nxt[:, 768:1024]], axis=0)
        gb_e = jnp.concatenate([zeros_h, cur[:, 512:768], nxt[:, 512:768]], axis=0)
        dc_e = jnp.concatenate([zeros_h, dcc_ref[...].astype(F32), dc_n], axis=0)
        v_e = xc_e * gc_e
        v1, v2 = _shift_dn(v_e, 1), _shift_dn(v_e, 2)
        w0, w1, w2 = cw_ref[0:1, :], cw_ref[1:2, :], cw_ref[2:3, :]
        cv = w2 * v_e + w1 * v1 + w0 * v2
        dcv = dc_e * gb_e
        dv = w2 * dcv + w1 * _shift_up(dcv, 1) + w0 * _shift_up(dcv, 2)
        tile = slice(HALO, HALO + ts)
        dpp_ref[:, 256:512] = (dv * gc_e)[tile].astype(BF16)
        dpp_ref[:, 512:768] = (dc_e * cv)[tile].astype(BF16)
        dpp_ref[:, 768:1024] = (dv * xc_e)[tile].astype(BF16)
        dcv_t = dcv[tile]
        dcw_ref[0:1, :] += jnp.sum(dcv_t * v2[tile], axis=0, keepdims=True)
        dcw_ref[1:2, :] += jnp.sum(dcv_t * v1[tile], axis=0, keepdims=True)
        dcw_ref[2:3, :] += jnp.sum(dcv_t * v_e[tile], axis=0, keepdims=True)

    hb = ts // HALO
    nblk = S // HALO
    cur = lambda i: (i, 0)
    const = lambda i: (0, 0)
    prev = lambda i: (jnp.maximum(i * hb - 1, 0), 0)
    nxt = lambda i: (jnp.minimum((i + 1) * hb, nblk - 1), 0)
    del n_e
    return pl.pallas_call(
        body, name=name, grid=(nt,),
        in_specs=[pl.BlockSpec((ts, 1024), cur), pl.BlockSpec((HALO, 1024), prev), pl.BlockSpec((HALO, 1024), nxt),
                  pl.BlockSpec((ts, 256), cur), pl.BlockSpec((HALO, 256), nxt),
                  pl.BlockSpec((ts, 256), cur), pl.BlockSpec((HALO, 256), nxt),
                  pl.BlockSpec((256, 256), const), pl.BlockSpec((256, 256), const),
                  pl.BlockSpec((1, 256), const), pl.BlockSpec((3, 256), const)],
        out_specs=[pl.BlockSpec((ts, 1024), cur), pl.BlockSpec((ts, 256), cur), pl.BlockSpec((ts, 256), cur),
                   pl.BlockSpec((1, 256), const), pl.BlockSpec((3, 256), const)],
        out_shape=(jax.ShapeDtypeStruct((S, 1024), BF16), jax.ShapeDtypeStruct((S, 256), BF16),
                   jax.ShapeDtypeStruct((S, 256), BF16), jax.ShapeDtypeStruct((1, 256), F32),
                   jax.ShapeDtypeStruct((3, 256), F32)),
        compiler_params=_cp("arbitrary"),
    )(pp, pp, pp, do_a, do_a, do_c, do_c, wbd, wbd_t, pool_scale, conv_w)


def _sigmoid(x):
    return 0.5 * jnp.tanh(0.5 * x) + 0.5


def _merge_fwd(o_a, o_b, o_c, glog, w_br, *, name, ts=512):
    S = o_a.shape[0]
    D = w_br.shape[1]
    ts = min(ts, S)

    def body(oa_ref, ob_ref, oc_ref, gl_ref, w_ref, m_ref):
        pa = jnp.dot(oa_ref[...], w_ref[0:256, :], preferred_element_type=F32)
        pb = jnp.dot(ob_ref[...], w_ref[256:768, :], preferred_element_type=F32)
        pc = jnp.dot(oc_ref[...], w_ref[768:1024, :], preferred_element_type=F32)
        m = _sigmoid(gl_ref[:, 0:D].astype(F32)) * pa
        m = m + _sigmoid(gl_ref[:, D:2 * D].astype(F32)) * pb
        m = m + _sigmoid(gl_ref[:, 2 * D:3 * D].astype(F32)) * pc
        m_ref[...] = m.astype(BF16)

    cur = lambda i: (i, 0)
    return pl.pallas_call(
        body, name=name, grid=(S // ts,),
        in_specs=[pl.BlockSpec((ts, 256), cur), pl.BlockSpec((ts, 512), cur), pl.BlockSpec((ts, 256), cur),
                  pl.BlockSpec((ts, 3 * D), cur), pl.BlockSpec((1024, D), lambda i: (0, 0))],
        out_specs=pl.BlockSpec((ts, D), cur),
        out_shape=jax.ShapeDtypeStruct((S, D), BF16),
        compiler_params=_cp("parallel"),
    )(o_a, o_b, o_c, glog, w_br)


def _merge_bwd(dm, o_a, o_b, o_c, glog, w_br, w_br_t, *, name, ts=256):
    S = o_a.shape[0]
    D = w_br.shape[1]
    ts = min(ts, S)

    def body(dm_ref, oa_ref, ob_ref, oc_ref, gl_ref, w_ref, wt_ref, dgl_ref, dp_ref, doa_ref, dob_ref, doc_ref):
        dmv = dm_ref[...].astype(F32)
        branches = ((oa_ref, 0, 256, doa_ref), (ob_ref, 256, 768, dob_ref), (oc_ref, 768, 1024, doc_ref))
        for b, (o_ref, r0, r1, do_ref) in enumerate(branches):
            prod = jnp.dot(o_ref[...], w_ref[r0:r1, :], preferred_element_type=F32)
            gate = _sigmoid(gl_ref[:, b * D:(b + 1) * D].astype(F32))
            dgl_ref[:, b * D:(b + 1) * D] = (dmv * prod * gate * (1.0 - gate)).astype(BF16)
            dprod = (dmv * gate).astype(BF16)
            dp_ref[:, b * D:(b + 1) * D] = dprod
            do_ref[...] = jnp.dot(dprod, wt_ref[:, r0:r1], preferred_element_type=F32).astype(BF16)

    cur = lambda i: (i, 0)
    const = lambda i: (0, 0)
    return pl.pallas_call(
        body, name=name, grid=(S // ts,),
        in_specs=[pl.BlockSpec((ts, D), cur), pl.BlockSpec((ts, 256), cur), pl.BlockSpec((ts, 512), cur),
                  pl.BlockSpec((ts, 256), cur), pl.BlockSpec((ts, 3 * D), cur),
                  pl.BlockSpec((1024, D), const), pl.BlockSpec((D, 1024), const)],
        out_specs=[pl.BlockSpec((ts, 3 * D), cur), pl.BlockSpec((ts, 3 * D), cur), pl.BlockSpec((ts, 256), cur),
                   pl.BlockSpec((ts, 512), cur), pl.BlockSpec((ts, 256), cur)],
        out_shape=(jax.ShapeDtypeStruct((S, 3 * D), BF16), jax.ShapeDtypeStruct((S, 3 * D), BF16),
                   jax.ShapeDtypeStruct((S, 256), BF16), jax.ShapeDtypeStruct((S, 512), BF16),
                   jax.ShapeDtypeStruct((S, 256), BF16)),
        compiler_params=_cp("parallel"),
    )(dm, o_a, o_b, o_c, glog, w_br, w_br_t)


FFN_CHUNK = 128
FFN_DOT_CHUNKS = 4


def _conv3(x, w_ref, cols):
    x1, x2 = _shift_dn(x, 1), _shift_dn(x, 2)
    return w_ref[2:3, cols] * x + w_ref[1:2, cols] * x1 + w_ref[0:1, cols] * x2, x1, x2


def _ffn_down_fwd(up_pre, fcw, w_down3, resid, gamma, beta, *, alpha, name, tc, ts=256, gather=None):
    S, F2 = up_pre.shape
    D = resid.shape[1]
    ts = min(ts, S)
    nt = S // ts
    nj = F2 // (2 * tc)
    has_ag = gather is not None
    n_g = len(gather) if has_ag else 0

    def body(cur_ref, prev_ref, w_ref, wd_ref, r_ref, g_ref, be_ref, *rest):
        h_ref, y_ref, xh_ref, rs_ref, up_ref = rest[n_g:n_g + 5]
        acc_ref = rest[2 * n_g + 5]
        if has_ag:
            ag_refs = (rest[:n_g], rest[n_g + 5:2 * n_g + 5]) + tuple(rest[2 * n_g + 6:2 * n_g + 9])
        i, j = pl.program_id(0), pl.program_id(1)
        if has_ag:
            @pl.when((i == 0) & (j == 0))
            def _():
                _ag_start(*ag_refs)

        part = None
        for c in range(tc // FFN_CHUNK):
            halves = []
            for half in range(2):
                cols = slice(half * tc + c * FFN_CHUNK, half * tc + (c + 1) * FFN_CHUNK)
                prev = jnp.where(i > 0, prev_ref[:, cols].astype(F32), 0.0)
                x = jnp.concatenate([prev, cur_ref[:, cols].astype(F32)], axis=0)
                halves.append(_conv3(x, w_ref, cols)[0][HALO:HALO + ts])
                up_ref[:, cols] = halves[-1].astype(BF16)
            a, b = halves
            h_ref[:, c * FFN_CHUNK:(c + 1) * FFN_CHUNK] = (a * _sigmoid(a) * b).astype(BF16)
            if (c + 1) % FFN_DOT_CHUNKS == 0 or c + 1 == tc // FFN_CHUNK:
                k0 = (c // FFN_DOT_CHUNKS) * FFN_DOT_CHUNKS * FFN_CHUNK
                piece = jnp.dot(h_ref[:, k0:(c + 1) * FFN_CHUNK], wd_ref[j, k0:(c + 1) * FFN_CHUNK, :],
                                preferred_element_type=F32)
                part = piece if part is None else part + piece

        @pl.when(j == 0)
        def _():
            acc_ref[...] = part

        @pl.when(j > 0)
        def _():
            acc_ref[...] += part

        @pl.when(j == nj - 1)
        def _():
            z = alpha * r_ref[...] + acc_ref[...]
            mu = jnp.mean(z, axis=-1, keepdims=True)
            zc = z - mu
            var = jnp.mean(zc * zc, axis=-1, keepdims=True)
            rstd = lax.rsqrt(var + LN_EPS)
            xhat = zc * rstd
            y_ref[...] = xhat * g_ref[...] + be_ref[...]
            xh_ref[...] = xhat.astype(BF16)
            rs_ref[...] = rstd

        if has_ag:
            @pl.when((i == nt - 1) & (j == nj - 1))
            def _():
                _ag_finish(*ag_refs)

    hb = ts // HALO
    row = lambda i, j: (i, 0)
    vec = lambda i, j: (0, 0)
    in_specs = [pl.BlockSpec((ts, 2 * tc), lambda i, j: (i, j)),
                pl.BlockSpec((HALO, 2 * tc), lambda i, j: (jnp.maximum(i * hb - 1, 0), j)),
                pl.BlockSpec((3, 2 * tc), lambda i, j: (0, j)),
                pl.BlockSpec((nj, tc, D), lambda i, j: (0, 0, 0)),
                pl.BlockSpec((ts, D), row), pl.BlockSpec((1, D), vec), pl.BlockSpec((1, D), vec)]
    out_specs = [pl.BlockSpec((ts, tc), lambda i, j: (i, j)), pl.BlockSpec((ts, D), row), pl.BlockSpec((ts, D), row),
                 pl.BlockSpec((ts, 1), row), pl.BlockSpec((ts, 2 * tc), lambda i, j: (i, j))]
    out_shape = [jax.ShapeDtypeStruct((S, F2 // 2), BF16), jax.ShapeDtypeStruct((S, D), F32),
                 jax.ShapeDtypeStruct((S, D), BF16), jax.ShapeDtypeStruct((S, 1), F32),
                 jax.ShapeDtypeStruct((S, F2), BF16)]
    args = [up_pre, up_pre, fcw, w_down3, resid, gamma, beta]
    scratch = [pltpu.VMEM((ts, D), F32)]
    if has_ag:
        in_specs += [pl.BlockSpec(memory_space=pl.ANY)] * n_g
        args += list(gather)
        out_specs += [pl.BlockSpec(memory_space=pl.ANY)] * n_g
        out_shape += [jax.ShapeDtypeStruct((N_DEV,) + g.shape, g.dtype) for g in gather]
        scratch += _ag_sems(n_g)
    res = pl.pallas_call(
        body, name=name, grid=(nt, nj), in_specs=in_specs, out_specs=out_specs, out_shape=out_shape,
        scratch_shapes=scratch, compiler_params=_cp("arbitrary", "arbitrary"),
    )(*args)
    return tuple(res[:5]) + ((list(res[5:]),) if has_ag else ())


def _ffn_up_bwd(up_pre, up, dh, fcw, w_up_t3, dz, *, alpha, name, tc, ts=256, scatter=None):
    S, F2 = up_pre.shape
    D = dz.shape[1]
    ts = min(ts, S)
    nt = S // ts
    nj = F2 // (2 * tc)
    has_rs = scatter is not None
    n_s = len(scatter) if has_rs else 0
    tile = slice(0, ts)

    def body(x_ref, upc_ref, upn_ref, dhc_ref, dhn_ref, w_ref, wt_ref, dz_ref, *rest):
        dpre_ref, dx_ref, dw_ref = rest[n_s:n_s + 3]
        acc_ref = rest[2 * n_s + 3]
        if has_rs:
            rs_refs = (rest[:n_s], rest[n_s + 3:2 * n_s + 3], rest[2 * n_s + 4], rest[2 * n_s + 5])
        i, j = pl.program_id(0), pl.program_id(1)

        @pl.when((i == 0) & (j == 0))
        def _():
            dw_ref[...] = jnp.zeros_like(dw_ref)
            if has_rs:
                _rs_chips_start(*rs_refs)

        part = None
        for c in range(tc // FFN_CHUNK):
            lanes = slice(c * FFN_CHUNK, (c + 1) * FFN_CHUNK)
            dh_n = jnp.where(i < nt - 1, dhn_ref[:, lanes].astype(F32), 0.0)
            dh_e = jnp.concatenate([dhc_ref[:, lanes].astype(F32), dh_n], axis=0)
            cols_of = [slice(half * tc + c * FFN_CHUNK, half * tc + (c + 1) * FFN_CHUNK) for half in range(2)]
            a, b = [jnp.concatenate([upc_ref[:, cols].astype(F32), upn_ref[:, cols].astype(F32)], axis=0)
                    for cols in cols_of]
            sg = _sigmoid(a)
            dups = [dh_e * b * (sg * (1.0 + a * (1.0 - sg))), dh_e * (a * sg)]
            for half in range(2):
                cols, dup = cols_of[half], dups[half]
                dup1, dup2 = _shift_up(dup, 1), _shift_up(dup, 2)
                dpre = w_ref[2:3, cols] * dup + w_ref[1:2, cols] * dup1 + w_ref[0:1, cols] * dup2
                dpre_ref[:, cols] = dpre[tile].astype(BF16)
                x = x_ref[:, cols].astype(F32)
                dw_ref[j, 0:1, cols] += jnp.sum(dup2[tile] * x, axis=0, keepdims=True)
                dw_ref[j, 1:2, cols] += jnp.sum(dup1[tile] * x, axis=0, keepdims=True)
                dw_ref[j, 2:3, cols] += jnp.sum(dup[tile] * x, axis=0, keepdims=True)
            if (c + 1) % FFN_DOT_CHUNKS == 0 or c + 1 == tc // FFN_CHUNK:
                k0 = (c // FFN_DOT_CHUNKS) * FFN_DOT_CHUNKS * FFN_CHUNK
                for half in range(2):
                    ks = slice(half * tc + k0, half * tc + (c + 1) * FFN_CHUNK)
                    piece = jnp.dot(dpre_ref[:, ks], wt_ref[j, ks, :], preferred_element_type=F32)
                    part = piece if part is None else part + piece

        @pl.when(j == 0)
        def _():
            acc_ref[...] = part

        @pl.when(j > 0)
        def _():
            acc_ref[...] += part

        @pl.when(j == nj - 1)
        def _():
            dx_ref[...] = acc_ref[...] + alpha * dz_ref[...]

        if has_rs:
            @pl.when((i == nt - 1) & (j == nj - 1))
            def _():
                _rs_chips_finish(*rs_refs)

    hb = ts // HALO
    nblk = S // HALO
    nxt = lambda i, j: (jnp.minimum((i + 1) * hb, nblk - 1), j)
    row = lambda i, j: (i, 0)
    in_specs = [pl.BlockSpec((ts, 2 * tc), lambda i, j: (i, j)),
                pl.BlockSpec((ts, 2 * tc), lambda i, j: (i, j)), pl.BlockSpec((HALO, 2 * tc), nxt),
                pl.BlockSpec((ts, tc), lambda i, j: (i, j)), pl.BlockSpec((HALO, tc), nxt),
                pl.BlockSpec((3, 2 * tc), lambda i, j: (0, j)),
                pl.BlockSpec((nj, 2 * tc, D), lambda i, j: (0, 0, 0)),
                pl.BlockSpec((ts, D), row)]
    out_specs = [pl.BlockSpec((ts, 2 * tc), lambda i, j: (i, j)), pl.BlockSpec((ts, D), row),
                 pl.BlockSpec((nj, 3, 2 * tc), lambda i, j: (0, 0, 0))]
    out_shape = [jax.ShapeDtypeStruct((S, F2), BF16), jax.ShapeDtypeStruct((S, D), F32),
                 jax.ShapeDtypeStruct((nj, 3, 2 * tc), F32)]
    args = [up_pre, up, up, dh, dh, fcw, w_up_t3, dz]
    scratch = [pltpu.VMEM((ts, D), F32)]
    if has_rs:
        in_specs += [pl.BlockSpec(memory_space=pl.ANY)] * n_s
        args += list(scatter)
        out_specs += [pl.BlockSpec(memory_space=pl.ANY)] * n_s
        out_shape += [jax.ShapeDtypeStruct((3,) + q.shape[1:], q.dtype) for q in scatter]
        scratch += _rs_sems(n_s)
    res = pl.pallas_call(
        body, name=name, grid=(nt, nj), in_specs=in_specs, out_specs=out_specs, out_shape=out_shape,
        scratch_shapes=scratch, compiler_params=_cp("arbitrary", "arbitrary"),
    )(*args)
    return tuple(res[:3]) + ((list(res[3:]),) if has_rs else ())


def _ln_bwd(dy, xhat, rstd, gamma, *, name, ts=512):
    S, D = dy.shape
    ts = min(ts, S)

    def body(dy_ref, xh_ref, rs_ref, g_ref, dz_ref, dg_ref, db_ref):
        _ln_bwd_tile(dy_ref[...], xh_ref, rs_ref, g_ref, dz_ref, dg_ref, db_ref, pl.program_id(0) == 0)

    cur = lambda i: (i, 0)
    const = lambda i: (0, 0)
    return pl.pallas_call(
        body, name=name, grid=(S // ts,),
        in_specs=[pl.BlockSpec((ts, D), cur), pl.BlockSpec((ts, D), cur), pl.BlockSpec((ts, 1), cur),
                  pl.BlockSpec((1, D), const)],
        out_specs=[pl.BlockSpec((ts, D), cur), pl.BlockSpec((1, D), const), pl.BlockSpec((1, D), const)],
        out_shape=(jax.ShapeDtypeStruct((S, D), F32), jax.ShapeDtypeStruct((1, D), F32),
                   jax.ShapeDtypeStruct((1, D), F32)),
        compiler_params=_cp("arbitrary"),
    )(dy, xhat, rstd, gamma)


def _loss_head(y, tgt, *, name, ts=512):
    S, D = y.shape
    ts = min(ts, S)

    def body(y_ref, t_ref, dy_ref, sq_ref):
        @pl.when(pl.program_id(0) == 0)
        def _():
            sq_ref[...] = jnp.zeros_like(sq_ref)

        e = y_ref[...] - t_ref[...]
        dy_ref[...] = e * (1.0 / D)
        sq_ref[...] += jnp.sum(e * e, axis=0, keepdims=True)

    cur = lambda i: (i, 0)
    return pl.pallas_call(
        body, name=name, grid=(S // ts,),
        in_specs=[pl.BlockSpec((ts, D), cur), pl.BlockSpec((ts, D), cur)],
        out_specs=[pl.BlockSpec((ts, D), cur), pl.BlockSpec((1, D), lambda i: (0, 0))],
        out_shape=(jax.ShapeDtypeStruct((S, D), F32), jax.ShapeDtypeStruct((1, D), F32)),
        compiler_params=_cp("arbitrary"),
    )(y, tgt)


def _adamw(w, g, m, v, *, name, tr=512):
    lead = w.shape[:-2]
    R, C = w.shape[-2:]
    tr = _div_tile(R, tr)
    c1 = 1.0 - ADAM_B1 ** ADAM_STEP
    c2 = 1.0 - ADAM_B2 ** ADAM_STEP

    def body(w_ref, g_ref, m_ref, v_ref, d_ref, mo_ref, vo_ref):
        gv = g_ref[...]
        m2 = ADAM_B1 * m_ref[...] + (1.0 - ADAM_B1) * gv
        v2 = ADAM_B2 * v_ref[...] + (1.0 - ADAM_B2) * (gv * gv)
        m_hat = m2 / c1
        v_hat = v2 / c2
        d_ref[...] = -ADAM_LR * (m_hat / (jnp.sqrt(v_hat) + ADAM_EPS) + ADAM_WD * w_ref[...])
        mo_ref[...] = m2
        vo_ref[...] = v2

    if lead:
        spec = pl.BlockSpec((1, tr, C), lambda l, i: (l, i, 0))
        grid = (lead[0], R // tr)
    else:
        spec = pl.BlockSpec((tr, C), lambda i: (i, 0))
        grid = (R // tr,)
    return pl.pallas_call(
        body, name=name, grid=grid,
        in_specs=[spec] * 4, out_specs=[spec] * 3,
        out_shape=(jax.ShapeDtypeStruct(w.shape, F32),) * 3,
        compiler_params=_cp(*(("parallel",) * len(grid))),
    )(w, g, m, v)


def _interleave_cols(w, nj):
    lead, f2 = w.shape[:-1], w.shape[-1]
    tc = f2 // (2 * nj)
    w = w.reshape(lead + (2, nj, tc))
    return jnp.swapaxes(w, -3, -2).reshape(lead + (f2,))


def _deinterleave_cols(w, nj):
    lead, f2 = w.shape[:-1], w.shape[-1]
    tc = f2 // (2 * nj)
    w = w.reshape(lead + (nj, 2, tc))
    return jnp.swapaxes(w, -3, -2).reshape(lead + (f2,))


def _block_diag(w_pool):
    return jnp.concatenate([jnp.pad(w_pool[g], ((0, 0), (64 * g, 192 - 64 * g))) for g in range(4)], axis=0)


def _pad_rows(v, rows):
    return jnp.pad(v, (0, rows * LANES - v.shape[0])).reshape(rows, LANES)


def kernel(x, positions, w_in, w_pool, pool_scale, attn_sinks, conv_w, w_branch_a, w_branch_b, w_branch_c, w_o, ln1_g, ln1_b, w_up, ffn_conv_w, w_down, ln2_g, ln2_b, loss_target, m_w_in, m_w_pool, m_pool_scale, m_attn_sinks, m_conv_w, m_w_branch_a, m_w_branch_b, m_w_branch_c, m_w_o, m_ln1_g, m_ln1_b, m_w_up, m_ffn_conv_w, m_w_down, m_ln2_g, m_ln2_b, v_w_in, v_w_pool, v_pool_scale, v_attn_sinks, v_conv_w, v_w_branch_a, v_w_branch_b, v_w_branch_c, v_w_o, v_ln1_g, v_ln1_b, v_w_up, v_ffn_conv_w, v_w_down, v_ln2_g, v_ln2_b):
    L, D, in_shard = w_in.shape
    S = x.shape[1]
    IN = in_shard * N_DEV
    F2 = w_up.shape[2] * N_DEV
    F = F2 // 2
    assert D == 1024 and IN == 1792 + 3 * D and x.shape[0] == 1 and S % 512 == 0
    alpha = (2 * L) ** 0.25
    NJ = 2
    TC = F // NJ
    xs = x.reshape(S, D)
    tgt = loss_target.reshape(S, D)

    big = [w_in, w_branch_a, w_branch_b, w_branch_c, w_o, w_up, w_down]
    PART_A, PART_B = (0, 1, 2, 3, 4), (5, 6)
    rows_l = [a.size // L // LANES for a in big]
    offs_l = [sum(rows_l[:k]) for k in range(len(big) + 1)]

    def pack_part(l, part):
        return [(big[k][l].T if k in (0, 5) else big[k][l]).astype(BF16) for k in part]

    n_cw, n_fw = conv_w.size, ffn_conv_w.size
    small_rows = -(-(n_cw + n_fw) // LANES)
    small = _pad_rows(jnp.concatenate([conv_w.reshape(-1), ffn_conv_w.reshape(-1)]), small_rows)
    gsmall = _all_gather(small, "ag_conv_weights").reshape(N_DEV, -1)
    conv_full = gsmall[:, :n_cw].reshape(N_DEV, L, 3, -1).transpose(1, 2, 0, 3).reshape(L, 3, 256)
    fcw_full = gsmall[:, n_cw:n_cw + n_fw].reshape(N_DEV, L, 3, -1).transpose(1, 2, 0, 3).reshape(L, 3, F2)
    fcw_full = _interleave_cols(fcw_full, NJ)

    def shard_of(g, part, k, shape):
        assert g[part.index(k)].shape == (N_DEV,) + shape
        return g[part.index(k)]

    def unpack_a(g):
        win_t = shard_of(g, PART_A, 0, (in_shard, D)).reshape(IN, D)
        wg_t = win_t[1792:]
        wp_t = jnp.concatenate([win_t[0:256], win_t[1024:1792]], axis=0)
        wq_t = win_t[256:1024]
        wg, wp, wq = wg_t.T, wp_t.T, wq_t.T
        if g[1] is None:
            return dict(wg=wg, wp=wp, wq=wq)
        wa = shard_of(g, PART_A, 1, (256, D // N_DEV)).transpose(1, 0, 2).reshape(256, D)
        wb = shard_of(g, PART_A, 2, (512, D // N_DEV)).transpose(1, 0, 2).reshape(512, D)
        wc = shard_of(g, PART_A, 3, (256, D // N_DEV)).transpose(1, 0, 2).reshape(256, D)
        wbr = jnp.concatenate([wa, wb, wc], axis=0)
        wo = shard_of(g, PART_A, 4, (D // N_DEV, D)).reshape(D, D)
        return dict(wg=wg, wp=wp, wq=wq, wg_t=wg_t, wp_t=wp_t, wq_t=wq_t, wbr=wbr, wbr_t=wbr.T, wo=wo, wo_t=wo.T)

    def unpack_b(g):
        nh = N_DEV // (2 * NJ)
        wup_t = shard_of(g, PART_B, 5, (F2 // N_DEV, D)).reshape(2, NJ, nh * (F2 // N_DEV), D)
        wup_t = wup_t.transpose(1, 0, 2, 3).reshape(F2, D)
        wdn = shard_of(g, PART_B, 6, (F // N_DEV, D)).reshape(F, D)
        return dict(wup=wup_t.T, wup_t=wup_t, wdn=wdn, wdn_t=wdn.T)

    def local_weights(l):
        wbd = _block_diag(w_pool[l]).astype(BF16)
        return dict(wbd=wbd, wbd_t=wbd.T, scale=pool_scale[l].reshape(1, 256), conv=conv_full[l],
                    fcw=fcw_full[l], sinks=jnp.broadcast_to(attn_sinks[l].reshape(8, 1), (8, 128)),
                    g1=ln1_g[l].reshape(1, D), b1=ln1_b[l].reshape(1, D),
                    g2=ln2_g[l].reshape(1, D), b2=ln2_b[l].reshape(1, D))

    inv_freq = ROPE_THETA ** (-jnp.arange(0, ROT_DIM, 2, dtype=F32) / ROT_DIM)
    head_lane = jnp.concatenate([inv_freq, inv_freq, jnp.zeros((HEAD_DIM - ROT_DIM,), F32)])
    head_sign = jnp.concatenate([-jnp.ones((8,), F32), jnp.ones((8,), F32), jnp.zeros((HEAD_DIM - ROT_DIM,), F32)])
    inv_lane = jnp.tile(head_lane, 2).reshape(1, 128)
    sign_lane = jnp.tile(head_sign, 2).reshape(1, 128)
    cos_t, sin_t = _rope_tables(positions.reshape(S, 1), inv_lane, sign_lane, "rope_tables")

    saved, W = [], []
    h_in = xs
    gathered_a = [_all_gather(pack_part(0, PART_A[:1]), "ag_weights_first")]
    for l in range(L):
        if l == 0:
            w_in_only = unpack_a(gathered_a + [None] * 4)
            (pg, pp, pq), later = _mm_fan(h_in, [w_in_only["wg"], w_in_only["wp"], w_in_only["wq"]], out_dtype=BF16,
                                          name="proj_in", gather=pack_part(0, PART_A[1:]) + pack_part(0, PART_B))
            gathered_a, gathered_b = gathered_a + later[:4], later[4:]
        w = {**unpack_a(gathered_a), **unpack_b(gathered_b), **local_weights(l)}
        W.append(w)
        if l > 0:
            pg, pp, pq = _mm_fan(h_in, [w["wg"], w["wp"], w["wq"]], out_dtype=BF16, name="proj_in")
        o_a, o_c = _poolconv_fwd(pp, w["wbd"], w["scale"], w["conv"], name="poolconv_fwd")
        o_b, lse = _attn_fwd(pq, cos_t, sin_t, w["sinks"], name="attn_fwd")
        merged = _merge_fwd(o_a, o_b, o_c, pg, w["wbr"], name="merge_fwd")
        x1, xh1, rs1 = _mm_ln(merged, w["wo"], h_in, w["g1"], w["b1"], alpha=alpha, name="wo_ln1")
        if l + 1 < L:
            up_pre, gathered_a = _mm(x1, w["wup"], out_dtype=BF16, name="ffn_up",
                                     gather=pack_part(l + 1, PART_A))
        else:
            up_pre = _mm(x1, w["wup"], out_dtype=BF16, name="ffn_up")
        down = dict(alpha=alpha, name="ffn_down", tc=TC)
        wdn3 = w["wdn"].reshape(NJ, TC, D)
        if l + 1 < L:
            hact, x2, xh2, rs2, up, gathered_b = _ffn_down_fwd(up_pre, w["fcw"], wdn3, x1, w["g2"], w["b2"],
                                                               gather=pack_part(l + 1, PART_B), **down)
        else:
            hact, x2, xh2, rs2, up = _ffn_down_fwd(up_pre, w["fcw"], wdn3, x1, w["g2"], w["b2"], **down)
        saved.append(dict(up=up,x0=h_in, pg=pg, pp=pp, pq=pq, o_a=o_a, o_b=o_b, o_c=o_c, lse=lse, merged=merged,
                          x1=x1, xh1=xh1, rs1=rs1, up_pre=up_pre, hact=hact, xh2=xh2, rs2=rs2))
        h_in = x2

    dy, sq_lanes = _loss_head(h_in, tgt, name="loss_head")

    def pack_grads(g):
        col = lambda a, n: a.reshape(a.shape[0], N_DEV, n).transpose(1, 0, 2)
        row = lambda a, n: a.reshape(N_DEV, n, a.shape[1])
        nh = N_DEV // (2 * NJ)
        up_t = g["w_up_t"].reshape(NJ, 2, nh * (F2 // N_DEV), D).transpose(1, 0, 2, 3).reshape(N_DEV, F2 // N_DEV, D)
        rest = [col(g["a"], D // N_DEV), col(g["b"], D // N_DEV), col(g["c"], D // N_DEV),
                row(g["w_o"], D // N_DEV), row(g["w_down"], F // N_DEV)]
        return [row(g["w_in_t"], in_shard).astype(BF16), up_t.astype(BF16),
                jnp.concatenate([p.reshape(N_DEV, -1, LANES).astype(BF16) for p in rest], axis=1)]

    my_c = lax.axis_index("c").astype(jnp.int32).reshape(1)
    my_chip = (2 * lax.axis_index("x") + lax.axis_index("y")).astype(jnp.int32).reshape(1)
    gw = [None] * L
    pair_sum = [None] * L
    from_chips = [None] * L
    for l in reversed(range(L)):
        w, sv = W[l], saved[l]
        if l == L - 1:
            dz2, dg2, db2 = _ln_bwd(dy, sv["xh2"], sv["rs2"], w["g2"], name="ln2_bwd")
        else:
            dz2, dg2, db2 = ln2_out
        dw_dn = _mm_tn(sv["hact"], dz2, name="down_bwd_w", tka=TC, na=NJ, tn=D, ts=1024)
        up_bwd = dict(alpha=alpha, name="ffn_up_bwd", tc=TC)
        dh = _mm(dz2, w["wdn_t"], out_dtype=BF16, name="down_bwd_x")
        wup_t3 = w["wup_t"].reshape(NJ, 2 * TC, D)
        if l + 1 < L:
            dpre, dx1, dfcw, from_chips[l + 1] = _ffn_up_bwd(sv["up_pre"], sv["up"], dh, w["fcw"], wup_t3, dz2,
                                                             scatter=pair_sum[l + 1], **up_bwd)
        else:
            dpre, dx1, dfcw = _ffn_up_bwd(sv["up_pre"], sv["up"], dh, w["fcw"], wup_t3, dz2, **up_bwd)
        dfcw = dfcw.transpose(1, 0, 2).reshape(3, F2)
        dw_up_t = _mm_tn(dpre, sv["x1"], name="up_bwd_w", tka=TC, na=2 * NJ, tn=D, ts=1024)
        dz1, dg1, db1 = _ln_bwd(dx1, sv["xh1"], sv["rs1"], w["g1"], name="ln1_bwd")
        dmerged = _mm(dz1, w["wo_t"], out_dtype=BF16, name="wo_bwd_x")
        dw_o = _mm_tn(sv["merged"], dz1, name="wo_bwd_w", tka=D, tn=D // 2, nb=2)
        dpg, dprod, do_a, do_b, do_c = _merge_bwd(dmerged, sv["o_a"], sv["o_b"], sv["o_c"], sv["pg"],
                                                  w["wbr"], w["wbr_t"], name="merge_bwd")
        dw_a = _mm_tn(sv["o_a"], dprod, name="branch_a_bwd_w", tka=256, tn=D, b_off=0)
        dw_b = _mm_tn(sv["o_b"], dprod, name="branch_b_bwd_w", tka=512, tn=D, b_off=1)
        dw_c = _mm_tn(sv["o_c"], dprod, name="branch_c_bwd_w", tka=256, tn=D, b_off=2)
        dpq, dsink = _attn_bwd(sv["pq"], cos_t, sin_t, w["sinks"], do_b, sv["o_b"], sv["lse"], name="attn_bwd")
        dpp, pooled, dmixed, dscale, dconv = _poolconv_bwd(sv["pp"], do_a, do_c, w["wbd"], w["wbd_t"], w["scale"],
                                                           w["conv"], name="poolconv_bwd")
        dwbd = _mm_tn(pooled, dmixed, name="pool_bwd_w", tka=256, tn=256)
        dx_args = ([dpg, dpp, dpq], [w["wg_t"], w["wp_t"], w["wq_t"]], dz1)
        if l > 0:
            below = saved[l - 1]
            ln2_out = _mm_sum(*dx_args, add_scale=alpha, name="proj_in_bwd_x",
                              ln=(below["xh2"], below["rs2"], W[l - 1]["g2"]))
        else:
            dx = _mm_sum(*dx_args, add_scale=alpha, name="proj_in_bwd_x")
        dw_g = _mm_tn(dpg, sv["x0"], name="proj_gate_bwd_w", tka=512, na=6, tn=D)
        dw_p = _mm_tn(dpp, sv["x0"], name="proj_poolconv_bwd_w", tka=512, na=2, tn=D)
        dw_q = _mm_tn(dpq, sv["x0"], name="proj_qkv_bwd_w", tka=384, na=2, tn=D)
        dw_in_t = jnp.concatenate([dw_p[0:256], dw_q, dw_p[256:1024], dw_g], axis=0)
        dw_pool = jnp.stack([dwbd[64 * g:64 * (g + 1), 64 * g:64 * (g + 1)] for g in range(4)])
        gw[l] = dict(w_in_t=dw_in_t, a=dw_a, b=dw_b, c=dw_c, w_o=dw_o, w_up_t=dw_up_t, w_down=dw_dn,
                     w_pool=dw_pool, scale=dscale, sinks=dsink[:, 0], conv=dconv, fcw=_deinterleave_cols(dfcw, NJ),
                     g1=dg1, b1=db1, g2=dg2, b2=db2)
        p_l = pack_grads(gw[l])
        from_sibling = _rs_sibling(p_l, "rs_sibling")
        pair_sum[l] = [_sum_sibling(p, r, my_c, "rs_sum_sibling") for p, r in zip(p_l, from_sibling)]
    grad_x = dx.reshape(1, S, D)
    from_chips[0] = _rs_chips(pair_sum[0], "rs_chips_last")
    g_layers = [[_sum_chips(q, r, my_chip, "rs_sum_chips") for q, r in zip(pair_sum[l], from_chips[l])]
                for l in range(L)]

    def stack(k):
        return jnp.stack([gw[l][k] for l in range(L)])

    rep_vec = jnp.concatenate([
        stack("w_pool").reshape(-1), stack("scale").reshape(-1), stack("g1").reshape(-1), stack("b1").reshape(-1),
        stack("g2").reshape(-1), stack("b2").reshape(-1)])
    n_rep_full = -(-rep_vec.shape[0] // LANES)
    sinks_row = jnp.pad(stack("sinks").reshape(-1), (0, LANES - 8 * L))
    rep_vec = jnp.concatenate([_pad_rows(rep_vec, n_rep_full).reshape(-1), sinks_row, sq_lanes.reshape(-1)])
    loss_row = n_rep_full + 1
    n_rep = -(-(loss_row + 1) // 8) * 8
    rep_rows = _pad_rows(rep_vec, n_rep)
    dconv_by_dev = stack("conv").reshape(L, 3, N_DEV, -1).transpose(2, 0, 1, 3).reshape(N_DEV, -1)
    dfcw_by_dev = stack("fcw").reshape(L, 3, N_DEV, -1).transpose(2, 0, 1, 3).reshape(N_DEV, -1)
    n_mine = -(-(small_rows) // 8) * 8
    by_dev = jnp.concatenate([dconv_by_dev, dfcw_by_dev], axis=1)
    by_dev = jnp.pad(by_dev, ((0, 0), (0, n_mine * LANES - by_dev.shape[1]))).reshape(N_DEV * n_mine, LANES)
    small_g = _all_gather(jnp.concatenate([rep_rows, by_dev], axis=0), "ag_small_grads")
    rep_sum, mine_sum, loss11 = _small_reduce(small_g, n_rep, n_mine, 1.0 / D, loss_row, "small_reduce")
    loss = loss11[0, 0]

    names_big = ["w_in", "w_branch_a", "w_branch_b", "w_branch_c", "w_o", "w_up", "w_down"]
    ms_big = [m_w_in, m_w_branch_a, m_w_branch_b, m_w_branch_c, m_w_o, m_w_up, m_w_down]
    vs_big = [v_w_in, v_w_branch_a, v_w_branch_b, v_w_branch_c, v_w_o, v_w_up, v_w_down]
    out = {}
    for k, name in enumerate(names_big):
        wk = big[k]
        if k in (0, 5):
            g_t = jnp.stack([g[0 if k == 0 else 1] for g in g_layers])
            tr_ = lambda a: jnp.swapaxes(a, 1, 2)
            d, mo, vo = _adamw(tr_(wk), g_t, tr_(ms_big[k]), tr_(vs_big[k]), name="adamw_" + name)
            out[name] = (tr_(g_t), tr_(d), tr_(mo), tr_(vo))
            continue
        else:
            rest_ks = (1, 2, 3, 4, 6)
            o = sum(rows_l[q] for q in rest_ks[:rest_ks.index(k)])
            g_nat = jnp.concatenate([g[2][o:o + rows_l[k]] for g in g_layers], axis=0).reshape(wk.shape)
        d, mo, vo = _adamw(wk, g_nat, ms_big[k], vs_big[k], name="adamw_" + name)
        out[name] = (g_nat, d, mo, vo)

    def rep_pack(wp_, sc_, g1_, b1_, g2_, b2_, sk_):
        v = jnp.concatenate([wp_.reshape(-1), sc_.reshape(-1), g1_.reshape(-1), b1_.reshape(-1), g2_.reshape(-1),
                             b2_.reshape(-1)])
        return _pad_rows(jnp.concatenate([_pad_rows(v, n_rep_full).reshape(-1), sk_.reshape(-1)]), n_rep)

    def mine_pack(cw_, fw_):
        return _pad_rows(jnp.concatenate([cw_.reshape(-1), fw_.reshape(-1)]), n_mine)

    w_rep = rep_pack(w_pool, pool_scale, ln1_g, ln1_b, ln2_g, ln2_b, attn_sinks)
    m_rep = rep_pack(m_w_pool, m_pool_scale, m_ln1_g, m_ln1_b, m_ln2_g, m_ln2_b, m_attn_sinks)
    v_rep = rep_pack(v_w_pool, v_pool_scale, v_ln1_g, v_ln1_b, v_ln2_g, v_ln2_b, v_attn_sinks)
    g_rep = jnp.concatenate([rep_sum[:loss_row], jnp.zeros((n_rep - loss_row, LANES), F32)], axis=0)
    rep_res = (g_rep,) + tuple(_adamw(w_rep, g_rep, m_rep, v_rep, name="adamw_replicated"))
    w_mine = mine_pack(conv_w, ffn_conv_w)
    mine_res = (mine_sum,) + tuple(_adamw(w_mine, mine_sum, mine_pack(m_conv_w, m_ffn_conv_w),
                                          mine_pack(v_conv_w, v_ffn_conv_w), name="adamw_conv"))

    def rep_unpack(buf):
        flat = buf.reshape(-1)
        res, o = {}, 0
        for nm, ref in (("w_pool", w_pool), ("pool_scale", pool_scale), ("ln1_g", ln1_g), ("ln1_b", ln1_b),
                        ("ln2_g", ln2_g), ("ln2_b", ln2_b)):
            res[nm] = flat[o:o + ref.size].reshape(ref.shape)
            o += ref.size
        o = n_rep_full * LANES
        res["attn_sinks"] = flat[o:o + attn_sinks.size].reshape(attn_sinks.shape)
        return res

    def mine_unpack(buf):
        flat = buf.reshape(-1)
        return {"conv_w": flat[:n_cw].reshape(conv_w.shape),
                "ffn_conv_w": flat[n_cw:n_cw + n_fw].reshape(ffn_conv_w.shape)}

    order = ["w_in", "w_pool", "pool_scale", "attn_sinks", "conv_w", "w_branch_a", "w_branch_b", "w_branch_c", "w_o",
             "ln1_g", "ln1_b", "w_up", "ffn_conv_w", "w_down", "ln2_g", "ln2_b"]
    results = [loss, grad_x]
    for kind in range(4):
        rep_k, mine_k = rep_unpack(rep_res[kind]), mine_unpack(mine_res[kind])
        for nm in order:
            if nm in out:
                results.append(out[nm][kind])
            elif nm in rep_k:
                results.append(rep_k[nm])
            else:
                results.append(mine_k[nm])
    return tuple(results)
```

```python
import functools

import jax
import jax.numpy as jnp
from jax import lax
from jax.experimental import pallas as pl
from jax.experimental.pallas import tpu as pltpu

F32 = jnp.float32
BF16 = jnp.bfloat16

HEAD_DIM = 64
N_Q_HEADS = 8
GROUP = 4
WINDOW = 128
ROT_DIM = 16
ROPE_THETA = 500000.0
POOL_WINDOWS = (2, 4, 8, 16)
LN_EPS = 1e-5
MASK_VALUE = -1e30
ADAM_LR, ADAM_B1, ADAM_B2, ADAM_EPS, ADAM_WD, ADAM_STEP = 0.001, 0.9, 0.999, 1e-08, 0.01, 10

N_DEV = 8
LANES = 1024
HALO = 16
MESH = pl.DeviceIdType.MESH
VMEM_LIMIT = 56 * 1024 * 1024


def _div_tile(n, want, mult=8):
    for t in range(min(want, n) // mult * mult, 0, -mult):
        if n % t == 0:
            return t
    return n


def _cp(*sem):
    return pltpu.CompilerParams(dimension_semantics=sem, vmem_limit_bytes=VMEM_LIMIT)


def _coords():
    return lax.axis_index("x"), lax.axis_index("y"), lax.axis_index("c")


def _all_gather(xs, name):
    xs = list(xs) if isinstance(xs, (list, tuple)) else [xs]
    n = len(xs)

    def body(*refs):
        ag_refs = (refs[:n], refs[n:2 * n]) + tuple(refs[2 * n:])
        _ag_start(*ag_refs)
        _ag_finish(*ag_refs)

    res = pl.pallas_call(
        body, name=name,
        out_shape=[jax.ShapeDtypeStruct((N_DEV,) + a.shape, a.dtype) for a in xs],
        in_specs=[pl.BlockSpec(memory_space=pl.ANY)] * n,
        out_specs=[pl.BlockSpec(memory_space=pl.ANY)] * n,
        scratch_shapes=_ag_sems(n),
    )(*xs)
    return res if n > 1 else res[0]


def _ag_sems(n):
    return [pltpu.SemaphoreType.DMA((7 * n,)), pltpu.SemaphoreType.DMA((7 * n,)), pltpu.SemaphoreType.DMA((n,))]


def _ag_copies(x_refs, out_refs, send_sems, recv_sems, local_sems):
    x, y, c = _coords()
    me, sibling = (x, y, c), (x, y, 1 - c)
    chips = [(1 - x, y), (x, 1 - y), (1 - x, 1 - y)]
    per_array = []
    for a, (x_ref, out_ref) in enumerate(zip(x_refs, out_refs)):
        def slot(px, py, pc, out_ref=out_ref):
            return out_ref.at[4 * px + 2 * py + pc]

        def copy(k, block, to, src=None, a=a, slot=slot):
            return pltpu.make_async_remote_copy(
                src_ref=slot(*block) if src is None else src, dst_ref=slot(*block),
                send_sem=send_sems.at[7 * a + k], recv_sem=recv_sems.at[7 * a + k],
                device_id=to, device_id_type=MESH)

        mine = pltpu.make_async_copy(x_ref, slot(*me), local_sems.at[a])
        first = [copy(0, me, sibling, src=x_ref)]
        first += [copy(1 + j, me, (*chip, c), src=x_ref) for j, chip in enumerate(chips)]
        passed = [copy(4 + j, (*chip, c), sibling) for j, chip in enumerate(chips)]
        from_chips = [copy(1 + j, (*chip, c), me) for j, chip in enumerate(chips)]
        from_sibling = [copy(0, sibling, me)] + [copy(4 + j, (*chip, 1 - c), me) for j, chip in enumerate(chips)]
        per_array.append((mine, first, passed, from_chips, from_sibling))
    return per_array


def _ag_start(*refs):
    for mine, first, _, _, _ in _ag_copies(*refs):
        mine.start()
        for cp in first:
            cp.start()


def _ag_finish(*refs):
    per_array = _ag_copies(*refs)
    for j in range(3):
        for _, _, passed, from_chips, _ in per_array:
            from_chips[j].wait_recv()
            passed[j].start()
    for mine, first, passed, _, from_sibling in per_array:
        for cp in from_sibling:
            cp.wait_recv()
        for cp in first + passed:
            cp.wait_send()
        mine.wait()


def _rs_sibling(ps, name):
    n = len(ps)

    def body(*refs):
        swap_refs = (refs[:n], refs[n:2 * n], refs[2 * n], refs[2 * n + 1])
        _swap_start(*swap_refs)
        _swap_finish(*swap_refs)

    return pl.pallas_call(
        body, name=name,
        out_shape=[jax.ShapeDtypeStruct((4,) + p.shape[1:], p.dtype) for p in ps],
        in_specs=[pl.BlockSpec(memory_space=pl.ANY)] * n,
        out_specs=[pl.BlockSpec(memory_space=pl.ANY)] * n,
        scratch_shapes=_swap_sems(n),
    )(*ps)


def _swap_sems(n):
    return [pltpu.SemaphoreType.DMA((4 * n,)), pltpu.SemaphoreType.DMA((4 * n,))]


def _swap_copies(p_refs, out_refs, send_sems, recv_sems):
    x, y, c = _coords()
    return [pltpu.make_async_remote_copy(
        src_ref=p_ref.at[4 * (j // 2) + 2 * (j % 2) + (1 - c)], dst_ref=out_ref.at[j],
        send_sem=send_sems.at[4 * a + j], recv_sem=recv_sems.at[4 * a + j],
        device_id=(x, y, 1 - c), device_id_type=MESH)
        for a, (p_ref, out_ref) in enumerate(zip(p_refs, out_refs)) for j in range(4)]


def _swap_start(*refs):
    for cp in _swap_copies(*refs):
        cp.start()


def _swap_finish(*refs):
    copies = _swap_copies(*refs)
    for cp in copies:
        cp.wait_recv()
    for cp in copies:
        cp.wait_send()


def _hosted(kind, arrays):
    n = len(arrays)
    if kind == "gather":
        return _ag_start, _ag_finish, [(N_DEV,) + a.shape for a in arrays], _ag_sems(n)
    assert kind == "swap"
    return _swap_start, _swap_finish, [(4,) + a.shape[1:] for a in arrays], _swap_sems(n)


def _rs_chips(qs, name):
    n = len(qs)

    def body(*refs):
        rs_refs = (refs[:n], refs[n:2 * n], refs[2 * n], refs[2 * n + 1])
        _rs_chips_start(*rs_refs)
        _rs_chips_finish(*rs_refs)

    return pl.pallas_call(
        body, name=name,
        out_shape=[jax.ShapeDtypeStruct((3,) + q.shape[1:], q.dtype) for q in qs],
        in_specs=[pl.BlockSpec(memory_space=pl.ANY)] * n,
        out_specs=[pl.BlockSpec(memory_space=pl.ANY)] * n,
        scratch_shapes=_rs_sems(n),
    )(*qs)


def _rs_sems(n):
    return [pltpu.SemaphoreType.DMA((3 * n,)), pltpu.SemaphoreType.DMA((3 * n,))]


def _rs_chips_copies(q_refs, out_refs, send_sems, recv_sems):
    x, y, c = _coords()
    chips = [(1 - x, y), (x, 1 - y), (1 - x, 1 - y)]
    return [pltpu.make_async_remote_copy(
        src_ref=q_ref.at[2 * cx + cy], dst_ref=out_ref.at[k],
        send_sem=send_sems.at[3 * a + k], recv_sem=recv_sems.at[3 * a + k], device_id=(cx, cy, c),
        device_id_type=MESH)
        for a, (q_ref, out_ref) in enumerate(zip(q_refs, out_refs)) for k, (cx, cy) in enumerate(chips)]


def _rs_chips_start(*refs):
    for cp in _rs_chips_copies(*refs):
        cp.start()


def _rs_chips_finish(*refs):
    copies = _rs_chips_copies(*refs)
    for cp in copies:
        cp.wait_recv()
    for cp in copies:
        cp.wait_send()


def _sum_sibling(p, recv, my_c, name, tr=512):
    _, R, C = p.shape
    tr = _div_tile(R, tr, 16)

    def body(c_ref, p_ref, r_ref, o_ref):
        o_ref[...] = (p_ref[...].astype(F32) + r_ref[...].astype(F32)).astype(o_ref.dtype)

    grid_spec = pltpu.PrefetchScalarGridSpec(
        num_scalar_prefetch=1, grid=(4, R // tr),
        in_specs=[pl.BlockSpec((1, tr, C), lambda j, r, c_ref: (4 * (j // 2) + 2 * (j % 2) + c_ref[0], r, 0)),
                  pl.BlockSpec((1, tr, C), lambda j, r, c_ref: (j, r, 0))],
        out_specs=pl.BlockSpec((1, tr, C), lambda j, r, c_ref: (j, r, 0)))
    return pl.pallas_call(body, name=name, grid_spec=grid_spec,
                          out_shape=jax.ShapeDtypeStruct((4, R, C), p.dtype),
                          compiler_params=_cp("parallel", "parallel"))(my_c, p, recv)


def _sum_chips(q, recv, my_chip, name, tr=512):
    _, R, C = q.shape
    tr = _div_tile(R, tr, 16)

    def body(i_ref, q_ref, r_ref, o_ref):
        acc = q_ref[0].astype(F32)
        for k in range(3):
            acc = acc + r_ref[k].astype(F32)
        o_ref[...] = acc

    grid_spec = pltpu.PrefetchScalarGridSpec(
        num_scalar_prefetch=1, grid=(R // tr,),
        in_specs=[pl.BlockSpec((1, tr, C), lambda r, i_ref: (i_ref[0], r, 0)),
                  pl.BlockSpec((3, tr, C), lambda r, i_ref: (0, r, 0))],
        out_specs=pl.BlockSpec((tr, C), lambda r, i_ref: (r, 0)))
    return pl.pallas_call(body, name=name, grid_spec=grid_spec,
                          out_shape=jax.ShapeDtypeStruct((R, C), F32),
                          compiler_params=_cp("parallel"))(my_chip, q, recv)


def _small_reduce(g, n_rep, n_mine, inv_d, loss_row, name):
    _, R, C = g.shape

    def body(g_ref, rep_ref, mine_ref, loss_ref):
        x, y, c = _coords()
        start = pl.multiple_of(n_rep + (4 * x + 2 * y + c) * n_mine, 8)
        rep = g_ref[0, 0:n_rep, :]
        mine = g_ref[0, pl.ds(start, n_mine), :]
        sq = g_ref[0, loss_row:loss_row + 1, :]
        for d in range(1, N_DEV):
            rep = rep + g_ref[d, 0:n_rep, :]
            mine = mine + g_ref[d, pl.ds(start, n_mine), :]
            sq = sq + g_ref[d, loss_row:loss_row + 1, :]
        rep_ref[...] = rep
        mine_ref[...] = mine
        loss_ref[...] = (0.5 * inv_d) * jnp.sum(sq, axis=1, keepdims=True)

    return pl.pallas_call(
        body, name=name,
        out_shape=(jax.ShapeDtypeStruct((n_rep, C), F32), jax.ShapeDtypeStruct((n_mine, C), F32),
                   jax.ShapeDtypeStruct((1, 1), F32)),
        compiler_params=pltpu.CompilerParams(vmem_limit_bytes=VMEM_LIMIT),
    )(g)


def _mm(a, b, *, out_dtype, name, tm=512, tn=None, tk=None, add=None, add_scale=1.0, gather=None, swap=None):
    M, K = a.shape
    N = b.shape[1]
    tm = min(tm, M)
    tn = N if tn is None else tn
    tk = K if tk is None else tk
    nk = K // tk
    has_add = add is not None
    hosted = gather if gather is not None else swap
    has_ag = hosted is not None
    n_g = len(hosted) if has_ag else 0
    if has_ag:
        comm_start, comm_finish, comm_shapes, comm_sems = _hosted("gather" if gather is not None else "swap", hosted)
    n_i, n_j = M // tm, N // tn

    def body(*refs):
        a_ref, b_ref = refs[0], refs[1]
        add_ref = refs[2] if has_add else None
        n_in = 2 + has_add + n_g
        o_ref = refs[n_in]
        if has_ag:
            ag_refs = (refs[n_in - n_g:n_in], refs[n_in + 1:n_in + 1 + n_g]) + tuple(
                refs[n_in + 1 + n_g:n_in + 1 + n_g + len(comm_sems)])
            pid = (pl.program_id(0), pl.program_id(1), pl.program_id(2))

            @pl.when((pid[0] == 0) & (pid[1] == 0) & (pid[2] == 0))
            def _():
                comm_start(*ag_refs)

        part = jnp.dot(a_ref[...].astype(BF16), b_ref[...].astype(BF16), preferred_element_type=F32)

        def finish(r):
            if has_add:
                r = r + add_scale * add_ref[...].astype(F32)
            o_ref[...] = r.astype(out_dtype)

        if nk == 1:
            finish(part)
        else:
            acc_ref = refs[-1]
            k = pl.program_id(2)

            @pl.when(k == 0)
            def _():
                acc_ref[...] = part

            @pl.when(k > 0)
            def _():
                acc_ref[...] += part

            @pl.when(k == nk - 1)
            def _():
                finish(acc_ref[...])

        if has_ag:
            @pl.when((pid[0] == n_i - 1) & (pid[1] == n_j - 1) & (pid[2] == nk - 1))
            def _():
                comm_finish(*ag_refs)

    b_mode = dict(pipeline_mode=pl.Buffered(1)) if (n_j == 1 and nk == 1) else {}
    in_specs = [pl.BlockSpec((tm, tk), lambda i, j, k: (i, k)),
                pl.BlockSpec((tk, tn), lambda i, j, k: (k, j), **b_mode)]
    args = [a, b]
    if has_add:
        in_specs.append(pl.BlockSpec((tm, tn), lambda i, j, k: (i, j)))
        args.append(add)
    out_specs = [pl.BlockSpec((tm, tn), lambda i, j, k: (i, j))]
    out_shape = [jax.ShapeDtypeStruct((M, N), out_dtype)]
    scratch = []
    if has_ag:
        in_specs += [pl.BlockSpec(memory_space=pl.ANY)] * n_g
        args += list(hosted)
        out_specs += [pl.BlockSpec(memory_space=pl.ANY)] * n_g
        out_shape += [jax.ShapeDtypeStruct(s, g.dtype) for s, g in zip(comm_shapes, hosted)]
        scratch += comm_sems
    if nk > 1:
        scratch.append(pltpu.VMEM((tm, tn), F32))
    sem = ("arbitrary",) * 3 if has_ag else ("parallel", "parallel", "arbitrary")
    res = pl.pallas_call(
        body, name=name, grid=(n_i, n_j, nk), in_specs=in_specs, out_specs=out_specs, out_shape=out_shape,
        scratch_shapes=scratch, compiler_params=_cp(*sem),
    )(*args)
    return (res[0], list(res[1:])) if has_ag else res[0]


def _mm_fan(a, bs, *, out_dtype, name, tm=512, gather=None):
    M, K = a.shape
    tm = min(tm, M)
    n = len(bs)
    n_i = M // tm
    n_g = len(gather) if gather is not None else 0

    def body(*refs):
        outs = refs[1 + n + n_g:1 + 2 * n + n_g]
        if n_g:
            ag_refs = (refs[1 + n:1 + n + n_g], refs[1 + 2 * n + n_g:1 + 2 * n + 2 * n_g]) + tuple(
                refs[1 + 2 * n + 2 * n_g:])

            @pl.when(pl.program_id(0) == 0)
            def _():
                _ag_start(*ag_refs)

        a_v = refs[0][...].astype(BF16)
        for k in range(n):
            outs[k][...] = jnp.dot(a_v, refs[1 + k][...].astype(BF16), preferred_element_type=F32).astype(out_dtype)

        if n_g:
            @pl.when(pl.program_id(0) == n_i - 1)
            def _():
                _ag_finish(*ag_refs)

    row = lambda i: (i, 0)
    hbm = pl.BlockSpec(memory_space=pl.ANY)
    res = pl.pallas_call(
        body, name=name, grid=(n_i,),
        in_specs=[pl.BlockSpec((tm, K), row)] + [pl.BlockSpec(b.shape, lambda i: (0, 0)) for b in bs] + [hbm] * n_g,
        out_specs=[pl.BlockSpec((tm, b.shape[1]), row) for b in bs] + [hbm] * n_g,
        out_shape=([jax.ShapeDtypeStruct((M, b.shape[1]), out_dtype) for b in bs]
                   + [jax.ShapeDtypeStruct((N_DEV,) + g.shape, g.dtype) for g in (gather or [])]),
        scratch_shapes=_ag_sems(n_g) if n_g else [],
        compiler_params=_cp("arbitrary" if n_g else "parallel"),
    )(a, *bs, *(gather or []))
    return (list(res[:n]), list(res[n:])) if n_g else list(res)


def _mm_sum(xs, bs, add, *, add_scale, name, tm=512, ln=None):
    M = xs[0].shape[0]
    N = bs[0].shape[1]
    tm = min(tm, M)
    n = len(xs)

    def body(*refs):
        acc = add_scale * refs[2 * n][...]
        for k in range(n):
            acc = acc + jnp.dot(refs[k][...].astype(BF16), refs[n + k][...].astype(BF16), preferred_element_type=F32)
        if ln is None:
            refs[2 * n + 1][...] = acc
        else:
            xh_ref, rs_ref, g_ref, dz_ref, dg_ref, db_ref = refs[2 * n + 1:]
            _ln_bwd_tile(acc, xh_ref, rs_ref, g_ref, dz_ref, dg_ref, db_ref, pl.program_id(0) == 0)

    row = lambda i: (i, 0)
    vec = lambda i: (0, 0)
    in_specs = ([pl.BlockSpec((tm, x.shape[1]), row) for x in xs]
                + [pl.BlockSpec(b.shape, vec) for b in bs] + [pl.BlockSpec((tm, N), row)])
    if ln is None:
        return pl.pallas_call(
            body, name=name, grid=(M // tm,), in_specs=in_specs, out_specs=pl.BlockSpec((tm, N), row),
            out_shape=jax.ShapeDtypeStruct((M, N), F32), compiler_params=_cp("parallel"),
        )(*xs, *bs, add)
    in_specs += [pl.BlockSpec((tm, N), row), pl.BlockSpec((tm, 1), row), pl.BlockSpec((1, N), vec)]
    return pl.pallas_call(
        body, name=name, grid=(M // tm,), in_specs=in_specs,
        out_specs=[pl.BlockSpec((tm, N), row), pl.BlockSpec((1, N), vec), pl.BlockSpec((1, N), vec)],
        out_shape=(jax.ShapeDtypeStruct((M, N), F32), jax.ShapeDtypeStruct((1, N), F32),
                   jax.ShapeDtypeStruct((1, N), F32)),
        compiler_params=_cp("arbitrary"),
    )(*xs, *bs, add, *ln)


def _ln_bwd_tile(dyv, xh_ref, rs_ref, g_ref, dz_ref, dg_ref, db_ref, first):
    @pl.when(first)
    def _():
        dg_ref[...] = jnp.zeros_like(dg_ref)
        db_ref[...] = jnp.zeros_like(db_ref)

    xh = xh_ref[...].astype(F32)
    dyg = dyv * g_ref[...]
    c1 = jnp.mean(dyg, axis=-1, keepdims=True)
    c2 = jnp.mean(dyg * xh, axis=-1, keepdims=True)
    dz_ref[...] = rs_ref[...] * (dyg - c1 - xh * c2)
    dg_ref[...] += jnp.sum(dyv * xh, axis=0, keepdims=True)
    db_ref[...] += jnp.sum(dyv, axis=0, keepdims=True)


def _mm_ln(a, b, resid, gamma, beta, *, alpha, name, tm=512, tk=None):
    M, K = a.shape
    D = b.shape[1]
    tm = min(tm, M)
    tk = K if tk is None else tk
    nk = K // tk

    def body(a_ref, b_ref, r_ref, g_ref, be_ref, y_ref, xh_ref, rs_ref, *scratch):
        part = jnp.dot(a_ref[...].astype(BF16), b_ref[...].astype(BF16), preferred_element_type=F32)

        def finish(acc):
            z = alpha * r_ref[...] + acc
            mu = jnp.mean(z, axis=-1, keepdims=True)
            zc = z - mu
            var = jnp.mean(zc * zc, axis=-1, keepdims=True)
            rstd = lax.rsqrt(var + LN_EPS)
            xhat = zc * rstd
            y_ref[...] = xhat * g_ref[...] + be_ref[...]
            xh_ref[...] = xhat.astype(BF16)
            rs_ref[...] = rstd

        if nk == 1:
            finish(part)
        else:
            acc_ref = scratch[0]
            k = pl.program_id(1)

            @pl.when(k == 0)
            def _():
                acc_ref[...] = part

            @pl.when(k > 0)
            def _():
                acc_ref[...] += part

            @pl.when(k == nk - 1)
            def _():
                finish(acc_ref[...])

    row = lambda i, k: (i, 0)
    vec = lambda i, k: (0, 0)
    return pl.pallas_call(
        body, name=name, grid=(M // tm, nk),
        in_specs=[pl.BlockSpec((tm, tk), lambda i, k: (i, k)), pl.BlockSpec((tk, D), lambda i, k: (k, 0)),
                  pl.BlockSpec((tm, D), row), pl.BlockSpec((1, D), vec), pl.BlockSpec((1, D), vec)],
        out_specs=[pl.BlockSpec((tm, D), row), pl.BlockSpec((tm, D), row), pl.BlockSpec((tm, 1), row)],
        out_shape=(jax.ShapeDtypeStruct((M, D), F32), jax.ShapeDtypeStruct((M, D), BF16),
                   jax.ShapeDtypeStruct((M, 1), F32)),
        scratch_shapes=[pltpu.VMEM((tm, D), F32)] if nk > 1 else [],
        compiler_params=_cp("parallel", "arbitrary"),
    )(a, b, resid, gamma, beta)


def _mm_tn(a, b, *, name, tka, tn, a_off=0, na=1, b_off=0, nb=1, ts=2048):
    S = a.shape[0]
    ts = min(ts, S)

    def body(a_ref, b_ref, o_ref):
        s = pl.program_id(2)
        part = lax.dot_general(a_ref[...].astype(BF16), b_ref[...].astype(BF16),
                               (((0,), (0,)), ((), ())), preferred_element_type=F32)

        @pl.when(s == 0)
        def _():
            o_ref[...] = part

        @pl.when(s > 0)
        def _():
            o_ref[...] += part

    return pl.pallas_call(
        body, name=name, grid=(na, nb, S // ts),
        in_specs=[pl.BlockSpec((ts, tka), lambda i, j, s: (s, a_off + i)),
                  pl.BlockSpec((ts, tn), lambda i, j, s: (s, b_off + j))],
        out_specs=pl.BlockSpec((tka, tn), lambda i, j, s: (i, j)),
        out_shape=jax.ShapeDtypeStruct((na * tka, nb * tn), F32),
        compiler_params=_cp("parallel", "parallel", "arbitrary"),
    )(a, b)


def _rope_tables(pos, inv_lane, sign_lane, name, ts=512):
    S = pos.shape[0]
    ts = min(ts, S)

    def body(p_ref, inv_ref, sg_ref, cos_ref, sin_ref):
        ang = p_ref[...].astype(F32) * inv_ref[...]
        cos_ref[...] = jnp.cos(ang)
        sin_ref[...] = jnp.sin(ang) * sg_ref[...]

    return pl.pallas_call(
        body, name=name, grid=(S // ts,),
        in_specs=[pl.BlockSpec((ts, 1), lambda i: (i, 0)), pl.BlockSpec((1, 128), lambda i: (0, 0)),
                  pl.BlockSpec((1, 128), lambda i: (0, 0))],
        out_specs=[pl.BlockSpec((ts, 128), lambda i: (i, 0))] * 2,
        out_shape=(jax.ShapeDtypeStruct((S, 128), F32),) * 2,
        compiler_params=_cp("parallel"),
    )(pos, inv_lane, sign_lane)


def _rope_swap(t):
    lane = lax.broadcasted_iota(jnp.int32, (1, 128), 1)
    lo = (lane % HEAD_DIM) < (ROT_DIM // 2)
    return jnp.where(lo, pltpu.roll(t, 128 - ROT_DIM // 2, 1), pltpu.roll(t, ROT_DIM // 2, 1))


def _rope_fwd(t, cos, sin):
    return t * cos + _rope_swap(t) * sin


def _rope_bwd(d, cos, sin):
    lane = lax.broadcasted_iota(jnp.int32, (1, 128), 1)
    return d * cos + jnp.where((lane % HEAD_DIM) < ROT_DIM, _rope_swap(d * sin), 0.0)


def _tile_heads(t):
    lane = lax.broadcasted_iota(jnp.int32, (1, 128), 1)
    r = pltpu.roll(t, 64, 1)
    h0 = jnp.where(lane < 64, t, r)
    h1 = jnp.where(lane < 64, r, t)
    return jnp.concatenate([h0, h0], axis=1), jnp.concatenate([h1, h1], axis=1)


def _fold_heads(d0, d1):
    lane = lax.broadcasted_iota(jnp.int32, (1, 128), 1)

    def fold(d):
        s = d[:, 0:128] + d[:, 128:256]
        return s + pltpu.roll(s, 64, 1)

    return jnp.where(lane < 64, fold(d0), fold(d1))


def _band4(n_keys):
    row = lax.broadcasted_iota(jnp.int32, (GROUP * WINDOW, n_keys), 0) % WINDOW
    col = lax.broadcasted_iota(jnp.int32, (GROUP * WINDOW, n_keys), 1)
    return (col > row) & (col <= row + WINDOW), col


def _head_masks():
    lane = lax.broadcasted_iota(jnp.int32, (1, GROUP * HEAD_DIM), 1)
    return [(lane // HEAD_DIM) == hl for hl in range(GROUP)]


def _stack_heads(t):
    zero = jnp.zeros_like(t)
    return jnp.concatenate([jnp.where(hm, t, zero) for hm in _head_masks()], axis=0)


def _unstack_heads(t4):
    out = None
    for hl, hm in enumerate(_head_masks()):
        part = jnp.where(hm, t4[hl * WINDOW:(hl + 1) * WINDOW], 0.0)
        out = part if out is None else out + part
    return out


def _sink_block(sink_ref, g):
    return jnp.concatenate([jnp.broadcast_to(sink_ref[g * GROUP + hl:g * GROUP + hl + 1, 0:1], (WINDOW, 256))
                            for hl in range(GROUP)], axis=0)


def _sink_column(sink_ref, g):
    return jnp.concatenate([jnp.broadcast_to(sink_ref[g * GROUP + hl:g * GROUP + hl + 1, 0:1], (WINDOW, 1))
                            for hl in range(GROUP)], axis=0)


def _attn_fwd(pq, cos_t, sin_t, sinks_b, *, name, ts=256):
    S = pq.shape[0]
    ts = min(ts, S)
    nq = ts // WINDOW
    scale = HEAD_DIM ** -0.5

    def body(cur_ref, prev_ref, cosc_ref, sinc_ref, cosp_ref, sinp_ref, sink_ref, o_ref, lse_ref):
        i = pl.program_id(0)
        cosc, sinc = cosc_ref[...], sinc_ref[...]
        q = cur_ref[:, 0:512].astype(F32)
        qr = jnp.concatenate(
            [_rope_fwd(q[:, j * 128:(j + 1) * 128], cosc, sinc) for j in range(4)], axis=1) * scale
        qr = qr.astype(BF16)
        kc = _rope_fwd(cur_ref[:, 512:640].astype(F32), cosc, sinc)
        kp = _rope_fwd(prev_ref[:, 0:128].astype(F32), cosp_ref[...], sinp_ref[...])
        k_all = jnp.concatenate([kp, kc], axis=0)
        v_all = jnp.concatenate([prev_ref[:, 128:256].astype(F32), cur_ref[:, 640:768].astype(F32)], axis=0)
        kt = [t.astype(BF16) for t in _tile_heads(k_all)]
        vt = [t.astype(BF16) for t in _tile_heads(v_all)]
        band, col = _band4(2 * WINDOW)
        ones = jnp.ones((2 * WINDOW, 256), BF16)
        key_t = lax.broadcasted_iota(jnp.int32, (2 * WINDOW, GROUP * WINDOW), 0)
        qry_t = lax.broadcasted_iota(jnp.int32, (2 * WINDOW, GROUP * WINDOW), 1) % WINDOW
        band_t = (key_t > qry_t) & (key_t <= qry_t + WINDOW)
        NT = (((1,), (1,)), ((), ()))
        for qb in range(nq):
            rows = slice(qb * WINDOW, (qb + 1) * WINDOW)
            keys = slice(qb * WINDOW, (qb + 2) * WINDOW)
            valid = band & ((col >= WINDOW) | (i * nq + qb > 0))
            valid_t = band_t & ((key_t >= WINDOW) | (i * nq + qb > 0))
            for g in range(2):
                qs = _stack_heads(qr[rows, g * 256:(g + 1) * 256])
                sink = _sink_block(sink_ref, g)
                s = lax.dot_general(qs, kt[g][keys], NT, preferred_element_type=F32)
                s_t = lax.dot_general(kt[g][keys], qs, NT, preferred_element_type=F32)
                m_t = jnp.max(jnp.where(valid_t, s_t, MASK_VALUE), axis=0, keepdims=True)
                m_rep = jnp.broadcast_to(m_t, (WINDOW, GROUP * WINDOW)).T
                m = jnp.maximum(jnp.concatenate([m_rep, m_rep], axis=1), sink)
                e = jnp.exp(jnp.where(valid, s, MASK_VALUE) - m).astype(BF16)
                l = jnp.dot(e, ones, preferred_element_type=F32) + jnp.exp(sink - m)
                pv = jnp.dot(e, vt[g][keys], preferred_element_type=F32)
                o_ref[rows, g * 256:(g + 1) * 256] = (_unstack_heads(pv) / _unstack_heads(l)).astype(BF16)
                lse4 = (m + jnp.log(l))[:, 0:1]
                for hl in range(GROUP):
                    h = g * GROUP + hl
                    lse_ref[rows, h:h + 1] = lse4[hl * WINDOW:(hl + 1) * WINDOW]

    hb = ts // WINDOW
    cur = lambda i: (i, 0)
    prev = lambda i: (jnp.maximum(i * hb - 1, 0), 0)
    return pl.pallas_call(
        body, name=name, grid=(S // ts,),
        in_specs=[pl.BlockSpec((ts, 768), cur),
                  pl.BlockSpec((WINDOW, 256), lambda i: (jnp.maximum(i * hb - 1, 0), 2)),
                  pl.BlockSpec((ts, 128), cur), pl.BlockSpec((ts, 128), cur),
                  pl.BlockSpec((WINDOW, 128), prev), pl.BlockSpec((WINDOW, 128), prev),
                  pl.BlockSpec((8, 128), lambda i: (0, 0))],
        out_specs=[pl.BlockSpec((ts, 512), cur), pl.BlockSpec((ts, 8), cur)],
        out_shape=(jax.ShapeDtypeStruct((S, 512), BF16), jax.ShapeDtypeStruct((S, 8), F32)),
        compiler_params=_cp("parallel"),
    )(pq, pq, cos_t, sin_t, cos_t, sin_t, sinks_b)


def _attn_bwd(pq, cos_t, sin_t, sinks_b, do, o, lse, *, name, ts=256):
    S = pq.shape[0]
    ts = min(ts, S)
    nq = ts // WINDOW
    nt = S // ts
    scale = HEAD_DIM ** -0.5
    NT = (((1,), (1,)), ((), ()))
    TN = (((0,), (0,)), ((), ()))

    def body(cur_ref, prev_ref, nxt_ref, cosc_ref, sinc_ref, cosp_ref, sinp_ref, cosn_ref, sinn_ref, sink_ref,
             doc_ref, don_ref, oc_ref, on_ref, lsec_ref, lsen_ref, dpq_ref, dsink_ref):
        i = pl.program_id(0)
        last = i == nt - 1
        cosc, sinc = cosc_ref[...], sinc_ref[...]
        cose = jnp.concatenate([cosc, cosn_ref[...]], axis=0)
        sine = jnp.concatenate([sinc, sinn_ref[...]], axis=0)
        q = jnp.concatenate([cur_ref[:, 0:512], nxt_ref[:, 0:512]], axis=0).astype(F32)
        qr = jnp.concatenate(
            [_rope_fwd(q[:, j * 128:(j + 1) * 128], cose, sine) for j in range(4)], axis=1) * scale
        qr = qr.astype(BF16)
        kc = _rope_fwd(cur_ref[:, 512:640].astype(F32), cosc, sinc)
        kp = _rope_fwd(prev_ref[:, 0:128].astype(F32), cosp_ref[...], sinp_ref[...])
        k_all = jnp.concatenate([kp, kc], axis=0)
        v_all = jnp.concatenate([prev_ref[:, 128:256].astype(F32), cur_ref[:, 640:768].astype(F32)], axis=0)
        kt = [t.astype(BF16) for t in _tile_heads(k_all)]
        vt = [t.astype(BF16) for t in _tile_heads(v_all)]
        don = jnp.where(last, jnp.zeros_like(don_ref[...]), don_ref[...])
        do_e = jnp.concatenate([doc_ref[...], don], axis=0)
        o_e = jnp.concatenate([oc_ref[...], on_ref[...]], axis=0)
        band2, col2 = _band4(2 * WINDOW)
        band1, _ = _band4(WINDOW)
        ones = jnp.ones((256, 256), BF16)

        @pl.when(i == 0)
        def _():
            dsink_ref[...] = jnp.zeros_like(dsink_ref)

        dk_acc = [[None] * (nq + 1) for _ in range(2)]
        dv_acc = [[None] * (nq + 1) for _ in range(2)]

        def add(acc, g, e, val):
            acc[g][e] = val if acc[g][e] is None else acc[g][e] + val

        for qb in range(nq + 1):
            halo = qb == nq
            rows = slice(qb * WINDOW, (qb + 1) * WINDOW)
            if halo:
                keys = slice(qb * WINDOW, (qb + 1) * WINDOW)
                valid = band1 & jnp.logical_not(last)
            else:
                keys = slice(qb * WINDOW, (qb + 2) * WINDOW)
                valid = band2 & ((col2 >= WINDOW) | (i * nq + qb > 0))
            dq_parts = []
            for g in range(2):
                qs = _stack_heads(qr[rows, g * 256:(g + 1) * 256])
                dos = _stack_heads(do_e[rows, g * 256:(g + 1) * 256])
                o_g = o_e[rows, g * 256:(g + 1) * 256].astype(F32)
                kt_b, vt_b = kt[g][keys], vt[g][keys]
                lse_src = lsen_ref if halo else lsec_ref
                lse_rows = slice(0, WINDOW) if halo else rows
                big_l = jnp.concatenate([lse_src[lse_rows, g * GROUP + hl:g * GROUP + hl + 1] for hl in range(GROUP)],
                                        axis=0)
                delta = jnp.dot((dos.astype(F32) * jnp.concatenate([o_g] * GROUP, axis=0)).astype(BF16), ones,
                                preferred_element_type=F32)
                s = lax.dot_general(qs, kt_b, NT, preferred_element_type=F32)
                p = jnp.exp(jnp.where(valid, s, MASK_VALUE) - big_l)
                dp = lax.dot_general(dos, vt_b, NT, preferred_element_type=F32)
                ds = (p * (dp - delta[:, 0:p.shape[1]])).astype(BF16)
                dk_g = lax.dot_general(ds, qs, TN, preferred_element_type=F32)
                dv_g = lax.dot_general(p.astype(BF16), dos, TN, preferred_element_type=F32)
                if not halo:
                    dq_parts.append(_unstack_heads(jnp.dot(ds, kt_b, preferred_element_type=F32)))
                    dsink4 = jnp.exp(_sink_column(sink_ref, g) - big_l) * delta[:, 0:1]
                    for hl in range(GROUP):
                        h = g * GROUP + hl
                        dsink_h = -jnp.sum(dsink4[hl * WINDOW:(hl + 1) * WINDOW], axis=0, keepdims=True)
                        dsink_ref[h:h + 1, :] += jnp.broadcast_to(dsink_h, (1, 128))
                add(dk_acc, g, qb, dk_g[0:WINDOW])
                add(dv_acc, g, qb, dv_g[0:WINDOW])
                if not halo:
                    add(dk_acc, g, qb + 1, dk_g[WINDOW:2 * WINDOW])
                    add(dv_acc, g, qb + 1, dv_g[WINDOW:2 * WINDOW])
            if not halo:
                cs, sn = cosc[rows], sinc[rows]
                for g in range(2):
                    dq_g = dq_parts[g] * scale
                    for j in range(2):
                        c0 = g * 256 + j * 128
                        dpq_ref[rows, c0:c0 + 128] = _rope_bwd(dq_g[:, j * 128:(j + 1) * 128], cs, sn).astype(BF16)
        for e in range(1, nq + 1):
            rows = slice((e - 1) * WINDOW, e * WINDOW)
            dk = _fold_heads(dk_acc[0][e], dk_acc[1][e])
            dv = _fold_heads(dv_acc[0][e], dv_acc[1][e])
            dpq_ref[rows, 512:640] = _rope_bwd(dk, cosc[rows], sinc[rows]).astype(BF16)
            dpq_ref[rows, 640:768] = dv.astype(BF16)

    hb = ts // WINDOW
    nblk = S // WINDOW
    cur = lambda i: (i, 0)
    prev = lambda i: (jnp.maximum(i * hb - 1, 0), 0)
    nxt = lambda i: (jnp.minimum((i + 1) * hb, nblk - 1), 0)
    return pl.pallas_call(
        body, name=name, grid=(nt,),
        in_specs=[pl.BlockSpec((ts, 768), cur),
                  pl.BlockSpec((WINDOW, 256), lambda i: (jnp.maximum(i * hb - 1, 0), 2)),
                  pl.BlockSpec((WINDOW, 768), nxt),
                  pl.BlockSpec((ts, 128), cur), pl.BlockSpec((ts, 128), cur),
                  pl.BlockSpec((WINDOW, 128), prev), pl.BlockSpec((WINDOW, 128), prev),
                  pl.BlockSpec((WINDOW, 128), nxt), pl.BlockSpec((WINDOW, 128), nxt),
                  pl.BlockSpec((8, 128), lambda i: (0, 0)),
                  pl.BlockSpec((ts, 512), cur), pl.BlockSpec((WINDOW, 512), nxt),
                  pl.BlockSpec((ts, 512), cur), pl.BlockSpec((WINDOW, 512), nxt),
                  pl.BlockSpec((ts, 8), cur), pl.BlockSpec((WINDOW, 8), nxt)],
        out_specs=[pl.BlockSpec((ts, 768), cur), pl.BlockSpec((8, 128), lambda i: (0, 0))],
        out_shape=(jax.ShapeDtypeStruct((S, 768), BF16), jax.ShapeDtypeStruct((8, 128), F32)),
        compiler_params=_cp("arbitrary"),
    )(pq, pq, pq, cos_t, sin_t, cos_t, sin_t, cos_t, sin_t, sinks_b, do, do, o, o, lse, lse)


def _shift_dn(x, k):
    return pltpu.roll(x, k, 0)


def _shift_up(x, k):
    return pltpu.roll(x, x.shape[0] - k, 0)


def _pool_lane_select(vals):
    lane = lax.broadcasted_iota(jnp.int32, (1, 256), 1)
    out = vals[3]
    for g in (2, 1, 0):
        out = jnp.where(lane < 64 * (g + 1), vals[g], out)
    return out


def _pool_inv_count(t0, n):
    t = t0 + lax.broadcasted_iota(jnp.int32, (n, 256), 0)
    lane = lax.broadcasted_iota(jnp.int32, (n, 256), 1)
    w = jnp.where(lane < 64, 2, jnp.where(lane < 128, 4, jnp.where(lane < 192, 8, 16)))
    return 1.0 / jnp.minimum(t + 1, w).astype(F32)


def _pooled(u_ext, t0, n):
    s2 = u_ext + _shift_dn(u_ext, 1)
    s4 = s2 + _shift_dn(s2, 2)
    s8 = s4 + _shift_dn(s4, 4)
    s16 = s8 + _shift_dn(s8, 8)
    win = _pool_lane_select([s2, s4, s8, s16])[HALO:HALO + n]
    return win * _pool_inv_count(t0, n) - u_ext[HALO:HALO + n]


def _poolconv_fwd(pp, wbd, pool_scale, conv_w, *, name, ts=512):
    S = pp.shape[0]
    ts = min(ts, S)

    def body(cur_ref, prev_ref, wbd_ref, sc_ref, cw_ref, oa_ref, oc_ref):
        i = pl.program_id(0)
        prev = jnp.where(i > 0, prev_ref[...].astype(F32), 0.0)
        u_ext = jnp.concatenate([prev[:, 0:256], cur_ref[:, 0:256].astype(F32)], axis=0)
        pooled = _pooled(u_ext, i * ts, ts)
        mixed = jnp.dot(pooled.astype(BF16), wbd_ref[...], preferred_element_type=F32)
        oa_ref[...] = (mixed * sc_ref[...]).astype(BF16)
        v_ext = jnp.concatenate([prev[:, 256:512] * prev[:, 768:1024],
                                 cur_ref[:, 256:512].astype(F32) * cur_ref[:, 768:1024].astype(F32)], axis=0)
        cv = cw_ref[2:3, :] * v_ext + cw_ref[1:2, :] * _shift_dn(v_ext, 1) + cw_ref[0:1, :] * _shift_dn(v_ext, 2)
        oc_ref[...] = (cur_ref[:, 512:768].astype(F32) * cv[HALO:HALO + ts]).astype(BF16)

    hb = ts // HALO
    cur = lambda i: (i, 0)
    const = lambda i: (0, 0)
    return pl.pallas_call(
        body, name=name, grid=(S // ts,),
        in_specs=[pl.BlockSpec((ts, 1024), cur),
                  pl.BlockSpec((HALO, 1024), lambda i: (jnp.maximum(i * hb - 1, 0), 0)),
                  pl.BlockSpec((256, 256), const), pl.BlockSpec((1, 256), const), pl.BlockSpec((3, 256), const)],
        out_specs=[pl.BlockSpec((ts, 256), cur)] * 2,
        out_shape=(jax.ShapeDtypeStruct((S, 256), BF16),) * 2,
        compiler_params=_cp("parallel"),
    )(pp, pp, wbd, pool_scale, conv_w)


def _poolconv_bwd(pp, do_a, do_c, wbd, wbd_t, pool_scale, conv_w, *, name, ts=512):
    S = pp.shape[0]
    ts = min(ts, S)
    nt = S // ts
    n_e = ts + 2 * HALO

    def body(cur_ref, prev_ref, nxt_ref, dac_ref, dan_ref, dcc_ref, dcn_ref, wbd_ref, wbdt_ref, sc_ref, cw_ref,
             dpp_ref, pooled_ref, dmixed_ref, dsc_ref, dcw_ref):
        i = pl.program_id(0)

        @pl.when(i == 0)
        def _():
            dsc_ref[...] = jnp.zeros_like(dsc_ref)
            dcw_ref[...] = jnp.zeros_like(dcw_ref)

        prev = jnp.where(i > 0, prev_ref[...].astype(F32), 0.0)
        nxt = nxt_ref[...].astype(F32)
        cur = cur_ref[...].astype(F32)
        not_last = i < nt - 1
        da_n = jnp.where(not_last, dan_ref[...].astype(F32), 0.0)
        dc_n = jnp.where(not_last, dcn_ref[...].astype(F32), 0.0)
        zeros_h = jnp.zeros((HALO, 256), F32)
        sc = sc_ref[...]

        u_ext = jnp.concatenate([prev[:, 0:256], cur[:, 0:256]], axis=0)
        pooled = _pooled(u_ext, i * ts, ts)
        pooled_b = pooled.astype(BF16)
        pooled_ref[...] = pooled_b
        mixed = jnp.dot(pooled_b, wbd_ref[...], preferred_element_type=F32)
        da_c = dac_ref[...].astype(F32)
        dsc_ref[...] += jnp.sum(da_c * mixed, axis=0, keepdims=True)
        dmixed_e = jnp.concatenate([da_c, da_n], axis=0) * sc
        dmixed_ref[...] = dmixed_e[0:ts].astype(BF16)
        dpooled = jnp.dot(dmixed_e.astype(BF16), wbdt_ref[...], preferred_element_type=F32)
        qd = dpooled * _pool_inv_count(i * ts, ts + HALO)
        f2 = qd + _shift_up(qd, 1)
        f4 = f2 + _shift_up(f2, 2)
        f8 = f4 + _shift_up(f4, 4)
        f16 = f8 + _shift_up(f8, 8)
        du = (_pool_lane_select([f2, f4, f8, f16]) - dpooled)[0:ts]
        dpp_ref[:, 0:256] = du.astype(BF16)

        xc_e = jnp.concatenate([prev[:, 256:512], cur[:, 256:512], nxt[:, 256:512]], axis=0)
        gc_e = jnp.concatenate([prev[:, 768:1024], cur[:, 768:1024], nxt[:, 768:1024]], axis=0)
        gb_e = jnp.concatenate([zeros_h, cur[:, 512:768], nxt[:, 512:768]], axis=0)
        dc_e = jnp.concatenate([zeros_h, dcc_ref[...].astype(F32), dc_n], axis=0)
        v_e = xc_e * gc_e
        v1, v2 = _shift_dn(v_e, 1), _shift_dn(v_e, 2)
        w0, w1, w2 = cw_ref[0:1, :], cw_ref[1:2, :], cw_ref[2:3, :]
        cv = w2 * v_e + w1 * v1 + w0 * v2
        dcv = dc_e * gb_e
        dv = w2 * dcv + w1 * _shift_up(dcv, 1) + w0 * _shift_up(dcv, 2)
        tile = slice(HALO, HALO + ts)
        dpp_ref[:, 256:512] = (dv * gc_e)[tile].astype(BF16)
        dpp_ref[:, 512:768] = (dc_e * cv)[tile].astype(BF16)
        dpp_ref[:, 768:1024] = (dv * xc_e)[tile].astype(BF16)
        dcv_t = dcv[tile]
        dcw_ref[0:1, :] += jnp.sum(dcv_t * v2[tile], axis=0, keepdims=True)
        dcw_ref[1:2, :] += jnp.sum(dcv_t * v1[tile], axis=0, keepdims=True)
        dcw_ref[2:3, :] += jnp.sum(dcv_t * v_e[tile], axis=0, keepdims=True)

    hb = ts // HALO
    nblk = S // HALO
    cur = lambda i: (i, 0)
    const = lambda i: (0, 0)
    prev = lambda i: (jnp.maximum(i * hb - 1, 0), 0)
    nxt = lambda i: (jnp.minimum((i + 1) * hb, nblk - 1), 0)
    del n_e
    return pl.pallas_call(
        body, name=name, grid=(nt,),
        in_specs=[pl.BlockSpec((ts, 1024), cur), pl.BlockSpec((HALO, 1024), prev), pl.BlockSpec((HALO, 1024), nxt),
                  pl.BlockSpec((ts, 256), cur), pl.BlockSpec((HALO, 256), nxt),
                  pl.BlockSpec((ts, 256), cur), pl.BlockSpec((HALO, 256), nxt),
                  pl.BlockSpec((256, 256), const), pl.BlockSpec((256, 256), const),
                  pl.BlockSpec((1, 256), const), pl.BlockSpec((3, 256), const)],
        out_specs=[pl.BlockSpec((ts, 1024), cur), pl.BlockSpec((ts, 256), cur), pl.BlockSpec((ts, 256), cur),
                   pl.BlockSpec((1, 256), const), pl.BlockSpec((3, 256), const)],
        out_shape=(jax.ShapeDtypeStruct((S, 1024), BF16), jax.ShapeDtypeStruct((S, 256), BF16),
                   jax.ShapeDtypeStruct((S, 256), BF16), jax.ShapeDtypeStruct((1, 256), F32),
                   jax.ShapeDtypeStruct((3, 256), F32)),
        compiler_params=_cp("arbitrary"),
    )(pp, pp, pp, do_a, do_a, do_c, do_c, wbd, wbd_t, pool_scale, conv_w)


def _sigmoid(x):
    return 0.5 * jnp.tanh(0.5 * x) + 0.5


def _merge_fwd(o_a, o_b, o_c, glog, w_br, *, name, ts=512):
    S = o_a.shape[0]
    D = w_br.shape[1]
    ts = min(ts, S)

    def body(oa_ref, ob_ref, oc_ref, gl_ref, w_ref, m_ref):
        pa = jnp.dot(oa_ref[...], w_ref[0:256, :], preferred_element_type=F32)
        pb = jnp.dot(ob_ref[...], w_ref[256:768, :], preferred_element_type=F32)
        pc = jnp.dot(oc_ref[...], w_ref[768:1024, :], preferred_element_type=F32)
        m = _sigmoid(gl_ref[:, 0:D].astype(F32)) * pa
        m = m + _sigmoid(gl_ref[:, D:2 * D].astype(F32)) * pb
        m = m + _sigmoid(gl_ref[:, 2 * D:3 * D].astype(F32)) * pc
        m_ref[...] = m.astype(BF16)

    cur = lambda i: (i, 0)
    return pl.pallas_call(
        body, name=name, grid=(S // ts,),
        in_specs=[pl.BlockSpec((ts, 256), cur), pl.BlockSpec((ts, 512), cur), pl.BlockSpec((ts, 256), cur),
                  pl.BlockSpec((ts, 3 * D), cur), pl.BlockSpec((1024, D), lambda i: (0, 0))],
        out_specs=pl.BlockSpec((ts, D), cur),
        out_shape=jax.ShapeDtypeStruct((S, D), BF16),
        compiler_params=_cp("parallel"),
    )(o_a, o_b, o_c, glog, w_br)


def _merge_bwd(dm, o_a, o_b, o_c, glog, w_br, w_br_t, *, name, ts=256):
    S = o_a.shape[0]
    D = w_br.shape[1]
    ts = min(ts, S)

    def body(dm_ref, oa_ref, ob_ref, oc_ref, gl_ref, w_ref, wt_ref, dgl_ref, dp_ref, doa_ref, dob_ref, doc_ref):
        dmv = dm_ref[...].astype(F32)
        branches = ((oa_ref, 0, 256, doa_ref), (ob_ref, 256, 768, dob_ref), (oc_ref, 768, 1024, doc_ref))
        for b, (o_ref, r0, r1, do_ref) in enumerate(branches):
            prod = jnp.dot(o_ref[...], w_ref[r0:r1, :], preferred_element_type=F32)
            gate = _sigmoid(gl_ref[:, b * D:(b + 1) * D].astype(F32))
            dgl_ref[:, b * D:(b + 1) * D] = (dmv * prod * gate * (1.0 - gate)).astype(BF16)
            dprod = (dmv * gate).astype(BF16)
            dp_ref[:, b * D:(b + 1) * D] = dprod
            do_ref[...] = jnp.dot(dprod, wt_ref[:, r0:r1], preferred_element_type=F32).astype(BF16)

    cur = lambda i: (i, 0)
    const = lambda i: (0, 0)
    return pl.pallas_call(
        body, name=name, grid=(S // ts,),
        in_specs=[pl.BlockSpec((ts, D), cur), pl.BlockSpec((ts, 256), cur), pl.BlockSpec((ts, 512), cur),
                  pl.BlockSpec((ts, 256), cur), pl.BlockSpec((ts, 3 * D), cur),
                  pl.BlockSpec((1024, D), const), pl.BlockSpec((D, 1024), const)],
        out_specs=[pl.BlockSpec((ts, 3 * D), cur), pl.BlockSpec((ts, 3 * D), cur), pl.BlockSpec((ts, 256), cur),
                   pl.BlockSpec((ts, 512), cur), pl.BlockSpec((ts, 256), cur)],
        out_shape=(jax.ShapeDtypeStruct((S, 3 * D), BF16), jax.ShapeDtypeStruct((S, 3 * D), BF16),
                   jax.ShapeDtypeStruct((S, 256), BF16), jax.ShapeDtypeStruct((S, 512), BF16),
                   jax.ShapeDtypeStruct((S, 256), BF16)),
        compiler_params=_cp("parallel"),
    )(dm, o_a, o_b, o_c, glog, w_br, w_br_t)


FFN_CHUNK = 128
FFN_DOT_CHUNKS = 4


def _conv3(x, w_ref, cols):
    x1, x2 = _shift_dn(x, 1), _shift_dn(x, 2)
    return w_ref[2:3, cols] * x + w_ref[1:2, cols] * x1 + w_ref[0:1, cols] * x2, x1, x2


def _ffn_down_fwd(up_pre, fcw, w_down3, resid, gamma, beta, *, alpha, name, tc, ts=256, gather=None):
    S, F2 = up_pre.shape
    D = resid.shape[1]
    ts = min(ts, S)
    nt = S // ts
    nj = F2 // (2 * tc)
    has_ag = gather is not None
    n_g = len(gather) if has_ag else 0

    def body(cur_ref, prev_ref, w_ref, wd_ref, r_ref, g_ref, be_ref, *rest):
        h_ref, y_ref, xh_ref, rs_ref, up_ref = rest[n_g:n_g + 5]
        acc_ref = rest[2 * n_g + 5]
        if has_ag:
            ag_refs = (rest[:n_g], rest[n_g + 5:2 * n_g + 5]) + tuple(rest[2 * n_g + 6:2 * n_g + 9])
        i, j = pl.program_id(0), pl.program_id(1)
        if has_ag:
            @pl.when((i == 0) & (j == 0))
            def _():
                _ag_start(*ag_refs)

        part = None
        for c in range(tc // FFN_CHUNK):
            halves = []
            for half in range(2):
                cols = slice(half * tc + c * FFN_CHUNK, half * tc + (c + 1) * FFN_CHUNK)
                prev = jnp.where(i > 0, prev_ref[:, cols].astype(F32), 0.0)
                x = jnp.concatenate([prev, cur_ref[:, cols].astype(F32)], axis=0)
                halves.append(_conv3(x, w_ref, cols)[0][HALO:HALO + ts])
                up_ref[:, cols] = halves[-1].astype(BF16)
            a, b = halves
            h_ref[:, c * FFN_CHUNK:(c + 1) * FFN_CHUNK] = (a * _sigmoid(a) * b).astype(BF16)
            if (c + 1) % FFN_DOT_CHUNKS == 0 or c + 1 == tc // FFN_CHUNK:
                k0 = (c // FFN_DOT_CHUNKS) * FFN_DOT_CHUNKS * FFN_CHUNK
                piece = jnp.dot(h_ref[:, k0:(c + 1) * FFN_CHUNK], wd_ref[j, k0:(c + 1) * FFN_CHUNK, :],
                                preferred_element_type=F32)
                part = piece if part is None else part + piece

        @pl.when(j == 0)
        def _():
            acc_ref[...] = part

        @pl.when(j > 0)
        def _():
            acc_ref[...] += part

        @pl.when(j == nj - 1)
        def _():
            z = alpha * r_ref[...] + acc_ref[...]
            mu = jnp.mean(z, axis=-1, keepdims=True)
            zc = z - mu
            var = jnp.mean(zc * zc, axis=-1, keepdims=True)
            rstd = lax.rsqrt(var + LN_EPS)
            xhat = zc * rstd
            y_ref[...] = xhat * g_ref[...] + be_ref[...]
            xh_ref[...] = xhat.astype(BF16)
            rs_ref[...] = rstd

        if has_ag:
            @pl.when((i == nt - 1) & (j == nj - 1))
            def _():
                _ag_finish(*ag_refs)

    hb = ts // HALO
    row = lambda i, j: (i, 0)
    vec = lambda i, j: (0, 0)
    in_specs = [pl.BlockSpec((ts, 2 * tc), lambda i, j: (i, j)),
                pl.BlockSpec((HALO, 2 * tc), lambda i, j: (jnp.maximum(i * hb - 1, 0), j)),
                pl.BlockSpec((3, 2 * tc), lambda i, j: (0, j)),
                pl.BlockSpec((nj, tc, D), lambda i, j: (0, 0, 0)),
                pl.BlockSpec((ts, D), row), pl.BlockSpec((1, D), vec), pl.BlockSpec((1, D), vec)]
    out_specs = [pl.BlockSpec((ts, tc), lambda i, j: (i, j)), pl.BlockSpec((ts, D), row), pl.BlockSpec((ts, D), row),
                 pl.BlockSpec((ts, 1), row), pl.BlockSpec((ts, 2 * tc), lambda i, j: (i, j))]
    out_shape = [jax.ShapeDtypeStruct((S, F2 // 2), BF16), jax.ShapeDtypeStruct((S, D), F32),
                 jax.ShapeDtypeStruct((S, D), BF16), jax.ShapeDtypeStruct((S, 1), F32),
                 jax.ShapeDtypeStruct((S, F2), BF16)]
    args = [up_pre, up_pre, fcw, w_down3, resid, gamma, beta]
    scratch = [pltpu.VMEM((ts, D), F32)]
    if has_ag:
        in_specs += [pl.BlockSpec(memory_space=pl.ANY)] * n_g
        args += list(gather)
        out_specs += [pl.BlockSpec(memory_space=pl.ANY)] * n_g
        out_shape += [jax.ShapeDtypeStruct((N_DEV,) + g.shape, g.dtype) for g in gather]
        scratch += _ag_sems(n_g)
    res = pl.pallas_call(
        body, name=name, grid=(nt, nj), in_specs=in_specs, out_specs=out_specs, out_shape=out_shape,
        scratch_shapes=scratch, compiler_params=_cp("arbitrary", "arbitrary"),
    )(*args)
    return tuple(res[:5]) + ((list(res[5:]),) if has_ag else ())


def _ffn_up_bwd(up_pre, up, dh, fcw, w_up_t3, dz, *, alpha, name, tc, ts=256, scatter=None):
    S, F2 = up_pre.shape
    D = dz.shape[1]
    ts = min(ts, S)
    nt = S // ts
    nj = F2 // (2 * tc)
    has_rs = scatter is not None
    n_s = len(scatter) if has_rs else 0
    tile = slice(0, ts)

    def body(x_ref, upc_ref, upn_ref, dhc_ref, dhn_ref, w_ref, wt_ref, dz_ref, *rest):
        dpre_ref, dx_ref, dw_ref = rest[n_s:n_s + 3]
        acc_ref = rest[2 * n_s + 3]
        if has_rs:
            rs_refs = (rest[:n_s], rest[n_s + 3:2 * n_s + 3], rest[2 * n_s + 4], rest[2 * n_s + 5])
        i, j = pl.program_id(0), pl.program_id(1)

        @pl.when((i == 0) & (j == 0))
        def _():
            dw_ref[...] = jnp.zeros_like(dw_ref)
            if has_rs:
                _rs_chips_start(*rs_refs)

        part = None
        for c in range(tc // FFN_CHUNK):
            lanes = slice(c * FFN_CHUNK, (c + 1) * FFN_CHUNK)
            dh_n = jnp.where(i < nt - 1, dhn_ref[:, lanes].astype(F32), 0.0)
            dh_e = jnp.concatenate([dhc_ref[:, lanes].astype(F32), dh_n], axis=0)
            cols_of = [slice(half * tc + c * FFN_CHUNK, half * tc + (c + 1) * FFN_CHUNK) for half in range(2)]
            a, b = [jnp.concatenate([upc_ref[:, cols].astype(F32), upn_ref[:, cols].astype(F32)], axis=0)
                    for cols in cols_of]
            sg = _sigmoid(a)
            dups = [dh_e * b * (sg * (1.0 + a * (1.0 - sg))), dh_e * (a * sg)]
            for half in range(2):
                cols, dup = cols_of[half], dups[half]
                dup1, dup2 = _shift_up(dup, 1), _shift_up(dup, 2)
                dpre = w_ref[2:3, cols] * dup + w_ref[1:2, cols] * dup1 + w_ref[0:1, cols] * dup2
                dpre_ref[:, cols] = dpre[tile].astype(BF16)
                x = x_ref[:, cols].astype(F32)
                dw_ref[j, 0:1, cols] += jnp.sum(dup2[tile] * x, axis=0, keepdims=True)
                dw_ref[j, 1:2, cols] += jnp.sum(dup1[tile] * x, axis=0, keepdims=True)
                dw_ref[j, 2:3, cols] += jnp.sum(dup[tile] * x, axis=0, keepdims=True)
            if (c + 1) % FFN_DOT_CHUNKS == 0 or c + 1 == tc // FFN_CHUNK:
                k0 = (c // FFN_DOT_CHUNKS) * FFN_DOT_CHUNKS * FFN_CHUNK
                for half in range(2):
                    ks = slice(half * tc + k0, half * tc + (c + 1) * FFN_CHUNK)
                    piece = jnp.dot(dpre_ref[:, ks], wt_ref[j, ks, :], preferred_element_type=F32)
                    part = piece if part is None else part + piece

        @pl.when(j == 0)
        def _():
            acc_ref[...] = part

        @pl.when(j > 0)
        def _():
            acc_ref[...] += part

        @pl.when(j == nj - 1)
        def _():
            dx_ref[...] = acc_ref[...] + alpha * dz_ref[...]

        if has_rs:
            @pl.when((i == nt - 1) & (j == nj - 1))
            def _():
                _rs_chips_finish(*rs_refs)

    hb = ts // HALO
    nblk = S // HALO
    nxt = lambda i, j: (jnp.minimum((i + 1) * hb, nblk - 1), j)
    row = lambda i, j: (i, 0)
    in_specs = [pl.BlockSpec((ts, 2 * tc), lambda i, j: (i, j)),
                pl.BlockSpec((ts, 2 * tc), lambda i, j: (i, j)), pl.BlockSpec((HALO, 2 * tc), nxt),
                pl.BlockSpec((ts, tc), lambda i, j: (i, j)), pl.BlockSpec((HALO, tc), nxt),
                pl.BlockSpec((3, 2 * tc), lambda i, j: (0, j)),
                pl.BlockSpec((nj, 2 * tc, D), lambda i, j: (0, 0, 0)),
                pl.BlockSpec((ts, D), row)]
    out_specs = [pl.BlockSpec((ts, 2 * tc), lambda i, j: (i, j)), pl.BlockSpec((ts, D), row),
                 pl.BlockSpec((nj, 3, 2 * tc), lambda i, j: (0, 0, 0))]
    out_shape = [jax.ShapeDtypeStruct((S, F2), BF16), jax.ShapeDtypeStruct((S, D), F32),
                 jax.ShapeDtypeStruct((nj, 3, 2 * tc), F32)]
    args = [up_pre, up, up, dh, dh, fcw, w_up_t3, dz]
    scratch = [pltpu.VMEM((ts, D), F32)]
    if has_rs:
        in_specs += [pl.BlockSpec(memory_space=pl.ANY)] * n_s
        args += list(scatter)
        out_specs += [pl.BlockSpec(memory_space=pl.ANY)] * n_s
        out_shape += [jax.ShapeDtypeStruct((3,) + q.shape[1:], q.dtype) for q in scatter]
        scratch += _rs_sems(n_s)
    res = pl.pallas_call(
        body, name=name, grid=(nt, nj), in_specs=in_specs, out_specs=out_specs, out_shape=out_shape,
        scratch_shapes=scratch, compiler_params=_cp("arbitrary", "arbitrary"),
    )(*args)
    return tuple(res[:3]) + ((list(res[3:]),) if has_rs else ())


def _ln_bwd(dy, xhat, rstd, gamma, *, name, ts=512):
    S, D = dy.shape
    ts = min(ts, S)

    def body(dy_ref, xh_ref, rs_ref, g_ref, dz_ref, dg_ref, db_ref):
        _ln_bwd_tile(dy_ref[...], xh_ref, rs_ref, g_ref, dz_ref, dg_ref, db_ref, pl.program_id(0) == 0)

    cur = lambda i: (i, 0)
    const = lambda i: (0, 0)
    return pl.pallas_call(
        body, name=name, grid=(S // ts,),
        in_specs=[pl.BlockSpec((ts, D), cur), pl.BlockSpec((ts, D), cur), pl.BlockSpec((ts, 1), cur),
                  pl.BlockSpec((1, D), const)],
        out_specs=[pl.BlockSpec((ts, D), cur), pl.BlockSpec((1, D), const), pl.BlockSpec((1, D), const)],
        out_shape=(jax.ShapeDtypeStruct((S, D), F32), jax.ShapeDtypeStruct((1, D), F32),
                   jax.ShapeDtypeStruct((1, D), F32)),
        compiler_params=_cp("arbitrary"),
    )(dy, xhat, rstd, gamma)


def _loss_head(y, tgt, *, name, ts=512):
    S, D = y.shape
    ts = min(ts, S)

    def body(y_ref, t_ref, dy_ref, sq_ref):
        @pl.when(pl.program_id(0) == 0)
        def _():
            sq_ref[...] = jnp.zeros_like(sq_ref)

        e = y_ref[...] - t_ref[...]
        dy_ref[...] = e * (1.0 / D)
        sq_ref[...] += jnp.sum(e * e, axis=0, keepdims=True)

    cur = lambda i: (i, 0)
    return pl.pallas_call(
        body, name=name, grid=(S // ts,),
        in_specs=[pl.BlockSpec((ts, D), cur), pl.BlockSpec((ts, D), cur)],
        out_specs=[pl.BlockSpec((ts, D), cur), pl.BlockSpec((1, D), lambda i: (0, 0))],
        out_shape=(jax.ShapeDtypeStruct((S, D), F32), jax.ShapeDtypeStruct((1, D), F32)),
        compiler_params=_cp("arbitrary"),
    )(y, tgt)


def _adamw(w, g, m, v, *, name, tr=512):
    lead = w.shape[:-2]
    R, C = w.shape[-2:]
    tr = _div_tile(R, tr)
    c1 = 1.0 - ADAM_B1 ** ADAM_STEP
    c2 = 1.0 - ADAM_B2 ** ADAM_STEP

    def body(w_ref, g_ref, m_ref, v_ref, d_ref, mo_ref, vo_ref):
        gv = g_ref[...]
        m2 = ADAM_B1 * m_ref[...] + (1.0 - ADAM_B1) * gv
        v2 = ADAM_B2 * v_ref[...] + (1.0 - ADAM_B2) * (gv * gv)
        m_hat = m2 / c1
        v_hat = v2 / c2
        d_ref[...] = -ADAM_LR * (m_hat / (jnp.sqrt(v_hat) + ADAM_EPS) + ADAM_WD * w_ref[...])
        mo_ref[...] = m2
        vo_ref[...] = v2

    if lead:
        spec = pl.BlockSpec((1, tr, C), lambda l, i: (l, i, 0))
        grid = (lead[0], R // tr)
    else:
        spec = pl.BlockSpec((tr, C), lambda i: (i, 0))
        grid = (R // tr,)
    return pl.pallas_call(
        body, name=name, grid=grid,
        in_specs=[spec] * 4, out_specs=[spec] * 3,
        out_shape=(jax.ShapeDtypeStruct(w.shape, F32),) * 3,
        compiler_params=_cp(*(("parallel",) * len(grid))),
    )(w, g, m, v)


def _interleave_cols(w, nj):
    lead, f2 = w.shape[:-1], w.shape[-1]
    tc = f2 // (2 * nj)
    w = w.reshape(lead + (2, nj, tc))
    return jnp.swapaxes(w, -3, -2).reshape(lead + (f2,))


def _deinterleave_cols(w, nj):
    lead, f2 = w.shape[:-1], w.shape[-1]
    tc = f2 // (2 * nj)
    w = w.reshape(lead + (nj, 2, tc))
    return jnp.swapaxes(w, -3, -2).reshape(lead + (f2,))


def _block_diag(w_pool):
    return jnp.concatenate([jnp.pad(w_pool[g], ((0, 0), (64 * g, 192 - 64 * g))) for g in range(4)], axis=0)


def _pad_rows(v, rows):
    return jnp.pad(v, (0, rows * LANES - v.shape[0])).reshape(rows, LANES)


def kernel(x, positions, w_in, w_pool, pool_scale, attn_sinks, conv_w, w_branch_a, w_branch_b, w_branch_c, w_o, ln1_g, ln1_b, w_up, ffn_conv_w, w_down, ln2_g, ln2_b, loss_target, m_w_in, m_w_pool, m_pool_scale, m_attn_sinks, m_conv_w, m_w_branch_a, m_w_branch_b, m_w_branch_c, m_w_o, m_ln1_g, m_ln1_b, m_w_up, m_ffn_conv_w, m_w_down, m_ln2_g, m_ln2_b, v_w_in, v_w_pool, v_pool_scale, v_attn_sinks, v_conv_w, v_w_branch_a, v_w_branch_b, v_w_branch_c, v_w_o, v_ln1_g, v_ln1_b, v_w_up, v_ffn_conv_w, v_w_down, v_ln2_g, v_ln2_b):
    L, D, in_shard = w_in.shape
    S = x.shape[1]
    IN = in_shard * N_DEV
    F2 = w_up.shape[2] * N_DEV
    F = F2 // 2
    assert D == 1024 and IN == 1792 + 3 * D and x.shape[0] == 1 and S % 512 == 0
    alpha = (2 * L) ** 0.25
    NJ = 2
    TC = F // NJ
    xs = x.reshape(S, D)
    tgt = loss_target.reshape(S, D)

    big = [w_in, w_branch_a, w_branch_b, w_branch_c, w_o, w_up, w_down]
    PART_A, PART_B = (0, 1, 2, 3, 4), (5, 6)
    rows_l = [a.size // L // LANES for a in big]
    offs_l = [sum(rows_l[:k]) for k in range(len(big) + 1)]

    def pack_part(l, part):
        return [(big[k][l].T if k == 0 else big[k][l]).astype(BF16) for k in part]

    n_cw, n_fw = conv_w.size, ffn_conv_w.size
    small_rows = -(-(n_cw + n_fw) // LANES)
    small = _pad_rows(jnp.concatenate([conv_w.reshape(-1), ffn_conv_w.reshape(-1)]), small_rows)
    gsmall = _all_gather(small, "ag_conv_weights").reshape(N_DEV, -1)
    conv_full = gsmall[:, :n_cw].reshape(N_DEV, L, 3, -1).transpose(1, 2, 0, 3).reshape(L, 3, 256)
    fcw_full = gsmall[:, n_cw:n_cw + n_fw].reshape(N_DEV, L, 3, -1).transpose(1, 2, 0, 3).reshape(L, 3, F2)
    fcw_full = _interleave_cols(fcw_full, NJ)

    def shard_of(g, part, k, shape):
        assert g[part.index(k)].shape == (N_DEV,) + shape
        return g[part.index(k)]

    def unpack_a(g):
        win_t = shard_of(g, PART_A, 0, (in_shard, D)).reshape(IN, D)
        wg_t = win_t[1792:]
        wp_t = jnp.concatenate([win_t[0:256], win_t[1024:1792]], axis=0)
        wq_t = win_t[256:1024]
        wg, wp, wq = wg_t.T, wp_t.T, wq_t.T
        if g[1] is None:
            return dict(wg=wg, wp=wp, wq=wq)
        wa = shard_of(g, PART_A, 1, (256, D // N_DEV)).transpose(1, 0, 2).reshape(256, D)
        wb = shard_of(g, PART_A, 2, (512, D // N_DEV)).transpose(1, 0, 2).reshape(512, D)
        wc = shard_of(g, PART_A, 3, (256, D // N_DEV)).transpose(1, 0, 2).reshape(256, D)
        wbr = jnp.concatenate([wa, wb, wc], axis=0)
        wo = shard_of(g, PART_A, 4, (D // N_DEV, D)).reshape(D, D)
        return dict(wg=wg, wp=wp, wq=wq, wg_t=wg_t, wp_t=wp_t, wq_t=wq_t, wbr=wbr, wbr_t=wbr.T, wo=wo, wo_t=wo.T)

    def unpack_b(g):
        nh = N_DEV // (2 * NJ)
        wup = shard_of(g, PART_B, 5, (D, F2 // N_DEV)).reshape(2, NJ, nh, D, F2 // N_DEV)
        wup = wup.transpose(3, 1, 0, 2, 4).reshape(D, F2)
        wdn = shard_of(g, PART_B, 6, (F // N_DEV, D)).reshape(F, D)
        return dict(wup=wup, wup_t=wup.T, wdn=wdn, wdn_t=wdn.T)

    def local_weights(l):
        wbd = _block_diag(w_pool[l]).astype(BF16)
        return dict(wbd=wbd, wbd_t=wbd.T, scale=pool_scale[l].reshape(1, 256), conv=conv_full[l],
                    fcw=fcw_full[l], sinks=jnp.broadcast_to(attn_sinks[l].reshape(8, 1), (8, 128)),
                    g1=ln1_g[l].reshape(1, D), b1=ln1_b[l].reshape(1, D),
                    g2=ln2_g[l].reshape(1, D), b2=ln2_b[l].reshape(1, D))

    inv_freq = ROPE_THETA ** (-jnp.arange(0, ROT_DIM, 2, dtype=F32) / ROT_DIM)
    head_lane = jnp.concatenate([inv_freq, inv_freq, jnp.zeros((HEAD_DIM - ROT_DIM,), F32)])
    head_sign = jnp.concatenate([-jnp.ones((8,), F32), jnp.ones((8,), F32), jnp.zeros((HEAD_DIM - ROT_DIM,), F32)])
    inv_lane = jnp.tile(head_lane, 2).reshape(1, 128)
    sign_lane = jnp.tile(head_sign, 2).reshape(1, 128)
    cos_t, sin_t = _rope_tables(positions.reshape(S, 1), inv_lane, sign_lane, "rope_tables")

    saved, W = [], []
    h_in = xs
    gathered_a = [_all_gather(pack_part(0, PART_A[:1]), "ag_weights_first")]
    for l in range(L):
        if l == 0:
            w_in_only = unpack_a(gathered_a + [None] * 4)
            (pg, pp, pq), later = _mm_fan(h_in, [w_in_only["wg"], w_in_only["wp"], w_in_only["wq"]], out_dtype=BF16,
                                          name="proj_in", gather=pack_part(0, PART_A[1:]) + pack_part(0, PART_B))
            gathered_a, gathered_b = gathered_a + later[:4], later[4:]
        w = {**unpack_a(gathered_a), **unpack_b(gathered_b), **local_weights(l)}
        W.append(w)
        if l > 0:
            pg, pp, pq = _mm_fan(h_in, [w["wg"], w["wp"], w["wq"]], out_dtype=BF16, name="proj_in")
        o_a, o_c = _poolconv_fwd(pp, w["wbd"], w["scale"], w["conv"], name="poolconv_fwd")
        o_b, lse = _attn_fwd(pq, cos_t, sin_t, w["sinks"], name="attn_fwd")
        merged = _merge_fwd(o_a, o_b, o_c, pg, w["wbr"], name="merge_fwd")
        x1, xh1, rs1 = _mm_ln(merged, w["wo"], h_in, w["g1"], w["b1"], alpha=alpha, name="wo_ln1")
        if l + 1 < L:
            up_pre, gathered_a = _mm(x1, w["wup"], out_dtype=BF16, name="ffn_up",
                                     gather=pack_part(l + 1, PART_A))
        else:
            up_pre = _mm(x1, w["wup"], out_dtype=BF16, name="ffn_up")
        down = dict(alpha=alpha, name="ffn_down", tc=TC)
        wdn3 = w["wdn"].reshape(NJ, TC, D)
        if l + 1 < L:
            hact, x2, xh2, rs2, up, gathered_b = _ffn_down_fwd(up_pre, w["fcw"], wdn3, x1, w["g2"], w["b2"],
                                                               gather=pack_part(l + 1, PART_B), **down)
        else:
            hact, x2, xh2, rs2, up = _ffn_down_fwd(up_pre, w["fcw"], wdn3, x1, w["g2"], w["b2"], **down)
        saved.append(dict(up=up,x0=h_in, pg=pg, pp=pp, pq=pq, o_a=o_a, o_b=o_b, o_c=o_c, lse=lse, merged=merged,
                          x1=x1, xh1=xh1, rs1=rs1, up_pre=up_pre, hact=hact, xh2=xh2, rs2=rs2))
        h_in = x2

    dy, sq_lanes = _loss_head(h_in, tgt, name="loss_head")

    def pack_grads(g):
        col = lambda a, n: a.reshape(a.shape[0], N_DEV, n).transpose(1, 0, 2)
        row = lambda a, n: a.reshape(N_DEV, n, a.shape[1])
        nh = N_DEV // (2 * NJ)
        up_t = g["w_up_t"].reshape(NJ, 2, nh * (F2 // N_DEV), D).transpose(1, 0, 2, 3).reshape(N_DEV, F2 // N_DEV, D)
        rest = [col(g["a"], D // N_DEV), col(g["b"], D // N_DEV), col(g["c"], D // N_DEV),
                row(g["w_o"], D // N_DEV), row(g["w_down"], F // N_DEV)]
        return [row(g["w_in_t"], in_shard).astype(BF16), up_t.astype(BF16),
                jnp.concatenate([p.reshape(N_DEV, -1, LANES).astype(BF16) for p in rest], axis=1)]

    my_c = lax.axis_index("c").astype(jnp.int32).reshape(1)
    my_chip = (2 * lax.axis_index("x") + lax.axis_index("y")).astype(jnp.int32).reshape(1)
    gw = [None] * L
    pair_sum = [None] * L
    from_chips = [None] * L
    for l in reversed(range(L)):
        w, sv = W[l], saved[l]
        if l == L - 1:
            dz2, dg2, db2 = _ln_bwd(dy, sv["xh2"], sv["rs2"], w["g2"], name="ln2_bwd")
        else:
            dz2, dg2, db2 = ln2_out
        dw_dn = _mm_tn(sv["hact"], dz2, name="down_bwd_w", tka=TC, na=NJ, tn=D, ts=1024)
        up_bwd = dict(alpha=alpha, name="ffn_up_bwd", tc=TC)
        if l + 1 < L:
            dh, from_sibling = _mm(dz2, w["wdn_t"], out_dtype=BF16, name="down_bwd_x", swap=packed_above)
            pair_sum[l + 1] = [_sum_sibling(p, r, my_c, "rs_sum_sibling") for p, r in zip(packed_above, from_sibling)]
        else:
            dh = _mm(dz2, w["wdn_t"], out_dtype=BF16, name="down_bwd_x")
        wup_t3 = w["wup_t"].reshape(NJ, 2 * TC, D)
        if l + 1 < L:
            dpre, dx1, dfcw, from_chips[l + 1] = _ffn_up_bwd(sv["up_pre"], sv["up"], dh, w["fcw"], wup_t3, dz2,
                                                             scatter=pair_sum[l + 1], **up_bwd)
        else:
            dpre, dx1, dfcw = _ffn_up_bwd(sv["up_pre"], sv["up"], dh, w["fcw"], wup_t3, dz2, **up_bwd)
        dfcw = dfcw.transpose(1, 0, 2).reshape(3, F2)
        dw_up_t = _mm_tn(dpre, sv["x1"], name="up_bwd_w", tka=TC, na=2 * NJ, tn=D, ts=1024)
        dz1, dg1, db1 = _ln_bwd(dx1, sv["xh1"], sv["rs1"], w["g1"], name="ln1_bwd")
        dmerged = _mm(dz1, w["wo_t"], out_dtype=BF16, name="wo_bwd_x")
        dw_o = _mm_tn(sv["merged"], dz1, name="wo_bwd_w", tka=D, tn=D // 2, nb=2)
        dpg, dprod, do_a, do_b, do_c = _merge_bwd(dmerged, sv["o_a"], sv["o_b"], sv["o_c"], sv["pg"],
                                                  w["wbr"], w["wbr_t"], name="merge_bwd")
        dw_a = _mm_tn(sv["o_a"], dprod, name="branch_a_bwd_w", tka=256, tn=D, b_off=0)
        dw_b = _mm_tn(sv["o_b"], dprod, name="branch_b_bwd_w", tka=512, tn=D, b_off=1)
        dw_c = _mm_tn(sv["o_c"], dprod, name="branch_c_bwd_w", tka=256, tn=D, b_off=2)
        dpq, dsink = _attn_bwd(sv["pq"], cos_t, sin_t, w["sinks"], do_b, sv["o_b"], sv["lse"], name="attn_bwd")
        dpp, pooled, dmixed, dscale, dconv = _poolconv_bwd(sv["pp"], do_a, do_c, w["wbd"], w["wbd_t"], w["scale"],
                                                           w["conv"], name="poolconv_bwd")
        dwbd = _mm_tn(pooled, dmixed, name="pool_bwd_w", tka=256, tn=256)
        dx_args = ([dpg, dpp, dpq], [w["wg_t"], w["wp_t"], w["wq_t"]], dz1)
        if l > 0:
            below = saved[l - 1]
            ln2_out = _mm_sum(*dx_args, add_scale=alpha, name="proj_in_bwd_x",
                              ln=(below["xh2"], below["rs2"], W[l - 1]["g2"]))
        else:
            dx = _mm_sum(*dx_args, add_scale=alpha, name="proj_in_bwd_x")
        dw_g = _mm_tn(dpg, sv["x0"], name="proj_gate_bwd_w", tka=512, na=6, tn=D)
        dw_p = _mm_tn(dpp, sv["x0"], name="proj_poolconv_bwd_w", tka=512, na=2, tn=D)
        dw_q = _mm_tn(dpq, sv["x0"], name="proj_qkv_bwd_w", tka=384, na=2, tn=D)
        dw_in_t = jnp.concatenate([dw_p[0:256], dw_q, dw_p[256:1024], dw_g], axis=0)
        dw_pool = jnp.stack([dwbd[64 * g:64 * (g + 1), 64 * g:64 * (g + 1)] for g in range(4)])
        gw[l] = dict(w_in_t=dw_in_t, a=dw_a, b=dw_b, c=dw_c, w_o=dw_o, w_up_t=dw_up_t, w_down=dw_dn,
                     w_pool=dw_pool, scale=dscale, sinks=dsink[:, 0], conv=dconv, fcw=_deinterleave_cols(dfcw, NJ),
                     g1=dg1, b1=db1, g2=dg2, b2=db2)
        packed_above = pack_grads(gw[l])
    from_sibling = _rs_sibling(packed_above, "rs_sibling_last")
    pair_sum[0] = [_sum_sibling(p, r, my_c, "rs_sum_sibling") for p, r in zip(packed_above, from_sibling)]
    grad_x = dx.reshape(1, S, D)
    from_chips[0] = _rs_chips(pair_sum[0], "rs_chips_last")
    g_layers = [[_sum_chips(q, r, my_chip, "rs_sum_chips") for q, r in zip(pair_sum[l], from_chips[l])]
                for l in range(L)]

    def stack(k):
        return jnp.stack([gw[l][k] for l in range(L)])

    rep_vec = jnp.concatenate([
        stack("w_pool").reshape(-1), stack("scale").reshape(-1), stack("g1").reshape(-1), stack("b1").reshape(-1),
        stack("g2").reshape(-1), stack("b2").reshape(-1)])
    n_rep_full = -(-rep_vec.shape[0] // LANES)
    sinks_row = jnp.pad(stack("sinks").reshape(-1), (0, LANES - 8 * L))
    rep_vec = jnp.concatenate([_pad_rows(rep_vec, n_rep_full).reshape(-1), sinks_row, sq_lanes.reshape(-1)])
    loss_row = n_rep_full + 1
    n_rep = -(-(loss_row + 1) // 8) * 8
    rep_rows = _pad_rows(rep_vec, n_rep)
    dconv_by_dev = stack("conv").reshape(L, 3, N_DEV, -1).transpose(2, 0, 1, 3).reshape(N_DEV, -1)
    dfcw_by_dev = stack("fcw").reshape(L, 3, N_DEV, -1).transpose(2, 0, 1, 3).reshape(N_DEV, -1)
    n_mine = -(-(small_rows) // 8) * 8
    by_dev = jnp.concatenate([dconv_by_dev, dfcw_by_dev], axis=1)
    by_dev = jnp.pad(by_dev, ((0, 0), (0, n_mine * LANES - by_dev.shape[1]))).reshape(N_DEV * n_mine, LANES)
    small_g = _all_gather(jnp.concatenate([rep_rows, by_dev], axis=0), "ag_small_grads")
    rep_sum, mine_sum, loss11 = _small_reduce(small_g, n_rep, n_mine, 1.0 / D, loss_row, "small_reduce")
    loss = loss11[0, 0]

    names_big = ["w_in", "w_branch_a", "w_branch_b", "w_branch_c", "w_o", "w_up", "w_down"]
    ms_big = [m_w_in, m_w_branch_a, m_w_branch_b, m_w_branch_c, m_w_o, m_w_up, m_w_down]
    vs_big = [v_w_in, v_w_branch_a, v_w_branch_b, v_w_branch_c, v_w_o, v_w_up, v_w_down]
    out = {}
    for k, name in enumerate(names_big):
        wk = big[k]
        if k in (0, 5):
            g_t = jnp.stack([g[0 if k == 0 else 1] for g in g_layers])
            tr_ = lambda a: jnp.swapaxes(a, 1, 2)
            d, mo, vo = _adamw(tr_(wk), g_t, tr_(ms_big[k]), tr_(vs_big[k]), name="adamw_" + name)
            out[name] = (tr_(g_t), tr_(d), tr_(mo), tr_(vo))
            continue
        else:
            rest_ks = (1, 2, 3, 4, 6)
            o = sum(rows_l[q] for q in rest_ks[:rest_ks.index(k)])
            g_nat = jnp.concatenate([g[2][o:o + rows_l[k]] for g in g_layers], axis=0).reshape(wk.shape)
        d, mo, vo = _adamw(wk, g_nat, ms_big[k], vs_big[k], name="adamw_" + name)
        out[name] = (g_nat, d, mo, vo)

    def rep_pack(wp_, sc_, g1_, b1_, g2_, b2_, sk_):
        v = jnp.concatenate([wp_.reshape(-1), sc_.reshape(-1), g1_.reshape(-1), b1_.reshape(-1), g2_.reshape(-1),
                             b2_.reshape(-1)])
        return _pad_rows(jnp.concatenate([_pad_rows(v, n_rep_full).reshape(-1), sk_.reshape(-1)]), n_rep)

    def mine_pack(cw_, fw_):
        return _pad_rows(jnp.concatenate([cw_.reshape(-1), fw_.reshape(-1)]), n_mine)

    w_rep = rep_pack(w_pool, pool_scale, ln1_g, ln1_b, ln2_g, ln2_b, attn_sinks)
    m_rep = rep_pack(m_w_pool, m_pool_scale, m_ln1_g, m_ln1_b, m_ln2_g, m_ln2_b, m_attn_sinks)
    v_rep = rep_pack(v_w_pool, v_pool_scale, v_ln1_g, v_ln1_b, v_ln2_g, v_ln2_b, v_attn_sinks)
    g_rep = jnp.concatenate([rep_sum[:loss_row], jnp.zeros((n_rep - loss_row, LANES), F32)], axis=0)
    rep_res = (g_rep,) + tuple(_adamw(w_rep, g_rep, m_rep, v_rep, name="adamw_replicated"))
    w_mine = mine_pack(conv_w, ffn_conv_w)
    mine_res = (mine_sum,) + tuple(_adamw(w_mine, mine_sum, mine_pack(m_conv_w, m_ffn_conv_w),
                                          mine_pack(v_conv_w, v_ffn_conv_w), name="adamw_conv"))

    def rep_unpack(buf):
        flat = buf.reshape(-1)
        res, o = {}, 0
        for nm, ref in (("w_pool", w_pool), ("pool_scale", pool_scale), ("ln1_g", ln1_g), ("ln1_b", ln1_b),
                        ("ln2_g", ln2_g), ("ln2_b", ln2_b)):
            res[nm] = flat[o:o + ref.size].reshape(ref.shape)
            o += ref.size
        o = n_rep_full * LANES
        res["attn_sinks"] = flat[o:o + attn_sinks.size].reshape(attn_sinks.shape)
        return res

    def mine_unpack(buf):
        flat = buf.reshape(-1)
        return {"conv_w": flat[:n_cw].reshape(conv_w.shape),
                "ffn_conv_w": flat[n_cw:n_cw + n_fw].reshape(ffn_conv_w.shape)}

    order = ["w_in", "w_pool", "pool_scale", "attn_sinks", "conv_w", "w_branch_a", "w_branch_b", "w_branch_c", "w_o",
             "ln1_g", "ln1_b", "w_up", "ffn_conv_w", "w_down", "ln2_g", "ln2_b"]
    results = [loss, grad_x]
    for kind in range(4):
        rep_k, mine_k = rep_unpack(rep_res[kind]), mine_unpack(mine_res[kind])
        for nm in order:
            if nm in out:
                results.append(out[nm][kind])
            elif nm in rep_k:
                results.append(rep_k[nm])
            else:
                results.append(mine_k[nm])
    return tuple(results)
```

```python
import functools

import jax
import jax.numpy as jnp
from jax import lax
from jax.experimental import pallas as pl
from jax.experimental.pallas import tpu as pltpu

F32 = jnp.float32
BF16 = jnp.bfloat16

HEAD_DIM = 64
N_Q_HEADS = 8
GROUP = 4
WINDOW = 128
ROT_DIM = 16
ROPE_THETA = 500000.0
POOL_WINDOWS = (2, 4, 8, 16)
LN_EPS = 1e-5
MASK_VALUE = -1e30
ADAM_LR, ADAM_B1, ADAM_B2, ADAM_EPS, ADAM_WD, ADAM_STEP = 0.001, 0.9, 0.999, 1e-08, 0.01, 10

N_DEV = 8
LANES = 1024
HALO = 16
MESH = pl.DeviceIdType.MESH
VMEM_LIMIT = 56 * 1024 * 1024


def _div_tile(n, want, mult=8):
    for t in range(min(want, n) // mult * mult, 0, -mult):
        if n % t == 0:
            return t
    return n


def _cp(*sem):
    return pltpu.CompilerParams(dimension_semantics=sem, vmem_limit_bytes=VMEM_LIMIT)


def _coords():
    return lax.axis_index("x"), lax.axis_index("y"), lax.axis_index("c")


def _all_gather(xs, name):
    xs = list(xs) if isinstance(xs, (list, tuple)) else [xs]
    n = len(xs)

    def body(*refs):
        ag_refs = (refs[:n], refs[n:2 * n]) + tuple(refs[2 * n:])
        _ag_start(*ag_refs)
        _ag_finish(*ag_refs)

    res = pl.pallas_call(
        body, name=name,
        out_shape=[jax.ShapeDtypeStruct((N_DEV,) + a.shape, a.dtype) for a in xs],
        in_specs=[pl.BlockSpec(memory_space=pl.ANY)] * n,
        out_specs=[pl.BlockSpec(memory_space=pl.ANY)] * n,
        scratch_shapes=_ag_sems(n),
    )(*xs)
    return res if n > 1 else res[0]


def _ag_sems(n):
    return [pltpu.SemaphoreType.DMA((7 * n,)), pltpu.SemaphoreType.DMA((7 * n,)), pltpu.SemaphoreType.DMA((n,))]


def _ag_copies(x_refs, out_refs, send_sems, recv_sems, local_sems):
    x, y, c = _coords()
    me, sibling = (x, y, c), (x, y, 1 - c)
    chips = [(1 - x, y), (x, 1 - y), (1 - x, 1 - y)]
    per_array = []
    for a, (x_ref, out_ref) in enumerate(zip(x_refs, out_refs)):
        def slot(px, py, pc, out_ref=out_ref):
            return out_ref.at[4 * px + 2 * py + pc]

        def copy(k, block, to, src=None, a=a, slot=slot):
            return pltpu.make_async_remote_copy(
                src_ref=slot(*block) if src is None else src, dst_ref=slot(*block),
                send_sem=send_sems.at[7 * a + k], recv_sem=recv_sems.at[7 * a + k],
                device_id=to, device_id_type=MESH)

        mine = pltpu.make_async_copy(x_ref, slot(*me), local_sems.at[a])
        first = [copy(0, me, sibling, src=x_ref)]
        first += [copy(1 + j, me, (*chip, c), src=x_ref) for j, chip in enumerate(chips)]
        passed = [copy(4 + j, (*chip, c), sibling) for j, chip in enumerate(chips)]
        from_chips = [copy(1 + j, (*chip, c), me) for j, chip in enumerate(chips)]
        from_sibling = [copy(0, sibling, me)] + [copy(4 + j, (*chip, 1 - c), me) for j, chip in enumerate(chips)]
        per_array.append((mine, first, passed, from_chips, from_sibling))
    return per_array


def _ag_start(*refs):
    for mine, first, _, _, _ in _ag_copies(*refs):
        mine.start()
        for cp in first:
            cp.start()


def _ag_finish(*refs):
    per_array = _ag_copies(*refs)
    for j in range(3):
        for _, _, passed, from_chips, _ in per_array:
            from_chips[j].wait_recv()
            passed[j].start()
    for mine, first, passed, _, from_sibling in per_array:
        for cp in from_sibling:
            cp.wait_recv()
        for cp in first + passed:
            cp.wait_send()
        mine.wait()


def _rs_sibling(ps, name):
    n = len(ps)

    def body(*refs):
        swap_refs = (refs[:n], refs[n:2 * n], refs[2 * n], refs[2 * n + 1])
        _swap_start(*swap_refs)
        _swap_finish(*swap_refs)

    return pl.pallas_call(
        body, name=name,
        out_shape=[jax.ShapeDtypeStruct((4,) + p.shape[1:], p.dtype) for p in ps],
        in_specs=[pl.BlockSpec(memory_space=pl.ANY)] * n,
        out_specs=[pl.BlockSpec(memory_space=pl.ANY)] * n,
        scratch_shapes=_swap_sems(n),
    )(*ps)


def _swap_sems(n):
    return [pltpu.SemaphoreType.DMA((4 * n,)), pltpu.SemaphoreType.DMA((4 * n,))]


def _swap_copies(p_refs, out_refs, send_sems, recv_sems):
    x, y, c = _coords()
    return [pltpu.make_async_remote_copy(
        src_ref=p_ref.at[4 * (j // 2) + 2 * (j % 2) + (1 - c)], dst_ref=out_ref.at[j],
        send_sem=send_sems.at[4 * a + j], recv_sem=recv_sems.at[4 * a + j],
        device_id=(x, y, 1 - c), device_id_type=MESH)
        for a, (p_ref, out_ref) in enumerate(zip(p_refs, out_refs)) for j in range(4)]


def _swap_start(*refs):
    for cp in _swap_copies(*refs):
        cp.start()


def _swap_finish(*refs):
    copies = _swap_copies(*refs)
    for cp in copies:
        cp.wait_recv()
    for cp in copies:
        cp.wait_send()


def _hosted(kind, arrays):
    n = len(arrays)
    if kind == "gather":
        return _ag_start, _ag_finish, [(N_DEV,) + a.shape for a in arrays], _ag_sems(n)
    assert kind == "swap"
    return _swap_start, _swap_finish, [(4,) + a.shape[1:] for a in arrays], _swap_sems(n)


def _rs_chips(qs, name):
    n = len(qs)

    def body(*refs):
        rs_refs = (refs[:n], refs[n:2 * n], refs[2 * n], refs[2 * n + 1])
        _rs_chips_start(*rs_refs)
        _rs_chips_finish(*rs_refs)

    return pl.pallas_call(
        body, name=name,
        out_shape=[jax.ShapeDtypeStruct((3,) + q.shape[1:], q.dtype) for q in qs],
        in_specs=[pl.BlockSpec(memory_space=pl.ANY)] * n,
        out_specs=[pl.BlockSpec(memory_space=pl.ANY)] * n,
        scratch_shapes=_rs_sems(n),
    )(*qs)


def _rs_sems(n):
    return [pltpu.SemaphoreType.DMA((3 * n,)), pltpu.SemaphoreType.DMA((3 * n,))]


def _rs_chips_copies(q_refs, out_refs, send_sems, recv_sems):
    x, y, c = _coords()
    chips = [(1 - x, y), (x, 1 - y), (1 - x, 1 - y)]
    return [pltpu.make_async_remote_copy(
        src_ref=q_ref.at[2 * cx + cy], dst_ref=out_ref.at[k],
        send_sem=send_sems.at[3 * a + k], recv_sem=recv_sems.at[3 * a + k], device_id=(cx, cy, c),
        device_id_type=MESH)
        for a, (q_ref, out_ref) in enumerate(zip(q_refs, out_refs)) for k, (cx, cy) in enumerate(chips)]


def _rs_chips_start(*refs):
    for cp in _rs_chips_copies(*refs):
        cp.start()


def _rs_chips_finish(*refs):
    copies = _rs_chips_copies(*refs)
    for cp in copies:
        cp.wait_recv()
    for cp in copies:
        cp.wait_send()


def _sum_sibling(p, recv, my_c, name, tr=512):
    _, R, C = p.shape
    tr = _div_tile(R, tr, 16)

    def body(c_ref, p_ref, r_ref, o_ref):
        o_ref[...] = (p_ref[...].astype(F32) + r_ref[...].astype(F32)).astype(o_ref.dtype)

    grid_spec = pltpu.PrefetchScalarGridSpec(
        num_scalar_prefetch=1, grid=(4, R // tr),
        in_specs=[pl.BlockSpec((1, tr, C), lambda j, r, c_ref: (4 * (j // 2) + 2 * (j % 2) + c_ref[0], r, 0)),
                  pl.BlockSpec((1, tr, C), lambda j, r, c_ref: (j, r, 0))],
        out_specs=pl.BlockSpec((1, tr, C), lambda j, r, c_ref: (j, r, 0)))
    return pl.pallas_call(body, name=name, grid_spec=grid_spec,
                          out_shape=jax.ShapeDtypeStruct((4, R, C), p.dtype),
                          compiler_params=_cp("parallel", "parallel"))(my_c, p, recv)


def _sum_chips(q, recv, my_chip, name, tr=512):
    _, R, C = q.shape
    tr = _div_tile(R, tr, 16)

    def body(i_ref, q_ref, r_ref, o_ref):
        acc = q_ref[0].astype(F32)
        for k in range(3):
            acc = acc + r_ref[k].astype(F32)
        o_ref[...] = acc

    grid_spec = pltpu.PrefetchScalarGridSpec(
        num_scalar_prefetch=1, grid=(R // tr,),
        in_specs=[pl.BlockSpec((1, tr, C), lambda r, i_ref: (i_ref[0], r, 0)),
                  pl.BlockSpec((3, tr, C), lambda r, i_ref: (0, r, 0))],
        out_specs=pl.BlockSpec((tr, C), lambda r, i_ref: (r, 0)))
    return pl.pallas_call(body, name=name, grid_spec=grid_spec,
                          out_shape=jax.ShapeDtypeStruct((R, C), F32),
                          compiler_params=_cp("parallel"))(my_chip, q, recv)


def _small_reduce(g, n_rep, n_mine, inv_d, loss_row, name):
    _, R, C = g.shape

    def body(g_ref, rep_ref, mine_ref, loss_ref):
        x, y, c = _coords()
        start = pl.multiple_of(n_rep + (4 * x + 2 * y + c) * n_mine, 8)
        rep = g_ref[0, 0:n_rep, :]
        mine = g_ref[0, pl.ds(start, n_mine), :]
        sq = g_ref[0, loss_row:loss_row + 1, :]
        for d in range(1, N_DEV):
            rep = rep + g_ref[d, 0:n_rep, :]
            mine = mine + g_ref[d, pl.ds(start, n_mine), :]
            sq = sq + g_ref[d, loss_row:loss_row + 1, :]
        rep_ref[...] = rep
        mine_ref[...] = mine
        loss_ref[...] = (0.5 * inv_d) * jnp.sum(sq, axis=1, keepdims=True)

    return pl.pallas_call(
        body, name=name,
        out_shape=(jax.ShapeDtypeStruct((n_rep, C), F32), jax.ShapeDtypeStruct((n_mine, C), F32),
                   jax.ShapeDtypeStruct((1, 1), F32)),
        compiler_params=pltpu.CompilerParams(vmem_limit_bytes=VMEM_LIMIT),
    )(g)


def _mm(a, b, *, out_dtype, name, tm=512, tn=None, tk=None, add=None, add_scale=1.0, gather=None, swap=None):
    M, K = a.shape
    N = b.shape[1]
    tm = min(tm, M)
    tn = N if tn is None else tn
    tk = K if tk is None else tk
    nk = K // tk
    has_add = add is not None
    hosted = gather if gather is not None else swap
    has_ag = hosted is not None
    n_g = len(hosted) if has_ag else 0
    if has_ag:
        comm_start, comm_finish, comm_shapes, comm_sems = _hosted("gather" if gather is not None else "swap", hosted)
    n_i, n_j = M // tm, N // tn

    def body(*refs):
        a_ref, b_ref = refs[0], refs[1]
        add_ref = refs[2] if has_add else None
        n_in = 2 + has_add + n_g
        o_ref = refs[n_in]
        if has_ag:
            ag_refs = (refs[n_in - n_g:n_in], refs[n_in + 1:n_in + 1 + n_g]) + tuple(
                refs[n_in + 1 + n_g:n_in + 1 + n_g + len(comm_sems)])
            pid = (pl.program_id(0), pl.program_id(1), pl.program_id(2))

            @pl.when((pid[0] == 0) & (pid[1] == 0) & (pid[2] == 0))
            def _():
                comm_start(*ag_refs)

        part = jnp.dot(a_ref[...].astype(BF16), b_ref[...].astype(BF16), preferred_element_type=F32)

        def finish(r):
            if has_add:
                r = r + add_scale * add_ref[...].astype(F32)
            o_ref[...] = r.astype(out_dtype)

        if nk == 1:
            finish(part)
        else:
            acc_ref = refs[-1]
            k = pl.program_id(2)

            @pl.when(k == 0)
            def _():
                acc_ref[...] = part

            @pl.when(k > 0)
            def _():
                acc_ref[...] += part

            @pl.when(k == nk - 1)
            def _():
                finish(acc_ref[...])

        if has_ag:
            @pl.when((pid[0] == n_i - 1) & (pid[1] == n_j - 1) & (pid[2] == nk - 1))
            def _():
                comm_finish(*ag_refs)

    b_mode = dict(pipeline_mode=pl.Buffered(1)) if (n_j == 1 and nk == 1) else {}
    in_specs = [pl.BlockSpec((tm, tk), lambda i, j, k: (i, k)),
                pl.BlockSpec((tk, tn), lambda i, j, k: (k, j), **b_mode)]
    args = [a, b]
    if has_add:
        in_specs.append(pl.BlockSpec((tm, tn), lambda i, j, k: (i, j)))
        args.append(add)
    out_specs = [pl.BlockSpec((tm, tn), lambda i, j, k: (i, j))]
    out_shape = [jax.ShapeDtypeStruct((M, N), out_dtype)]
    scratch = []
    if has_ag:
        in_specs += [pl.BlockSpec(memory_space=pl.ANY)] * n_g
        args += list(hosted)
        out_specs += [pl.BlockSpec(memory_space=pl.ANY)] * n_g
        out_shape += [jax.ShapeDtypeStruct(s, g.dtype) for s, g in zip(comm_shapes, hosted)]
        scratch += comm_sems
    if nk > 1:
        scratch.append(pltpu.VMEM((tm, tn), F32))
    sem = ("arbitrary",) * 3 if has_ag else ("parallel", "parallel", "arbitrary")
    res = pl.pallas_call(
        body, name=name, grid=(n_i, n_j, nk), in_specs=in_specs, out_specs=out_specs, out_shape=out_shape,
        scratch_shapes=scratch, compiler_params=_cp(*sem),
    )(*args)
    return (res[0], list(res[1:])) if has_ag else res[0]


def _mm_fan(a, bs, *, out_dtype, name, tm=512, gather=None):
    M, K = a.shape
    tm = min(tm, M)
    n = len(bs)
    n_i = M // tm
    n_g = len(gather) if gather is not None else 0

    def body(*refs):
        outs = refs[1 + n + n_g:1 + 2 * n + n_g]
        if n_g:
            ag_refs = (refs[1 + n:1 + n + n_g], refs[1 + 2 * n + n_g:1 + 2 * n + 2 * n_g]) + tuple(
                refs[1 + 2 * n + 2 * n_g:])

            @pl.when(pl.program_id(0) == 0)
            def _():
                _ag_start(*ag_refs)

        a_v = refs[0][...].astype(BF16)
        for k in range(n):
            outs[k][...] = jnp.dot(a_v, refs[1 + k][...].astype(BF16), preferred_element_type=F32).astype(out_dtype)

        if n_g:
            @pl.when(pl.program_id(0) == n_i - 1)
            def _():
                _ag_finish(*ag_refs)

    row = lambda i: (i, 0)
    hbm = pl.BlockSpec(memory_space=pl.ANY)
    res = pl.pallas_call(
        body, name=name, grid=(n_i,),
        in_specs=[pl.BlockSpec((tm, K), row)] + [pl.BlockSpec(b.shape, lambda i: (0, 0)) for b in bs] + [hbm] * n_g,
        out_specs=[pl.BlockSpec((tm, b.shape[1]), row) for b in bs] + [hbm] * n_g,
        out_shape=([jax.ShapeDtypeStruct((M, b.shape[1]), out_dtype) for b in bs]
                   + [jax.ShapeDtypeStruct((N_DEV,) + g.shape, g.dtype) for g in (gather or [])]),
        scratch_shapes=_ag_sems(n_g) if n_g else [],
        compiler_params=_cp("arbitrary" if n_g else "parallel"),
    )(a, *bs, *(gather or []))
    return (list(res[:n]), list(res[n:])) if n_g else list(res)


def _mm_sum(xs, bs, add, *, add_scale, name, tm=512, ln=None):
    M = xs[0].shape[0]
    N = bs[0].shape[1]
    tm = min(tm, M)
    n = len(xs)

    def body(*refs):
        acc = add_scale * refs[2 * n][...]
        for k in range(n):
            acc = acc + jnp.dot(refs[k][...].astype(BF16), refs[n + k][...].astype(BF16), preferred_element_type=F32)
        if ln is None:
            refs[2 * n + 1][...] = acc
        else:
            xh_ref, rs_ref, g_ref, dz_ref, dg_ref, db_ref = refs[2 * n + 1:]
            _ln_bwd_tile(acc, xh_ref, rs_ref, g_ref, dz_ref, dg_ref, db_ref, pl.program_id(0) == 0)

    row = lambda i: (i, 0)
    vec = lambda i: (0, 0)
    in_specs = ([pl.BlockSpec((tm, x.shape[1]), row) for x in xs]
                + [pl.BlockSpec(b.shape, vec) for b in bs] + [pl.BlockSpec((tm, N), row)])
    if ln is None:
        return pl.pallas_call(
            body, name=name, grid=(M // tm,), in_specs=in_specs, out_specs=pl.BlockSpec((tm, N), row),
            out_shape=jax.ShapeDtypeStruct((M, N), F32), compiler_params=_cp("parallel"),
        )(*xs, *bs, add)
    in_specs += [pl.BlockSpec((tm, N), row), pl.BlockSpec((tm, 1), row), pl.BlockSpec((1, N), vec)]
    return pl.pallas_call(
        body, name=name, grid=(M // tm,), in_specs=in_specs,
        out_specs=[pl.BlockSpec((tm, N), row), pl.BlockSpec((1, N), vec), pl.BlockSpec((1, N), vec)],
        out_shape=(jax.ShapeDtypeStruct((M, N), F32), jax.ShapeDtypeStruct((1, N), F32),
                   jax.ShapeDtypeStruct((1, N), F32)),
        compiler_params=_cp("arbitrary"),
    )(*xs, *bs, add, *ln)


def _ln_bwd_tile(dyv, xh_ref, rs_ref, g_ref, dz_ref, dg_ref, db_ref, first):
    @pl.when(first)
    def _():
        dg_ref[...] = jnp.zeros_like(dg_ref)
        db_ref[...] = jnp.zeros_like(db_ref)

    xh = xh_ref[...].astype(F32)
    dyg = dyv * g_ref[...]
    c1 = jnp.mean(dyg, axis=-1, keepdims=True)
    c2 = jnp.mean(dyg * xh, axis=-1, keepdims=True)
    dz_ref[...] = rs_ref[...] * (dyg - c1 - xh * c2)
    dg_ref[...] += jnp.sum(dyv * xh, axis=0, keepdims=True)
    db_ref[...] += jnp.sum(dyv, axis=0, keepdims=True)


def _mm_ln(a, b, resid, gamma, beta, *, alpha, name, tm=512, tk=None):
    M, K = a.shape
    D = b.shape[1]
    tm = min(tm, M)
    tk = K if tk is None else tk
    nk = K // tk

    def body(a_ref, b_ref, r_ref, g_ref, be_ref, y_ref, xh_ref, rs_ref, *scratch):
        part = jnp.dot(a_ref[...].astype(BF16), b_ref[...].astype(BF16), preferred_element_type=F32)

        def finish(acc):
            z = alpha * r_ref[...] + acc
            mu = jnp.mean(z, axis=-1, keepdims=True)
            zc = z - mu
            var = jnp.mean(zc * zc, axis=-1, keepdims=True)
            rstd = lax.rsqrt(var + LN_EPS)
            xhat = zc * rstd
            y_ref[...] = xhat * g_ref[...] + be_ref[...]
            xh_ref[...] = xhat.astype(BF16)
            rs_ref[...] = rstd

        if nk == 1:
            finish(part)
        else:
            acc_ref = scratch[0]
            k = pl.program_id(1)

            @pl.when(k == 0)
            def _():
                acc_ref[...] = part

            @pl.when(k > 0)
            def _():
                acc_ref[...] += part

            @pl.when(k == nk - 1)
            def _():
                finish(acc_ref[...])

    row = lambda i, k: (i, 0)
    vec = lambda i, k: (0, 0)
    return pl.pallas_call(
        body, name=name, grid=(M // tm, nk),
        in_specs=[pl.BlockSpec((tm, tk), lambda i, k: (i, k)), pl.BlockSpec((tk, D), lambda i, k: (k, 0)),
                  pl.BlockSpec((tm, D), row), pl.BlockSpec((1, D), vec), pl.BlockSpec((1, D), vec)],
        out_specs=[pl.BlockSpec((tm, D), row), pl.BlockSpec((tm, D), row), pl.BlockSpec((tm, 1), row)],
        out_shape=(jax.ShapeDtypeStruct((M, D), F32), jax.ShapeDtypeStruct((M, D), BF16),
                   jax.ShapeDtypeStruct((M, 1), F32)),
        scratch_shapes=[pltpu.VMEM((tm, D), F32)] if nk > 1 else [],
        compiler_params=_cp("parallel", "arbitrary"),
    )(a, b, resid, gamma, beta)


def _mm_tn(a, b, *, name, tka, tn, a_off=0, na=1, b_off=0, nb=1, ts=2048, out_dtype=F32):
    S = a.shape[0]
    ts = min(ts, S)
    ns = S // ts
    direct = out_dtype == F32

    def body(a_ref, b_ref, o_ref, *scratch):
        acc_ref = o_ref if direct else scratch[0]
        s = pl.program_id(2)
        part = lax.dot_general(a_ref[...].astype(BF16), b_ref[...].astype(BF16),
                               (((0,), (0,)), ((), ())), preferred_element_type=F32)

        @pl.when(s == 0)
        def _():
            acc_ref[...] = part

        @pl.when(s > 0)
        def _():
            acc_ref[...] += part

        if not direct:
            @pl.when(s == ns - 1)
            def _():
                o_ref[...] = acc_ref[...].astype(out_dtype)

    return pl.pallas_call(
        body, name=name, grid=(na, nb, ns),
        in_specs=[pl.BlockSpec((ts, tka), lambda i, j, s: (s, a_off + i)),
                  pl.BlockSpec((ts, tn), lambda i, j, s: (s, b_off + j))],
        out_specs=pl.BlockSpec((tka, tn), lambda i, j, s: (i, j)),
        out_shape=jax.ShapeDtypeStruct((na * tka, nb * tn), out_dtype),
        scratch_shapes=[] if direct else [pltpu.VMEM((tka, tn), F32)],
        compiler_params=_cp("parallel", "parallel", "arbitrary"),
    )(a, b)


def _rope_tables(pos, inv_lane, sign_lane, name, ts=512):
    S = pos.shape[0]
    ts = min(ts, S)

    def body(p_ref, inv_ref, sg_ref, cos_ref, sin_ref):
        ang = p_ref[...].astype(F32) * inv_ref[...]
        cos_ref[...] = jnp.cos(ang)
        sin_ref[...] = jnp.sin(ang) * sg_ref[...]

    return pl.pallas_call(
        body, name=name, grid=(S // ts,),
        in_specs=[pl.BlockSpec((ts, 1), lambda i: (i, 0)), pl.BlockSpec((1, 128), lambda i: (0, 0)),
                  pl.BlockSpec((1, 128), lambda i: (0, 0))],
        out_specs=[pl.BlockSpec((ts, 128), lambda i: (i, 0))] * 2,
        out_shape=(jax.ShapeDtypeStruct((S, 128), F32),) * 2,
        compiler_params=_cp("parallel"),
    )(pos, inv_lane, sign_lane)


def _rope_swap(t):
    lane = lax.broadcasted_iota(jnp.int32, (1, 128), 1)
    lo = (lane % HEAD_DIM) < (ROT_DIM // 2)
    return jnp.where(lo, pltpu.roll(t, 128 - ROT_DIM // 2, 1), pltpu.roll(t, ROT_DIM // 2, 1))


def _rope_fwd(t, cos, sin):
    return t * cos + _rope_swap(t) * sin


def _rope_bwd(d, cos, sin):
    lane = lax.broadcasted_iota(jnp.int32, (1, 128), 1)
    return d * cos + jnp.where((lane % HEAD_DIM) < ROT_DIM, _rope_swap(d * sin), 0.0)


def _tile_heads(t):
    lane = lax.broadcasted_iota(jnp.int32, (1, 128), 1)
    r = pltpu.roll(t, 64, 1)
    h0 = jnp.where(lane < 64, t, r)
    h1 = jnp.where(lane < 64, r, t)
    return jnp.concatenate([h0, h0], axis=1), jnp.concatenate([h1, h1], axis=1)


def _fold_heads(d0, d1):
    lane = lax.broadcasted_iota(jnp.int32, (1, 128), 1)

    def fold(d):
        s = d[:, 0:128] + d[:, 128:256]
        return s + pltpu.roll(s, 64, 1)

    return jnp.where(lane < 64, fold(d0), fold(d1))


def _band4(n_keys):
    row = lax.broadcasted_iota(jnp.int32, (GROUP * WINDOW, n_keys), 0) % WINDOW
    col = lax.broadcasted_iota(jnp.int32, (GROUP * WINDOW, n_keys), 1)
    return (col > row) & (col <= row + WINDOW), col


def _head_masks():
    lane = lax.broadcasted_iota(jnp.int32, (1, GROUP * HEAD_DIM), 1)
    return [(lane // HEAD_DIM) == hl for hl in range(GROUP)]


def _stack_heads(t):
    zero = jnp.zeros_like(t)
    return jnp.concatenate([jnp.where(hm, t, zero) for hm in _head_masks()], axis=0)


def _unstack_heads(t4):
    out = None
    for hl, hm in enumerate(_head_masks()):
        part = jnp.where(hm, t4[hl * WINDOW:(hl + 1) * WINDOW], 0.0)
        out = part if out is None else out + part
    return out


def _sink_block(sink_ref, g):
    return jnp.concatenate([jnp.broadcast_to(sink_ref[g * GROUP + hl:g * GROUP + hl + 1, 0:1], (WINDOW, 256))
                            for hl in range(GROUP)], axis=0)


def _sink_column(sink_ref, g):
    return jnp.concatenate([jnp.broadcast_to(sink_ref[g * GROUP + hl:g * GROUP + hl + 1, 0:1], (WINDOW, 1))
                            for hl in range(GROUP)], axis=0)


def _attn_fwd(pq, cos_t, sin_t, sinks_b, *, name, ts=256):
    S = pq.shape[0]
    ts = min(ts, S)
    nq = ts // WINDOW
    scale = HEAD_DIM ** -0.5

    def body(cur_ref, prev_ref, cosc_ref, sinc_ref, cosp_ref, sinp_ref, sink_ref, o_ref, lse_ref):
        i = pl.program_id(0)
        cosc, sinc = cosc_ref[...], sinc_ref[...]
        q = cur_ref[:, 0:512].astype(F32)
        qr = jnp.concatenate(
            [_rope_fwd(q[:, j * 128:(j + 1) * 128], cosc, sinc) for j in range(4)], axis=1) * scale
        qr = qr.astype(BF16)
        kc = _rope_fwd(cur_ref[:, 512:640].astype(F32), cosc, sinc)
        kp = _rope_fwd(prev_ref[:, 0:128].astype(F32), cosp_ref[...], sinp_ref[...])
        k_all = jnp.concatenate([kp, kc], axis=0)
        v_all = jnp.concatenate([prev_ref[:, 128:256].astype(F32), cur_ref[:, 640:768].astype(F32)], axis=0)
        kt = [t.astype(BF16) for t in _tile_heads(k_all)]
        vt = [t.astype(BF16) for t in _tile_heads(v_all)]
        band, col = _band4(2 * WINDOW)
        ones = jnp.ones((2 * WINDOW, 256), BF16)
        key_t = lax.broadcasted_iota(jnp.int32, (2 * WINDOW, GROUP * WINDOW), 0)
        qry_t = lax.broadcasted_iota(jnp.int32, (2 * WINDOW, GROUP * WINDOW), 1) % WINDOW
        band_t = (key_t > qry_t) & (key_t <= qry_t + WINDOW)
        NT = (((1,), (1,)), ((), ()))
        for qb in range(nq):
            rows = slice(qb * WINDOW, (qb + 1) * WINDOW)
            keys = slice(qb * WINDOW, (qb + 2) * WINDOW)
            valid = band & ((col >= WINDOW) | (i * nq + qb > 0))
            valid_t = band_t & ((key_t >= WINDOW) | (i * nq + qb > 0))
            for g in range(2):
                qs = _stack_heads(qr[rows, g * 256:(g + 1) * 256])
                sink = _sink_block(sink_ref, g)
                s = lax.dot_general(qs, kt[g][keys], NT, preferred_element_type=F32)
                s_t = lax.dot_general(kt[g][keys], qs, NT, preferred_element_type=F32)
                m_t = jnp.max(jnp.where(valid_t, s_t, MASK_VALUE), axis=0, keepdims=True)
                m_rep = jnp.broadcast_to(m_t, (WINDOW, GROUP * WINDOW)).T
                m = jnp.maximum(jnp.concatenate([m_rep, m_rep], axis=1), sink)
                e = jnp.exp(jnp.where(valid, s, MASK_VALUE) - m).astype(BF16)
                l = jnp.dot(e, ones, preferred_element_type=F32) + jnp.exp(sink - m)
                pv = jnp.dot(e, vt[g][keys], preferred_element_type=F32)
                o_ref[rows, g * 256:(g + 1) * 256] = (_unstack_heads(pv) / _unstack_heads(l)).astype(BF16)
                lse4 = (m + jnp.log(l))[:, 0:1]
                for hl in range(GROUP):
                    h = g * GROUP + hl
                    lse_ref[rows, h:h + 1] = lse4[hl * WINDOW:(hl + 1) * WINDOW]

    hb = ts // WINDOW
    cur = lambda i: (i, 0)
    prev = lambda i: (jnp.maximum(i * hb - 1, 0), 0)
    return pl.pallas_call(
        body, name=name, grid=(S // ts,),
        in_specs=[pl.BlockSpec((ts, 768), cur),
                  pl.BlockSpec((WINDOW, 256), lambda i: (jnp.maximum(i * hb - 1, 0), 2)),
                  pl.BlockSpec((ts, 128), cur), pl.BlockSpec((ts, 128), cur),
                  pl.BlockSpec((WINDOW, 128), prev), pl.BlockSpec((WINDOW, 128), prev),
                  pl.BlockSpec((8, 128), lambda i: (0, 0))],
        out_specs=[pl.BlockSpec((ts, 512), cur), pl.BlockSpec((ts, 8), cur)],
        out_shape=(jax.ShapeDtypeStruct((S, 512), BF16), jax.ShapeDtypeStruct((S, 8), F32)),
        compiler_params=_cp("parallel"),
    )(pq, pq, cos_t, sin_t, cos_t, sin_t, sinks_b)


def _attn_bwd(pq, cos_t, sin_t, sinks_b, do, o, lse, *, name, ts=256):
    S = pq.shape[0]
    ts = min(ts, S)
    nq = ts // WINDOW
    nt = S // ts
    scale = HEAD_DIM ** -0.5
    NT = (((1,), (1,)), ((), ()))
    TN = (((0,), (0,)), ((), ()))

    def body(cur_ref, prev_ref, nxt_ref, cosc_ref, sinc_ref, cosp_ref, sinp_ref, cosn_ref, sinn_ref, sink_ref,
             doc_ref, don_ref, oc_ref, on_ref, lsec_ref, lsen_ref, dpq_ref, dsink_ref):
        i = pl.program_id(0)
        last = i == nt - 1
        cosc, sinc = cosc_ref[...], sinc_ref[...]
        cose = jnp.concatenate([cosc, cosn_ref[...]], axis=0)
        sine = jnp.concatenate([sinc, sinn_ref[...]], axis=0)
        q = jnp.concatenate([cur_ref[:, 0:512], nxt_ref[:, 0:512]], axis=0).astype(F32)
        qr = jnp.concatenate(
            [_rope_fwd(q[:, j * 128:(j + 1) * 128], cose, sine) for j in range(4)], axis=1) * scale
        qr = qr.astype(BF16)
        kc = _rope_fwd(cur_ref[:, 512:640].astype(F32), cosc, sinc)
        kp = _rope_fwd(prev_ref[:, 0:128].astype(F32), cosp_ref[...], sinp_ref[...])
        k_all = jnp.concatenate([kp, kc], axis=0)
        v_all = jnp.concatenate([prev_ref[:, 128:256].astype(F32), cur_ref[:, 640:768].astype(F32)], axis=0)
        kt = [t.astype(BF16) for t in _tile_heads(k_all)]
        vt = [t.astype(BF16) for t in _tile_heads(v_all)]
        don = jnp.where(last, jnp.zeros_like(don_ref[...]), don_ref[...])
        do_e = jnp.concatenate([doc_ref[...], don], axis=0)
        o_e = jnp.concatenate([oc_ref[...], on_ref[...]], axis=0)
        band2, col2 = _band4(2 * WINDOW)
        band1, _ = _band4(WINDOW)
        ones = jnp.ones((256, 256), BF16)

        @pl.when(i == 0)
        def _():
            dsink_ref[...] = jnp.zeros_like(dsink_ref)

        dk_acc = [[None] * (nq + 1) for _ in range(2)]
        dv_acc = [[None] * (nq + 1) for _ in range(2)]

        def add(acc, g, e, val):
            acc[g][e] = val if acc[g][e] is None else acc[g][e] + val

        for qb in range(nq + 1):
            halo = qb == nq
            rows = slice(qb * WINDOW, (qb + 1) * WINDOW)
            if halo:
                keys = slice(qb * WINDOW, (qb + 1) * WINDOW)
                valid = band1 & jnp.logical_not(last)
            else:
                keys = slice(qb * WINDOW, (qb + 2) * WINDOW)
                valid = band2 & ((col2 >= WINDOW) | (i * nq + qb > 0))
            dq_parts = []
            for g in range(2):
                qs = _stack_heads(qr[rows, g * 256:(g + 1) * 256])
                dos = _stack_heads(do_e[rows, g * 256:(g + 1) * 256])
                o_g = o_e[rows, g * 256:(g + 1) * 256].astype(F32)
                kt_b, vt_b = kt[g][keys], vt[g][keys]
                lse_src = lsen_ref if halo else lsec_ref
                lse_rows = slice(0, WINDOW) if halo else rows
                big_l = jnp.concatenate([lse_src[lse_rows, g * GROUP + hl:g * GROUP + hl + 1] for hl in range(GROUP)],
                                        axis=0)
                delta = jnp.dot((dos.astype(F32) * jnp.concatenate([o_g] * GROUP, axis=0)).astype(BF16), ones,
                                preferred_element_type=F32)
                s = lax.dot_general(qs, kt_b, NT, preferred_element_type=F32)
                p = jnp.exp(jnp.where(valid, s, MASK_VALUE) - big_l)
                dp = lax.dot_general(dos, vt_b, NT, preferred_element_type=F32)
                ds = (p * (dp - delta[:, 0:p.shape[1]])).astype(BF16)
                dk_g = lax.dot_general(ds, qs, TN, preferred_element_type=F32)
                dv_g = lax.dot_general(p.astype(BF16), dos, TN, preferred_element_type=F32)
                if not halo:
                    dq_parts.append(_unstack_heads(jnp.dot(ds, kt_b, preferred_element_type=F32)))
                    dsink4 = jnp.exp(_sink_column(sink_ref, g) - big_l) * delta[:, 0:1]
                    for hl in range(GROUP):
                        h = g * GROUP + hl
                        dsink_h = -jnp.sum(dsink4[hl * WINDOW:(hl + 1) * WINDOW], axis=0, keepdims=True)
                        dsink_ref[h:h + 1, :] += jnp.broadcast_to(dsink_h, (1, 128))
                add(dk_acc, g, qb, dk_g[0:WINDOW])
                add(dv_acc, g, qb, dv_g[0:WINDOW])
                if not halo:
                    add(dk_acc, g, qb + 1, dk_g[WINDOW:2 * WINDOW])
                    add(dv_acc, g, qb + 1, dv_g[WINDOW:2 * WINDOW])
            if not halo:
                cs, sn = cosc[rows], sinc[rows]
                for g in range(2):
                    dq_g = dq_parts[g] * scale
                    for j in range(2):
                        c0 = g * 256 + j * 128
                        dpq_ref[rows, c0:c0 + 128] = _rope_bwd(dq_g[:, j * 128:(j + 1) * 128], cs, sn).astype(BF16)
        for e in range(1, nq + 1):
            rows = slice((e - 1) * WINDOW, e * WINDOW)
            dk = _fold_heads(dk_acc[0][e], dk_acc[1][e])
            dv = _fold_heads(dv_acc[0][e], dv_acc[1][e])
            dpq_ref[rows, 512:640] = _rope_bwd(dk, cosc[rows], sinc[rows]).astype(BF16)
            dpq_ref[rows, 640:768] = dv.astype(BF16)

    hb = ts // WINDOW
    nblk = S // WINDOW
    cur = lambda i: (i, 0)
    prev = lambda i: (jnp.maximum(i * hb - 1, 0), 0)
    nxt = lambda i: (jnp.minimum((i + 1) * hb, nblk - 1), 0)
    return pl.pallas_call(
        body, name=name, grid=(nt,),
        in_specs=[pl.BlockSpec((ts, 768), cur),
                  pl.BlockSpec((WINDOW, 256), lambda i: (jnp.maximum(i * hb - 1, 0), 2)),
                  pl.BlockSpec((WINDOW, 768), nxt),
                  pl.BlockSpec((ts, 128), cur), pl.BlockSpec((ts, 128), cur),
                  pl.BlockSpec((WINDOW, 128), prev), pl.BlockSpec((WINDOW, 128), prev),
                  pl.BlockSpec((WINDOW, 128), nxt), pl.BlockSpec((WINDOW, 128), nxt),
                  pl.BlockSpec((8, 128), lambda i: (0, 0)),
                  pl.BlockSpec((ts, 512), cur), pl.BlockSpec((WINDOW, 512), nxt),
                  pl.BlockSpec((ts, 512), cur), pl.BlockSpec((WINDOW, 512), nxt),
                  pl.BlockSpec((ts, 8), cur), pl.BlockSpec((WINDOW, 8), nxt)],
        out_specs=[pl.BlockSpec((ts, 768), cur), pl.BlockSpec((8, 128), lambda i: (0, 0))],
        out_shape=(jax.ShapeDtypeStruct((S, 768), BF16), jax.ShapeDtypeStruct((8, 128), F32)),
        compiler_params=_cp("arbitrary"),
    )(pq, pq, pq, cos_t, sin_t, cos_t, sin_t, cos_t, sin_t, sinks_b, do, do, o, o, lse, lse)


def _shift_dn(x, k):
    return pltpu.roll(x, k, 0)


def _shift_up(x, k):
    return pltpu.roll(x, x.shape[0] - k, 0)


def _pool_lane_select(vals):
    lane = lax.broadcasted_iota(jnp.int32, (1, 256), 1)
    out = vals[3]
    for g in (2, 1, 0):
        out = jnp.where(lane < 64 * (g + 1), vals[g], out)
    return out


def _pool_inv_count(t0, n):
    t = t0 + lax.broadcasted_iota(jnp.int32, (n, 256), 0)
    lane = lax.broadcasted_iota(jnp.int32, (n, 256), 1)
    w = jnp.where(lane < 64, 2, jnp.where(lane < 128, 4, jnp.where(lane < 192, 8, 16)))
    return 1.0 / jnp.minimum(t + 1, w).astype(F32)


def _pooled(u_ext, t0, n):
    s2 = u_ext + _shift_dn(u_ext, 1)
    s4 = s2 + _shift_dn(s2, 2)
    s8 = s4 + _shift_dn(s4, 4)
    s16 = s8 + _shift_dn(s8, 8)
    win = _pool_lane_select([s2, s4, s8, s16])[HALO:HALO + n]
    return win * _pool_inv_count(t0, n) - u_ext[HALO:HALO + n]


def _poolconv_fwd(pp, wbd, pool_scale, conv_w, *, name, ts=512):
    S = pp.shape[0]
    ts = min(ts, S)

    def body(cur_ref, prev_ref, wbd_ref, sc_ref, cw_ref, oa_ref, oc_ref):
        i = pl.program_id(0)
        prev = jnp.where(i > 0, prev_ref[...].astype(F32), 0.0)
        u_ext = jnp.concatenate([prev[:, 0:256], cur_ref[:, 0:256].astype(F32)], axis=0)
        pooled = _pooled(u_ext, i * ts, ts)
        mixed = jnp.dot(pooled.astype(BF16), wbd_ref[...], preferred_element_type=F32)
        oa_ref[...] = (mixed * sc_ref[...]).astype(BF16)
        v_ext = jnp.concatenate([prev[:, 256:512] * prev[:, 768:1024],
                                 cur_ref[:, 256:512].astype(F32) * cur_ref[:, 768:1024].astype(F32)], axis=0)
        cv = cw_ref[2:3, :] * v_ext + cw_ref[1:2, :] * _shift_dn(v_ext, 1) + cw_ref[0:1, :] * _shift_dn(v_ext, 2)
        oc_ref[...] = (cur_ref[:, 512:768].astype(F32) * cv[HALO:HALO + ts]).astype(BF16)

    hb = ts // HALO
    cur = lambda i: (i, 0)
    const = lambda i: (0, 0)
    return pl.pallas_call(
        body, name=name, grid=(S // ts,),
        in_specs=[pl.BlockSpec((ts, 1024), cur),
                  pl.BlockSpec((HALO, 1024), lambda i: (jnp.maximum(i * hb - 1, 0), 0)),
                  pl.BlockSpec((256, 256), const), pl.BlockSpec((1, 256), const), pl.BlockSpec((3, 256), const)],
        out_specs=[pl.BlockSpec((ts, 256), cur)] * 2,
        out_shape=(jax.ShapeDtypeStruct((S, 256), BF16),) * 2,
        compiler_params=_cp("parallel"),
    )(pp, pp, wbd, pool_scale, conv_w)


def _poolconv_bwd(pp, do_a, do_c, wbd, wbd_t, pool_scale, conv_w, *, name, ts=512):
    S = pp.shape[0]
    ts = min(ts, S)
    nt = S // ts
    n_e = ts + 2 * HALO

    def body(cur_ref, prev_ref, nxt_ref, dac_ref, dan_ref, dcc_ref, dcn_ref, wbd_ref, wbdt_ref, sc_ref, cw_ref,
             dpp_ref, pooled_ref, dmixed_ref, dsc_ref, dcw_ref):
        i = pl.program_id(0)

        @pl.when(i == 0)
        def _():
            dsc_ref[...] = jnp.zeros_like(dsc_ref)
            dcw_ref[...] = jnp.zeros_like(dcw_ref)

        prev = jnp.where(i > 0, prev_ref[...].astype(F32), 0.0)
        nxt = nxt_ref[...].astype(F32)
        cur = cur_ref[...].astype(F32)
        not_last = i < nt - 1
        da_n = jnp.where(not_last, dan_ref[...].astype(F32), 0.0)
        dc_n = jnp.where(not_last, dcn_ref[...].astype(F32), 0.0)
        zeros_h = jnp.zeros((HALO, 256), F32)
        sc = sc_ref[...]

        u_ext = jnp.concatenate([prev[:, 0:256], cur[:, 0:256]], axis=0)
        pooled = _pooled(u_ext, i * ts, ts)
        pooled_b = pooled.astype(BF16)
        pooled_ref[...] = pooled_b
        mixed = jnp.dot(pooled_b, wbd_ref[...], preferred_element_type=F32)
        da_c = dac_ref[...].astype(F32)
        dsc_ref[...] += jnp.sum(da_c * mixed, axis=0, keepdims=True)
        dmixed_e = jnp.concatenate([da_c, da_n], axis=0) * sc
        dmixed_ref[...] = dmixed_e[0:ts].astype(BF16)
        dpooled = jnp.dot(dmixed_e.astype(BF16), wbdt_ref[...], preferred_element_type=F32)
        qd = dpooled * _pool_inv_count(i * ts, ts + HALO)
        f2 = qd + _shift_up(qd, 1)
        f4 = f2 + _shift_up(f2, 2)
        f8 = f4 + _shift_up(f4, 4)
        f16 = f8 + _shift_up(f8, 8)
        du = (_pool_lane_select([f2, f4, f8, f16]) - dpooled)[0:ts]
        dpp_ref[:, 0:256] = du.astype(BF16)

        xc_e = jnp.concatenate([prev[:, 256:512], cur[:, 256:512], nxt[:, 256:512]], axis=0)
        gc_e = jnp.concatenate([prev[:, 768:1024], cur[:, 768:1024], nxt[:, 768:1024]], axis=0)
        gb_e = jnp.concatenate([zeros_h, cur[:, 512:768], nxt[:, 512:768]], axis=0)
        dc_e = jnp.concatenate([zeros_h, dcc_ref[...].astype(F32), dc_n], axis=0)
        v_e = xc_e * gc_e
        v1, v2 = _shift_dn(v_e, 1), _shift_dn(v_e, 2)
        w0, w1, w2 = cw_ref[0:1, :], cw_ref[1:2, :], cw_ref[2:3, :]
        cv = w2 * v_e + w1 * v1 + w0 * v2
        dcv = dc_e * gb_e
        dv = w2 * dcv + w1 * _shift_up(dcv, 1) + w0 * _shift_up(dcv, 2)
        tile = slice(HALO, HALO + ts)
        dpp_ref[:, 256:512] = (dv * gc_e)[tile].astype(BF16)
        dpp_ref[:, 512:768] = (dc_e * cv)[tile].astype(BF16)
        dpp_ref[:, 768:1024] = (dv * xc_e)[tile].astype(BF16)
        dcv_t = dcv[tile]
        dcw_ref[0:1, :] += jnp.sum(dcv_t * v2[tile], axis=0, keepdims=True)
        dcw_ref[1:2, :] += jnp.sum(dcv_t * v1[tile], axis=0, keepdims=True)
        dcw_ref[2:3, :] += jnp.sum(dcv_t * v_e[tile], axis=0, keepdims=True)

    hb = ts // HALO
    nblk = S // HALO
    cur = lambda i: (i, 0)
    const = lambda i: (0, 0)
    prev = lambda i: (jnp.maximum(i * hb - 1, 0), 0)
    nxt = lambda i: (jnp.minimum((i + 1) * hb, nblk - 1), 0)
    del n_e
    return pl.pallas_call(
        body, name=name, grid=(nt,),
        in_specs=[pl.BlockSpec((ts, 1024), cur), pl.BlockSpec((HALO, 1024), prev), pl.BlockSpec((HALO, 1024), nxt),
                  pl.BlockSpec((ts, 256), cur), pl.BlockSpec((HALO, 256), nxt),
                  pl.BlockSpec((ts, 256), cur), pl.BlockSpec((HALO, 256), nxt),
                  pl.BlockSpec((256, 256), const), pl.BlockSpec((256, 256), const),
                  pl.BlockSpec((1, 256), const), pl.BlockSpec((3, 256), const)],
        out_specs=[pl.BlockSpec((ts, 1024), cur), pl.BlockSpec((ts, 256), cur), pl.BlockSpec((ts, 256), cur),
                   pl.BlockSpec((1, 256), const), pl.BlockSpec((3, 256), const)],
        out_shape=(jax.ShapeDtypeStruct((S, 1024), BF16), jax.ShapeDtypeStruct((S, 256), BF16),
                   jax.ShapeDtypeStruct((S, 256), BF16), jax.ShapeDtypeStruct((1, 256), F32),
                   jax.ShapeDtypeStruct((3, 256), F32)),
        compiler_params=_cp("arbitrary"),
    )(pp, pp, pp, do_a, do_a, do_c, do_c, wbd, wbd_t, pool_scale, conv_w)


def _sigmoid(x):
    return 0.5 * jnp.tanh(0.5 * x) + 0.5


def _merge_fwd(o_a, o_b, o_c, glog, w_br, *, name, ts=512):
    S = o_a.shape[0]
    D = w_br.shape[1]
    ts = min(ts, S)

    def body(oa_ref, ob_ref, oc_ref, gl_ref, w_ref, m_ref):
        pa = jnp.dot(oa_ref[...], w_ref[0:256, :], preferred_element_type=F32)
        pb = jnp.dot(ob_ref[...], w_ref[256:768, :], preferred_element_type=F32)
        pc = jnp.dot(oc_ref[...], w_ref[768:1024, :], preferred_element_type=F32)
        m = _sigmoid(gl_ref[:, 0:D].astype(F32)) * pa
        m = m + _sigmoid(gl_ref[:, D:2 * D].astype(F32)) * pb
        m = m + _sigmoid(gl_ref[:, 2 * D:3 * D].astype(F32)) * pc
        m_ref[...] = m.astype(BF16)

    cur = lambda i: (i, 0)
    return pl.pallas_call(
        body, name=name, grid=(S // ts,),
        in_specs=[pl.BlockSpec((ts, 256), cur), pl.BlockSpec((ts, 512), cur), pl.BlockSpec((ts, 256), cur),
                  pl.BlockSpec((ts, 3 * D), cur), pl.BlockSpec((1024, D), lambda i: (0, 0))],
        out_specs=pl.BlockSpec((ts, D), cur),
        out_shape=jax.ShapeDtypeStruct((S, D), BF16),
        compiler_params=_cp("parallel"),
    )(o_a, o_b, o_c, glog, w_br)


def _merge_bwd(dm, o_a, o_b, o_c, glog, w_br, w_br_t, *, name, ts=256):
    S = o_a.shape[0]
    D = w_br.shape[1]
    ts = min(ts, S)

    def body(dm_ref, oa_ref, ob_ref, oc_ref, gl_ref, w_ref, wt_ref, dgl_ref, dp_ref, doa_ref, dob_ref, doc_ref):
        dmv = dm_ref[...].astype(F32)
        branches = ((oa_ref, 0, 256, doa_ref), (ob_ref, 256, 768, dob_ref), (oc_ref, 768, 1024, doc_ref))
        for b, (o_ref, r0, r1, do_ref) in enumerate(branches):
            prod = jnp.dot(o_ref[...], w_ref[r0:r1, :], preferred_element_type=F32)
            gate = _sigmoid(gl_ref[:, b * D:(b + 1) * D].astype(F32))
            dgl_ref[:, b * D:(b + 1) * D] = (dmv * prod * gate * (1.0 - gate)).astype(BF16)
            dprod = (dmv * gate).astype(BF16)
            dp_ref[:, b * D:(b + 1) * D] = dprod
            do_ref[...] = jnp.dot(dprod, wt_ref[:, r0:r1], preferred_element_type=F32).astype(BF16)

    cur = lambda i: (i, 0)
    const = lambda i: (0, 0)
    return pl.pallas_call(
        body, name=name, grid=(S // ts,),
        in_specs=[pl.BlockSpec((ts, D), cur), pl.BlockSpec((ts, 256), cur), pl.BlockSpec((ts, 512), cur),
                  pl.BlockSpec((ts, 256), cur), pl.BlockSpec((ts, 3 * D), cur),
                  pl.BlockSpec((1024, D), const), pl.BlockSpec((D, 1024), const)],
        out_specs=[pl.BlockSpec((ts, 3 * D), cur), pl.BlockSpec((ts, 3 * D), cur), pl.BlockSpec((ts, 256), cur),
                   pl.BlockSpec((ts, 512), cur), pl.BlockSpec((ts, 256), cur)],
        out_shape=(jax.ShapeDtypeStruct((S, 3 * D), BF16), jax.ShapeDtypeStruct((S, 3 * D), BF16),
                   jax.ShapeDtypeStruct((S, 256), BF16), jax.ShapeDtypeStruct((S, 512), BF16),
                   jax.ShapeDtypeStruct((S, 256), BF16)),
        compiler_params=_cp("parallel"),
    )(dm, o_a, o_b, o_c, glog, w_br, w_br_t)


FFN_CHUNK = 128
FFN_DOT_CHUNKS = 4


def _conv3(x, w_ref, cols):
    x1, x2 = _shift_dn(x, 1), _shift_dn(x, 2)
    return w_ref[2:3, cols] * x + w_ref[1:2, cols] * x1 + w_ref[0:1, cols] * x2, x1, x2


def _ffn_down_fwd(up_pre, fcw, w_down3, resid, gamma, beta, *, alpha, name, tc, ts=256, gather=None):
    S, F2 = up_pre.shape
    D = resid.shape[1]
    ts = min(ts, S)
    nt = S // ts
    nj = F2 // (2 * tc)
    has_ag = gather is not None
    n_g = len(gather) if has_ag else 0

    def body(cur_ref, prev_ref, w_ref, wd_ref, r_ref, g_ref, be_ref, *rest):
        h_ref, y_ref, xh_ref, rs_ref, up_ref = rest[n_g:n_g + 5]
        acc_ref = rest[2 * n_g + 5]
        if has_ag:
            ag_refs = (rest[:n_g], rest[n_g + 5:2 * n_g + 5]) + tuple(rest[2 * n_g + 6:2 * n_g + 9])
        i, j = pl.program_id(0), pl.program_id(1)
        if has_ag:
            @pl.when((i == 0) & (j == 0))
            def _():
                _ag_start(*ag_refs)

        part = None
        for c in range(tc // FFN_CHUNK):
            halves = []
            for half in range(2):
                cols = slice(half * tc + c * FFN_CHUNK, half * tc + (c + 1) * FFN_CHUNK)
                prev = jnp.where(i > 0, prev_ref[:, cols].astype(F32), 0.0)
                x = jnp.concatenate([prev, cur_ref[:, cols].astype(F32)], axis=0)
                halves.append(_conv3(x, w_ref, cols)[0][HALO:HALO + ts])
                up_ref[:, cols] = halves[-1].astype(BF16)
            a, b = halves
            h_ref[:, c * FFN_CHUNK:(c + 1) * FFN_CHUNK] = (a * _sigmoid(a) * b).astype(BF16)
            if (c + 1) % FFN_DOT_CHUNKS == 0 or c + 1 == tc // FFN_CHUNK:
                k0 = (c // FFN_DOT_CHUNKS) * FFN_DOT_CHUNKS * FFN_CHUNK
                piece = jnp.dot(h_ref[:, k0:(c + 1) * FFN_CHUNK], wd_ref[j, k0:(c + 1) * FFN_CHUNK, :],
                                preferred_element_type=F32)
                part = piece if part is None else part + piece

        @pl.when(j == 0)
        def _():
            acc_ref[...] = part

        @pl.when(j > 0)
        def _():
            acc_ref[...] += part

        @pl.when(j == nj - 1)
        def _():
            z = alpha * r_ref[...] + acc_ref[...]
            mu = jnp.mean(z, axis=-1, keepdims=True)
            zc = z - mu
            var = jnp.mean(zc * zc, axis=-1, keepdims=True)
            rstd = lax.rsqrt(var + LN_EPS)
            xhat = zc * rstd
            y_ref[...] = xhat * g_ref[...] + be_ref[...]
            xh_ref[...] = xhat.astype(BF16)
            rs_ref[...] = rstd

        if has_ag:
            @pl.when((i == nt - 1) & (j == nj - 1))
            def _():
                _ag_finish(*ag_refs)

    hb = ts // HALO
    row = lambda i, j: (i, 0)
    vec = lambda i, j: (0, 0)
    in_specs = [pl.BlockSpec((ts, 2 * tc), lambda i, j: (i, j)),
                pl.BlockSpec((HALO, 2 * tc), lambda i, j: (jnp.maximum(i * hb - 1, 0), j)),
                pl.BlockSpec((3, 2 * tc), lambda i, j: (0, j)),
                pl.BlockSpec((nj, tc, D), lambda i, j: (0, 0, 0)),
                pl.BlockSpec((ts, D), row), pl.BlockSpec((1, D), vec), pl.BlockSpec((1, D), vec)]
    out_specs = [pl.BlockSpec((ts, tc), lambda i, j: (i, j)), pl.BlockSpec((ts, D), row), pl.BlockSpec((ts, D), row),
                 pl.BlockSpec((ts, 1), row), pl.BlockSpec((ts, 2 * tc), lambda i, j: (i, j))]
    out_shape = [jax.ShapeDtypeStruct((S, F2 // 2), BF16), jax.ShapeDtypeStruct((S, D), F32),
                 jax.ShapeDtypeStruct((S, D), BF16), jax.ShapeDtypeStruct((S, 1), F32),
                 jax.ShapeDtypeStruct((S, F2), BF16)]
    args = [up_pre, up_pre, fcw, w_down3, resid, gamma, beta]
    scratch = [pltpu.VMEM((ts, D), F32)]
    if has_ag:
        in_specs += [pl.BlockSpec(memory_space=pl.ANY)] * n_g
        args += list(gather)
        out_specs += [pl.BlockSpec(memory_space=pl.ANY)] * n_g
        out_shape += [jax.ShapeDtypeStruct((N_DEV,) + g.shape, g.dtype) for g in gather]
        scratch += _ag_sems(n_g)
    res = pl.pallas_call(
        body, name=name, grid=(nt, nj), in_specs=in_specs, out_specs=out_specs, out_shape=out_shape,
        scratch_shapes=scratch, compiler_params=_cp("arbitrary", "arbitrary"),
    )(*args)
    return tuple(res[:5]) + ((list(res[5:]),) if has_ag else ())


def _ffn_up_bwd(up_pre, up, dh, fcw, w_up_t3, dz, *, alpha, name, tc, ts=256, scatter=None):
    S, F2 = up_pre.shape
    D = dz.shape[1]
    ts = min(ts, S)
    nt = S // ts
    nj = F2 // (2 * tc)
    has_rs = scatter is not None
    n_s = len(scatter) if has_rs else 0
    tile = slice(0, ts)

    def body(x_ref, upc_ref, upn_ref, dhc_ref, dhn_ref, w_ref, wt_ref, dz_ref, *rest):
        dpre_ref, dx_ref, dw_ref = rest[n_s:n_s + 3]
        acc_ref = rest[2 * n_s + 3]
        if has_rs:
            rs_refs = (rest[:n_s], rest[n_s + 3:2 * n_s + 3], rest[2 * n_s + 4], rest[2 * n_s + 5])
        i, j = pl.program_id(0), pl.program_id(1)

        @pl.when((i == 0) & (j == 0))
        def _():
            dw_ref[...] = jnp.zeros_like(dw_ref)
            if has_rs:
                _rs_chips_start(*rs_refs)

        part = None
        for c in range(tc // FFN_CHUNK):
            lanes = slice(c * FFN_CHUNK, (c + 1) * FFN_CHUNK)
            dh_n = jnp.where(i < nt - 1, dhn_ref[:, lanes].astype(F32), 0.0)
            dh_e = jnp.concatenate([dhc_ref[:, lanes].astype(F32), dh_n], axis=0)
            cols_of = [slice(half * tc + c * FFN_CHUNK, half * tc + (c + 1) * FFN_CHUNK) for half in range(2)]
            a, b = [jnp.concatenate([upc_ref[:, cols].astype(F32), upn_ref[:, cols].astype(F32)], axis=0)
                    for cols in cols_of]
            sg = _sigmoid(a)
            dups = [dh_e * b * (sg * (1.0 + a * (1.0 - sg))), dh_e * (a * sg)]
            for half in range(2):
                cols, dup = cols_of[half], dups[half]
                dup1, dup2 = _shift_up(dup, 1), _shift_up(dup, 2)
                dpre = w_ref[2:3, cols] * dup + w_ref[1:2, cols] * dup1 + w_ref[0:1, cols] * dup2
                dpre_ref[:, cols] = dpre[tile].astype(BF16)
                x = x_ref[:, cols].astype(F32)
                dw_ref[j, 0:1, cols] += jnp.sum(dup2[tile] * x, axis=0, keepdims=True)
                dw_ref[j, 1:2, cols] += jnp.sum(dup1[tile] * x, axis=0, keepdims=True)
                dw_ref[j, 2:3, cols] += jnp.sum(dup[tile] * x, axis=0, keepdims=True)
            if (c + 1) % FFN_DOT_CHUNKS == 0 or c + 1 == tc // FFN_CHUNK:
                k0 = (c // FFN_DOT_CHUNKS) * FFN_DOT_CHUNKS * FFN_CHUNK
                for half in range(2):
                    ks = slice(half * tc + k0, half * tc + (c + 1) * FFN_CHUNK)
                    piece = jnp.dot(dpre_ref[:, ks], wt_ref[j, ks, :], preferred_element_type=F32)
                    part = piece if part is None else part + piece

        @pl.when(j == 0)
        def _():
            acc_ref[...] = part

        @pl.when(j > 0)
        def _():
            acc_ref[...] += part

        @pl.when(j == nj - 1)
        def _():
            dx_ref[...] = acc_ref[...] + alpha * dz_ref[...]

        if has_rs:
            @pl.when((i == nt - 1) & (j == nj - 1))
            def _():
                _rs_chips_finish(*rs_refs)

    hb = ts // HALO
    nblk = S // HALO
    nxt = lambda i, j: (jnp.minimum((i + 1) * hb, nblk - 1), j)
    row = lambda i, j: (i, 0)
    in_specs = [pl.BlockSpec((ts, 2 * tc), lambda i, j: (i, j)),
                pl.BlockSpec((ts, 2 * tc), lambda i, j: (i, j)), pl.BlockSpec((HALO, 2 * tc), nxt),
                pl.BlockSpec((ts, tc), lambda i, j: (i, j)), pl.BlockSpec((HALO, tc), nxt),
                pl.BlockSpec((3, 2 * tc), lambda i, j: (0, j)),
                pl.BlockSpec((nj, 2 * tc, D), lambda i, j: (0, 0, 0)),
                pl.BlockSpec((ts, D), row)]
    out_specs = [pl.BlockSpec((ts, 2 * tc), lambda i, j: (i, j)), pl.BlockSpec((ts, D), row),
                 pl.BlockSpec((nj, 3, 2 * tc), lambda i, j: (0, 0, 0))]
    out_shape = [jax.ShapeDtypeStruct((S, F2), BF16), jax.ShapeDtypeStruct((S, D), F32),
                 jax.ShapeDtypeStruct((nj, 3, 2 * tc), F32)]
    args = [up_pre, up, up, dh, dh, fcw, w_up_t3, dz]
    scratch = [pltpu.VMEM((ts, D), F32)]
    if has_rs:
        in_specs += [pl.BlockSpec(memory_space=pl.ANY)] * n_s
        args += list(scatter)
        out_specs += [pl.BlockSpec(memory_space=pl.ANY)] * n_s
        out_shape += [jax.ShapeDtypeStruct((3,) + q.shape[1:], q.dtype) for q in scatter]
        scratch += _rs_sems(n_s)
    res = pl.pallas_call(
        body, name=name, grid=(nt, nj), in_specs=in_specs, out_specs=out_specs, out_shape=out_shape,
        scratch_shapes=scratch, compiler_params=_cp("arbitrary", "arbitrary"),
    )(*args)
    return tuple(res[:3]) + ((list(res[3:]),) if has_rs else ())


def _ln_bwd(dy, xhat, rstd, gamma, *, name, ts=512):
    S, D = dy.shape
    ts = min(ts, S)

    def body(dy_ref, xh_ref, rs_ref, g_ref, dz_ref, dg_ref, db_ref):
        _ln_bwd_tile(dy_ref[...], xh_ref, rs_ref, g_ref, dz_ref, dg_ref, db_ref, pl.program_id(0) == 0)

    cur = lambda i: (i, 0)
    const = lambda i: (0, 0)
    return pl.pallas_call(
        body, name=name, grid=(S // ts,),
        in_specs=[pl.BlockSpec((ts, D), cur), pl.BlockSpec((ts, D), cur), pl.BlockSpec((ts, 1), cur),
                  pl.BlockSpec((1, D), const)],
        out_specs=[pl.BlockSpec((ts, D), cur), pl.BlockSpec((1, D), const), pl.BlockSpec((1, D), const)],
        out_shape=(jax.ShapeDtypeStruct((S, D), F32), jax.ShapeDtypeStruct((1, D), F32),
                   jax.ShapeDtypeStruct((1, D), F32)),
        compiler_params=_cp("arbitrary"),
    )(dy, xhat, rstd, gamma)


def _loss_head(y, tgt, *, name, ts=512):
    S, D = y.shape
    ts = min(ts, S)

    def body(y_ref, t_ref, dy_ref, sq_ref):
        @pl.when(pl.program_id(0) == 0)
        def _():
            sq_ref[...] = jnp.zeros_like(sq_ref)

        e = y_ref[...] - t_ref[...]
        dy_ref[...] = e * (1.0 / D)
        sq_ref[...] += jnp.sum(e * e, axis=0, keepdims=True)

    cur = lambda i: (i, 0)
    return pl.pallas_call(
        body, name=name, grid=(S // ts,),
        in_specs=[pl.BlockSpec((ts, D), cur), pl.BlockSpec((ts, D), cur)],
        out_specs=[pl.BlockSpec((ts, D), cur), pl.BlockSpec((1, D), lambda i: (0, 0))],
        out_shape=(jax.ShapeDtypeStruct((S, D), F32), jax.ShapeDtypeStruct((1, D), F32)),
        compiler_params=_cp("arbitrary"),
    )(y, tgt)


def _adamw(w, g, m, v, *, name, tr=512):
    lead = w.shape[:-2]
    R, C = w.shape[-2:]
    tr = _div_tile(R, tr)
    c1 = 1.0 - ADAM_B1 ** ADAM_STEP
    c2 = 1.0 - ADAM_B2 ** ADAM_STEP

    def body(w_ref, g_ref, m_ref, v_ref, d_ref, mo_ref, vo_ref):
        gv = g_ref[...]
        m2 = ADAM_B1 * m_ref[...] + (1.0 - ADAM_B1) * gv
        v2 = ADAM_B2 * v_ref[...] + (1.0 - ADAM_B2) * (gv * gv)
        m_hat = m2 / c1
        v_hat = v2 / c2
        d_ref[...] = -ADAM_LR * (m_hat / (jnp.sqrt(v_hat) + ADAM_EPS) + ADAM_WD * w_ref[...])
        mo_ref[...] = m2
        vo_ref[...] = v2

    if lead:
        spec = pl.BlockSpec((1, tr, C), lambda l, i: (l, i, 0))
        grid = (lead[0], R // tr)
    else:
        spec = pl.BlockSpec((tr, C), lambda i: (i, 0))
        grid = (R // tr,)
    return pl.pallas_call(
        body, name=name, grid=grid,
        in_specs=[spec] * 4, out_specs=[spec] * 3,
        out_shape=(jax.ShapeDtypeStruct(w.shape, F32),) * 3,
        compiler_params=_cp(*(("parallel",) * len(grid))),
    )(w, g, m, v)


def _interleave_cols(w, nj):
    lead, f2 = w.shape[:-1], w.shape[-1]
    tc = f2 // (2 * nj)
    w = w.reshape(lead + (2, nj, tc))
    return jnp.swapaxes(w, -3, -2).reshape(lead + (f2,))


def _deinterleave_cols(w, nj):
    lead, f2 = w.shape[:-1], w.shape[-1]
    tc = f2 // (2 * nj)
    w = w.reshape(lead + (nj, 2, tc))
    return jnp.swapaxes(w, -3, -2).reshape(lead + (f2,))


def _block_diag(w_pool):
    return jnp.concatenate([jnp.pad(w_pool[g], ((0, 0), (64 * g, 192 - 64 * g))) for g in range(4)], axis=0)


def _pad_rows(v, rows):
    return jnp.pad(v, (0, rows * LANES - v.shape[0])).reshape(rows, LANES)


def kernel(x, positions, w_in, w_pool, pool_scale, attn_sinks, conv_w, w_branch_a, w_branch_b, w_branch_c, w_o, ln1_g, ln1_b, w_up, ffn_conv_w, w_down, ln2_g, ln2_b, loss_target, m_w_in, m_w_pool, m_pool_scale, m_attn_sinks, m_conv_w, m_w_branch_a, m_w_branch_b, m_w_branch_c, m_w_o, m_ln1_g, m_ln1_b, m_w_up, m_ffn_conv_w, m_w_down, m_ln2_g, m_ln2_b, v_w_in, v_w_pool, v_pool_scale, v_attn_sinks, v_conv_w, v_w_branch_a, v_w_branch_b, v_w_branch_c, v_w_o, v_ln1_g, v_ln1_b, v_w_up, v_ffn_conv_w, v_w_down, v_ln2_g, v_ln2_b):
    L, D, in_shard = w_in.shape
    S = x.shape[1]
    IN = in_shard * N_DEV
    F2 = w_up.shape[2] * N_DEV
    F = F2 // 2
    assert D == 1024 and IN == 1792 + 3 * D and x.shape[0] == 1 and S % 512 == 0
    alpha = (2 * L) ** 0.25
    NJ = 2
    TC = F // NJ
    xs = x.reshape(S, D)
    tgt = loss_target.reshape(S, D)

    big = [w_in, w_branch_a, w_branch_b, w_branch_c, w_o, w_up, w_down]
    PART_A, PART_B = (0, 1, 2, 3, 4), (5, 6)
    rows_l = [a.size // L // LANES for a in big]
    offs_l = [sum(rows_l[:k]) for k in range(len(big) + 1)]

    def pack_part(l, part):
        return [(big[k][l].T if k == 0 else big[k][l]).astype(BF16) for k in part]

    n_cw, n_fw = conv_w.size, ffn_conv_w.size
    small_rows = -(-(n_cw + n_fw) // LANES)
    small = _pad_rows(jnp.concatenate([conv_w.reshape(-1), ffn_conv_w.reshape(-1)]), small_rows)
    gsmall = _all_gather(small, "ag_conv_weights").reshape(N_DEV, -1)
    conv_full = gsmall[:, :n_cw].reshape(N_DEV, L, 3, -1).transpose(1, 2, 0, 3).reshape(L, 3, 256)
    fcw_full = gsmall[:, n_cw:n_cw + n_fw].reshape(N_DEV, L, 3, -1).transpose(1, 2, 0, 3).reshape(L, 3, F2)
    fcw_full = _interleave_cols(fcw_full, NJ)

    def shard_of(g, part, k, shape):
        assert g[part.index(k)].shape == (N_DEV,) + shape
        return g[part.index(k)]

    def unpack_a(g):
        win_t = shard_of(g, PART_A, 0, (in_shard, D)).reshape(IN, D)
        wg_t = win_t[1792:]
        wp_t = jnp.concatenate([win_t[0:256], win_t[1024:1792]], axis=0)
        wq_t = win_t[256:1024]
        wg, wp, wq = wg_t.T, wp_t.T, wq_t.T
        if g[1] is None:
            return dict(wg=wg, wp=wp, wq=wq)
        wa = shard_of(g, PART_A, 1, (256, D // N_DEV)).transpose(1, 0, 2).reshape(256, D)
        wb = shard_of(g, PART_A, 2, (512, D // N_DEV)).transpose(1, 0, 2).reshape(512, D)
        wc = shard_of(g, PART_A, 3, (256, D // N_DEV)).transpose(1, 0, 2).reshape(256, D)
        wbr = jnp.concatenate([wa, wb, wc], axis=0)
        wo = shard_of(g, PART_A, 4, (D // N_DEV, D)).reshape(D, D)
        return dict(wg=wg, wp=wp, wq=wq, wg_t=wg_t, wp_t=wp_t, wq_t=wq_t, wbr=wbr, wbr_t=wbr.T, wo=wo, wo_t=wo.T)

    def unpack_b(g):
        nh = N_DEV // (2 * NJ)
        wup = shard_of(g, PART_B, 5, (D, F2 // N_DEV)).reshape(2, NJ, nh, D, F2 // N_DEV)
        wup = wup.transpose(3, 1, 0, 2, 4).reshape(D, F2)
        wdn = shard_of(g, PART_B, 6, (F // N_DEV, D)).reshape(F, D)
        return dict(wup=wup, wup_t=wup.T, wdn=wdn, wdn_t=wdn.T)

    def local_weights(l):
        wbd = _block_diag(w_pool[l]).astype(BF16)
        return dict(wbd=wbd, wbd_t=wbd.T, scale=pool_scale[l].reshape(1, 256), conv=conv_full[l],
                    fcw=fcw_full[l], sinks=jnp.broadcast_to(attn_sinks[l].reshape(8, 1), (8, 128)),
                    g1=ln1_g[l].reshape(1, D), b1=ln1_b[l].reshape(1, D),
                    g2=ln2_g[l].reshape(1, D), b2=ln2_b[l].reshape(1, D))

    inv_freq = ROPE_THETA ** (-jnp.arange(0, ROT_DIM, 2, dtype=F32) / ROT_DIM)
    head_lane = jnp.concatenate([inv_freq, inv_freq, jnp.zeros((HEAD_DIM - ROT_DIM,), F32)])
    head_sign = jnp.concatenate([-jnp.ones((8,), F32), jnp.ones((8,), F32), jnp.zeros((HEAD_DIM - ROT_DIM,), F32)])
    inv_lane = jnp.tile(head_lane, 2).reshape(1, 128)
    sign_lane = jnp.tile(head_sign, 2).reshape(1, 128)
    cos_t, sin_t = _rope_tables(positions.reshape(S, 1), inv_lane, sign_lane, "rope_tables")

    saved, W = [], []
    h_in = xs
    gathered_a = [_all_gather(pack_part(0, PART_A[:1]), "ag_weights_first")]
    for l in range(L):
        if l == 0:
            w_in_only = unpack_a(gathered_a + [None] * 4)
            (pg, pp, pq), later = _mm_fan(h_in, [w_in_only["wg"], w_in_only["wp"], w_in_only["wq"]], out_dtype=BF16,
                                          name="proj_in", gather=pack_part(0, PART_A[1:]) + pack_part(0, PART_B))
            gathered_a, gathered_b = gathered_a + later[:4], later[4:]
        w = {**unpack_a(gathered_a), **unpack_b(gathered_b), **local_weights(l)}
        W.append(w)
        if l > 0:
            pg, pp, pq = _mm_fan(h_in, [w["wg"], w["wp"], w["wq"]], out_dtype=BF16, name="proj_in")
        o_a, o_c = _poolconv_fwd(pp, w["wbd"], w["scale"], w["conv"], name="poolconv_fwd")
        o_b, lse = _attn_fwd(pq, cos_t, sin_t, w["sinks"], name="attn_fwd")
        merged = _merge_fwd(o_a, o_b, o_c, pg, w["wbr"], name="merge_fwd")
        x1, xh1, rs1 = _mm_ln(merged, w["wo"], h_in, w["g1"], w["b1"], alpha=alpha, name="wo_ln1")
        if l + 1 < L:
            up_pre, gathered_a = _mm(x1, w["wup"], out_dtype=BF16, name="ffn_up",
                                     gather=pack_part(l + 1, PART_A))
        else:
            up_pre = _mm(x1, w["wup"], out_dtype=BF16, name="ffn_up")
        down = dict(alpha=alpha, name="ffn_down", tc=TC)
        wdn3 = w["wdn"].reshape(NJ, TC, D)
        if l + 1 < L:
            hact, x2, xh2, rs2, up, gathered_b = _ffn_down_fwd(up_pre, w["fcw"], wdn3, x1, w["g2"], w["b2"],
                                                               gather=pack_part(l + 1, PART_B), **down)
        else:
            hact, x2, xh2, rs2, up = _ffn_down_fwd(up_pre, w["fcw"], wdn3, x1, w["g2"], w["b2"], **down)
        saved.append(dict(up=up,x0=h_in, pg=pg, pp=pp, pq=pq, o_a=o_a, o_b=o_b, o_c=o_c, lse=lse, merged=merged,
                          x1=x1, xh1=xh1, rs1=rs1, up_pre=up_pre, hact=hact, xh2=xh2, rs2=rs2))
        h_in = x2

    dy, sq_lanes = _loss_head(h_in, tgt, name="loss_head")

    def pack_grads(g):
        col = lambda a, n: a.reshape(a.shape[0], N_DEV, n).transpose(1, 0, 2)
        row = lambda a, n: a.reshape(N_DEV, n, a.shape[1])
        nh = N_DEV // (2 * NJ)
        up_t = g["w_up_t"].reshape(NJ, 2, nh * (F2 // N_DEV), D).transpose(1, 0, 2, 3).reshape(N_DEV, F2 // N_DEV, D)
        rest = [col(g["a"], D // N_DEV), col(g["b"], D // N_DEV), col(g["c"], D // N_DEV),
                row(g["w_o"], D // N_DEV), row(g["w_down"], F // N_DEV)]
        return [row(g["w_in_t"], in_shard).astype(BF16), up_t.astype(BF16),
                jnp.concatenate([p.reshape(N_DEV, -1, LANES).astype(BF16) for p in rest], axis=1)]

    my_c = lax.axis_index("c").astype(jnp.int32).reshape(1)
    my_chip = (2 * lax.axis_index("x") + lax.axis_index("y")).astype(jnp.int32).reshape(1)
    gw = [None] * L
    pair_sum = [None] * L
    from_chips = [None] * L
    for l in reversed(range(L)):
        w, sv = W[l], saved[l]
        if l == L - 1:
            dz2, dg2, db2 = _ln_bwd(dy, sv["xh2"], sv["rs2"], w["g2"], name="ln2_bwd")
        else:
            dz2, dg2, db2 = ln2_out
        dw_dn = _mm_tn(sv["hact"], dz2, name="down_bwd_w", tka=TC, na=NJ, tn=D, ts=1024, out_dtype=BF16)
        up_bwd = dict(alpha=alpha, name="ffn_up_bwd", tc=TC)
        if l + 1 < L:
            dh, from_sibling = _mm(dz2, w["wdn_t"], out_dtype=BF16, name="down_bwd_x", swap=packed_above)
            pair_sum[l + 1] = [_sum_sibling(p, r, my_c, "rs_sum_sibling") for p, r in zip(packed_above, from_sibling)]
        else:
            dh = _mm(dz2, w["wdn_t"], out_dtype=BF16, name="down_bwd_x")
        wup_t3 = w["wup_t"].reshape(NJ, 2 * TC, D)
        if l + 1 < L:
            dpre, dx1, dfcw, from_chips[l + 1] = _ffn_up_bwd(sv["up_pre"], sv["up"], dh, w["fcw"], wup_t3, dz2,
                                                             scatter=pair_sum[l + 1], **up_bwd)
        else:
            dpre, dx1, dfcw = _ffn_up_bwd(sv["up_pre"], sv["up"], dh, w["fcw"], wup_t3, dz2, **up_bwd)
        dfcw = dfcw.transpose(1, 0, 2).reshape(3, F2)
        dw_up_t = _mm_tn(dpre, sv["x1"], name="up_bwd_w", tka=TC, na=2 * NJ, tn=D, ts=1024,
                         out_dtype=BF16)
        dz1, dg1, db1 = _ln_bwd(dx1, sv["xh1"], sv["rs1"], w["g1"], name="ln1_bwd")
        dmerged = _mm(dz1, w["wo_t"], out_dtype=BF16, name="wo_bwd_x")
        dw_o = _mm_tn(sv["merged"], dz1, name="wo_bwd_w", tka=D, tn=D // 2, nb=2, out_dtype=BF16)
        dpg, dprod, do_a, do_b, do_c = _merge_bwd(dmerged, sv["o_a"], sv["o_b"], sv["o_c"], sv["pg"],
                                                  w["wbr"], w["wbr_t"], name="merge_bwd")
        dw_a = _mm_tn(sv["o_a"], dprod, name="branch_a_bwd_w", tka=256, tn=D, b_off=0, out_dtype=BF16)
        dw_b = _mm_tn(sv["o_b"], dprod, name="branch_b_bwd_w", tka=512, tn=D, b_off=1, out_dtype=BF16)
        dw_c = _mm_tn(sv["o_c"], dprod, name="branch_c_bwd_w", tka=256, tn=D, b_off=2, out_dtype=BF16)
        dpq, dsink = _attn_bwd(sv["pq"], cos_t, sin_t, w["sinks"], do_b, sv["o_b"], sv["lse"], name="attn_bwd")
        dpp, pooled, dmixed, dscale, dconv = _poolconv_bwd(sv["pp"], do_a, do_c, w["wbd"], w["wbd_t"], w["scale"],
                                                           w["conv"], name="poolconv_bwd")
        dwbd = _mm_tn(pooled, dmixed, name="pool_bwd_w", tka=256, tn=256)
        dx_args = ([dpg, dpp, dpq], [w["wg_t"], w["wp_t"], w["wq_t"]], dz1)
        if l > 0:
            below = saved[l - 1]
            ln2_out = _mm_sum(*dx_args, add_scale=alpha, name="proj_in_bwd_x",
                              ln=(below["xh2"], below["rs2"], W[l - 1]["g2"]))
        else:
            dx = _mm_sum(*dx_args, add_scale=alpha, name="proj_in_bwd_x")
        dw_g = _mm_tn(dpg, sv["x0"], name="proj_gate_bwd_w", tka=1024, na=3, tn=D, out_dtype=BF16)
        dw_p = _mm_tn(dpp, sv["x0"], name="proj_poolconv_bwd_w", tka=512, na=2, tn=D, out_dtype=BF16)
        dw_q = _mm_tn(dpq, sv["x0"], name="proj_qkv_bwd_w", tka=384, na=2, tn=D, out_dtype=BF16)
        dw_in_t = jnp.concatenate([dw_p[0:256], dw_q, dw_p[256:1024], dw_g], axis=0)
        dw_pool = jnp.stack([dwbd[64 * g:64 * (g + 1), 64 * g:64 * (g + 1)] for g in range(4)])
        gw[l] = dict(w_in_t=dw_in_t, a=dw_a, b=dw_b, c=dw_c, w_o=dw_o, w_up_t=dw_up_t, w_down=dw_dn,
                     w_pool=dw_pool, scale=dscale, sinks=dsink[:, 0], conv=dconv, fcw=_deinterleave_cols(dfcw, NJ),
                     g1=dg1, b1=db1, g2=dg2, b2=db2)
        packed_above = pack_grads(gw[l])
    from_sibling = _rs_sibling(packed_above, "rs_sibling_last")
    pair_sum[0] = [_sum_sibling(p, r, my_c, "rs_sum_sibling") for p, r in zip(packed_above, from_sibling)]
    grad_x = dx.reshape(1, S, D)
    from_chips[0] = _rs_chips(pair_sum[0], "rs_chips_last")
    g_layers = [[_sum_chips(q, r, my_chip, "rs_sum_chips") for q, r in zip(pair_sum[l], from_chips[l])]
                for l in range(L)]

    def stack(k):
        return jnp.stack([gw[l][k] for l in range(L)])

    rep_vec = jnp.concatenate([
        stack("w_pool").reshape(-1), stack("scale").reshape(-1), stack("g1").reshape(-1), stack("b1").reshape(-1),
        stack("g2").reshape(-1), stack("b2").reshape(-1)])
    n_rep_full = -(-rep_vec.shape[0] // LANES)
    sinks_row = jnp.pad(stack("sinks").reshape(-1), (0, LANES - 8 * L))
    rep_vec = jnp.concatenate([_pad_rows(rep_vec, n_rep_full).reshape(-1), sinks_row, sq_lanes.reshape(-1)])
    loss_row = n_rep_full + 1
    n_rep = -(-(loss_row + 1) // 8) * 8
    rep_rows = _pad_rows(rep_vec, n_rep)
    dconv_by_dev = stack("conv").reshape(L, 3, N_DEV, -1).transpose(2, 0, 1, 3).reshape(N_DEV, -1)
    dfcw_by_dev = stack("fcw").reshape(L, 3, N_DEV, -1).transpose(2, 0, 1, 3).reshape(N_DEV, -1)
    n_mine = -(-(small_rows) // 8) * 8
    by_dev = jnp.concatenate([dconv_by_dev, dfcw_by_dev], axis=1)
    by_dev = jnp.pad(by_dev, ((0, 0), (0, n_mine * LANES - by_dev.shape[1]))).reshape(N_DEV * n_mine, LANES)
    small_g = _all_gather(jnp.concatenate([rep_rows, by_dev], axis=0), "ag_small_grads")
    rep_sum, mine_sum, loss11 = _small_reduce(small_g, n_rep, n_mine, 1.0 / D, loss_row, "small_reduce")
    loss = loss11[0, 0]

    names_big = ["w_in", "w_branch_a", "w_branch_b", "w_branch_c", "w_o", "w_up", "w_down"]
    ms_big = [m_w_in, m_w_branch_a, m_w_branch_b, m_w_branch_c, m_w_o, m_w_up, m_w_down]
    vs_big = [v_w_in, v_w_branch_a, v_w_branch_b, v_w_branch_c, v_w_o, v_w_up, v_w_down]
    out = {}
    for k, name in enumerate(names_big):
        wk = big[k]
        if k in (0, 5):
            g_t = jnp.stack([g[0 if k == 0 else 1] for g in g_layers])
            tr_ = lambda a: jnp.swapaxes(a, 1, 2)
            d, mo, vo = _adamw(tr_(wk), g_t, tr_(ms_big[k]), tr_(vs_big[k]), name="adamw_" + name)
            out[name] = (tr_(g_t), tr_(d), tr_(mo), tr_(vo))
            continue
        else:
            rest_ks = (1, 2, 3, 4, 6)
            o = sum(rows_l[q] for q in rest_ks[:rest_ks.index(k)])
            g_nat = jnp.concatenate([g[2][o:o + rows_l[k]] for g in g_layers], axis=0).reshape(wk.shape)
        d, mo, vo = _adamw(wk, g_nat, ms_big[k], vs_big[k], name="adamw_" + name)
        out[name] = (g_nat, d, mo, vo)

    def rep_pack(wp_, sc_, g1_, b1_, g2_, b2_, sk_):
        v = jnp.concatenate([wp_.reshape(-1), sc_.reshape(-1), g1_.reshape(-1), b1_.reshape(-1), g2_.reshape(-1),
                             b2_.reshape(-1)])
        return _pad_rows(jnp.concatenate([_pad_rows(v, n_rep_full).reshape(-1), sk_.reshape(-1)]), n_rep)

    def mine_pack(cw_, fw_):
        return _pad_rows(jnp.concatenate([cw_.reshape(-1), fw_.reshape(-1)]), n_mine)

    w_rep = rep_pack(w_pool, pool_scale, ln1_g, ln1_b, ln2_g, ln2_b, attn_sinks)
    m_rep = rep_pack(m_w_pool, m_pool_scale, m_ln1_g, m_ln1_b, m_ln2_g, m_ln2_b, m_attn_sinks)
    v_rep = rep_pack(v_w_pool, v_pool_scale, v_ln1_g, v_ln1_b, v_ln2_g, v_ln2_b, v_attn_sinks)
    g_rep = jnp.concatenate([rep_sum[:loss_row], jnp.zeros((n_rep - loss_row, LANES), F32)], axis=0)
    rep_res = (g_rep,) + tuple(_adamw(w_rep, g_rep, m_rep, v_rep, name="adamw_replicated"))
    w_mine = mine_pack(conv_w, ffn_conv_w)
    mine_res = (mine_sum,) + tuple(_adamw(w_mine, mine_sum, mine_pack(m_conv_w, m_ffn_conv_w),
                                          mine_pack(v_conv_w, v_ffn_conv_w), name="adamw_conv"))

    def rep_unpack(buf):
        flat = buf.reshape(-1)
        res, o = {}, 0
        for nm, ref in (("w_pool", w_pool), ("pool_scale", pool_scale), ("ln1_g", ln1_g), ("ln1_b", ln1_b),
                        ("ln2_g", ln2_g), ("ln2_b", ln2_b)):
            res[nm] = flat[o:o + ref.size].reshape(ref.shape)
            o += ref.size
        o = n_rep_full * LANES
        res["attn_sinks"] = flat[o:o + attn_sinks.size].reshape(attn_sinks.shape)
        return res

    def mine_unpack(buf):
        flat = buf.reshape(-1)
        return {"conv_w": flat[:n_cw].reshape(conv_w.shape),
                "ffn_conv_w": flat[n_cw:n_cw + n_fw].reshape(ffn_conv_w.shape)}

    order = ["w_in", "w_pool", "pool_scale", "attn_sinks", "conv_w", "w_branch_a", "w_branch_b", "w_branch_c", "w_o",
             "ln1_g", "ln1_b", "w_up", "ffn_conv_w", "w_down", "ln2_g", "ln2_b"]
    results = [loss, grad_x]
    for kind in range(4):
        rep_k, mine_k = rep_unpack(rep_res[kind]), mine_unpack(mine_res[kind])
        for nm in order:
            if nm in out:
                results.append(out[nm][kind])
            elif nm in rep_k:
                results.append(rep_k[nm])
            else:
                results.append(mine_k[nm])
    return tuple(results)
```

```python
import functools

import jax
import jax.numpy as jnp
from jax import lax
from jax.experimental import pallas as pl
from jax.experimental.pallas import tpu as pltpu

F32 = jnp.float32
BF16 = jnp.bfloat16

HEAD_DIM = 64
N_Q_HEADS = 8
GROUP = 4
WINDOW = 128
ROT_DIM = 16
ROPE_THETA = 500000.0
POOL_WINDOWS = (2, 4, 8, 16)
LN_EPS = 1e-5
MASK_VALUE = -1e30
ADAM_LR, ADAM_B1, ADAM_B2, ADAM_EPS, ADAM_WD, ADAM_STEP = 0.001, 0.9, 0.999, 1e-08, 0.01, 10

N_DEV = 8
LANES = 1024
HALO = 16
MESH = pl.DeviceIdType.MESH
VMEM_LIMIT = 56 * 1024 * 1024


def _div_tile(n, want, mult=8):
    for t in range(min(want, n) // mult * mult, 0, -mult):
        if n % t == 0:
            return t
    return n


def _cp(*sem):
    return pltpu.CompilerParams(dimension_semantics=sem, vmem_limit_bytes=VMEM_LIMIT)


def _coords():
    return lax.axis_index("x"), lax.axis_index("y"), lax.axis_index("c")


def _all_gather(xs, name):
    xs = list(xs) if isinstance(xs, (list, tuple)) else [xs]
    n = len(xs)

    def body(*refs):
        ag_refs = (refs[:n], refs[n:2 * n]) + tuple(refs[2 * n:])
        _ag_start(*ag_refs)
        _ag_finish(*ag_refs)

    res = pl.pallas_call(
        body, name=name,
        out_shape=[jax.ShapeDtypeStruct((N_DEV,) + a.shape, a.dtype) for a in xs],
        in_specs=[pl.BlockSpec(memory_space=pl.ANY)] * n,
        out_specs=[pl.BlockSpec(memory_space=pl.ANY)] * n,
        scratch_shapes=_ag_sems(n),
    )(*xs)
    return res if n > 1 else res[0]


def _ag_sems(n):
    return [pltpu.SemaphoreType.DMA((7 * n,)), pltpu.SemaphoreType.DMA((7 * n,)), pltpu.SemaphoreType.DMA((n,))]


def _ag_copies(x_refs, out_refs, send_sems, recv_sems, local_sems):
    x, y, c = _coords()
    me, sibling = (x, y, c), (x, y, 1 - c)
    chips = [(1 - x, y), (x, 1 - y), (1 - x, 1 - y)]
    per_array = []
    for a, (x_ref, out_ref) in enumerate(zip(x_refs, out_refs)):
        def slot(px, py, pc, out_ref=out_ref):
            return out_ref.at[4 * px + 2 * py + pc]

        def copy(k, block, to, src=None, a=a, slot=slot):
            return pltpu.make_async_remote_copy(
                src_ref=slot(*block) if src is None else src, dst_ref=slot(*block),
                send_sem=send_sems.at[7 * a + k], recv_sem=recv_sems.at[7 * a + k],
                device_id=to, device_id_type=MESH)

        mine = pltpu.make_async_copy(x_ref, slot(*me), local_sems.at[a])
        first = [copy(0, me, sibling, src=x_ref)]
        first += [copy(1 + j, me, (*chip, c), src=x_ref) for j, chip in enumerate(chips)]
        passed = [copy(4 + j, (*chip, c), sibling) for j, chip in enumerate(chips)]
        from_chips = [copy(1 + j, (*chip, c), me) for j, chip in enumerate(chips)]
        from_sibling = [copy(0, sibling, me)] + [copy(4 + j, (*chip, 1 - c), me) for j, chip in enumerate(chips)]
        per_array.append((mine, first, passed, from_chips, from_sibling))
    return per_array


def _ag_start(*refs):
    for mine, first, _, _, _ in _ag_copies(*refs):
        mine.start()
        for cp in first:
            cp.start()


def _ag_finish(*refs):
    per_array = _ag_copies(*refs)
    for j in range(3):
        for _, _, passed, from_chips, _ in per_array:
            from_chips[j].wait_recv()
            passed[j].start()
    for mine, first, passed, _, from_sibling in per_array:
        for cp in from_sibling:
            cp.wait_recv()
        for cp in first + passed:
            cp.wait_send()
        mine.wait()


def _rs_sibling(ps, name):
    n = len(ps)

    def body(*refs):
        swap_refs = (refs[:n], refs[n:2 * n], refs[2 * n], refs[2 * n + 1])
        _swap_start(*swap_refs)
        _swap_finish(*swap_refs)

    return pl.pallas_call(
        body, name=name,
        out_shape=[jax.ShapeDtypeStruct((4,) + p.shape[1:], p.dtype) for p in ps],
        in_specs=[pl.BlockSpec(memory_space=pl.ANY)] * n,
        out_specs=[pl.BlockSpec(memory_space=pl.ANY)] * n,
        scratch_shapes=_swap_sems(n),
    )(*ps)


def _swap_sems(n):
    return [pltpu.SemaphoreType.DMA((4 * n,)), pltpu.SemaphoreType.DMA((4 * n,))]


def _swap_copies(p_refs, out_refs, send_sems, recv_sems):
    x, y, c = _coords()
    return [pltpu.make_async_remote_copy(
        src_ref=p_ref.at[4 * (j // 2) + 2 * (j % 2) + (1 - c)], dst_ref=out_ref.at[j],
        send_sem=send_sems.at[4 * a + j], recv_sem=recv_sems.at[4 * a + j],
        device_id=(x, y, 1 - c), device_id_type=MESH)
        for a, (p_ref, out_ref) in enumerate(zip(p_refs, out_refs)) for j in range(4)]


def _swap_start(*refs):
    for cp in _swap_copies(*refs):
        cp.start()


def _swap_finish(*refs):
    copies = _swap_copies(*refs)
    for cp in copies:
        cp.wait_recv()
    for cp in copies:
        cp.wait_send()


def _hosted(kind, arrays):
    n = len(arrays)
    if kind == "gather":
        return _ag_start, _ag_finish, [(N_DEV,) + a.shape for a in arrays], _ag_sems(n)
    assert kind == "swap"
    return _swap_start, _swap_finish, [(4,) + a.shape[1:] for a in arrays], _swap_sems(n)


def _rs_chips(qs, name):
    n = len(qs)

    def body(*refs):
        rs_refs = (refs[:n], refs[n:2 * n], refs[2 * n], refs[2 * n + 1])
        _rs_chips_start(*rs_refs)
        _rs_chips_finish(*rs_refs)

    return pl.pallas_call(
        body, name=name,
        out_shape=[jax.ShapeDtypeStruct((3,) + q.shape[1:], q.dtype) for q in qs],
        in_specs=[pl.BlockSpec(memory_space=pl.ANY)] * n,
        out_specs=[pl.BlockSpec(memory_space=pl.ANY)] * n,
        scratch_shapes=_rs_sems(n),
    )(*qs)


def _rs_sems(n):
    return [pltpu.SemaphoreType.DMA((3 * n,)), pltpu.SemaphoreType.DMA((3 * n,))]


def _rs_chips_copies(q_refs, out_refs, send_sems, recv_sems):
    x, y, c = _coords()
    chips = [(1 - x, y), (x, 1 - y), (1 - x, 1 - y)]
    return [pltpu.make_async_remote_copy(
        src_ref=q_ref.at[2 * cx + cy], dst_ref=out_ref.at[k],
        send_sem=send_sems.at[3 * a + k], recv_sem=recv_sems.at[3 * a + k], device_id=(cx, cy, c),
        device_id_type=MESH)
        for a, (q_ref, out_ref) in enumerate(zip(q_refs, out_refs)) for k, (cx, cy) in enumerate(chips)]


def _rs_chips_start(*refs):
    for cp in _rs_chips_copies(*refs):
        cp.start()


def _rs_chips_finish(*refs):
    copies = _rs_chips_copies(*refs)
    for cp in copies:
        cp.wait_recv()
    for cp in copies:
        cp.wait_send()


def _sum_sibling(p, recv, my_c, name, tr=512):
    _, R, C = p.shape
    tr = _div_tile(R, tr, 16)

    def body(c_ref, p_ref, r_ref, o_ref):
        o_ref[...] = (p_ref[...].astype(F32) + r_ref[...].astype(F32)).astype(o_ref.dtype)

    grid_spec = pltpu.PrefetchScalarGridSpec(
        num_scalar_prefetch=1, grid=(4, R // tr),
        in_specs=[pl.BlockSpec((1, tr, C), lambda j, r, c_ref: (4 * (j // 2) + 2 * (j % 2) + c_ref[0], r, 0)),
                  pl.BlockSpec((1, tr, C), lambda j, r, c_ref: (j, r, 0))],
        out_specs=pl.BlockSpec((1, tr, C), lambda j, r, c_ref: (j, r, 0)))
    return pl.pallas_call(body, name=name, grid_spec=grid_spec,
                          out_shape=jax.ShapeDtypeStruct((4, R, C), p.dtype),
                          compiler_params=_cp("parallel", "parallel"))(my_c, p, recv)


def _sum_chips(q, recv, my_chip, name, tr=512):
    _, R, C = q.shape
    tr = _div_tile(R, tr, 16)

    def body(i_ref, q_ref, r_ref, o_ref):
        acc = q_ref[0].astype(F32)
        for k in range(3):
            acc = acc + r_ref[k].astype(F32)
        o_ref[...] = acc

    grid_spec = pltpu.PrefetchScalarGridSpec(
        num_scalar_prefetch=1, grid=(R // tr,),
        in_specs=[pl.BlockSpec((1, tr, C), lambda r, i_ref: (i_ref[0], r, 0)),
                  pl.BlockSpec((3, tr, C), lambda r, i_ref: (0, r, 0))],
        out_specs=pl.BlockSpec((tr, C), lambda r, i_ref: (r, 0)))
    return pl.pallas_call(body, name=name, grid_spec=grid_spec,
                          out_shape=jax.ShapeDtypeStruct((R, C), F32),
                          compiler_params=_cp("parallel"))(my_chip, q, recv)


def _small_reduce(g, n_rep, n_mine, inv_d, loss_row, name):
    _, R, C = g.shape

    def body(g_ref, rep_ref, mine_ref, loss_ref):
        x, y, c = _coords()
        start = pl.multiple_of(n_rep + (4 * x + 2 * y + c) * n_mine, 8)
        rep = g_ref[0, 0:n_rep, :]
        mine = g_ref[0, pl.ds(start, n_mine), :]
        sq = g_ref[0, loss_row:loss_row + 1, :]
        for d in range(1, N_DEV):
            rep = rep + g_ref[d, 0:n_rep, :]
            mine = mine + g_ref[d, pl.ds(start, n_mine), :]
            sq = sq + g_ref[d, loss_row:loss_row + 1, :]
        rep_ref[...] = rep
        mine_ref[...] = mine
        loss_ref[...] = (0.5 * inv_d) * jnp.sum(sq, axis=1, keepdims=True)

    return pl.pallas_call(
        body, name=name,
        out_shape=(jax.ShapeDtypeStruct((n_rep, C), F32), jax.ShapeDtypeStruct((n_mine, C), F32),
                   jax.ShapeDtypeStruct((1, 1), F32)),
        compiler_params=pltpu.CompilerParams(vmem_limit_bytes=VMEM_LIMIT),
    )(g)


def _mm(a, b, *, out_dtype, name, tm=512, tn=None, tk=None, add=None, add_scale=1.0, gather=None, swap=None):
    M, K = a.shape
    N = b.shape[1]
    tm = min(tm, M)
    tn = N if tn is None else tn
    tk = K if tk is None else tk
    nk = K // tk
    has_add = add is not None
    hosted = gather if gather is not None else swap
    has_ag = hosted is not None
    n_g = len(hosted) if has_ag else 0
    if has_ag:
        comm_start, comm_finish, comm_shapes, comm_sems = _hosted("gather" if gather is not None else "swap", hosted)
    n_i, n_j = M // tm, N // tn

    def body(*refs):
        a_ref, b_ref = refs[0], refs[1]
        add_ref = refs[2] if has_add else None
        n_in = 2 + has_add + n_g
        o_ref = refs[n_in]
        if has_ag:
            ag_refs = (refs[n_in - n_g:n_in], refs[n_in + 1:n_in + 1 + n_g]) + tuple(
                refs[n_in + 1 + n_g:n_in + 1 + n_g + len(comm_sems)])
            pid = (pl.program_id(0), pl.program_id(1), pl.program_id(2))

            @pl.when((pid[0] == 0) & (pid[1] == 0) & (pid[2] == 0))
            def _():
                comm_start(*ag_refs)

        part = jnp.dot(a_ref[...].astype(BF16), b_ref[...].astype(BF16), preferred_element_type=F32)

        def finish(r):
            if has_add:
                r = r + add_scale * add_ref[...].astype(F32)
            o_ref[...] = r.astype(out_dtype)

        if nk == 1:
            finish(part)
        else:
            acc_ref = refs[-1]
            k = pl.program_id(2)

            @pl.when(k == 0)
            def _():
                acc_ref[...] = part

            @pl.when(k > 0)
            def _():
                acc_ref[...] += part

            @pl.when(k == nk - 1)
            def _():
                finish(acc_ref[...])

        if has_ag:
            @pl.when((pid[0] == n_i - 1) & (pid[1] == n_j - 1) & (pid[2] == nk - 1))
            def _():
                comm_finish(*ag_refs)

    b_mode = dict(pipeline_mode=pl.Buffered(1)) if (n_j == 1 and nk == 1) else {}
    in_specs = [pl.BlockSpec((tm, tk), lambda i, j, k: (i, k)),
                pl.BlockSpec((tk, tn), lambda i, j, k: (k, j), **b_mode)]
    args = [a, b]
    if has_add:
        in_specs.append(pl.BlockSpec((tm, tn), lambda i, j, k: (i, j)))
        args.append(add)
    out_specs = [pl.BlockSpec((tm, tn), lambda i, j, k: (i, j))]
    out_shape = [jax.ShapeDtypeStruct((M, N), out_dtype)]
    scratch = []
    if has_ag:
        in_specs += [pl.BlockSpec(memory_space=pl.ANY)] * n_g
        args += list(hosted)
        out_specs += [pl.BlockSpec(memory_space=pl.ANY)] * n_g
        out_shape += [jax.ShapeDtypeStruct(s, g.dtype) for s, g in zip(comm_shapes, hosted)]
        scratch += comm_sems
    if nk > 1:
        scratch.append(pltpu.VMEM((tm, tn), F32))
    sem = ("arbitrary",) * 3 if has_ag else ("parallel", "parallel", "arbitrary")
    res = pl.pallas_call(
        body, name=name, grid=(n_i, n_j, nk), in_specs=in_specs, out_specs=out_specs, out_shape=out_shape,
        scratch_shapes=scratch, compiler_params=_cp(*sem),
    )(*args)
    return (res[0], list(res[1:])) if has_ag else res[0]


def _mm_fan(a, bs, *, out_dtype, name, tm=512, gather=None):
    M, K = a.shape
    tm = min(tm, M)
    n = len(bs)
    n_i = M // tm
    n_g = len(gather) if gather is not None else 0

    def body(*refs):
        outs = refs[1 + n + n_g:1 + 2 * n + n_g]
        if n_g:
            ag_refs = (refs[1 + n:1 + n + n_g], refs[1 + 2 * n + n_g:1 + 2 * n + 2 * n_g]) + tuple(
                refs[1 + 2 * n + 2 * n_g:])

            @pl.when(pl.program_id(0) == 0)
            def _():
                _ag_start(*ag_refs)

        a_v = refs[0][...].astype(BF16)
        for k in range(n):
            outs[k][...] = jnp.dot(a_v, refs[1 + k][...].astype(BF16), preferred_element_type=F32).astype(out_dtype)

        if n_g:
            @pl.when(pl.program_id(0) == n_i - 1)
            def _():
                _ag_finish(*ag_refs)

    row = lambda i: (i, 0)
    hbm = pl.BlockSpec(memory_space=pl.ANY)
    res = pl.pallas_call(
        body, name=name, grid=(n_i,),
        in_specs=[pl.BlockSpec((tm, K), row)] + [pl.BlockSpec(b.shape, lambda i: (0, 0)) for b in bs] + [hbm] * n_g,
        out_specs=[pl.BlockSpec((tm, b.shape[1]), row) for b in bs] + [hbm] * n_g,
        out_shape=([jax.ShapeDtypeStruct((M, b.shape[1]), out_dtype) for b in bs]
                   + [jax.ShapeDtypeStruct((N_DEV,) + g.shape, g.dtype) for g in (gather or [])]),
        scratch_shapes=_ag_sems(n_g) if n_g else [],
        compiler_params=_cp("arbitrary" if n_g else "parallel"),
    )(a, *bs, *(gather or []))
    return (list(res[:n]), list(res[n:])) if n_g else list(res)


def _mm_sum(xs, bs, add, *, add_scale, name, tm=512, ln=None, scatter=None):
    M = xs[0].shape[0]
    N = bs[0].shape[1]
    tm = min(tm, M)
    n = len(xs)
    n_i = M // tm
    n_s = len(scatter) if scatter is not None else 0
    assert not (n_s and ln is not None)

    def body(*refs):
        if n_s:
            rs_refs = (refs[2 * n + 1:2 * n + 1 + n_s], refs[2 * n + 2 + n_s:2 * n + 2 + 2 * n_s],
                       refs[2 * n + 2 + 2 * n_s], refs[2 * n + 3 + 2 * n_s])

            @pl.when(pl.program_id(0) == 0)
            def _():
                _rs_chips_start(*rs_refs)

        acc = add_scale * refs[2 * n][...]
        for k in range(n):
            acc = acc + jnp.dot(refs[k][...].astype(BF16), refs[n + k][...].astype(BF16), preferred_element_type=F32)
        if ln is None:
            refs[2 * n + 1 + n_s][...] = acc
        else:
            xh_ref, rs_ref, g_ref, dz_ref, dg_ref, db_ref = refs[2 * n + 1:]
            _ln_bwd_tile(acc, xh_ref, rs_ref, g_ref, dz_ref, dg_ref, db_ref, pl.program_id(0) == 0)

        if n_s:
            @pl.when(pl.program_id(0) == n_i - 1)
            def _():
                _rs_chips_finish(*rs_refs)

    row = lambda i: (i, 0)
    vec = lambda i: (0, 0)
    hbm = pl.BlockSpec(memory_space=pl.ANY)
    in_specs = ([pl.BlockSpec((tm, x.shape[1]), row) for x in xs]
                + [pl.BlockSpec(b.shape, vec) for b in bs] + [pl.BlockSpec((tm, N), row)])
    if ln is None:
        res = pl.pallas_call(
            body, name=name, grid=(n_i,), in_specs=in_specs + [hbm] * n_s,
            out_specs=[pl.BlockSpec((tm, N), row)] + [hbm] * n_s,
            out_shape=([jax.ShapeDtypeStruct((M, N), F32)]
                       + [jax.ShapeDtypeStruct((3,) + q.shape[1:], q.dtype) for q in (scatter or [])]),
            scratch_shapes=_rs_sems(n_s) if n_s else [],
            compiler_params=_cp("arbitrary" if n_s else "parallel"),
        )(*xs, *bs, add, *(scatter or []))
        return (res[0], list(res[1:])) if n_s else res[0]
    in_specs += [pl.BlockSpec((tm, N), row), pl.BlockSpec((tm, 1), row), pl.BlockSpec((1, N), vec)]
    return pl.pallas_call(
        body, name=name, grid=(M // tm,), in_specs=in_specs,
        out_specs=[pl.BlockSpec((tm, N), row), pl.BlockSpec((1, N), vec), pl.BlockSpec((1, N), vec)],
        out_shape=(jax.ShapeDtypeStruct((M, N), F32), jax.ShapeDtypeStruct((1, N), F32),
                   jax.ShapeDtypeStruct((1, N), F32)),
        compiler_params=_cp("arbitrary"),
    )(*xs, *bs, add, *ln)


def _ln_bwd_tile(dyv, xh_ref, rs_ref, g_ref, dz_ref, dg_ref, db_ref, first):
    @pl.when(first)
    def _():
        dg_ref[...] = jnp.zeros_like(dg_ref)
        db_ref[...] = jnp.zeros_like(db_ref)

    xh = xh_ref[...].astype(F32)
    dyg = dyv * g_ref[...]
    c1 = jnp.mean(dyg, axis=-1, keepdims=True)
    c2 = jnp.mean(dyg * xh, axis=-1, keepdims=True)
    dz_ref[...] = rs_ref[...] * (dyg - c1 - xh * c2)
    dg_ref[...] += jnp.sum(dyv * xh, axis=0, keepdims=True)
    db_ref[...] += jnp.sum(dyv, axis=0, keepdims=True)


def _mm_ln(a, b, resid, gamma, beta, *, alpha, name, tm=512, tk=None):
    M, K = a.shape
    D = b.shape[1]
    tm = min(tm, M)
    tk = K if tk is None else tk
    nk = K // tk

    def body(a_ref, b_ref, r_ref, g_ref, be_ref, y_ref, xh_ref, rs_ref, *scratch):
        part = jnp.dot(a_ref[...].astype(BF16), b_ref[...].astype(BF16), preferred_element_type=F32)

        def finish(acc):
            z = alpha * r_ref[...] + acc
            mu = jnp.mean(z, axis=-1, keepdims=True)
            zc = z - mu
            var = jnp.mean(zc * zc, axis=-1, keepdims=True)
            rstd = lax.rsqrt(var + LN_EPS)
            xhat = zc * rstd
            y_ref[...] = xhat * g_ref[...] + be_ref[...]
            xh_ref[...] = xhat.astype(BF16)
            rs_ref[...] = rstd

        if nk == 1:
            finish(part)
        else:
            acc_ref = scratch[0]
            k = pl.program_id(1)

            @pl.when(k == 0)
            def _():
                acc_ref[...] = part

            @pl.when(k > 0)
            def _():
                acc_ref[...] += part

            @pl.when(k == nk - 1)
            def _():
                finish(acc_ref[...])

    row = lambda i, k: (i, 0)
    vec = lambda i, k: (0, 0)
    return pl.pallas_call(
        body, name=name, grid=(M // tm, nk),
        in_specs=[pl.BlockSpec((tm, tk), lambda i, k: (i, k)), pl.BlockSpec((tk, D), lambda i, k: (k, 0)),
                  pl.BlockSpec((tm, D), row), pl.BlockSpec((1, D), vec), pl.BlockSpec((1, D), vec)],
        out_specs=[pl.BlockSpec((tm, D), row), pl.BlockSpec((tm, D), row), pl.BlockSpec((tm, 1), row)],
        out_shape=(jax.ShapeDtypeStruct((M, D), F32), jax.ShapeDtypeStruct((M, D), BF16),
                   jax.ShapeDtypeStruct((M, 1), F32)),
        scratch_shapes=[pltpu.VMEM((tm, D), F32)] if nk > 1 else [],
        compiler_params=_cp("parallel", "arbitrary"),
    )(a, b, resid, gamma, beta)


def _mm_tn(a, b, *, name, tka, tn, a_off=0, na=1, b_off=0, nb=1, ts=2048, out_dtype=F32):
    S = a.shape[0]
    ts = min(ts, S)
    ns = S // ts
    direct = out_dtype == F32

    def body(a_ref, b_ref, o_ref, *scratch):
        acc_ref = o_ref if direct else scratch[0]
        s = pl.program_id(2)
        part = lax.dot_general(a_ref[...].astype(BF16), b_ref[...].astype(BF16),
                               (((0,), (0,)), ((), ())), preferred_element_type=F32)

        @pl.when(s == 0)
        def _():
            acc_ref[...] = part

        @pl.when(s > 0)
        def _():
            acc_ref[...] += part

        if not direct:
            @pl.when(s == ns - 1)
            def _():
                o_ref[...] = acc_ref[...].astype(out_dtype)

    return pl.pallas_call(
        body, name=name, grid=(na, nb, ns),
        in_specs=[pl.BlockSpec((ts, tka), lambda i, j, s: (s, a_off + i)),
                  pl.BlockSpec((ts, tn), lambda i, j, s: (s, b_off + j))],
        out_specs=pl.BlockSpec((tka, tn), lambda i, j, s: (i, j)),
        out_shape=jax.ShapeDtypeStruct((na * tka, nb * tn), out_dtype),
        scratch_shapes=[] if direct else [pltpu.VMEM((tka, tn), F32)],
        compiler_params=_cp("parallel", "parallel", "arbitrary"),
    )(a, b)


def _rope_tables(pos, inv_lane, sign_lane, name, ts=512):
    S = pos.shape[0]
    ts = min(ts, S)

    def body(p_ref, inv_ref, sg_ref, cos_ref, sin_ref):
        ang = p_ref[...].astype(F32) * inv_ref[...]
        cos_ref[...] = jnp.cos(ang)
        sin_ref[...] = jnp.sin(ang) * sg_ref[...]

    return pl.pallas_call(
        body, name=name, grid=(S // ts,),
        in_specs=[pl.BlockSpec((ts, 1), lambda i: (i, 0)), pl.BlockSpec((1, 128), lambda i: (0, 0)),
                  pl.BlockSpec((1, 128), lambda i: (0, 0))],
        out_specs=[pl.BlockSpec((ts, 128), lambda i: (i, 0))] * 2,
        out_shape=(jax.ShapeDtypeStruct((S, 128), F32),) * 2,
        compiler_params=_cp("parallel"),
    )(pos, inv_lane, sign_lane)


def _rope_swap(t):
    lane = lax.broadcasted_iota(jnp.int32, (1, 128), 1)
    lo = (lane % HEAD_DIM) < (ROT_DIM // 2)
    return jnp.where(lo, pltpu.roll(t, 128 - ROT_DIM // 2, 1), pltpu.roll(t, ROT_DIM // 2, 1))


def _rope_fwd(t, cos, sin):
    return t * cos + _rope_swap(t) * sin


def _rope_bwd(d, cos, sin):
    lane = lax.broadcasted_iota(jnp.int32, (1, 128), 1)
    return d * cos + jnp.where((lane % HEAD_DIM) < ROT_DIM, _rope_swap(d * sin), 0.0)


def _tile_heads(t):
    lane = lax.broadcasted_iota(jnp.int32, (1, 128), 1)
    r = pltpu.roll(t, 64, 1)
    h0 = jnp.where(lane < 64, t, r)
    h1 = jnp.where(lane < 64, r, t)
    return jnp.concatenate([h0, h0], axis=1), jnp.concatenate([h1, h1], axis=1)


def _fold_heads(d0, d1):
    lane = lax.broadcasted_iota(jnp.int32, (1, 128), 1)

    def fold(d):
        s = d[:, 0:128] + d[:, 128:256]
        return s + pltpu.roll(s, 64, 1)

    return jnp.where(lane < 64, fold(d0), fold(d1))


def _band4(n_keys):
    row = lax.broadcasted_iota(jnp.int32, (GROUP * WINDOW, n_keys), 0) % WINDOW
    col = lax.broadcasted_iota(jnp.int32, (GROUP * WINDOW, n_keys), 1)
    return (col > row) & (col <= row + WINDOW), col


def _head_masks():
    lane = lax.broadcasted_iota(jnp.int32, (1, GROUP * HEAD_DIM), 1)
    return [(lane // HEAD_DIM) == hl for hl in range(GROUP)]


def _stack_heads(t):
    zero = jnp.zeros_like(t)
    return jnp.concatenate([jnp.where(hm, t, zero) for hm in _head_masks()], axis=0)


def _unstack_heads(t4):
    out = None
    for hl, hm in enumerate(_head_masks()):
        part = jnp.where(hm, t4[hl * WINDOW:(hl + 1) * WINDOW], 0.0)
        out = part if out is None else out + part
    return out


def _sink_block(sink_ref, g):
    return jnp.concatenate([jnp.broadcast_to(sink_ref[g * GROUP + hl:g * GROUP + hl + 1, 0:1], (WINDOW, 256))
                            for hl in range(GROUP)], axis=0)


def _sink_column(sink_ref, g):
    return jnp.concatenate([jnp.broadcast_to(sink_ref[g * GROUP + hl:g * GROUP + hl + 1, 0:1], (WINDOW, 1))
                            for hl in range(GROUP)], axis=0)


def _attn_fwd(pq, cos_t, sin_t, sinks_b, *, name, ts=256):
    S = pq.shape[0]
    ts = min(ts, S)
    nq = ts // WINDOW
    scale = HEAD_DIM ** -0.5

    def body(cur_ref, prev_ref, cosc_ref, sinc_ref, cosp_ref, sinp_ref, sink_ref, o_ref, lse_ref):
        i = pl.program_id(0)
        cosc, sinc = cosc_ref[...], sinc_ref[...]
        q = cur_ref[:, 0:512].astype(F32)
        qr = jnp.concatenate(
            [_rope_fwd(q[:, j * 128:(j + 1) * 128], cosc, sinc) for j in range(4)], axis=1) * scale
        qr = qr.astype(BF16)
        kc = _rope_fwd(cur_ref[:, 512:640].astype(F32), cosc, sinc)
        kp = _rope_fwd(prev_ref[:, 0:128].astype(F32), cosp_ref[...], sinp_ref[...])
        k_all = jnp.concatenate([kp, kc], axis=0)
        v_all = jnp.concatenate([prev_ref[:, 128:256].astype(F32), cur_ref[:, 640:768].astype(F32)], axis=0)
        kt = [t.astype(BF16) for t in _tile_heads(k_all)]
        vt = [t.astype(BF16) for t in _tile_heads(v_all)]
        band, col = _band4(2 * WINDOW)
        ones = jnp.ones((2 * WINDOW, 256), BF16)
        key_t = lax.broadcasted_iota(jnp.int32, (2 * WINDOW, GROUP * WINDOW), 0)
        qry_t = lax.broadcasted_iota(jnp.int32, (2 * WINDOW, GROUP * WINDOW), 1) % WINDOW
        band_t = (key_t > qry_t) & (key_t <= qry_t + WINDOW)
        NT = (((1,), (1,)), ((), ()))
        for qb in range(nq):
            rows = slice(qb * WINDOW, (qb + 1) * WINDOW)
            keys = slice(qb * WINDOW, (qb + 2) * WINDOW)
            valid = band & ((col >= WINDOW) | (i * nq + qb > 0))
            valid_t = band_t & ((key_t >= WINDOW) | (i * nq + qb > 0))
            for g in range(2):
                qs = _stack_heads(qr[rows, g * 256:(g + 1) * 256])
                sink = _sink_block(sink_ref, g)
                s = lax.dot_general(qs, kt[g][keys], NT, preferred_element_type=F32)
                s_t = lax.dot_general(kt[g][keys], qs, NT, preferred_element_type=F32)
                m_t = jnp.max(jnp.where(valid_t, s_t, MASK_VALUE), axis=0, keepdims=True)
                m_rep = jnp.broadcast_to(m_t, (WINDOW, GROUP * WINDOW)).T
                m = jnp.maximum(jnp.concatenate([m_rep, m_rep], axis=1), sink)
                e = jnp.exp(jnp.where(valid, s, MASK_VALUE) - m).astype(BF16)
                l = jnp.dot(e, ones, preferred_element_type=F32) + jnp.exp(sink - m)
                pv = jnp.dot(e, vt[g][keys], preferred_element_type=F32)
                o_ref[rows, g * 256:(g + 1) * 256] = (_unstack_heads(pv) / _unstack_heads(l)).astype(BF16)
                lse4 = (m + jnp.log(l))[:, 0:1]
                for hl in range(GROUP):
                    h = g * GROUP + hl
                    lse_ref[rows, h:h + 1] = lse4[hl * WINDOW:(hl + 1) * WINDOW]

    hb = ts // WINDOW
    cur = lambda i: (i, 0)
    prev = lambda i: (jnp.maximum(i * hb - 1, 0), 0)
    return pl.pallas_call(
        body, name=name, grid=(S // ts,),
        in_specs=[pl.BlockSpec((ts, 768), cur),
                  pl.BlockSpec((WINDOW, 256), lambda i: (jnp.maximum(i * hb - 1, 0), 2)),
                  pl.BlockSpec((ts, 128), cur), pl.BlockSpec((ts, 128), cur),
                  pl.BlockSpec((WINDOW, 128), prev), pl.BlockSpec((WINDOW, 128), prev),
                  pl.BlockSpec((8, 128), lambda i: (0, 0))],
        out_specs=[pl.BlockSpec((ts, 512), cur), pl.BlockSpec((ts, 8), cur)],
        out_shape=(jax.ShapeDtypeStruct((S, 512), BF16), jax.ShapeDtypeStruct((S, 8), F32)),
        compiler_params=_cp("parallel"),
    )(pq, pq, cos_t, sin_t, cos_t, sin_t, sinks_b)


def _attn_bwd(pq, cos_t, sin_t, sinks_b, do, o, lse, *, name, ts=256):
    S = pq.shape[0]
    ts = min(ts, S)
    nq = ts // WINDOW
    nt = S // ts
    scale = HEAD_DIM ** -0.5
    NT = (((1,), (1,)), ((), ()))
    TN = (((0,), (0,)), ((), ()))

    def body(cur_ref, prev_ref, nxt_ref, cosc_ref, sinc_ref, cosp_ref, sinp_ref, cosn_ref, sinn_ref, sink_ref,
             doc_ref, don_ref, oc_ref, on_ref, lsec_ref, lsen_ref, dpq_ref, dsink_ref):
        i = pl.program_id(0)
        last = i == nt - 1
        cosc, sinc = cosc_ref[...], sinc_ref[...]
        cose = jnp.concatenate([cosc, cosn_ref[...]], axis=0)
        sine = jnp.concatenate([sinc, sinn_ref[...]], axis=0)
        q = jnp.concatenate([cur_ref[:, 0:512], nxt_ref[:, 0:512]], axis=0).astype(F32)
        qr = jnp.concatenate(
            [_rope_fwd(q[:, j * 128:(j + 1) * 128], cose, sine) for j in range(4)], axis=1) * scale
        qr = qr.astype(BF16)
        kc = _rope_fwd(cur_ref[:, 512:640].astype(F32), cosc, sinc)
        kp = _rope_fwd(prev_ref[:, 0:128].astype(F32), cosp_ref[...], sinp_ref[...])
        k_all = jnp.concatenate([kp, kc], axis=0)
        v_all = jnp.concatenate([prev_ref[:, 128:256].astype(F32), cur_ref[:, 640:768].astype(F32)], axis=0)
        kt = [t.astype(BF16) for t in _tile_heads(k_all)]
        vt = [t.astype(BF16) for t in _tile_heads(v_all)]
        don = jnp.where(last, jnp.zeros_like(don_ref[...]), don_ref[...])
        do_e = jnp.concatenate([doc_ref[...], don], axis=0)
        o_e = jnp.concatenate([oc_ref[...], on_ref[...]], axis=0)
        band2, col2 = _band4(2 * WINDOW)
        band1, _ = _band4(WINDOW)
        ones = jnp.ones((256, 256), BF16)

        @pl.when(i == 0)
        def _():
            dsink_ref[...] = jnp.zeros_like(dsink_ref)

        dk_acc = [[None] * (nq + 1) for _ in range(2)]
        dv_acc = [[None] * (nq + 1) for _ in range(2)]

        def add(acc, g, e, val):
            acc[g][e] = val if acc[g][e] is None else acc[g][e] + val

        for qb in range(nq + 1):
            halo = qb == nq
            rows = slice(qb * WINDOW, (qb + 1) * WINDOW)
            if halo:
                keys = slice(qb * WINDOW, (qb + 1) * WINDOW)
                valid = band1 & jnp.logical_not(last)
            else:
                keys = slice(qb * WINDOW, (qb + 2) * WINDOW)
                valid = band2 & ((col2 >= WINDOW) | (i * nq + qb > 0))
            dq_parts = []
            for g in range(2):
                qs = _stack_heads(qr[rows, g * 256:(g + 1) * 256])
                dos = _stack_heads(do_e[rows, g * 256:(g + 1) * 256])
                o_g = o_e[rows, g * 256:(g + 1) * 256].astype(F32)
                kt_b, vt_b = kt[g][keys], vt[g][keys]
                lse_src = lsen_ref if halo else lsec_ref
                lse_rows = slice(0, WINDOW) if halo else rows
                big_l = jnp.concatenate([lse_src[lse_rows, g * GROUP + hl:g * GROUP + hl + 1] for hl in range(GROUP)],
                                        axis=0)
                delta = jnp.dot((dos.astype(F32) * jnp.concatenate([o_g] * GROUP, axis=0)).astype(BF16), ones,
                                preferred_element_type=F32)
                s = lax.dot_general(qs, kt_b, NT, preferred_element_type=F32)
                p = jnp.exp(jnp.where(valid, s, MASK_VALUE) - big_l)
                dp = lax.dot_general(dos, vt_b, NT, preferred_element_type=F32)
                ds = (p * (dp - delta[:, 0:p.shape[1]])).astype(BF16)
                dk_g = lax.dot_general(ds, qs, TN, preferred_element_type=F32)
                dv_g = lax.dot_general(p.astype(BF16), dos, TN, preferred_element_type=F32)
                if not halo:
                    dq_parts.append(_unstack_heads(jnp.dot(ds, kt_b, preferred_element_type=F32)))
                    dsink4 = jnp.exp(_sink_column(sink_ref, g) - big_l) * delta[:, 0:1]
                    for hl in range(GROUP):
                        h = g * GROUP + hl
                        dsink_h = -jnp.sum(dsink4[hl * WINDOW:(hl + 1) * WINDOW], axis=0, keepdims=True)
                        dsink_ref[h:h + 1, :] += jnp.broadcast_to(dsink_h, (1, 128))
                add(dk_acc, g, qb, dk_g[0:WINDOW])
                add(dv_acc, g, qb, dv_g[0:WINDOW])
                if not halo:
                    add(dk_acc, g, qb + 1, dk_g[WINDOW:2 * WINDOW])
                    add(dv_acc, g, qb + 1, dv_g[WINDOW:2 * WINDOW])
            if not halo:
                cs, sn = cosc[rows], sinc[rows]
                for g in range(2):
                    dq_g = dq_parts[g] * scale
                    for j in range(2):
                        c0 = g * 256 + j * 128
                        dpq_ref[rows, c0:c0 + 128] = _rope_bwd(dq_g[:, j * 128:(j + 1) * 128], cs, sn).astype(BF16)
        for e in range(1, nq + 1):
            rows = slice((e - 1) * WINDOW, e * WINDOW)
            dk = _fold_heads(dk_acc[0][e], dk_acc[1][e])
            dv = _fold_heads(dv_acc[0][e], dv_acc[1][e])
            dpq_ref[rows, 512:640] = _rope_bwd(dk, cosc[rows], sinc[rows]).astype(BF16)
            dpq_ref[rows, 640:768] = dv.astype(BF16)

    hb = ts // WINDOW
    nblk = S // WINDOW
    cur = lambda i: (i, 0)
    prev = lambda i: (jnp.maximum(i * hb - 1, 0), 0)
    nxt = lambda i: (jnp.minimum((i + 1) * hb, nblk - 1), 0)
    return pl.pallas_call(
        body, name=name, grid=(nt,),
        in_specs=[pl.BlockSpec((ts, 768), cur),
                  pl.BlockSpec((WINDOW, 256), lambda i: (jnp.maximum(i * hb - 1, 0), 2)),
                  pl.BlockSpec((WINDOW, 768), nxt),
                  pl.BlockSpec((ts, 128), cur), pl.BlockSpec((ts, 128), cur),
                  pl.BlockSpec((WINDOW, 128), prev), pl.BlockSpec((WINDOW, 128), prev),
                  pl.BlockSpec((WINDOW, 128), nxt), pl.BlockSpec((WINDOW, 128), nxt),
                  pl.BlockSpec((8, 128), lambda i: (0, 0)),
                  pl.BlockSpec((ts, 512), cur), pl.BlockSpec((WINDOW, 512), nxt),
                  pl.BlockSpec((ts, 512), cur), pl.BlockSpec((WINDOW, 512), nxt),
                  pl.BlockSpec((ts, 8), cur), pl.BlockSpec((WINDOW, 8), nxt)],
        out_specs=[pl.BlockSpec((ts, 768), cur), pl.BlockSpec((8, 128), lambda i: (0, 0))],
        out_shape=(jax.ShapeDtypeStruct((S, 768), BF16), jax.ShapeDtypeStruct((8, 128), F32)),
        compiler_params=_cp("arbitrary"),
    )(pq, pq, pq, cos_t, sin_t, cos_t, sin_t, cos_t, sin_t, sinks_b, do, do, o, o, lse, lse)


def _shift_dn(x, k):
    return pltpu.roll(x, k, 0)


def _shift_up(x, k):
    return pltpu.roll(x, x.shape[0] - k, 0)


def _pool_lane_select(vals):
    lane = lax.broadcasted_iota(jnp.int32, (1, 256), 1)
    out = vals[3]
    for g in (2, 1, 0):
        out = jnp.where(lane < 64 * (g + 1), vals[g], out)
    return out


def _pool_inv_count(t0, n):
    t = t0 + lax.broadcasted_iota(jnp.int32, (n, 256), 0)
    lane = lax.broadcasted_iota(jnp.int32, (n, 256), 1)
    w = jnp.where(lane < 64, 2, jnp.where(lane < 128, 4, jnp.where(lane < 192, 8, 16)))
    return 1.0 / jnp.minimum(t + 1, w).astype(F32)


def _pooled(u_ext, t0, n):
    s2 = u_ext + _shift_dn(u_ext, 1)
    s4 = s2 + _shift_dn(s2, 2)
    s8 = s4 + _shift_dn(s4, 4)
    s16 = s8 + _shift_dn(s8, 8)
    win = _pool_lane_select([s2, s4, s8, s16])[HALO:HALO + n]
    return win * _pool_inv_count(t0, n) - u_ext[HALO:HALO + n]


def _poolconv_fwd(pp, wbd, pool_scale, conv_w, *, name, ts=512):
    S = pp.shape[0]
    ts = min(ts, S)

    def body(cur_ref, prev_ref, wbd_ref, sc_ref, cw_ref, oa_ref, oc_ref):
        i = pl.program_id(0)
        prev = jnp.where(i > 0, prev_ref[...].astype(F32), 0.0)
        u_ext = jnp.concatenate([prev[:, 0:256], cur_ref[:, 0:256].astype(F32)], axis=0)
        pooled = _pooled(u_ext, i * ts, ts)
        mixed = jnp.dot(pooled.astype(BF16), wbd_ref[...], preferred_element_type=F32)
        oa_ref[...] = (mixed * sc_ref[...]).astype(BF16)
        v_ext = jnp.concatenate([prev[:, 256:512] * prev[:, 768:1024],
                                 cur_ref[:, 256:512].astype(F32) * cur_ref[:, 768:1024].astype(F32)], axis=0)
        cv = cw_ref[2:3, :] * v_ext + cw_ref[1:2, :] * _shift_dn(v_ext, 1) + cw_ref[0:1, :] * _shift_dn(v_ext, 2)
        oc_ref[...] = (cur_ref[:, 512:768].astype(F32) * cv[HALO:HALO + ts]).astype(BF16)

    hb = ts // HALO
    cur = lambda i: (i, 0)
    const = lambda i: (0, 0)
    return pl.pallas_call(
        body, name=name, grid=(S // ts,),
        in_specs=[pl.BlockSpec((ts, 1024), cur),
                  pl.BlockSpec((HALO, 1024), lambda i: (jnp.maximum(i * hb - 1, 0), 0)),
                  pl.BlockSpec((256, 256), const), pl.BlockSpec((1, 256), const), pl.BlockSpec((3, 256), const)],
        out_specs=[pl.BlockSpec((ts, 256), cur)] * 2,
        out_shape=(jax.ShapeDtypeStruct((S, 256), BF16),) * 2,
        compiler_params=_cp("parallel"),
    )(pp, pp, wbd, pool_scale, conv_w)


def _poolconv_bwd(pp, do_a, do_c, wbd, wbd_t, pool_scale, conv_w, *, name, ts=512):
    S = pp.shape[0]
    ts = min(ts, S)
    nt = S // ts
    n_e = ts + 2 * HALO

    def body(cur_ref, prev_ref, nxt_ref, dac_ref, dan_ref, dcc_ref, dcn_ref, wbd_ref, wbdt_ref, sc_ref, cw_ref,
             dpp_ref, pooled_ref, dmixed_ref, dsc_ref, dcw_ref):
        i = pl.program_id(0)

        @pl.when(i == 0)
        def _():
            dsc_ref[...] = jnp.zeros_like(dsc_ref)
            dcw_ref[...] = jnp.zeros_like(dcw_ref)

        prev = jnp.where(i > 0, prev_ref[...].astype(F32), 0.0)
        nxt = nxt_ref[...].astype(F32)
        cur = cur_ref[...].astype(F32)
        not_last = i < nt - 1
        da_n = jnp.where(not_last, dan_ref[...].astype(F32), 0.0)
        dc_n = jnp.where(not_last, dcn_ref[...].astype(F32), 0.0)
        zeros_h = jnp.zeros((HALO, 256), F32)
        sc = sc_ref[...]

        u_ext = jnp.concatenate([prev[:, 0:256], cur[:, 0:256]], axis=0)
        pooled = _pooled(u_ext, i * ts, ts)
        pooled_b = pooled.astype(BF16)
        pooled_ref[...] = pooled_b
        mixed = jnp.dot(pooled_b, wbd_ref[...], preferred_element_type=F32)
        da_c = dac_ref[...].astype(F32)
        dsc_ref[...] += jnp.sum(da_c * mixed, axis=0, keepdims=True)
        dmixed_e = jnp.concatenate([da_c, da_n], axis=0) * sc
        dmixed_ref[...] = dmixed_e[0:ts].astype(BF16)
        dpooled = jnp.dot(dmixed_e.astype(BF16), wbdt_ref[...], preferred_element_type=F32)
        qd = dpooled * _pool_inv_count(i * ts, ts + HALO)
        f2 = qd + _shift_up(qd, 1)
        f4 = f2 + _shift_up(f2, 2)
        f8 = f4 + _shift_up(f4, 4)
        f16 = f8 + _shift_up(f8, 8)
        du = (_pool_lane_select([f2, f4, f8, f16]) - dpooled)[0:ts]
        dpp_ref[:, 0:256] = du.astype(BF16)

        xc_e = jnp.concatenate([prev[:, 256:512], cur[:, 256:512], nxt[:, 256:512]], axis=0)
        gc_e = jnp.concatenate([prev[:, 768:1024], cur[:, 768:1024], nxt[:, 768:1024]], axis=0)
        gb_e = jnp.concatenate([zeros_h, cur[:, 512:768], nxt[:, 512:768]], axis=0)
        dc_e = jnp.concatenate([zeros_h, dcc_ref[...].astype(F32), dc_n], axis=0)
        v_e = xc_e * gc_e
        v1, v2 = _shift_dn(v_e, 1), _shift_dn(v_e, 2)
        w0, w1, w2 = cw_ref[0:1, :], cw_ref[1:2, :], cw_ref[2:3, :]
        cv = w2 * v_e + w1 * v1 + w0 * v2
        dcv = dc_e * gb_e
        dv = w2 * dcv + w1 * _shift_up(dcv, 1) + w0 * _shift_up(dcv, 2)
        tile = slice(HALO, HALO + ts)
        dpp_ref[:, 256:512] = (dv * gc_e)[tile].astype(BF16)
        dpp_ref[:, 512:768] = (dc_e * cv)[tile].astype(BF16)
        dpp_ref[:, 768:1024] = (dv * xc_e)[tile].astype(BF16)
        dcv_t = dcv[tile]
        dcw_ref[0:1, :] += jnp.sum(dcv_t * v2[tile], axis=0, keepdims=True)
        dcw_ref[1:2, :] += jnp.sum(dcv_t * v1[tile], axis=0, keepdims=True)
        dcw_ref[2:3, :] += jnp.sum(dcv_t * v_e[tile], axis=0, keepdims=True)

    hb = ts // HALO
    nblk = S // HALO
    cur = lambda i: (i, 0)
    const = lambda i: (0, 0)
    prev = lambda i: (jnp.maximum(i * hb - 1, 0), 0)
    nxt = lambda i: (jnp.minimum((i + 1) * hb, nblk - 1), 0)
    del n_e
    return pl.pallas_call(
        body, name=name, grid=(nt,),
        in_specs=[pl.BlockSpec((ts, 1024), cur), pl.BlockSpec((HALO, 1024), prev), pl.BlockSpec((HALO, 1024), nxt),
                  pl.BlockSpec((ts, 256), cur), pl.BlockSpec((HALO, 256), nxt),
                  pl.BlockSpec((ts, 256), cur), pl.BlockSpec((HALO, 256), nxt),
                  pl.BlockSpec((256, 256), const), pl.BlockSpec((256, 256), const),
                  pl.BlockSpec((1, 256), const), pl.BlockSpec((3, 256), const)],
        out_specs=[pl.BlockSpec((ts, 1024), cur), pl.BlockSpec((ts, 256), cur), pl.BlockSpec((ts, 256), cur),
                   pl.BlockSpec((1, 256), const), pl.BlockSpec((3, 256), const)],
        out_shape=(jax.ShapeDtypeStruct((S, 1024), BF16), jax.ShapeDtypeStruct((S, 256), BF16),
                   jax.ShapeDtypeStruct((S, 256), BF16), jax.ShapeDtypeStruct((1, 256), F32),
                   jax.ShapeDtypeStruct((3, 256), F32)),
        compiler_params=_cp("arbitrary"),
    )(pp, pp, pp, do_a, do_a, do_c, do_c, wbd, wbd_t, pool_scale, conv_w)


def _sigmoid(x):
    return 0.5 * jnp.tanh(0.5 * x) + 0.5


def _merge_fwd(o_a, o_b, o_c, glog, w_br, *, name, ts=512):
    S = o_a.shape[0]
    D = w_br.shape[1]
    ts = min(ts, S)

    def body(oa_ref, ob_ref, oc_ref, gl_ref, w_ref, m_ref):
        pa = jnp.dot(oa_ref[...], w_ref[0:256, :], preferred_element_type=F32)
        pb = jnp.dot(ob_ref[...], w_ref[256:768, :], preferred_element_type=F32)
        pc = jnp.dot(oc_ref[...], w_ref[768:1024, :], preferred_element_type=F32)
        m = _sigmoid(gl_ref[:, 0:D].astype(F32)) * pa
        m = m + _sigmoid(gl_ref[:, D:2 * D].astype(F32)) * pb
        m = m + _sigmoid(gl_ref[:, 2 * D:3 * D].astype(F32)) * pc
        m_ref[...] = m.astype(BF16)

    cur = lambda i: (i, 0)
    return pl.pallas_call(
        body, name=name, grid=(S // ts,),
        in_specs=[pl.BlockSpec((ts, 256), cur), pl.BlockSpec((ts, 512), cur), pl.BlockSpec((ts, 256), cur),
                  pl.BlockSpec((ts, 3 * D), cur), pl.BlockSpec((1024, D), lambda i: (0, 0))],
        out_specs=pl.BlockSpec((ts, D), cur),
        out_shape=jax.ShapeDtypeStruct((S, D), BF16),
        compiler_params=_cp("parallel"),
    )(o_a, o_b, o_c, glog, w_br)


def _merge_bwd(dm, o_a, o_b, o_c, glog, w_br, w_br_t, *, name, ts=256):
    S = o_a.shape[0]
    D = w_br.shape[1]
    ts = min(ts, S)

    def body(dm_ref, oa_ref, ob_ref, oc_ref, gl_ref, w_ref, wt_ref, dgl_ref, dp_ref, doa_ref, dob_ref, doc_ref):
        dmv = dm_ref[...].astype(F32)
        branches = ((oa_ref, 0, 256, doa_ref), (ob_ref, 256, 768, dob_ref), (oc_ref, 768, 1024, doc_ref))
        for b, (o_ref, r0, r1, do_ref) in enumerate(branches):
            prod = jnp.dot(o_ref[...], w_ref[r0:r1, :], preferred_element_type=F32)
            gate = _sigmoid(gl_ref[:, b * D:(b + 1) * D].astype(F32))
            dgl_ref[:, b * D:(b + 1) * D] = (dmv * prod * gate * (1.0 - gate)).astype(BF16)
            dprod = (dmv * gate).astype(BF16)
            dp_ref[:, b * D:(b + 1) * D] = dprod
            do_ref[...] = jnp.dot(dprod, wt_ref[:, r0:r1], preferred_element_type=F32).astype(BF16)

    cur = lambda i: (i, 0)
    const = lambda i: (0, 0)
    return pl.pallas_call(
        body, name=name, grid=(S // ts,),
        in_specs=[pl.BlockSpec((ts, D), cur), pl.BlockSpec((ts, 256), cur), pl.BlockSpec((ts, 512), cur),
                  pl.BlockSpec((ts, 256), cur), pl.BlockSpec((ts, 3 * D), cur),
                  pl.BlockSpec((1024, D), const), pl.BlockSpec((D, 1024), const)],
        out_specs=[pl.BlockSpec((ts, 3 * D), cur), pl.BlockSpec((ts, 3 * D), cur), pl.BlockSpec((ts, 256), cur),
                   pl.BlockSpec((ts, 512), cur), pl.BlockSpec((ts, 256), cur)],
        out_shape=(jax.ShapeDtypeStruct((S, 3 * D), BF16), jax.ShapeDtypeStruct((S, 3 * D), BF16),
                   jax.ShapeDtypeStruct((S, 256), BF16), jax.ShapeDtypeStruct((S, 512), BF16),
                   jax.ShapeDtypeStruct((S, 256), BF16)),
        compiler_params=_cp("parallel"),
    )(dm, o_a, o_b, o_c, glog, w_br, w_br_t)


FFN_CHUNK = 128
FFN_DOT_CHUNKS = 4


def _conv3(x, w_ref, cols):
    x1, x2 = _shift_dn(x, 1), _shift_dn(x, 2)
    return w_ref[2:3, cols] * x + w_ref[1:2, cols] * x1 + w_ref[0:1, cols] * x2, x1, x2


def _ffn_down_fwd(up_pre, fcw, w_down3, resid, gamma, beta, *, alpha, name, tc, ts=256, gather=None):
    S, F2 = up_pre.shape
    D = resid.shape[1]
    ts = min(ts, S)
    nt = S // ts
    nj = F2 // (2 * tc)
    has_ag = gather is not None
    n_g = len(gather) if has_ag else 0

    def body(cur_ref, prev_ref, w_ref, wd_ref, r_ref, g_ref, be_ref, *rest):
        h_ref, y_ref, xh_ref, rs_ref, up_ref = rest[n_g:n_g + 5]
        acc_ref = rest[2 * n_g + 5]
        if has_ag:
            ag_refs = (rest[:n_g], rest[n_g + 5:2 * n_g + 5]) + tuple(rest[2 * n_g + 6:2 * n_g + 9])
        i, j = pl.program_id(0), pl.program_id(1)
        if has_ag:
            @pl.when((i == 0) & (j == 0))
            def _():
                _ag_start(*ag_refs)

        part = None
        for c in range(tc // FFN_CHUNK):
            halves = []
            for half in range(2):
                cols = slice(half * tc + c * FFN_CHUNK, half * tc + (c + 1) * FFN_CHUNK)
                prev = jnp.where(i > 0, prev_ref[:, cols].astype(F32), 0.0)
                x = jnp.concatenate([prev, cur_ref[:, cols].astype(F32)], axis=0)
                halves.append(_conv3(x, w_ref, cols)[0][HALO:HALO + ts])
                up_ref[:, cols] = halves[-1].astype(BF16)
            a, b = halves
            h_ref[:, c * FFN_CHUNK:(c + 1) * FFN_CHUNK] = (a * _sigmoid(a) * b).astype(BF16)
            if (c + 1) % FFN_DOT_CHUNKS == 0 or c + 1 == tc // FFN_CHUNK:
                k0 = (c // FFN_DOT_CHUNKS) * FFN_DOT_CHUNKS * FFN_CHUNK
                piece = jnp.dot(h_ref[:, k0:(c + 1) * FFN_CHUNK], wd_ref[j, k0:(c + 1) * FFN_CHUNK, :],
                                preferred_element_type=F32)
                part = piece if part is None else part + piece

        @pl.when(j == 0)
        def _():
            acc_ref[...] = part

        @pl.when(j > 0)
        def _():
            acc_ref[...] += part

        @pl.when(j == nj - 1)
        def _():
            z = alpha * r_ref[...] + acc_ref[...]
            mu = jnp.mean(z, axis=-1, keepdims=True)
            zc = z - mu
            var = jnp.mean(zc * zc, axis=-1, keepdims=True)
            rstd = lax.rsqrt(var + LN_EPS)
            xhat = zc * rstd
            y_ref[...] = xhat * g_ref[...] + be_ref[...]
            xh_ref[...] = xhat.astype(BF16)
            rs_ref[...] = rstd

        if has_ag:
            @pl.when((i == nt - 1) & (j == nj - 1))
            def _():
                _ag_finish(*ag_refs)

    hb = ts // HALO
    row = lambda i, j: (i, 0)
    vec = lambda i, j: (0, 0)
    in_specs = [pl.BlockSpec((ts, 2 * tc), lambda i, j: (i, j)),
                pl.BlockSpec((HALO, 2 * tc), lambda i, j: (jnp.maximum(i * hb - 1, 0), j)),
                pl.BlockSpec((3, 2 * tc), lambda i, j: (0, j)),
                pl.BlockSpec((nj, tc, D), lambda i, j: (0, 0, 0)),
                pl.BlockSpec((ts, D), row), pl.BlockSpec((1, D), vec), pl.BlockSpec((1, D), vec)]
    out_specs = [pl.BlockSpec((ts, tc), lambda i, j: (i, j)), pl.BlockSpec((ts, D), row), pl.BlockSpec((ts, D), row),
                 pl.BlockSpec((ts, 1), row), pl.BlockSpec((ts, 2 * tc), lambda i, j: (i, j))]
    out_shape = [jax.ShapeDtypeStruct((S, F2 // 2), BF16), jax.ShapeDtypeStruct((S, D), F32),
                 jax.ShapeDtypeStruct((S, D), BF16), jax.ShapeDtypeStruct((S, 1), F32),
                 jax.ShapeDtypeStruct((S, F2), BF16)]
    args = [up_pre, up_pre, fcw, w_down3, resid, gamma, beta]
    scratch = [pltpu.VMEM((ts, D), F32)]
    if has_ag:
        in_specs += [pl.BlockSpec(memory_space=pl.ANY)] * n_g
        args += list(gather)
        out_specs += [pl.BlockSpec(memory_space=pl.ANY)] * n_g
        out_shape += [jax.ShapeDtypeStruct((N_DEV,) + g.shape, g.dtype) for g in gather]
        scratch += _ag_sems(n_g)
    res = pl.pallas_call(
        body, name=name, grid=(nt, nj), in_specs=in_specs, out_specs=out_specs, out_shape=out_shape,
        scratch_shapes=scratch, compiler_params=_cp("arbitrary", "arbitrary"),
    )(*args)
    return tuple(res[:5]) + ((list(res[5:]),) if has_ag else ())


def _ffn_up_bwd(up_pre, up, dh, fcw, w_up_t3, dz, *, alpha, name, tc, ts=256, scatter=None):
    S, F2 = up_pre.shape
    D = dz.shape[1]
    ts = min(ts, S)
    nt = S // ts
    nj = F2 // (2 * tc)
    has_rs = scatter is not None
    n_s = len(scatter) if has_rs else 0
    tile = slice(0, ts)

    def body(x_ref, upc_ref, upn_ref, dhc_ref, dhn_ref, w_ref, wt_ref, dz_ref, *rest):
        dpre_ref, dx_ref, dw_ref = rest[n_s:n_s + 3]
        acc_ref = rest[2 * n_s + 3]
        if has_rs:
            rs_refs = (rest[:n_s], rest[n_s + 3:2 * n_s + 3], rest[2 * n_s + 4], rest[2 * n_s + 5])
        i, j = pl.program_id(0), pl.program_id(1)

        @pl.when((i == 0) & (j == 0))
        def _():
            dw_ref[...] = jnp.zeros_like(dw_ref)
            if has_rs:
                _rs_chips_start(*rs_refs)

        part = None
        for c in range(tc // FFN_CHUNK):
            lanes = slice(c * FFN_CHUNK, (c + 1) * FFN_CHUNK)
            dh_n = jnp.where(i < nt - 1, dhn_ref[:, lanes].astype(F32), 0.0)
            dh_e = jnp.concatenate([dhc_ref[:, lanes].astype(F32), dh_n], axis=0)
            cols_of = [slice(half * tc + c * FFN_CHUNK, half * tc + (c + 1) * FFN_CHUNK) for half in range(2)]
            a, b = [jnp.concatenate([upc_ref[:, cols].astype(F32), upn_ref[:, cols].astype(F32)], axis=0)
                    for cols in cols_of]
            sg = _sigmoid(a)
            dups = [dh_e * b * (sg * (1.0 + a * (1.0 - sg))), dh_e * (a * sg)]
            for half in range(2):
                cols, dup = cols_of[half], dups[half]
                dup1, dup2 = _shift_up(dup, 1), _shift_up(dup, 2)
                dpre = w_ref[2:3, cols] * dup + w_ref[1:2, cols] * dup1 + w_ref[0:1, cols] * dup2
                dpre_ref[:, cols] = dpre[tile].astype(BF16)
                x = x_ref[:, cols].astype(F32)
                dw_ref[j, 0:1, cols] += jnp.sum(dup2[tile] * x, axis=0, keepdims=True)
                dw_ref[j, 1:2, cols] += jnp.sum(dup1[tile] * x, axis=0, keepdims=True)
                dw_ref[j, 2:3, cols] += jnp.sum(dup[tile] * x, axis=0, keepdims=True)
            if (c + 1) % FFN_DOT_CHUNKS == 0 or c + 1 == tc // FFN_CHUNK:
                k0 = (c // FFN_DOT_CHUNKS) * FFN_DOT_CHUNKS * FFN_CHUNK
                for half in range(2):
                    ks = slice(half * tc + k0, half * tc + (c + 1) * FFN_CHUNK)
                    piece = jnp.dot(dpre_ref[:, ks], wt_ref[j, ks, :], preferred_element_type=F32)
                    part = piece if part is None else part + piece

        @pl.when(j == 0)
        def _():
            acc_ref[...] = part

        @pl.when(j > 0)
        def _():
            acc_ref[...] += part

        @pl.when(j == nj - 1)
        def _():
            dx_ref[...] = acc_ref[...] + alpha * dz_ref[...]

        if has_rs:
            @pl.when((i == nt - 1) & (j == nj - 1))
            def _():
                _rs_chips_finish(*rs_refs)

    hb = ts // HALO
    nblk = S // HALO
    nxt = lambda i, j: (jnp.minimum((i + 1) * hb, nblk - 1), j)
    row = lambda i, j: (i, 0)
    in_specs = [pl.BlockSpec((ts, 2 * tc), lambda i, j: (i, j)),
                pl.BlockSpec((ts, 2 * tc), lambda i, j: (i, j)), pl.BlockSpec((HALO, 2 * tc), nxt),
                pl.BlockSpec((ts, tc), lambda i, j: (i, j)), pl.BlockSpec((HALO, tc), nxt),
                pl.BlockSpec((3, 2 * tc), lambda i, j: (0, j)),
                pl.BlockSpec((nj, 2 * tc, D), lambda i, j: (0, 0, 0)),
                pl.BlockSpec((ts, D), row)]
    out_specs = [pl.BlockSpec((ts, 2 * tc), lambda i, j: (i, j)), pl.BlockSpec((ts, D), row),
                 pl.BlockSpec((nj, 3, 2 * tc), lambda i, j: (0, 0, 0))]
    out_shape = [jax.ShapeDtypeStruct((S, F2), BF16), jax.ShapeDtypeStruct((S, D), F32),
                 jax.ShapeDtypeStruct((nj, 3, 2 * tc), F32)]
    args = [up_pre, up, up, dh, dh, fcw, w_up_t3, dz]
    scratch = [pltpu.VMEM((ts, D), F32)]
    if has_rs:
        in_specs += [pl.BlockSpec(memory_space=pl.ANY)] * n_s
        args += list(scatter)
        out_specs += [pl.BlockSpec(memory_space=pl.ANY)] * n_s
        out_shape += [jax.ShapeDtypeStruct((3,) + q.shape[1:], q.dtype) for q in scatter]
        scratch += _rs_sems(n_s)
    res = pl.pallas_call(
        body, name=name, grid=(nt, nj), in_specs=in_specs, out_specs=out_specs, out_shape=out_shape,
        scratch_shapes=scratch, compiler_params=_cp("arbitrary", "arbitrary"),
    )(*args)
    return tuple(res[:3]) + ((list(res[3:]),) if has_rs else ())


def _ln_bwd(dy, xhat, rstd, gamma, *, name, ts=512):
    S, D = dy.shape
    ts = min(ts, S)

    def body(dy_ref, xh_ref, rs_ref, g_ref, dz_ref, dg_ref, db_ref):
        _ln_bwd_tile(dy_ref[...], xh_ref, rs_ref, g_ref, dz_ref, dg_ref, db_ref, pl.program_id(0) == 0)

    cur = lambda i: (i, 0)
    const = lambda i: (0, 0)
    return pl.pallas_call(
        body, name=name, grid=(S // ts,),
        in_specs=[pl.BlockSpec((ts, D), cur), pl.BlockSpec((ts, D), cur), pl.BlockSpec((ts, 1), cur),
                  pl.BlockSpec((1, D), const)],
        out_specs=[pl.BlockSpec((ts, D), cur), pl.BlockSpec((1, D), const), pl.BlockSpec((1, D), const)],
        out_shape=(jax.ShapeDtypeStruct((S, D), F32), jax.ShapeDtypeStruct((1, D), F32),
                   jax.ShapeDtypeStruct((1, D), F32)),
        compiler_params=_cp("arbitrary"),
    )(dy, xhat, rstd, gamma)


def _loss_head(y, tgt, *, name, ts=512):
    S, D = y.shape
    ts = min(ts, S)

    def body(y_ref, t_ref, dy_ref, sq_ref):
        @pl.when(pl.program_id(0) == 0)
        def _():
            sq_ref[...] = jnp.zeros_like(sq_ref)

        e = y_ref[...] - t_ref[...]
        dy_ref[...] = e * (1.0 / D)
        sq_ref[...] += jnp.sum(e * e, axis=0, keepdims=True)

    cur = lambda i: (i, 0)
    return pl.pallas_call(
        body, name=name, grid=(S // ts,),
        in_specs=[pl.BlockSpec((ts, D), cur), pl.BlockSpec((ts, D), cur)],
        out_specs=[pl.BlockSpec((ts, D), cur), pl.BlockSpec((1, D), lambda i: (0, 0))],
        out_shape=(jax.ShapeDtypeStruct((S, D), F32), jax.ShapeDtypeStruct((1, D), F32)),
        compiler_params=_cp("arbitrary"),
    )(y, tgt)


def _adamw(w, g, m, v, *, name, tr=512):
    lead = w.shape[:-2]
    R, C = w.shape[-2:]
    tr = _div_tile(R, tr)
    c1 = 1.0 - ADAM_B1 ** ADAM_STEP
    c2 = 1.0 - ADAM_B2 ** ADAM_STEP

    def body(w_ref, g_ref, m_ref, v_ref, d_ref, mo_ref, vo_ref):
        gv = g_ref[...]
        m2 = ADAM_B1 * m_ref[...] + (1.0 - ADAM_B1) * gv
        v2 = ADAM_B2 * v_ref[...] + (1.0 - ADAM_B2) * (gv * gv)
        m_hat = m2 / c1
        v_hat = v2 / c2
        d_ref[...] = -ADAM_LR * (m_hat / (jnp.sqrt(v_hat) + ADAM_EPS) + ADAM_WD * w_ref[...])
        mo_ref[...] = m2
        vo_ref[...] = v2

    if lead:
        spec = pl.BlockSpec((1, tr, C), lambda l, i: (l, i, 0))
        grid = (lead[0], R // tr)
    else:
        spec = pl.BlockSpec((tr, C), lambda i: (i, 0))
        grid = (R // tr,)
    return pl.pallas_call(
        body, name=name, grid=grid,
        in_specs=[spec] * 4, out_specs=[spec] * 3,
        out_shape=(jax.ShapeDtypeStruct(w.shape, F32),) * 3,
        compiler_params=_cp(*(("parallel",) * len(grid))),
    )(w, g, m, v)


def _interleave_cols(w, nj):
    lead, f2 = w.shape[:-1], w.shape[-1]
    tc = f2 // (2 * nj)
    w = w.reshape(lead + (2, nj, tc))
    return jnp.swapaxes(w, -3, -2).reshape(lead + (f2,))


def _deinterleave_cols(w, nj):
    lead, f2 = w.shape[:-1], w.shape[-1]
    tc = f2 // (2 * nj)
    w = w.reshape(lead + (nj, 2, tc))
    return jnp.swapaxes(w, -3, -2).reshape(lead + (f2,))


def _block_diag(w_pool):
    return jnp.concatenate([jnp.pad(w_pool[g], ((0, 0), (64 * g, 192 - 64 * g))) for g in range(4)], axis=0)


def _pad_rows(v, rows):
    return jnp.pad(v, (0, rows * LANES - v.shape[0])).reshape(rows, LANES)


def kernel(x, positions, w_in, w_pool, pool_scale, attn_sinks, conv_w, w_branch_a, w_branch_b, w_branch_c, w_o, ln1_g, ln1_b, w_up, ffn_conv_w, w_down, ln2_g, ln2_b, loss_target, m_w_in, m_w_pool, m_pool_scale, m_attn_sinks, m_conv_w, m_w_branch_a, m_w_branch_b, m_w_branch_c, m_w_o, m_ln1_g, m_ln1_b, m_w_up, m_ffn_conv_w, m_w_down, m_ln2_g, m_ln2_b, v_w_in, v_w_pool, v_pool_scale, v_attn_sinks, v_conv_w, v_w_branch_a, v_w_branch_b, v_w_branch_c, v_w_o, v_ln1_g, v_ln1_b, v_w_up, v_ffn_conv_w, v_w_down, v_ln2_g, v_ln2_b):
    L, D, in_shard = w_in.shape
    S = x.shape[1]
    IN = in_shard * N_DEV
    F2 = w_up.shape[2] * N_DEV
    F = F2 // 2
    assert D == 1024 and IN == 1792 + 3 * D and x.shape[0] == 1 and S % 512 == 0
    alpha = (2 * L) ** 0.25
    NJ = 2
    TC = F // NJ
    xs = x.reshape(S, D)
    tgt = loss_target.reshape(S, D)

    big = [w_in, w_branch_a, w_branch_b, w_branch_c, w_o, w_up, w_down]
    PART_A, PART_B = (0, 1, 2, 3, 4), (5, 6)
    rows_l = [a.size // L // LANES for a in big]
    offs_l = [sum(rows_l[:k]) for k in range(len(big) + 1)]

    def pack_part(l, part):
        return [(big[k][l].T if k == 0 else big[k][l]).astype(BF16) for k in part]

    n_cw, n_fw = conv_w.size, ffn_conv_w.size
    small_rows = -(-(n_cw + n_fw) // LANES)
    small = _pad_rows(jnp.concatenate([conv_w.reshape(-1), ffn_conv_w.reshape(-1)]), small_rows)
    gsmall = _all_gather(small, "ag_conv_weights").reshape(N_DEV, -1)
    conv_full = gsmall[:, :n_cw].reshape(N_DEV, L, 3, -1).transpose(1, 2, 0, 3).reshape(L, 3, 256)
    fcw_full = gsmall[:, n_cw:n_cw + n_fw].reshape(N_DEV, L, 3, -1).transpose(1, 2, 0, 3).reshape(L, 3, F2)
    fcw_full = _interleave_cols(fcw_full, NJ)

    def shard_of(g, part, k, shape):
        assert g[part.index(k)].shape == (N_DEV,) + shape
        return g[part.index(k)]

    def unpack_a(g):
        win_t = shard_of(g, PART_A, 0, (in_shard, D)).reshape(IN, D)
        wg_t = win_t[1792:]
        wp_t = jnp.concatenate([win_t[0:256], win_t[1024:1792]], axis=0)
        wq_t = win_t[256:1024]
        wg, wp, wq = wg_t.T, wp_t.T, wq_t.T
        if g[1] is None:
            return dict(wg=wg, wp=wp, wq=wq)
        wa = shard_of(g, PART_A, 1, (256, D // N_DEV)).transpose(1, 0, 2).reshape(256, D)
        wb = shard_of(g, PART_A, 2, (512, D // N_DEV)).transpose(1, 0, 2).reshape(512, D)
        wc = shard_of(g, PART_A, 3, (256, D // N_DEV)).transpose(1, 0, 2).reshape(256, D)
        wbr = jnp.concatenate([wa, wb, wc], axis=0)
        wo = shard_of(g, PART_A, 4, (D // N_DEV, D)).reshape(D, D)
        return dict(wg=wg, wp=wp, wq=wq, wg_t=wg_t, wp_t=wp_t, wq_t=wq_t, wbr=wbr, wbr_t=wbr.T, wo=wo, wo_t=wo.T)

    def unpack_b(g):
        nh = N_DEV // (2 * NJ)
        wup = shard_of(g, PART_B, 5, (D, F2 // N_DEV)).reshape(2, NJ, nh, D, F2 // N_DEV)
        wup = wup.transpose(3, 1, 0, 2, 4).reshape(D, F2)
        wdn = shard_of(g, PART_B, 6, (F // N_DEV, D)).reshape(F, D)
        return dict(wup=wup, wup_t=wup.T, wdn=wdn, wdn_t=wdn.T)

    def local_weights(l):
        wbd = _block_diag(w_pool[l]).astype(BF16)
        return dict(wbd=wbd, wbd_t=wbd.T, scale=pool_scale[l].reshape(1, 256), conv=conv_full[l],
                    fcw=fcw_full[l], sinks=jnp.broadcast_to(attn_sinks[l].reshape(8, 1), (8, 128)),
                    g1=ln1_g[l].reshape(1, D), b1=ln1_b[l].reshape(1, D),
                    g2=ln2_g[l].reshape(1, D), b2=ln2_b[l].reshape(1, D))

    inv_freq = ROPE_THETA ** (-jnp.arange(0, ROT_DIM, 2, dtype=F32) / ROT_DIM)
    head_lane = jnp.concatenate([inv_freq, inv_freq, jnp.zeros((HEAD_DIM - ROT_DIM,), F32)])
    head_sign = jnp.concatenate([-jnp.ones((8,), F32), jnp.ones((8,), F32), jnp.zeros((HEAD_DIM - ROT_DIM,), F32)])
    inv_lane = jnp.tile(head_lane, 2).reshape(1, 128)
    sign_lane = jnp.tile(head_sign, 2).reshape(1, 128)
    cos_t, sin_t = _rope_tables(positions.reshape(S, 1), inv_lane, sign_lane, "rope_tables")

    saved, W = [], []
    h_in = xs
    gathered_a = [_all_gather(pack_part(0, PART_A[:1]), "ag_weights_first")]
    for l in range(L):
        if l == 0:
            w_in_only = unpack_a(gathered_a + [None] * 4)
            (pg, pp, pq), later = _mm_fan(h_in, [w_in_only["wg"], w_in_only["wp"], w_in_only["wq"]], out_dtype=BF16,
                                          name="proj_in", gather=pack_part(0, PART_A[1:]) + pack_part(0, PART_B))
            gathered_a, gathered_b = gathered_a + later[:4], later[4:]
        w = {**unpack_a(gathered_a), **unpack_b(gathered_b), **local_weights(l)}
        W.append(w)
        if l > 0:
            pg, pp, pq = _mm_fan(h_in, [w["wg"], w["wp"], w["wq"]], out_dtype=BF16, name="proj_in")
        o_a, o_c = _poolconv_fwd(pp, w["wbd"], w["scale"], w["conv"], name="poolconv_fwd")
        o_b, lse = _attn_fwd(pq, cos_t, sin_t, w["sinks"], name="attn_fwd")
        merged = _merge_fwd(o_a, o_b, o_c, pg, w["wbr"], name="merge_fwd")
        x1, xh1, rs1 = _mm_ln(merged, w["wo"], h_in, w["g1"], w["b1"], alpha=alpha, name="wo_ln1")
        if l + 1 < L:
            up_pre, gathered_a = _mm(x1, w["wup"], out_dtype=BF16, name="ffn_up",
                                     gather=pack_part(l + 1, PART_A))
        else:
            up_pre = _mm(x1, w["wup"], out_dtype=BF16, name="ffn_up")
        down = dict(alpha=alpha, name="ffn_down", tc=TC)
        wdn3 = w["wdn"].reshape(NJ, TC, D)
        if l + 1 < L:
            hact, x2, xh2, rs2, up, gathered_b = _ffn_down_fwd(up_pre, w["fcw"], wdn3, x1, w["g2"], w["b2"],
                                                               gather=pack_part(l + 1, PART_B), **down)
        else:
            hact, x2, xh2, rs2, up = _ffn_down_fwd(up_pre, w["fcw"], wdn3, x1, w["g2"], w["b2"], **down)
        saved.append(dict(up=up,x0=h_in, pg=pg, pp=pp, pq=pq, o_a=o_a, o_b=o_b, o_c=o_c, lse=lse, merged=merged,
                          x1=x1, xh1=xh1, rs1=rs1, up_pre=up_pre, hact=hact, xh2=xh2, rs2=rs2))
        h_in = x2

    dy, sq_lanes = _loss_head(h_in, tgt, name="loss_head")

    def pack_up(dw_up_t):
        nh = N_DEV // (2 * NJ)
        t = dw_up_t.reshape(NJ, 2, nh * (F2 // N_DEV), D).transpose(1, 0, 2, 3)
        return t.reshape(N_DEV, F2 // N_DEV, D).astype(BF16)

    def pack_grads(g):
        col = lambda a, n: a.reshape(a.shape[0], N_DEV, n).transpose(1, 0, 2)
        row = lambda a, n: a.reshape(N_DEV, n, a.shape[1])
        rest = [col(g["a"], D // N_DEV), col(g["b"], D // N_DEV), col(g["c"], D // N_DEV),
                row(g["w_o"], D // N_DEV), row(g["w_down"], F // N_DEV)]
        return [row(g["w_in_t"], in_shard).astype(BF16), pack_up(g["w_up_t"]),
                jnp.concatenate([p.reshape(N_DEV, -1, LANES).astype(BF16) for p in rest], axis=1)]

    my_c = lax.axis_index("c").astype(jnp.int32).reshape(1)
    my_chip = (2 * lax.axis_index("x") + lax.axis_index("y")).astype(jnp.int32).reshape(1)
    gw = [None] * L
    pair_sum = [None] * L
    from_chips = [None] * L
    for l in reversed(range(L)):
        w, sv = W[l], saved[l]
        if l == L - 1:
            dz2, dg2, db2 = _ln_bwd(dy, sv["xh2"], sv["rs2"], w["g2"], name="ln2_bwd")
        else:
            dz2, dg2, db2 = ln2_out
        dw_dn = _mm_tn(sv["hact"], dz2, name="down_bwd_w", tka=TC, na=NJ, tn=D, ts=1024, out_dtype=BF16)
        up_bwd = dict(alpha=alpha, name="ffn_up_bwd", tc=TC)
        if l + 1 < L:
            dh, from_sibling = _mm(dz2, w["wdn_t"], out_dtype=BF16, name="down_bwd_x", swap=packed_above)
            pair_sum[l + 1] = [_sum_sibling(p, r, my_c, "rs_sum_sibling") for p, r in zip(packed_above, from_sibling)]
        else:
            dh = _mm(dz2, w["wdn_t"], out_dtype=BF16, name="down_bwd_x")
        wup_t3 = w["wup_t"].reshape(NJ, 2 * TC, D)
        if l + 1 < L:
            dpre, dx1, dfcw, from_chips[l + 1] = _ffn_up_bwd(sv["up_pre"], sv["up"], dh, w["fcw"], wup_t3, dz2,
                                                             scatter=pair_sum[l + 1], **up_bwd)
        else:
            dpre, dx1, dfcw = _ffn_up_bwd(sv["up_pre"], sv["up"], dh, w["fcw"], wup_t3, dz2, **up_bwd)
        dfcw = dfcw.transpose(1, 0, 2).reshape(3, F2)
        dw_up_t = _mm_tn(dpre, sv["x1"], name="up_bwd_w", tka=TC, na=2 * NJ, tn=D, ts=1024,
                         out_dtype=BF16)
        dz1, dg1, db1 = _ln_bwd(dx1, sv["xh1"], sv["rs1"], w["g1"], name="ln1_bwd")
        if l == 0:
            early = [pack_up(dw_up_t)]
            dmerged, sib = _mm(dz1, w["wo_t"], out_dtype=BF16, name="wo_bwd_x", swap=early)
            pair_early = [_sum_sibling(early[0], sib[0], my_c, "rs_sum_sibling")]
        else:
            dmerged = _mm(dz1, w["wo_t"], out_dtype=BF16, name="wo_bwd_x")
        dw_o = _mm_tn(sv["merged"], dz1, name="wo_bwd_w", tka=D, tn=D, out_dtype=BF16)
        dpg, dprod, do_a, do_b, do_c = _merge_bwd(dmerged, sv["o_a"], sv["o_b"], sv["o_c"], sv["pg"],
                                                  w["wbr"], w["wbr_t"], name="merge_bwd")
        dw_a = _mm_tn(sv["o_a"], dprod, name="branch_a_bwd_w", tka=256, tn=D, b_off=0, out_dtype=BF16)
        dw_b = _mm_tn(sv["o_b"], dprod, name="branch_b_bwd_w", tka=512, tn=D, b_off=1, out_dtype=BF16)
        dw_c = _mm_tn(sv["o_c"], dprod, name="branch_c_bwd_w", tka=256, tn=D, b_off=2, out_dtype=BF16)
        dpq, dsink = _attn_bwd(sv["pq"], cos_t, sin_t, w["sinks"], do_b, sv["o_b"], sv["lse"], name="attn_bwd")
        dpp, pooled, dmixed, dscale, dconv = _poolconv_bwd(sv["pp"], do_a, do_c, w["wbd"], w["wbd_t"], w["scale"],
                                                           w["conv"], name="poolconv_bwd")
        dwbd = _mm_tn(pooled, dmixed, name="pool_bwd_w", tka=256, tn=256)
        dx_args = ([dpg, dpp, dpq], [w["wg_t"], w["wp_t"], w["wq_t"]], dz1)
        if l > 0:
            below = saved[l - 1]
            ln2_out = _mm_sum(*dx_args, add_scale=alpha, name="proj_in_bwd_x",
                              ln=(below["xh2"], below["rs2"], W[l - 1]["g2"]))
        else:
            dx, chips_early = _mm_sum(*dx_args, add_scale=alpha, name="proj_in_bwd_x", scatter=pair_early)
        dw_g = _mm_tn(dpg, sv["x0"], name="proj_gate_bwd_w", tka=1024, na=3, tn=D, out_dtype=BF16)
        dw_p = _mm_tn(dpp, sv["x0"], name="proj_poolconv_bwd_w", tka=1024, tn=D, out_dtype=BF16)
        dw_q = _mm_tn(dpq, sv["x0"], name="proj_qkv_bwd_w", tka=768, tn=D, out_dtype=BF16)
        dw_in_t = jnp.concatenate([dw_p[0:256], dw_q, dw_p[256:1024], dw_g], axis=0)
        dw_pool = jnp.stack([dwbd[64 * g:64 * (g + 1), 64 * g:64 * (g + 1)] for g in range(4)])
        gw[l] = dict(w_in_t=dw_in_t, a=dw_a, b=dw_b, c=dw_c, w_o=dw_o, w_up_t=dw_up_t, w_down=dw_dn,
                     w_pool=dw_pool, scale=dscale, sinks=dsink[:, 0], conv=dconv, fcw=_deinterleave_cols(dfcw, NJ),
                     g1=dg1, b1=db1, g2=dg2, b2=db2)
        packed_above = pack_grads(gw[l])
    late = [packed_above[0], packed_above[2]]
    from_sibling = _rs_sibling(late, "rs_sibling_last")
    pair_late = [_sum_sibling(p, r, my_c, "rs_sum_sibling") for p, r in zip(late, from_sibling)]
    chips_late = _rs_chips(pair_late, "rs_chips_last")
    pair_sum[0] = [pair_late[0], pair_early[0], pair_late[1]]
    from_chips[0] = [chips_late[0], chips_early[0], chips_late[1]]
    grad_x = dx.reshape(1, S, D)
    g_layers = [[_sum_chips(q, r, my_chip, "rs_sum_chips") for q, r in zip(pair_sum[l], from_chips[l])]
                for l in range(L)]

    def stack(k):
        return jnp.stack([gw[l][k] for l in range(L)])

    rep_vec = jnp.concatenate([
        stack("w_pool").reshape(-1), stack("scale").reshape(-1), stack("g1").reshape(-1), stack("b1").reshape(-1),
        stack("g2").reshape(-1), stack("b2").reshape(-1)])
    n_rep_full = -(-rep_vec.shape[0] // LANES)
    sinks_row = jnp.pad(stack("sinks").reshape(-1), (0, LANES - 8 * L))
    rep_vec = jnp.concatenate([_pad_rows(rep_vec, n_rep_full).reshape(-1), sinks_row, sq_lanes.reshape(-1)])
    loss_row = n_rep_full + 1
    n_rep = -(-(loss_row + 1) // 8) * 8
    rep_rows = _pad_rows(rep_vec, n_rep)
    dconv_by_dev = stack("conv").reshape(L, 3, N_DEV, -1).transpose(2, 0, 1, 3).reshape(N_DEV, -1)
    dfcw_by_dev = stack("fcw").reshape(L, 3, N_DEV, -1).transpose(2, 0, 1, 3).reshape(N_DEV, -1)
    n_mine = -(-(small_rows) // 8) * 8
    by_dev = jnp.concatenate([dconv_by_dev, dfcw_by_dev], axis=1)
    by_dev = jnp.pad(by_dev, ((0, 0), (0, n_mine * LANES - by_dev.shape[1]))).reshape(N_DEV * n_mine, LANES)
    small_g = _all_gather(jnp.concatenate([rep_rows, by_dev], axis=0), "ag_small_grads")
    rep_sum, mine_sum, loss11 = _small_reduce(small_g, n_rep, n_mine, 1.0 / D, loss_row, "small_reduce")
    loss = loss11[0, 0]

    names_big = ["w_in", "w_branch_a", "w_branch_b", "w_branch_c", "w_o", "w_up", "w_down"]
    ms_big = [m_w_in, m_w_branch_a, m_w_branch_b, m_w_branch_c, m_w_o, m_w_up, m_w_down]
    vs_big = [v_w_in, v_w_branch_a, v_w_branch_b, v_w_branch_c, v_w_o, v_w_up, v_w_down]
    out = {}
    for k, name in enumerate(names_big):
        wk = big[k]
        if k in (0, 5):
            g_t = jnp.stack([g[0 if k == 0 else 1] for g in g_layers])
            tr_ = lambda a: jnp.swapaxes(a, 1, 2)
            d, mo, vo = _adamw(tr_(wk), g_t, tr_(ms_big[k]), tr_(vs_big[k]), name="adamw_" + name)
            out[name] = (tr_(g_t), tr_(d), tr_(mo), tr_(vo))
            continue
        else:
            rest_ks = (1, 2, 3, 4, 6)
            o = sum(rows_l[q] for q in rest_ks[:rest_ks.index(k)])
            g_nat = jnp.concatenate([g[2][o:o + rows_l[k]] for g in g_layers], axis=0).reshape(wk.shape)
        d, mo, vo = _adamw(wk, g_nat, ms_big[k], vs_big[k], name="adamw_" + name)
        out[name] = (g_nat, d, mo, vo)

    def rep_pack(wp_, sc_, g1_, b1_, g2_, b2_, sk_):
        v = jnp.concatenate([wp_.reshape(-1), sc_.reshape(-1), g1_.reshape(-1), b1_.reshape(-1), g2_.reshape(-1),
                             b2_.reshape(-1)])
        return _pad_rows(jnp.concatenate([_pad_rows(v, n_rep_full).reshape(-1), sk_.reshape(-1)]), n_rep)

    def mine_pack(cw_, fw_):
        return _pad_rows(jnp.concatenate([cw_.reshape(-1), fw_.reshape(-1)]), n_mine)

    w_rep = rep_pack(w_pool, pool_scale, ln1_g, ln1_b, ln2_g, ln2_b, attn_sinks)
    m_rep = rep_pack(m_w_pool, m_pool_scale, m_ln1_g, m_ln1_b, m_ln2_g, m_ln2_b, m_attn_sinks)
    v_rep = rep_pack(v_w_pool, v_pool_scale, v_ln1_g, v_ln1_b, v_ln2_g, v_ln2_b, v_attn_sinks)
    g_rep = jnp.concatenate([rep_sum[:loss_row], jnp.zeros((n_rep - loss_row, LANES), F32)], axis=0)
    rep_res = (g_rep,) + tuple(_adamw(w_rep, g_rep, m_rep, v_rep, name="adamw_replicated"))
    w_mine = mine_pack(conv_w, ffn_conv_w)
    mine_res = (mine_sum,) + tuple(_adamw(w_mine, mine_sum, mine_pack(m_conv_w, m_ffn_conv_w),
                                          mine_pack(v_conv_w, v_ffn_conv_w), name="adamw_conv"))

    def rep_unpack(buf):
        flat = buf.reshape(-1)
        res, o = {}, 0
        for nm, ref in (("w_pool", w_pool), ("pool_scale", pool_scale), ("ln1_g", ln1_g), ("ln1_b", ln1_b),
                        ("ln2_g", ln2_g), ("ln2_b", ln2_b)):
            res[nm] = flat[o:o + ref.size].reshape(ref.shape)
            o += ref.size
        o = n_rep_full * LANES
        res["attn_sinks"] = flat[o:o + attn_sinks.size].reshape(attn_sinks.shape)
        return res

    def mine_unpack(buf):
        flat = buf.reshape(-1)
        return {"conv_w": flat[:n_cw].reshape(conv_w.shape),
                "ffn_conv_w": flat[n_cw:n_cw + n_fw].reshape(ffn_conv_w.shape)}

    order = ["w_in", "w_pool", "pool_scale", "attn_sinks", "conv_w", "w_branch_a", "w_branch_b", "w_branch_c", "w_o",
             "ln1_g", "ln1_b", "w_up", "ffn_conv_w", "w_down", "ln2_g", "ln2_b"]
    results = [loss, grad_x]
    for kind in range(4):
        rep_k, mine_k = rep_unpack(rep_res[kind]), mine_unpack(mine_res[kind])
        for nm in order:
            if nm in out:
                results.append(out[nm][kind])
            elif nm in rep_k:
                results.append(rep_k[nm])
            else:
                results.append(mine_k[nm])
    return tuple(results)
```

```python
import jax
import jax.numpy as jnp
from jax import lax
from jax.experimental import pallas as pl
from jax.experimental.pallas import tpu as pltpu

F32 = jnp.float32
BF16 = jnp.bfloat16

HEAD_DIM = 64
GROUP = 4
WINDOW = 128
ROT_DIM = 16
ROPE_THETA = 500000.0
LN_EPS = 1e-5
MASK_VALUE = -1e30
ADAM_LR, ADAM_B1, ADAM_B2, ADAM_EPS, ADAM_WD, ADAM_STEP = 0.001, 0.9, 0.999, 1e-08, 0.01, 10

N_DEV = 8
LANES = 1024
HALO = 16
MESH = pl.DeviceIdType.MESH
VMEM_LIMIT = 56 * 1024 * 1024


def _div_tile(n, want, mult=8):
    for t in range(min(want, n) // mult * mult, 0, -mult):
        if n % t == 0:
            return t
    return n


def _cp(*sem):
    return pltpu.CompilerParams(dimension_semantics=sem, vmem_limit_bytes=VMEM_LIMIT)


def _coords():
    return lax.axis_index("x"), lax.axis_index("y"), lax.axis_index("c")


def _all_gather(xs, name):
    xs = list(xs) if isinstance(xs, (list, tuple)) else [xs]
    n = len(xs)

    def body(*refs):
        ag_refs = (refs[:n], refs[n:2 * n]) + tuple(refs[2 * n:])
        _ag_start(*ag_refs)
        _ag_finish(*ag_refs)

    res = pl.pallas_call(
        body, name=name,
        out_shape=[jax.ShapeDtypeStruct((N_DEV,) + a.shape, a.dtype) for a in xs],
        in_specs=[pl.BlockSpec(memory_space=pl.ANY)] * n,
        out_specs=[pl.BlockSpec(memory_space=pl.ANY)] * n,
        scratch_shapes=_ag_sems(n),
    )(*xs)
    return res if n > 1 else res[0]


def _ag_sems(n):
    return [pltpu.SemaphoreType.DMA((7 * n,)), pltpu.SemaphoreType.DMA((7 * n,)), pltpu.SemaphoreType.DMA((n,))]


def _ag_copies(x_refs, out_refs, send_sems, recv_sems, local_sems):
    x, y, c = _coords()
    me, sibling = (x, y, c), (x, y, 1 - c)
    chips = [(1 - x, y), (x, 1 - y), (1 - x, 1 - y)]
    per_array = []
    for a, (x_ref, out_ref) in enumerate(zip(x_refs, out_refs)):
        def slot(px, py, pc, out_ref=out_ref):
            return out_ref.at[4 * px + 2 * py + pc]

        def copy(k, block, to, src=None, a=a, slot=slot):
            return pltpu.make_async_remote_copy(
                src_ref=slot(*block) if src is None else src, dst_ref=slot(*block),
                send_sem=send_sems.at[7 * a + k], recv_sem=recv_sems.at[7 * a + k],
                device_id=to, device_id_type=MESH)

        mine = pltpu.make_async_copy(x_ref, slot(*me), local_sems.at[a])
        first = [copy(0, me, sibling, src=x_ref)]
        first += [copy(1 + j, me, (*chip, c), src=x_ref) for j, chip in enumerate(chips)]
        passed = [copy(4 + j, (*chip, c), sibling) for j, chip in enumerate(chips)]
        from_chips = [copy(1 + j, (*chip, c), me) for j, chip in enumerate(chips)]
        from_sibling = [copy(0, sibling, me)] + [copy(4 + j, (*chip, 1 - c), me) for j, chip in enumerate(chips)]
        per_array.append((mine, first, passed, from_chips, from_sibling))
    return per_array


def _ag_start(*refs):
    for mine, first, _, _, _ in _ag_copies(*refs):
        mine.start()
        for cp in first:
            cp.start()


def _ag_finish(*refs):
    per_array = _ag_copies(*refs)
    for j in range(3):
        for _, _, passed, from_chips, _ in per_array:
            from_chips[j].wait_recv()
            passed[j].start()
    for mine, first, passed, _, from_sibling in per_array:
        for cp in from_sibling:
            cp.wait_recv()
        for cp in first + passed:
            cp.wait_send()
        mine.wait()


def _rs_sibling(ps, name):
    n = len(ps)

    def body(*refs):
        swap_refs = (refs[:n], refs[n:2 * n], refs[2 * n], refs[2 * n + 1])
        _swap_start(*swap_refs)
        _swap_finish(*swap_refs)

    return pl.pallas_call(
        body, name=name,
        out_shape=[jax.ShapeDtypeStruct((4,) + p.shape[1:], p.dtype) for p in ps],
        in_specs=[pl.BlockSpec(memory_space=pl.ANY)] * n,
        out_specs=[pl.BlockSpec(memory_space=pl.ANY)] * n,
        scratch_shapes=_swap_sems(n),
    )(*ps)


def _swap_sems(n):
    return [pltpu.SemaphoreType.DMA((4 * n,)), pltpu.SemaphoreType.DMA((4 * n,))]


def _swap_copies(p_refs, out_refs, send_sems, recv_sems):
    x, y, c = _coords()
    return [pltpu.make_async_remote_copy(
        src_ref=p_ref.at[4 * (j // 2) + 2 * (j % 2) + (1 - c)], dst_ref=out_ref.at[j],
        send_sem=send_sems.at[4 * a + j], recv_sem=recv_sems.at[4 * a + j],
        device_id=(x, y, 1 - c), device_id_type=MESH)
        for a, (p_ref, out_ref) in enumerate(zip(p_refs, out_refs)) for j in range(4)]


def _swap_start(*refs):
    for cp in _swap_copies(*refs):
        cp.start()


def _swap_finish(*refs):
    copies = _swap_copies(*refs)
    for cp in copies:
        cp.wait_recv()
    for cp in copies:
        cp.wait_send()


def _hosted(kind, arrays):
    n = len(arrays)
    if kind == "gather":
        return _ag_start, _ag_finish, [(N_DEV,) + a.shape for a in arrays], _ag_sems(n)
    assert kind == "swap"
    return _swap_start, _swap_finish, [(4,) + a.shape[1:] for a in arrays], _swap_sems(n)


def _rs_chips(qs, name):
    n = len(qs)

    def body(*refs):
        rs_refs = (refs[:n], refs[n:2 * n], refs[2 * n], refs[2 * n + 1])
        _rs_chips_start(*rs_refs)
        _rs_chips_finish(*rs_refs)

    return pl.pallas_call(
        body, name=name,
        out_shape=[jax.ShapeDtypeStruct((3,) + q.shape[1:], q.dtype) for q in qs],
        in_specs=[pl.BlockSpec(memory_space=pl.ANY)] * n,
        out_specs=[pl.BlockSpec(memory_space=pl.ANY)] * n,
        scratch_shapes=_rs_sems(n),
    )(*qs)


def _rs_sems(n):
    return [pltpu.SemaphoreType.DMA((3 * n,)), pltpu.SemaphoreType.DMA((3 * n,))]


def _rs_chips_copies(q_refs, out_refs, send_sems, recv_sems):
    x, y, c = _coords()
    chips = [(1 - x, y), (x, 1 - y), (1 - x, 1 - y)]
    return [pltpu.make_async_remote_copy(
        src_ref=q_ref.at[2 * cx + cy], dst_ref=out_ref.at[k],
        send_sem=send_sems.at[3 * a + k], recv_sem=recv_sems.at[3 * a + k], device_id=(cx, cy, c),
        device_id_type=MESH)
        for a, (q_ref, out_ref) in enumerate(zip(q_refs, out_refs)) for k, (cx, cy) in enumerate(chips)]


def _rs_chips_start(*refs):
    for cp in _rs_chips_copies(*refs):
        cp.start()


def _rs_chips_finish(*refs):
    copies = _rs_chips_copies(*refs)
    for cp in copies:
        cp.wait_recv()
    for cp in copies:
        cp.wait_send()


def _sum_sibling(p, recv, my_c, name, tr=512):
    _, R, C = p.shape
    tr = _div_tile(R, tr, 16)

    def body(c_ref, p_ref, r_ref, o_ref):
        o_ref[...] = (p_ref[...].astype(F32) + r_ref[...].astype(F32)).astype(o_ref.dtype)

    grid_spec = pltpu.PrefetchScalarGridSpec(
        num_scalar_prefetch=1, grid=(4, R // tr),
        in_specs=[pl.BlockSpec((1, tr, C), lambda j, r, c_ref: (4 * (j // 2) + 2 * (j % 2) + c_ref[0], r, 0)),
                  pl.BlockSpec((1, tr, C), lambda j, r, c_ref: (j, r, 0))],
        out_specs=pl.BlockSpec((1, tr, C), lambda j, r, c_ref: (j, r, 0)))
    return pl.pallas_call(body, name=name, grid_spec=grid_spec,
                          out_shape=jax.ShapeDtypeStruct((4, R, C), p.dtype),
                          compiler_params=_cp("parallel", "parallel"))(my_c, p, recv)


def _sum_chips(q, recv, my_chip, name, tr=512):
    _, R, C = q.shape
    tr = _div_tile(R, tr, 16)

    def body(i_ref, q_ref, r_ref, o_ref):
        acc = q_ref[0].astype(F32)
        for k in range(3):
            acc = acc + r_ref[k].astype(F32)
        o_ref[...] = acc

    grid_spec = pltpu.PrefetchScalarGridSpec(
        num_scalar_prefetch=1, grid=(R // tr,),
        in_specs=[pl.BlockSpec((1, tr, C), lambda r, i_ref: (i_ref[0], r, 0)),
                  pl.BlockSpec((3, tr, C), lambda r, i_ref: (0, r, 0))],
        out_specs=pl.BlockSpec((tr, C), lambda r, i_ref: (r, 0)))
    return pl.pallas_call(body, name=name, grid_spec=grid_spec,
                          out_shape=jax.ShapeDtypeStruct((R, C), F32),
                          compiler_params=_cp("parallel"))(my_chip, q, recv)


def _small_reduce(g, n_rep, n_mine, inv_d, loss_row, name):
    _, R, C = g.shape

    def body(g_ref, rep_ref, mine_ref, loss_ref):
        x, y, c = _coords()
        start = pl.multiple_of(n_rep + (4 * x + 2 * y + c) * n_mine, 8)
        rep = g_ref[0, 0:n_rep, :]
        mine = g_ref[0, pl.ds(start, n_mine), :]
        sq = g_ref[0, loss_row:loss_row + 1, :]
        for d in range(1, N_DEV):
            rep = rep + g_ref[d, 0:n_rep, :]
            mine = mine + g_ref[d, pl.ds(start, n_mine), :]
            sq = sq + g_ref[d, loss_row:loss_row + 1, :]
        rep_ref[...] = rep
        mine_ref[...] = mine
        loss_ref[...] = (0.5 * inv_d) * jnp.sum(sq, axis=1, keepdims=True)

    return pl.pallas_call(
        body, name=name,
        out_shape=(jax.ShapeDtypeStruct((n_rep, C), F32), jax.ShapeDtypeStruct((n_mine, C), F32),
                   jax.ShapeDtypeStruct((1, 1), F32)),
        compiler_params=pltpu.CompilerParams(vmem_limit_bytes=VMEM_LIMIT),
    )(g)


def _mm(a, b, *, out_dtype, name, tm=512, tn=None, tk=None, add=None, add_scale=1.0, gather=None, swap=None):
    M, K = a.shape
    N = b.shape[1]
    tm = min(tm, M)
    tn = N if tn is None else tn
    tk = K if tk is None else tk
    nk = K // tk
    has_add = add is not None
    hosted = gather if gather is not None else swap
    has_ag = hosted is not None
    n_g = len(hosted) if has_ag else 0
    if has_ag:
        comm_start, comm_finish, comm_shapes, comm_sems = _hosted("gather" if gather is not None else "swap", hosted)
    n_i, n_j = M // tm, N // tn

    def body(*refs):
        a_ref, b_ref = refs[0], refs[1]
        add_ref = refs[2] if has_add else None
        n_in = 2 + has_add + n_g
        o_ref = refs[n_in]
        if has_ag:
            ag_refs = (refs[n_in - n_g:n_in], refs[n_in + 1:n_in + 1 + n_g]) + tuple(
                refs[n_in + 1 + n_g:n_in + 1 + n_g + len(comm_sems)])
            pid = (pl.program_id(0), pl.program_id(1), pl.program_id(2))

            @pl.when((pid[0] == 0) & (pid[1] == 0) & (pid[2] == 0))
            def _():
                comm_start(*ag_refs)

        part = jnp.dot(a_ref[...].astype(BF16), b_ref[...].astype(BF16), preferred_element_type=F32)

        def finish(r):
            if has_add:
                r = r + add_scale * add_ref[...].astype(F32)
            o_ref[...] = r.astype(out_dtype)

        if nk == 1:
            finish(part)
        else:
            acc_ref = refs[-1]
            k = pl.program_id(2)

            @pl.when(k == 0)
            def _():
                acc_ref[...] = part

            @pl.when(k > 0)
            def _():
                acc_ref[...] += part

            @pl.when(k == nk - 1)
            def _():
                finish(acc_ref[...])

        if has_ag:
            @pl.when((pid[0] == n_i - 1) & (pid[1] == n_j - 1) & (pid[2] == nk - 1))
            def _():
                comm_finish(*ag_refs)

    b_mode = dict(pipeline_mode=pl.Buffered(1)) if (n_j == 1 and nk == 1) else {}
    in_specs = [pl.BlockSpec((tm, tk), lambda i, j, k: (i, k)),
                pl.BlockSpec((tk, tn), lambda i, j, k: (k, j), **b_mode)]
    args = [a, b]
    if has_add:
        in_specs.append(pl.BlockSpec((tm, tn), lambda i, j, k: (i, j)))
        args.append(add)
    out_specs = [pl.BlockSpec((tm, tn), lambda i, j, k: (i, j))]
    out_shape = [jax.ShapeDtypeStruct((M, N), out_dtype)]
    scratch = []
    if has_ag:
        in_specs += [pl.BlockSpec(memory_space=pl.ANY)] * n_g
        args += list(hosted)
        out_specs += [pl.BlockSpec(memory_space=pl.ANY)] * n_g
        out_shape += [jax.ShapeDtypeStruct(s, g.dtype) for s, g in zip(comm_shapes, hosted)]
        scratch += comm_sems
    if nk > 1:
        scratch.append(pltpu.VMEM((tm, tn), F32))
    sem = ("arbitrary",) * 3 if has_ag else ("parallel", "parallel", "arbitrary")
    res = pl.pallas_call(
        body, name=name, grid=(n_i, n_j, nk), in_specs=in_specs, out_specs=out_specs, out_shape=out_shape,
        scratch_shapes=scratch, compiler_params=_cp(*sem),
    )(*args)
    return (res[0], list(res[1:])) if has_ag else res[0]


def _mm_fan(a, bs, *, out_dtype, name, tm=512, gather=None):
    M, K = a.shape
    tm = min(tm, M)
    n = len(bs)
    n_i = M // tm
    n_g = len(gather) if gather is not None else 0

    def body(*refs):
        outs = refs[1 + n + n_g:1 + 2 * n + n_g]
        if n_g:
            ag_refs = (refs[1 + n:1 + n + n_g], refs[1 + 2 * n + n_g:1 + 2 * n + 2 * n_g]) + tuple(
                refs[1 + 2 * n + 2 * n_g:])

            @pl.when(pl.program_id(0) == 0)
            def _():
                _ag_start(*ag_refs)

        a_v = refs[0][...].astype(BF16)
        for k in range(n):
            outs[k][...] = jnp.dot(a_v, refs[1 + k][...].astype(BF16), preferred_element_type=F32).astype(out_dtype)

        if n_g:
            @pl.when(pl.program_id(0) == n_i - 1)
            def _():
                _ag_finish(*ag_refs)

    row = lambda i: (i, 0)
    hbm = pl.BlockSpec(memory_space=pl.ANY)
    res = pl.pallas_call(
        body, name=name, grid=(n_i,),
        in_specs=[pl.BlockSpec((tm, K), row)] + [pl.BlockSpec(b.shape, lambda i: (0, 0)) for b in bs] + [hbm] * n_g,
        out_specs=[pl.BlockSpec((tm, b.shape[1]), row) for b in bs] + [hbm] * n_g,
        out_shape=([jax.ShapeDtypeStruct((M, b.shape[1]), out_dtype) for b in bs]
                   + [jax.ShapeDtypeStruct((N_DEV,) + g.shape, g.dtype) for g in (gather or [])]),
        scratch_shapes=_ag_sems(n_g) if n_g else [],
        compiler_params=_cp("arbitrary" if n_g else "parallel"),
    )(a, *bs, *(gather or []))
    return (list(res[:n]), list(res[n:])) if n_g else list(res)


def _mm_sum(xs, bs, add, *, add_scale, name, tm=512, ln=None, scatter=None):
    M = xs[0].shape[0]
    N = bs[0].shape[1]
    tm = min(tm, M)
    n = len(xs)
    n_i = M // tm
    n_s = len(scatter) if scatter is not None else 0
    assert not (n_s and ln is not None)

    def body(*refs):
        if n_s:
            rs_refs = (refs[2 * n + 1:2 * n + 1 + n_s], refs[2 * n + 2 + n_s:2 * n + 2 + 2 * n_s],
                       refs[2 * n + 2 + 2 * n_s], refs[2 * n + 3 + 2 * n_s])

            @pl.when(pl.program_id(0) == 0)
            def _():
                _rs_chips_start(*rs_refs)

        acc = add_scale * refs[2 * n][...]
        for k in range(n):
            acc = acc + jnp.dot(refs[k][...].astype(BF16), refs[n + k][...].astype(BF16), preferred_element_type=F32)
        if ln is None:
            refs[2 * n + 1 + n_s][...] = acc
        else:
            xh_ref, rs_ref, g_ref, dz_ref, dg_ref, db_ref = refs[2 * n + 1:]
            _ln_bwd_tile(acc, xh_ref, rs_ref, g_ref, dz_ref, dg_ref, db_ref, pl.program_id(0) == 0)

        if n_s:
            @pl.when(pl.program_id(0) == n_i - 1)
            def _():
                _rs_chips_finish(*rs_refs)

    row = lambda i: (i, 0)
    vec = lambda i: (0, 0)
    hbm = pl.BlockSpec(memory_space=pl.ANY)
    in_specs = ([pl.BlockSpec((tm, x.shape[1]), row) for x in xs]
                + [pl.BlockSpec(b.shape, vec) for b in bs] + [pl.BlockSpec((tm, N), row)])
    if ln is None:
        res = pl.pallas_call(
            body, name=name, grid=(n_i,), in_specs=in_specs + [hbm] * n_s,
            out_specs=[pl.BlockSpec((tm, N), row)] + [hbm] * n_s,
            out_shape=([jax.ShapeDtypeStruct((M, N), F32)]
                       + [jax.ShapeDtypeStruct((3,) + q.shape[1:], q.dtype) for q in (scatter or [])]),
            scratch_shapes=_rs_sems(n_s) if n_s else [],
            compiler_params=_cp("arbitrary" if n_s else "parallel"),
        )(*xs, *bs, add, *(scatter or []))
        return (res[0], list(res[1:])) if n_s else res[0]
    in_specs += [pl.BlockSpec((tm, N), row), pl.BlockSpec((tm, 1), row), pl.BlockSpec((1, N), vec)]
    return pl.pallas_call(
        body, name=name, grid=(M // tm,), in_specs=in_specs,
        out_specs=[pl.BlockSpec((tm, N), row), pl.BlockSpec((1, N), vec), pl.BlockSpec((1, N), vec)],
        out_shape=(jax.ShapeDtypeStruct((M, N), F32), jax.ShapeDtypeStruct((1, N), F32),
                   jax.ShapeDtypeStruct((1, N), F32)),
        compiler_params=_cp("arbitrary"),
    )(*xs, *bs, add, *ln)


def _ln_bwd_tile(dyv, xh_ref, rs_ref, g_ref, dz_ref, dg_ref, db_ref, first):
    @pl.when(first)
    def _():
        dg_ref[...] = jnp.zeros_like(dg_ref)
        db_ref[...] = jnp.zeros_like(db_ref)

    xh = xh_ref[...].astype(F32)
    dyg = dyv * g_ref[...]
    c1 = jnp.mean(dyg, axis=-1, keepdims=True)
    c2 = jnp.mean(dyg * xh, axis=-1, keepdims=True)
    dz_ref[...] = rs_ref[...] * (dyg - c1 - xh * c2)
    dg_ref[...] += jnp.sum(dyv * xh, axis=0, keepdims=True)
    db_ref[...] += jnp.sum(dyv, axis=0, keepdims=True)


def _mm_ln(a, b, resid, gamma, beta, *, alpha, name, tm=512, tk=None):
    M, K = a.shape
    D = b.shape[1]
    tm = min(tm, M)
    tk = K if tk is None else tk
    nk = K // tk

    def body(a_ref, b_ref, r_ref, g_ref, be_ref, y_ref, xh_ref, rs_ref, *scratch):
        part = jnp.dot(a_ref[...].astype(BF16), b_ref[...].astype(BF16), preferred_element_type=F32)

        def finish(acc):
            z = alpha * r_ref[...] + acc
            mu = jnp.mean(z, axis=-1, keepdims=True)
            zc = z - mu
            var = jnp.mean(zc * zc, axis=-1, keepdims=True)
            rstd = lax.rsqrt(var + LN_EPS)
            xhat = zc * rstd
            y_ref[...] = xhat * g_ref[...] + be_ref[...]
            xh_ref[...] = xhat.astype(BF16)
            rs_ref[...] = rstd

        if nk == 1:
            finish(part)
        else:
            acc_ref = scratch[0]
            k = pl.program_id(1)

            @pl.when(k == 0)
            def _():
                acc_ref[...] = part

            @pl.when(k > 0)
            def _():
                acc_ref[...] += part

            @pl.when(k == nk - 1)
            def _():
                finish(acc_ref[...])

    row = lambda i, k: (i, 0)
    vec = lambda i, k: (0, 0)
    return pl.pallas_call(
        body, name=name, grid=(M // tm, nk),
        in_specs=[pl.BlockSpec((tm, tk), lambda i, k: (i, k)), pl.BlockSpec((tk, D), lambda i, k: (k, 0)),
                  pl.BlockSpec((tm, D), row), pl.BlockSpec((1, D), vec), pl.BlockSpec((1, D), vec)],
        out_specs=[pl.BlockSpec((tm, D), row), pl.BlockSpec((tm, D), row), pl.BlockSpec((tm, 1), row)],
        out_shape=(jax.ShapeDtypeStruct((M, D), F32), jax.ShapeDtypeStruct((M, D), BF16),
                   jax.ShapeDtypeStruct((M, 1), F32)),
        scratch_shapes=[pltpu.VMEM((tm, D), F32)] if nk > 1 else [],
        compiler_params=_cp("parallel", "arbitrary"),
    )(a, b, resid, gamma, beta)


def _mm_tn(a, b, *, name, tka, tn, a_off=0, na=1, b_off=0, nb=1, ts=2048, out_dtype=F32):
    S = a.shape[0]
    ts = min(ts, S)
    ns = S // ts
    direct = out_dtype == F32

    def body(a_ref, b_ref, o_ref, *scratch):
        acc_ref = o_ref if direct else scratch[0]
        s = pl.program_id(2)
        part = lax.dot_general(a_ref[...].astype(BF16), b_ref[...].astype(BF16),
                               (((0,), (0,)), ((), ())), preferred_element_type=F32)

        @pl.when(s == 0)
        def _():
            acc_ref[...] = part

        @pl.when(s > 0)
        def _():
            acc_ref[...] += part

        if not direct:
            @pl.when(s == ns - 1)
            def _():
                o_ref[...] = acc_ref[...].astype(out_dtype)

    return pl.pallas_call(
        body, name=name, grid=(na, nb, ns),
        in_specs=[pl.BlockSpec((ts, tka), lambda i, j, s: (s, a_off + i)),
                  pl.BlockSpec((ts, tn), lambda i, j, s: (s, b_off + j))],
        out_specs=pl.BlockSpec((tka, tn), lambda i, j, s: (i, j)),
        out_shape=jax.ShapeDtypeStruct((na * tka, nb * tn), out_dtype),
        scratch_shapes=[] if direct else [pltpu.VMEM((tka, tn), F32)],
        compiler_params=_cp("parallel", "parallel", "arbitrary"),
    )(a, b)


def _rope_tables(pos, inv_lane, sign_lane, name, ts=512):
    S = pos.shape[0]
    ts = min(ts, S)

    def body(p_ref, inv_ref, sg_ref, cos_ref, sin_ref):
        ang = p_ref[...].astype(F32) * inv_ref[...]
        cos_ref[...] = jnp.cos(ang)
        sin_ref[...] = jnp.sin(ang) * sg_ref[...]

    return pl.pallas_call(
        body, name=name, grid=(S // ts,),
        in_specs=[pl.BlockSpec((ts, 1), lambda i: (i, 0)), pl.BlockSpec((1, 128), lambda i: (0, 0)),
                  pl.BlockSpec((1, 128), lambda i: (0, 0))],
        out_specs=[pl.BlockSpec((ts, 128), lambda i: (i, 0))] * 2,
        out_shape=(jax.ShapeDtypeStruct((S, 128), F32),) * 2,
        compiler_params=_cp("parallel"),
    )(pos, inv_lane, sign_lane)


def _rope_swap(t):
    lane = lax.broadcasted_iota(jnp.int32, (1, 128), 1)
    lo = (lane % HEAD_DIM) < (ROT_DIM // 2)
    return jnp.where(lo, pltpu.roll(t, 128 - ROT_DIM // 2, 1), pltpu.roll(t, ROT_DIM // 2, 1))


def _rope_fwd(t, cos, sin):
    return t * cos + _rope_swap(t) * sin


def _rope_bwd(d, cos, sin):
    lane = lax.broadcasted_iota(jnp.int32, (1, 128), 1)
    return d * cos + jnp.where((lane % HEAD_DIM) < ROT_DIM, _rope_swap(d * sin), 0.0)


def _tile_heads(t):
    lane = lax.broadcasted_iota(jnp.int32, (1, 128), 1)
    r = pltpu.roll(t, 64, 1)
    h0 = jnp.where(lane < 64, t, r)
    h1 = jnp.where(lane < 64, r, t)
    return jnp.concatenate([h0, h0], axis=1), jnp.concatenate([h1, h1], axis=1)


def _fold_heads(d0, d1):
    lane = lax.broadcasted_iota(jnp.int32, (1, 128), 1)

    def fold(d):
        s = d[:, 0:128] + d[:, 128:256]
        return s + pltpu.roll(s, 64, 1)

    return jnp.where(lane < 64, fold(d0), fold(d1))


def _band4(n_keys):
    row = lax.broadcasted_iota(jnp.int32, (GROUP * WINDOW, n_keys), 0) % WINDOW
    col = lax.broadcasted_iota(jnp.int32, (GROUP * WINDOW, n_keys), 1)
    return (col > row) & (col <= row + WINDOW), col


def _head_masks():
    lane = lax.broadcasted_iota(jnp.int32, (1, GROUP * HEAD_DIM), 1)
    return [(lane // HEAD_DIM) == hl for hl in range(GROUP)]


def _stack_heads(t):
    zero = jnp.zeros_like(t)
    return jnp.concatenate([jnp.where(hm, t, zero) for hm in _head_masks()], axis=0)


def _unstack_heads(t4):
    out = None
    for hl, hm in enumerate(_head_masks()):
        part = jnp.where(hm, t4[hl * WINDOW:(hl + 1) * WINDOW], 0.0)
        out = part if out is None else out + part
    return out


def _sink_block(sink_ref, g):
    return jnp.concatenate([jnp.broadcast_to(sink_ref[g * GROUP + hl:g * GROUP + hl + 1, 0:1], (WINDOW, 256))
                            for hl in range(GROUP)], axis=0)


def _sink_column(sink_ref, g):
    return jnp.concatenate([jnp.broadcast_to(sink_ref[g * GROUP + hl:g * GROUP + hl + 1, 0:1], (WINDOW, 1))
                            for hl in range(GROUP)], axis=0)


def _attn_fwd(pq, cos_t, sin_t, sinks_b, *, name, ts=256):
    S = pq.shape[0]
    ts = min(ts, S)
    nq = ts // WINDOW
    scale = HEAD_DIM ** -0.5

    def body(cur_ref, prev_ref, cosc_ref, sinc_ref, cosp_ref, sinp_ref, sink_ref, o_ref, lse_ref):
        i = pl.program_id(0)
        cosc, sinc = cosc_ref[...], sinc_ref[...]
        q = cur_ref[:, 0:512].astype(F32)
        qr = jnp.concatenate(
            [_rope_fwd(q[:, j * 128:(j + 1) * 128], cosc, sinc) for j in range(4)], axis=1) * scale
        qr = qr.astype(BF16)
        kc = _rope_fwd(cur_ref[:, 512:640].astype(F32), cosc, sinc)
        kp = _rope_fwd(prev_ref[:, 0:128].astype(F32), cosp_ref[...], sinp_ref[...])
        k_all = jnp.concatenate([kp, kc], axis=0)
        v_all = jnp.concatenate([prev_ref[:, 128:256].astype(F32), cur_ref[:, 640:768].astype(F32)], axis=0)
        kt = [t.astype(BF16) for t in _tile_heads(k_all)]
        vt = [t.astype(BF16) for t in _tile_heads(v_all)]
        band, col = _band4(2 * WINDOW)
        ones = jnp.ones((2 * WINDOW, 256), BF16)
        key_t = lax.broadcasted_iota(jnp.int32, (2 * WINDOW, GROUP * WINDOW), 0)
        qry_t = lax.broadcasted_iota(jnp.int32, (2 * WINDOW, GROUP * WINDOW), 1) % WINDOW
        band_t = (key_t > qry_t) & (key_t <= qry_t + WINDOW)
        NT = (((1,), (1,)), ((), ()))
        for qb in range(nq):
            rows = slice(qb * WINDOW, (qb + 1) * WINDOW)
            keys = slice(qb * WINDOW, (qb + 2) * WINDOW)
            valid = band & ((col >= WINDOW) | (i * nq + qb > 0))
            valid_t = band_t & ((key_t >= WINDOW) | (i * nq + qb > 0))
            for g in range(2):
                qs = _stack_heads(qr[rows, g * 256:(g + 1) * 256])
                sink = _sink_block(sink_ref, g)
                s = lax.dot_general(qs, kt[g][keys], NT, preferred_element_type=F32)
                s_t = lax.dot_general(kt[g][keys], qs, NT, preferred_element_type=F32)
                m_t = jnp.max(jnp.where(valid_t, s_t, MASK_VALUE), axis=0, keepdims=True)
                m_rep = jnp.broadcast_to(m_t, (WINDOW, GROUP * WINDOW)).T
                m = jnp.maximum(jnp.concatenate([m_rep, m_rep], axis=1), sink)
                e = jnp.exp(jnp.where(valid, s, MASK_VALUE) - m).astype(BF16)
                l = jnp.dot(e, ones, preferred_element_type=F32) + jnp.exp(sink - m)
                pv = jnp.dot(e, vt[g][keys], preferred_element_type=F32)
                o_ref[rows, g * 256:(g + 1) * 256] = (_unstack_heads(pv) / _unstack_heads(l)).astype(BF16)
                lse4 = (m + jnp.log(l))[:, 0:1]
                for hl in range(GROUP):
                    h = g * GROUP + hl
                    lse_ref[rows, h:h + 1] = lse4[hl * WINDOW:(hl + 1) * WINDOW]

    hb = ts // WINDOW
    cur = lambda i: (i, 0)
    prev = lambda i: (jnp.maximum(i * hb - 1, 0), 0)
    return pl.pallas_call(
        body, name=name, grid=(S // ts,),
        in_specs=[pl.BlockSpec((ts, 768), cur),
                  pl.BlockSpec((WINDOW, 256), lambda i: (jnp.maximum(i * hb - 1, 0), 2)),
                  pl.BlockSpec((ts, 128), cur), pl.BlockSpec((ts, 128), cur),
                  pl.BlockSpec((WINDOW, 128), prev), pl.BlockSpec((WINDOW, 128), prev),
                  pl.BlockSpec((8, 128), lambda i: (0, 0))],
        out_specs=[pl.BlockSpec((ts, 512), cur), pl.BlockSpec((ts, 8), cur)],
        out_shape=(jax.ShapeDtypeStruct((S, 512), BF16), jax.ShapeDtypeStruct((S, 8), F32)),
        compiler_params=_cp("parallel"),
    )(pq, pq, cos_t, sin_t, cos_t, sin_t, sinks_b)


def _attn_bwd(pq, cos_t, sin_t, sinks_b, do, o, lse, *, name, ts=256):
    S = pq.shape[0]
    ts = min(ts, S)
    nq = ts // WINDOW
    nt = S // ts
    scale = HEAD_DIM ** -0.5
    NT = (((1,), (1,)), ((), ()))
    TN = (((0,), (0,)), ((), ()))

    def body(cur_ref, prev_ref, nxt_ref, cosc_ref, sinc_ref, cosp_ref, sinp_ref, cosn_ref, sinn_ref, sink_ref,
             doc_ref, don_ref, oc_ref, on_ref, lsec_ref, lsen_ref, dpq_ref, dsink_ref):
        i = pl.program_id(0)
        last = i == nt - 1
        cosc, sinc = cosc_ref[...], sinc_ref[...]
        cose = jnp.concatenate([cosc, cosn_ref[...]], axis=0)
        sine = jnp.concatenate([sinc, sinn_ref[...]], axis=0)
        q = jnp.concatenate([cur_ref[:, 0:512], nxt_ref[:, 0:512]], axis=0).astype(F32)
        qr = jnp.concatenate(
            [_rope_fwd(q[:, j * 128:(j + 1) * 128], cose, sine) for j in range(4)], axis=1) * scale
        qr = qr.astype(BF16)
        kc = _rope_fwd(cur_ref[:, 512:640].astype(F32), cosc, sinc)
        kp = _rope_fwd(prev_ref[:, 0:128].astype(F32), cosp_ref[...], sinp_ref[...])
        k_all = jnp.concatenate([kp, kc], axis=0)
        v_all = jnp.concatenate([prev_ref[:, 128:256].astype(F32), cur_ref[:, 640:768].astype(F32)], axis=0)
        kt = [t.astype(BF16) for t in _tile_heads(k_all)]
        vt = [t.astype(BF16) for t in _tile_heads(v_all)]
        don = jnp.where(last, jnp.zeros_like(don_ref[...]), don_ref[...])
        do_e = jnp.concatenate([doc_ref[...], don], axis=0)
        o_e = jnp.concatenate([oc_ref[...], on_ref[...]], axis=0)
        band2, col2 = _band4(2 * WINDOW)
        band1, _ = _band4(WINDOW)
        ones = jnp.ones((256, 256), BF16)

        @pl.when(i == 0)
        def _():
            dsink_ref[...] = jnp.zeros_like(dsink_ref)

        dk_acc = [[None] * (nq + 1) for _ in range(2)]
        dv_acc = [[None] * (nq + 1) for _ in range(2)]

        def add(acc, g, e, val):
            acc[g][e] = val if acc[g][e] is None else acc[g][e] + val

        for qb in range(nq + 1):
            halo = qb == nq
            rows = slice(qb * WINDOW, (qb + 1) * WINDOW)
            if halo:
                keys = slice(qb * WINDOW, (qb + 1) * WINDOW)
                valid = band1 & jnp.logical_not(last)
            else:
                keys = slice(qb * WINDOW, (qb + 2) * WINDOW)
                valid = band2 & ((col2 >= WINDOW) | (i * nq + qb > 0))
            dq_parts = []
            for g in range(2):
                qs = _stack_heads(qr[rows, g * 256:(g + 1) * 256])
                dos = _stack_heads(do_e[rows, g * 256:(g + 1) * 256])
                o_g = o_e[rows, g * 256:(g + 1) * 256].astype(F32)
                kt_b, vt_b = kt[g][keys], vt[g][keys]
                lse_src = lsen_ref if halo else lsec_ref
                lse_rows = slice(0, WINDOW) if halo else rows
                big_l = jnp.concatenate([lse_src[lse_rows, g * GROUP + hl:g * GROUP + hl + 1] for hl in range(GROUP)],
                                        axis=0)
                delta = jnp.dot((dos.astype(F32) * jnp.concatenate([o_g] * GROUP, axis=0)).astype(BF16), ones,
                                preferred_element_type=F32)
                s = lax.dot_general(qs, kt_b, NT, preferred_element_type=F32)
                p = jnp.exp(jnp.where(valid, s, MASK_VALUE) - big_l)
                dp = lax.dot_general(dos, vt_b, NT, preferred_element_type=F32)
                ds = (p * (dp - delta[:, 0:p.shape[1]])).astype(BF16)
                dk_g = lax.dot_general(ds, qs, TN, preferred_element_type=F32)
                dv_g = lax.dot_general(p.astype(BF16), dos, TN, preferred_element_type=F32)
                if not halo:
                    dq_parts.append(_unstack_heads(jnp.dot(ds, kt_b, preferred_element_type=F32)))
                    dsink4 = jnp.exp(_sink_column(sink_ref, g) - big_l) * delta[:, 0:1]
                    for hl in range(GROUP):
                        h = g * GROUP + hl
                        dsink_h = -jnp.sum(dsink4[hl * WINDOW:(hl + 1) * WINDOW], axis=0, keepdims=True)
                        dsink_ref[h:h + 1, :] += jnp.broadcast_to(dsink_h, (1, 128))
                add(dk_acc, g, qb, dk_g[0:WINDOW])
                add(dv_acc, g, qb, dv_g[0:WINDOW])
                if not halo:
                    add(dk_acc, g, qb + 1, dk_g[WINDOW:2 * WINDOW])
                    add(dv_acc, g, qb + 1, dv_g[WINDOW:2 * WINDOW])
            if not halo:
                cs, sn = cosc[rows], sinc[rows]
                for g in range(2):
                    dq_g = dq_parts[g] * scale
                    for j in range(2):
                        c0 = g * 256 + j * 128
                        dpq_ref[rows, c0:c0 + 128] = _rope_bwd(dq_g[:, j * 128:(j + 1) * 128], cs, sn).astype(BF16)
        for e in range(1, nq + 1):
            rows = slice((e - 1) * WINDOW, e * WINDOW)
            dk = _fold_heads(dk_acc[0][e], dk_acc[1][e])
            dv = _fold_heads(dv_acc[0][e], dv_acc[1][e])
            dpq_ref[rows, 512:640] = _rope_bwd(dk, cosc[rows], sinc[rows]).astype(BF16)
            dpq_ref[rows, 640:768] = dv.astype(BF16)

    hb = ts // WINDOW
    nblk = S // WINDOW
    cur = lambda i: (i, 0)
    prev = lambda i: (jnp.maximum(i * hb - 1, 0), 0)
    nxt = lambda i: (jnp.minimum((i + 1) * hb, nblk - 1), 0)
    return pl.pallas_call(
        body, name=name, grid=(nt,),
        in_specs=[pl.BlockSpec((ts, 768), cur),
                  pl.BlockSpec((WINDOW, 256), lambda i: (jnp.maximum(i * hb - 1, 0), 2)),
                  pl.BlockSpec((WINDOW, 768), nxt),
                  pl.BlockSpec((ts, 128), cur), pl.BlockSpec((ts, 128), cur),
                  pl.BlockSpec((WINDOW, 128), prev), pl.BlockSpec((WINDOW, 128), prev),
                  pl.BlockSpec((WINDOW, 128), nxt), pl.BlockSpec((WINDOW, 128), nxt),
                  pl.BlockSpec((8, 128), lambda i: (0, 0)),
                  pl.BlockSpec((ts, 512), cur), pl.BlockSpec((WINDOW, 512), nxt),
                  pl.BlockSpec((ts, 512), cur), pl.BlockSpec((WINDOW, 512), nxt),
                  pl.BlockSpec((ts, 8), cur), pl.BlockSpec((WINDOW, 8), nxt)],
        out_specs=[pl.BlockSpec((ts, 768), cur), pl.BlockSpec((8, 128), lambda i: (0, 0))],
        out_shape=(jax.ShapeDtypeStruct((S, 768), BF16), jax.ShapeDtypeStruct((8, 128), F32)),
        compiler_params=_cp("arbitrary"),
    )(pq, pq, pq, cos_t, sin_t, cos_t, sin_t, cos_t, sin_t, sinks_b, do, do, o, o, lse, lse)


def _shift_dn(x, k):
    return pltpu.roll(x, k, 0)


def _shift_up(x, k):
    return pltpu.roll(x, x.shape[0] - k, 0)


def _pool_lane_select(vals):
    lane = lax.broadcasted_iota(jnp.int32, (1, 256), 1)
    out = vals[3]
    for g in (2, 1, 0):
        out = jnp.where(lane < 64 * (g + 1), vals[g], out)
    return out


def _pool_inv_count(t0, n):
    t = t0 + lax.broadcasted_iota(jnp.int32, (n, 256), 0)
    lane = lax.broadcasted_iota(jnp.int32, (n, 256), 1)
    w = jnp.where(lane < 64, 2, jnp.where(lane < 128, 4, jnp.where(lane < 192, 8, 16)))
    return 1.0 / jnp.minimum(t + 1, w).astype(F32)


def _pooled(u_ext, t0, n):
    s2 = u_ext + _shift_dn(u_ext, 1)
    s4 = s2 + _shift_dn(s2, 2)
    s8 = s4 + _shift_dn(s4, 4)
    s16 = s8 + _shift_dn(s8, 8)
    win = _pool_lane_select([s2, s4, s8, s16])[HALO:HALO + n]
    return win * _pool_inv_count(t0, n) - u_ext[HALO:HALO + n]


def _poolconv_fwd(pp, wbd, pool_scale, conv_w, *, name, ts=512):
    S = pp.shape[0]
    ts = min(ts, S)

    def body(cur_ref, prev_ref, wbd_ref, sc_ref, cw_ref, oa_ref, oc_ref):
        i = pl.program_id(0)
        prev = jnp.where(i > 0, prev_ref[...].astype(F32), 0.0)
        u_ext = jnp.concatenate([prev[:, 0:256], cur_ref[:, 0:256].astype(F32)], axis=0)
        pooled = _pooled(u_ext, i * ts, ts)
        mixed = jnp.dot(pooled.astype(BF16), wbd_ref[...], preferred_element_type=F32)
        oa_ref[...] = (mixed * sc_ref[...]).astype(BF16)
        v_ext = jnp.concatenate([prev[:, 256:512] * prev[:, 768:1024],
                                 cur_ref[:, 256:512].astype(F32) * cur_ref[:, 768:1024].astype(F32)], axis=0)
        cv = cw_ref[2:3, :] * v_ext + cw_ref[1:2, :] * _shift_dn(v_ext, 1) + cw_ref[0:1, :] * _shift_dn(v_ext, 2)
        oc_ref[...] = (cur_ref[:, 512:768].astype(F32) * cv[HALO:HALO + ts]).astype(BF16)

    hb = ts // HALO
    cur = lambda i: (i, 0)
    const = lambda i: (0, 0)
    return pl.pallas_call(
        body, name=name, grid=(S // ts,),
        in_specs=[pl.BlockSpec((ts, 1024), cur),
                  pl.BlockSpec((HALO, 1024), lambda i: (jnp.maximum(i * hb - 1, 0), 0)),
                  pl.BlockSpec((256, 256), const), pl.BlockSpec((1, 256), const), pl.BlockSpec((3, 256), const)],
        out_specs=[pl.BlockSpec((ts, 256), cur)] * 2,
        out_shape=(jax.ShapeDtypeStruct((S, 256), BF16),) * 2,
        compiler_params=_cp("parallel"),
    )(pp, pp, wbd, pool_scale, conv_w)


def _poolconv_bwd(pp, do_a, do_c, wbd, wbd_t, pool_scale, conv_w, *, name, ts=512):
    S = pp.shape[0]
    ts = min(ts, S)
    nt = S // ts
    n_e = ts + 2 * HALO

    def body(cur_ref, prev_ref, nxt_ref, dac_ref, dan_ref, dcc_ref, dcn_ref, wbd_ref, wbdt_ref, sc_ref, cw_ref,
             dpp_ref, pooled_ref, dmixed_ref, dsc_ref, dcw_ref):
        i = pl.program_id(0)

        @pl.when(i == 0)
        def _():
            dsc_ref[...] = jnp.zeros_like(dsc_ref)
            dcw_ref[...] = jnp.zeros_like(dcw_ref)

        prev = jnp.where(i > 0, prev_ref[...].astype(F32), 0.0)
        nxt = nxt_ref[...].astype(F32)
        cur = cur_ref[...].astype(F32)
        not_last = i < nt - 1
        da_n = jnp.where(not_last, dan_ref[...].astype(F32), 0.0)
        dc_n = jnp.where(not_last, dcn_ref[...].astype(F32), 0.0)
        zeros_h = jnp.zeros((HALO, 256), F32)
        sc = sc_ref[...]

        u_ext = jnp.concatenate([prev[:, 0:256], cur[:, 0:256]], axis=0)
        pooled = _pooled(u_ext, i * ts, ts)
        pooled_b = pooled.astype(BF16)
        pooled_ref[...] = pooled_b
        mixed = jnp.dot(pooled_b, wbd_ref[...], preferred_element_type=F32)
        da_c = dac_ref[...].astype(F32)
        dsc_ref[...] += jnp.sum(da_c * mixed, axis=0, keepdims=True)
        dmixed_e = jnp.concatenate([da_c, da_n], axis=0) * sc
        dmixed_ref[...] = dmixed_e[0:ts].astype(BF16)
        dpooled = jnp.dot(dmixed_e.astype(BF16), wbdt_ref[...], preferred_element_type=F32)
        qd = dpooled * _pool_inv_count(i * ts, ts + HALO)
        f2 = qd + _shift_up(qd, 1)
        f4 = f2 + _shift_up(f2, 2)
        f8 = f4 + _shift_up(f4, 4)
        f16 = f8 + _shift_up(f8, 8)
        du = (_pool_lane_select([f2, f4, f8, f16]) - dpooled)[0:ts]
        dpp_ref[:, 0:256] = du.astype(BF16)

        xc_e = jnp.concatenate([prev[:, 256:512], cur[:, 256:512], nxt[:, 256:512]], axis=0)
        gc_e = jnp.concatenate([prev[:, 768:1024], cur[:, 768:1024], nxt[:, 768:1024]], axis=0)
        gb_e = jnp.concatenate([zeros_h, cur[:, 512:768], nxt[:, 512:768]], axis=0)
        dc_e = jnp.concatenate([zeros_h, dcc_ref[...].astype(F32), dc_n], axis=0)
        v_e = xc_e * gc_e
        v1, v2 = _shift_dn(v_e, 1), _shift_dn(v_e, 2)
        w0, w1, w2 = cw_ref[0:1, :], cw_ref[1:2, :], cw_ref[2:3, :]
        cv = w2 * v_e + w1 * v1 + w0 * v2
        dcv = dc_e * gb_e
        dv = w2 * dcv + w1 * _shift_up(dcv, 1) + w0 * _shift_up(dcv, 2)
        tile = slice(HALO, HALO + ts)
        dpp_ref[:, 256:512] = (dv * gc_e)[tile].astype(BF16)
        dpp_ref[:, 512:768] = (dc_e * cv)[tile].astype(BF16)
        dpp_ref[:, 768:1024] = (dv * xc_e)[tile].astype(BF16)
        dcv_t = dcv[tile]
        dcw_ref[0:1, :] += jnp.sum(dcv_t * v2[tile], axis=0, keepdims=True)
        dcw_ref[1:2, :] += jnp.sum(dcv_t * v1[tile], axis=0, keepdims=True)
        dcw_ref[2:3, :] += jnp.sum(dcv_t * v_e[tile], axis=0, keepdims=True)

    hb = ts // HALO
    nblk = S // HALO
    cur = lambda i: (i, 0)
    const = lambda i: (0, 0)
    prev = lambda i: (jnp.maximum(i * hb - 1, 0), 0)
    nxt = lambda i: (jnp.minimum((i + 1) * hb, nblk - 1), 0)
    del n_e
    return pl.pallas_call(
        body, name=name, grid=(nt,),
        in_specs=[pl.BlockSpec((ts, 1024), cur), pl.BlockSpec((HALO, 1024), prev), pl.BlockSpec((HALO, 1024), nxt),
                  pl.BlockSpec((ts, 256), cur), pl.BlockSpec((HALO, 256), nxt),
                  pl.BlockSpec((ts, 256), cur), pl.BlockSpec((HALO, 256), nxt),
                  pl.BlockSpec((256, 256), const), pl.BlockSpec((256, 256), const),
                  pl.BlockSpec((1, 256), const), pl.BlockSpec((3, 256), const)],
        out_specs=[pl.BlockSpec((ts, 1024), cur), pl.BlockSpec((ts, 256), cur), pl.BlockSpec((ts, 256), cur),
                   pl.BlockSpec((1, 256), const), pl.BlockSpec((3, 256), const)],
        out_shape=(jax.ShapeDtypeStruct((S, 1024), BF16), jax.ShapeDtypeStruct((S, 256), BF16),
                   jax.ShapeDtypeStruct((S, 256), BF16), jax.ShapeDtypeStruct((1, 256), F32),
                   jax.ShapeDtypeStruct((3, 256), F32)),
        compiler_params=_cp("arbitrary"),
    )(pp, pp, pp, do_a, do_a, do_c, do_c, wbd, wbd_t, pool_scale, conv_w)


def _sigmoid(x):
    return 0.5 * jnp.tanh(0.5 * x) + 0.5


def _merge_fwd(o_a, o_b, o_c, glog, w_br, *, name, ts=512):
    S = o_a.shape[0]
    D = w_br.shape[1]
    ts = min(ts, S)

    def body(oa_ref, ob_ref, oc_ref, gl_ref, w_ref, m_ref):
        pa = jnp.dot(oa_ref[...], w_ref[0:256, :], preferred_element_type=F32)
        pb = jnp.dot(ob_ref[...], w_ref[256:768, :], preferred_element_type=F32)
        pc = jnp.dot(oc_ref[...], w_ref[768:1024, :], preferred_element_type=F32)
        m = _sigmoid(gl_ref[:, 0:D].astype(F32)) * pa
        m = m + _sigmoid(gl_ref[:, D:2 * D].astype(F32)) * pb
        m = m + _sigmoid(gl_ref[:, 2 * D:3 * D].astype(F32)) * pc
        m_ref[...] = m.astype(BF16)

    cur = lambda i: (i, 0)
    return pl.pallas_call(
        body, name=name, grid=(S // ts,),
        in_specs=[pl.BlockSpec((ts, 256), cur), pl.BlockSpec((ts, 512), cur), pl.BlockSpec((ts, 256), cur),
                  pl.BlockSpec((ts, 3 * D), cur), pl.BlockSpec((1024, D), lambda i: (0, 0))],
        out_specs=pl.BlockSpec((ts, D), cur),
        out_shape=jax.ShapeDtypeStruct((S, D), BF16),
        compiler_params=_cp("parallel"),
    )(o_a, o_b, o_c, glog, w_br)


def _merge_bwd(dm, o_a, o_b, o_c, glog, w_br, w_br_t, *, name, ts=256):
    S = o_a.shape[0]
    D = w_br.shape[1]
    ts = min(ts, S)

    def body(dm_ref, oa_ref, ob_ref, oc_ref, gl_ref, w_ref, wt_ref, dgl_ref, dp_ref, doa_ref, dob_ref, doc_ref):
        dmv = dm_ref[...].astype(F32)
        branches = ((oa_ref, 0, 256, doa_ref), (ob_ref, 256, 768, dob_ref), (oc_ref, 768, 1024, doc_ref))
        for b, (o_ref, r0, r1, do_ref) in enumerate(branches):
            prod = jnp.dot(o_ref[...], w_ref[r0:r1, :], preferred_element_type=F32)
            gate = _sigmoid(gl_ref[:, b * D:(b + 1) * D].astype(F32))
            dgl_ref[:, b * D:(b + 1) * D] = (dmv * prod * gate * (1.0 - gate)).astype(BF16)
            dprod = (dmv * gate).astype(BF16)
            dp_ref[:, b * D:(b + 1) * D] = dprod
            do_ref[...] = jnp.dot(dprod, wt_ref[:, r0:r1], preferred_element_type=F32).astype(BF16)

    cur = lambda i: (i, 0)
    const = lambda i: (0, 0)
    return pl.pallas_call(
        body, name=name, grid=(S // ts,),
        in_specs=[pl.BlockSpec((ts, D), cur), pl.BlockSpec((ts, 256), cur), pl.BlockSpec((ts, 512), cur),
                  pl.BlockSpec((ts, 256), cur), pl.BlockSpec((ts, 3 * D), cur),
                  pl.BlockSpec((1024, D), const), pl.BlockSpec((D, 1024), const)],
        out_specs=[pl.BlockSpec((ts, 3 * D), cur), pl.BlockSpec((ts, 3 * D), cur), pl.BlockSpec((ts, 256), cur),
                   pl.BlockSpec((ts, 512), cur), pl.BlockSpec((ts, 256), cur)],
        out_shape=(jax.ShapeDtypeStruct((S, 3 * D), BF16), jax.ShapeDtypeStruct((S, 3 * D), BF16),
                   jax.ShapeDtypeStruct((S, 256), BF16), jax.ShapeDtypeStruct((S, 512), BF16),
                   jax.ShapeDtypeStruct((S, 256), BF16)),
        compiler_params=_cp("parallel"),
    )(dm, o_a, o_b, o_c, glog, w_br, w_br_t)


FFN_CHUNK = 128
FFN_DOT_CHUNKS = 4


def _conv3(x, w_ref, cols):
    x1, x2 = _shift_dn(x, 1), _shift_dn(x, 2)
    return w_ref[2:3, cols] * x + w_ref[1:2, cols] * x1 + w_ref[0:1, cols] * x2, x1, x2


def _ffn_down_fwd(up_pre, fcw, w_down3, resid, gamma, beta, *, alpha, name, tc, ts=512, gather=None):
    S, F2 = up_pre.shape
    D = resid.shape[1]
    ts = min(ts, S)
    nt = S // ts
    nj = F2 // (2 * tc)
    has_ag = gather is not None
    n_g = len(gather) if has_ag else 0

    def body(cur_ref, prev_ref, w_ref, wd_ref, r_ref, g_ref, be_ref, *rest):
        h_ref, y_ref, xh_ref, rs_ref, up_ref = rest[n_g:n_g + 5]
        acc_ref = rest[2 * n_g + 5]
        if has_ag:
            ag_refs = (rest[:n_g], rest[n_g + 5:2 * n_g + 5]) + tuple(rest[2 * n_g + 6:2 * n_g + 9])
        i, j = pl.program_id(0), pl.program_id(1)
        if has_ag:
            @pl.when((i == 0) & (j == 0))
            def _():
                _ag_start(*ag_refs)

        part = None
        for c in range(tc // FFN_CHUNK):
            halves = []
            for half in range(2):
                cols = slice(half * tc + c * FFN_CHUNK, half * tc + (c + 1) * FFN_CHUNK)
                prev = jnp.where(i > 0, prev_ref[:, cols].astype(F32), 0.0)
                x = jnp.concatenate([prev, cur_ref[:, cols].astype(F32)], axis=0)
                halves.append(_conv3(x, w_ref, cols)[0][HALO:HALO + ts])
                up_ref[:, cols] = halves[-1].astype(BF16)
            a, b = halves
            h_ref[:, c * FFN_CHUNK:(c + 1) * FFN_CHUNK] = (a * _sigmoid(a) * b).astype(BF16)
            if (c + 1) % FFN_DOT_CHUNKS == 0 or c + 1 == tc // FFN_CHUNK:
                k0 = (c // FFN_DOT_CHUNKS) * FFN_DOT_CHUNKS * FFN_CHUNK
                piece = jnp.dot(h_ref[:, k0:(c + 1) * FFN_CHUNK], wd_ref[j, k0:(c + 1) * FFN_CHUNK, :],
                                preferred_element_type=F32)
                part = piece if part is None else part + piece

        @pl.when(j == 0)
        def _():
            acc_ref[...] = part

        @pl.when(j > 0)
        def _():
            acc_ref[...] += part

        @pl.when(j == nj - 1)
        def _():
            z = alpha * r_ref[...] + acc_ref[...]
            mu = jnp.mean(z, axis=-1, keepdims=True)
            zc = z - mu
            var = jnp.mean(zc * zc, axis=-1, keepdims=True)
            rstd = lax.rsqrt(var + LN_EPS)
            xhat = zc * rstd
            y_ref[...] = xhat * g_ref[...] + be_ref[...]
            xh_ref[...] = xhat.astype(BF16)
            rs_ref[...] = rstd

        if has_ag:
            @pl.when((i == nt - 1) & (j == nj - 1))
            def _():
                _ag_finish(*ag_refs)

    hb = ts // HALO
    row = lambda i, j: (i, 0)
    vec = lambda i, j: (0, 0)
    in_specs = [pl.BlockSpec((ts, 2 * tc), lambda i, j: (i, j)),
                pl.BlockSpec((HALO, 2 * tc), lambda i, j: (jnp.maximum(i * hb - 1, 0), j)),
                pl.BlockSpec((3, 2 * tc), lambda i, j: (0, j)),
                pl.BlockSpec((nj, tc, D), lambda i, j: (0, 0, 0), pipeline_mode=pl.Buffered(1)),
                pl.BlockSpec((ts, D), row), pl.BlockSpec((1, D), vec), pl.BlockSpec((1, D), vec)]
    out_specs = [pl.BlockSpec((ts, tc), lambda i, j: (i, j)), pl.BlockSpec((ts, D), row), pl.BlockSpec((ts, D), row),
                 pl.BlockSpec((ts, 1), row), pl.BlockSpec((ts, 2 * tc), lambda i, j: (i, j))]
    out_shape = [jax.ShapeDtypeStruct((S, F2 // 2), BF16), jax.ShapeDtypeStruct((S, D), F32),
                 jax.ShapeDtypeStruct((S, D), BF16), jax.ShapeDtypeStruct((S, 1), F32),
                 jax.ShapeDtypeStruct((S, F2), BF16)]
    args = [up_pre, up_pre, fcw, w_down3, resid, gamma, beta]
    scratch = [pltpu.VMEM((ts, D), F32)]
    if has_ag:
        in_specs += [pl.BlockSpec(memory_space=pl.ANY)] * n_g
        args += list(gather)
        out_specs += [pl.BlockSpec(memory_space=pl.ANY)] * n_g
        out_shape += [jax.ShapeDtypeStruct((N_DEV,) + g.shape, g.dtype) for g in gather]
        scratch += _ag_sems(n_g)
    res = pl.pallas_call(
        body, name=name, grid=(nt, nj), in_specs=in_specs, out_specs=out_specs, out_shape=out_shape,
        scratch_shapes=scratch, compiler_params=_cp("arbitrary", "arbitrary"),
    )(*args)
    return tuple(res[:5]) + ((list(res[5:]),) if has_ag else ())


def _ffn_up_bwd(up_pre, up, dh, fcw, w_up_t3, dz, *, alpha, name, tc, ts=256, scatter=None):
    S, F2 = up_pre.shape
    D = dz.shape[1]
    ts = min(ts, S)
    nt = S // ts
    nj = F2 // (2 * tc)
    has_rs = scatter is not None
    n_s = len(scatter) if has_rs else 0
    tile = slice(0, ts)

    def body(x_ref, upc_ref, upn_ref, dhc_ref, dhn_ref, w_ref, wt_ref, dz_ref, *rest):
        dpre_ref, dx_ref, dw_ref = rest[n_s:n_s + 3]
        acc_ref = rest[2 * n_s + 3]
        if has_rs:
            rs_refs = (rest[:n_s], rest[n_s + 3:2 * n_s + 3], rest[2 * n_s + 4], rest[2 * n_s + 5])
        i, j = pl.program_id(0), pl.program_id(1)

        @pl.when((i == 0) & (j == 0))
        def _():
            dw_ref[...] = jnp.zeros_like(dw_ref)
            if has_rs:
                _rs_chips_start(*rs_refs)

        part = None
        for c in range(tc // FFN_CHUNK):
            lanes = slice(c * FFN_CHUNK, (c + 1) * FFN_CHUNK)
            dh_n = jnp.where(i < nt - 1, dhn_ref[:, lanes].astype(F32), 0.0)
            dh_e = jnp.concatenate([dhc_ref[:, lanes].astype(F32), dh_n], axis=0)
            cols_of = [slice(half * tc + c * FFN_CHUNK, half * tc + (c + 1) * FFN_CHUNK) for half in range(2)]
            a, b = [jnp.concatenate([upc_ref[:, cols].astype(F32), upn_ref[:, cols].astype(F32)], axis=0)
                    for cols in cols_of]
            sg = _sigmoid(a)
            dups = [dh_e * b * (sg * (1.0 + a * (1.0 - sg))), dh_e * (a * sg)]
            for half in range(2):
                cols, dup = cols_of[half], dups[half]
                dup1, dup2 = _shift_up(dup, 1), _shift_up(dup, 2)
                dpre = w_ref[2:3, cols] * dup + w_ref[1:2, cols] * dup1 + w_ref[0:1, cols] * dup2
                dpre_ref[:, cols] = dpre[tile].astype(BF16)
                x = x_ref[:, cols].astype(F32)
                dw_ref[j, 0:1, cols] += jnp.sum(dup2[tile] * x, axis=0, keepdims=True)
                dw_ref[j, 1:2, cols] += jnp.sum(dup1[tile] * x, axis=0, keepdims=True)
                dw_ref[j, 2:3, cols] += jnp.sum(dup[tile] * x, axis=0, keepdims=True)
            if (c + 1) % FFN_DOT_CHUNKS == 0 or c + 1 == tc // FFN_CHUNK:
                k0 = (c // FFN_DOT_CHUNKS) * FFN_DOT_CHUNKS * FFN_CHUNK
                for half in range(2):
                    ks = slice(half * tc + k0, half * tc + (c + 1) * FFN_CHUNK)
                    piece = jnp.dot(dpre_ref[:, ks], wt_ref[j, ks, :], preferred_element_type=F32)
                    part = piece if part is None else part + piece

        @pl.when(j == 0)
        def _():
            acc_ref[...] = part

        @pl.when(j > 0)
        def _():
            acc_ref[...] += part

        @pl.when(j == nj - 1)
        def _():
            dx_ref[...] = acc_ref[...] + alpha * dz_ref[...]

        if has_rs:
            @pl.when((i == nt - 1) & (j == nj - 1))
            def _():
                _rs_chips_finish(*rs_refs)

    hb = ts // HALO
    nblk = S // HALO
    nxt = lambda i, j: (jnp.minimum((i + 1) * hb, nblk - 1), j)
    row = lambda i, j: (i, 0)
    in_specs = [pl.BlockSpec((ts, 2 * tc), lambda i, j: (i, j)),
                pl.BlockSpec((ts, 2 * tc), lambda i, j: (i, j)), pl.BlockSpec((HALO, 2 * tc), nxt),
                pl.BlockSpec((ts, tc), lambda i, j: (i, j)), pl.BlockSpec((HALO, tc), nxt),
                pl.BlockSpec((3, 2 * tc), lambda i, j: (0, j)),
                pl.BlockSpec((nj, 2 * tc, D), lambda i, j: (0, 0, 0), pipeline_mode=pl.Buffered(1)),
                pl.BlockSpec((ts, D), row)]
    out_specs = [pl.BlockSpec((ts, 2 * tc), lambda i, j: (i, j)), pl.BlockSpec((ts, D), row),
                 pl.BlockSpec((nj, 3, 2 * tc), lambda i, j: (0, 0, 0))]
    out_shape = [jax.ShapeDtypeStruct((S, F2), BF16), jax.ShapeDtypeStruct((S, D), F32),
                 jax.ShapeDtypeStruct((nj, 3, 2 * tc), F32)]
    args = [up_pre, up, up, dh, dh, fcw, w_up_t3, dz]
    scratch = [pltpu.VMEM((ts, D), F32)]
    if has_rs:
        in_specs += [pl.BlockSpec(memory_space=pl.ANY)] * n_s
        args += list(scatter)
        out_specs += [pl.BlockSpec(memory_space=pl.ANY)] * n_s
        out_shape += [jax.ShapeDtypeStruct((3,) + q.shape[1:], q.dtype) for q in scatter]
        scratch += _rs_sems(n_s)
    res = pl.pallas_call(
        body, name=name, grid=(nt, nj), in_specs=in_specs, out_specs=out_specs, out_shape=out_shape,
        scratch_shapes=scratch, compiler_params=_cp("arbitrary", "arbitrary"),
    )(*args)
    return tuple(res[:3]) + ((list(res[3:]),) if has_rs else ())


def _ln_bwd(dy, xhat, rstd, gamma, *, name, ts=512):
    S, D = dy.shape
    ts = min(ts, S)

    def body(dy_ref, xh_ref, rs_ref, g_ref, dz_ref, dg_ref, db_ref):
        _ln_bwd_tile(dy_ref[...], xh_ref, rs_ref, g_ref, dz_ref, dg_ref, db_ref, pl.program_id(0) == 0)

    cur = lambda i: (i, 0)
    const = lambda i: (0, 0)
    return pl.pallas_call(
        body, name=name, grid=(S // ts,),
        in_specs=[pl.BlockSpec((ts, D), cur), pl.BlockSpec((ts, D), cur), pl.BlockSpec((ts, 1), cur),
                  pl.BlockSpec((1, D), const)],
        out_specs=[pl.BlockSpec((ts, D), cur), pl.BlockSpec((1, D), const), pl.BlockSpec((1, D), const)],
        out_shape=(jax.ShapeDtypeStruct((S, D), F32), jax.ShapeDtypeStruct((1, D), F32),
                   jax.ShapeDtypeStruct((1, D), F32)),
        compiler_params=_cp("arbitrary"),
    )(dy, xhat, rstd, gamma)


def _loss_head(y, tgt, *, name, ts=512):
    S, D = y.shape
    ts = min(ts, S)

    def body(y_ref, t_ref, dy_ref, sq_ref):
        @pl.when(pl.program_id(0) == 0)
        def _():
            sq_ref[...] = jnp.zeros_like(sq_ref)

        e = y_ref[...] - t_ref[...]
        dy_ref[...] = e * (1.0 / D)
        sq_ref[...] += jnp.sum(e * e, axis=0, keepdims=True)

    cur = lambda i: (i, 0)
    return pl.pallas_call(
        body, name=name, grid=(S // ts,),
        in_specs=[pl.BlockSpec((ts, D), cur), pl.BlockSpec((ts, D), cur)],
        out_specs=[pl.BlockSpec((ts, D), cur), pl.BlockSpec((1, D), lambda i: (0, 0))],
        out_shape=(jax.ShapeDtypeStruct((S, D), F32), jax.ShapeDtypeStruct((1, D), F32)),
        compiler_params=_cp("arbitrary"),
    )(y, tgt)


def _adamw(w, g, m, v, *, name, tr=512):
    lead = w.shape[:-2]
    R, C = w.shape[-2:]
    tr = _div_tile(R, tr)
    c1 = 1.0 - ADAM_B1 ** ADAM_STEP
    c2 = 1.0 - ADAM_B2 ** ADAM_STEP

    def body(w_ref, g_ref, m_ref, v_ref, d_ref, mo_ref, vo_ref):
        gv = g_ref[...]
        m2 = ADAM_B1 * m_ref[...] + (1.0 - ADAM_B1) * gv
        v2 = ADAM_B2 * v_ref[...] + (1.0 - ADAM_B2) * (gv * gv)
        m_hat = m2 / c1
        v_hat = v2 / c2
        d_ref[...] = -ADAM_LR * (m_hat / (jnp.sqrt(v_hat) + ADAM_EPS) + ADAM_WD * w_ref[...])
        mo_ref[...] = m2
        vo_ref[...] = v2

    if lead:
        spec = pl.BlockSpec((1, tr, C), lambda l, i: (l, i, 0))
        grid = (lead[0], R // tr)
    else:
        spec = pl.BlockSpec((tr, C), lambda i: (i, 0))
        grid = (R // tr,)
    return pl.pallas_call(
        body, name=name, grid=grid,
        in_specs=[spec] * 4, out_specs=[spec] * 3,
        out_shape=(jax.ShapeDtypeStruct(w.shape, F32),) * 3,
        compiler_params=_cp(*(("parallel",) * len(grid))),
    )(w, g, m, v)


def _interleave_cols(w, nj):
    lead, f2 = w.shape[:-1], w.shape[-1]
    tc = f2 // (2 * nj)
    w = w.reshape(lead + (2, nj, tc))
    return jnp.swapaxes(w, -3, -2).reshape(lead + (f2,))


def _deinterleave_cols(w, nj):
    lead, f2 = w.shape[:-1], w.shape[-1]
    tc = f2 // (2 * nj)
    w = w.reshape(lead + (nj, 2, tc))
    return jnp.swapaxes(w, -3, -2).reshape(lead + (f2,))


def _block_diag(w_pool):
    return jnp.concatenate([jnp.pad(w_pool[g], ((0, 0), (64 * g, 192 - 64 * g))) for g in range(4)], axis=0)


def _pad_rows(v, rows):
    return jnp.pad(v, (0, rows * LANES - v.shape[0])).reshape(rows, LANES)


def kernel(x, positions, w_in, w_pool, pool_scale, attn_sinks, conv_w, w_branch_a, w_branch_b, w_branch_c, w_o, ln1_g, ln1_b, w_up, ffn_conv_w, w_down, ln2_g, ln2_b, loss_target, m_w_in, m_w_pool, m_pool_scale, m_attn_sinks, m_conv_w, m_w_branch_a, m_w_branch_b, m_w_branch_c, m_w_o, m_ln1_g, m_ln1_b, m_w_up, m_ffn_conv_w, m_w_down, m_ln2_g, m_ln2_b, v_w_in, v_w_pool, v_pool_scale, v_attn_sinks, v_conv_w, v_w_branch_a, v_w_branch_b, v_w_branch_c, v_w_o, v_ln1_g, v_ln1_b, v_w_up, v_ffn_conv_w, v_w_down, v_ln2_g, v_ln2_b):
    L, D, in_shard = w_in.shape
    S = x.shape[1]
    IN = in_shard * N_DEV
    F2 = w_up.shape[2] * N_DEV
    F = F2 // 2
    assert D == 1024 and IN == 1792 + 3 * D and x.shape[0] == 1 and S % 512 == 0
    alpha = (2 * L) ** 0.25
    NJ = 2
    TC = F // NJ
    xs = x.reshape(S, D)
    tgt = loss_target.reshape(S, D)

    big = [w_in, w_branch_a, w_branch_b, w_branch_c, w_o, w_up, w_down]
    PART_A, PART_B = (0, 1, 2, 3, 4), (5, 6)
    rows_l = [a.size // L // LANES for a in big]
    offs_l = [sum(rows_l[:k]) for k in range(len(big) + 1)]

    def pack_part(l, part):
        return [(big[k][l].T if k == 0 else big[k][l]).astype(BF16) for k in part]

    n_cw, n_fw = conv_w.size, ffn_conv_w.size
    small_rows = -(-(n_cw + n_fw) // LANES)
    small = _pad_rows(jnp.concatenate([conv_w.reshape(-1), ffn_conv_w.reshape(-1)]), small_rows)
    gsmall = _all_gather(small, "ag_conv_weights").reshape(N_DEV, -1)
    conv_full = gsmall[:, :n_cw].reshape(N_DEV, L, 3, -1).transpose(1, 2, 0, 3).reshape(L, 3, 256)
    fcw_full = gsmall[:, n_cw:n_cw + n_fw].reshape(N_DEV, L, 3, -1).transpose(1, 2, 0, 3).reshape(L, 3, F2)
    fcw_full = _interleave_cols(fcw_full, NJ)

    def shard_of(g, part, k, shape):
        assert g[part.index(k)].shape == (N_DEV,) + shape
        return g[part.index(k)]

    def unpack_a(g):
        win_t = shard_of(g, PART_A, 0, (in_shard, D)).reshape(IN, D)
        wg_t = win_t[1792:]
        wp_t = jnp.concatenate([win_t[0:256], win_t[1024:1792]], axis=0)
        wq_t = win_t[256:1024]
        wg, wp, wq = wg_t.T, wp_t.T, wq_t.T
        if g[1] is None:
            return dict(wg=wg, wp=wp, wq=wq)
        wa = shard_of(g, PART_A, 1, (256, D // N_DEV)).transpose(1, 0, 2).reshape(256, D)
        wb = shard_of(g, PART_A, 2, (512, D // N_DEV)).transpose(1, 0, 2).reshape(512, D)
        wc = shard_of(g, PART_A, 3, (256, D // N_DEV)).transpose(1, 0, 2).reshape(256, D)
        wbr = jnp.concatenate([wa, wb, wc], axis=0)
        wo = shard_of(g, PART_A, 4, (D // N_DEV, D)).reshape(D, D)
        return dict(wg=wg, wp=wp, wq=wq, wg_t=wg_t, wp_t=wp_t, wq_t=wq_t, wbr=wbr, wbr_t=wbr.T, wo=wo, wo_t=wo.T)

    def unpack_b(g):
        nh = N_DEV // (2 * NJ)
        wup = shard_of(g, PART_B, 5, (D, F2 // N_DEV)).reshape(2, NJ, nh, D, F2 // N_DEV)
        wup = wup.transpose(3, 1, 0, 2, 4).reshape(D, F2)
        wdn = shard_of(g, PART_B, 6, (F // N_DEV, D)).reshape(F, D)
        return dict(wup=wup, wup_t=wup.T, wdn=wdn, wdn_t=wdn.T)

    def local_weights(l):
        wbd = _block_diag(w_pool[l]).astype(BF16)
        return dict(wbd=wbd, wbd_t=wbd.T, scale=pool_scale[l].reshape(1, 256), conv=conv_full[l],
                    fcw=fcw_full[l], sinks=jnp.broadcast_to(attn_sinks[l].reshape(8, 1), (8, 128)),
                    g1=ln1_g[l].reshape(1, D), b1=ln1_b[l].reshape(1, D),
                    g2=ln2_g[l].reshape(1, D), b2=ln2_b[l].reshape(1, D))

    inv_freq = ROPE_THETA ** (-jnp.arange(0, ROT_DIM, 2, dtype=F32) / ROT_DIM)
    head_lane = jnp.concatenate([inv_freq, inv_freq, jnp.zeros((HEAD_DIM - ROT_DIM,), F32)])
    head_sign = jnp.concatenate([-jnp.ones((8,), F32), jnp.ones((8,), F32), jnp.zeros((HEAD_DIM - ROT_DIM,), F32)])
    inv_lane = jnp.tile(head_lane, 2).reshape(1, 128)
    sign_lane = jnp.tile(head_sign, 2).reshape(1, 128)
    cos_t, sin_t = _rope_tables(positions.reshape(S, 1), inv_lane, sign_lane, "rope_tables")

    saved, W = [], []
    h_in = xs
    gathered_a = [_all_gather(pack_part(0, PART_A[:1]), "ag_weights_first")]
    for l in range(L):
        if l == 0:
            w_in_only = unpack_a(gathered_a + [None] * 4)
            (pg, pp, pq), later = _mm_fan(h_in, [w_in_only["wg"], w_in_only["wp"], w_in_only["wq"]], out_dtype=BF16,
                                          name="proj_in", gather=pack_part(0, PART_A[1:]) + pack_part(0, PART_B))
            gathered_a, gathered_b = gathered_a + later[:4], later[4:]
        w = {**unpack_a(gathered_a), **unpack_b(gathered_b), **local_weights(l)}
        W.append(w)
        if l > 0:
            pg, pp, pq = _mm_fan(h_in, [w["wg"], w["wp"], w["wq"]], out_dtype=BF16, name="proj_in")
        o_a, o_c = _poolconv_fwd(pp, w["wbd"], w["scale"], w["conv"], name="poolconv_fwd")
        o_b, lse = _attn_fwd(pq, cos_t, sin_t, w["sinks"], name="attn_fwd")
        merged = _merge_fwd(o_a, o_b, o_c, pg, w["wbr"], name="merge_fwd")
        x1, xh1, rs1 = _mm_ln(merged, w["wo"], h_in, w["g1"], w["b1"], alpha=alpha, name="wo_ln1")
        if l + 1 < L:
            up_pre, gathered_a = _mm(x1, w["wup"], out_dtype=BF16, name="ffn_up",
                                     gather=pack_part(l + 1, PART_A))
        else:
            up_pre = _mm(x1, w["wup"], out_dtype=BF16, name="ffn_up")
        down = dict(alpha=alpha, name="ffn_down", tc=TC)
        wdn3 = w["wdn"].reshape(NJ, TC, D)
        if l + 1 < L:
            hact, x2, xh2, rs2, up, gathered_b = _ffn_down_fwd(up_pre, w["fcw"], wdn3, x1, w["g2"], w["b2"],
                                                               gather=pack_part(l + 1, PART_B), **down)
        else:
            hact, x2, xh2, rs2, up = _ffn_down_fwd(up_pre, w["fcw"], wdn3, x1, w["g2"], w["b2"], **down)
        saved.append(dict(up=up,x0=h_in, pg=pg, pp=pp, pq=pq, o_a=o_a, o_b=o_b, o_c=o_c, lse=lse, merged=merged,
                          x1=x1, xh1=xh1, rs1=rs1, up_pre=up_pre, hact=hact, xh2=xh2, rs2=rs2))
        h_in = x2

    dy, sq_lanes = _loss_head(h_in, tgt, name="loss_head")

    def pack_up(dw_up_t):
        nh = N_DEV // (2 * NJ)
        t = dw_up_t.reshape(NJ, 2, nh * (F2 // N_DEV), D).transpose(1, 0, 2, 3)
        return t.reshape(N_DEV, F2 // N_DEV, D).astype(BF16)

    def pack_grads(g):
        col = lambda a, n: a.reshape(a.shape[0], N_DEV, n).transpose(1, 0, 2)
        row = lambda a, n: a.reshape(N_DEV, n, a.shape[1])
        rest = [col(g["a"], D // N_DEV), col(g["b"], D // N_DEV), col(g["c"], D // N_DEV),
                row(g["w_o"], D // N_DEV), row(g["w_down"], F // N_DEV)]
        return [row(g["w_in_t"], in_shard).astype(BF16), pack_up(g["w_up_t"]),
                jnp.concatenate([p.reshape(N_DEV, -1, LANES).astype(BF16) for p in rest], axis=1)]

    my_c = lax.axis_index("c").astype(jnp.int32).reshape(1)
    my_chip = (2 * lax.axis_index("x") + lax.axis_index("y")).astype(jnp.int32).reshape(1)
    gw = [None] * L
    pair_sum = [None] * L
    from_chips = [None] * L
    for l in reversed(range(L)):
        w, sv = W[l], saved[l]
        if l == L - 1:
            dz2, dg2, db2 = _ln_bwd(dy, sv["xh2"], sv["rs2"], w["g2"], name="ln2_bwd")
        else:
            dz2, dg2, db2 = ln2_out
        dw_dn = _mm_tn(sv["hact"], dz2, name="down_bwd_w", tka=TC, na=NJ, tn=D, ts=1024, out_dtype=BF16)
        up_bwd = dict(alpha=alpha, name="ffn_up_bwd", tc=TC)
        if l + 1 < L:
            dh, from_sibling = _mm(dz2, w["wdn_t"], out_dtype=BF16, name="down_bwd_x", swap=packed_above)
            pair_sum[l + 1] = [_sum_sibling(p, r, my_c, "rs_sum_sibling") for p, r in zip(packed_above, from_sibling)]
        else:
            dh = _mm(dz2, w["wdn_t"], out_dtype=BF16, name="down_bwd_x")
        wup_t3 = w["wup_t"].reshape(NJ, 2 * TC, D)
        if l + 1 < L:
            dpre, dx1, dfcw, from_chips[l + 1] = _ffn_up_bwd(sv["up_pre"], sv["up"], dh, w["fcw"], wup_t3, dz2,
                                                             scatter=pair_sum[l + 1], **up_bwd)
        else:
            dpre, dx1, dfcw = _ffn_up_bwd(sv["up_pre"], sv["up"], dh, w["fcw"], wup_t3, dz2, **up_bwd)
        dfcw = dfcw.transpose(1, 0, 2).reshape(3, F2)
        dw_up_t = _mm_tn(dpre, sv["x1"], name="up_bwd_w", tka=TC, na=2 * NJ, tn=D, ts=1024,
                         out_dtype=BF16)
        dz1, dg1, db1 = _ln_bwd(dx1, sv["xh1"], sv["rs1"], w["g1"], name="ln1_bwd")
        if l == 0:
            early = [pack_up(dw_up_t)]
            dmerged, sib = _mm(dz1, w["wo_t"], out_dtype=BF16, name="wo_bwd_x", swap=early)
            pair_early = [_sum_sibling(early[0], sib[0], my_c, "rs_sum_sibling")]
        else:
            dmerged = _mm(dz1, w["wo_t"], out_dtype=BF16, name="wo_bwd_x")
        dw_o = _mm_tn(sv["merged"], dz1, name="wo_bwd_w", tka=D, tn=D, out_dtype=BF16)
        dpg, dprod, do_a, do_b, do_c = _merge_bwd(dmerged, sv["o_a"], sv["o_b"], sv["o_c"], sv["pg"],
                                                  w["wbr"], w["wbr_t"], name="merge_bwd")
        dw_a = _mm_tn(sv["o_a"], dprod, name="branch_a_bwd_w", tka=256, tn=D, b_off=0, out_dtype=BF16)
        dw_b = _mm_tn(sv["o_b"], dprod, name="branch_b_bwd_w", tka=512, tn=D, b_off=1, out_dtype=BF16)
        dw_c = _mm_tn(sv["o_c"], dprod, name="branch_c_bwd_w", tka=256, tn=D, b_off=2, out_dtype=BF16)
        dpq, dsink = _attn_bwd(sv["pq"], cos_t, sin_t, w["sinks"], do_b, sv["o_b"], sv["lse"], name="attn_bwd")
        dpp, pooled, dmixed, dscale, dconv = _poolconv_bwd(sv["pp"], do_a, do_c, w["wbd"], w["wbd_t"], w["scale"],
                                                           w["conv"], name="poolconv_bwd")
        dwbd = _mm_tn(pooled, dmixed, name="pool_bwd_w", tka=256, tn=256)
        dx_args = ([dpg, dpp, dpq], [w["wg_t"], w["wp_t"], w["wq_t"]], dz1)
        if l > 0:
            below = saved[l - 1]
            ln2_out = _mm_sum(*dx_args, add_scale=alpha, name="proj_in_bwd_x",
                              ln=(below["xh2"], below["rs2"], W[l - 1]["g2"]))
        else:
            dx, chips_early = _mm_sum(*dx_args, add_scale=alpha, name="proj_in_bwd_x", scatter=pair_early)
        dw_g = _mm_tn(dpg, sv["x0"], name="proj_gate_bwd_w", tka=1024, na=3, tn=D, out_dtype=BF16)
        dw_p = _mm_tn(dpp, sv["x0"], name="proj_poolconv_bwd_w", tka=1024, tn=D, out_dtype=BF16)
        dw_q = _mm_tn(dpq, sv["x0"], name="proj_qkv_bwd_w", tka=768, tn=D, out_dtype=BF16)
        dw_in_t = jnp.concatenate([dw_p[0:256], dw_q, dw_p[256:1024], dw_g], axis=0)
        dw_pool = jnp.stack([dwbd[64 * g:64 * (g + 1), 64 * g:64 * (g + 1)] for g in range(4)])
        gw[l] = dict(w_in_t=dw_in_t, a=dw_a, b=dw_b, c=dw_c, w_o=dw_o, w_up_t=dw_up_t, w_down=dw_dn,
                     w_pool=dw_pool, scale=dscale, sinks=dsink[:, 0], conv=dconv, fcw=_deinterleave_cols(dfcw, NJ),
                     g1=dg1, b1=db1, g2=dg2, b2=db2)
        packed_above = pack_grads(gw[l])
    late = [packed_above[0], packed_above[2]]
    from_sibling = _rs_sibling(late, "rs_sibling_last")
    pair_late = [_sum_sibling(p, r, my_c, "rs_sum_sibling") for p, r in zip(late, from_sibling)]
    chips_late = _rs_chips(pair_late, "rs_chips_last")
    pair_sum[0] = [pair_late[0], pair_early[0], pair_late[1]]
    from_chips[0] = [chips_late[0], chips_early[0], chips_late[1]]
    grad_x = dx.reshape(1, S, D)
    g_layers = [[_sum_chips(q, r, my_chip, "rs_sum_chips") for q, r in zip(pair_sum[l], from_chips[l])]
                for l in range(L)]

    def stack(k):
        return jnp.stack([gw[l][k] for l in range(L)])

    rep_vec = jnp.concatenate([
        stack("w_pool").reshape(-1), stack("scale").reshape(-1), stack("g1").reshape(-1), stack("b1").reshape(-1),
        stack("g2").reshape(-1), stack("b2").reshape(-1)])
    n_rep_full = -(-rep_vec.shape[0] // LANES)
    sinks_row = jnp.pad(stack("sinks").reshape(-1), (0, LANES - 8 * L))
    rep_vec = jnp.concatenate([_pad_rows(rep_vec, n_rep_full).reshape(-1), sinks_row, sq_lanes.reshape(-1)])
    loss_row = n_rep_full + 1
    n_rep = -(-(loss_row + 1) // 8) * 8
    rep_rows = _pad_rows(rep_vec, n_rep)
    dconv_by_dev = stack("conv").reshape(L, 3, N_DEV, -1).transpose(2, 0, 1, 3).reshape(N_DEV, -1)
    dfcw_by_dev = stack("fcw").reshape(L, 3, N_DEV, -1).transpose(2, 0, 1, 3).reshape(N_DEV, -1)
    n_mine = -(-(small_rows) // 8) * 8
    by_dev = jnp.concatenate([dconv_by_dev, dfcw_by_dev], axis=1)
    by_dev = jnp.pad(by_dev, ((0, 0), (0, n_mine * LANES - by_dev.shape[1]))).reshape(N_DEV * n_mine, LANES)
    small_g = _all_gather(jnp.concatenate([rep_rows, by_dev], axis=0), "ag_small_grads")
    rep_sum, mine_sum, loss11 = _small_reduce(small_g, n_rep, n_mine, 1.0 / D, loss_row, "small_reduce")
    loss = loss11[0, 0]

    names_big = ["w_in", "w_branch_a", "w_branch_b", "w_branch_c", "w_o", "w_up", "w_down"]
    ms_big = [m_w_in, m_w_branch_a, m_w_branch_b, m_w_branch_c, m_w_o, m_w_up, m_w_down]
    vs_big = [v_w_in, v_w_branch_a, v_w_branch_b, v_w_branch_c, v_w_o, v_w_up, v_w_down]
    out = {}
    for k, name in enumerate(names_big):
        wk = big[k]
        if k in (0, 5):
            g_t = jnp.stack([g[0 if k == 0 else 1] for g in g_layers])
            tr_ = lambda a: jnp.swapaxes(a, 1, 2)
            d, mo, vo = _adamw(tr_(wk), g_t, tr_(ms_big[k]), tr_(vs_big[k]), name="adamw_" + name)
            out[name] = (tr_(g_t), tr_(d), tr_(mo), tr_(vo))
            continue
        else:
            rest_ks = (1, 2, 3, 4, 6)
            o = sum(rows_l[q] for q in rest_ks[:rest_ks.index(k)])
            g_nat = jnp.concatenate([g[2][o:o + rows_l[k]] for g in g_layers], axis=0).reshape(wk.shape)
        d, mo, vo = _adamw(wk, g_nat, ms_big[k], vs_big[k], name="adamw_" + name)
        out[name] = (g_nat, d, mo, vo)

    def rep_pack(wp_, sc_, g1_, b1_, g2_, b2_, sk_):
        v = jnp.concatenate([wp_.reshape(-1), sc_.reshape(-1), g1_.reshape(-1), b1_.reshape(-1), g2_.reshape(-1),
                             b2_.reshape(-1)])
        return _pad_rows(jnp.concatenate([_pad_rows(v, n_rep_full).reshape(-1), sk_.reshape(-1)]), n_rep)

    def mine_pack(cw_, fw_):
        return _pad_rows(jnp.concatenate([cw_.reshape(-1), fw_.reshape(-1)]), n_mine)

    w_rep = rep_pack(w_pool, pool_scale, ln1_g, ln1_b, ln2_g, ln2_b, attn_sinks)
    m_rep = rep_pack(m_w_pool, m_pool_scale, m_ln1_g, m_ln1_b, m_ln2_g, m_ln2_b, m_attn_sinks)
    v_rep = rep_pack(v_w_pool, v_pool_scale, v_ln1_g, v_ln1_b, v_ln2_g, v_ln2_b, v_attn_sinks)
    g_rep = jnp.concatenate([rep_sum[:loss_row], jnp.zeros((n_rep - loss_row, LANES), F32)], axis=0)
    rep_res = (g_rep,) + tuple(_adamw(w_rep, g_rep, m_rep, v_rep, name="adamw_replicated"))
    w_mine = mine_pack(conv_w, ffn_conv_w)
    mine_res = (mine_sum,) + tuple(_adamw(w_mine, mine_sum, mine_pack(m_conv_w, m_ffn_conv_w),
                                          mine_pack(v_conv_w, v_ffn_conv_w), name="adamw_conv"))

    def rep_unpack(buf):
        flat = buf.reshape(-1)
        res, o = {}, 0
        for nm, ref in (("w_pool", w_pool), ("pool_scale", pool_scale), ("ln1_g", ln1_g), ("ln1_b", ln1_b),
                        ("ln2_g", ln2_g), ("ln2_b", ln2_b)):
            res[nm] = flat[o:o + ref.size].reshape(ref.shape)
            o += ref.size
        o = n_rep_full * LANES
        res["attn_sinks"] = flat[o:o + attn_sinks.size].reshape(attn_sinks.shape)
        return res

    def mine_unpack(buf):
        flat = buf.reshape(-1)
        return {"conv_w": flat[:n_cw].reshape(conv_w.shape),
                "ffn_conv_w": flat[n_cw:n_cw + n_fw].reshape(ffn_conv_w.shape)}

    order = ["w_in", "w_pool", "pool_scale", "attn_sinks", "conv_w", "w_branch_a", "w_branch_b", "w_branch_c", "w_o",
             "ln1_g", "ln1_b", "w_up", "ffn_conv_w", "w_down", "ln2_g", "ln2_b"]
    results = [loss, grad_x]
    for kind in range(4):
        rep_k, mine_k = rep_unpack(rep_res[kind]), mine_unpack(mine_res[kind])
        for nm in order:
            if nm in out:
                results.append(out[nm][kind])
            elif nm in rep_k:
                results.append(rep_k[nm])
            else:
                results.append(mine_k[nm])
    return tuple(results)
```

```python
import jax
import jax.numpy as jnp
from jax import lax
from jax.experimental import pallas as pl
from jax.experimental.pallas import tpu as pltpu

F32 = jnp.float32
BF16 = jnp.bfloat16

HEAD_DIM = 64
GROUP = 4
WINDOW = 128
ROT_DIM = 16
ROPE_THETA = 500000.0
LN_EPS = 1e-5
MASK_VALUE = -1e30
ADAM_LR, ADAM_B1, ADAM_B2, ADAM_EPS, ADAM_WD, ADAM_STEP = 0.001, 0.9, 0.999, 1e-08, 0.01, 10

N_DEV = 8
LANES = 1024
HALO = 16
MESH = pl.DeviceIdType.MESH
VMEM_LIMIT = 56 * 1024 * 1024


def _div_tile(n, want, mult=8):
    for t in range(min(want, n) // mult * mult, 0, -mult):
        if n % t == 0:
            return t
    return n


def _cp(*sem):
    return pltpu.CompilerParams(dimension_semantics=sem, vmem_limit_bytes=VMEM_LIMIT)


def _coords():
    return lax.axis_index("x"), lax.axis_index("y"), lax.axis_index("c")


def _all_gather(xs, name):
    xs = list(xs) if isinstance(xs, (list, tuple)) else [xs]
    n = len(xs)

    def body(*refs):
        ag_refs = (refs[:n], refs[n:2 * n]) + tuple(refs[2 * n:])
        _ag_start(*ag_refs)
        _ag_finish(*ag_refs)

    res = pl.pallas_call(
        body, name=name,
        out_shape=[jax.ShapeDtypeStruct((N_DEV,) + a.shape, a.dtype) for a in xs],
        in_specs=[pl.BlockSpec(memory_space=pl.ANY)] * n,
        out_specs=[pl.BlockSpec(memory_space=pl.ANY)] * n,
        scratch_shapes=_ag_sems(n),
    )(*xs)
    return res if n > 1 else res[0]


def _ag_sems(n):
    return [pltpu.SemaphoreType.DMA((7 * n,)), pltpu.SemaphoreType.DMA((7 * n,)), pltpu.SemaphoreType.DMA((n,))]


def _ag_copies(x_refs, out_refs, send_sems, recv_sems, local_sems):
    x, y, c = _coords()
    me, sibling = (x, y, c), (x, y, 1 - c)
    chips = [(1 - x, y), (x, 1 - y), (1 - x, 1 - y)]
    per_array = []
    for a, (x_ref, out_ref) in enumerate(zip(x_refs, out_refs)):
        def slot(px, py, pc, out_ref=out_ref):
            return out_ref.at[4 * px + 2 * py + pc]

        def copy(k, block, to, src=None, a=a, slot=slot):
            return pltpu.make_async_remote_copy(
                src_ref=slot(*block) if src is None else src, dst_ref=slot(*block),
                send_sem=send_sems.at[7 * a + k], recv_sem=recv_sems.at[7 * a + k],
                device_id=to, device_id_type=MESH)

        mine = pltpu.make_async_copy(x_ref, slot(*me), local_sems.at[a])
        first = [copy(0, me, sibling, src=x_ref)]
        first += [copy(1 + j, me, (*chip, c), src=x_ref) for j, chip in enumerate(chips)]
        passed = [copy(4 + j, (*chip, c), sibling) for j, chip in enumerate(chips)]
        from_chips = [copy(1 + j, (*chip, c), me) for j, chip in enumerate(chips)]
        from_sibling = [copy(0, sibling, me)] + [copy(4 + j, (*chip, 1 - c), me) for j, chip in enumerate(chips)]
        per_array.append((mine, first, passed, from_chips, from_sibling))
    return per_array


def _ag_start(*refs):
    for mine, first, _, _, _ in _ag_copies(*refs):
        mine.start()
        for cp in first:
            cp.start()


def _ag_finish(*refs):
    per_array = _ag_copies(*refs)
    for j in range(3):
        for _, _, passed, from_chips, _ in per_array:
            from_chips[j].wait_recv()
            passed[j].start()
    for mine, first, passed, _, from_sibling in per_array:
        for cp in from_sibling:
            cp.wait_recv()
        for cp in first + passed:
            cp.wait_send()
        mine.wait()


def _rs_sibling(ps, name):
    n = len(ps)

    def body(*refs):
        swap_refs = (refs[:n], refs[n:2 * n], refs[2 * n], refs[2 * n + 1])
        _swap_start(*swap_refs)
        _swap_finish(*swap_refs)

    return pl.pallas_call(
        body, name=name,
        out_shape=[jax.ShapeDtypeStruct((4,) + p.shape[1:], p.dtype) for p in ps],
        in_specs=[pl.BlockSpec(memory_space=pl.ANY)] * n,
        out_specs=[pl.BlockSpec(memory_space=pl.ANY)] * n,
        scratch_shapes=_swap_sems(n),
    )(*ps)


def _swap_sems(n):
    return [pltpu.SemaphoreType.DMA((4 * n,)), pltpu.SemaphoreType.DMA((4 * n,))]


def _swap_copies(p_refs, out_refs, send_sems, recv_sems):
    x, y, c = _coords()
    return [pltpu.make_async_remote_copy(
        src_ref=p_ref.at[4 * (j // 2) + 2 * (j % 2) + (1 - c)], dst_ref=out_ref.at[j],
        send_sem=send_sems.at[4 * a + j], recv_sem=recv_sems.at[4 * a + j],
        device_id=(x, y, 1 - c), device_id_type=MESH)
        for a, (p_ref, out_ref) in enumerate(zip(p_refs, out_refs)) for j in range(4)]


def _swap_start(*refs):
    for cp in _swap_copies(*refs):
        cp.start()


def _swap_finish(*refs):
    copies = _swap_copies(*refs)
    for cp in copies:
        cp.wait_recv()
    for cp in copies:
        cp.wait_send()


def _hosted(kind, arrays):
    n = len(arrays)
    if kind == "gather":
        return _ag_start, _ag_finish, [(N_DEV,) + a.shape for a in arrays], _ag_sems(n)
    assert kind == "swap"
    return _swap_start, _swap_finish, [(4,) + a.shape[1:] for a in arrays], _swap_sems(n)


def _rs_chips(qs, name):
    n = len(qs)

    def body(*refs):
        rs_refs = (refs[:n], refs[n:2 * n], refs[2 * n], refs[2 * n + 1])
        _rs_chips_start(*rs_refs)
        _rs_chips_finish(*rs_refs)

    return pl.pallas_call(
        body, name=name,
        out_shape=[jax.ShapeDtypeStruct((3,) + q.shape[1:], q.dtype) for q in qs],
        in_specs=[pl.BlockSpec(memory_space=pl.ANY)] * n,
        out_specs=[pl.BlockSpec(memory_space=pl.ANY)] * n,
        scratch_shapes=_rs_sems(n),
    )(*qs)


def _rs_sems(n):
    return [pltpu.SemaphoreType.DMA((3 * n,)), pltpu.SemaphoreType.DMA((3 * n,))]


def _rs_chips_copies(q_refs, out_refs, send_sems, recv_sems):
    x, y, c = _coords()
    chips = [(1 - x, y), (x, 1 - y), (1 - x, 1 - y)]
    return [pltpu.make_async_remote_copy(
        src_ref=q_ref.at[2 * cx + cy], dst_ref=out_ref.at[k],
        send_sem=send_sems.at[3 * a + k], recv_sem=recv_sems.at[3 * a + k], device_id=(cx, cy, c),
        device_id_type=MESH)
        for a, (q_ref, out_ref) in enumerate(zip(q_refs, out_refs)) for k, (cx, cy) in enumerate(chips)]


def _rs_chips_start(*refs):
    for cp in _rs_chips_copies(*refs):
        cp.start()


def _rs_chips_finish(*refs):
    copies = _rs_chips_copies(*refs)
    for cp in copies:
        cp.wait_recv()
    for cp in copies:
        cp.wait_send()


def _sum_sibling(p, recv, my_c, name, tr=512):
    _, R, C = p.shape
    tr = _div_tile(R, tr, 16)

    def body(c_ref, p_ref, r_ref, o_ref):
        o_ref[...] = (p_ref[...].astype(F32) + r_ref[...].astype(F32)).astype(o_ref.dtype)

    grid_spec = pltpu.PrefetchScalarGridSpec(
        num_scalar_prefetch=1, grid=(4, R // tr),
        in_specs=[pl.BlockSpec((1, tr, C), lambda j, r, c_ref: (4 * (j // 2) + 2 * (j % 2) + c_ref[0], r, 0)),
                  pl.BlockSpec((1, tr, C), lambda j, r, c_ref: (j, r, 0))],
        out_specs=pl.BlockSpec((1, tr, C), lambda j, r, c_ref: (j, r, 0)))
    return pl.pallas_call(body, name=name, grid_spec=grid_spec,
                          out_shape=jax.ShapeDtypeStruct((4, R, C), p.dtype),
                          compiler_params=_cp("parallel", "parallel"))(my_c, p, recv)


def _sum_chips(q, recv, my_chip, name, tr=512):
    _, R, C = q.shape
    tr = _div_tile(R, tr, 16)

    def body(i_ref, q_ref, r_ref, o_ref):
        acc = q_ref[0].astype(F32)
        for k in range(3):
            acc = acc + r_ref[k].astype(F32)
        o_ref[...] = acc

    grid_spec = pltpu.PrefetchScalarGridSpec(
        num_scalar_prefetch=1, grid=(R // tr,),
        in_specs=[pl.BlockSpec((1, tr, C), lambda r, i_ref: (i_ref[0], r, 0)),
                  pl.BlockSpec((3, tr, C), lambda r, i_ref: (0, r, 0))],
        out_specs=pl.BlockSpec((tr, C), lambda r, i_ref: (r, 0)))
    return pl.pallas_call(body, name=name, grid_spec=grid_spec,
                          out_shape=jax.ShapeDtypeStruct((R, C), F32),
                          compiler_params=_cp("parallel"))(my_chip, q, recv)


def _small_reduce(g, n_rep, n_mine, inv_d, loss_row, name):
    _, R, C = g.shape

    def body(g_ref, rep_ref, mine_ref, loss_ref):
        x, y, c = _coords()
        start = pl.multiple_of(n_rep + (4 * x + 2 * y + c) * n_mine, 8)
        rep = g_ref[0, 0:n_rep, :]
        mine = g_ref[0, pl.ds(start, n_mine), :]
        sq = g_ref[0, loss_row:loss_row + 1, :]
        for d in range(1, N_DEV):
            rep = rep + g_ref[d, 0:n_rep, :]
            mine = mine + g_ref[d, pl.ds(start, n_mine), :]
            sq = sq + g_ref[d, loss_row:loss_row + 1, :]
        rep_ref[...] = rep
        mine_ref[...] = mine
        loss_ref[...] = (0.5 * inv_d) * jnp.sum(sq, axis=1, keepdims=True)

    return pl.pallas_call(
        body, name=name,
        out_shape=(jax.ShapeDtypeStruct((n_rep, C), F32), jax.ShapeDtypeStruct((n_mine, C), F32),
                   jax.ShapeDtypeStruct((1, 1), F32)),
        compiler_params=pltpu.CompilerParams(vmem_limit_bytes=VMEM_LIMIT),
    )(g)


def _mm(a, b, *, out_dtype, name, tm=512, tn=None, tk=None, add=None, add_scale=1.0, gather=None, swap=None):
    M, K = a.shape
    N = b.shape[1]
    tm = min(tm, M)
    tn = N if tn is None else tn
    tk = K if tk is None else tk
    nk = K // tk
    has_add = add is not None
    hosted = gather if gather is not None else swap
    has_ag = hosted is not None
    n_g = len(hosted) if has_ag else 0
    if has_ag:
        comm_start, comm_finish, comm_shapes, comm_sems = _hosted("gather" if gather is not None else "swap", hosted)
    n_i, n_j = M // tm, N // tn

    def body(*refs):
        a_ref, b_ref = refs[0], refs[1]
        add_ref = refs[2] if has_add else None
        n_in = 2 + has_add + n_g
        o_ref = refs[n_in]
        if has_ag:
            ag_refs = (refs[n_in - n_g:n_in], refs[n_in + 1:n_in + 1 + n_g]) + tuple(
                refs[n_in + 1 + n_g:n_in + 1 + n_g + len(comm_sems)])
            pid = (pl.program_id(0), pl.program_id(1), pl.program_id(2))

            @pl.when((pid[0] == 0) & (pid[1] == 0) & (pid[2] == 0))
            def _():
                comm_start(*ag_refs)

        part = jnp.dot(a_ref[...].astype(BF16), b_ref[...].astype(BF16), preferred_element_type=F32)

        def finish(r):
            if has_add:
                r = r + add_scale * add_ref[...].astype(F32)
            o_ref[...] = r.astype(out_dtype)

        if nk == 1:
            finish(part)
        else:
            acc_ref = refs[-1]
            k = pl.program_id(2)

            @pl.when(k == 0)
            def _():
                acc_ref[...] = part

            @pl.when(k > 0)
            def _():
                acc_ref[...] += part

            @pl.when(k == nk - 1)
            def _():
                finish(acc_ref[...])

        if has_ag:
            @pl.when((pid[0] == n_i - 1) & (pid[1] == n_j - 1) & (pid[2] == nk - 1))
            def _():
                comm_finish(*ag_refs)

    b_mode = dict(pipeline_mode=pl.Buffered(1)) if (n_j == 1 and nk == 1) else {}
    in_specs = [pl.BlockSpec((tm, tk), lambda i, j, k: (i, k)),
                pl.BlockSpec((tk, tn), lambda i, j, k: (k, j), **b_mode)]
    args = [a, b]
    if has_add:
        in_specs.append(pl.BlockSpec((tm, tn), lambda i, j, k: (i, j)))
        args.append(add)
    out_specs = [pl.BlockSpec((tm, tn), lambda i, j, k: (i, j))]
    out_shape = [jax.ShapeDtypeStruct((M, N), out_dtype)]
    scratch = []
    if has_ag:
        in_specs += [pl.BlockSpec(memory_space=pl.ANY)] * n_g
        args += list(hosted)
        out_specs += [pl.BlockSpec(memory_space=pl.ANY)] * n_g
        out_shape += [jax.ShapeDtypeStruct(s, g.dtype) for s, g in zip(comm_shapes, hosted)]
        scratch += comm_sems
    if nk > 1:
        scratch.append(pltpu.VMEM((tm, tn), F32))
    sem = ("arbitrary",) * 3 if has_ag else ("parallel", "parallel", "arbitrary")
    res = pl.pallas_call(
        body, name=name, grid=(n_i, n_j, nk), in_specs=in_specs, out_specs=out_specs, out_shape=out_shape,
        scratch_shapes=scratch, compiler_params=_cp(*sem),
    )(*args)
    return (res[0], list(res[1:])) if has_ag else res[0]


def _mm_fan(a, bs, *, out_dtype, name, tm=512, gather=None):
    M, K = a.shape
    tm = min(tm, M)
    n = len(bs)
    n_i = M // tm
    n_g = len(gather) if gather is not None else 0

    def body(*refs):
        outs = refs[1 + n + n_g:1 + 2 * n + n_g]
        if n_g:
            ag_refs = (refs[1 + n:1 + n + n_g], refs[1 + 2 * n + n_g:1 + 2 * n + 2 * n_g]) + tuple(
                refs[1 + 2 * n + 2 * n_g:])

            @pl.when(pl.program_id(0) == 0)
            def _():
                _ag_start(*ag_refs)

        a_v = refs[0][...].astype(BF16)
        for k in range(n):
            outs[k][...] = jnp.dot(a_v, refs[1 + k][...].astype(BF16), preferred_element_type=F32).astype(out_dtype)

        if n_g:
            @pl.when(pl.program_id(0) == n_i - 1)
            def _():
                _ag_finish(*ag_refs)

    row = lambda i: (i, 0)
    hbm = pl.BlockSpec(memory_space=pl.ANY)
    res = pl.pallas_call(
        body, name=name, grid=(n_i,),
        in_specs=[pl.BlockSpec((tm, K), row)] + [pl.BlockSpec(b.shape, lambda i: (0, 0)) for b in bs] + [hbm] * n_g,
        out_specs=[pl.BlockSpec((tm, b.shape[1]), row) for b in bs] + [hbm] * n_g,
        out_shape=([jax.ShapeDtypeStruct((M, b.shape[1]), out_dtype) for b in bs]
                   + [jax.ShapeDtypeStruct((N_DEV,) + g.shape, g.dtype) for g in (gather or [])]),
        scratch_shapes=_ag_sems(n_g) if n_g else [],
        compiler_params=_cp("arbitrary" if n_g else "parallel"),
    )(a, *bs, *(gather or []))
    return (list(res[:n]), list(res[n:])) if n_g else list(res)


def _mm_sum(xs, bs, add, *, add_scale, name, tm=512, ln=None, scatter=None):
    M = xs[0].shape[0]
    N = bs[0].shape[1]
    tm = min(tm, M)
    n = len(xs)
    n_i = M // tm
    n_s = len(scatter) if scatter is not None else 0
    assert not (n_s and ln is not None)

    def body(*refs):
        if n_s:
            rs_refs = (refs[2 * n + 1:2 * n + 1 + n_s], refs[2 * n + 2 + n_s:2 * n + 2 + 2 * n_s],
                       refs[2 * n + 2 + 2 * n_s], refs[2 * n + 3 + 2 * n_s])

            @pl.when(pl.program_id(0) == 0)
            def _():
                _rs_chips_start(*rs_refs)

        acc = add_scale * refs[2 * n][...]
        for k in range(n):
            acc = acc + jnp.dot(refs[k][...].astype(BF16), refs[n + k][...].astype(BF16), preferred_element_type=F32)
        if ln is None:
            refs[2 * n + 1 + n_s][...] = acc
        else:
            xh_ref, rs_ref, g_ref, dz_ref, dg_ref, db_ref = refs[2 * n + 1:]
            _ln_bwd_tile(acc, xh_ref, rs_ref, g_ref, dz_ref, dg_ref, db_ref, pl.program_id(0) == 0)

        if n_s:
            @pl.when(pl.program_id(0) == n_i - 1)
            def _():
                _rs_chips_finish(*rs_refs)

    row = lambda i: (i, 0)
    vec = lambda i: (0, 0)
    hbm = pl.BlockSpec(memory_space=pl.ANY)
    in_specs = ([pl.BlockSpec((tm, x.shape[1]), row) for x in xs]
                + [pl.BlockSpec(b.shape, vec) for b in bs] + [pl.BlockSpec((tm, N), row)])
    if ln is None:
        res = pl.pallas_call(
            body, name=name, grid=(n_i,), in_specs=in_specs + [hbm] * n_s,
            out_specs=[pl.BlockSpec((tm, N), row)] + [hbm] * n_s,
            out_shape=([jax.ShapeDtypeStruct((M, N), F32)]
                       + [jax.ShapeDtypeStruct((3,) + q.shape[1:], q.dtype) for q in (scatter or [])]),
            scratch_shapes=_rs_sems(n_s) if n_s else [],
            compiler_params=_cp("arbitrary" if n_s else "parallel"),
        )(*xs, *bs, add, *(scatter or []))
        return (res[0], list(res[1:])) if n_s else res[0]
    in_specs += [pl.BlockSpec((tm, N), row), pl.BlockSpec((tm, 1), row), pl.BlockSpec((1, N), vec)]
    return pl.pallas_call(
        body, name=name, grid=(M // tm,), in_specs=in_specs,
        out_specs=[pl.BlockSpec((tm, N), row), pl.BlockSpec((1, N), vec), pl.BlockSpec((1, N), vec)],
        out_shape=(jax.ShapeDtypeStruct((M, N), F32), jax.ShapeDtypeStruct((1, N), F32),
                   jax.ShapeDtypeStruct((1, N), F32)),
        compiler_params=_cp("arbitrary"),
    )(*xs, *bs, add, *ln)


def _ln_bwd_tile(dyv, xh_ref, rs_ref, g_ref, dz_ref, dg_ref, db_ref, first):
    @pl.when(first)
    def _():
        dg_ref[...] = jnp.zeros_like(dg_ref)
        db_ref[...] = jnp.zeros_like(db_ref)

    xh = xh_ref[...].astype(F32)
    dyg = dyv * g_ref[...]
    c1 = jnp.mean(dyg, axis=-1, keepdims=True)
    c2 = jnp.mean(dyg * xh, axis=-1, keepdims=True)
    dz_ref[...] = rs_ref[...] * (dyg - c1 - xh * c2)
    dg_ref[...] += jnp.sum(dyv * xh, axis=0, keepdims=True)
    db_ref[...] += jnp.sum(dyv, axis=0, keepdims=True)


def _mm_ln(a, b, resid, gamma, beta, *, alpha, name, tm=512, tk=None):
    M, K = a.shape
    D = b.shape[1]
    tm = min(tm, M)
    tk = K if tk is None else tk
    nk = K // tk

    def body(a_ref, b_ref, r_ref, g_ref, be_ref, y_ref, xh_ref, rs_ref, *scratch):
        part = jnp.dot(a_ref[...].astype(BF16), b_ref[...].astype(BF16), preferred_element_type=F32)

        def finish(acc):
            z = alpha * r_ref[...] + acc
            mu = jnp.mean(z, axis=-1, keepdims=True)
            zc = z - mu
            var = jnp.mean(zc * zc, axis=-1, keepdims=True)
            rstd = lax.rsqrt(var + LN_EPS)
            xhat = zc * rstd
            y_ref[...] = xhat * g_ref[...] + be_ref[...]
            xh_ref[...] = xhat.astype(BF16)
            rs_ref[...] = rstd

        if nk == 1:
            finish(part)
        else:
            acc_ref = scratch[0]
            k = pl.program_id(1)

            @pl.when(k == 0)
            def _():
                acc_ref[...] = part

            @pl.when(k > 0)
            def _():
                acc_ref[...] += part

            @pl.when(k == nk - 1)
            def _():
                finish(acc_ref[...])

    row = lambda i, k: (i, 0)
    vec = lambda i, k: (0, 0)
    return pl.pallas_call(
        body, name=name, grid=(M // tm, nk),
        in_specs=[pl.BlockSpec((tm, tk), lambda i, k: (i, k)), pl.BlockSpec((tk, D), lambda i, k: (k, 0)),
                  pl.BlockSpec((tm, D), row), pl.BlockSpec((1, D), vec), pl.BlockSpec((1, D), vec)],
        out_specs=[pl.BlockSpec((tm, D), row), pl.BlockSpec((tm, D), row), pl.BlockSpec((tm, 1), row)],
        out_shape=(jax.ShapeDtypeStruct((M, D), F32), jax.ShapeDtypeStruct((M, D), BF16),
                   jax.ShapeDtypeStruct((M, 1), F32)),
        scratch_shapes=[pltpu.VMEM((tm, D), F32)] if nk > 1 else [],
        compiler_params=_cp("parallel", "arbitrary"),
    )(a, b, resid, gamma, beta)


def _mm_tn(a, b, *, name, tka, tn, a_off=0, na=1, b_off=0, nb=1, ts=2048, out_dtype=F32):
    S = a.shape[0]
    ts = min(ts, S)
    ns = S // ts
    direct = out_dtype == F32

    def body(a_ref, b_ref, o_ref, *scratch):
        acc_ref = o_ref if direct else scratch[0]
        s = pl.program_id(2)
        part = lax.dot_general(a_ref[...].astype(BF16), b_ref[...].astype(BF16),
                               (((0,), (0,)), ((), ())), preferred_element_type=F32)

        @pl.when(s == 0)
        def _():
            acc_ref[...] = part

        @pl.when(s > 0)
        def _():
            acc_ref[...] += part

        if not direct:
            @pl.when(s == ns - 1)
            def _():
                o_ref[...] = acc_ref[...].astype(out_dtype)

    return pl.pallas_call(
        body, name=name, grid=(na, nb, ns),
        in_specs=[pl.BlockSpec((ts, tka), lambda i, j, s: (s, a_off + i)),
                  pl.BlockSpec((ts, tn), lambda i, j, s: (s, b_off + j))],
        out_specs=pl.BlockSpec((tka, tn), lambda i, j, s: (i, j)),
        out_shape=jax.ShapeDtypeStruct((na * tka, nb * tn), out_dtype),
        scratch_shapes=[] if direct else [pltpu.VMEM((tka, tn), F32)],
        compiler_params=_cp("parallel", "parallel", "arbitrary"),
    )(a, b)


def _rope_tables(pos, inv_lane, sign_lane, name, ts=512):
    S = pos.shape[0]
    ts = min(ts, S)

    def body(p_ref, inv_ref, sg_ref, cos_ref, sin_ref):
        ang = p_ref[...].astype(F32) * inv_ref[...]
        cos_ref[...] = jnp.cos(ang)
        sin_ref[...] = jnp.sin(ang) * sg_ref[...]

    return pl.pallas_call(
        body, name=name, grid=(S // ts,),
        in_specs=[pl.BlockSpec((ts, 1), lambda i: (i, 0)), pl.BlockSpec((1, 128), lambda i: (0, 0)),
                  pl.BlockSpec((1, 128), lambda i: (0, 0))],
        out_specs=[pl.BlockSpec((ts, 128), lambda i: (i, 0))] * 2,
        out_shape=(jax.ShapeDtypeStruct((S, 128), F32),) * 2,
        compiler_params=_cp("parallel"),
    )(pos, inv_lane, sign_lane)


def _rope_swap(t):
    lane = lax.broadcasted_iota(jnp.int32, (1, 128), 1)
    lo = (lane % HEAD_DIM) < (ROT_DIM // 2)
    return jnp.where(lo, pltpu.roll(t, 128 - ROT_DIM // 2, 1), pltpu.roll(t, ROT_DIM // 2, 1))


def _rope_fwd(t, cos, sin):
    return t * cos + _rope_swap(t) * sin


def _rope_bwd(d, cos, sin):
    lane = lax.broadcasted_iota(jnp.int32, (1, 128), 1)
    return d * cos + jnp.where((lane % HEAD_DIM) < ROT_DIM, _rope_swap(d * sin), 0.0)


def _tile_heads(t):
    lane = lax.broadcasted_iota(jnp.int32, (1, 128), 1)
    r = pltpu.roll(t, 64, 1)
    h0 = jnp.where(lane < 64, t, r)
    h1 = jnp.where(lane < 64, r, t)
    return jnp.concatenate([h0, h0], axis=1), jnp.concatenate([h1, h1], axis=1)


def _fold_heads(d0, d1):
    lane = lax.broadcasted_iota(jnp.int32, (1, 128), 1)

    def fold(d):
        s = d[:, 0:128] + d[:, 128:256]
        return s + pltpu.roll(s, 64, 1)

    return jnp.where(lane < 64, fold(d0), fold(d1))


def _band4(n_keys):
    row = lax.broadcasted_iota(jnp.int32, (GROUP * WINDOW, n_keys), 0) % WINDOW
    col = lax.broadcasted_iota(jnp.int32, (GROUP * WINDOW, n_keys), 1)
    return (col > row) & (col <= row + WINDOW), col


def _head_masks():
    lane = lax.broadcasted_iota(jnp.int32, (1, GROUP * HEAD_DIM), 1)
    return [(lane // HEAD_DIM) == hl for hl in range(GROUP)]


def _stack_heads(t):
    zero = jnp.zeros_like(t)
    return jnp.concatenate([jnp.where(hm, t, zero) for hm in _head_masks()], axis=0)


def _unstack_heads(t4):
    out = None
    for hl, hm in enumerate(_head_masks()):
        part = jnp.where(hm, t4[hl * WINDOW:(hl + 1) * WINDOW], 0.0)
        out = part if out is None else out + part
    return out


def _sink_block(sink_ref, g):
    return jnp.concatenate([jnp.broadcast_to(sink_ref[g * GROUP + hl:g * GROUP + hl + 1, 0:1], (WINDOW, 256))
                            for hl in range(GROUP)], axis=0)


def _sink_column(sink_ref, g):
    return jnp.concatenate([jnp.broadcast_to(sink_ref[g * GROUP + hl:g * GROUP + hl + 1, 0:1], (WINDOW, 1))
                            for hl in range(GROUP)], axis=0)


def _attn_fwd(pq, cos_t, sin_t, sinks_b, *, name, ts=512):
    S = pq.shape[0]
    ts = min(ts, S)
    nq = ts // WINDOW
    scale = HEAD_DIM ** -0.5

    def body(cur_ref, prev_ref, cosc_ref, sinc_ref, cosp_ref, sinp_ref, sink_ref, o_ref, lse_ref):
        i = pl.program_id(0)
        cosc, sinc = cosc_ref[...], sinc_ref[...]
        q = cur_ref[:, 0:512].astype(F32)
        qr = jnp.concatenate(
            [_rope_fwd(q[:, j * 128:(j + 1) * 128], cosc, sinc) for j in range(4)], axis=1) * scale
        qr = qr.astype(BF16)
        kc = _rope_fwd(cur_ref[:, 512:640].astype(F32), cosc, sinc)
        kp = _rope_fwd(prev_ref[:, 0:128].astype(F32), cosp_ref[...], sinp_ref[...])
        k_all = jnp.concatenate([kp, kc], axis=0)
        v_all = jnp.concatenate([prev_ref[:, 128:256].astype(F32), cur_ref[:, 640:768].astype(F32)], axis=0)
        kt = [t.astype(BF16) for t in _tile_heads(k_all)]
        vt = [t.astype(BF16) for t in _tile_heads(v_all)]
        band, col = _band4(2 * WINDOW)
        ones = jnp.ones((2 * WINDOW, 256), BF16)
        key_t = lax.broadcasted_iota(jnp.int32, (2 * WINDOW, GROUP * WINDOW), 0)
        qry_t = lax.broadcasted_iota(jnp.int32, (2 * WINDOW, GROUP * WINDOW), 1) % WINDOW
        band_t = (key_t > qry_t) & (key_t <= qry_t + WINDOW)
        NT = (((1,), (1,)), ((), ()))
        for qb in range(nq):
            rows = slice(qb * WINDOW, (qb + 1) * WINDOW)
            keys = slice(qb * WINDOW, (qb + 2) * WINDOW)
            valid = band & ((col >= WINDOW) | (i * nq + qb > 0))
            valid_t = band_t & ((key_t >= WINDOW) | (i * nq + qb > 0))
            for g in range(2):
                qs = _stack_heads(qr[rows, g * 256:(g + 1) * 256])
                sink = _sink_block(sink_ref, g)
                s = lax.dot_general(qs, kt[g][keys], NT, preferred_element_type=F32)
                s_t = lax.dot_general(kt[g][keys], qs, NT, preferred_element_type=F32)
                m_t = jnp.max(jnp.where(valid_t, s_t, MASK_VALUE), axis=0, keepdims=True)
                m_rep = jnp.broadcast_to(m_t, (WINDOW, GROUP * WINDOW)).T
                m = jnp.maximum(jnp.concatenate([m_rep, m_rep], axis=1), sink)
                e = jnp.exp(jnp.where(valid, s, MASK_VALUE) - m).astype(BF16)
                l = jnp.dot(e, ones, preferred_element_type=F32) + jnp.exp(sink - m)
                pv = jnp.dot(e, vt[g][keys], preferred_element_type=F32)
                o_ref[rows, g * 256:(g + 1) * 256] = (_unstack_heads(pv) / _unstack_heads(l)).astype(BF16)
                lse4 = (m + jnp.log(l))[:, 0:1]
                for hl in range(GROUP):
                    h = g * GROUP + hl
                    lse_ref[rows, h:h + 1] = lse4[hl * WINDOW:(hl + 1) * WINDOW]

    hb = ts // WINDOW
    cur = lambda i: (i, 0)
    prev = lambda i: (jnp.maximum(i * hb - 1, 0), 0)
    return pl.pallas_call(
        body, name=name, grid=(S // ts,),
        in_specs=[pl.BlockSpec((ts, 768), cur),
                  pl.BlockSpec((WINDOW, 256), lambda i: (jnp.maximum(i * hb - 1, 0), 2)),
                  pl.BlockSpec((ts, 128), cur), pl.BlockSpec((ts, 128), cur),
                  pl.BlockSpec((WINDOW, 128), prev), pl.BlockSpec((WINDOW, 128), prev),
                  pl.BlockSpec((8, 128), lambda i: (0, 0))],
        out_specs=[pl.BlockSpec((ts, 512), cur), pl.BlockSpec((ts, 8), cur)],
        out_shape=(jax.ShapeDtypeStruct((S, 512), BF16), jax.ShapeDtypeStruct((S, 8), F32)),
        compiler_params=_cp("parallel"),
    )(pq, pq, cos_t, sin_t, cos_t, sin_t, sinks_b)


def _attn_bwd(pq, cos_t, sin_t, sinks_b, do, o, lse, *, name, ts=512):
    S = pq.shape[0]
    ts = min(ts, S)
    nq = ts // WINDOW
    nt = S // ts
    scale = HEAD_DIM ** -0.5
    NT = (((1,), (1,)), ((), ()))
    TN = (((0,), (0,)), ((), ()))

    def body(cur_ref, prev_ref, nxt_ref, cosc_ref, sinc_ref, cosp_ref, sinp_ref, cosn_ref, sinn_ref, sink_ref,
             doc_ref, don_ref, oc_ref, on_ref, lsec_ref, lsen_ref, dpq_ref, dsink_ref):
        i = pl.program_id(0)
        last = i == nt - 1
        cosc, sinc = cosc_ref[...], sinc_ref[...]
        cose = jnp.concatenate([cosc, cosn_ref[...]], axis=0)
        sine = jnp.concatenate([sinc, sinn_ref[...]], axis=0)
        q = jnp.concatenate([cur_ref[:, 0:512], nxt_ref[:, 0:512]], axis=0).astype(F32)
        qr = jnp.concatenate(
            [_rope_fwd(q[:, j * 128:(j + 1) * 128], cose, sine) for j in range(4)], axis=1) * scale
        qr = qr.astype(BF16)
        kc = _rope_fwd(cur_ref[:, 512:640].astype(F32), cosc, sinc)
        kp = _rope_fwd(prev_ref[:, 0:128].astype(F32), cosp_ref[...], sinp_ref[...])
        k_all = jnp.concatenate([kp, kc], axis=0)
        v_all = jnp.concatenate([prev_ref[:, 128:256].astype(F32), cur_ref[:, 640:768].astype(F32)], axis=0)
        kt = [t.astype(BF16) for t in _tile_heads(k_all)]
        vt = [t.astype(BF16) for t in _tile_heads(v_all)]
        don = jnp.where(last, jnp.zeros_like(don_ref[...]), don_ref[...])
        do_e = jnp.concatenate([doc_ref[...], don], axis=0)
        o_e = jnp.concatenate([oc_ref[...], on_ref[...]], axis=0)
        band2, col2 = _band4(2 * WINDOW)
        band1, _ = _band4(WINDOW)
        ones = jnp.ones((256, 256), BF16)

        @pl.when(i == 0)
        def _():
            dsink_ref[...] = jnp.zeros_like(dsink_ref)

        dk_acc = [[None] * (nq + 1) for _ in range(2)]
        dv_acc = [[None] * (nq + 1) for _ in range(2)]

        def add(acc, g, e, val):
            acc[g][e] = val if acc[g][e] is None else acc[g][e] + val

        for qb in range(nq + 1):
            halo = qb == nq
            rows = slice(qb * WINDOW, (qb + 1) * WINDOW)
            if halo:
                keys = slice(qb * WINDOW, (qb + 1) * WINDOW)
                valid = band1 & jnp.logical_not(last)
            else:
                keys = slice(qb * WINDOW, (qb + 2) * WINDOW)
                valid = band2 & ((col2 >= WINDOW) | (i * nq + qb > 0))
            dq_parts = []
            for g in range(2):
                qs = _stack_heads(qr[rows, g * 256:(g + 1) * 256])
                dos = _stack_heads(do_e[rows, g * 256:(g + 1) * 256])
                o_g = o_e[rows, g * 256:(g + 1) * 256].astype(F32)
                kt_b, vt_b = kt[g][keys], vt[g][keys]
                lse_src = lsen_ref if halo else lsec_ref
                lse_rows = slice(0, WINDOW) if halo else rows
                big_l = jnp.concatenate([lse_src[lse_rows, g * GROUP + hl:g * GROUP + hl + 1] for hl in range(GROUP)],
                                        axis=0)
                delta = jnp.dot((dos.astype(F32) * jnp.concatenate([o_g] * GROUP, axis=0)).astype(BF16), ones,
                                preferred_element_type=F32)
                s = lax.dot_general(qs, kt_b, NT, preferred_element_type=F32)
                p = jnp.exp(jnp.where(valid, s, MASK_VALUE) - big_l)
                dp = lax.dot_general(dos, vt_b, NT, preferred_element_type=F32)
                ds = (p * (dp - delta[:, 0:p.shape[1]])).astype(BF16)
                dk_g = lax.dot_general(ds, qs, TN, preferred_element_type=F32)
                dv_g = lax.dot_general(p.astype(BF16), dos, TN, preferred_element_type=F32)
                if not halo:
                    dq_parts.append(_unstack_heads(jnp.dot(ds, kt_b, preferred_element_type=F32)))
                    dsink4 = jnp.exp(_sink_column(sink_ref, g) - big_l) * delta[:, 0:1]
                    for hl in range(GROUP):
                        h = g * GROUP + hl
                        dsink_h = -jnp.sum(dsink4[hl * WINDOW:(hl + 1) * WINDOW], axis=0, keepdims=True)
                        dsink_ref[h:h + 1, :] += jnp.broadcast_to(dsink_h, (1, 128))
                add(dk_acc, g, qb, dk_g[0:WINDOW])
                add(dv_acc, g, qb, dv_g[0:WINDOW])
                if not halo:
                    add(dk_acc, g, qb + 1, dk_g[WINDOW:2 * WINDOW])
                    add(dv_acc, g, qb + 1, dv_g[WINDOW:2 * WINDOW])
            if not halo:
                cs, sn = cosc[rows], sinc[rows]
                for g in range(2):
                    dq_g = dq_parts[g] * scale
                    for j in range(2):
                        c0 = g * 256 + j * 128
                        dpq_ref[rows, c0:c0 + 128] = _rope_bwd(dq_g[:, j * 128:(j + 1) * 128], cs, sn).astype(BF16)
        for e in range(1, nq + 1):
            rows = slice((e - 1) * WINDOW, e * WINDOW)
            dk = _fold_heads(dk_acc[0][e], dk_acc[1][e])
            dv = _fold_heads(dv_acc[0][e], dv_acc[1][e])
            dpq_ref[rows, 512:640] = _rope_bwd(dk, cosc[rows], sinc[rows]).astype(BF16)
            dpq_ref[rows, 640:768] = dv.astype(BF16)

    hb = ts // WINDOW
    nblk = S // WINDOW
    cur = lambda i: (i, 0)
    prev = lambda i: (jnp.maximum(i * hb - 1, 0), 0)
    nxt = lambda i: (jnp.minimum((i + 1) * hb, nblk - 1), 0)
    return pl.pallas_call(
        body, name=name, grid=(nt,),
        in_specs=[pl.BlockSpec((ts, 768), cur),
                  pl.BlockSpec((WINDOW, 256), lambda i: (jnp.maximum(i * hb - 1, 0), 2)),
                  pl.BlockSpec((WINDOW, 768), nxt),
                  pl.BlockSpec((ts, 128), cur), pl.BlockSpec((ts, 128), cur),
                  pl.BlockSpec((WINDOW, 128), prev), pl.BlockSpec((WINDOW, 128), prev),
                  pl.BlockSpec((WINDOW, 128), nxt), pl.BlockSpec((WINDOW, 128), nxt),
                  pl.BlockSpec((8, 128), lambda i: (0, 0)),
                  pl.BlockSpec((ts, 512), cur), pl.BlockSpec((WINDOW, 512), nxt),
                  pl.BlockSpec((ts, 512), cur), pl.BlockSpec((WINDOW, 512), nxt),
                  pl.BlockSpec((ts, 8), cur), pl.BlockSpec((WINDOW, 8), nxt)],
        out_specs=[pl.BlockSpec((ts, 768), cur), pl.BlockSpec((8, 128), lambda i: (0, 0))],
        out_shape=(jax.ShapeDtypeStruct((S, 768), BF16), jax.ShapeDtypeStruct((8, 128), F32)),
        compiler_params=_cp("arbitrary"),
    )(pq, pq, pq, cos_t, sin_t, cos_t, sin_t, cos_t, sin_t, sinks_b, do, do, o, o, lse, lse)


def _shift_dn(x, k):
    return pltpu.roll(x, k, 0)


def _shift_up(x, k):
    return pltpu.roll(x, x.shape[0] - k, 0)


def _pool_lane_select(vals):
    lane = lax.broadcasted_iota(jnp.int32, (1, 256), 1)
    out = vals[3]
    for g in (2, 1, 0):
        out = jnp.where(lane < 64 * (g + 1), vals[g], out)
    return out


def _pool_inv_count(t0, n):
    t = t0 + lax.broadcasted_iota(jnp.int32, (n, 256), 0)
    lane = lax.broadcasted_iota(jnp.int32, (n, 256), 1)
    w = jnp.where(lane < 64, 2, jnp.where(lane < 128, 4, jnp.where(lane < 192, 8, 16)))
    return 1.0 / jnp.minimum(t + 1, w).astype(F32)


def _pooled(u_ext, t0, n):
    s2 = u_ext + _shift_dn(u_ext, 1)
    s4 = s2 + _shift_dn(s2, 2)
    s8 = s4 + _shift_dn(s4, 4)
    s16 = s8 + _shift_dn(s8, 8)
    win = _pool_lane_select([s2, s4, s8, s16])[HALO:HALO + n]
    return win * _pool_inv_count(t0, n) - u_ext[HALO:HALO + n]


def _poolconv_fwd(pp, wbd, pool_scale, conv_w, *, name, ts=512):
    S = pp.shape[0]
    ts = min(ts, S)

    def body(cur_ref, prev_ref, wbd_ref, sc_ref, cw_ref, oa_ref, oc_ref):
        i = pl.program_id(0)
        prev = jnp.where(i > 0, prev_ref[...].astype(F32), 0.0)
        u_ext = jnp.concatenate([prev[:, 0:256], cur_ref[:, 0:256].astype(F32)], axis=0)
        pooled = _pooled(u_ext, i * ts, ts)
        mixed = jnp.dot(pooled.astype(BF16), wbd_ref[...], preferred_element_type=F32)
        oa_ref[...] = (mixed * sc_ref[...]).astype(BF16)
        v_ext = jnp.concatenate([prev[:, 256:512] * prev[:, 768:1024],
                                 cur_ref[:, 256:512].astype(F32) * cur_ref[:, 768:1024].astype(F32)], axis=0)
        cv = cw_ref[2:3, :] * v_ext + cw_ref[1:2, :] * _shift_dn(v_ext, 1) + cw_ref[0:1, :] * _shift_dn(v_ext, 2)
        oc_ref[...] = (cur_ref[:, 512:768].astype(F32) * cv[HALO:HALO + ts]).astype(BF16)

    hb = ts // HALO
    cur = lambda i: (i, 0)
    const = lambda i: (0, 0)
    return pl.pallas_call(
        body, name=name, grid=(S // ts,),
        in_specs=[pl.BlockSpec((ts, 1024), cur),
                  pl.BlockSpec((HALO, 1024), lambda i: (jnp.maximum(i * hb - 1, 0), 0)),
                  pl.BlockSpec((256, 256), const), pl.BlockSpec((1, 256), const), pl.BlockSpec((3, 256), const)],
        out_specs=[pl.BlockSpec((ts, 256), cur)] * 2,
        out_shape=(jax.ShapeDtypeStruct((S, 256), BF16),) * 2,
        compiler_params=_cp("parallel"),
    )(pp, pp, wbd, pool_scale, conv_w)


def _poolconv_bwd(pp, do_a, do_c, wbd, wbd_t, pool_scale, conv_w, *, name, ts=512):
    S = pp.shape[0]
    ts = min(ts, S)
    nt = S // ts
    n_e = ts + 2 * HALO

    def body(cur_ref, prev_ref, nxt_ref, dac_ref, dan_ref, dcc_ref, dcn_ref, wbd_ref, wbdt_ref, sc_ref, cw_ref,
             dpp_ref, pooled_ref, dmixed_ref, dsc_ref, dcw_ref):
        i = pl.program_id(0)

        @pl.when(i == 0)
        def _():
            dsc_ref[...] = jnp.zeros_like(dsc_ref)
            dcw_ref[...] = jnp.zeros_like(dcw_ref)

        prev = jnp.where(i > 0, prev_ref[...].astype(F32), 0.0)
        nxt = nxt_ref[...].astype(F32)
        cur = cur_ref[...].astype(F32)
        not_last = i < nt - 1
        da_n = jnp.where(not_last, dan_ref[...].astype(F32), 0.0)
        dc_n = jnp.where(not_last, dcn_ref[...].astype(F32), 0.0)
        zeros_h = jnp.zeros((HALO, 256), F32)
        sc = sc_ref[...]

        u_ext = jnp.concatenate([prev[:, 0:256], cur[:, 0:256]], axis=0)
        pooled = _pooled(u_ext, i * ts, ts)
        pooled_b = pooled.astype(BF16)
        pooled_ref[...] = pooled_b
        mixed = jnp.dot(pooled_b, wbd_ref[...], preferred_element_type=F32)
        da_c = dac_ref[...].astype(F32)
        dsc_ref[...] += jnp.sum(da_c * mixed, axis=0, keepdims=True)
        dmixed_e = jnp.concatenate([da_c, da_n], axis=0) * sc
        dmixed_ref[...] = dmixed_e[0:ts].astype(BF16)
        dpooled = jnp.dot(dmixed_e.astype(BF16), wbdt_ref[...], preferred_element_type=F32)
        qd = dpooled * _pool_inv_count(i * ts, ts + HALO)
        f2 = qd + _shift_up(qd, 1)
        f4 = f2 + _shift_up(f2, 2)
        f8 = f4 + _shift_up(f4, 4)
        f16 = f8 + _shift_up(f8, 8)
        du = (_pool_lane_select([f2, f4, f8, f16]) - dpooled)[0:ts]
        dpp_ref[:, 0:256] = du.astype(BF16)

        xc_e = jnp.concatenate([prev[:, 256:512], cur[:, 256:512], nxt[:, 256:512]], axis=0)
        gc_e = jnp.concatenate([prev[:, 768:1024], cur[:, 768:1024], nxt[:, 768:1024]], axis=0)
        gb_e = jnp.concatenate([zeros_h, cur[:, 512:768], nxt[:, 512:768]], axis=0)
        dc_e = jnp.concatenate([zeros_h, dcc_ref[...].astype(F32), dc_n], axis=0)
        v_e = xc_e * gc_e
        v1, v2 = _shift_dn(v_e, 1), _shift_dn(v_e, 2)
        w0, w1, w2 = cw_ref[0:1, :], cw_ref[1:2, :], cw_ref[2:3, :]
        cv = w2 * v_e + w1 * v1 + w0 * v2
        dcv = dc_e * gb_e
        dv = w2 * dcv + w1 * _shift_up(dcv, 1) + w0 * _shift_up(dcv, 2)
        tile = slice(HALO, HALO + ts)
        dpp_ref[:, 256:512] = (dv * gc_e)[tile].astype(BF16)
        dpp_ref[:, 512:768] = (dc_e * cv)[tile].astype(BF16)
        dpp_ref[:, 768:1024] = (dv * xc_e)[tile].astype(BF16)
        dcv_t = dcv[tile]
        dcw_ref[0:1, :] += jnp.sum(dcv_t * v2[tile], axis=0, keepdims=True)
        dcw_ref[1:2, :] += jnp.sum(dcv_t * v1[tile], axis=0, keepdims=True)
        dcw_ref[2:3, :] += jnp.sum(dcv_t * v_e[tile], axis=0, keepdims=True)

    hb = ts // HALO
    nblk = S // HALO
    cur = lambda i: (i, 0)
    const = lambda i: (0, 0)
    prev = lambda i: (jnp.maximum(i * hb - 1, 0), 0)
    nxt = lambda i: (jnp.minimum((i + 1) * hb, nblk - 1), 0)
    del n_e
    return pl.pallas_call(
        body, name=name, grid=(nt,),
        in_specs=[pl.BlockSpec((ts, 1024), cur), pl.BlockSpec((HALO, 1024), prev), pl.BlockSpec((HALO, 1024), nxt),
                  pl.BlockSpec((ts, 256), cur), pl.BlockSpec((HALO, 256), nxt),
                  pl.BlockSpec((ts, 256), cur), pl.BlockSpec((HALO, 256), nxt),
                  pl.BlockSpec((256, 256), const), pl.BlockSpec((256, 256), const),
                  pl.BlockSpec((1, 256), const), pl.BlockSpec((3, 256), const)],
        out_specs=[pl.BlockSpec((ts, 1024), cur), pl.BlockSpec((ts, 256), cur), pl.BlockSpec((ts, 256), cur),
                   pl.BlockSpec((1, 256), const), pl.BlockSpec((3, 256), const)],
        out_shape=(jax.ShapeDtypeStruct((S, 1024), BF16), jax.ShapeDtypeStruct((S, 256), BF16),
                   jax.ShapeDtypeStruct((S, 256), BF16), jax.ShapeDtypeStruct((1, 256), F32),
                   jax.ShapeDtypeStruct((3, 256), F32)),
        compiler_params=_cp("arbitrary"),
    )(pp, pp, pp, do_a, do_a, do_c, do_c, wbd, wbd_t, pool_scale, conv_w)


def _sigmoid(x):
    return 0.5 * jnp.tanh(0.5 * x) + 0.5


def _merge_fwd(o_a, o_b, o_c, glog, w_br, *, name, ts=512):
    S = o_a.shape[0]
    D = w_br.shape[1]
    ts = min(ts, S)

    def body(oa_ref, ob_ref, oc_ref, gl_ref, w_ref, m_ref):
        pa = jnp.dot(oa_ref[...], w_ref[0:256, :], preferred_element_type=F32)
        pb = jnp.dot(ob_ref[...], w_ref[256:768, :], preferred_element_type=F32)
        pc = jnp.dot(oc_ref[...], w_ref[768:1024, :], preferred_element_type=F32)
        m = _sigmoid(gl_ref[:, 0:D].astype(F32)) * pa
        m = m + _sigmoid(gl_ref[:, D:2 * D].astype(F32)) * pb
        m = m + _sigmoid(gl_ref[:, 2 * D:3 * D].astype(F32)) * pc
        m_ref[...] = m.astype(BF16)

    cur = lambda i: (i, 0)
    return pl.pallas_call(
        body, name=name, grid=(S // ts,),
        in_specs=[pl.BlockSpec((ts, 256), cur), pl.BlockSpec((ts, 512), cur), pl.BlockSpec((ts, 256), cur),
                  pl.BlockSpec((ts, 3 * D), cur), pl.BlockSpec((1024, D), lambda i: (0, 0))],
        out_specs=pl.BlockSpec((ts, D), cur),
        out_shape=jax.ShapeDtypeStruct((S, D), BF16),
        compiler_params=_cp("parallel"),
    )(o_a, o_b, o_c, glog, w_br)


def _merge_bwd(dm, o_a, o_b, o_c, glog, w_br, w_br_t, *, name, ts=256):
    S = o_a.shape[0]
    D = w_br.shape[1]
    ts = min(ts, S)

    def body(dm_ref, oa_ref, ob_ref, oc_ref, gl_ref, w_ref, wt_ref, dgl_ref, dp_ref, doa_ref, dob_ref, doc_ref):
        dmv = dm_ref[...].astype(F32)
        branches = ((oa_ref, 0, 256, doa_ref), (ob_ref, 256, 768, dob_ref), (oc_ref, 768, 1024, doc_ref))
        for b, (o_ref, r0, r1, do_ref) in enumerate(branches):
            prod = jnp.dot(o_ref[...], w_ref[r0:r1, :], preferred_element_type=F32)
            gate = _sigmoid(gl_ref[:, b * D:(b + 1) * D].astype(F32))
            dgl_ref[:, b * D:(b + 1) * D] = (dmv * prod * gate * (1.0 - gate)).astype(BF16)
            dprod = (dmv * gate).astype(BF16)
            dp_ref[:, b * D:(b + 1) * D] = dprod
            do_ref[...] = jnp.dot(dprod, wt_ref[:, r0:r1], preferred_element_type=F32).astype(BF16)

    cur = lambda i: (i, 0)
    const = lambda i: (0, 0)
    return pl.pallas_call(
        body, name=name, grid=(S // ts,),
        in_specs=[pl.BlockSpec((ts, D), cur), pl.BlockSpec((ts, 256), cur), pl.BlockSpec((ts, 512), cur),
                  pl.BlockSpec((ts, 256), cur), pl.BlockSpec((ts, 3 * D), cur),
                  pl.BlockSpec((1024, D), const), pl.BlockSpec((D, 1024), const)],
        out_specs=[pl.BlockSpec((ts, 3 * D), cur), pl.BlockSpec((ts, 3 * D), cur), pl.BlockSpec((ts, 256), cur),
                   pl.BlockSpec((ts, 512), cur), pl.BlockSpec((ts, 256), cur)],
        out_shape=(jax.ShapeDtypeStruct((S, 3 * D), BF16), jax.ShapeDtypeStruct((S, 3 * D), BF16),
                   jax.ShapeDtypeStruct((S, 256), BF16), jax.ShapeDtypeStruct((S, 512), BF16),
                   jax.ShapeDtypeStruct((S, 256), BF16)),
        compiler_params=_cp("parallel"),
    )(dm, o_a, o_b, o_c, glog, w_br, w_br_t)


FFN_CHUNK = 128
FFN_DOT_CHUNKS = 4


def _conv3(x, w_ref, cols):
    x1, x2 = _shift_dn(x, 1), _shift_dn(x, 2)
    return w_ref[2:3, cols] * x + w_ref[1:2, cols] * x1 + w_ref[0:1, cols] * x2, x1, x2


def _ffn_down_fwd(up_pre, fcw, w_down3, resid, gamma, beta, *, alpha, name, tc, ts=512, gather=None):
    S, F2 = up_pre.shape
    D = resid.shape[1]
    ts = min(ts, S)
    nt = S // ts
    nj = F2 // (2 * tc)
    has_ag = gather is not None
    n_g = len(gather) if has_ag else 0

    def body(cur_ref, prev_ref, w_ref, wd_ref, r_ref, g_ref, be_ref, *rest):
        h_ref, y_ref, xh_ref, rs_ref, up_ref = rest[n_g:n_g + 5]
        acc_ref = rest[2 * n_g + 5]
        if has_ag:
            ag_refs = (rest[:n_g], rest[n_g + 5:2 * n_g + 5]) + tuple(rest[2 * n_g + 6:2 * n_g + 9])
        i, j = pl.program_id(0), pl.program_id(1)
        if has_ag:
            @pl.when((i == 0) & (j == 0))
            def _():
                _ag_start(*ag_refs)

        part = None
        for c in range(tc // FFN_CHUNK):
            halves = []
            for half in range(2):
                cols = slice(half * tc + c * FFN_CHUNK, half * tc + (c + 1) * FFN_CHUNK)
                prev = jnp.where(i > 0, prev_ref[:, cols].astype(F32), 0.0)
                x = jnp.concatenate([prev, cur_ref[:, cols].astype(F32)], axis=0)
                halves.append(_conv3(x, w_ref, cols)[0][HALO:HALO + ts])
                up_ref[:, cols] = halves[-1].astype(BF16)
            a, b = halves
            h_ref[:, c * FFN_CHUNK:(c + 1) * FFN_CHUNK] = (a * _sigmoid(a) * b).astype(BF16)
            if (c + 1) % FFN_DOT_CHUNKS == 0 or c + 1 == tc // FFN_CHUNK:
                k0 = (c // FFN_DOT_CHUNKS) * FFN_DOT_CHUNKS * FFN_CHUNK
                piece = jnp.dot(h_ref[:, k0:(c + 1) * FFN_CHUNK], wd_ref[j, k0:(c + 1) * FFN_CHUNK, :],
                                preferred_element_type=F32)
                part = piece if part is None else part + piece

        @pl.when(j == 0)
        def _():
            acc_ref[...] = part

        @pl.when(j > 0)
        def _():
            acc_ref[...] += part

        @pl.when(j == nj - 1)
        def _():
            z = alpha * r_ref[...] + acc_ref[...]
            mu = jnp.mean(z, axis=-1, keepdims=True)
            zc = z - mu
            var = jnp.mean(zc * zc, axis=-1, keepdims=True)
            rstd = lax.rsqrt(var + LN_EPS)
            xhat = zc * rstd
            y_ref[...] = xhat * g_ref[...] + be_ref[...]
            xh_ref[...] = xhat.astype(BF16)
            rs_ref[...] = rstd

        if has_ag:
            @pl.when((i == nt - 1) & (j == nj - 1))
            def _():
                _ag_finish(*ag_refs)

    hb = ts // HALO
    row = lambda i, j: (i, 0)
    vec = lambda i, j: (0, 0)
    in_specs = [pl.BlockSpec((ts, 2 * tc), lambda i, j: (i, j)),
                pl.BlockSpec((HALO, 2 * tc), lambda i, j: (jnp.maximum(i * hb - 1, 0), j)),
                pl.BlockSpec((3, 2 * tc), lambda i, j: (0, j)),
                pl.BlockSpec((nj, tc, D), lambda i, j: (0, 0, 0), pipeline_mode=pl.Buffered(1)),
                pl.BlockSpec((ts, D), row), pl.BlockSpec((1, D), vec), pl.BlockSpec((1, D), vec)]
    out_specs = [pl.BlockSpec((ts, tc), lambda i, j: (i, j)), pl.BlockSpec((ts, D), row), pl.BlockSpec((ts, D), row),
                 pl.BlockSpec((ts, 1), row), pl.BlockSpec((ts, 2 * tc), lambda i, j: (i, j))]
    out_shape = [jax.ShapeDtypeStruct((S, F2 // 2), BF16), jax.ShapeDtypeStruct((S, D), F32),
                 jax.ShapeDtypeStruct((S, D), BF16), jax.ShapeDtypeStruct((S, 1), F32),
                 jax.ShapeDtypeStruct((S, F2), BF16)]
    args = [up_pre, up_pre, fcw, w_down3, resid, gamma, beta]
    scratch = [pltpu.VMEM((ts, D), F32)]
    if has_ag:
        in_specs += [pl.BlockSpec(memory_space=pl.ANY)] * n_g
        args += list(gather)
        out_specs += [pl.BlockSpec(memory_space=pl.ANY)] * n_g
        out_shape += [jax.ShapeDtypeStruct((N_DEV,) + g.shape, g.dtype) for g in gather]
        scratch += _ag_sems(n_g)
    res = pl.pallas_call(
        body, name=name, grid=(nt, nj), in_specs=in_specs, out_specs=out_specs, out_shape=out_shape,
        scratch_shapes=scratch, compiler_params=_cp("arbitrary", "arbitrary"),
    )(*args)
    return tuple(res[:5]) + ((list(res[5:]),) if has_ag else ())


def _ffn_up_bwd(up_pre, up, dh, fcw, w_up_t3, dz, *, alpha, name, tc, ts=256, scatter=None):
    S, F2 = up_pre.shape
    D = dz.shape[1]
    ts = min(ts, S)
    nt = S // ts
    nj = F2 // (2 * tc)
    has_rs = scatter is not None
    n_s = len(scatter) if has_rs else 0
    tile = slice(0, ts)

    def body(x_ref, upc_ref, upn_ref, dhc_ref, dhn_ref, w_ref, wt_ref, dz_ref, *rest):
        dpre_ref, dx_ref, dw_ref = rest[n_s:n_s + 3]
        acc_ref = rest[2 * n_s + 3]
        if has_rs:
            rs_refs = (rest[:n_s], rest[n_s + 3:2 * n_s + 3], rest[2 * n_s + 4], rest[2 * n_s + 5])
        i, j = pl.program_id(0), pl.program_id(1)

        @pl.when((i == 0) & (j == 0))
        def _():
            dw_ref[...] = jnp.zeros_like(dw_ref)
            if has_rs:
                _rs_chips_start(*rs_refs)

        part = None
        for c in range(tc // FFN_CHUNK):
            lanes = slice(c * FFN_CHUNK, (c + 1) * FFN_CHUNK)
            dh_n = jnp.where(i < nt - 1, dhn_ref[:, lanes].astype(F32), 0.0)
            dh_e = jnp.concatenate([dhc_ref[:, lanes].astype(F32), dh_n], axis=0)
            cols_of = [slice(half * tc + c * FFN_CHUNK, half * tc + (c + 1) * FFN_CHUNK) for half in range(2)]
            a, b = [jnp.concatenate([upc_ref[:, cols].astype(F32), upn_ref[:, cols].astype(F32)], axis=0)
                    for cols in cols_of]
            sg = _sigmoid(a)
            dups = [dh_e * b * (sg * (1.0 + a * (1.0 - sg))), dh_e * (a * sg)]
            for half in range(2):
                cols, dup = cols_of[half], dups[half]
                dup1, dup2 = _shift_up(dup, 1), _shift_up(dup, 2)
                dpre = w_ref[2:3, cols] * dup + w_ref[1:2, cols] * dup1 + w_ref[0:1, cols] * dup2
                dpre_ref[:, cols] = dpre[tile].astype(BF16)
                x = x_ref[:, cols].astype(F32)
                dw_ref[j, 0:1, cols] += jnp.sum(dup2[tile] * x, axis=0, keepdims=True)
                dw_ref[j, 1:2, cols] += jnp.sum(dup1[tile] * x, axis=0, keepdims=True)
                dw_ref[j, 2:3, cols] += jnp.sum(dup[tile] * x, axis=0, keepdims=True)
            if (c + 1) % FFN_DOT_CHUNKS == 0 or c + 1 == tc // FFN_CHUNK:
                k0 = (c // FFN_DOT_CHUNKS) * FFN_DOT_CHUNKS * FFN_CHUNK
                for half in range(2):
                    ks = slice(half * tc + k0, half * tc + (c + 1) * FFN_CHUNK)
                    piece = jnp.dot(dpre_ref[:, ks], wt_ref[j, ks, :], preferred_element_type=F32)
                    part = piece if part is None else part + piece

        @pl.when(j == 0)
        def _():
            acc_ref[...] = part

        @pl.when(j > 0)
        def _():
            acc_ref[...] += part

        @pl.when(j == nj - 1)
        def _():
            dx_ref[...] = acc_ref[...] + alpha * dz_ref[...]

        if has_rs:
            @pl.when((i == nt - 1) & (j == nj - 1))
            def _():
                _rs_chips_finish(*rs_refs)

    hb = ts // HALO
    nblk = S // HALO
    nxt = lambda i, j: (jnp.minimum((i + 1) * hb, nblk - 1), j)
    row = lambda i, j: (i, 0)
    in_specs = [pl.BlockSpec((ts, 2 * tc), lambda i, j: (i, j)),
                pl.BlockSpec((ts, 2 * tc), lambda i, j: (i, j)), pl.BlockSpec((HALO, 2 * tc), nxt),
                pl.BlockSpec((ts, tc), lambda i, j: (i, j)), pl.BlockSpec((HALO, tc), nxt),
                pl.BlockSpec((3, 2 * tc), lambda i, j: (0, j)),
                pl.BlockSpec((nj, 2 * tc, D), lambda i, j: (0, 0, 0), pipeline_mode=pl.Buffered(1)),
                pl.BlockSpec((ts, D), row)]
    out_specs = [pl.BlockSpec((ts, 2 * tc), lambda i, j: (i, j)), pl.BlockSpec((ts, D), row),
                 pl.BlockSpec((nj, 3, 2 * tc), lambda i, j: (0, 0, 0))]
    out_shape = [jax.ShapeDtypeStruct((S, F2), BF16), jax.ShapeDtypeStruct((S, D), F32),
                 jax.ShapeDtypeStruct((nj, 3, 2 * tc), F32)]
    args = [up_pre, up, up, dh, dh, fcw, w_up_t3, dz]
    scratch = [pltpu.VMEM((ts, D), F32)]
    if has_rs:
        in_specs += [pl.BlockSpec(memory_space=pl.ANY)] * n_s
        args += list(scatter)
        out_specs += [pl.BlockSpec(memory_space=pl.ANY)] * n_s
        out_shape += [jax.ShapeDtypeStruct((3,) + q.shape[1:], q.dtype) for q in scatter]
        scratch += _rs_sems(n_s)
    res = pl.pallas_call(
        body, name=name, grid=(nt, nj), in_specs=in_specs, out_specs=out_specs, out_shape=out_shape,
        scratch_shapes=scratch, compiler_params=_cp("arbitrary", "arbitrary"),
    )(*args)
    return tuple(res[:3]) + ((list(res[3:]),) if has_rs else ())


def _ln_bwd(dy, xhat, rstd, gamma, *, name, ts=512):
    S, D = dy.shape
    ts = min(ts, S)

    def body(dy_ref, xh_ref, rs_ref, g_ref, dz_ref, dg_ref, db_ref):
        _ln_bwd_tile(dy_ref[...], xh_ref, rs_ref, g_ref, dz_ref, dg_ref, db_ref, pl.program_id(0) == 0)

    cur = lambda i: (i, 0)
    const = lambda i: (0, 0)
    return pl.pallas_call(
        body, name=name, grid=(S // ts,),
        in_specs=[pl.BlockSpec((ts, D), cur), pl.BlockSpec((ts, D), cur), pl.BlockSpec((ts, 1), cur),
                  pl.BlockSpec((1, D), const)],
        out_specs=[pl.BlockSpec((ts, D), cur), pl.BlockSpec((1, D), const), pl.BlockSpec((1, D), const)],
        out_shape=(jax.ShapeDtypeStruct((S, D), F32), jax.ShapeDtypeStruct((1, D), F32),
                   jax.ShapeDtypeStruct((1, D), F32)),
        compiler_params=_cp("arbitrary"),
    )(dy, xhat, rstd, gamma)


def _loss_head(y, tgt, *, name, ts=512):
    S, D = y.shape
    ts = min(ts, S)

    def body(y_ref, t_ref, dy_ref, sq_ref):
        @pl.when(pl.program_id(0) == 0)
        def _():
            sq_ref[...] = jnp.zeros_like(sq_ref)

        e = y_ref[...] - t_ref[...]
        dy_ref[...] = e * (1.0 / D)
        sq_ref[...] += jnp.sum(e * e, axis=0, keepdims=True)

    cur = lambda i: (i, 0)
    return pl.pallas_call(
        body, name=name, grid=(S // ts,),
        in_specs=[pl.BlockSpec((ts, D), cur), pl.BlockSpec((ts, D), cur)],
        out_specs=[pl.BlockSpec((ts, D), cur), pl.BlockSpec((1, D), lambda i: (0, 0))],
        out_shape=(jax.ShapeDtypeStruct((S, D), F32), jax.ShapeDtypeStruct((1, D), F32)),
        compiler_params=_cp("arbitrary"),
    )(y, tgt)


def _adamw(w, g, m, v, *, name, tr=512):
    lead = w.shape[:-2]
    R, C = w.shape[-2:]
    tr = _div_tile(R, tr)
    c1 = 1.0 - ADAM_B1 ** ADAM_STEP
    c2 = 1.0 - ADAM_B2 ** ADAM_STEP

    def body(w_ref, g_ref, m_ref, v_ref, d_ref, mo_ref, vo_ref):
        gv = g_ref[...]
        m2 = ADAM_B1 * m_ref[...] + (1.0 - ADAM_B1) * gv
        v2 = ADAM_B2 * v_ref[...] + (1.0 - ADAM_B2) * (gv * gv)
        m_hat = m2 / c1
        v_hat = v2 / c2
        d_ref[...] = -ADAM_LR * (m_hat / (jnp.sqrt(v_hat) + ADAM_EPS) + ADAM_WD * w_ref[...])
        mo_ref[...] = m2
        vo_ref[...] = v2

    if lead:
        spec = pl.BlockSpec((1, tr, C), lambda l, i: (l, i, 0))
        grid = (lead[0], R // tr)
    else:
        spec = pl.BlockSpec((tr, C), lambda i: (i, 0))
        grid = (R // tr,)
    return pl.pallas_call(
        body, name=name, grid=grid,
        in_specs=[spec] * 4, out_specs=[spec] * 3,
        out_shape=(jax.ShapeDtypeStruct(w.shape, F32),) * 3,
        compiler_params=_cp(*(("parallel",) * len(grid))),
    )(w, g, m, v)


def _interleave_cols(w, nj):
    lead, f2 = w.shape[:-1], w.shape[-1]
    tc = f2 // (2 * nj)
    w = w.reshape(lead + (2, nj, tc))
    return jnp.swapaxes(w, -3, -2).reshape(lead + (f2,))


def _deinterleave_cols(w, nj):
    lead, f2 = w.shape[:-1], w.shape[-1]
    tc = f2 // (2 * nj)
    w = w.reshape(lead + (nj, 2, tc))
    return jnp.swapaxes(w, -3, -2).reshape(lead + (f2,))


def _block_diag(w_pool):
    return jnp.concatenate([jnp.pad(w_pool[g], ((0, 0), (64 * g, 192 - 64 * g))) for g in range(4)], axis=0)


def _pad_rows(v, rows):
    return jnp.pad(v, (0, rows * LANES - v.shape[0])).reshape(rows, LANES)


def kernel(x, positions, w_in, w_pool, pool_scale, attn_sinks, conv_w, w_branch_a, w_branch_b, w_branch_c, w_o, ln1_g, ln1_b, w_up, ffn_conv_w, w_down, ln2_g, ln2_b, loss_target, m_w_in, m_w_pool, m_pool_scale, m_attn_sinks, m_conv_w, m_w_branch_a, m_w_branch_b, m_w_branch_c, m_w_o, m_ln1_g, m_ln1_b, m_w_up, m_ffn_conv_w, m_w_down, m_ln2_g, m_ln2_b, v_w_in, v_w_pool, v_pool_scale, v_attn_sinks, v_conv_w, v_w_branch_a, v_w_branch_b, v_w_branch_c, v_w_o, v_ln1_g, v_ln1_b, v_w_up, v_ffn_conv_w, v_w_down, v_ln2_g, v_ln2_b):
    L, D, in_shard = w_in.shape
    S = x.shape[1]
    IN = in_shard * N_DEV
    F2 = w_up.shape[2] * N_DEV
    F = F2 // 2
    assert D == 1024 and IN == 1792 + 3 * D and x.shape[0] == 1 and S % 512 == 0
    alpha = (2 * L) ** 0.25
    NJ = 2
    TC = F // NJ
    xs = x.reshape(S, D)
    tgt = loss_target.reshape(S, D)

    big = [w_in, w_branch_a, w_branch_b, w_branch_c, w_o, w_up, w_down]
    PART_A, PART_B = (0, 1, 2, 3, 4), (5, 6)
    rows_l = [a.size // L // LANES for a in big]
    offs_l = [sum(rows_l[:k]) for k in range(len(big) + 1)]

    def pack_part(l, part):
        return [(big[k][l].T if k == 0 else big[k][l]).astype(BF16) for k in part]

    n_cw, n_fw = conv_w.size, ffn_conv_w.size
    small_rows = -(-(n_cw + n_fw) // LANES)
    small = _pad_rows(jnp.concatenate([conv_w.reshape(-1), ffn_conv_w.reshape(-1)]), small_rows)
    gsmall = _all_gather(small, "ag_conv_weights").reshape(N_DEV, -1)
    conv_full = gsmall[:, :n_cw].reshape(N_DEV, L, 3, -1).transpose(1, 2, 0, 3).reshape(L, 3, 256)
    fcw_full = gsmall[:, n_cw:n_cw + n_fw].reshape(N_DEV, L, 3, -1).transpose(1, 2, 0, 3).reshape(L, 3, F2)
    fcw_full = _interleave_cols(fcw_full, NJ)

    def shard_of(g, part, k, shape):
        assert g[part.index(k)].shape == (N_DEV,) + shape
        return g[part.index(k)]

    def unpack_a(g):
        win_t = shard_of(g, PART_A, 0, (in_shard, D)).reshape(IN, D)
        wg_t = win_t[1792:]
        wp_t = jnp.concatenate([win_t[0:256], win_t[1024:1792]], axis=0)
        wq_t = win_t[256:1024]
        wg, wp, wq = wg_t.T, wp_t.T, wq_t.T
        if g[1] is None:
            return dict(wg=wg, wp=wp, wq=wq)
        wa = shard_of(g, PART_A, 1, (256, D // N_DEV)).transpose(1, 0, 2).reshape(256, D)
        wb = shard_of(g, PART_A, 2, (512, D // N_DEV)).transpose(1, 0, 2).reshape(512, D)
        wc = shard_of(g, PART_A, 3, (256, D // N_DEV)).transpose(1, 0, 2).reshape(256, D)
        wbr = jnp.concatenate([wa, wb, wc], axis=0)
        wo = shard_of(g, PART_A, 4, (D // N_DEV, D)).reshape(D, D)
        return dict(wg=wg, wp=wp, wq=wq, wg_t=wg_t, wp_t=wp_t, wq_t=wq_t, wbr=wbr, wbr_t=wbr.T, wo=wo, wo_t=wo.T)

    def unpack_b(g):
        nh = N_DEV // (2 * NJ)
        wup = shard_of(g, PART_B, 5, (D, F2 // N_DEV)).reshape(2, NJ, nh, D, F2 // N_DEV)
        wup = wup.transpose(3, 1, 0, 2, 4).reshape(D, F2)
        wdn = shard_of(g, PART_B, 6, (F // N_DEV, D)).reshape(F, D)
        return dict(wup=wup, wup_t=wup.T, wdn=wdn, wdn_t=wdn.T)

    def local_weights(l):
        wbd = _block_diag(w_pool[l]).astype(BF16)
        return dict(wbd=wbd, wbd_t=wbd.T, scale=pool_scale[l].reshape(1, 256), conv=conv_full[l],
                    fcw=fcw_full[l], sinks=jnp.broadcast_to(attn_sinks[l].reshape(8, 1), (8, 128)),
                    g1=ln1_g[l].reshape(1, D), b1=ln1_b[l].reshape(1, D),
                    g2=ln2_g[l].reshape(1, D), b2=ln2_b[l].reshape(1, D))

    inv_freq = ROPE_THETA ** (-jnp.arange(0, ROT_DIM, 2, dtype=F32) / ROT_DIM)
    head_lane = jnp.concatenate([inv_freq, inv_freq, jnp.zeros((HEAD_DIM - ROT_DIM,), F32)])
    head_sign = jnp.concatenate([-jnp.ones((8,), F32), jnp.ones((8,), F32), jnp.zeros((HEAD_DIM - ROT_DIM,), F32)])
    inv_lane = jnp.tile(head_lane, 2).reshape(1, 128)
    sign_lane = jnp.tile(head_sign, 2).reshape(1, 128)
    cos_t, sin_t = _rope_tables(positions.reshape(S, 1), inv_lane, sign_lane, "rope_tables")

    saved, W = [], []
    h_in = xs
    gathered_a = [_all_gather(pack_part(0, PART_A[:1]), "ag_weights_first")]
    for l in range(L):
        if l == 0:
            w_in_only = unpack_a(gathered_a + [None] * 4)
            (pg, pp, pq), later = _mm_fan(h_in, [w_in_only["wg"], w_in_only["wp"], w_in_only["wq"]], out_dtype=BF16,
                                          name="proj_in", gather=pack_part(0, PART_A[1:]) + pack_part(0, PART_B))
            gathered_a, gathered_b = gathered_a + later[:4], later[4:]
        w = {**unpack_a(gathered_a), **unpack_b(gathered_b), **local_weights(l)}
        W.append(w)
        if l > 0:
            pg, pp, pq = _mm_fan(h_in, [w["wg"], w["wp"], w["wq"]], out_dtype=BF16, name="proj_in")
        o_a, o_c = _poolconv_fwd(pp, w["wbd"], w["scale"], w["conv"], name="poolconv_fwd")
        o_b, lse = _attn_fwd(pq, cos_t, sin_t, w["sinks"], name="attn_fwd")
        merged = _merge_fwd(o_a, o_b, o_c, pg, w["wbr"], name="merge_fwd")
        x1, xh1, rs1 = _mm_ln(merged, w["wo"], h_in, w["g1"], w["b1"], alpha=alpha, name="wo_ln1")
        if l + 1 < L:
            up_pre, gathered_a = _mm(x1, w["wup"], out_dtype=BF16, name="ffn_up",
                                     gather=pack_part(l + 1, PART_A))
        else:
            up_pre = _mm(x1, w["wup"], out_dtype=BF16, name="ffn_up")
        down = dict(alpha=alpha, name="ffn_down", tc=TC)
        wdn3 = w["wdn"].reshape(NJ, TC, D)
        if l + 1 < L:
            hact, x2, xh2, rs2, up, gathered_b = _ffn_down_fwd(up_pre, w["fcw"], wdn3, x1, w["g2"], w["b2"],
                                                               gather=pack_part(l + 1, PART_B), **down)
        else:
            hact, x2, xh2, rs2, up = _ffn_down_fwd(up_pre, w["fcw"], wdn3, x1, w["g2"], w["b2"], **down)
        saved.append(dict(up=up,x0=h_in, pg=pg, pp=pp, pq=pq, o_a=o_a, o_b=o_b, o_c=o_c, lse=lse, merged=merged,
                          x1=x1, xh1=xh1, rs1=rs1, up_pre=up_pre, hact=hact, xh2=xh2, rs2=rs2))
        h_in = x2

    dy, sq_lanes = _loss_head(h_in, tgt, name="loss_head")

    def pack_up(dw_up_t):
        nh = N_DEV // (2 * NJ)
        t = dw_up_t.reshape(NJ, 2, nh * (F2 // N_DEV), D).transpose(1, 0, 2, 3)
        return t.reshape(N_DEV, F2 // N_DEV, D).astype(BF16)

    def pack_grads(g):
        col = lambda a, n: a.reshape(a.shape[0], N_DEV, n).transpose(1, 0, 2)
        row = lambda a, n: a.reshape(N_DEV, n, a.shape[1])
        rest = [col(g["a"], D // N_DEV), col(g["b"], D // N_DEV), col(g["c"], D // N_DEV),
                row(g["w_o"], D // N_DEV), row(g["w_down"], F // N_DEV)]
        return [row(g["w_in_t"], in_shard).astype(BF16), pack_up(g["w_up_t"]),
                jnp.concatenate([p.reshape(N_DEV, -1, LANES).astype(BF16) for p in rest], axis=1)]

    my_c = lax.axis_index("c").astype(jnp.int32).reshape(1)
    my_chip = (2 * lax.axis_index("x") + lax.axis_index("y")).astype(jnp.int32).reshape(1)
    gw = [None] * L
    pair_sum = [None] * L
    from_chips = [None] * L
    for l in reversed(range(L)):
        w, sv = W[l], saved[l]
        if l == L - 1:
            dz2, dg2, db2 = _ln_bwd(dy, sv["xh2"], sv["rs2"], w["g2"], name="ln2_bwd")
        else:
            dz2, dg2, db2 = ln2_out
        dw_dn = _mm_tn(sv["hact"], dz2, name="down_bwd_w", tka=TC, na=NJ, tn=D, ts=1024, out_dtype=BF16)
        up_bwd = dict(alpha=alpha, name="ffn_up_bwd", tc=TC)
        if l + 1 < L:
            dh, from_sibling = _mm(dz2, w["wdn_t"], out_dtype=BF16, name="down_bwd_x", swap=packed_above)
            pair_sum[l + 1] = [_sum_sibling(p, r, my_c, "rs_sum_sibling") for p, r in zip(packed_above, from_sibling)]
        else:
            dh = _mm(dz2, w["wdn_t"], out_dtype=BF16, name="down_bwd_x")
        wup_t3 = w["wup_t"].reshape(NJ, 2 * TC, D)
        if l + 1 < L:
            dpre, dx1, dfcw, from_chips[l + 1] = _ffn_up_bwd(sv["up_pre"], sv["up"], dh, w["fcw"], wup_t3, dz2,
                                                             scatter=pair_sum[l + 1], **up_bwd)
        else:
            dpre, dx1, dfcw = _ffn_up_bwd(sv["up_pre"], sv["up"], dh, w["fcw"], wup_t3, dz2, **up_bwd)
        dfcw = dfcw.transpose(1, 0, 2).reshape(3, F2)
        dw_up_t = _mm_tn(dpre, sv["x1"], name="up_bwd_w", tka=TC, na=2 * NJ, tn=D, ts=1024,
                         out_dtype=BF16)
        dz1, dg1, db1 = _ln_bwd(dx1, sv["xh1"], sv["rs1"], w["g1"], name="ln1_bwd")
        if l == 0:
            early = [pack_up(dw_up_t)]
            dmerged, sib = _mm(dz1, w["wo_t"], out_dtype=BF16, name="wo_bwd_x", swap=early)
            pair_early = [_sum_sibling(early[0], sib[0], my_c, "rs_sum_sibling")]
        else:
            dmerged = _mm(dz1, w["wo_t"], out_dtype=BF16, name="wo_bwd_x")
        dw_o = _mm_tn(sv["merged"], dz1, name="wo_bwd_w", tka=D, tn=D, out_dtype=BF16)
        dpg, dprod, do_a, do_b, do_c = _merge_bwd(dmerged, sv["o_a"], sv["o_b"], sv["o_c"], sv["pg"],
                                                  w["wbr"], w["wbr_t"], name="merge_bwd")
        dw_a = _mm_tn(sv["o_a"], dprod, name="branch_a_bwd_w", tka=256, tn=D, b_off=0, out_dtype=BF16)
        dw_b = _mm_tn(sv["o_b"], dprod, name="branch_b_bwd_w", tka=512, tn=D, b_off=1, out_dtype=BF16)
        dw_c = _mm_tn(sv["o_c"], dprod, name="branch_c_bwd_w", tka=256, tn=D, b_off=2, out_dtype=BF16)
        dpq, dsink = _attn_bwd(sv["pq"], cos_t, sin_t, w["sinks"], do_b, sv["o_b"], sv["lse"], name="attn_bwd")
        dpp, pooled, dmixed, dscale, dconv = _poolconv_bwd(sv["pp"], do_a, do_c, w["wbd"], w["wbd_t"], w["scale"],
                                                           w["conv"], name="poolconv_bwd")
        dwbd = _mm_tn(pooled, dmixed, name="pool_bwd_w", tka=256, tn=256)
        dx_args = ([dpg, dpp, dpq], [w["wg_t"], w["wp_t"], w["wq_t"]], dz1)
        if l > 0:
            below = saved[l - 1]
            ln2_out = _mm_sum(*dx_args, add_scale=alpha, name="proj_in_bwd_x",
                              ln=(below["xh2"], below["rs2"], W[l - 1]["g2"]))
        else:
            dx, chips_early = _mm_sum(*dx_args, add_scale=alpha, name="proj_in_bwd_x", scatter=pair_early)
        dw_g = _mm_tn(dpg, sv["x0"], name="proj_gate_bwd_w", tka=1024, na=3, tn=D, out_dtype=BF16)
        dw_p = _mm_tn(dpp, sv["x0"], name="proj_poolconv_bwd_w", tka=1024, tn=D, out_dtype=BF16)
        dw_q = _mm_tn(dpq, sv["x0"], name="proj_qkv_bwd_w", tka=768, tn=D, out_dtype=BF16)
        dw_in_t = jnp.concatenate([dw_p[0:256], dw_q, dw_p[256:1024], dw_g], axis=0)
        dw_pool = jnp.stack([dwbd[64 * g:64 * (g + 1), 64 * g:64 * (g + 1)] for g in range(4)])
        gw[l] = dict(w_in_t=dw_in_t, a=dw_a, b=dw_b, c=dw_c, w_o=dw_o, w_up_t=dw_up_t, w_down=dw_dn,
                     w_pool=dw_pool, scale=dscale, sinks=dsink[:, 0], conv=dconv, fcw=_deinterleave_cols(dfcw, NJ),
                     g1=dg1, b1=db1, g2=dg2, b2=db2)
        packed_above = pack_grads(gw[l])
    late = [packed_above[0], packed_above[2]]
    from_sibling = _rs_sibling(late, "rs_sibling_last")
    pair_late = [_sum_sibling(p, r, my_c, "rs_sum_sibling") for p, r in zip(late, from_sibling)]
    chips_late = _rs_chips(pair_late, "rs_chips_last")
    pair_sum[0] = [pair_late[0], pair_early[0], pair_late[1]]
    from_chips[0] = [chips_late[0], chips_early[0], chips_late[1]]
    grad_x = dx.reshape(1, S, D)
    g_layers = [[_sum_chips(q, r, my_chip, "rs_sum_chips") for q, r in zip(pair_sum[l], from_chips[l])]
                for l in range(L)]

    def stack(k):
        return jnp.stack([gw[l][k] for l in range(L)])

    rep_vec = jnp.concatenate([
        stack("w_pool").reshape(-1), stack("scale").reshape(-1), stack("g1").reshape(-1), stack("b1").reshape(-1),
        stack("g2").reshape(-1), stack("b2").reshape(-1)])
    n_rep_full = -(-rep_vec.shape[0] // LANES)
    sinks_row = jnp.pad(stack("sinks").reshape(-1), (0, LANES - 8 * L))
    rep_vec = jnp.concatenate([_pad_rows(rep_vec, n_rep_full).reshape(-1), sinks_row, sq_lanes.reshape(-1)])
    loss_row = n_rep_full + 1
    n_rep = -(-(loss_row + 1) // 8) * 8
    rep_rows = _pad_rows(rep_vec, n_rep)
    dconv_by_dev = stack("conv").reshape(L, 3, N_DEV, -1).transpose(2, 0, 1, 3).reshape(N_DEV, -1)
    dfcw_by_dev = stack("fcw").reshape(L, 3, N_DEV, -1).transpose(2, 0, 1, 3).reshape(N_DEV, -1)
    n_mine = -(-(small_rows) // 8) * 8
    by_dev = jnp.concatenate([dconv_by_dev, dfcw_by_dev], axis=1)
    by_dev = jnp.pad(by_dev, ((0, 0), (0, n_mine * LANES - by_dev.shape[1]))).reshape(N_DEV * n_mine, LANES)
    small_g = _all_gather(jnp.concatenate([rep_rows, by_dev], axis=0), "ag_small_grads")
    rep_sum, mine_sum, loss11 = _small_reduce(small_g, n_rep, n_mine, 1.0 / D, loss_row, "small_reduce")
    loss = loss11[0, 0]

    names_big = ["w_in", "w_branch_a", "w_branch_b", "w_branch_c", "w_o", "w_up", "w_down"]
    ms_big = [m_w_in, m_w_branch_a, m_w_branch_b, m_w_branch_c, m_w_o, m_w_up, m_w_down]
    vs_big = [v_w_in, v_w_branch_a, v_w_branch_b, v_w_branch_c, v_w_o, v_w_up, v_w_down]
    out = {}
    for k, name in enumerate(names_big):
        wk = big[k]
        if k in (0, 5):
            g_t = jnp.stack([g[0 if k == 0 else 1] for g in g_layers])
            tr_ = lambda a: jnp.swapaxes(a, 1, 2)
            d, mo, vo = _adamw(tr_(wk), g_t, tr_(ms_big[k]), tr_(vs_big[k]), name="adamw_" + name)
            out[name] = (tr_(g_t), tr_(d), tr_(mo), tr_(vo))
            continue
        else:
            rest_ks = (1, 2, 3, 4, 6)
            o = sum(rows_l[q] for q in rest_ks[:rest_ks.index(k)])
            g_nat = jnp.concatenate([g[2][o:o + rows_l[k]] for g in g_layers], axis=0).reshape(wk.shape)
        d, mo, vo = _adamw(wk, g_nat, ms_big[k], vs_big[k], name="adamw_" + name)
        out[name] = (g_nat, d, mo, vo)

    def rep_pack(wp_, sc_, g1_, b1_, g2_, b2_, sk_):
        v = jnp.concatenate([wp_.reshape(-1), sc_.reshape(-1), g1_.reshape(-1), b1_.reshape(-1), g2_.reshape(-1),
                             b2_.reshape(-1)])
        return _pad_rows(jnp.concatenate([_pad_rows(v, n_rep_full).reshape(-1), sk_.reshape(-1)]), n_rep)

    def mine_pack(cw_, fw_):
        return _pad_rows(jnp.concatenate([cw_.reshape(-1), fw_.reshape(-1)]), n_mine)

    w_rep = rep_pack(w_pool, pool_scale, ln1_g, ln1_b, ln2_g, ln2_b, attn_sinks)
    m_rep = rep_pack(m_w_pool, m_pool_scale, m_ln1_g, m_ln1_b, m_ln2_g, m_ln2_b, m_attn_sinks)
    v_rep = rep_pack(v_w_pool, v_pool_scale, v_ln1_g, v_ln1_b, v_ln2_g, v_ln2_b, v_attn_sinks)
    g_rep = jnp.concatenate([rep_sum[:loss_row], jnp.zeros((n_rep - loss_row, LANES), F32)], axis=0)
    rep_res = (g_rep,) + tuple(_adamw(w_rep, g_rep, m_rep, v_rep, name="adamw_replicated"))
    w_mine = mine_pack(conv_w, ffn_conv_w)
    mine_res = (mine_sum,) + tuple(_adamw(w_mine, mine_sum, mine_pack(m_conv_w, m_ffn_conv_w),
                                          mine_pack(v_conv_w, v_ffn_conv_w), name="adamw_conv"))

    def rep_unpack(buf):
        flat = buf.reshape(-1)
        res, o = {}, 0
        for nm, ref in (("w_pool", w_pool), ("pool_scale", pool_scale), ("ln1_g", ln1_g), ("ln1_b", ln1_b),
                        ("ln2_g", ln2_g), ("ln2_b", ln2_b)):
            res[nm] = flat[o:o + ref.size].reshape(ref.shape)
            o += ref.size
        o = n_rep_full * LANES
        res["attn_sinks"] = flat[o:o + attn_sinks.size].reshape(attn_sinks.shape)
        return res

    def mine_unpack(buf):
        flat = buf.reshape(-1)
        return {"conv_w": flat[:n_cw].reshape(conv_w.shape),
                "ffn_conv_w": flat[n_cw:n_cw + n_fw].reshape(ffn_conv_w.shape)}

    order = ["w_in", "w_pool", "pool_scale", "attn_sinks", "conv_w", "w_branch_a", "w_branch_b", "w_branch_c", "w_o",
             "ln1_g", "ln1_b", "w_up", "ffn_conv_w", "w_down", "ln2_g", "ln2_b"]
    results = [loss, grad_x]
    for kind in range(4):
        rep_k, mine_k = rep_unpack(rep_res[kind]), mine_unpack(mine_res[kind])
        for nm in order:
            if nm in out:
                results.append(out[nm][kind])
            elif nm in rep_k:
                results.append(rep_k[nm])
            else:
                results.append(mine_k[nm])
    return tuple(results)
```

```python
import jax
import jax.numpy as jnp
from jax import lax
from jax.experimental import pallas as pl
from jax.experimental.pallas import tpu as pltpu

F32 = jnp.float32
BF16 = jnp.bfloat16

HEAD_DIM = 64
GROUP = 4
WINDOW = 128
ROT_DIM = 16
ROPE_THETA = 500000.0
LN_EPS = 1e-5
MASK_VALUE = -1e30
ADAM_LR, ADAM_B1, ADAM_B2, ADAM_EPS, ADAM_WD, ADAM_STEP = 0.001, 0.9, 0.999, 1e-08, 0.01, 10

N_DEV = 8
LANES = 1024
HALO = 16
MESH = pl.DeviceIdType.MESH
VMEM_LIMIT = 56 * 1024 * 1024


def _div_tile(n, want, mult=8):
    for t in range(min(want, n) // mult * mult, 0, -mult):
        if n % t == 0:
            return t
    return n


def _cp(*sem):
    return pltpu.CompilerParams(dimension_semantics=sem, vmem_limit_bytes=VMEM_LIMIT)


def _coords():
    return lax.axis_index("x"), lax.axis_index("y"), lax.axis_index("c")


def _all_gather(xs, name):
    xs = list(xs) if isinstance(xs, (list, tuple)) else [xs]
    n = len(xs)

    def body(*refs):
        ag_refs = (refs[:n], refs[n:2 * n]) + tuple(refs[2 * n:])
        _ag_start(*ag_refs)
        _ag_finish(*ag_refs)

    res = pl.pallas_call(
        body, name=name,
        out_shape=[jax.ShapeDtypeStruct((N_DEV,) + a.shape, a.dtype) for a in xs],
        in_specs=[pl.BlockSpec(memory_space=pl.ANY)] * n,
        out_specs=[pl.BlockSpec(memory_space=pl.ANY)] * n,
        scratch_shapes=_ag_sems(n),
    )(*xs)
    return res if n > 1 else res[0]


def _ag_sems(n):
    return [pltpu.SemaphoreType.DMA((7 * n,)), pltpu.SemaphoreType.DMA((7 * n,)), pltpu.SemaphoreType.DMA((n,))]


def _ag_copies(x_refs, out_refs, send_sems, recv_sems, local_sems):
    x, y, c = _coords()
    me, sibling = (x, y, c), (x, y, 1 - c)
    chips = [(1 - x, y), (x, 1 - y), (1 - x, 1 - y)]
    per_array = []
    for a, (x_ref, out_ref) in enumerate(zip(x_refs, out_refs)):
        def slot(px, py, pc, out_ref=out_ref):
            return out_ref.at[4 * px + 2 * py + pc]

        def copy(k, block, to, src=None, a=a, slot=slot):
            return pltpu.make_async_remote_copy(
                src_ref=slot(*block) if src is None else src, dst_ref=slot(*block),
                send_sem=send_sems.at[7 * a + k], recv_sem=recv_sems.at[7 * a + k],
                device_id=to, device_id_type=MESH)

        mine = pltpu.make_async_copy(x_ref, slot(*me), local_sems.at[a])
        first = [copy(0, me, sibling, src=x_ref)]
        first += [copy(1 + j, me, (*chip, c), src=x_ref) for j, chip in enumerate(chips)]
        passed = [copy(4 + j, (*chip, c), sibling) for j, chip in enumerate(chips)]
        from_chips = [copy(1 + j, (*chip, c), me) for j, chip in enumerate(chips)]
        from_sibling = [copy(0, sibling, me)] + [copy(4 + j, (*chip, 1 - c), me) for j, chip in enumerate(chips)]
        per_array.append((mine, first, passed, from_chips, from_sibling))
    return per_array


def _ag_start(*refs):
    for mine, first, _, _, _ in _ag_copies(*refs):
        mine.start()
        for cp in first:
            cp.start()


def _ag_finish(*refs):
    per_array = _ag_copies(*refs)
    for j in range(3):
        for _, _, passed, from_chips, _ in per_array:
            from_chips[j].wait_recv()
            passed[j].start()
    for mine, first, passed, _, from_sibling in per_array:
        for cp in from_sibling:
            cp.wait_recv()
        for cp in first + passed:
            cp.wait_send()
        mine.wait()


def _rs_sibling(ps, name):
    n = len(ps)

    def body(*refs):
        swap_refs = (refs[:n], refs[n:2 * n], refs[2 * n], refs[2 * n + 1])
        _swap_start(*swap_refs)
        _swap_finish(*swap_refs)

    return pl.pallas_call(
        body, name=name,
        out_shape=[jax.ShapeDtypeStruct((4,) + p.shape[1:], p.dtype) for p in ps],
        in_specs=[pl.BlockSpec(memory_space=pl.ANY)] * n,
        out_specs=[pl.BlockSpec(memory_space=pl.ANY)] * n,
        scratch_shapes=_swap_sems(n),
    )(*ps)


def _swap_sems(n):
    return [pltpu.SemaphoreType.DMA((4 * n,)), pltpu.SemaphoreType.DMA((4 * n,))]


def _swap_copies(p_refs, out_refs, send_sems, recv_sems):
    x, y, c = _coords()
    return [pltpu.make_async_remote_copy(
        src_ref=p_ref.at[4 * (j // 2) + 2 * (j % 2) + (1 - c)], dst_ref=out_ref.at[j],
        send_sem=send_sems.at[4 * a + j], recv_sem=recv_sems.at[4 * a + j],
        device_id=(x, y, 1 - c), device_id_type=MESH)
        for a, (p_ref, out_ref) in enumerate(zip(p_refs, out_refs)) for j in range(4)]


def _swap_start(*refs):
    for cp in _swap_copies(*refs):
        cp.start()


def _swap_finish(*refs):
    copies = _swap_copies(*refs)
    for cp in copies:
        cp.wait_recv()
    for cp in copies:
        cp.wait_send()


def _hosted(kind, arrays):
    n = len(arrays)
    if kind == "gather":
        return _ag_start, _ag_finish, [(N_DEV,) + a.shape for a in arrays], _ag_sems(n)
    assert kind == "swap"
    return _swap_start, _swap_finish, [(4,) + a.shape[1:] for a in arrays], _swap_sems(n)


def _rs_chips(qs, name):
    n = len(qs)

    def body(*refs):
        rs_refs = (refs[:n], refs[n:2 * n], refs[2 * n], refs[2 * n + 1])
        _rs_chips_start(*rs_refs)
        _rs_chips_finish(*rs_refs)

    return pl.pallas_call(
        body, name=name,
        out_shape=[jax.ShapeDtypeStruct((3,) + q.shape[1:], q.dtype) for q in qs],
        in_specs=[pl.BlockSpec(memory_space=pl.ANY)] * n,
        out_specs=[pl.BlockSpec(memory_space=pl.ANY)] * n,
        scratch_shapes=_rs_sems(n),
    )(*qs)


def _rs_sems(n):
    return [pltpu.SemaphoreType.DMA((3 * n,)), pltpu.SemaphoreType.DMA((3 * n,))]


def _rs_chips_copies(q_refs, out_refs, send_sems, recv_sems):
    x, y, c = _coords()
    chips = [(1 - x, y), (x, 1 - y), (1 - x, 1 - y)]
    return [pltpu.make_async_remote_copy(
        src_ref=q_ref.at[2 * cx + cy], dst_ref=out_ref.at[k],
        send_sem=send_sems.at[3 * a + k], recv_sem=recv_sems.at[3 * a + k], device_id=(cx, cy, c),
        device_id_type=MESH)
        for a, (q_ref, out_ref) in enumerate(zip(q_refs, out_refs)) for k, (cx, cy) in enumerate(chips)]


def _rs_chips_start(*refs):
    for cp in _rs_chips_copies(*refs):
        cp.start()


def _rs_chips_finish(*refs):
    copies = _rs_chips_copies(*refs)
    for cp in copies:
        cp.wait_recv()
    for cp in copies:
        cp.wait_send()


def _sum_sibling(p, recv, my_c, name, tr=512):
    _, R, C = p.shape
    tr = _div_tile(R, tr, 16)

    def body(c_ref, p_ref, r_ref, o_ref):
        o_ref[...] = (p_ref[...].astype(F32) + r_ref[...].astype(F32)).astype(o_ref.dtype)

    grid_spec = pltpu.PrefetchScalarGridSpec(
        num_scalar_prefetch=1, grid=(4, R // tr),
        in_specs=[pl.BlockSpec((1, tr, C), lambda j, r, c_ref: (4 * (j // 2) + 2 * (j % 2) + c_ref[0], r, 0)),
                  pl.BlockSpec((1, tr, C), lambda j, r, c_ref: (j, r, 0))],
        out_specs=pl.BlockSpec((1, tr, C), lambda j, r, c_ref: (j, r, 0)))
    return pl.pallas_call(body, name=name, grid_spec=grid_spec,
                          out_shape=jax.ShapeDtypeStruct((4, R, C), p.dtype),
                          compiler_params=_cp("parallel", "parallel"))(my_c, p, recv)


def _sum_chips(q, recv, my_chip, name, tr=512):
    _, R, C = q.shape
    tr = _div_tile(R, tr, 16)

    def body(i_ref, q_ref, r_ref, o_ref):
        acc = q_ref[0].astype(F32)
        for k in range(3):
            acc = acc + r_ref[k].astype(F32)
        o_ref[...] = acc

    grid_spec = pltpu.PrefetchScalarGridSpec(
        num_scalar_prefetch=1, grid=(R // tr,),
        in_specs=[pl.BlockSpec((1, tr, C), lambda r, i_ref: (i_ref[0], r, 0)),
                  pl.BlockSpec((3, tr, C), lambda r, i_ref: (0, r, 0))],
        out_specs=pl.BlockSpec((tr, C), lambda r, i_ref: (r, 0)))
    return pl.pallas_call(body, name=name, grid_spec=grid_spec,
                          out_shape=jax.ShapeDtypeStruct((R, C), F32),
                          compiler_params=_cp("parallel"))(my_chip, q, recv)


def _small_reduce(g, n_rep, n_mine, inv_d, loss_row, name):
    _, R, C = g.shape

    def body(g_ref, rep_ref, mine_ref, loss_ref):
        x, y, c = _coords()
        start = pl.multiple_of(n_rep + (4 * x + 2 * y + c) * n_mine, 8)
        rep = g_ref[0, 0:n_rep, :]
        mine = g_ref[0, pl.ds(start, n_mine), :]
        sq = g_ref[0, loss_row:loss_row + 1, :]
        for d in range(1, N_DEV):
            rep = rep + g_ref[d, 0:n_rep, :]
            mine = mine + g_ref[d, pl.ds(start, n_mine), :]
            sq = sq + g_ref[d, loss_row:loss_row + 1, :]
        rep_ref[...] = rep
        mine_ref[...] = mine
        loss_ref[...] = (0.5 * inv_d) * jnp.sum(sq, axis=1, keepdims=True)

    return pl.pallas_call(
        body, name=name,
        out_shape=(jax.ShapeDtypeStruct((n_rep, C), F32), jax.ShapeDtypeStruct((n_mine, C), F32),
                   jax.ShapeDtypeStruct((1, 1), F32)),
        compiler_params=pltpu.CompilerParams(vmem_limit_bytes=VMEM_LIMIT),
    )(g)


def _mm(a, b, *, out_dtype, name, tm=512, tn=None, tk=None, add=None, add_scale=1.0, gather=None, swap=None):
    M, K = a.shape
    N = b.shape[1]
    tm = min(tm, M)
    tn = N if tn is None else tn
    tk = K if tk is None else tk
    nk = K // tk
    has_add = add is not None
    hosted = gather if gather is not None else swap
    has_ag = hosted is not None
    n_g = len(hosted) if has_ag else 0
    if has_ag:
        comm_start, comm_finish, comm_shapes, comm_sems = _hosted("gather" if gather is not None else "swap", hosted)
    n_i, n_j = M // tm, N // tn

    def body(*refs):
        a_ref, b_ref = refs[0], refs[1]
        add_ref = refs[2] if has_add else None
        n_in = 2 + has_add + n_g
        o_ref = refs[n_in]
        if has_ag:
            ag_refs = (refs[n_in - n_g:n_in], refs[n_in + 1:n_in + 1 + n_g]) + tuple(
                refs[n_in + 1 + n_g:n_in + 1 + n_g + len(comm_sems)])
            pid = (pl.program_id(0), pl.program_id(1), pl.program_id(2))

            @pl.when((pid[0] == 0) & (pid[1] == 0) & (pid[2] == 0))
            def _():
                comm_start(*ag_refs)

        part = jnp.dot(a_ref[...].astype(BF16), b_ref[...].astype(BF16), preferred_element_type=F32)

        def finish(r):
            if has_add:
                r = r + add_scale * add_ref[...].astype(F32)
            o_ref[...] = r.astype(out_dtype)

        if nk == 1:
            finish(part)
        else:
            acc_ref = refs[-1]
            k = pl.program_id(2)

            @pl.when(k == 0)
            def _():
                acc_ref[...] = part

            @pl.when(k > 0)
            def _():
                acc_ref[...] += part

            @pl.when(k == nk - 1)
            def _():
                finish(acc_ref[...])

        if has_ag:
            @pl.when((pid[0] == n_i - 1) & (pid[1] == n_j - 1) & (pid[2] == nk - 1))
            def _():
                comm_finish(*ag_refs)

    b_mode = dict(pipeline_mode=pl.Buffered(1)) if (n_j == 1 and nk == 1) else {}
    in_specs = [pl.BlockSpec((tm, tk), lambda i, j, k: (i, k)),
                pl.BlockSpec((tk, tn), lambda i, j, k: (k, j), **b_mode)]
    args = [a, b]
    if has_add:
        in_specs.append(pl.BlockSpec((tm, tn), lambda i, j, k: (i, j)))
        args.append(add)
    out_specs = [pl.BlockSpec((tm, tn), lambda i, j, k: (i, j))]
    out_shape = [jax.ShapeDtypeStruct((M, N), out_dtype)]
    scratch = []
    if has_ag:
        in_specs += [pl.BlockSpec(memory_space=pl.ANY)] * n_g
        args += list(hosted)
        out_specs += [pl.BlockSpec(memory_space=pl.ANY)] * n_g
        out_shape += [jax.ShapeDtypeStruct(s, g.dtype) for s, g in zip(comm_shapes, hosted)]
        scratch += comm_sems
    if nk > 1:
        scratch.append(pltpu.VMEM((tm, tn), F32))
    sem = ("arbitrary",) * 3 if has_ag else ("parallel", "parallel", "arbitrary")
    res = pl.pallas_call(
        body, name=name, grid=(n_i, n_j, nk), in_specs=in_specs, out_specs=out_specs, out_shape=out_shape,
        scratch_shapes=scratch, compiler_params=_cp(*sem),
    )(*args)
    return (res[0], list(res[1:])) if has_ag else res[0]


def _mm_fan(a, bs, *, out_dtype, name, tm=512, gather=None):
    M, K = a.shape
    tm = min(tm, M)
    n = len(bs)
    n_i = M // tm
    n_g = len(gather) if gather is not None else 0

    def body(*refs):
        outs = refs[1 + n + n_g:1 + 2 * n + n_g]
        if n_g:
            ag_refs = (refs[1 + n:1 + n + n_g], refs[1 + 2 * n + n_g:1 + 2 * n + 2 * n_g]) + tuple(
                refs[1 + 2 * n + 2 * n_g:])

            @pl.when(pl.program_id(0) == 0)
            def _():
                _ag_start(*ag_refs)

        a_v = refs[0][...].astype(BF16)
        for k in range(n):
            outs[k][...] = jnp.dot(a_v, refs[1 + k][...].astype(BF16), preferred_element_type=F32).astype(out_dtype)

        if n_g:
            @pl.when(pl.program_id(0) == n_i - 1)
            def _():
                _ag_finish(*ag_refs)

    row = lambda i: (i, 0)
    hbm = pl.BlockSpec(memory_space=pl.ANY)
    res = pl.pallas_call(
        body, name=name, grid=(n_i,),
        in_specs=[pl.BlockSpec((tm, K), row)] + [pl.BlockSpec(b.shape, lambda i: (0, 0)) for b in bs] + [hbm] * n_g,
        out_specs=[pl.BlockSpec((tm, b.shape[1]), row) for b in bs] + [hbm] * n_g,
        out_shape=([jax.ShapeDtypeStruct((M, b.shape[1]), out_dtype) for b in bs]
                   + [jax.ShapeDtypeStruct((N_DEV,) + g.shape, g.dtype) for g in (gather or [])]),
        scratch_shapes=_ag_sems(n_g) if n_g else [],
        compiler_params=_cp("arbitrary" if n_g else "parallel"),
    )(a, *bs, *(gather or []))
    return (list(res[:n]), list(res[n:])) if n_g else list(res)


def _mm_sum(xs, bs, add, *, add_scale, name, tm=512, ln=None, scatter=None):
    M = xs[0].shape[0]
    N = bs[0].shape[1]
    tm = min(tm, M)
    n = len(xs)
    n_i = M // tm
    n_s = len(scatter) if scatter is not None else 0
    assert not (n_s and ln is not None)

    def body(*refs):
        if n_s:
            rs_refs = (refs[2 * n + 1:2 * n + 1 + n_s], refs[2 * n + 2 + n_s:2 * n + 2 + 2 * n_s],
                       refs[2 * n + 2 + 2 * n_s], refs[2 * n + 3 + 2 * n_s])

            @pl.when(pl.program_id(0) == 0)
            def _():
                _rs_chips_start(*rs_refs)

        acc = add_scale * refs[2 * n][...]
        for k in range(n):
            acc = acc + jnp.dot(refs[k][...].astype(BF16), refs[n + k][...].astype(BF16), preferred_element_type=F32)
        if ln is None:
            refs[2 * n + 1 + n_s][...] = acc
        else:
            xh_ref, rs_ref, g_ref, dz_ref, dg_ref, db_ref = refs[2 * n + 1:]
            _ln_bwd_tile(acc, xh_ref, rs_ref, g_ref, dz_ref, dg_ref, db_ref, pl.program_id(0) == 0)

        if n_s:
            @pl.when(pl.program_id(0) == n_i - 1)
            def _():
                _rs_chips_finish(*rs_refs)

    row = lambda i: (i, 0)
    vec = lambda i: (0, 0)
    hbm = pl.BlockSpec(memory_space=pl.ANY)
    in_specs = ([pl.BlockSpec((tm, x.shape[1]), row) for x in xs]
                + [pl.BlockSpec(b.shape, vec) for b in bs] + [pl.BlockSpec((tm, N), row)])
    if ln is None:
        res = pl.pallas_call(
            body, name=name, grid=(n_i,), in_specs=in_specs + [hbm] * n_s,
            out_specs=[pl.BlockSpec((tm, N), row)] + [hbm] * n_s,
            out_shape=([jax.ShapeDtypeStruct((M, N), F32)]
                       + [jax.ShapeDtypeStruct((3,) + q.shape[1:], q.dtype) for q in (scatter or [])]),
            scratch_shapes=_rs_sems(n_s) if n_s else [],
            compiler_params=_cp("arbitrary" if n_s else "parallel"),
        )(*xs, *bs, add, *(scatter or []))
        return (res[0], list(res[1:])) if n_s else res[0]
    in_specs += [pl.BlockSpec((tm, N), row), pl.BlockSpec((tm, 1), row), pl.BlockSpec((1, N), vec)]
    return pl.pallas_call(
        body, name=name, grid=(M // tm,), in_specs=in_specs,
        out_specs=[pl.BlockSpec((tm, N), row), pl.BlockSpec((1, N), vec), pl.BlockSpec((1, N), vec)],
        out_shape=(jax.ShapeDtypeStruct((M, N), F32), jax.ShapeDtypeStruct((1, N), F32),
                   jax.ShapeDtypeStruct((1, N), F32)),
        compiler_params=_cp("arbitrary"),
    )(*xs, *bs, add, *ln)


def _ln_bwd_tile(dyv, xh_ref, rs_ref, g_ref, dz_ref, dg_ref, db_ref, first):
    @pl.when(first)
    def _():
        dg_ref[...] = jnp.zeros_like(dg_ref)
        db_ref[...] = jnp.zeros_like(db_ref)

    xh = xh_ref[...].astype(F32)
    dyg = dyv * g_ref[...]
    c1 = jnp.mean(dyg, axis=-1, keepdims=True)
    c2 = jnp.mean(dyg * xh, axis=-1, keepdims=True)
    dz_ref[...] = rs_ref[...] * (dyg - c1 - xh * c2)
    dg_ref[...] += jnp.sum(dyv * xh, axis=0, keepdims=True)
    db_ref[...] += jnp.sum(dyv, axis=0, keepdims=True)


def _mm_ln(a, b, resid, gamma, beta, *, alpha, name, tm=512, tk=None):
    M, K = a.shape
    D = b.shape[1]
    tm = min(tm, M)
    tk = K if tk is None else tk
    nk = K // tk

    def body(a_ref, b_ref, r_ref, g_ref, be_ref, y_ref, xh_ref, rs_ref, *scratch):
        part = jnp.dot(a_ref[...].astype(BF16), b_ref[...].astype(BF16), preferred_element_type=F32)

        def finish(acc):
            z = alpha * r_ref[...] + acc
            mu = jnp.mean(z, axis=-1, keepdims=True)
            zc = z - mu
            var = jnp.mean(zc * zc, axis=-1, keepdims=True)
            rstd = lax.rsqrt(var + LN_EPS)
            xhat = zc * rstd
            y_ref[...] = xhat * g_ref[...] + be_ref[...]
            xh_ref[...] = xhat.astype(BF16)
            rs_ref[...] = rstd

        if nk == 1:
            finish(part)
        else:
            acc_ref = scratch[0]
            k = pl.program_id(1)

            @pl.when(k == 0)
            def _():
                acc_ref[...] = part

            @pl.when(k > 0)
            def _():
                acc_ref[...] += part

            @pl.when(k == nk - 1)
            def _():
                finish(acc_ref[...])

    row = lambda i, k: (i, 0)
    vec = lambda i, k: (0, 0)
    return pl.pallas_call(
        body, name=name, grid=(M // tm, nk),
        in_specs=[pl.BlockSpec((tm, tk), lambda i, k: (i, k)), pl.BlockSpec((tk, D), lambda i, k: (k, 0)),
                  pl.BlockSpec((tm, D), row), pl.BlockSpec((1, D), vec), pl.BlockSpec((1, D), vec)],
        out_specs=[pl.BlockSpec((tm, D), row), pl.BlockSpec((tm, D), row), pl.BlockSpec((tm, 1), row)],
        out_shape=(jax.ShapeDtypeStruct((M, D), F32), jax.ShapeDtypeStruct((M, D), BF16),
                   jax.ShapeDtypeStruct((M, 1), F32)),
        scratch_shapes=[pltpu.VMEM((tm, D), F32)] if nk > 1 else [],
        compiler_params=_cp("parallel", "arbitrary"),
    )(a, b, resid, gamma, beta)


def _mm_tn(a, b, *, name, tka, tn, a_off=0, na=1, b_off=0, nb=1, ts=2048, out_dtype=F32):
    S = a.shape[0]
    ts = min(ts, S)
    ns = S // ts
    direct = out_dtype == F32

    def body(a_ref, b_ref, o_ref, *scratch):
        acc_ref = o_ref if direct else scratch[0]
        s = pl.program_id(2)
        part = lax.dot_general(a_ref[...].astype(BF16), b_ref[...].astype(BF16),
                               (((0,), (0,)), ((), ())), preferred_element_type=F32)

        @pl.when(s == 0)
        def _():
            acc_ref[...] = part

        @pl.when(s > 0)
        def _():
            acc_ref[...] += part

        if not direct:
            @pl.when(s == ns - 1)
            def _():
                o_ref[...] = acc_ref[...].astype(out_dtype)

    return pl.pallas_call(
        body, name=name, grid=(na, nb, ns),
        in_specs=[pl.BlockSpec((ts, tka), lambda i, j, s: (s, a_off + i)),
                  pl.BlockSpec((ts, tn), lambda i, j, s: (s, b_off + j))],
        out_specs=pl.BlockSpec((tka, tn), lambda i, j, s: (i, j)),
        out_shape=jax.ShapeDtypeStruct((na * tka, nb * tn), out_dtype),
        scratch_shapes=[] if direct else [pltpu.VMEM((tka, tn), F32)],
        compiler_params=_cp("parallel", "parallel", "arbitrary"),
    )(a, b)


def _rope_tables(pos, inv_lane, sign_lane, name, ts=512):
    S = pos.shape[0]
    ts = min(ts, S)

    def body(p_ref, inv_ref, sg_ref, cos_ref, sin_ref):
        ang = p_ref[...].astype(F32) * inv_ref[...]
        cos_ref[...] = jnp.cos(ang)
        sin_ref[...] = jnp.sin(ang) * sg_ref[...]

    return pl.pallas_call(
        body, name=name, grid=(S // ts,),
        in_specs=[pl.BlockSpec((ts, 1), lambda i: (i, 0)), pl.BlockSpec((1, 128), lambda i: (0, 0)),
                  pl.BlockSpec((1, 128), lambda i: (0, 0))],
        out_specs=[pl.BlockSpec((ts, 128), lambda i: (i, 0))] * 2,
        out_shape=(jax.ShapeDtypeStruct((S, 128), F32),) * 2,
        compiler_params=_cp("parallel"),
    )(pos, inv_lane, sign_lane)


def _rope_swap(t):
    lane = lax.broadcasted_iota(jnp.int32, (1, 128), 1)
    lo = (lane % HEAD_DIM) < (ROT_DIM // 2)
    return jnp.where(lo, pltpu.roll(t, 128 - ROT_DIM // 2, 1), pltpu.roll(t, ROT_DIM // 2, 1))


def _rope_fwd(t, cos, sin):
    return t * cos + _rope_swap(t) * sin


def _rope_bwd(d, cos, sin):
    lane = lax.broadcasted_iota(jnp.int32, (1, 128), 1)
    return d * cos + jnp.where((lane % HEAD_DIM) < ROT_DIM, _rope_swap(d * sin), 0.0)


def _tile_heads(t):
    lane = lax.broadcasted_iota(jnp.int32, (1, 128), 1)
    r = pltpu.roll(t, 64, 1)
    h0 = jnp.where(lane < 64, t, r)
    h1 = jnp.where(lane < 64, r, t)
    return jnp.concatenate([h0, h0], axis=1), jnp.concatenate([h1, h1], axis=1)


def _fold_heads(d0, d1):
    lane = lax.broadcasted_iota(jnp.int32, (1, 128), 1)

    def fold(d):
        s = d[:, 0:128] + d[:, 128:256]
        return s + pltpu.roll(s, 64, 1)

    return jnp.where(lane < 64, fold(d0), fold(d1))


def _band4(n_keys):
    row = lax.broadcasted_iota(jnp.int32, (GROUP * WINDOW, n_keys), 0) % WINDOW
    col = lax.broadcasted_iota(jnp.int32, (GROUP * WINDOW, n_keys), 1)
    return (col > row) & (col <= row + WINDOW), col


def _head_masks():
    lane = lax.broadcasted_iota(jnp.int32, (1, GROUP * HEAD_DIM), 1)
    return [(lane // HEAD_DIM) == hl for hl in range(GROUP)]


def _stack_heads(t):
    zero = jnp.zeros_like(t)
    return jnp.concatenate([jnp.where(hm, t, zero) for hm in _head_masks()], axis=0)


def _unstack_heads(t4):
    out = None
    for hl, hm in enumerate(_head_masks()):
        part = jnp.where(hm, t4[hl * WINDOW:(hl + 1) * WINDOW], 0.0)
        out = part if out is None else out + part
    return out


def _sink_block(sink_ref, g):
    return jnp.concatenate([jnp.broadcast_to(sink_ref[g * GROUP + hl:g * GROUP + hl + 1, 0:1], (WINDOW, 256))
                            for hl in range(GROUP)], axis=0)


def _sink_column(sink_ref, g):
    return jnp.concatenate([jnp.broadcast_to(sink_ref[g * GROUP + hl:g * GROUP + hl + 1, 0:1], (WINDOW, 1))
                            for hl in range(GROUP)], axis=0)


def _attn_fwd(pq, cos_t, sin_t, sinks_b, *, name, ts=1024):
    S = pq.shape[0]
    ts = min(ts, S)
    nq = ts // WINDOW
    scale = HEAD_DIM ** -0.5

    def body(cur_ref, prev_ref, cosc_ref, sinc_ref, cosp_ref, sinp_ref, sink_ref, o_ref, lse_ref):
        i = pl.program_id(0)
        cosc, sinc = cosc_ref[...], sinc_ref[...]
        q = cur_ref[:, 0:512].astype(F32)
        qr = jnp.concatenate(
            [_rope_fwd(q[:, j * 128:(j + 1) * 128], cosc, sinc) for j in range(4)], axis=1) * scale
        qr = qr.astype(BF16)
        kc = _rope_fwd(cur_ref[:, 512:640].astype(F32), cosc, sinc)
        kp = _rope_fwd(prev_ref[:, 0:128].astype(F32), cosp_ref[...], sinp_ref[...])
        k_all = jnp.concatenate([kp, kc], axis=0)
        v_all = jnp.concatenate([prev_ref[:, 128:256].astype(F32), cur_ref[:, 640:768].astype(F32)], axis=0)
        kt = [t.astype(BF16) for t in _tile_heads(k_all)]
        vt = [t.astype(BF16) for t in _tile_heads(v_all)]
        band, col = _band4(2 * WINDOW)
        ones = jnp.ones((2 * WINDOW, 256), BF16)
        key_t = lax.broadcasted_iota(jnp.int32, (2 * WINDOW, GROUP * WINDOW), 0)
        qry_t = lax.broadcasted_iota(jnp.int32, (2 * WINDOW, GROUP * WINDOW), 1) % WINDOW
        band_t = (key_t > qry_t) & (key_t <= qry_t + WINDOW)
        NT = (((1,), (1,)), ((), ()))
        for qb in range(nq):
            rows = slice(qb * WINDOW, (qb + 1) * WINDOW)
            keys = slice(qb * WINDOW, (qb + 2) * WINDOW)
            valid = band & ((col >= WINDOW) | (i * nq + qb > 0))
            valid_t = band_t & ((key_t >= WINDOW) | (i * nq + qb > 0))
            for g in range(2):
                qs = _stack_heads(qr[rows, g * 256:(g + 1) * 256])
                sink = _sink_block(sink_ref, g)
                s = lax.dot_general(qs, kt[g][keys], NT, preferred_element_type=F32)
                s_t = lax.dot_general(kt[g][keys], qs, NT, preferred_element_type=F32)
                m_t = jnp.max(jnp.where(valid_t, s_t, MASK_VALUE), axis=0, keepdims=True)
                m_rep = jnp.broadcast_to(m_t, (WINDOW, GROUP * WINDOW)).T
                m = jnp.maximum(jnp.concatenate([m_rep, m_rep], axis=1), sink)
                e = jnp.exp(jnp.where(valid, s, MASK_VALUE) - m).astype(BF16)
                l = jnp.dot(e, ones, preferred_element_type=F32) + jnp.exp(sink - m)
                pv = jnp.dot(e, vt[g][keys], preferred_element_type=F32)
                o_ref[rows, g * 256:(g + 1) * 256] = (_unstack_heads(pv) / _unstack_heads(l)).astype(BF16)
                lse4 = (m + jnp.log(l))[:, 0:1]
                for hl in range(GROUP):
                    h = g * GROUP + hl
                    lse_ref[rows, h:h + 1] = lse4[hl * WINDOW:(hl + 1) * WINDOW]

    hb = ts // WINDOW
    cur = lambda i: (i, 0)
    prev = lambda i: (jnp.maximum(i * hb - 1, 0), 0)
    return pl.pallas_call(
        body, name=name, grid=(S // ts,),
        in_specs=[pl.BlockSpec((ts, 768), cur),
                  pl.BlockSpec((WINDOW, 256), lambda i: (jnp.maximum(i * hb - 1, 0), 2)),
                  pl.BlockSpec((ts, 128), cur), pl.BlockSpec((ts, 128), cur),
                  pl.BlockSpec((WINDOW, 128), prev), pl.BlockSpec((WINDOW, 128), prev),
                  pl.BlockSpec((8, 128), lambda i: (0, 0))],
        out_specs=[pl.BlockSpec((ts, 512), cur), pl.BlockSpec((ts, 8), cur)],
        out_shape=(jax.ShapeDtypeStruct((S, 512), BF16), jax.ShapeDtypeStruct((S, 8), F32)),
        compiler_params=_cp("parallel"),
    )(pq, pq, cos_t, sin_t, cos_t, sin_t, sinks_b)


def _attn_bwd(pq, cos_t, sin_t, sinks_b, do, o, lse, *, name, ts=1024):
    S = pq.shape[0]
    ts = min(ts, S)
    nq = ts // WINDOW
    nt = S // ts
    scale = HEAD_DIM ** -0.5
    NT = (((1,), (1,)), ((), ()))
    TN = (((0,), (0,)), ((), ()))

    def body(cur_ref, prev_ref, nxt_ref, cosc_ref, sinc_ref, cosp_ref, sinp_ref, cosn_ref, sinn_ref, sink_ref,
             doc_ref, don_ref, oc_ref, on_ref, lsec_ref, lsen_ref, dpq_ref, dsink_ref):
        i = pl.program_id(0)
        last = i == nt - 1
        cosc, sinc = cosc_ref[...], sinc_ref[...]
        cose = jnp.concatenate([cosc, cosn_ref[...]], axis=0)
        sine = jnp.concatenate([sinc, sinn_ref[...]], axis=0)
        q = jnp.concatenate([cur_ref[:, 0:512], nxt_ref[:, 0:512]], axis=0).astype(F32)
        qr = jnp.concatenate(
            [_rope_fwd(q[:, j * 128:(j + 1) * 128], cose, sine) for j in range(4)], axis=1) * scale
        qr = qr.astype(BF16)
        kc = _rope_fwd(cur_ref[:, 512:640].astype(F32), cosc, sinc)
        kp = _rope_fwd(prev_ref[:, 0:128].astype(F32), cosp_ref[...], sinp_ref[...])
        k_all = jnp.concatenate([kp, kc], axis=0)
        v_all = jnp.concatenate([prev_ref[:, 128:256].astype(F32), cur_ref[:, 640:768].astype(F32)], axis=0)
        kt = [t.astype(BF16) for t in _tile_heads(k_all)]
        vt = [t.astype(BF16) for t in _tile_heads(v_all)]
        don = jnp.where(last, jnp.zeros_like(don_ref[...]), don_ref[...])
        do_e = jnp.concatenate([doc_ref[...], don], axis=0)
        o_e = jnp.concatenate([oc_ref[...], on_ref[...]], axis=0)
        band2, col2 = _band4(2 * WINDOW)
        band1, _ = _band4(WINDOW)
        ones = jnp.ones((256, 256), BF16)

        @pl.when(i == 0)
        def _():
            dsink_ref[...] = jnp.zeros_like(dsink_ref)

        dk_acc = [[None] * (nq + 1) for _ in range(2)]
        dv_acc = [[None] * (nq + 1) for _ in range(2)]

        def add(acc, g, e, val):
            acc[g][e] = val if acc[g][e] is None else acc[g][e] + val

        for qb in range(nq + 1):
            halo = qb == nq
            rows = slice(qb * WINDOW, (qb + 1) * WINDOW)
            if halo:
                keys = slice(qb * WINDOW, (qb + 1) * WINDOW)
                valid = band1 & jnp.logical_not(last)
            else:
                keys = slice(qb * WINDOW, (qb + 2) * WINDOW)
                valid = band2 & ((col2 >= WINDOW) | (i * nq + qb > 0))
            dq_parts = []
            for g in range(2):
                qs = _stack_heads(qr[rows, g * 256:(g + 1) * 256])
                dos = _stack_heads(do_e[rows, g * 256:(g + 1) * 256])
                o_g = o_e[rows, g * 256:(g + 1) * 256].astype(F32)
                kt_b, vt_b = kt[g][keys], vt[g][keys]
                lse_src = lsen_ref if halo else lsec_ref
                lse_rows = slice(0, WINDOW) if halo else rows
                big_l = jnp.concatenate([lse_src[lse_rows, g * GROUP + hl:g * GROUP + hl + 1] for hl in range(GROUP)],
                                        axis=0)
                delta = jnp.dot((dos.astype(F32) * jnp.concatenate([o_g] * GROUP, axis=0)).astype(BF16), ones,
                                preferred_element_type=F32)
                s = lax.dot_general(qs, kt_b, NT, preferred_element_type=F32)
                p = jnp.exp(jnp.where(valid, s, MASK_VALUE) - big_l)
                dp = lax.dot_general(dos, vt_b, NT, preferred_element_type=F32)
                ds = (p * (dp - delta[:, 0:p.shape[1]])).astype(BF16)
                dk_g = lax.dot_general(ds, qs, TN, preferred_element_type=F32)
                dv_g = lax.dot_general(p.astype(BF16), dos, TN, preferred_element_type=F32)
                if not halo:
                    dq_parts.append(_unstack_heads(jnp.dot(ds, kt_b, preferred_element_type=F32)))
                    dsink4 = jnp.exp(_sink_column(sink_ref, g) - big_l) * delta[:, 0:1]
                    for hl in range(GROUP):
                        h = g * GROUP + hl
                        dsink_h = -jnp.sum(dsink4[hl * WINDOW:(hl + 1) * WINDOW], axis=0, keepdims=True)
                        dsink_ref[h:h + 1, :] += jnp.broadcast_to(dsink_h, (1, 128))
                add(dk_acc, g, qb, dk_g[0:WINDOW])
                add(dv_acc, g, qb, dv_g[0:WINDOW])
                if not halo:
                    add(dk_acc, g, qb + 1, dk_g[WINDOW:2 * WINDOW])
                    add(dv_acc, g, qb + 1, dv_g[WINDOW:2 * WINDOW])
            if not halo:
                cs, sn = cosc[rows], sinc[rows]
                for g in range(2):
                    dq_g = dq_parts[g] * scale
                    for j in range(2):
                        c0 = g * 256 + j * 128
                        dpq_ref[rows, c0:c0 + 128] = _rope_bwd(dq_g[:, j * 128:(j + 1) * 128], cs, sn).astype(BF16)
        for e in range(1, nq + 1):
            rows = slice((e - 1) * WINDOW, e * WINDOW)
            dk = _fold_heads(dk_acc[0][e], dk_acc[1][e])
            dv = _fold_heads(dv_acc[0][e], dv_acc[1][e])
            dpq_ref[rows, 512:640] = _rope_bwd(dk, cosc[rows], sinc[rows]).astype(BF16)
            dpq_ref[rows, 640:768] = dv.astype(BF16)

    hb = ts // WINDOW
    nblk = S // WINDOW
    cur = lambda i: (i, 0)
    prev = lambda i: (jnp.maximum(i * hb - 1, 0), 0)
    nxt = lambda i: (jnp.minimum((i + 1) * hb, nblk - 1), 0)
    return pl.pallas_call(
        body, name=name, grid=(nt,),
        in_specs=[pl.BlockSpec((ts, 768), cur),
                  pl.BlockSpec((WINDOW, 256), lambda i: (jnp.maximum(i * hb - 1, 0), 2)),
                  pl.BlockSpec((WINDOW, 768), nxt),
                  pl.BlockSpec((ts, 128), cur), pl.BlockSpec((ts, 128), cur),
                  pl.BlockSpec((WINDOW, 128), prev), pl.BlockSpec((WINDOW, 128), prev),
                  pl.BlockSpec((WINDOW, 128), nxt), pl.BlockSpec((WINDOW, 128), nxt),
                  pl.BlockSpec((8, 128), lambda i: (0, 0)),
                  pl.BlockSpec((ts, 512), cur), pl.BlockSpec((WINDOW, 512), nxt),
                  pl.BlockSpec((ts, 512), cur), pl.BlockSpec((WINDOW, 512), nxt),
                  pl.BlockSpec((ts, 8), cur), pl.BlockSpec((WINDOW, 8), nxt)],
        out_specs=[pl.BlockSpec((ts, 768), cur), pl.BlockSpec((8, 128), lambda i: (0, 0))],
        out_shape=(jax.ShapeDtypeStruct((S, 768), BF16), jax.ShapeDtypeStruct((8, 128), F32)),
        compiler_params=_cp("arbitrary"),
    )(pq, pq, pq, cos_t, sin_t, cos_t, sin_t, cos_t, sin_t, sinks_b, do, do, o, o, lse, lse)


def _shift_dn(x, k):
    return pltpu.roll(x, k, 0)


def _shift_up(x, k):
    return pltpu.roll(x, x.shape[0] - k, 0)


def _pool_lane_select(vals):
    lane = lax.broadcasted_iota(jnp.int32, (1, 256), 1)
    out = vals[3]
    for g in (2, 1, 0):
        out = jnp.where(lane < 64 * (g + 1), vals[g], out)
    return out


def _pool_inv_count(t0, n):
    t = t0 + lax.broadcasted_iota(jnp.int32, (n, 256), 0)
    lane = lax.broadcasted_iota(jnp.int32, (n, 256), 1)
    w = jnp.where(lane < 64, 2, jnp.where(lane < 128, 4, jnp.where(lane < 192, 8, 16)))
    return 1.0 / jnp.minimum(t + 1, w).astype(F32)


def _pooled(u_ext, t0, n):
    s2 = u_ext + _shift_dn(u_ext, 1)
    s4 = s2 + _shift_dn(s2, 2)
    s8 = s4 + _shift_dn(s4, 4)
    s16 = s8 + _shift_dn(s8, 8)
    win = _pool_lane_select([s2, s4, s8, s16])[HALO:HALO + n]
    return win * _pool_inv_count(t0, n) - u_ext[HALO:HALO + n]


def _poolconv_fwd(pp, wbd, pool_scale, conv_w, *, name, ts=512):
    S = pp.shape[0]
    ts = min(ts, S)

    def body(cur_ref, prev_ref, wbd_ref, sc_ref, cw_ref, oa_ref, oc_ref):
        i = pl.program_id(0)
        prev = jnp.where(i > 0, prev_ref[...].astype(F32), 0.0)
        u_ext = jnp.concatenate([prev[:, 0:256], cur_ref[:, 0:256].astype(F32)], axis=0)
        pooled = _pooled(u_ext, i * ts, ts)
        mixed = jnp.dot(pooled.astype(BF16), wbd_ref[...], preferred_element_type=F32)
        oa_ref[...] = (mixed * sc_ref[...]).astype(BF16)
        v_ext = jnp.concatenate([prev[:, 256:512] * prev[:, 768:1024],
                                 cur_ref[:, 256:512].astype(F32) * cur_ref[:, 768:1024].astype(F32)], axis=0)
        cv = cw_ref[2:3, :] * v_ext + cw_ref[1:2, :] * _shift_dn(v_ext, 1) + cw_ref[0:1, :] * _shift_dn(v_ext, 2)
        oc_ref[...] = (cur_ref[:, 512:768].astype(F32) * cv[HALO:HALO + ts]).astype(BF16)

    hb = ts // HALO
    cur = lambda i: (i, 0)
    const = lambda i: (0, 0)
    return pl.pallas_call(
        body, name=name, grid=(S // ts,),
        in_specs=[pl.BlockSpec((ts, 1024), cur),
                  pl.BlockSpec((HALO, 1024), lambda i: (jnp.maximum(i * hb - 1, 0), 0)),
                  pl.BlockSpec((256, 256), const), pl.BlockSpec((1, 256), const), pl.BlockSpec((3, 256), const)],
        out_specs=[pl.BlockSpec((ts, 256), cur)] * 2,
        out_shape=(jax.ShapeDtypeStruct((S, 256), BF16),) * 2,
        compiler_params=_cp("parallel"),
    )(pp, pp, wbd, pool_scale, conv_w)


def _poolconv_bwd(pp, do_a, do_c, wbd, wbd_t, pool_scale, conv_w, *, name, ts=512):
    S = pp.shape[0]
    ts = min(ts, S)
    nt = S // ts
    n_e = ts + 2 * HALO

    def body(cur_ref, prev_ref, nxt_ref, dac_ref, dan_ref, dcc_ref, dcn_ref, wbd_ref, wbdt_ref, sc_ref, cw_ref,
             dpp_ref, pooled_ref, dmixed_ref, dsc_ref, dcw_ref):
        i = pl.program_id(0)

        @pl.when(i == 0)
        def _():
            dsc_ref[...] = jnp.zeros_like(dsc_ref)
            dcw_ref[...] = jnp.zeros_like(dcw_ref)

        prev = jnp.where(i > 0, prev_ref[...].astype(F32), 0.0)
        nxt = nxt_ref[...].astype(F32)
        cur = cur_ref[...].astype(F32)
        not_last = i < nt - 1
        da_n = jnp.where(not_last, dan_ref[...].astype(F32), 0.0)
        dc_n = jnp.where(not_last, dcn_ref[...].astype(F32), 0.0)
        zeros_h = jnp.zeros((HALO, 256), F32)
        sc = sc_ref[...]

        u_ext = jnp.concatenate([prev[:, 0:256], cur[:, 0:256]], axis=0)
        pooled = _pooled(u_ext, i * ts, ts)
        pooled_b = pooled.astype(BF16)
        pooled_ref[...] = pooled_b
        mixed = jnp.dot(pooled_b, wbd_ref[...], preferred_element_type=F32)
        da_c = dac_ref[...].astype(F32)
        dsc_ref[...] += jnp.sum(da_c * mixed, axis=0, keepdims=True)
        dmixed_e = jnp.concatenate([da_c, da_n], axis=0) * sc
        dmixed_ref[...] = dmixed_e[0:ts].astype(BF16)
        dpooled = jnp.dot(dmixed_e.astype(BF16), wbdt_ref[...], preferred_element_type=F32)
        qd = dpooled * _pool_inv_count(i * ts, ts + HALO)
        f2 = qd + _shift_up(qd, 1)
        f4 = f2 + _shift_up(f2, 2)
        f8 = f4 + _shift_up(f4, 4)
        f16 = f8 + _shift_up(f8, 8)
        du = (_pool_lane_select([f2, f4, f8, f16]) - dpooled)[0:ts]
        dpp_ref[:, 0:256] = du.astype(BF16)

        xc_e = jnp.concatenate([prev[:, 256:512], cur[:, 256:512], nxt[:, 256:512]], axis=0)
        gc_e = jnp.concatenate([prev[:, 768:1024], cur[:, 768:1024], nxt[:, 768:1024]], axis=0)
        gb_e = jnp.concatenate([zeros_h, cur[:, 512:768], nxt[:, 512:768]], axis=0)
        dc_e = jnp.concatenate([zeros_h, dcc_ref[...].astype(F32), dc_n], axis=0)
        v_e = xc_e * gc_e
        v1, v2 = _shift_dn(v_e, 1), _shift_dn(v_e, 2)
        w0, w1, w2 = cw_ref[0:1, :], cw_ref[1:2, :], cw_ref[2:3, :]
        cv = w2 * v_e + w1 * v1 + w0 * v2
        dcv = dc_e * gb_e
        dv = w2 * dcv + w1 * _shift_up(dcv, 1) + w0 * _shift_up(dcv, 2)
        tile = slice(HALO, HALO + ts)
        dpp_ref[:, 256:512] = (dv * gc_e)[tile].astype(BF16)
        dpp_ref[:, 512:768] = (dc_e * cv)[tile].astype(BF16)
        dpp_ref[:, 768:1024] = (dv * xc_e)[tile].astype(BF16)
        dcv_t = dcv[tile]
        dcw_ref[0:1, :] += jnp.sum(dcv_t * v2[tile], axis=0, keepdims=True)
        dcw_ref[1:2, :] += jnp.sum(dcv_t * v1[tile], axis=0, keepdims=True)
        dcw_ref[2:3, :] += jnp.sum(dcv_t * v_e[tile], axis=0, keepdims=True)

    hb = ts // HALO
    nblk = S // HALO
    cur = lambda i: (i, 0)
    const = lambda i: (0, 0)
    prev = lambda i: (jnp.maximum(i * hb - 1, 0), 0)
    nxt = lambda i: (jnp.minimum((i + 1) * hb, nblk - 1), 0)
    del n_e
    return pl.pallas_call(
        body, name=name, grid=(nt,),
        in_specs=[pl.BlockSpec((ts, 1024), cur), pl.BlockSpec((HALO, 1024), prev), pl.BlockSpec((HALO, 1024), nxt),
                  pl.BlockSpec((ts, 256), cur), pl.BlockSpec((HALO, 256), nxt),
                  pl.BlockSpec((ts, 256), cur), pl.BlockSpec((HALO, 256), nxt),
                  pl.BlockSpec((256, 256), const), pl.BlockSpec((256, 256), const),
                  pl.BlockSpec((1, 256), const), pl.BlockSpec((3, 256), const)],
        out_specs=[pl.BlockSpec((ts, 1024), cur), pl.BlockSpec((ts, 256), cur), pl.BlockSpec((ts, 256), cur),
                   pl.BlockSpec((1, 256), const), pl.BlockSpec((3, 256), const)],
        out_shape=(jax.ShapeDtypeStruct((S, 1024), BF16), jax.ShapeDtypeStruct((S, 256), BF16),
                   jax.ShapeDtypeStruct((S, 256), BF16), jax.ShapeDtypeStruct((1, 256), F32),
                   jax.ShapeDtypeStruct((3, 256), F32)),
        compiler_params=_cp("arbitrary"),
    )(pp, pp, pp, do_a, do_a, do_c, do_c, wbd, wbd_t, pool_scale, conv_w)


def _sigmoid(x):
    return 0.5 * jnp.tanh(0.5 * x) + 0.5


def _merge_fwd(o_a, o_b, o_c, glog, w_br, *, name, ts=512):
    S = o_a.shape[0]
    D = w_br.shape[1]
    ts = min(ts, S)

    def body(oa_ref, ob_ref, oc_ref, gl_ref, w_ref, m_ref):
        pa = jnp.dot(oa_ref[...], w_ref[0:256, :], preferred_element_type=F32)
        pb = jnp.dot(ob_ref[...], w_ref[256:768, :], preferred_element_type=F32)
        pc = jnp.dot(oc_ref[...], w_ref[768:1024, :], preferred_element_type=F32)
        m = _sigmoid(gl_ref[:, 0:D].astype(F32)) * pa
        m = m + _sigmoid(gl_ref[:, D:2 * D].astype(F32)) * pb
        m = m + _sigmoid(gl_ref[:, 2 * D:3 * D].astype(F32)) * pc
        m_ref[...] = m.astype(BF16)

    cur = lambda i: (i, 0)
    return pl.pallas_call(
        body, name=name, grid=(S // ts,),
        in_specs=[pl.BlockSpec((ts, 256), cur), pl.BlockSpec((ts, 512), cur), pl.BlockSpec((ts, 256), cur),
                  pl.BlockSpec((ts, 3 * D), cur), pl.BlockSpec((1024, D), lambda i: (0, 0))],
        out_specs=pl.BlockSpec((ts, D), cur),
        out_shape=jax.ShapeDtypeStruct((S, D), BF16),
        compiler_params=_cp("parallel"),
    )(o_a, o_b, o_c, glog, w_br)


def _merge_bwd(dm, o_a, o_b, o_c, glog, w_br, w_br_t, *, name, ts=256):
    S = o_a.shape[0]
    D = w_br.shape[1]
    ts = min(ts, S)

    def body(dm_ref, oa_ref, ob_ref, oc_ref, gl_ref, w_ref, wt_ref, dgl_ref, dp_ref, doa_ref, dob_ref, doc_ref):
        dmv = dm_ref[...].astype(F32)
        branches = ((oa_ref, 0, 256, doa_ref), (ob_ref, 256, 768, dob_ref), (oc_ref, 768, 1024, doc_ref))
        for b, (o_ref, r0, r1, do_ref) in enumerate(branches):
            prod = jnp.dot(o_ref[...], w_ref[r0:r1, :], preferred_element_type=F32)
            gate = _sigmoid(gl_ref[:, b * D:(b + 1) * D].astype(F32))
            dgl_ref[:, b * D:(b + 1) * D] = (dmv * prod * gate * (1.0 - gate)).astype(BF16)
            dprod = (dmv * gate).astype(BF16)
            dp_ref[:, b * D:(b + 1) * D] = dprod
            do_ref[...] = jnp.dot(dprod, wt_ref[:, r0:r1], preferred_element_type=F32).astype(BF16)

    cur = lambda i: (i, 0)
    const = lambda i: (0, 0)
    return pl.pallas_call(
        body, name=name, grid=(S // ts,),
        in_specs=[pl.BlockSpec((ts, D), cur), pl.BlockSpec((ts, 256), cur), pl.BlockSpec((ts, 512), cur),
                  pl.BlockSpec((ts, 256), cur), pl.BlockSpec((ts, 3 * D), cur),
                  pl.BlockSpec((1024, D), const), pl.BlockSpec((D, 1024), const)],
        out_specs=[pl.BlockSpec((ts, 3 * D), cur), pl.BlockSpec((ts, 3 * D), cur), pl.BlockSpec((ts, 256), cur),
                   pl.BlockSpec((ts, 512), cur), pl.BlockSpec((ts, 256), cur)],
        out_shape=(jax.ShapeDtypeStruct((S, 3 * D), BF16), jax.ShapeDtypeStruct((S, 3 * D), BF16),
                   jax.ShapeDtypeStruct((S, 256), BF16), jax.ShapeDtypeStruct((S, 512), BF16),
                   jax.ShapeDtypeStruct((S, 256), BF16)),
        compiler_params=_cp("parallel"),
    )(dm, o_a, o_b, o_c, glog, w_br, w_br_t)


FFN_CHUNK = 128
FFN_DOT_CHUNKS = 4


def _conv3(x, w_ref, cols):
    x1, x2 = _shift_dn(x, 1), _shift_dn(x, 2)
    return w_ref[2:3, cols] * x + w_ref[1:2, cols] * x1 + w_ref[0:1, cols] * x2, x1, x2


def _ffn_down_fwd(up_pre, fcw, w_down3, resid, gamma, beta, *, alpha, name, tc, ts=512, gather=None):
    S, F2 = up_pre.shape
    D = resid.shape[1]
    ts = min(ts, S)
    nt = S // ts
    nj = F2 // (2 * tc)
    has_ag = gather is not None
    n_g = len(gather) if has_ag else 0

    def body(cur_ref, prev_ref, w_ref, wd_ref, r_ref, g_ref, be_ref, *rest):
        h_ref, y_ref, xh_ref, rs_ref, up_ref = rest[n_g:n_g + 5]
        acc_ref = rest[2 * n_g + 5]
        if has_ag:
            ag_refs = (rest[:n_g], rest[n_g + 5:2 * n_g + 5]) + tuple(rest[2 * n_g + 6:2 * n_g + 9])
        i, j = pl.program_id(0), pl.program_id(1)
        if has_ag:
            @pl.when((i == 0) & (j == 0))
            def _():
                _ag_start(*ag_refs)

        part = None
        for c in range(tc // FFN_CHUNK):
            halves = []
            for half in range(2):
                cols = slice(half * tc + c * FFN_CHUNK, half * tc + (c + 1) * FFN_CHUNK)
                prev = jnp.where(i > 0, prev_ref[:, cols].astype(F32), 0.0)
                x = jnp.concatenate([prev, cur_ref[:, cols].astype(F32)], axis=0)
                halves.append(_conv3(x, w_ref, cols)[0][HALO:HALO + ts])
                up_ref[:, cols] = halves[-1].astype(BF16)
            a, b = halves
            h_ref[:, c * FFN_CHUNK:(c + 1) * FFN_CHUNK] = (a * _sigmoid(a) * b).astype(BF16)
            if (c + 1) % FFN_DOT_CHUNKS == 0 or c + 1 == tc // FFN_CHUNK:
                k0 = (c // FFN_DOT_CHUNKS) * FFN_DOT_CHUNKS * FFN_CHUNK
                piece = jnp.dot(h_ref[:, k0:(c + 1) * FFN_CHUNK], wd_ref[j, k0:(c + 1) * FFN_CHUNK, :],
                                preferred_element_type=F32)
                part = piece if part is None else part + piece

        @pl.when(j == 0)
        def _():
            acc_ref[...] = part

        @pl.when(j > 0)
        def _():
            acc_ref[...] += part

        @pl.when(j == nj - 1)
        def _():
            z = alpha * r_ref[...] + acc_ref[...]
            mu = jnp.mean(z, axis=-1, keepdims=True)
            zc = z - mu
            var = jnp.mean(zc * zc, axis=-1, keepdims=True)
            rstd = lax.rsqrt(var + LN_EPS)
            xhat = zc * rstd
            y_ref[...] = xhat * g_ref[...] + be_ref[...]
            xh_ref[...] = xhat.astype(BF16)
            rs_ref[...] = rstd

        if has_ag:
            @pl.when((i == nt - 1) & (j == nj - 1))
            def _():
                _ag_finish(*ag_refs)

    hb = ts // HALO
    row = lambda i, j: (i, 0)
    vec = lambda i, j: (0, 0)
    in_specs = [pl.BlockSpec((ts, 2 * tc), lambda i, j: (i, j)),
                pl.BlockSpec((HALO, 2 * tc), lambda i, j: (jnp.maximum(i * hb - 1, 0), j)),
                pl.BlockSpec((3, 2 * tc), lambda i, j: (0, j)),
                pl.BlockSpec((nj, tc, D), lambda i, j: (0, 0, 0), pipeline_mode=pl.Buffered(1)),
                pl.BlockSpec((ts, D), row), pl.BlockSpec((1, D), vec), pl.BlockSpec((1, D), vec)]
    out_specs = [pl.BlockSpec((ts, tc), lambda i, j: (i, j)), pl.BlockSpec((ts, D), row), pl.BlockSpec((ts, D), row),
                 pl.BlockSpec((ts, 1), row), pl.BlockSpec((ts, 2 * tc), lambda i, j: (i, j))]
    out_shape = [jax.ShapeDtypeStruct((S, F2 // 2), BF16), jax.ShapeDtypeStruct((S, D), F32),
                 jax.ShapeDtypeStruct((S, D), BF16), jax.ShapeDtypeStruct((S, 1), F32),
                 jax.ShapeDtypeStruct((S, F2), BF16)]
    args = [up_pre, up_pre, fcw, w_down3, resid, gamma, beta]
    scratch = [pltpu.VMEM((ts, D), F32)]
    if has_ag:
        in_specs += [pl.BlockSpec(memory_space=pl.ANY)] * n_g
        args += list(gather)
        out_specs += [pl.BlockSpec(memory_space=pl.ANY)] * n_g
        out_shape += [jax.ShapeDtypeStruct((N_DEV,) + g.shape, g.dtype) for g in gather]
        scratch += _ag_sems(n_g)
    res = pl.pallas_call(
        body, name=name, grid=(nt, nj), in_specs=in_specs, out_specs=out_specs, out_shape=out_shape,
        scratch_shapes=scratch, compiler_params=_cp("arbitrary", "arbitrary"),
    )(*args)
    return tuple(res[:5]) + ((list(res[5:]),) if has_ag else ())


def _ffn_up_bwd(up_pre, up, dh, fcw, w_up_t3, dz, *, alpha, name, tc, ts=256, scatter=None):
    S, F2 = up_pre.shape
    D = dz.shape[1]
    ts = min(ts, S)
    nt = S // ts
    nj = F2 // (2 * tc)
    has_rs = scatter is not None
    n_s = len(scatter) if has_rs else 0
    tile = slice(0, ts)

    def body(x_ref, upc_ref, upn_ref, dhc_ref, dhn_ref, w_ref, wt_ref, dz_ref, *rest):
        dpre_ref, dx_ref, dw_ref = rest[n_s:n_s + 3]
        acc_ref = rest[2 * n_s + 3]
        if has_rs:
            rs_refs = (rest[:n_s], rest[n_s + 3:2 * n_s + 3], rest[2 * n_s + 4], rest[2 * n_s + 5])
        i, j = pl.program_id(0), pl.program_id(1)

        @pl.when((i == 0) & (j == 0))
        def _():
            dw_ref[...] = jnp.zeros_like(dw_ref)
            if has_rs:
                _rs_chips_start(*rs_refs)

        part = None
        for c in range(tc // FFN_CHUNK):
            lanes = slice(c * FFN_CHUNK, (c + 1) * FFN_CHUNK)
            dh_n = jnp.where(i < nt - 1, dhn_ref[:, lanes].astype(F32), 0.0)
            dh_e = jnp.concatenate([dhc_ref[:, lanes].astype(F32), dh_n], axis=0)
            cols_of = [slice(half * tc + c * FFN_CHUNK, half * tc + (c + 1) * FFN_CHUNK) for half in range(2)]
            a, b = [jnp.concatenate([upc_ref[:, cols].astype(F32), upn_ref[:, cols].astype(F32)], axis=0)
                    for cols in cols_of]
            sg = _sigmoid(a)
            dups = [dh_e * b * (sg * (1.0 + a * (1.0 - sg))), dh_e * (a * sg)]
            for half in range(2):
                cols, dup = cols_of[half], dups[half]
                dup1, dup2 = _shift_up(dup, 1), _shift_up(dup, 2)
                dpre = w_ref[2:3, cols] * dup + w_ref[1:2, cols] * dup1 + w_ref[0:1, cols] * dup2
                dpre_ref[:, cols] = dpre[tile].astype(BF16)
                x = x_ref[:, cols].astype(F32)
                dw_ref[j, 0:1, cols] += jnp.sum(dup2[tile] * x, axis=0, keepdims=True)
                dw_ref[j, 1:2, cols] += jnp.sum(dup1[tile] * x, axis=0, keepdims=True)
                dw_ref[j, 2:3, cols] += jnp.sum(dup[tile] * x, axis=0, keepdims=True)
            if (c + 1) % FFN_DOT_CHUNKS == 0 or c + 1 == tc // FFN_CHUNK:
                k0 = (c // FFN_DOT_CHUNKS) * FFN_DOT_CHUNKS * FFN_CHUNK
                for half in range(2):
                    ks = slice(half * tc + k0, half * tc + (c + 1) * FFN_CHUNK)
                    piece = jnp.dot(dpre_ref[:, ks], wt_ref[j, ks, :], preferred_element_type=F32)
                    part = piece if part is None else part + piece

        @pl.when(j == 0)
        def _():
            acc_ref[...] = part

        @pl.when(j > 0)
        def _():
            acc_ref[...] += part

        @pl.when(j == nj - 1)
        def _():
            dx_ref[...] = acc_ref[...] + alpha * dz_ref[...]

        if has_rs:
            @pl.when((i == nt - 1) & (j == nj - 1))
            def _():
                _rs_chips_finish(*rs_refs)

    hb = ts // HALO
    nblk = S // HALO
    nxt = lambda i, j: (jnp.minimum((i + 1) * hb, nblk - 1), j)
    row = lambda i, j: (i, 0)
    in_specs = [pl.BlockSpec((ts, 2 * tc), lambda i, j: (i, j)),
                pl.BlockSpec((ts, 2 * tc), lambda i, j: (i, j)), pl.BlockSpec((HALO, 2 * tc), nxt),
                pl.BlockSpec((ts, tc), lambda i, j: (i, j)), pl.BlockSpec((HALO, tc), nxt),
                pl.BlockSpec((3, 2 * tc), lambda i, j: (0, j)),
                pl.BlockSpec((nj, 2 * tc, D), lambda i, j: (0, 0, 0), pipeline_mode=pl.Buffered(1)),
                pl.BlockSpec((ts, D), row)]
    out_specs = [pl.BlockSpec((ts, 2 * tc), lambda i, j: (i, j)), pl.BlockSpec((ts, D), row),
                 pl.BlockSpec((nj, 3, 2 * tc), lambda i, j: (0, 0, 0))]
    out_shape = [jax.ShapeDtypeStruct((S, F2), BF16), jax.ShapeDtypeStruct((S, D), F32),
                 jax.ShapeDtypeStruct((nj, 3, 2 * tc), F32)]
    args = [up_pre, up, up, dh, dh, fcw, w_up_t3, dz]
    scratch = [pltpu.VMEM((ts, D), F32)]
    if has_rs:
        in_specs += [pl.BlockSpec(memory_space=pl.ANY)] * n_s
        args += list(scatter)
        out_specs += [pl.BlockSpec(memory_space=pl.ANY)] * n_s
        out_shape += [jax.ShapeDtypeStruct((3,) + q.shape[1:], q.dtype) for q in scatter]
        scratch += _rs_sems(n_s)
    res = pl.pallas_call(
        body, name=name, grid=(nt, nj), in_specs=in_specs, out_specs=out_specs, out_shape=out_shape,
        scratch_shapes=scratch, compiler_params=_cp("arbitrary", "arbitrary"),
    )(*args)
    return tuple(res[:3]) + ((list(res[3:]),) if has_rs else ())


def _ln_bwd(dy, xhat, rstd, gamma, *, name, ts=512):
    S, D = dy.shape
    ts = min(ts, S)

    def body(dy_ref, xh_ref, rs_ref, g_ref, dz_ref, dg_ref, db_ref):
        _ln_bwd_tile(dy_ref[...], xh_ref, rs_ref, g_ref, dz_ref, dg_ref, db_ref, pl.program_id(0) == 0)

    cur = lambda i: (i, 0)
    const = lambda i: (0, 0)
    return pl.pallas_call(
        body, name=name, grid=(S // ts,),
        in_specs=[pl.BlockSpec((ts, D), cur), pl.BlockSpec((ts, D), cur), pl.BlockSpec((ts, 1), cur),
                  pl.BlockSpec((1, D), const)],
        out_specs=[pl.BlockSpec((ts, D), cur), pl.BlockSpec((1, D), const), pl.BlockSpec((1, D), const)],
        out_shape=(jax.ShapeDtypeStruct((S, D), F32), jax.ShapeDtypeStruct((1, D), F32),
                   jax.ShapeDtypeStruct((1, D), F32)),
        compiler_params=_cp("arbitrary"),
    )(dy, xhat, rstd, gamma)


def _loss_head(y, tgt, *, name, ts=512):
    S, D = y.shape
    ts = min(ts, S)

    def body(y_ref, t_ref, dy_ref, sq_ref):
        @pl.when(pl.program_id(0) == 0)
        def _():
            sq_ref[...] = jnp.zeros_like(sq_ref)

        e = y_ref[...] - t_ref[...]
        dy_ref[...] = e * (1.0 / D)
        sq_ref[...] += jnp.sum(e * e, axis=0, keepdims=True)

    cur = lambda i: (i, 0)
    return pl.pallas_call(
        body, name=name, grid=(S // ts,),
        in_specs=[pl.BlockSpec((ts, D), cur), pl.BlockSpec((ts, D), cur)],
        out_specs=[pl.BlockSpec((ts, D), cur), pl.BlockSpec((1, D), lambda i: (0, 0))],
        out_shape=(jax.ShapeDtypeStruct((S, D), F32), jax.ShapeDtypeStruct((1, D), F32)),
        compiler_params=_cp("arbitrary"),
    )(y, tgt)


def _adamw(w, g, m, v, *, name, tr=512):
    lead = w.shape[:-2]
    R, C = w.shape[-2:]
    tr = _div_tile(R, tr)
    c1 = 1.0 - ADAM_B1 ** ADAM_STEP
    c2 = 1.0 - ADAM_B2 ** ADAM_STEP

    def body(w_ref, g_ref, m_ref, v_ref, d_ref, mo_ref, vo_ref):
        gv = g_ref[...]
        m2 = ADAM_B1 * m_ref[...] + (1.0 - ADAM_B1) * gv
        v2 = ADAM_B2 * v_ref[...] + (1.0 - ADAM_B2) * (gv * gv)
        m_hat = m2 / c1
        v_hat = v2 / c2
        d_ref[...] = -ADAM_LR * (m_hat / (jnp.sqrt(v_hat) + ADAM_EPS) + ADAM_WD * w_ref[...])
        mo_ref[...] = m2
        vo_ref[...] = v2

    if lead:
        spec = pl.BlockSpec((1, tr, C), lambda l, i: (l, i, 0))
        grid = (lead[0], R // tr)
    else:
        spec = pl.BlockSpec((tr, C), lambda i: (i, 0))
        grid = (R // tr,)
    return pl.pallas_call(
        body, name=name, grid=grid,
        in_specs=[spec] * 4, out_specs=[spec] * 3,
        out_shape=(jax.ShapeDtypeStruct(w.shape, F32),) * 3,
        compiler_params=_cp(*(("parallel",) * len(grid))),
    )(w, g, m, v)


def _interleave_cols(w, nj):
    lead, f2 = w.shape[:-1], w.shape[-1]
    tc = f2 // (2 * nj)
    w = w.reshape(lead + (2, nj, tc))
    return jnp.swapaxes(w, -3, -2).reshape(lead + (f2,))


def _deinterleave_cols(w, nj):
    lead, f2 = w.shape[:-1], w.shape[-1]
    tc = f2 // (2 * nj)
    w = w.reshape(lead + (nj, 2, tc))
    return jnp.swapaxes(w, -3, -2).reshape(lead + (f2,))


def _block_diag(w_pool):
    return jnp.concatenate([jnp.pad(w_pool[g], ((0, 0), (64 * g, 192 - 64 * g))) for g in range(4)], axis=0)


def _pad_rows(v, rows):
    return jnp.pad(v, (0, rows * LANES - v.shape[0])).reshape(rows, LANES)


def kernel(x, positions, w_in, w_pool, pool_scale, attn_sinks, conv_w, w_branch_a, w_branch_b, w_branch_c, w_o, ln1_g, ln1_b, w_up, ffn_conv_w, w_down, ln2_g, ln2_b, loss_target, m_w_in, m_w_pool, m_pool_scale, m_attn_sinks, m_conv_w, m_w_branch_a, m_w_branch_b, m_w_branch_c, m_w_o, m_ln1_g, m_ln1_b, m_w_up, m_ffn_conv_w, m_w_down, m_ln2_g, m_ln2_b, v_w_in, v_w_pool, v_pool_scale, v_attn_sinks, v_conv_w, v_w_branch_a, v_w_branch_b, v_w_branch_c, v_w_o, v_ln1_g, v_ln1_b, v_w_up, v_ffn_conv_w, v_w_down, v_ln2_g, v_ln2_b):
    L, D, in_shard = w_in.shape
    S = x.shape[1]
    IN = in_shard * N_DEV
    F2 = w_up.shape[2] * N_DEV
    F = F2 // 2
    assert D == 1024 and IN == 1792 + 3 * D and x.shape[0] == 1 and S % 512 == 0
    alpha = (2 * L) ** 0.25
    NJ = 2
    TC = F // NJ
    xs = x.reshape(S, D)
    tgt = loss_target.reshape(S, D)

    big = [w_in, w_branch_a, w_branch_b, w_branch_c, w_o, w_up, w_down]
    PART_A, PART_B = (0, 1, 2, 3, 4), (5, 6)
    rows_l = [a.size // L // LANES for a in big]
    offs_l = [sum(rows_l[:k]) for k in range(len(big) + 1)]

    def pack_part(l, part):
        return [(big[k][l].T if k == 0 else big[k][l]).astype(BF16) for k in part]

    n_cw, n_fw = conv_w.size, ffn_conv_w.size
    small_rows = -(-(n_cw + n_fw) // LANES)
    small = _pad_rows(jnp.concatenate([conv_w.reshape(-1), ffn_conv_w.reshape(-1)]), small_rows)
    gsmall = _all_gather(small, "ag_conv_weights").reshape(N_DEV, -1)
    conv_full = gsmall[:, :n_cw].reshape(N_DEV, L, 3, -1).transpose(1, 2, 0, 3).reshape(L, 3, 256)
    fcw_full = gsmall[:, n_cw:n_cw + n_fw].reshape(N_DEV, L, 3, -1).transpose(1, 2, 0, 3).reshape(L, 3, F2)
    fcw_full = _interleave_cols(fcw_full, NJ)

    def shard_of(g, part, k, shape):
        assert g[part.index(k)].shape == (N_DEV,) + shape
        return g[part.index(k)]

    def unpack_a(g):
        win_t = shard_of(g, PART_A, 0, (in_shard, D)).reshape(IN, D)
        wg_t = win_t[1792:]
        wp_t = jnp.concatenate([win_t[0:256], win_t[1024:1792]], axis=0)
        wq_t = win_t[256:1024]
        wg, wp, wq = wg_t.T, wp_t.T, wq_t.T
        if g[1] is None:
            return dict(wg=wg, wp=wp, wq=wq)
        wa = shard_of(g, PART_A, 1, (256, D // N_DEV)).transpose(1, 0, 2).reshape(256, D)
        wb = shard_of(g, PART_A, 2, (512, D // N_DEV)).transpose(1, 0, 2).reshape(512, D)
        wc = shard_of(g, PART_A, 3, (256, D // N_DEV)).transpose(1, 0, 2).reshape(256, D)
        wbr = jnp.concatenate([wa, wb, wc], axis=0)
        wo = shard_of(g, PART_A, 4, (D // N_DEV, D)).reshape(D, D)
        return dict(wg=wg, wp=wp, wq=wq, wg_t=wg_t, wp_t=wp_t, wq_t=wq_t, wbr=wbr, wbr_t=wbr.T, wo=wo, wo_t=wo.T)

    def unpack_b(g):
        nh = N_DEV // (2 * NJ)
        wup = shard_of(g, PART_B, 5, (D, F2 // N_DEV)).reshape(2, NJ, nh, D, F2 // N_DEV)
        wup = wup.transpose(3, 1, 0, 2, 4).reshape(D, F2)
        wdn = shard_of(g, PART_B, 6, (F // N_DEV, D)).reshape(F, D)
        return dict(wup=wup, wup_t=wup.T, wdn=wdn, wdn_t=wdn.T)

    def local_weights(l):
        wbd = _block_diag(w_pool[l]).astype(BF16)
        return dict(wbd=wbd, wbd_t=wbd.T, scale=pool_scale[l].reshape(1, 256), conv=conv_full[l],
                    fcw=fcw_full[l], sinks=jnp.broadcast_to(attn_sinks[l].reshape(8, 1), (8, 128)),
                    g1=ln1_g[l].reshape(1, D), b1=ln1_b[l].reshape(1, D),
                    g2=ln2_g[l].reshape(1, D), b2=ln2_b[l].reshape(1, D))

    inv_freq = ROPE_THETA ** (-jnp.arange(0, ROT_DIM, 2, dtype=F32) / ROT_DIM)
    head_lane = jnp.concatenate([inv_freq, inv_freq, jnp.zeros((HEAD_DIM - ROT_DIM,), F32)])
    head_sign = jnp.concatenate([-jnp.ones((8,), F32), jnp.ones((8,), F32), jnp.zeros((HEAD_DIM - ROT_DIM,), F32)])
    inv_lane = jnp.tile(head_lane, 2).reshape(1, 128)
    sign_lane = jnp.tile(head_sign, 2).reshape(1, 128)
    cos_t, sin_t = _rope_tables(positions.reshape(S, 1), inv_lane, sign_lane, "rope_tables")

    saved, W = [], []
    h_in = xs
    gathered_a = [_all_gather(pack_part(0, PART_A[:1]), "ag_weights_first")]
    for l in range(L):
        if l == 0:
            w_in_only = unpack_a(gathered_a + [None] * 4)
            (pg, pp, pq), later = _mm_fan(h_in, [w_in_only["wg"], w_in_only["wp"], w_in_only["wq"]], out_dtype=BF16,
                                          name="proj_in", gather=pack_part(0, PART_A[1:]) + pack_part(0, PART_B))
            gathered_a, gathered_b = gathered_a + later[:4], later[4:]
        w = {**unpack_a(gathered_a), **unpack_b(gathered_b), **local_weights(l)}
        W.append(w)
        if l > 0:
            pg, pp, pq = _mm_fan(h_in, [w["wg"], w["wp"], w["wq"]], out_dtype=BF16, name="proj_in")
        o_a, o_c = _poolconv_fwd(pp, w["wbd"], w["scale"], w["conv"], name="poolconv_fwd")
        o_b, lse = _attn_fwd(pq, cos_t, sin_t, w["sinks"], name="attn_fwd")
        merged = _merge_fwd(o_a, o_b, o_c, pg, w["wbr"], name="merge_fwd")
        x1, xh1, rs1 = _mm_ln(merged, w["wo"], h_in, w["g1"], w["b1"], alpha=alpha, name="wo_ln1")
        if l + 1 < L:
            up_pre, gathered_a = _mm(x1, w["wup"], out_dtype=BF16, name="ffn_up",
                                     gather=pack_part(l + 1, PART_A))
        else:
            up_pre = _mm(x1, w["wup"], out_dtype=BF16, name="ffn_up")
        down = dict(alpha=alpha, name="ffn_down", tc=TC)
        wdn3 = w["wdn"].reshape(NJ, TC, D)
        if l + 1 < L:
            hact, x2, xh2, rs2, up, gathered_b = _ffn_down_fwd(up_pre, w["fcw"], wdn3, x1, w["g2"], w["b2"],
                                                               gather=pack_part(l + 1, PART_B), **down)
        else:
            hact, x2, xh2, rs2, up = _ffn_down_fwd(up_pre, w["fcw"], wdn3, x1, w["g2"], w["b2"], **down)
        saved.append(dict(up=up,x0=h_in, pg=pg, pp=pp, pq=pq, o_a=o_a, o_b=o_b, o_c=o_c, lse=lse, merged=merged,
                          x1=x1, xh1=xh1, rs1=rs1, up_pre=up_pre, hact=hact, xh2=xh2, rs2=rs2))
        h_in = x2

    dy, sq_lanes = _loss_head(h_in, tgt, name="loss_head")

    def pack_up(dw_up_t):
        nh = N_DEV // (2 * NJ)
        t = dw_up_t.reshape(NJ, 2, nh * (F2 // N_DEV), D).transpose(1, 0, 2, 3)
        return t.reshape(N_DEV, F2 // N_DEV, D).astype(BF16)

    def pack_grads(g):
        col = lambda a, n: a.reshape(a.shape[0], N_DEV, n).transpose(1, 0, 2)
        row = lambda a, n: a.reshape(N_DEV, n, a.shape[1])
        rest = [col(g["a"], D // N_DEV), col(g["b"], D // N_DEV), col(g["c"], D // N_DEV),
                row(g["w_o"], D // N_DEV), row(g["w_down"], F // N_DEV)]
        return [row(g["w_in_t"], in_shard).astype(BF16), pack_up(g["w_up_t"]),
                jnp.concatenate([p.reshape(N_DEV, -1, LANES).astype(BF16) for p in rest], axis=1)]

    my_c = lax.axis_index("c").astype(jnp.int32).reshape(1)
    my_chip = (2 * lax.axis_index("x") + lax.axis_index("y")).astype(jnp.int32).reshape(1)
    gw = [None] * L
    pair_sum = [None] * L
    from_chips = [None] * L
    for l in reversed(range(L)):
        w, sv = W[l], saved[l]
        if l == L - 1:
            dz2, dg2, db2 = _ln_bwd(dy, sv["xh2"], sv["rs2"], w["g2"], name="ln2_bwd")
        else:
            dz2, dg2, db2 = ln2_out
        dw_dn = _mm_tn(sv["hact"], dz2, name="down_bwd_w", tka=TC, na=NJ, tn=D, ts=1024, out_dtype=BF16)
        up_bwd = dict(alpha=alpha, name="ffn_up_bwd", tc=TC)
        if l + 1 < L:
            dh, from_sibling = _mm(dz2, w["wdn_t"], out_dtype=BF16, name="down_bwd_x", swap=packed_above)
            pair_sum[l + 1] = [_sum_sibling(p, r, my_c, "rs_sum_sibling") for p, r in zip(packed_above, from_sibling)]
        else:
            dh = _mm(dz2, w["wdn_t"], out_dtype=BF16, name="down_bwd_x")
        wup_t3 = w["wup_t"].reshape(NJ, 2 * TC, D)
        if l + 1 < L:
            dpre, dx1, dfcw, from_chips[l + 1] = _ffn_up_bwd(sv["up_pre"], sv["up"], dh, w["fcw"], wup_t3, dz2,
                                                             scatter=pair_sum[l + 1], **up_bwd)
        else:
            dpre, dx1, dfcw = _ffn_up_bwd(sv["up_pre"], sv["up"], dh, w["fcw"], wup_t3, dz2, **up_bwd)
        dfcw = dfcw.transpose(1, 0, 2).reshape(3, F2)
        dw_up_t = _mm_tn(dpre, sv["x1"], name="up_bwd_w", tka=TC, na=2 * NJ, tn=D, ts=1024,
                         out_dtype=BF16)
        dz1, dg1, db1 = _ln_bwd(dx1, sv["xh1"], sv["rs1"], w["g1"], name="ln1_bwd")
        if l == 0:
            early = [pack_up(dw_up_t)]
            dmerged, sib = _mm(dz1, w["wo_t"], out_dtype=BF16, name="wo_bwd_x", swap=early)
            pair_early = [_sum_sibling(early[0], sib[0], my_c, "rs_sum_sibling")]
        else:
            dmerged = _mm(dz1, w["wo_t"], out_dtype=BF16, name="wo_bwd_x")
        dw_o = _mm_tn(sv["merged"], dz1, name="wo_bwd_w", tka=D, tn=D, out_dtype=BF16)
        dpg, dprod, do_a, do_b, do_c = _merge_bwd(dmerged, sv["o_a"], sv["o_b"], sv["o_c"], sv["pg"],
                                                  w["wbr"], w["wbr_t"], name="merge_bwd")
        dw_a = _mm_tn(sv["o_a"], dprod, name="branch_a_bwd_w", tka=256, tn=D, b_off=0, out_dtype=BF16)
        dw_b = _mm_tn(sv["o_b"], dprod, name="branch_b_bwd_w", tka=512, tn=D, b_off=1, out_dtype=BF16)
        dw_c = _mm_tn(sv["o_c"], dprod, name="branch_c_bwd_w", tka=256, tn=D, b_off=2, out_dtype=BF16)
        dpq, dsink = _attn_bwd(sv["pq"], cos_t, sin_t, w["sinks"], do_b, sv["o_b"], sv["lse"], name="attn_bwd")
        dpp, pooled, dmixed, dscale, dconv = _poolconv_bwd(sv["pp"], do_a, do_c, w["wbd"], w["wbd_t"], w["scale"],
                                                           w["conv"], name="poolconv_bwd")
        dwbd = _mm_tn(pooled, dmixed, name="pool_bwd_w", tka=256, tn=256)
        dx_args = ([dpg, dpp, dpq], [w["wg_t"], w["wp_t"], w["wq_t"]], dz1)
        if l > 0:
            below = saved[l - 1]
            ln2_out = _mm_sum(*dx_args, add_scale=alpha, name="proj_in_bwd_x",
                              ln=(below["xh2"], below["rs2"], W[l - 1]["g2"]))
        else:
            dx, chips_early = _mm_sum(*dx_args, add_scale=alpha, name="proj_in_bwd_x", scatter=pair_early)
        dw_g = _mm_tn(dpg, sv["x0"], name="proj_gate_bwd_w", tka=1024, na=3, tn=D, out_dtype=BF16)
        dw_p = _mm_tn(dpp, sv["x0"], name="proj_poolconv_bwd_w", tka=1024, tn=D, out_dtype=BF16)
        dw_q = _mm_tn(dpq, sv["x0"], name="proj_qkv_bwd_w", tka=768, tn=D, out_dtype=BF16)
        dw_in_t = jnp.concatenate([dw_p[0:256], dw_q, dw_p[256:1024], dw_g], axis=0)
        dw_pool = jnp.stack([dwbd[64 * g:64 * (g + 1), 64 * g:64 * (g + 1)] for g in range(4)])
        gw[l] = dict(w_in_t=dw_in_t, a=dw_a, b=dw_b, c=dw_c, w_o=dw_o, w_up_t=dw_up_t, w_down=dw_dn,
                     w_pool=dw_pool, scale=dscale, sinks=dsink[:, 0], conv=dconv, fcw=_deinterleave_cols(dfcw, NJ),
                     g1=dg1, b1=db1, g2=dg2, b2=db2)
        packed_above = pack_grads(gw[l])
    late = [packed_above[0], packed_above[2]]
    from_sibling = _rs_sibling(late, "rs_sibling_last")
    pair_late = [_sum_sibling(p, r, my_c, "rs_sum_sibling") for p, r in zip(late, from_sibling)]
    chips_late = _rs_chips(pair_late, "rs_chips_last")
    pair_sum[0] = [pair_late[0], pair_early[0], pair_late[1]]
    from_chips[0] = [chips_late[0], chips_early[0], chips_late[1]]
    grad_x = dx.reshape(1, S, D)
    g_layers = [[_sum_chips(q, r, my_chip, "rs_sum_chips") for q, r in zip(pair_sum[l], from_chips[l])]
                for l in range(L)]

    def stack(k):
        return jnp.stack([gw[l][k] for l in range(L)])

    rep_vec = jnp.concatenate([
        stack("w_pool").reshape(-1), stack("scale").reshape(-1), stack("g1").reshape(-1), stack("b1").reshape(-1),
        stack("g2").reshape(-1), stack("b2").reshape(-1)])
    n_rep_full = -(-rep_vec.shape[0] // LANES)
    sinks_row = jnp.pad(stack("sinks").reshape(-1), (0, LANES - 8 * L))
    rep_vec = jnp.concatenate([_pad_rows(rep_vec, n_rep_full).reshape(-1), sinks_row, sq_lanes.reshape(-1)])
    loss_row = n_rep_full + 1
    n_rep = -(-(loss_row + 1) // 8) * 8
    rep_rows = _pad_rows(rep_vec, n_rep)
    dconv_by_dev = stack("conv").reshape(L, 3, N_DEV, -1).transpose(2, 0, 1, 3).reshape(N_DEV, -1)
    dfcw_by_dev = stack("fcw").reshape(L, 3, N_DEV, -1).transpose(2, 0, 1, 3).reshape(N_DEV, -1)
    n_mine = -(-(small_rows) // 8) * 8
    by_dev = jnp.concatenate([dconv_by_dev, dfcw_by_dev], axis=1)
    by_dev = jnp.pad(by_dev, ((0, 0), (0, n_mine * LANES - by_dev.shape[1]))).reshape(N_DEV * n_mine, LANES)
    small_g = _all_gather(jnp.concatenate([rep_rows, by_dev], axis=0), "ag_small_grads")
    rep_sum, mine_sum, loss11 = _small_reduce(small_g, n_rep, n_mine, 1.0 / D, loss_row, "small_reduce")
    loss = loss11[0, 0]

    names_big = ["w_in", "w_branch_a", "w_branch_b", "w_branch_c", "w_o", "w_up", "w_down"]
    ms_big = [m_w_in, m_w_branch_a, m_w_branch_b, m_w_branch_c, m_w_o, m_w_up, m_w_down]
    vs_big = [v_w_in, v_w_branch_a, v_w_branch_b, v_w_branch_c, v_w_o, v_w_up, v_w_down]
    out = {}
    for k, name in enumerate(names_big):
        wk = big[k]
        if k in (0, 5):
            g_t = jnp.stack([g[0 if k == 0 else 1] for g in g_layers])
            tr_ = lambda a: jnp.swapaxes(a, 1, 2)
            d, mo, vo = _adamw(tr_(wk), g_t, tr_(ms_big[k]), tr_(vs_big[k]), name="adamw_" + name)
            out[name] = (tr_(g_t), tr_(d), tr_(mo), tr_(vo))
            continue
        else:
            rest_ks = (1, 2, 3, 4, 6)
            o = sum(rows_l[q] for q in rest_ks[:rest_ks.index(k)])
            g_nat = jnp.concatenate([g[2][o:o + rows_l[k]] for g in g_layers], axis=0).reshape(wk.shape)
        d, mo, vo = _adamw(wk, g_nat, ms_big[k], vs_big[k], name="adamw_" + name)
        out[name] = (g_nat, d, mo, vo)

    def rep_pack(wp_, sc_, g1_, b1_, g2_, b2_, sk_):
        v = jnp.concatenate([wp_.reshape(-1), sc_.reshape(-1), g1_.reshape(-1), b1_.reshape(-1), g2_.reshape(-1),
                             b2_.reshape(-1)])
        return _pad_rows(jnp.concatenate([_pad_rows(v, n_rep_full).reshape(-1), sk_.reshape(-1)]), n_rep)

    def mine_pack(cw_, fw_):
        return _pad_rows(jnp.concatenate([cw_.reshape(-1), fw_.reshape(-1)]), n_mine)

    w_rep = rep_pack(w_pool, pool_scale, ln1_g, ln1_b, ln2_g, ln2_b, attn_sinks)
    m_rep = rep_pack(m_w_pool, m_pool_scale, m_ln1_g, m_ln1_b, m_ln2_g, m_ln2_b, m_attn_sinks)
    v_rep = rep_pack(v_w_pool, v_pool_scale, v_ln1_g, v_ln1_b, v_ln2_g, v_ln2_b, v_attn_sinks)
    g_rep = jnp.concatenate([rep_sum[:loss_row], jnp.zeros((n_rep - loss_row, LANES), F32)], axis=0)
    rep_res = (g_rep,) + tuple(_adamw(w_rep, g_rep, m_rep, v_rep, name="adamw_replicated"))
    w_mine = mine_pack(conv_w, ffn_conv_w)
    mine_res = (mine_sum,) + tuple(_adamw(w_mine, mine_sum, mine_pack(m_conv_w, m_ffn_conv_w),
                                          mine_pack(v_conv_w, v_ffn_conv_w), name="adamw_conv"))

    def rep_unpack(buf):
        flat = buf.reshape(-1)
        res, o = {}, 0
        for nm, ref in (("w_pool", w_pool), ("pool_scale", pool_scale), ("ln1_g", ln1_g), ("ln1_b", ln1_b),
                        ("ln2_g", ln2_g), ("ln2_b", ln2_b)):
            res[nm] = flat[o:o + ref.size].reshape(ref.shape)
            o += ref.size
        o = n_rep_full * LANES
        res["attn_sinks"] = flat[o:o + attn_sinks.size].reshape(attn_sinks.shape)
        return res

    def mine_unpack(buf):
        flat = buf.reshape(-1)
        return {"conv_w": flat[:n_cw].reshape(conv_w.shape),
                "ffn_conv_w": flat[n_cw:n_cw + n_fw].reshape(ffn_conv_w.shape)}

    order = ["w_in", "w_pool", "pool_scale", "attn_sinks", "conv_w", "w_branch_a", "w_branch_b", "w_branch_c", "w_o",
             "ln1_g", "ln1_b", "w_up", "ffn_conv_w", "w_down", "ln2_g", "ln2_b"]
    results = [loss, grad_x]
    for kind in range(4):
        rep_k, mine_k = rep_unpack(rep_res[kind]), mine_unpack(mine_res[kind])
        for nm in order:
            if nm in out:
                results.append(out[nm][kind])
            elif nm in rep_k:
                results.append(rep_k[nm])
            else:
                results.append(mine_k[nm])
    return tuple(results)
```
